```python
import jax, jax.numpy as jnp
from jax import lax
import numpy as np

D_MODEL = 1024
BATCH = 32
SEQ = 2048
DEPTH = 2

N_MIXERS = 2
N_ATTN_LAYERS = (DEPTH + 1) // 2
N_REC_LAYERS = DEPTH // 2

HEAD_DIM = 64
N_HEADS = D_MODEL // HEAD_DIM
N_KV_HEADS = 2
GROUP = N_HEADS // N_KV_HEADS
ATTN_WIDTH = N_HEADS * HEAD_DIM
KV_WIDTH = N_KV_HEADS * HEAD_DIM
ATTN_IN = 2 * ATTN_WIDTH + 2 * KV_WIDTH
WINDOW = 128
ATTN_BLOCK = 128
ROPE_THETA = 500000.0
ROPE_DIM = HEAD_DIM // 4

REC_HEADS = 8
REC_KEY_DIM = 128
REC_VALUE_DIM = D_MODEL // REC_HEADS
FORGET_DIM = REC_HEADS * REC_KEY_DIM
REC_WIDTH = REC_HEADS * REC_VALUE_DIM
REC_IN = 2 * FORGET_DIM + 2 * REC_WIDTH
REC_CHUNK = 32

NORM_EPS = 1e-6

kernel_name = "hybrid_swa_sink_hgrn2_interleaved"


def rmsnorm(x, w):
    xf = x.astype(jnp.float32)
    y = xf * lax.rsqrt(jnp.mean(xf * xf, axis=-1, keepdims=True) + NORM_EPS)
    return (y * w.astype(jnp.float32)).astype(x.dtype)


def partial_rope(x, positions):
    half = ROPE_DIM // 2
    inv_freq = ROPE_THETA ** (-(jnp.arange(half, dtype=jnp.float32) * 2.0 / ROPE_DIM))
    ang = positions.astype(jnp.float32)[..., None] * inv_freq
    cos = jnp.cos(ang)[:, :, None, :]
    sin = jnp.sin(ang)[:, :, None, :]
    x1 = x[..., :half].astype(jnp.float32)
    x2 = x[..., half:ROPE_DIM].astype(jnp.float32)
    r1 = x1 * cos - x2 * sin
    r2 = x2 * cos + x1 * sin
    return jnp.concatenate([r1.astype(x.dtype), r2.astype(x.dtype), x[..., ROPE_DIM:]], axis=-1)


def sliding_window_gqa(q, k, v, sinks):
    B, T = q.shape[0], q.shape[1]
    nb = T // ATTN_BLOCK
    qb = q.reshape(B, nb, ATTN_BLOCK, N_KV_HEADS, GROUP, HEAD_DIM).transpose(1, 0, 2, 3, 4, 5)

    def span(t):
        tb = t.reshape(B, nb, ATTN_BLOCK, N_KV_HEADS, HEAD_DIM)
        prev = jnp.pad(tb, ((0, 0), (1, 0), (0, 0), (0, 0), (0, 0)))[:, :-1]
        return jnp.concatenate([prev, tb], axis=2).transpose(1, 0, 2, 3, 4)

    kk, vv = span(k), span(v)
    q_rel = jnp.arange(ATTN_BLOCK)[:, None] + ATTN_BLOCK
    k_rel = jnp.arange(2 * ATTN_BLOCK)[None, :]
    band = (k_rel <= q_rel) & (q_rel - k_rel < WINDOW)
    sink = sinks.astype(jnp.float32).reshape(N_KV_HEADS, GROUP)[None, :, :, None]
    scale = HEAD_DIM ** -0.5

    def block(args):
        qi, ki, vi, idx = args
        s = jnp.einsum('bqhgd,bkhd->bhgqk', qi, ki, preferred_element_type=jnp.float32) * scale
        valid = band & ((idx > 0) | (k_rel >= ATTN_BLOCK))
        s = jnp.where(valid, s, -jnp.inf)
        m = jnp.maximum(jnp.max(s, axis=-1), sink)
        p = jnp.exp(s - m[..., None])
        denom = jnp.sum(p, axis=-1) + jnp.exp(sink - m)
        p = (p / denom[..., None]).astype(vi.dtype)
        return jnp.einsum('bhgqk,bkhd->bqhgd', p, vi)

    out = lax.map(block, (qb, kk, vv, jnp.arange(nb)))
    return out.transpose(1, 0, 2, 3, 4, 5).reshape(B, T, ATTN_WIDTH)


def attention_mixer(h, positions, w_in, b_in, sinks, w_out, b_out):
    B, T = h.shape[0], h.shape[1]
    proj = h @ w_in + b_in
    q, k, v, z = jnp.split(proj, [ATTN_WIDTH, ATTN_WIDTH + KV_WIDTH, ATTN_WIDTH + 2 * KV_WIDTH], axis=-1)
    q = partial_rope(q.reshape(B, T, N_HEADS, HEAD_DIM), positions)
    k = partial_rope(k.reshape(B, T, N_KV_HEADS, HEAD_DIM), positions)
    v = v.reshape(B, T, N_KV_HEADS, HEAD_DIM)
    o = sliding_window_gqa(q, k, v, sinks)
    return (o * jax.nn.silu(z)) @ w_out + b_out


def chunked_gated_recurrence(q, k, v, log_f):
    B, T, H, K = q.shape
    V = v.shape[-1]
    nc = T // REC_CHUNK

    def to_chunks(t):
        return t.reshape(B, nc, REC_CHUNK, H, t.shape[-1]).transpose(1, 0, 3, 2, 4)

    qc, kc, vc, gc = to_chunks(q), to_chunks(k), to_chunks(v), to_chunks(log_f)
    causal = jnp.tril(jnp.ones((REC_CHUNK, REC_CHUNK), dtype=bool))[:, :, None]

    def step(S, inp):
        qi, ki, vi, gi = inp
        b = jnp.cumsum(gi, axis=2)
        b_last = b[:, :, -1:, :]
        o_inter = jnp.einsum('bhck,bhkv->bhcv', qi * jnp.exp(b), S)
        diff = b[:, :, :, None, :] - b[:, :, None, :, :]
        decay = jnp.exp(jnp.where(causal, diff, -jnp.inf))
        scores = jnp.einsum('bhtk,bhtsk->bhts', qi, decay * ki[:, :, None, :, :])
        o_intra = jnp.einsum('bhts,bhsv->bhtv', scores, vi)
        S_new = S * jnp.exp(b_last)[:, :, 0, :, None] + jnp.einsum(
            'bhck,bhcv->bhkv', ki * jnp.exp(b_last - b), vi)
        return S_new, o_inter + o_intra

    S0 = jnp.zeros((B, H, K, V), jnp.float32)
    _, o = lax.scan(step, S0, (qc, kc, vc, gc))
    return o.transpose(1, 0, 3, 2, 4).reshape(B, T, H, V)


def hgrn2_mixer(h, lower_bound, w_in, gnorm_w, w_out):
    B, T = h.shape[0], h.shape[1]
    proj = h @ w_in
    q, f, i, z = jnp.split(proj, [FORGET_DIM, 2 * FORGET_DIM, 2 * FORGET_DIM + REC_WIDTH], axis=-1)
    q = jax.nn.silu(q.astype(jnp.float32)).reshape(B, T, REC_HEADS, REC_KEY_DIM)
    lb = lower_bound.astype(jnp.float32)
    log_f = jnp.logaddexp(jnp.log(lb), jnp.log1p(-lb) + jax.nn.log_sigmoid(f.astype(jnp.float32)))
    k = -jnp.expm1(log_f)
    log_f = log_f.reshape(B, T, REC_HEADS, REC_KEY_DIM)
    k = k.reshape(B, T, REC_HEADS, REC_KEY_DIM)
    v = i.astype(jnp.float32).reshape(B, T, REC_HEADS, REC_VALUE_DIM)
    o = chunked_gated_recurrence(q, k, v, log_f)
    o = o * lax.rsqrt(jnp.mean(o * o, axis=-1, keepdims=True) + NORM_EPS) * gnorm_w.astype(jnp.float32)
    o = o.reshape(B, T, REC_WIDTH) * jax.nn.silu(z.astype(jnp.float32))
    return o.astype(h.dtype) @ w_out


def _fwd_setup_inputs(seed: int = 0) -> dict:
    key = jax.random.key(seed)
    ks = jax.random.split(key, 16)
    f32 = jnp.float32
    x = jax.random.normal(ks[0], (BATCH, SEQ, D_MODEL), f32)
    offsets = jax.random.randint(ks[1], (BATCH, 1), 0, 4096, dtype=jnp.int32)
    positions = offsets + jnp.arange(SEQ, dtype=jnp.int32)[None, :]
    pre_norm_w = 1.0 + 0.02 * jax.random.normal(ks[2], (DEPTH, D_MODEL), f32)
    post_norm_w = 1.0 + 0.02 * jax.random.normal(ks[3], (DEPTH, D_MODEL), f32)
    attn_w_in = jax.random.normal(ks[4], (N_ATTN_LAYERS, D_MODEL, ATTN_IN), f32) * D_MODEL ** -0.5
    attn_b_in = 0.02 * jax.random.normal(ks[5], (N_ATTN_LAYERS, ATTN_IN), f32)
    attn_sinks = 0.5 * jax.random.normal(ks[6], (N_ATTN_LAYERS, N_HEADS), f32)
    attn_w_out = jax.random.normal(ks[7], (N_ATTN_LAYERS, ATTN_WIDTH, D_MODEL), f32) * ATTN_WIDTH ** -0.5
    attn_b_out = 0.02 * jax.random.normal(ks[8], (N_ATTN_LAYERS, D_MODEL), f32)
    rec_w_in = jax.random.normal(ks[9], (N_REC_LAYERS, D_MODEL, REC_IN), f32) * D_MODEL ** -0.5
    rec_lb_logits = 0.5 * jax.random.normal(ks[10], (DEPTH, FORGET_DIM), f32)
    rec_gnorm_w = 1.0 + 0.02 * jax.random.normal(ks[11], (N_REC_LAYERS, REC_VALUE_DIM), f32)
    rec_w_out = jax.random.normal(ks[12], (N_REC_LAYERS, REC_WIDTH, D_MODEL), f32) * REC_WIDTH ** -0.5
    return {"x": x, "positions": positions, "pre_norm_w": pre_norm_w, "post_norm_w": post_norm_w,
            "attn_w_in": attn_w_in, "attn_b_in": attn_b_in, "attn_sinks": attn_sinks,
            "attn_w_out": attn_w_out, "attn_b_out": attn_b_out, "rec_w_in": rec_w_in,
            "rec_lb_logits": rec_lb_logits, "rec_gnorm_w": rec_gnorm_w, "rec_w_out": rec_w_out}


def _fwd_reference(x, positions, pre_norm_w, post_norm_w, attn_w_in, attn_b_in, attn_sinks,
              attn_w_out, attn_b_out, rec_w_in, rec_lb_logits, rec_gnorm_w, rec_w_out):
    probs = jax.nn.softmax(rec_lb_logits.astype(jnp.float32), axis=0)
    cum = jnp.cumsum(probs, axis=0)
    lower_bounds = cum - cum[0:1]
    for layer in range(DEPTH):
        h = rmsnorm(x, pre_norm_w[layer])
        j = layer // N_MIXERS
        if layer % N_MIXERS == 0:
            y = attention_mixer(h, positions, attn_w_in[j], attn_b_in[j], attn_sinks[j],
                                attn_w_out[j], attn_b_out[j])
        else:
            y = hgrn2_mixer(h, lower_bounds[layer], rec_w_in[j], rec_gnorm_w[j], rec_w_out[j])
        x = x + rmsnorm(y, post_norm_w[layer])
    return x


import jax as _jax
import jax.numpy as _jnp

TWIN_FORMAT = 'train_step'
FWD_PARAMS = ['x', 'positions', 'pre_norm_w', 'post_norm_w', 'attn_w_in', 'attn_b_in', 'attn_sinks', 'attn_w_out', 'attn_b_out', 'rec_w_in', 'rec_lb_logits', 'rec_gnorm_w', 'rec_w_out']
TWIN_WEIGHTS = ['pre_norm_w', 'post_norm_w', 'attn_w_in', 'attn_b_in', 'attn_sinks', 'attn_w_out', 'attn_b_out', 'rec_w_in', 'rec_lb_logits', 'rec_gnorm_w', 'rec_w_out']
TWIN_DIFF_INPUT = 'x'
TWIN_INPUTS = ['x', 'positions', 'pre_norm_w', 'post_norm_w', 'attn_w_in', 'attn_b_in', 'attn_sinks', 'attn_w_out', 'attn_b_out', 'rec_w_in', 'rec_lb_logits', 'rec_gnorm_w', 'rec_w_out', 'loss_target', 'm_pre_norm_w', 'm_post_norm_w', 'm_attn_w_in', 'm_attn_b_in', 'm_attn_sinks', 'm_attn_w_out', 'm_attn_b_out', 'm_rec_w_in', 'm_rec_lb_logits', 'm_rec_gnorm_w', 'm_rec_w_out', 'v_pre_norm_w', 'v_post_norm_w', 'v_attn_w_in', 'v_attn_b_in', 'v_attn_sinks', 'v_attn_w_out', 'v_attn_b_out', 'v_rec_w_in', 'v_rec_lb_logits', 'v_rec_gnorm_w', 'v_rec_w_out']
TWIN_OUTPUTS = ['loss', 'grad_x', 'grad_pre_norm_w', 'grad_post_norm_w', 'grad_attn_w_in', 'grad_attn_b_in', 'grad_attn_sinks', 'grad_attn_w_out', 'grad_attn_b_out', 'grad_rec_w_in', 'grad_rec_lb_logits', 'grad_rec_gnorm_w', 'grad_rec_w_out', 'delta_pre_norm_w', 'delta_post_norm_w', 'delta_attn_w_in', 'delta_attn_b_in', 'delta_attn_sinks', 'delta_attn_w_out', 'delta_attn_b_out', 'delta_rec_w_in', 'delta_rec_lb_logits', 'delta_rec_gnorm_w', 'delta_rec_w_out', 'new_m_pre_norm_w', 'new_m_post_norm_w', 'new_m_attn_w_in', 'new_m_attn_b_in', 'new_m_attn_sinks', 'new_m_attn_w_out', 'new_m_attn_b_out', 'new_m_rec_w_in', 'new_m_rec_lb_logits', 'new_m_rec_gnorm_w', 'new_m_rec_w_out', 'new_v_pre_norm_w', 'new_v_post_norm_w', 'new_v_attn_w_in', 'new_v_attn_b_in', 'new_v_attn_sinks', 'new_v_attn_w_out', 'new_v_attn_b_out', 'new_v_rec_w_in', 'new_v_rec_lb_logits', 'new_v_rec_gnorm_w', 'new_v_rec_w_out']
TWIN_LEAF_KINDS = {'loss': 'loss', 'grad_x': 'grad_x', 'grad_pre_norm_w': 'grad_w', 'grad_post_norm_w': 'grad_w', 'grad_attn_w_in': 'grad_w', 'grad_attn_b_in': 'grad_w', 'grad_attn_sinks': 'grad_w', 'grad_attn_w_out': 'grad_w', 'grad_attn_b_out': 'grad_w', 'grad_rec_w_in': 'grad_w', 'grad_rec_lb_logits': 'grad_w', 'grad_rec_gnorm_w': 'grad_w', 'grad_rec_w_out': 'grad_w', 'delta_pre_norm_w': 'delta_w', 'delta_post_norm_w': 'delta_w', 'delta_attn_w_in': 'delta_w', 'delta_attn_b_in': 'delta_w', 'delta_attn_sinks': 'delta_w', 'delta_attn_w_out': 'delta_w', 'delta_attn_b_out': 'delta_w', 'delta_rec_w_in': 'delta_w', 'delta_rec_lb_logits': 'delta_w', 'delta_rec_gnorm_w': 'delta_w', 'delta_rec_w_out': 'delta_w', 'new_m_pre_norm_w': 'new_m', 'new_m_post_norm_w': 'new_m', 'new_m_attn_w_in': 'new_m', 'new_m_attn_b_in': 'new_m', 'new_m_attn_sinks': 'new_m', 'new_m_attn_w_out': 'new_m', 'new_m_attn_b_out': 'new_m', 'new_m_rec_w_in': 'new_m', 'new_m_rec_lb_logits': 'new_m', 'new_m_rec_gnorm_w': 'new_m', 'new_m_rec_w_out': 'new_m', 'new_v_pre_norm_w': 'new_v', 'new_v_post_norm_w': 'new_v', 'new_v_attn_w_in': 'new_v', 'new_v_attn_b_in': 'new_v', 'new_v_attn_sinks': 'new_v', 'new_v_attn_w_out': 'new_v', 'new_v_attn_b_out': 'new_v', 'new_v_rec_w_in': 'new_v', 'new_v_rec_lb_logits': 'new_v', 'new_v_rec_gnorm_w': 'new_v', 'new_v_rec_w_out': 'new_v'}


def _forward(args):
    return _fwd_reference(*[args[k] for k in FWD_PARAMS])


def _output_shape():
    out = _jax.eval_shape(lambda: _forward(_fwd_setup_inputs(0)))
    return out.shape, out.dtype

N_MICROBATCH = 1
ADAM_LR = 0.001
ADAM_B1 = 0.9
ADAM_B2 = 0.999
ADAM_EPS = 1e-08
ADAM_WD = 0.01
ADAM_STEP = 10
PER_EXAMPLE_BATCH_AXIS = {'x': 0, 'positions': 0, 'loss_target': 0}
SHARED_INPUTS = []
_WEIGHT_DTYPES = {'pre_norm_w': _jnp.float32, 'post_norm_w': _jnp.float32, 'attn_w_in': _jnp.float32, 'attn_b_in': _jnp.float32, 'attn_sinks': _jnp.float32, 'attn_w_out': _jnp.float32, 'attn_b_out': _jnp.float32, 'rec_w_in': _jnp.float32, 'rec_lb_logits': _jnp.float32, 'rec_gnorm_w': _jnp.float32, 'rec_w_out': _jnp.float32}
MOMENT_SCALE = {'pre_norm_w': 1.282936e+00, 'post_norm_w': 6.412407e+01, 'attn_w_in': 7.978748e-01, 'attn_b_in': 2.194413e+01, 'attn_sinks': 2.509689e-01, 'attn_w_out': 7.878793e-01, 'attn_b_out': 1.491837e+02, 'rec_w_in': 7.442759e-01, 'rec_lb_logits': 3.947437e-02, 'rec_gnorm_w': 4.470379e+00, 'rec_w_out': 1.356754e+00}


def _to_microbatches(a, axis):
    t = _jnp.moveaxis(a, axis, 0)
    t = t.reshape((N_MICROBATCH, t.shape[0] // N_MICROBATCH) + t.shape[1:])
    return _jnp.moveaxis(t, 1, axis + 1)


def setup_inputs(seed: int = 0) -> dict:
    inp = _fwd_setup_inputs(seed)
    key = _jax.random.fold_in(_jax.random.key(seed), 7919)
    shape, _ = _output_shape()
    out = dict(inp)
    out["loss_target"] = _jax.random.normal(_jax.random.fold_in(key, 0), shape, _jnp.float32)
    for i, name in enumerate(TWIN_WEIGHTS):
        w = inp[name].astype(_jnp.float32)
        if MOMENT_SCALE is None:
            s = _jnp.sqrt(_jnp.mean(_jnp.square(w)) + 1e-30)
        else:
            s = MOMENT_SCALE[name]
        km, kv = _jax.random.split(_jax.random.fold_in(key, i + 1))
        out[name] = w
        out["m_" + name] = s * _jax.random.normal(km, w.shape, _jnp.float32)
        out["v_" + name] = (s * s) * _jax.random.uniform(kv, w.shape, _jnp.float32, 0.5, 1.5)
    if N_MICROBATCH > 1:
        for name, axis in PER_EXAMPLE_BATCH_AXIS.items():
            out[name] = _to_microbatches(out[name], axis)
    return {'x': out['x'], 'positions': out['positions'], 'pre_norm_w': out['pre_norm_w'], 'post_norm_w': out['post_norm_w'], 'attn_w_in': out['attn_w_in'], 'attn_b_in': out['attn_b_in'], 'attn_sinks': out['attn_sinks'], 'attn_w_out': out['attn_w_out'], 'attn_b_out': out['attn_b_out'], 'rec_w_in': out['rec_w_in'], 'rec_lb_logits': out['rec_lb_logits'], 'rec_gnorm_w': out['rec_gnorm_w'], 'rec_w_out': out['rec_w_out'], 'loss_target': out['loss_target'], 'm_pre_norm_w': out['m_pre_norm_w'], 'm_post_norm_w': out['m_post_norm_w'], 'm_attn_w_in': out['m_attn_w_in'], 'm_attn_b_in': out['m_attn_b_in'], 'm_attn_sinks': out['m_attn_sinks'], 'm_attn_w_out': out['m_attn_w_out'], 'm_attn_b_out': out['m_attn_b_out'], 'm_rec_w_in': out['m_rec_w_in'], 'm_rec_lb_logits': out['m_rec_lb_logits'], 'm_rec_gnorm_w': out['m_rec_gnorm_w'], 'm_rec_w_out': out['m_rec_w_out'], 'v_pre_norm_w': out['v_pre_norm_w'], 'v_post_norm_w': out['v_post_norm_w'], 'v_attn_w_in': out['v_attn_w_in'], 'v_attn_b_in': out['v_attn_b_in'], 'v_attn_sinks': out['v_attn_sinks'], 'v_attn_w_out': out['v_attn_w_out'], 'v_attn_b_out': out['v_attn_b_out'], 'v_rec_w_in': out['v_rec_w_in'], 'v_rec_lb_logits': out['v_rec_lb_logits'], 'v_rec_gnorm_w': out['v_rec_gnorm_w'], 'v_rec_w_out': out['v_rec_w_out']}


def _loss(weights, diff, rest, loss_target):
    with _jax.named_scope("forward"):
        args = {**rest, TWIN_DIFF_INPUT: diff, **{k: w.astype(_WEIGHT_DTYPES[k]) for k, w in weights.items()}}
        y = _forward(args)
    with _jax.named_scope("loss_head"):
        err = _jnp.square(y.astype(_jnp.float32) - loss_target)
        return 0.5 * _jnp.sum(_jnp.mean(err, axis=-1)) if err.ndim else 0.5 * err


def _adamw(w, g, m, v):
    m = ADAM_B1 * m + (1.0 - ADAM_B1) * g
    v = ADAM_B2 * v + (1.0 - ADAM_B2) * _jnp.square(g)
    m_hat = m / (1.0 - ADAM_B1 ** ADAM_STEP)
    v_hat = v / (1.0 - ADAM_B2 ** ADAM_STEP)
    delta = -ADAM_LR * (m_hat / (_jnp.sqrt(v_hat) + ADAM_EPS) + ADAM_WD * w)
    return delta, m, v


def reference(x, positions, pre_norm_w, post_norm_w, attn_w_in, attn_b_in, attn_sinks, attn_w_out, attn_b_out, rec_w_in, rec_lb_logits, rec_gnorm_w, rec_w_out, loss_target, m_pre_norm_w, m_post_norm_w, m_attn_w_in, m_attn_b_in, m_attn_sinks, m_attn_w_out, m_attn_b_out, m_rec_w_in, m_rec_lb_logits, m_rec_gnorm_w, m_rec_w_out, v_pre_norm_w, v_post_norm_w, v_attn_w_in, v_attn_b_in, v_attn_sinks, v_attn_w_out, v_attn_b_out, v_rec_w_in, v_rec_lb_logits, v_rec_gnorm_w, v_rec_w_out):
    given = dict(x=x, positions=positions, pre_norm_w=pre_norm_w, post_norm_w=post_norm_w, attn_w_in=attn_w_in, attn_b_in=attn_b_in, attn_sinks=attn_sinks, attn_w_out=attn_w_out, attn_b_out=attn_b_out, rec_w_in=rec_w_in, rec_lb_logits=rec_lb_logits, rec_gnorm_w=rec_gnorm_w, rec_w_out=rec_w_out, loss_target=loss_target, m_pre_norm_w=m_pre_norm_w, m_post_norm_w=m_post_norm_w, m_attn_w_in=m_attn_w_in, m_attn_b_in=m_attn_b_in, m_attn_sinks=m_attn_sinks, m_attn_w_out=m_attn_w_out, m_attn_b_out=m_attn_b_out, m_rec_w_in=m_rec_w_in, m_rec_lb_logits=m_rec_lb_logits, m_rec_gnorm_w=m_rec_gnorm_w, m_rec_w_out=m_rec_w_out, v_pre_norm_w=v_pre_norm_w, v_post_norm_w=v_post_norm_w, v_attn_w_in=v_attn_w_in, v_attn_b_in=v_attn_b_in, v_attn_sinks=v_attn_sinks, v_attn_w_out=v_attn_w_out, v_attn_b_out=v_attn_b_out, v_rec_w_in=v_rec_w_in, v_rec_lb_logits=v_rec_lb_logits, v_rec_gnorm_w=v_rec_gnorm_w, v_rec_w_out=v_rec_w_out)
    weights = {n: given[n] for n in TWIN_WEIGHTS}
    shared = {n: given[n] for n in SHARED_INPUTS}
    per_example = {n: given[n] for n in ['x', 'positions']}
    grad_fn = _jax.value_and_grad(_loss, argnums=(0, 1))

    def one_microbatch(ex, loss_target):
        ex = dict(ex)
        diff = ex.pop(TWIN_DIFF_INPUT)
        return grad_fn(weights, diff, {**shared, **ex}, loss_target)

    if N_MICROBATCH == 1:
        loss, (grad_w, grad_x) = one_microbatch(per_example, given["loss_target"])
    else:
        def body(carry, xs):
            loss_sum, grad_sum = carry
            l_k, (gw_k, gx_k) = one_microbatch(xs[0], xs[1])
            with _jax.named_scope("update"):
                return (loss_sum + l_k, _jax.tree.map(_jnp.add, grad_sum, gw_k)), gx_k

        init = (_jnp.zeros((), _jnp.float32), _jax.tree.map(_jnp.zeros_like, weights))
        (loss, grad_w), grad_x = _jax.lax.scan(body, init, (per_example, given["loss_target"]))
    with _jax.named_scope("update"):
        delta_w, new_m, new_v = {}, {}, {}
        for n in TWIN_WEIGHTS:
            delta_w[n], new_m[n], new_v[n] = _adamw(weights[n], grad_w[n], given["m_" + n], given["v_" + n])
    return (loss, grad_x, *[grad_w[n] for n in TWIN_WEIGHTS], *[delta_w[n] for n in TWIN_WEIGHTS],
            *[new_m[n] for n in TWIN_WEIGHTS], *[new_v[n] for n in TWIN_WEIGHTS])
```

```python
import numpy as np
import jax
import jax.numpy as jnp
from jax import lax
from jax.experimental import pallas as pl
from jax.experimental.pallas import tpu as pltpu

F32, BF16 = jnp.float32, jnp.bfloat16
MESH = pl.DeviceIdType.MESH
N_DEV = 8

D_MODEL = 1024
N_HEADS, HEAD_DIM, N_KV, GROUP = 16, 64, 2, 8
ATTN_WIDTH, KV_WIDTH = 1024, 128
ATTN_IN = 2 * ATTN_WIDTH + 2 * KV_WIDTH
BLK = 128
ROPE_THETA, ROPE_HALF = 500000.0, 8
REC_HEADS, REC_K = 8, 128
REC_IN = 4 * 1024
CH = 32
NORM_EPS = 1e-6
ADAM_LR, ADAM_B1, ADAM_B2, ADAM_EPS, ADAM_WD, ADAM_STEP = 0.001, 0.9, 0.999, 1e-08, 0.01, 10

LANES, SUBLANES = 128, 8
TM = 512
NT_DIMS = (((1,), (1,)), ((), ()))
TN_DIMS = (((0,), (0,)), ((), ()))
MB = 2 ** 20


def _params(sem=None, vmem_mb=48, **kw):
    return pltpu.CompilerParams(dimension_semantics=sem, vmem_limit_bytes=vmem_mb * MB, **kw)


def _col_chunk(m):
    return 768 if m % 1024 else 1024


def _sigmoid(x):
    return 1.0 / (1.0 + jnp.exp(-x))


def _split3(x):
    hi = x.astype(BF16)
    r1 = x - hi.astype(F32)
    mid = r1.astype(BF16)
    lo = (r1 - mid.astype(F32)).astype(BF16)
    return hi, mid, lo


def _dot3(l_bf, x):
    hi, mid, lo = _split3(x)
    return (jnp.dot(l_bf, hi, preferred_element_type=F32) + jnp.dot(l_bf, mid, preferred_element_type=F32)
            + jnp.dot(l_bf, lo, preferred_element_type=F32))


def _rope_tables(pos_col):
    n = pos_col.shape[0]
    lane = np.arange(LANES) % HEAD_DIM
    inv = np.float32(ROPE_THETA) ** (-(np.arange(ROPE_HALF, dtype=np.float32) * np.float32(2.0) / np.float32(2 * ROPE_HALF)))
    freq = np.where(lane < 2 * ROPE_HALF, inv[lane % ROPE_HALF], 0.0).astype(np.float32)[None, :]
    sign = np.where(lane < ROPE_HALF, -1.0, np.where(lane < 2 * ROPE_HALF, 1.0, 0.0)).astype(np.float32)[None, :]

    def body(p_ref, f_ref, s_ref, c_out, s_out):
        ang = p_ref[...] * f_ref[...]
        c_out[...] = jnp.cos(ang)
        s_out[...] = jnp.sin(ang) * s_ref[...]

    row = pl.BlockSpec((TM, 1), lambda i: (i, 0))
    vec = pl.BlockSpec((1, LANES), lambda i: (0, 0))
    out = pl.BlockSpec((TM, LANES), lambda i: (i, 0))
    return pl.pallas_call(
        body, name="rope_tables", grid=(n // TM,), in_specs=[row, vec, vec], out_specs=[out, out],
        out_shape=[jax.ShapeDtypeStruct((n, LANES), F32)] * 2, compiler_params=_params(("arbitrary",)),
    )(pos_col, jnp.asarray(freq), jnp.asarray(sign))


def _rope_apply(xv, c, s, lm):
    partner = jnp.where(lm < ROPE_HALF, pltpu.roll(xv, LANES - ROPE_HALF, 1), pltpu.roll(xv, ROPE_HALF, 1))
    return xv * c + partner * s


def _rope_bwd(dy, c, s, lm):
    t = dy * s
    partner = jnp.where(lm < ROPE_HALF, pltpu.roll(t, LANES - ROPE_HALF, 1),
                        jnp.where(lm < 2 * ROPE_HALF, pltpu.roll(t, ROPE_HALF, 1), 0.0))
    return dy * c + partner


def _lower_bound(lb_logits):
    def body(l_ref, o_ref):
        l0, l1 = l_ref[0:1, :], l_ref[1:2, :]
        m = jnp.maximum(l0, l1)
        e0, e1 = jnp.exp(l0 - m), jnp.exp(l1 - m)
        o_ref[...] = e1 / (e0 + e1)

    return pl.pallas_call(body, name="lower_bound", out_shape=jax.ShapeDtypeStruct((1, lb_logits.shape[1]), F32))(lb_logits)


def _norm_matmul(x, pw, w, bias, name):
    n, m = x.shape[0], w.shape[1]
    cn = _col_chunk(m)
    has_bias = bias is not None

    def body(*refs):
        if has_bias:
            x_ref, pw_ref, w_ref, b_ref, p_ref, h_ref = refs
        else:
            x_ref, pw_ref, w_ref, p_ref, h_ref = refs
        xv = x_ref[...]
        r = lax.rsqrt(jnp.mean(xv * xv, axis=-1, keepdims=True) + NORM_EPS)
        h = ((xv * r) * pw_ref[...]).astype(BF16)
        h_ref[...] = h
        for j in range(0, m, cn):
            acc = jnp.dot(h, w_ref[:, j:j + cn], preferred_element_type=F32)
            if has_bias:
                acc = acc + b_ref[:, j:j + cn]
            p_ref[:, j:j + cn] = acc

    rows = pl.BlockSpec((TM, D_MODEL), lambda i: (i, 0))
    const = lambda shape: pl.BlockSpec(shape, lambda i: (0, 0))
    in_specs = [rows, const((1, D_MODEL)), const((D_MODEL, m))] + ([const((1, m))] if has_bias else [])
    args = (x, pw, w) + ((bias,) if has_bias else ())
    return pl.pallas_call(
        body, name=name, grid=(n // TM,), in_specs=in_specs,
        out_specs=[pl.BlockSpec((TM, m), lambda i: (i, 0)), rows],
        out_shape=[jax.ShapeDtypeStruct((n, m), F32), jax.ShapeDtypeStruct((n, D_MODEL), BF16)],
        compiler_params=_params(("arbitrary",), 56),
    )(*args)


def _outproj_postnorm(g, w, bias, xres, qw, tgt, name):
    n = g.shape[0]
    has_bias, has_loss = bias is not None, tgt is not None
    steps = n // TM

    def body(*refs):
        refs = list(refs)
        g_ref, w_ref = refs.pop(0), refs.pop(0)
        b_ref = refs.pop(0) if has_bias else None
        x_ref, qw_ref = refs.pop(0), refs.pop(0)
        t_ref = refs.pop(0) if has_loss else None
        y_ref, o_ref = refs.pop(0), refs.pop(0)
        y = jnp.dot(g_ref[...], w_ref[...], preferred_element_type=F32)
        if has_bias:
            y = y + b_ref[...]
        y_ref[...] = y
        r = lax.rsqrt(jnp.mean(y * y, axis=-1, keepdims=True) + NORM_EPS)
        xn = x_ref[...] + (y * r) * qw_ref[...]
        if not has_loss:
            o_ref[...] = xn
        else:
            loss_ref, acc_ref = refs
            i = pl.program_id(0)
            e = xn - t_ref[...]
            o_ref[...] = e * (1.0 / D_MODEL)

            @pl.when(i == 0)
            def _():
                acc_ref[...] = jnp.zeros_like(acc_ref)

            acc_ref[...] += jnp.sum(e * e, axis=0, keepdims=True)

            @pl.when(i == steps - 1)
            def _():
                loss_ref[...] = jnp.full(loss_ref.shape, jnp.sum(acc_ref[...]) * (0.5 / D_MODEL), F32)

    rows = pl.BlockSpec((TM, D_MODEL), lambda i: (i, 0))
    const = lambda shape: pl.BlockSpec(shape, lambda i: (0, 0))
    in_specs = [rows, const((D_MODEL, D_MODEL))] + ([const((1, D_MODEL))] if has_bias else []) + [rows, const((1, D_MODEL))]
    args = [g, w] + ([bias] if has_bias else []) + [xres, qw]
    out_specs = [rows, rows]
    out_shape = [jax.ShapeDtypeStruct((n, D_MODEL), F32)] * 2
    scratch = []
    if has_loss:
        in_specs.append(rows)
        args.append(tgt)
        out_specs.append(const((SUBLANES, LANES)))
        out_shape.append(jax.ShapeDtypeStruct((SUBLANES, LANES), F32))
        scratch = [pltpu.VMEM((1, D_MODEL), F32)]
    return pl.pallas_call(
        body, name=name, grid=(steps,), in_specs=in_specs, out_specs=out_specs, out_shape=out_shape,
        scratch_shapes=scratch, compiler_params=_params(("arbitrary",), 48),
    )(*args)


_QCOL, _ZCOL, _KCOL, _VCOL = 0, 1024, 2048, 2176


def _attn_probs(qr, kcat, h, sink_ref, valid):
    qst = jnp.concatenate(
        [qr[4 * h + g // 2][:, (g % 2) * HEAD_DIM:(g % 2 + 1) * HEAD_DIM] for g in range(GROUP)], axis=0).astype(BF16)
    kh = kcat[:, h * HEAD_DIM:(h + 1) * HEAD_DIM].astype(BF16)
    s = lax.dot_general(qst, kh, NT_DIMS, preferred_element_type=F32)
    s = jnp.where(valid, s, -1e30)
    sk = jnp.concatenate([jnp.full((BLK, 1), sink_ref[0, h * GROUP + g], F32) for g in range(GROUP)], axis=0)
    m = jnp.maximum(jnp.max(s, axis=-1, keepdims=True), sk)
    p = jnp.exp(s - m)
    esk = jnp.exp(sk - m)
    inv = 1.0 / (jnp.sum(p, axis=-1, keepdims=True) + esk)
    return qst, kh, p * inv, esk * inv


def _attn_mask(i):
    row = lax.broadcasted_iota(jnp.int32, (GROUP * BLK, 2 * BLK), 0) & (BLK - 1)
    col = lax.broadcasted_iota(jnp.int32, (GROUP * BLK, 2 * BLK), 1)
    return (col > row) & (col <= row + BLK) & ((col >= BLK) | (i > 0))


def _unstack_heads(xh, c2):
    return jnp.concatenate([xh[(2 * c2) * BLK:(2 * c2 + 1) * BLK], xh[(2 * c2 + 1) * BLK:(2 * c2 + 2) * BLK]], axis=1)


def _stack_heads(chunks, h):
    return jnp.concatenate(
        [chunks[4 * h + g // 2][:, (g % 2) * HEAD_DIM:(g % 2 + 1) * HEAD_DIM] for g in range(GROUP)], axis=0)


def _attn_fwd(p, ct, st, sinks, b_loc, nb):
    n = p.shape[0]

    def body(sink_ref, q_ref, z_ref, kc_ref, kp_ref, vc_ref, vp_ref, cc_ref, sc_ref, cp_ref, sp_ref, o_ref, g_ref):
        i = pl.program_id(1)
        lane = lax.broadcasted_iota(jnp.int32, (BLK, LANES), 1)
        lm = lane & (HEAD_DIM - 1)
        cc, sc = cc_ref[...], sc_ref[...]
        kcat = jnp.concatenate([_rope_apply(kp_ref[...], cp_ref[...], sp_ref[...], lm),
                                _rope_apply(kc_ref[...], cc, sc, lm)], axis=0)
        vcat = jnp.concatenate([vp_ref[...], vc_ref[...]], axis=0)
        qr = [_rope_apply(q_ref[:, c * LANES:(c + 1) * LANES], cc, sc, lm) * (HEAD_DIM ** -0.5) for c in range(8)]
        valid = _attn_mask(i)
        for h in range(N_KV):
            _, _, pn, _ = _attn_probs(qr, kcat, h, sink_ref, valid)
            vh = vcat[:, h * HEAD_DIM:(h + 1) * HEAD_DIM].astype(BF16)
            oh = jnp.dot(pn.astype(BF16), vh, preferred_element_type=F32)
            for c2 in range(4):
                oc = _unstack_heads(oh, c2)
                cols = slice((4 * h + c2) * LANES, (4 * h + c2 + 1) * LANES)
                zc = z_ref[:, cols]
                o_ref[:, cols] = oc
                g_ref[:, cols] = (oc * (zc * _sigmoid(zc))).astype(BF16)

    cur = lambda b, i: b * nb + i
    prev = lambda b, i: b * nb + jnp.maximum(i - 1, 0)
    wide = lambda cb: pl.BlockSpec((BLK, ATTN_WIDTH), lambda b, i: (cur(b, i), cb))
    kv = lambda rowf, cb: pl.BlockSpec((BLK, LANES), lambda b, i: (rowf(b, i), cb))
    in_specs = [pl.BlockSpec(memory_space=pltpu.SMEM), wide(0), wide(1),
                kv(cur, _KCOL // LANES), kv(prev, _KCOL // LANES), kv(cur, _VCOL // LANES), kv(prev, _VCOL // LANES),
                kv(cur, 0), kv(cur, 0), kv(prev, 0), kv(prev, 0)]
    return pl.pallas_call(
        body, name="attn_fwd", grid=(b_loc, nb), in_specs=in_specs, out_specs=[wide(0), wide(0)],
        out_shape=[jax.ShapeDtypeStruct((n, ATTN_WIDTH), F32), jax.ShapeDtypeStruct((n, ATTN_WIDTH), BF16)],
        compiler_params=_params(("arbitrary", "arbitrary"), 48),
    )(sinks, p, p, p, p, p, p, ct, st, ct, st)


def _attn_bwd(p, ct, st, sinks, o, dg, b_loc, nb):
    n = p.shape[0]

    def body(sink_ref, q_ref, z_ref, kc_ref, kp_ref, vc_ref, vp_ref, cc_ref, sc_ref, cp_ref, sp_ref, o_ref, dg_ref,
             dp_ref, ds_ref, dq_s, dz_s, dk_s, dv_s):
        b, i = pl.program_id(0), pl.program_id(1)
        lane = lax.broadcasted_iota(jnp.int32, (BLK, LANES), 1)
        lm = lane & (HEAD_DIM - 1)

        @pl.when((b == 0) & (i == 0))
        def _():
            ds_ref[...] = jnp.zeros_like(ds_ref)

        @pl.when(i < nb)
        def _compute():
            cc, sc = cc_ref[...], sc_ref[...]
            kcat = jnp.concatenate([_rope_apply(kp_ref[...], cp_ref[...], sp_ref[...], lm),
                                    _rope_apply(kc_ref[...], cc, sc, lm)], axis=0)
            vcat = jnp.concatenate([vp_ref[...], vc_ref[...]], axis=0)
            qr = [_rope_apply(q_ref[:, c * LANES:(c + 1) * LANES], cc, sc, lm) * (HEAD_DIM ** -0.5) for c in range(8)]
            valid = _attn_mask(i)
            do_chunks, o_chunks, dz_chunks = [], [], []
            for c in range(8):
                cols = slice(c * LANES, (c + 1) * LANES)
                zc, oc, dgc = z_ref[:, cols], o_ref[:, cols], dg_ref[:, cols]
                sg = _sigmoid(zc)
                do_chunks.append(dgc * (zc * sg))
                dz_chunks.append(dgc * oc * (sg * (1.0 + zc * (1.0 - sg))))
                o_chunks.append(oc)
            dq_chunks = [None] * 8
            dk_h, dv_h = [], []
            ds_acc = jnp.zeros((SUBLANES, LANES), F32)
            tile_lane = lax.broadcasted_iota(jnp.int32, (SUBLANES, LANES), 1)
            tile_row = lax.broadcasted_iota(jnp.int32, (SUBLANES, LANES), 0)
            for h in range(N_KV):
                qst, kh, pn, psink = _attn_probs(qr, kcat, h, sink_ref, valid)
                vh = vcat[:, h * HEAD_DIM:(h + 1) * HEAD_DIM].astype(BF16)
                do_st = _stack_heads(do_chunks, h)
                delta = jnp.sum(do_st * _stack_heads(o_chunks, h), axis=-1, keepdims=True)
                do_bf = do_st.astype(BF16)
                dpm = lax.dot_general(do_bf, vh, NT_DIMS, preferred_element_type=F32)
                dsm = (pn * (dpm - delta)).astype(BF16)
                sink_term = psink * delta
                for g in range(GROUP):
                    val = -jnp.sum(sink_term[g * BLK:(g + 1) * BLK])
                    ds_acc = ds_acc + jnp.where((tile_lane == h * GROUP + g) & (tile_row == 0), val, 0.0)
                dq_st = jnp.dot(dsm, kh, preferred_element_type=F32) * (HEAD_DIM ** -0.5)
                dk_h.append(lax.dot_general(dsm, qst, TN_DIMS, preferred_element_type=F32))
                dv_h.append(lax.dot_general(pn.astype(BF16), do_bf, TN_DIMS, preferred_element_type=F32))
                for c2 in range(4):
                    dq_chunks[4 * h + c2] = _rope_bwd(_unstack_heads(dq_st, c2), cc, sc, lm)
            ds_ref[...] += ds_acc
            dk_full = jnp.concatenate(dk_h, axis=1)
            dv_full = jnp.concatenate(dv_h, axis=1)

            @pl.when(i >= 1)
            def _emit():
                dp_ref[:, _QCOL:_QCOL + ATTN_WIDTH] = dq_s[...]
                dp_ref[:, _ZCOL:_ZCOL + ATTN_WIDTH] = dz_s[...]
                dp_ref[:, _KCOL:_KCOL + KV_WIDTH] = _rope_bwd(dk_s[...] + dk_full[:BLK], cp_ref[...], sp_ref[...], lm)
                dp_ref[:, _VCOL:_VCOL + KV_WIDTH] = dv_s[...] + dv_full[:BLK]

            for c in range(8):
                dq_s[:, c * LANES:(c + 1) * LANES] = dq_chunks[c]
                dz_s[:, c * LANES:(c + 1) * LANES] = dz_chunks[c]
            dk_s[...] = dk_full[BLK:]
            dv_s[...] = dv_full[BLK:]

        @pl.when(i == nb)
        def _final():
            dp_ref[:, _QCOL:_QCOL + ATTN_WIDTH] = dq_s[...]
            dp_ref[:, _ZCOL:_ZCOL + ATTN_WIDTH] = dz_s[...]
            dp_ref[:, _KCOL:_KCOL + KV_WIDTH] = _rope_bwd(dk_s[...], cc_ref[...], sc_ref[...], lm)
            dp_ref[:, _VCOL:_VCOL + KV_WIDTH] = dv_s[...]

    cur = lambda b, i: b * nb + jnp.minimum(i, nb - 1)
    prev = lambda b, i: b * nb + jnp.maximum(jnp.minimum(i, nb - 1) - 1, 0)
    emit = lambda b, i: b * nb + jnp.maximum(i - 1, 0)
    wide = lambda cb: pl.BlockSpec((BLK, ATTN_WIDTH), lambda b, i: (cur(b, i), cb))
    kv = lambda rowf, cb: pl.BlockSpec((BLK, LANES), lambda b, i: (rowf(b, i), cb))
    in_specs = [pl.BlockSpec(memory_space=pltpu.SMEM), wide(0), wide(1),
                kv(cur, _KCOL // LANES), kv(prev, _KCOL // LANES), kv(cur, _VCOL // LANES), kv(prev, _VCOL // LANES),
                kv(cur, 0), kv(cur, 0), kv(prev, 0), kv(prev, 0), wide(0), wide(0)]
    out_specs = [pl.BlockSpec((BLK, ATTN_IN), lambda b, i: (emit(b, i), 0)),
                 pl.BlockSpec((SUBLANES, LANES), lambda b, i: (0, 0))]
    return pl.pallas_call(
        body, name="attn_bwd", grid=(b_loc, nb + 1), in_specs=in_specs, out_specs=out_specs,
        out_shape=[jax.ShapeDtypeStruct((n, ATTN_IN), F32), jax.ShapeDtypeStruct((SUBLANES, LANES), F32)],
        scratch_shapes=[pltpu.VMEM((BLK, ATTN_WIDTH), F32), pltpu.VMEM((BLK, ATTN_WIDTH), F32),
                        pltpu.VMEM((BLK, KV_WIDTH), F32), pltpu.VMEM((BLK, KV_WIDTH), F32)],
        compiler_params=_params(("arbitrary", "arbitrary"), 48),
    )(sinks, p, p, p, p, p, p, ct, st, ct, st, o, dg)


_CUM_ROWS = 256
_ROWS_OF = [CH - SUBLANES * (s // SUBLANES) for s in range(CH)]
_OFF_OF = [sum(_ROWS_OF[:s]) for s in range(CH)]
_PAIR_ROWS = sum(_ROWS_OF)


def _tri(lower):
    r = lax.broadcasted_iota(jnp.int32, (_CUM_ROWS, _CUM_ROWS), 0)
    c = lax.broadcasted_iota(jnp.int32, (_CUM_ROWS, _CUM_ROWS), 1)
    same = (r ^ c) < CH
    return (same & ((c <= r) if lower else (c >= r))).astype(BF16)


def _gates(qp, fp, lb):
    e = jnp.exp(-jnp.abs(fp))
    log_sig = jnp.minimum(fp, 0.0) - jnp.log(1.0 + e)
    a = jnp.log(lb)
    c = jnp.log(1.0 - lb) + log_sig
    g = jnp.maximum(a, c) + jnp.log(1.0 + jnp.exp(-jnp.abs(a - c)))
    sig_neg = jnp.where(fp >= 0, e, 1.0) / (1.0 + e)
    return qp * _sigmoid(qp), g, (1.0 - lb) * sig_neg, sig_neg


def _pair_rows(bc, s):
    r0 = SUBLANES * (s // SUBLANES)
    return jnp.exp(jnp.minimum(bc[r0:, :] - bc[s:s + 1, :], 0.0))


def _hgrn_specs(tb, nt, reverse):
    tmap = (lambda t: nt - 1 - t) if reverse else (lambda t: t)
    blk = lambda cb: pl.BlockSpec((tb, REC_K), lambda h, b, t: (b * nt + tmap(t), cb * REC_HEADS + h))
    head = pl.BlockSpec((tb, REC_K), lambda h, b, t: (b * nt + tmap(t), h))
    lbs = pl.BlockSpec((1, REC_K), lambda h, b, t: (0, h))
    gws = pl.BlockSpec((1, REC_K), lambda h, b, t: (0, 0))
    hist = pl.BlockSpec((1, 1, tb // CH, REC_K, REC_K), lambda h, b, t: (h, b, tmap(t), 0, 0))
    return blk, head, lbs, gws, hist


def _hgrn_fwd(p, lb, gw, b_loc, t_len):
    n = p.shape[0]
    tb = min(TM, t_len)
    nt, nck = t_len // tb, tb // CH

    def body(qp_ref, fp_ref, i_ref, z_ref, lb_ref, gw_ref, oraw_ref, g_ref, sh_ref, q_s, k_s, b_s, o_s, st_ref):
        @pl.when(pl.program_id(2) == 0)
        def _():
            st_ref[...] = jnp.zeros_like(st_ref)

        qv, g, kk, _ = _gates(qp_ref[...], fp_ref[...], lb_ref[...])
        q_s[...] = qv
        k_s[...] = kk
        low = _tri(True)
        for r in range(0, tb, _CUM_ROWS):
            b_s[r:r + _CUM_ROWS, :] = _dot3(low, g[r:r + _CUM_ROWS, :])
        ones = jnp.ones((REC_K, REC_K), BF16)
        sub = lax.broadcasted_iota(jnp.int32, (SUBLANES, REC_K), 0)

        def chunk(c, carry):
            rs = pl.ds(pl.multiple_of(c * CH, CH), CH)
            q, k, bc, v = q_s[rs, :], k_s[rs, :], b_s[rs, :], i_ref[rs, :]
            bl = bc[CH - 1:CH, :]
            st = st_ref[...]
            sh_ref[0, 0, c] = st
            o = lax.dot_general((q * jnp.exp(bc)).astype(BF16), st.astype(BF16), NT_DIMS, preferred_element_type=F32)
            w = jnp.concatenate([q[SUBLANES * (s // SUBLANES):, :] * _pair_rows(bc, s) * k[s:s + 1, :] for s in range(CH)], axis=0)
            a = jnp.dot(w.astype(BF16), ones, preferred_element_type=F32)
            acc = [jnp.zeros((SUBLANES, REC_K), F32) for _ in range(CH // SUBLANES)]
            for s in range(CH):
                j = s // SUBLANES
                vs = v[s:s + 1, :]
                for jj in range(j, CH // SUBLANES):
                    blk = a[_OFF_OF[s] + (jj - j) * SUBLANES:_OFF_OF[s] + (jj - j + 1) * SUBLANES, :]
                    if jj == j:
                        blk = jnp.where(sub >= s - j * SUBLANES, blk, 0.0)
                    acc[jj] = acc[jj] + blk * vs
            o_s[rs, :] = o + jnp.concatenate(acc, axis=0)
            kd = k * jnp.exp(bl - bc)
            st_ref[...] = st * jnp.exp(bl) + lax.dot_general(v.astype(BF16), kd.astype(BF16), TN_DIMS, preferred_element_type=F32)
            return carry

        lax.fori_loop(0, nck, chunk, 0)
        o = o_s[...]
        oraw_ref[...] = o
        zc = z_ref[...]
        on = (o * lax.rsqrt(jnp.mean(o * o, axis=-1, keepdims=True) + NORM_EPS)) * gw_ref[...]
        g_ref[...] = (on * (zc * _sigmoid(zc))).astype(BF16)

    blk, head, lbs, gws, hist = _hgrn_specs(tb, nt, False)
    return pl.pallas_call(
        body, name="hgrn_fwd", grid=(REC_HEADS, b_loc, nt),
        in_specs=[blk(0), blk(1), blk(2), blk(3), lbs, gws], out_specs=[head, head, hist],
        out_shape=[jax.ShapeDtypeStruct((n, 1024), F32), jax.ShapeDtypeStruct((n, 1024), BF16),
                   jax.ShapeDtypeStruct((REC_HEADS, b_loc, t_len // CH, REC_K, REC_K), F32)],
        scratch_shapes=[pltpu.VMEM((tb, REC_K), F32)] * 4 + [pltpu.VMEM((REC_K, REC_K), F32)],
        compiler_params=_params(("arbitrary", "arbitrary", "arbitrary"), 48),
    )(p, p, p, p, lb, gw)


def _hgrn_bwd(p, lb, gw, oraw, sh, dg, b_loc, t_len):
    n = p.shape[0]
    tb = min(TM, t_len)
    nt, nck = t_len // tb, tb // CH

    def body(qp_ref, fp_ref, i_ref, z_ref, lb_ref, gw_ref, oraw_ref, dg_ref, sh_ref,
             dq_ref, df_ref, di_ref, dz_ref, dlb_ref, dgw_ref,
             q_s, k_s, b_s, do_s, dqv_s, dk_s, db_s, rowk_s, rowv_s, dst_ref):
        b, t = pl.program_id(1), pl.program_id(2)

        @pl.when(t == 0)
        def _():
            dst_ref[...] = jnp.zeros_like(dst_ref)

        @pl.when((b == 0) & (t == 0))
        def _():
            dlb_ref[...] = jnp.zeros_like(dlb_ref)
            dgw_ref[...] = jnp.zeros_like(dgw_ref)

        lbv, qp, fp = lb_ref[...], qp_ref[...], fp_ref[...]
        qv, g, kk, sig_neg = _gates(qp, fp, lbv)
        q_s[...] = qv
        k_s[...] = kk
        low = _tri(True)
        for r in range(0, tb, _CUM_ROWS):
            b_s[r:r + _CUM_ROWS, :] = _dot3(low, g[r:r + _CUM_ROWS, :])
        o, zc, dgv, gwv = oraw_ref[...], z_ref[...], dg_ref[...], gw_ref[...]
        rn = lax.rsqrt(jnp.mean(o * o, axis=-1, keepdims=True) + NORM_EPS)
        on = o * rn
        sgz = _sigmoid(zc)
        dz_ref[...] = dgv * (on * gwv) * (sgz * (1.0 + zc * (1.0 - sgz)))
        dpre = dgv * (zc * sgz)
        dgw_ref[0] += jnp.sum(dpre * on, axis=0, keepdims=True)
        don = dpre * gwv
        do_s[...] = rn * (don - on * jnp.mean(don * on, axis=-1, keepdims=True))

        ones = jnp.ones((REC_K, REC_K), BF16)
        sub = lax.broadcasted_iota(jnp.int32, (SUBLANES, REC_K), 0)
        rowid = lax.broadcasted_iota(jnp.int32, (CH, REC_K), 0)
        ngrp = CH // SUBLANES

        def chunk(ci, carry):
            c = nck - 1 - ci
            rs = pl.ds(pl.multiple_of(c * CH, CH), CH)
            q, k, bc, v, do = q_s[rs, :], k_s[rs, :], b_s[rs, :], i_ref[rs, :], do_s[rs, :]
            bl = bc[CH - 1:CH, :]
            st, dst = sh_ref[0, 0, c], dst_ref[...]
            eb, ebl, ekd = jnp.exp(bc), jnp.exp(bl), jnp.exp(bl - bc)
            qe, kd = q * eb, k * ekd
            do_bf, dst_bf = do.astype(BF16), dst.astype(BF16)
            dqe = jnp.dot(do_bf, st.astype(BF16), preferred_element_type=F32)
            dkd = jnp.dot(v.astype(BF16), dst_bf, preferred_element_type=F32)
            dv = lax.dot_general(kd.astype(BF16), dst_bf, NT_DIMS, preferred_element_type=F32)
            dbl = jnp.sum(dkd * kd, axis=0, keepdims=True) + ebl * jnp.sum(st * dst, axis=0, keepdims=True)
            dst_ref[...] = dst * ebl + lax.dot_general(do_bf, qe.astype(BF16), TN_DIMS, preferred_element_type=F32)
            dec = [_pair_rows(bc, s) for s in range(CH)]
            w = jnp.concatenate([q[SUBLANES * (s // SUBLANES):, :] * dec[s] * k[s:s + 1, :] for s in range(CH)], axis=0)
            x = jnp.concatenate([do[SUBLANES * (s // SUBLANES):, :] * v[s:s + 1, :] for s in range(CH)], axis=0)
            a = jnp.dot(w.astype(BF16), ones, preferred_element_type=F32)
            da = jnp.dot(x.astype(BF16), ones, preferred_element_type=F32)
            dq_acc = [jnp.zeros((SUBLANES, REC_K), F32) for _ in range(ngrp)]
            for s in range(CH):
                j = s // SUBLANES
                r0 = j * SUBLANES
                ks = k[s:s + 1, :]
                uk = jnp.zeros((SUBLANES, REC_K), F32)
                uv = jnp.zeros((SUBLANES, REC_K), F32)
                for jj in range(j, ngrp):
                    lo, hi = _OFF_OF[s] + (jj - j) * SUBLANES, _OFF_OF[s] + (jj - j + 1) * SUBLANES
                    a_blk, da_blk = a[lo:hi, :], da[lo:hi, :]
                    if jj == j:
                        keep = sub >= s - r0
                        a_blk, da_blk = jnp.where(keep, a_blk, 0.0), jnp.where(keep, da_blk, 0.0)
                    rows = slice(jj * SUBLANES, (jj + 1) * SUBLANES)
                    tt = da_blk * dec[s][(jj - j) * SUBLANES:(jj - j + 1) * SUBLANES, :]
                    dq_acc[jj] = dq_acc[jj] + tt * ks
                    uk = uk + tt * q[rows, :]
                    uv = uv + a_blk * do[rows, :]
                rowk_s[s:s + 1, :] = jnp.sum(uk, axis=0, keepdims=True)
                rowv_s[s:s + 1, :] = jnp.sum(uv, axis=0, keepdims=True)
            dq_in = jnp.concatenate(dq_acc, axis=0)
            dk_in = rowk_s[...]
            dqv_s[rs, :] = dqe * eb + dq_in
            dk_s[rs, :] = dkd * ekd + dk_in
            di_ref[rs, :] = dv + rowv_s[...]
            db = dqe * qe - dkd * kd + q * dq_in - k * dk_in
            db_s[rs, :] = db + jnp.where(rowid == CH - 1, dbl, 0.0)
            return carry

        lax.fori_loop(0, nck, chunk, 0)
        up = _tri(False)
        sgq = _sigmoid(qp)
        dq_ref[...] = dqv_s[...] * (sgq * (1.0 + qp * (1.0 - sgq)))
        dlb_acc = jnp.zeros((1, REC_K), F32)
        for r in range(0, tb, _CUM_ROWS):
            rows = slice(r, r + _CUM_ROWS)
            dgl = _dot3(up, db_s[rows, :])
            dfg = dgl * jnp.exp(-g[rows, :]) - dk_s[rows, :]
            sn = sig_neg[rows, :]
            df_ref[rows, :] = dfg * (1.0 - lbv) * (1.0 - sn) * sn
            dlb_acc = dlb_acc + jnp.sum(dfg * sn, axis=0, keepdims=True)
        dlb_ref[...] += dlb_acc

    blk, head, lbs, gws, hist = _hgrn_specs(tb, nt, True)
    out_specs = [head, head, head, head, lbs, pl.BlockSpec((1, 1, REC_K), lambda h, b, t: (h, 0, 0))]
    out_shape = [jax.ShapeDtypeStruct((n, 1024), F32)] * 4 + [
        jax.ShapeDtypeStruct((1, 1024), F32), jax.ShapeDtypeStruct((REC_HEADS, 1, REC_K), F32)]
    return pl.pallas_call(
        body, name="hgrn_bwd", grid=(REC_HEADS, b_loc, nt),
        in_specs=[blk(0), blk(1), blk(2), blk(3), lbs, gws, head, head, hist], out_specs=out_specs, out_shape=out_shape,
        scratch_shapes=[pltpu.VMEM((tb, REC_K), F32)] * 7 + [pltpu.VMEM((CH, REC_K), F32)] * 2 + [pltpu.VMEM((REC_K, REC_K), F32)],
        compiler_params=_params(("arbitrary", "arbitrary", "arbitrary"), 48),
    )(p, p, p, p, lb, gw, oraw, dg, sh)


def _postnorm_bwd_nt(dxo, y, qw, w, has_bias, name):
    n = dxo.shape[0]

    def body(dx_ref, y_ref, qw_ref, w_ref, dg_ref, dy_ref, dqw_ref, db_ref):
        @pl.when(pl.program_id(0) == 0)
        def _():
            dqw_ref[...] = jnp.zeros_like(dqw_ref)
            db_ref[...] = jnp.zeros_like(db_ref)

        yv, dxv = y_ref[...], dx_ref[...]
        r = lax.rsqrt(jnp.mean(yv * yv, axis=-1, keepdims=True) + NORM_EPS)
        u = yv * r
        du = dxv * qw_ref[...]
        dy = r * (du - u * jnp.mean(du * u, axis=-1, keepdims=True))
        dqw_ref[...] += jnp.sum(dxv * u, axis=0, keepdims=True)
        if has_bias:
            db_ref[...] += jnp.sum(dy, axis=0, keepdims=True)
        dyb = dy.astype(BF16)
        dy_ref[...] = dyb
        dg_ref[...] = lax.dot_general(dyb, w_ref[...], NT_DIMS, preferred_element_type=F32)

    rows = pl.BlockSpec((TM, D_MODEL), lambda i: (i, 0))
    const = lambda shape: pl.BlockSpec(shape, lambda i: (0, 0))
    return pl.pallas_call(
        body, name=name, grid=(n // TM,), in_specs=[rows, rows, const((1, D_MODEL)), const((D_MODEL, D_MODEL))],
        out_specs=[rows, rows, const((1, D_MODEL)), const((1, D_MODEL))],
        out_shape=[jax.ShapeDtypeStruct((n, D_MODEL), F32), jax.ShapeDtypeStruct((n, D_MODEL), BF16),
                   jax.ShapeDtypeStruct((1, D_MODEL), F32), jax.ShapeDtypeStruct((1, D_MODEL), F32)],
        compiler_params=_params(("arbitrary",), 48),
    )(dxo, y, qw, w)


def _nt_prenorm_bwd(dps, w, x, pw, dxo, has_bias, name):
    n = x.shape[0]
    widths = [d.shape[1] for d in dps]
    m = sum(widths)
    npieces = len(dps)

    def body(*refs):
        dp_refs = refs[:npieces]
        w_ref, x_ref, pw_ref, dxo_ref, dx_ref, dpw_ref, db_ref = refs[npieces:]

        @pl.when(pl.program_id(0) == 0)
        def _():
            dpw_ref[...] = jnp.zeros_like(dpw_ref)
            db_ref[...] = jnp.zeros_like(db_ref)

        dh = jnp.zeros((TM, D_MODEL), F32)
        off = 0
        for dp_ref, wd in zip(dp_refs, widths):
            cn = _col_chunk(wd)
            for j in range(0, wd, cn):
                dpc = dp_ref[:, j:j + cn]
                if has_bias:
                    db_ref[:, off + j:off + j + cn] += jnp.sum(dpc, axis=0, keepdims=True)
                dh = dh + lax.dot_general(dpc.astype(BF16), w_ref[:, off + j:off + j + cn], NT_DIMS, preferred_element_type=F32)
            off += wd
        xv = x_ref[...]
        r = lax.rsqrt(jnp.mean(xv * xv, axis=-1, keepdims=True) + NORM_EPS)
        xn = xv * r
        dpw_ref[...] += jnp.sum(dh * xn, axis=0, keepdims=True)
        dxn = dh * pw_ref[...]
        dx_ref[...] = dxo_ref[...] + r * (dxn - xn * jnp.mean(dxn * xn, axis=-1, keepdims=True))

    rows = pl.BlockSpec((TM, D_MODEL), lambda i: (i, 0))
    const = lambda shape: pl.BlockSpec(shape, lambda i: (0, 0))
    in_specs = [pl.BlockSpec((TM, wd), lambda i: (i, 0)) for wd in widths] + [const((D_MODEL, m)), rows, const((1, D_MODEL)), rows]
    return pl.pallas_call(
        body, name=name, grid=(n // TM,), in_specs=in_specs,
        out_specs=[rows, const((1, D_MODEL)), const((1, m))],
        out_shape=[jax.ShapeDtypeStruct((n, D_MODEL), F32), jax.ShapeDtypeStruct((1, D_MODEL), F32),
                   jax.ShapeDtypeStruct((1, m), F32)],
        compiler_params=_params(("arbitrary",), 56),
    )(*dps, w, x, pw, dxo)


def _matmul_tn(a, b, name):
    n, k = a.shape
    m = b.shape[1]
    tk, tm, tn = 512, _col_chunk(m), 1024 if n % 1024 == 0 else n

    def body(a_ref, b_ref, o_ref):
        @pl.when(pl.program_id(2) == 0)
        def _():
            o_ref[...] = jnp.zeros_like(o_ref)

        o_ref[...] += lax.dot_general(a_ref[...], b_ref[...].astype(BF16), TN_DIMS, preferred_element_type=F32)

    return pl.pallas_call(
        body, name=name, grid=(k // tk, m // tm, n // tn),
        in_specs=[pl.BlockSpec((tn, tk), lambda i, j, l: (l, i)), pl.BlockSpec((tn, tm), lambda i, j, l: (l, j))],
        out_specs=pl.BlockSpec((tk, tm), lambda i, j, l: (i, j)),
        out_shape=jax.ShapeDtypeStruct((k, m), F32),
        compiler_params=_params(("arbitrary", "arbitrary", "arbitrary"), 48),
    )(a, b)


def _local_step(x, pos_col, tgt, pre_w, post_w, wa_in, ba_in, sinks, wa_out, ba_out, wr_in, lb_logits, gnorm_w, wr_out, b_loc, t_len):
    nb = t_len // BLK
    ct, st = _rope_tables(pos_col)
    lb = _lower_bound(lb_logits)
    p0, h0 = _norm_matmul(x, pre_w[0:1], wa_in, ba_in, "attn_in_proj")
    o0, g0 = _attn_fwd(p0, ct, st, sinks, b_loc, nb)
    y0, x1 = _outproj_postnorm(g0, wa_out, ba_out, x, post_w[0:1], None, "attn_out_proj")
    p1, h1 = _norm_matmul(x1, pre_w[1:2], wr_in, None, "rec_in_proj")
    o1, g1, sh = _hgrn_fwd(p1, lb, gnorm_w, b_loc, t_len)
    y1, dx2, loss_tile = _outproj_postnorm(g1, wr_out, None, x1, post_w[1:2], tgt, "rec_out_proj_loss")
    dg1, dy1, dpost1, _ = _postnorm_bwd_nt(dx2, y1, post_w[1:2], wr_out, False, "rec_out_bwd")
    d_wr_out = _matmul_tn(g1, dy1, "rec_w_out_grad")
    dq1, df1, di1, dz1, dlb, dgw = _hgrn_bwd(p1, lb, gnorm_w, o1, sh, dg1, b_loc, t_len)
    dps1 = [dq1, df1, di1, dz1]
    dx1, dpre1, _ = _nt_prenorm_bwd(dps1, wr_in, x1, pre_w[1:2], dx2, False, "rec_in_bwd")
    d_wr_in = [_matmul_tn(h1, dpk, f"rec_w_in_grad_{k}") for k, dpk in enumerate(dps1)]
    dg0, dy0, dpost0, dba_out = _postnorm_bwd_nt(dx1, y0, post_w[0:1], wa_out, True, "attn_out_bwd")
    d_wa_out = _matmul_tn(g0, dy0, "attn_w_out_grad")
    dp0, dsink_tile = _attn_bwd(p0, ct, st, sinks, o0, dg0, b_loc, nb)
    dx0, dpre0, dba_in = _nt_prenorm_bwd([dp0], wa_in, x, pre_w[0:1], dx1, True, "attn_in_bwd")
    d_wa_in = _matmul_tn(h0, dp0, "attn_w_in_grad")
    small = dict(pre=jnp.concatenate([dpre0, dpre1], axis=0), post=jnp.concatenate([dpost0, dpost1], axis=0),
                 ba_in=dba_in, sinks=dsink_tile[0:1, 0:N_HEADS], ba_out=dba_out, lb=dlb, gnorm=jnp.sum(dgw, axis=0))
    return loss_tile, dx0, d_wa_in, d_wa_out, d_wr_in, d_wr_out, small


def _my_id():
    return lax.axis_index("x") * 4 + lax.axis_index("y") * 2 + lax.axis_index("c")


def _peer(k):
    x, y, c = lax.axis_index("x"), lax.axis_index("y"), lax.axis_index("c")
    return (x ^ ((k >> 2) & 1), y ^ ((k >> 1) & 1), c ^ (k & 1))


def _peer_id(k):
    return _my_id() ^ k


def _all_gather(shards):
    nsh = len(shards)

    def body(*refs):
        ins, outs = refs[:nsh], refs[nsh:2 * nsh]
        send_sems, recv_sems, local_sems = refs[2 * nsh:]
        me = _my_id()
        copies = []
        for a in range(nsh):
            mine = pltpu.make_async_copy(ins[a], outs[a].at[me], local_sems.at[a])
            mine.start()
            copies.append(mine)
        remote = []
        for a in range(nsh):
            for k in range(1, N_DEV):
                cp = pltpu.make_async_remote_copy(src_ref=ins[a], dst_ref=outs[a].at[me], send_sem=send_sems.at[a, k - 1],
                                                  recv_sem=recv_sems.at[a, k - 1], device_id=_peer(k), device_id_type=MESH)
                cp.start()
                remote.append((a, k, cp))
        for a, k, cp in remote:
            pltpu.make_async_remote_copy(src_ref=ins[a], dst_ref=outs[a].at[_peer_id(k)], send_sem=send_sems.at[a, k - 1],
                                         recv_sem=recv_sems.at[a, k - 1], device_id=_peer(k), device_id_type=MESH).wait_recv()
        for a, k, cp in remote:
            cp.wait_send()
        for cp in copies:
            cp.wait()

    hbm = pl.BlockSpec(memory_space=pl.ANY)
    return pl.pallas_call(
        body, name="comm_all_gather", in_specs=[hbm] * nsh, out_specs=[hbm] * nsh,
        out_shape=[jax.ShapeDtypeStruct((N_DEV,) + s.shape, s.dtype) for s in shards],
        scratch_shapes=[pltpu.SemaphoreType.DMA((nsh, N_DEV - 1)), pltpu.SemaphoreType.DMA((nsh, N_DEV - 1)),
                        pltpu.SemaphoreType.DMA((nsh,))],
    )(*shards)


def _adamw(w, g, m, v):
    m2 = ADAM_B1 * m + (1.0 - ADAM_B1) * g
    v2 = ADAM_B2 * v + (1.0 - ADAM_B2) * (g * g)
    m_hat = m2 / (1.0 - ADAM_B1 ** ADAM_STEP)
    v_hat = v2 / (1.0 - ADAM_B2 ** ADAM_STEP)
    delta = -ADAM_LR * (m_hat / (jnp.sqrt(v_hat) + ADAM_EPS) + ADAM_WD * w)
    return delta, m2, v2


def _reduce_scatter_adamw(part_bf, own, w, m, v, name):
    r, c = own.shape
    rc = 64 if r % 64 == 0 else r

    def body(part_ref, own_ref, w_ref, m_ref, v_ref, g_ref, d_ref, m2_ref, v2_ref, land_ref, send_sems, recv_sems):
        me = _my_id()
        sent = []
        for k in range(1, N_DEV):
            cp = pltpu.make_async_remote_copy(src_ref=part_ref.at[_peer_id(k)], dst_ref=land_ref.at[k - 1],
                                              send_sem=send_sems.at[k - 1], recv_sem=recv_sems.at[k - 1],
                                              device_id=_peer(k), device_id_type=MESH)
            cp.start()
            sent.append(cp)
        for cp in sent:
            cp.wait_recv()
        for cp in sent:
            cp.wait_send()

        def rows(i, carry):
            rs = pl.ds(pl.multiple_of(i * rc, rc), rc)
            g = jnp.zeros((rc, c), F32)
            for dev in range(N_DEV):
                k = dev ^ me
                share = jnp.where(k == 0, own_ref[rs, :], land_ref[jnp.maximum(k - 1, 0), rs, :].astype(F32))
                g = g + share
            delta, m2, v2 = _adamw(w_ref[rs, :], g, m_ref[rs, :], v_ref[rs, :])
            g_ref[rs, :] = g
            d_ref[rs, :] = delta
            m2_ref[rs, :] = m2
            v2_ref[rs, :] = v2
            return carry

        lax.fori_loop(0, r // rc, rows, 0)

    vmem = pl.BlockSpec(memory_space=pltpu.VMEM)
    return pl.pallas_call(
        body, name=name, in_specs=[pl.BlockSpec(memory_space=pl.ANY)] + [vmem] * 4, out_specs=[vmem] * 4,
        out_shape=[jax.ShapeDtypeStruct((r, c), F32)] * 4,
        scratch_shapes=[pltpu.VMEM((N_DEV - 1, r, c), BF16), pltpu.SemaphoreType.DMA((N_DEV - 1,)), pltpu.SemaphoreType.DMA((N_DEV - 1,))],
        compiler_params=_params(None, 56),
    )(part_bf, own, w, m, v)


_SMALL = [("pre_norm_w", 2048), ("post_norm_w", 2048), ("attn_b_in", 2304), ("attn_sinks", 16), ("attn_b_out", 1024),
          ("rec_lb_logits", 2048), ("rec_gnorm_w", 128)]
_TILE = SUBLANES * LANES


def _small_rows(size):
    return -(-size // _TILE) * SUBLANES


_SMALL_OFF = {}
_r = 0
for _name, _size in _SMALL:
    _SMALL_OFF[_name] = _r
    _r += _small_rows(_size)
_SMALL_ROWS = _r


def _pack_small(pieces):
    out = []
    for name, size in _SMALL:
        flat = pieces[name].reshape(-1).astype(F32)
        out.append(jnp.pad(flat, (0, _small_rows(size) * LANES - size)).reshape(-1, LANES))
    return jnp.concatenate(out, axis=0)


def _unpack_small(packed, shapes):
    return {name: packed[_SMALL_OFF[name]:_SMALL_OFF[name] + _small_rows(size)].reshape(-1)[:size].reshape(shapes[name])
            for name, size in _SMALL}


def _small_allreduce_adamw(gpart, w, m, v):
    lb0 = _SMALL_OFF["rec_lb_logits"]

    def body(gp_ref, w_ref, m_ref, v_ref, g_ref, d_ref, m2_ref, v2_ref, land_ref, send_sems, recv_sems):
        me = _my_id()
        sent = []
        for k in range(1, N_DEV):
            cp = pltpu.make_async_remote_copy(src_ref=gp_ref, dst_ref=land_ref.at[k - 1], send_sem=send_sems.at[k - 1],
                                              recv_sem=recv_sems.at[k - 1], device_id=_peer(k), device_id_type=MESH)
            cp.start()
            sent.append(cp)
        for cp in sent:
            cp.wait_recv()
        for cp in sent:
            cp.wait_send()
        g = jnp.zeros((_SMALL_ROWS, LANES), F32)
        for dev in range(N_DEV):
            k = dev ^ me
            g = g + jnp.where(k == 0, gp_ref[...], land_ref[jnp.maximum(k - 1, 0)])
        g_ref[...] = g
        l0, l1 = w_ref[lb0:lb0 + SUBLANES, :], w_ref[lb0 + SUBLANES:lb0 + 2 * SUBLANES, :]
        mx = jnp.maximum(l0, l1)
        e0, e1 = jnp.exp(l0 - mx), jnp.exp(l1 - mx)
        p1 = e1 / (e0 + e1)
        dl1 = (1.0 - p1) * p1 * g[lb0:lb0 + SUBLANES, :]
        g_ref[lb0:lb0 + SUBLANES, :] = -dl1
        g_ref[lb0 + SUBLANES:lb0 + 2 * SUBLANES, :] = dl1
        delta, m2, v2 = _adamw(w_ref[...], g_ref[...], m_ref[...], v_ref[...])
        d_ref[...] = delta
        m2_ref[...] = m2
        v2_ref[...] = v2

    vmem = pl.BlockSpec(memory_space=pltpu.VMEM)
    return pl.pallas_call(
        body, name="comm_small_allreduce_adamw", in_specs=[vmem] * 4, out_specs=[vmem] * 4,
        out_shape=[jax.ShapeDtypeStruct((_SMALL_ROWS, LANES), F32)] * 4,
        scratch_shapes=[pltpu.VMEM((N_DEV - 1, _SMALL_ROWS, LANES), F32), pltpu.SemaphoreType.DMA((N_DEV - 1,)),
                        pltpu.SemaphoreType.DMA((N_DEV - 1,))],
    )(gpart, w, m, v)


def _qzkv(a):
    return jnp.concatenate([a[..., :1024], a[..., 1280:], a[..., 1024:1280]], axis=-1)


def _qkvz(a):
    return jnp.concatenate([a[..., :1024], a[..., 2048:], a[..., 1024:2048]], axis=-1)


def _by_owner_cols(dw):
    k, m = dw.shape
    return dw.reshape(k, N_DEV, m // N_DEV).transpose(1, 0, 2)


def kernel(x, positions, pre_norm_w, post_norm_w, attn_w_in, attn_b_in, attn_sinks, attn_w_out, attn_b_out, rec_w_in, rec_lb_logits, rec_gnorm_w, rec_w_out, loss_target, m_pre_norm_w, m_post_norm_w, m_attn_w_in, m_attn_b_in, m_attn_sinks, m_attn_w_out, m_attn_b_out, m_rec_w_in, m_rec_lb_logits, m_rec_gnorm_w, m_rec_w_out, v_pre_norm_w, v_post_norm_w, v_attn_w_in, v_attn_b_in, v_attn_sinks, v_attn_w_out, v_attn_b_out, v_rec_w_in, v_rec_lb_logits, v_rec_gnorm_w, v_rec_w_out):
    b_loc, t_len, _ = x.shape
    n = b_loc * t_len
    me = _my_id()
    ga_in, ga_out, gr_in, gr_out = _all_gather([attn_w_in[0].astype(BF16), attn_w_out[0].astype(BF16),
                                                rec_w_in[0].astype(BF16), rec_w_out[0].astype(BF16)])
    wa_in = _qzkv(ga_in.transpose(1, 0, 2).reshape(D_MODEL, ATTN_IN))
    wr_in = gr_in.transpose(1, 0, 2).reshape(D_MODEL, REC_IN)
    wa_out = ga_out.reshape(ATTN_WIDTH, D_MODEL)
    wr_out = gr_out.reshape(1024, D_MODEL)

    loss_tile, dx, d_wa_in, d_wa_out, d_wr_in, d_wr_out, small = _local_step(
        x.reshape(n, D_MODEL), positions.reshape(n, 1).astype(F32), loss_target.reshape(n, D_MODEL),
        pre_norm_w, post_norm_w, wa_in, _qzkv(attn_b_in), attn_sinks, wa_out, attn_b_out, wr_in, rec_lb_logits,
        rec_gnorm_w, wr_out, b_loc, t_len)
    loss = lax.psum(loss_tile[0, 0], ("x", "y", "c"))

    def finish(part, w, m, v, name):
        own = lax.dynamic_index_in_dim(part, me, 0, keepdims=False)
        g, d, m2, v2 = _reduce_scatter_adamw(part.astype(BF16), own, w[0], m[0], v[0], name)
        return g[None], d[None], m2[None], v2[None]

    r_a_in = finish(_by_owner_cols(_qkvz(d_wa_in)), attn_w_in, m_attn_w_in, v_attn_w_in, "comm_rs_attn_w_in")
    r_a_out = finish(d_wa_out.reshape(N_DEV, ATTN_WIDTH // N_DEV, D_MODEL), attn_w_out, m_attn_w_out, v_attn_w_out, "comm_rs_attn_w_out")
    r_r_in = finish(_by_owner_cols(jnp.concatenate(d_wr_in, axis=1)), rec_w_in, m_rec_w_in, v_rec_w_in, "comm_rs_rec_w_in")
    r_r_out = finish(d_wr_out.reshape(N_DEV, 1024 // N_DEV, D_MODEL), rec_w_out, m_rec_w_out, v_rec_w_out, "comm_rs_rec_w_out")

    gsmall = dict(pre_norm_w=small["pre"], post_norm_w=small["post"], attn_b_in=_qkvz(small["ba_in"]), attn_sinks=small["sinks"],
                  attn_b_out=small["ba_out"], rec_lb_logits=jnp.concatenate([small["lb"], jnp.zeros_like(small["lb"])], axis=0),
                  rec_gnorm_w=small["gnorm"])
    wsmall = dict(pre_norm_w=pre_norm_w, post_norm_w=post_norm_w, attn_b_in=attn_b_in, attn_sinks=attn_sinks,
                  attn_b_out=attn_b_out, rec_lb_logits=rec_lb_logits, rec_gnorm_w=rec_gnorm_w)
    msmall = dict(pre_norm_w=m_pre_norm_w, post_norm_w=m_post_norm_w, attn_b_in=m_attn_b_in, attn_sinks=m_attn_sinks,
                  attn_b_out=m_attn_b_out, rec_lb_logits=m_rec_lb_logits, rec_gnorm_w=m_rec_gnorm_w)
    vsmall = dict(pre_norm_w=v_pre_norm_w, post_norm_w=v_post_norm_w, attn_b_in=v_attn_b_in, attn_sinks=v_attn_sinks,
                  attn_b_out=v_attn_b_out, rec_lb_logits=v_rec_lb_logits, rec_gnorm_w=v_rec_gnorm_w)
    shapes = {k: a.shape for k, a in wsmall.items()}
    packed = _small_allreduce_adamw(_pack_small(gsmall), _pack_small(wsmall), _pack_small(msmall), _pack_small(vsmall))
    sg, sd, sm, sv = [_unpack_small(a, shapes) for a in packed]

    big = {"attn_w_in": r_a_in, "attn_w_out": r_a_out, "rec_w_in": r_r_in, "rec_w_out": r_r_out}
    order = ["pre_norm_w", "post_norm_w", "attn_w_in", "attn_b_in", "attn_sinks", "attn_w_out", "attn_b_out", "rec_w_in",
             "rec_lb_logits", "rec_gnorm_w", "rec_w_out"]
    outs = [loss, dx.reshape(b_loc, t_len, D_MODEL)]
    for idx, small_set in enumerate((sg, sd, sm, sv)):
        outs += [big[nm][idx] if nm in big else small_set[nm] for nm in order]
    return tuple(outs)
```

```python
import numpy as np
import jax
import jax.numpy as jnp
from jax import lax
from jax.experimental import pallas as pl
from jax.experimental.pallas import tpu as pltpu

F32, BF16 = jnp.float32, jnp.bfloat16
MESH = pl.DeviceIdType.MESH
N_DEV = 8

D_MODEL = 1024
N_HEADS, HEAD_DIM, N_KV, GROUP = 16, 64, 2, 8
ATTN_WIDTH, KV_WIDTH = 1024, 128
ATTN_IN = 2 * ATTN_WIDTH + 2 * KV_WIDTH
BLK = 128
ROPE_THETA, ROPE_HALF = 500000.0, 8
REC_HEADS, REC_K = 8, 128
REC_IN = 4 * 1024
CH = 32
NORM_EPS = 1e-6
ADAM_LR, ADAM_B1, ADAM_B2, ADAM_EPS, ADAM_WD, ADAM_STEP = 0.001, 0.9, 0.999, 1e-08, 0.01, 10

LANES, SUBLANES = 128, 8
TM = 512
NT_DIMS = (((1,), (1,)), ((), ()))
TN_DIMS = (((0,), (0,)), ((), ()))
MB = 2 ** 20


def _params(sem=None, vmem_mb=48, **kw):
    return pltpu.CompilerParams(dimension_semantics=sem, vmem_limit_bytes=vmem_mb * MB, **kw)


def _col_chunk(m):
    return 768 if m % 1024 else 1024


def _sigmoid(x):
    return 1.0 / (1.0 + jnp.exp(-x))


def _split3(x):
    hi = x.astype(BF16)
    r1 = x - hi.astype(F32)
    mid = r1.astype(BF16)
    lo = (r1 - mid.astype(F32)).astype(BF16)
    return hi, mid, lo


def _dot3(l_bf, x):
    hi, mid, lo = _split3(x)
    return (jnp.dot(l_bf, hi, preferred_element_type=F32) + jnp.dot(l_bf, mid, preferred_element_type=F32)
            + jnp.dot(l_bf, lo, preferred_element_type=F32))


def _rope_tables(pos_col):
    n = pos_col.shape[0]
    lane = np.arange(LANES) % HEAD_DIM
    inv = np.float32(ROPE_THETA) ** (-(np.arange(ROPE_HALF, dtype=np.float32) * np.float32(2.0) / np.float32(2 * ROPE_HALF)))
    freq = np.where(lane < 2 * ROPE_HALF, inv[lane % ROPE_HALF], 0.0).astype(np.float32)[None, :]
    sign = np.where(lane < ROPE_HALF, -1.0, np.where(lane < 2 * ROPE_HALF, 1.0, 0.0)).astype(np.float32)[None, :]

    def body(p_ref, f_ref, s_ref, c_out, s_out):
        ang = p_ref[...] * f_ref[...]
        c_out[...] = jnp.cos(ang)
        s_out[...] = jnp.sin(ang) * s_ref[...]

    row = pl.BlockSpec((TM, 1), lambda i: (i, 0))
    vec = pl.BlockSpec((1, LANES), lambda i: (0, 0))
    out = pl.BlockSpec((TM, LANES), lambda i: (i, 0))
    return pl.pallas_call(
        body, name="rope_tables", grid=(n // TM,), in_specs=[row, vec, vec], out_specs=[out, out],
        out_shape=[jax.ShapeDtypeStruct((n, LANES), F32)] * 2, compiler_params=_params(("arbitrary",)),
    )(pos_col, jnp.asarray(freq), jnp.asarray(sign))


def _rope_apply(xv, c, s, lm):
    partner = jnp.where(lm < ROPE_HALF, pltpu.roll(xv, LANES - ROPE_HALF, 1), pltpu.roll(xv, ROPE_HALF, 1))
    return xv * c + partner * s


def _rope_bwd(dy, c, s, lm):
    t = dy * s
    partner = jnp.where(lm < ROPE_HALF, pltpu.roll(t, LANES - ROPE_HALF, 1),
                        jnp.where(lm < 2 * ROPE_HALF, pltpu.roll(t, ROPE_HALF, 1), 0.0))
    return dy * c + partner


def _lower_bound(lb_logits):
    def body(l_ref, o_ref):
        l0, l1 = l_ref[0:1, :], l_ref[1:2, :]
        m = jnp.maximum(l0, l1)
        e0, e1 = jnp.exp(l0 - m), jnp.exp(l1 - m)
        o_ref[...] = e1 / (e0 + e1)

    return pl.pallas_call(body, name="lower_bound", out_shape=jax.ShapeDtypeStruct((1, lb_logits.shape[1]), F32))(lb_logits)


def _norm_matmul(x, pw, w, bias, name):
    n, m = x.shape[0], w.shape[1]
    cn = _col_chunk(m)
    has_bias = bias is not None

    def body(*refs):
        if has_bias:
            x_ref, pw_ref, w_ref, b_ref, p_ref, h_ref = refs
        else:
            x_ref, pw_ref, w_ref, p_ref, h_ref = refs
        xv = x_ref[...]
        r = lax.rsqrt(jnp.mean(xv * xv, axis=-1, keepdims=True) + NORM_EPS)
        h = ((xv * r) * pw_ref[...]).astype(BF16)
        h_ref[...] = h
        for j in range(0, m, cn):
            acc = jnp.dot(h, w_ref[:, j:j + cn], preferred_element_type=F32)
            if has_bias:
                acc = acc + b_ref[:, j:j + cn]
            p_ref[:, j:j + cn] = acc

    rows = pl.BlockSpec((TM, D_MODEL), lambda i: (i, 0))
    const = lambda shape: pl.BlockSpec(shape, lambda i: (0, 0))
    in_specs = [rows, const((1, D_MODEL)), const((D_MODEL, m))] + ([const((1, m))] if has_bias else [])
    args = (x, pw, w) + ((bias,) if has_bias else ())
    return pl.pallas_call(
        body, name=name, grid=(n // TM,), in_specs=in_specs,
        out_specs=[pl.BlockSpec((TM, m), lambda i: (i, 0)), rows],
        out_shape=[jax.ShapeDtypeStruct((n, m), F32), jax.ShapeDtypeStruct((n, D_MODEL), BF16)],
        compiler_params=_params(("arbitrary",), 56),
    )(*args)


def _outproj_postnorm(g, w, bias, xres, qw, tgt, name):
    n = g.shape[0]
    has_bias, has_loss = bias is not None, tgt is not None
    steps = n // TM

    def body(*refs):
        refs = list(refs)
        g_ref, w_ref = refs.pop(0), refs.pop(0)
        b_ref = refs.pop(0) if has_bias else None
        x_ref, qw_ref = refs.pop(0), refs.pop(0)
        t_ref = refs.pop(0) if has_loss else None
        y_ref, o_ref = refs.pop(0), refs.pop(0)
        y = jnp.dot(g_ref[...], w_ref[...], preferred_element_type=F32)
        if has_bias:
            y = y + b_ref[...]
        y_ref[...] = y
        r = lax.rsqrt(jnp.mean(y * y, axis=-1, keepdims=True) + NORM_EPS)
        xn = x_ref[...] + (y * r) * qw_ref[...]
        if not has_loss:
            o_ref[...] = xn
        else:
            loss_ref, acc_ref = refs
            i = pl.program_id(0)
            e = xn - t_ref[...]
            o_ref[...] = e * (1.0 / D_MODEL)

            @pl.when(i == 0)
            def _():
                acc_ref[...] = jnp.zeros_like(acc_ref)

            acc_ref[...] += jnp.sum(e * e, axis=0, keepdims=True)

            @pl.when(i == steps - 1)
            def _():
                loss_ref[...] = jnp.full(loss_ref.shape, jnp.sum(acc_ref[...]) * (0.5 / D_MODEL), F32)

    rows = pl.BlockSpec((TM, D_MODEL), lambda i: (i, 0))
    const = lambda shape: pl.BlockSpec(shape, lambda i: (0, 0))
    in_specs = [rows, const((D_MODEL, D_MODEL))] + ([const((1, D_MODEL))] if has_bias else []) + [rows, const((1, D_MODEL))]
    args = [g, w] + ([bias] if has_bias else []) + [xres, qw]
    out_specs = [rows, rows]
    out_shape = [jax.ShapeDtypeStruct((n, D_MODEL), F32)] * 2
    scratch = []
    if has_loss:
        in_specs.append(rows)
        args.append(tgt)
        out_specs.append(const((SUBLANES, LANES)))
        out_shape.append(jax.ShapeDtypeStruct((SUBLANES, LANES), F32))
        scratch = [pltpu.VMEM((1, D_MODEL), F32)]
    return pl.pallas_call(
        body, name=name, grid=(steps,), in_specs=in_specs, out_specs=out_specs, out_shape=out_shape,
        scratch_shapes=scratch, compiler_params=_params(("arbitrary",), 48),
    )(*args)


_QCOL, _ZCOL, _KCOL, _VCOL = 0, 1024, 2048, 2176


def _attn_probs(qr, kcat, h, sink_ref, valid):
    qst = jnp.concatenate(
        [qr[4 * h + g // 2][:, (g % 2) * HEAD_DIM:(g % 2 + 1) * HEAD_DIM] for g in range(GROUP)], axis=0).astype(BF16)
    kh = kcat[:, h * HEAD_DIM:(h + 1) * HEAD_DIM].astype(BF16)
    s = lax.dot_general(qst, kh, NT_DIMS, preferred_element_type=F32)
    s = jnp.where(valid, s, -1e30)
    sk = jnp.concatenate([jnp.full((BLK, 1), sink_ref[0, h * GROUP + g], F32) for g in range(GROUP)], axis=0)
    m = jnp.maximum(jnp.max(s, axis=-1, keepdims=True), sk)
    p = jnp.exp(s - m)
    esk = jnp.exp(sk - m)
    inv = 1.0 / (jnp.sum(p, axis=-1, keepdims=True) + esk)
    return qst, kh, p * inv, esk * inv


def _attn_mask(i):
    row = lax.broadcasted_iota(jnp.int32, (GROUP * BLK, 2 * BLK), 0) & (BLK - 1)
    col = lax.broadcasted_iota(jnp.int32, (GROUP * BLK, 2 * BLK), 1)
    return (col > row) & (col <= row + BLK) & ((col >= BLK) | (i > 0))


def _unstack_heads(xh, c2):
    return jnp.concatenate([xh[(2 * c2) * BLK:(2 * c2 + 1) * BLK], xh[(2 * c2 + 1) * BLK:(2 * c2 + 2) * BLK]], axis=1)


def _stack_heads(chunks, h):
    return jnp.concatenate(
        [chunks[4 * h + g // 2][:, (g % 2) * HEAD_DIM:(g % 2 + 1) * HEAD_DIM] for g in range(GROUP)], axis=0)


def _attn_fwd(p, ct, st, sinks, b_loc, nb, shards):
    n = p.shape[0]
    nsh = len(shards)

    def body(sink_ref, q_ref, z_ref, kc_ref, kp_ref, vc_ref, vp_ref, cc_ref, sc_ref, cp_ref, sp_ref, *rest):
        sh_in, (o_ref, g_ref), sh_out, sems = rest[:nsh], rest[nsh:nsh + 2], rest[nsh + 2:2 * nsh + 2], rest[2 * nsh + 2:]
        b, i = pl.program_id(0), pl.program_id(1)

        @pl.when((b == 0) & (i == 0))
        def _():
            _gather_start(sh_in, sh_out, sems)

        lane = lax.broadcasted_iota(jnp.int32, (BLK, LANES), 1)
        lm = lane & (HEAD_DIM - 1)
        cc, sc = cc_ref[...], sc_ref[...]
        kcat = jnp.concatenate([_rope_apply(kp_ref[...], cp_ref[...], sp_ref[...], lm),
                                _rope_apply(kc_ref[...], cc, sc, lm)], axis=0)
        vcat = jnp.concatenate([vp_ref[...], vc_ref[...]], axis=0)
        qr = [_rope_apply(q_ref[:, c * LANES:(c + 1) * LANES], cc, sc, lm) * (HEAD_DIM ** -0.5) for c in range(8)]
        valid = _attn_mask(i)
        for h in range(N_KV):
            _, _, pn, _ = _attn_probs(qr, kcat, h, sink_ref, valid)
            vh = vcat[:, h * HEAD_DIM:(h + 1) * HEAD_DIM].astype(BF16)
            oh = jnp.dot(pn.astype(BF16), vh, preferred_element_type=F32)
            for c2 in range(4):
                oc = _unstack_heads(oh, c2)
                cols = slice((4 * h + c2) * LANES, (4 * h + c2 + 1) * LANES)
                zc = z_ref[:, cols]
                o_ref[:, cols] = oc
                g_ref[:, cols] = (oc * (zc * _sigmoid(zc))).astype(BF16)

        @pl.when((b == b_loc - 1) & (i == nb - 1))
        def _():
            _gather_wait(sh_in, sh_out, sems)

    cur = lambda b, i: b * nb + i
    prev = lambda b, i: b * nb + jnp.maximum(i - 1, 0)
    wide = lambda cb: pl.BlockSpec((BLK, ATTN_WIDTH), lambda b, i: (cur(b, i), cb))
    kv = lambda rowf, cb: pl.BlockSpec((BLK, LANES), lambda b, i: (rowf(b, i), cb))
    hbm = pl.BlockSpec(memory_space=pl.ANY)
    in_specs = [pl.BlockSpec(memory_space=pltpu.SMEM), wide(0), wide(1),
                kv(cur, _KCOL // LANES), kv(prev, _KCOL // LANES), kv(cur, _VCOL // LANES), kv(prev, _VCOL // LANES),
                kv(cur, 0), kv(cur, 0), kv(prev, 0), kv(prev, 0)] + [hbm] * nsh
    return pl.pallas_call(
        body, name="attn_fwd", grid=(b_loc, nb), in_specs=in_specs, out_specs=[wide(0), wide(0)] + [hbm] * nsh,
        out_shape=[jax.ShapeDtypeStruct((n, ATTN_WIDTH), F32), jax.ShapeDtypeStruct((n, ATTN_WIDTH), BF16)] + _gather_shapes(shards),
        scratch_shapes=_gather_sems(nsh), compiler_params=_params(("arbitrary", "arbitrary"), 48),
    )(sinks, p, p, p, p, p, p, ct, st, ct, st, *shards)


def _attn_bwd(p, ct, st, sinks, o, dg, b_loc, nb, parts):
    n = p.shape[0]
    nparts = len(parts)

    def body(sink_ref, q_ref, z_ref, kc_ref, kp_ref, vc_ref, vp_ref, cc_ref, sc_ref, cp_ref, sp_ref, o_ref, dg_ref, *rest):
        part_refs, (dp_ref, ds_ref), land_refs = rest[:nparts], rest[nparts:nparts + 2], rest[nparts + 2:2 * nparts + 2]
        dq_s, dz_s, dk_s, dv_s = rest[2 * nparts + 2:2 * nparts + 6]
        sems = rest[2 * nparts + 6:]
        b, i = pl.program_id(0), pl.program_id(1)

        @pl.when((b == 0) & (i == 0))
        def _():
            _scatter_start(part_refs, land_refs, sems)

        @pl.when((b == b_loc - 1) & (i == nb))
        def _():
            _scatter_wait(part_refs, land_refs, sems)

        lane = lax.broadcasted_iota(jnp.int32, (BLK, LANES), 1)
        lm = lane & (HEAD_DIM - 1)

        @pl.when((b == 0) & (i == 0))
        def _():
            ds_ref[...] = jnp.zeros_like(ds_ref)

        @pl.when(i < nb)
        def _compute():
            cc, sc = cc_ref[...], sc_ref[...]
            kcat = jnp.concatenate([_rope_apply(kp_ref[...], cp_ref[...], sp_ref[...], lm),
                                    _rope_apply(kc_ref[...], cc, sc, lm)], axis=0)
            vcat = jnp.concatenate([vp_ref[...], vc_ref[...]], axis=0)
            qr = [_rope_apply(q_ref[:, c * LANES:(c + 1) * LANES], cc, sc, lm) * (HEAD_DIM ** -0.5) for c in range(8)]
            valid = _attn_mask(i)
            do_chunks, o_chunks, dz_chunks = [], [], []
            for c in range(8):
                cols = slice(c * LANES, (c + 1) * LANES)
                zc, oc, dgc = z_ref[:, cols], o_ref[:, cols], dg_ref[:, cols]
                sg = _sigmoid(zc)
                do_chunks.append(dgc * (zc * sg))
                dz_chunks.append(dgc * oc * (sg * (1.0 + zc * (1.0 - sg))))
                o_chunks.append(oc)
            dq_chunks = [None] * 8
            dk_h, dv_h = [], []
            ds_acc = jnp.zeros((SUBLANES, LANES), F32)
            tile_lane = lax.broadcasted_iota(jnp.int32, (SUBLANES, LANES), 1)
            tile_row = lax.broadcasted_iota(jnp.int32, (SUBLANES, LANES), 0)
            for h in range(N_KV):
                qst, kh, pn, psink = _attn_probs(qr, kcat, h, sink_ref, valid)
                vh = vcat[:, h * HEAD_DIM:(h + 1) * HEAD_DIM].astype(BF16)
                do_st = _stack_heads(do_chunks, h)
                delta = jnp.sum(do_st * _stack_heads(o_chunks, h), axis=-1, keepdims=True)
                do_bf = do_st.astype(BF16)
                dpm = lax.dot_general(do_bf, vh, NT_DIMS, preferred_element_type=F32)
                dsm = (pn * (dpm - delta)).astype(BF16)
                sink_term = psink * delta
                for g in range(GROUP):
                    val = -jnp.sum(sink_term[g * BLK:(g + 1) * BLK])
                    ds_acc = ds_acc + jnp.where((tile_lane == h * GROUP + g) & (tile_row == 0), val, 0.0)
                dq_st = jnp.dot(dsm, kh, preferred_element_type=F32) * (HEAD_DIM ** -0.5)
                dk_h.append(lax.dot_general(dsm, qst, TN_DIMS, preferred_element_type=F32))
                dv_h.append(lax.dot_general(pn.astype(BF16), do_bf, TN_DIMS, preferred_element_type=F32))
                for c2 in range(4):
                    dq_chunks[4 * h + c2] = _rope_bwd(_unstack_heads(dq_st, c2), cc, sc, lm)
            ds_ref[...] += ds_acc
            dk_full = jnp.concatenate(dk_h, axis=1)
            dv_full = jnp.concatenate(dv_h, axis=1)

            @pl.when(i >= 1)
            def _emit():
                dp_ref[:, _QCOL:_QCOL + ATTN_WIDTH] = dq_s[...]
                dp_ref[:, _ZCOL:_ZCOL + ATTN_WIDTH] = dz_s[...]
                dp_ref[:, _KCOL:_KCOL + KV_WIDTH] = _rope_bwd(dk_s[...] + dk_full[:BLK], cp_ref[...], sp_ref[...], lm)
                dp_ref[:, _VCOL:_VCOL + KV_WIDTH] = dv_s[...] + dv_full[:BLK]

            for c in range(8):
                dq_s[:, c * LANES:(c + 1) * LANES] = dq_chunks[c]
                dz_s[:, c * LANES:(c + 1) * LANES] = dz_chunks[c]
            dk_s[...] = dk_full[BLK:]
            dv_s[...] = dv_full[BLK:]

        @pl.when(i == nb)
        def _final():
            dp_ref[:, _QCOL:_QCOL + ATTN_WIDTH] = dq_s[...]
            dp_ref[:, _ZCOL:_ZCOL + ATTN_WIDTH] = dz_s[...]
            dp_ref[:, _KCOL:_KCOL + KV_WIDTH] = _rope_bwd(dk_s[...], cc_ref[...], sc_ref[...], lm)
            dp_ref[:, _VCOL:_VCOL + KV_WIDTH] = dv_s[...]

    cur = lambda b, i: b * nb + jnp.minimum(i, nb - 1)
    prev = lambda b, i: b * nb + jnp.maximum(jnp.minimum(i, nb - 1) - 1, 0)
    emit = lambda b, i: b * nb + jnp.maximum(i - 1, 0)
    hbm = pl.BlockSpec(memory_space=pl.ANY)
    wide = lambda cb: pl.BlockSpec((BLK, ATTN_WIDTH), lambda b, i: (cur(b, i), cb))
    kv = lambda rowf, cb: pl.BlockSpec((BLK, LANES), lambda b, i: (rowf(b, i), cb))
    in_specs = [pl.BlockSpec(memory_space=pltpu.SMEM), wide(0), wide(1),
                kv(cur, _KCOL // LANES), kv(prev, _KCOL // LANES), kv(cur, _VCOL // LANES), kv(prev, _VCOL // LANES),
                kv(cur, 0), kv(cur, 0), kv(prev, 0), kv(prev, 0), wide(0), wide(0)] + [hbm] * nparts
    out_specs = [pl.BlockSpec((BLK, ATTN_IN), lambda b, i: (emit(b, i), 0)),
                 pl.BlockSpec((SUBLANES, LANES), lambda b, i: (0, 0))] + [hbm] * nparts
    return pl.pallas_call(
        body, name="attn_bwd", grid=(b_loc, nb + 1), in_specs=in_specs, out_specs=out_specs,
        out_shape=[jax.ShapeDtypeStruct((n, ATTN_IN), F32), jax.ShapeDtypeStruct((SUBLANES, LANES), F32)] + _scatter_lands(parts),
        scratch_shapes=[pltpu.VMEM((BLK, ATTN_WIDTH), F32), pltpu.VMEM((BLK, ATTN_WIDTH), F32),
                        pltpu.VMEM((BLK, KV_WIDTH), F32), pltpu.VMEM((BLK, KV_WIDTH), F32)] + _scatter_sems(nparts),
        compiler_params=_params(("arbitrary", "arbitrary"), 48),
    )(sinks, p, p, p, p, p, p, ct, st, ct, st, o, dg, *parts)


_CUM_ROWS = 256
_ROWS_OF = [CH - SUBLANES * (s // SUBLANES) for s in range(CH)]
_OFF_OF = [sum(_ROWS_OF[:s]) for s in range(CH)]
_PAIR_ROWS = sum(_ROWS_OF)


def _tri(lower):
    r = lax.broadcasted_iota(jnp.int32, (_CUM_ROWS, _CUM_ROWS), 0)
    c = lax.broadcasted_iota(jnp.int32, (_CUM_ROWS, _CUM_ROWS), 1)
    same = (r ^ c) < CH
    return (same & ((c <= r) if lower else (c >= r))).astype(BF16)


def _gates(qp, fp, lb):
    e = jnp.exp(-jnp.abs(fp))
    log_sig = jnp.minimum(fp, 0.0) - jnp.log(1.0 + e)
    a = jnp.log(lb)
    c = jnp.log(1.0 - lb) + log_sig
    g = jnp.maximum(a, c) + jnp.log(1.0 + jnp.exp(-jnp.abs(a - c)))
    sig_neg = jnp.where(fp >= 0, e, 1.0) / (1.0 + e)
    return qp * _sigmoid(qp), g, (1.0 - lb) * sig_neg, sig_neg


def _pair_rows(bc, s):
    r0 = SUBLANES * (s // SUBLANES)
    return jnp.exp(jnp.minimum(bc[r0:, :] - bc[s:s + 1, :], 0.0))


HP = 4
_HW = HP * REC_K


def _hgrn_specs(tb, nt, reverse):
    tmap = (lambda t: nt - 1 - t) if reverse else (lambda t: t)
    groups = REC_HEADS // HP
    blk = lambda cb: pl.BlockSpec((tb, _HW), lambda h, b, t: (b * nt + tmap(t), cb * groups + h))
    head = pl.BlockSpec((tb, _HW), lambda h, b, t: (b * nt + tmap(t), h))
    lbs = pl.BlockSpec((1, _HW), lambda h, b, t: (0, h))
    gws = pl.BlockSpec((1, REC_K), lambda h, b, t: (0, 0))
    hist = pl.BlockSpec((HP, 1, tb // CH, REC_K, REC_K), lambda h, b, t: (h, b, tmap(t), 0, 0))
    return blk, head, lbs, gws, hist


def _cumsum_chunks(tri, x, out_ref, tb):
    for r in range(0, tb, _CUM_ROWS):
        out_ref[r:r + _CUM_ROWS, :] = _dot3(tri, x[r:r + _CUM_ROWS, :])


def _hgrn_fwd(p, lb, gw, b_loc, t_len):
    n = p.shape[0]
    tb = min(TM, t_len)
    nt, nck = t_len // tb, tb // CH

    def body(qp_ref, fp_ref, i_ref, z_ref, lb_ref, gw_ref, oraw_ref, g_ref, sh_ref, q_s, k_s, b_s, o_s, st_ref):
        @pl.when(pl.program_id(2) == 0)
        def _():
            st_ref[...] = jnp.zeros_like(st_ref)

        qv, g, kk, _ = _gates(qp_ref[...], fp_ref[...], lb_ref[...])
        q_s[...] = qv
        k_s[...] = kk
        _cumsum_chunks(_tri(True), g, b_s, tb)
        ones = jnp.ones((REC_K, REC_K), BF16)
        sub = lax.broadcasted_iota(jnp.int32, (SUBLANES, REC_K), 0)

        def chunk(c, carry):
            for hp in range(HP):
                head_chunk(c, hp)
            return carry

        def head_chunk(c, hp):
            rs = pl.ds(pl.multiple_of(c * CH, CH), CH)
            cs = slice(hp * REC_K, (hp + 1) * REC_K)
            q, k, bc, v = q_s[rs, cs], k_s[rs, cs], b_s[rs, cs], i_ref[rs, cs]
            bl = bc[CH - 1:CH, :]
            st = st_ref[hp]
            sh_ref[hp, 0, c] = st
            o = lax.dot_general((q * jnp.exp(bc)).astype(BF16), st.astype(BF16), NT_DIMS, preferred_element_type=F32)
            w = jnp.concatenate([q[SUBLANES * (s // SUBLANES):, :] * _pair_rows(bc, s) * k[s:s + 1, :] for s in range(CH)], axis=0)
            a = jnp.dot(w.astype(BF16), ones, preferred_element_type=F32)
            acc = [jnp.zeros((SUBLANES, REC_K), F32) for _ in range(CH // SUBLANES)]
            for s in range(CH):
                j = s // SUBLANES
                vs = v[s:s + 1, :]
                for jj in range(j, CH // SUBLANES):
                    blk = a[_OFF_OF[s] + (jj - j) * SUBLANES:_OFF_OF[s] + (jj - j + 1) * SUBLANES, :]
                    if jj == j:
                        blk = jnp.where(sub >= s - j * SUBLANES, blk, 0.0)
                    acc[jj] = acc[jj] + blk * vs
            o_s[rs, cs] = o + jnp.concatenate(acc, axis=0)
            kd = k * jnp.exp(bl - bc)
            st_ref[hp] = st * jnp.exp(bl) + lax.dot_general(v.astype(BF16), kd.astype(BF16), TN_DIMS, preferred_element_type=F32)

        lax.fori_loop(0, nck, chunk, 0)
        oraw_ref[...] = o_s[...]
        for hp in range(HP):
            cs = slice(hp * REC_K, (hp + 1) * REC_K)
            o, zc = o_s[:, cs], z_ref[:, cs]
            on = (o * lax.rsqrt(jnp.mean(o * o, axis=-1, keepdims=True) + NORM_EPS)) * gw_ref[...]
            g_ref[:, cs] = (on * (zc * _sigmoid(zc))).astype(BF16)

    blk, head, lbs, gws, hist = _hgrn_specs(tb, nt, False)
    return pl.pallas_call(
        body, name="hgrn_fwd", grid=(REC_HEADS // HP, b_loc, nt),
        in_specs=[blk(0), blk(1), blk(2), blk(3), lbs, gws], out_specs=[head, head, hist],
        out_shape=[jax.ShapeDtypeStruct((n, 1024), F32), jax.ShapeDtypeStruct((n, 1024), BF16),
                   jax.ShapeDtypeStruct((REC_HEADS, b_loc, t_len // CH, REC_K, REC_K), F32)],
        scratch_shapes=[pltpu.VMEM((tb, _HW), F32)] * 4 + [pltpu.VMEM((HP, REC_K, REC_K), F32)],
        compiler_params=_params(("arbitrary", "arbitrary", "arbitrary"), 48),
    )(p, p, p, p, lb, gw)


def _hgrn_bwd(p, lb, gw, oraw, sh, dg, b_loc, t_len):
    n = p.shape[0]
    tb = min(TM, t_len)
    nt, nck = t_len // tb, tb // CH

    def body(qp_ref, fp_ref, i_ref, z_ref, lb_ref, gw_ref, oraw_ref, dg_ref, sh_ref,
             dq_ref, df_ref, di_ref, dz_ref, dlb_ref, dgw_ref,
             q_s, k_s, b_s, do_s, dqv_s, dk_s, db_s, rowk_s, rowv_s, dst_ref):
        b, t = pl.program_id(1), pl.program_id(2)

        @pl.when(t == 0)
        def _():
            dst_ref[...] = jnp.zeros_like(dst_ref)

        @pl.when((b == 0) & (t == 0))
        def _():
            dlb_ref[...] = jnp.zeros_like(dlb_ref)
            dgw_ref[...] = jnp.zeros_like(dgw_ref)

        lbv, qp, fp = lb_ref[...], qp_ref[...], fp_ref[...]
        qv, g, kk, sig_neg = _gates(qp, fp, lbv)
        q_s[...] = qv
        k_s[...] = kk
        _cumsum_chunks(_tri(True), g, b_s, tb)
        gwv = gw_ref[...]
        for hp in range(HP):
            cs = slice(hp * REC_K, (hp + 1) * REC_K)
            o, zc, dgv = oraw_ref[:, cs], z_ref[:, cs], dg_ref[:, cs]
            rn = lax.rsqrt(jnp.mean(o * o, axis=-1, keepdims=True) + NORM_EPS)
            on = o * rn
            sgz = _sigmoid(zc)
            dz_ref[:, cs] = dgv * (on * gwv) * (sgz * (1.0 + zc * (1.0 - sgz)))
            dpre = dgv * (zc * sgz)
            dgw_ref[hp] += jnp.sum(dpre * on, axis=0, keepdims=True)
            don = dpre * gwv
            do_s[:, cs] = rn * (don - on * jnp.mean(don * on, axis=-1, keepdims=True))

        ones = jnp.ones((REC_K, REC_K), BF16)
        sub = lax.broadcasted_iota(jnp.int32, (SUBLANES, REC_K), 0)
        rowid = lax.broadcasted_iota(jnp.int32, (CH, REC_K), 0)
        ngrp = CH // SUBLANES

        def chunk(ci, carry):
            for hp in range(HP):
                head_chunk(nck - 1 - ci, hp)
            return carry

        def head_chunk(c, hp):
            rs = pl.ds(pl.multiple_of(c * CH, CH), CH)
            cs = slice(hp * REC_K, (hp + 1) * REC_K)
            rowk, rowv = rowk_s.at[hp], rowv_s.at[hp]
            q, k, bc, v, do = q_s[rs, cs], k_s[rs, cs], b_s[rs, cs], i_ref[rs, cs], do_s[rs, cs]
            bl = bc[CH - 1:CH, :]
            st, dst = sh_ref[hp, 0, c], dst_ref[hp]
            eb, ebl, ekd = jnp.exp(bc), jnp.exp(bl), jnp.exp(bl - bc)
            qe, kd = q * eb, k * ekd
            do_bf, dst_bf = do.astype(BF16), dst.astype(BF16)
            dqe = jnp.dot(do_bf, st.astype(BF16), preferred_element_type=F32)
            dkd = jnp.dot(v.astype(BF16), dst_bf, preferred_element_type=F32)
            dv = lax.dot_general(kd.astype(BF16), dst_bf, NT_DIMS, preferred_element_type=F32)
            dbl = jnp.sum(dkd * kd, axis=0, keepdims=True) + ebl * jnp.sum(st * dst, axis=0, keepdims=True)
            dst_ref[hp] = dst * ebl + lax.dot_general(do_bf, qe.astype(BF16), TN_DIMS, preferred_element_type=F32)
            dec = [_pair_rows(bc, s) for s in range(CH)]
            w = jnp.concatenate([q[SUBLANES * (s // SUBLANES):, :] * dec[s] * k[s:s + 1, :] for s in range(CH)], axis=0)
            x = jnp.concatenate([do[SUBLANES * (s // SUBLANES):, :] * v[s:s + 1, :] for s in range(CH)], axis=0)
            a = jnp.dot(w.astype(BF16), ones, preferred_element_type=F32)
            da = jnp.dot(x.astype(BF16), ones, preferred_element_type=F32)
            dq_acc = [jnp.zeros((SUBLANES, REC_K), F32) for _ in range(ngrp)]
            for s in range(CH):
                j = s // SUBLANES
                r0 = j * SUBLANES
                ks = k[s:s + 1, :]
                uk = jnp.zeros((SUBLANES, REC_K), F32)
                uv = jnp.zeros((SUBLANES, REC_K), F32)
                for jj in range(j, ngrp):
                    lo, hi = _OFF_OF[s] + (jj - j) * SUBLANES, _OFF_OF[s] + (jj - j + 1) * SUBLANES
                    a_blk, da_blk = a[lo:hi, :], da[lo:hi, :]
                    if jj == j:
                        keep = sub >= s - r0
                        a_blk, da_blk = jnp.where(keep, a_blk, 0.0), jnp.where(keep, da_blk, 0.0)
                    rows = slice(jj * SUBLANES, (jj + 1) * SUBLANES)
                    tt = da_blk * dec[s][(jj - j) * SUBLANES:(jj - j + 1) * SUBLANES, :]
                    dq_acc[jj] = dq_acc[jj] + tt * ks
                    uk = uk + tt * q[rows, :]
                    uv = uv + a_blk * do[rows, :]
                rowk[s:s + 1, :] = jnp.sum(uk, axis=0, keepdims=True)
                rowv[s:s + 1, :] = jnp.sum(uv, axis=0, keepdims=True)
            dq_in = jnp.concatenate(dq_acc, axis=0)
            dk_in = rowk[...]
            dqv_s[rs, cs] = dqe * eb + dq_in
            dk_s[rs, cs] = dkd * ekd + dk_in
            di_ref[rs, cs] = dv + rowv[...]
            db = dqe * qe - dkd * kd + q * dq_in - k * dk_in
            db_s[rs, cs] = db + jnp.where(rowid == CH - 1, dbl, 0.0)

        lax.fori_loop(0, nck, chunk, 0)
        up = _tri(False)
        sgq = _sigmoid(qp)
        dq_ref[...] = dqv_s[...] * (sgq * (1.0 + qp * (1.0 - sgq)))
        dlb_acc = jnp.zeros((1, _HW), F32)
        for r in range(0, tb, _CUM_ROWS):
            rows = slice(r, r + _CUM_ROWS)
            dgl = _dot3(up, db_s[rows, :])
            dfg = dgl * jnp.exp(-g[rows, :]) - dk_s[rows, :]
            sn = sig_neg[rows, :]
            df_ref[rows, :] = dfg * (1.0 - lbv) * (1.0 - sn) * sn
            dlb_acc = dlb_acc + jnp.sum(dfg * sn, axis=0, keepdims=True)
        dlb_ref[...] += dlb_acc

    blk, head, lbs, gws, hist = _hgrn_specs(tb, nt, True)
    out_specs = [head, head, head, head, lbs, pl.BlockSpec((HP, 1, REC_K), lambda h, b, t: (h, 0, 0))]
    out_shape = [jax.ShapeDtypeStruct((n, 1024), F32)] * 4 + [
        jax.ShapeDtypeStruct((1, 1024), F32), jax.ShapeDtypeStruct((REC_HEADS, 1, REC_K), F32)]
    return pl.pallas_call(
        body, name="hgrn_bwd", grid=(REC_HEADS // HP, b_loc, nt),
        in_specs=[blk(0), blk(1), blk(2), blk(3), lbs, gws, head, head, hist], out_specs=out_specs, out_shape=out_shape,
        scratch_shapes=[pltpu.VMEM((tb, _HW), F32)] * 7 + [pltpu.VMEM((HP, CH, REC_K), F32)] * 2 + [pltpu.VMEM((HP, REC_K, REC_K), F32)],
        compiler_params=_params(("arbitrary", "arbitrary", "arbitrary"), 48),
    )(p, p, p, p, lb, gw, oraw, dg, sh)


def _postnorm_bwd_nt(dxo, y, qw, w, has_bias, name):
    n = dxo.shape[0]

    def body(dx_ref, y_ref, qw_ref, w_ref, dg_ref, dy_ref, dqw_ref, db_ref):
        @pl.when(pl.program_id(0) == 0)
        def _():
            dqw_ref[...] = jnp.zeros_like(dqw_ref)
            db_ref[...] = jnp.zeros_like(db_ref)

        yv, dxv = y_ref[...], dx_ref[...]
        r = lax.rsqrt(jnp.mean(yv * yv, axis=-1, keepdims=True) + NORM_EPS)
        u = yv * r
        du = dxv * qw_ref[...]
        dy = r * (du - u * jnp.mean(du * u, axis=-1, keepdims=True))
        dqw_ref[...] += jnp.sum(dxv * u, axis=0, keepdims=True)
        if has_bias:
            db_ref[...] += jnp.sum(dy, axis=0, keepdims=True)
        dyb = dy.astype(BF16)
        dy_ref[...] = dyb
        dg_ref[...] = lax.dot_general(dyb, w_ref[...], NT_DIMS, preferred_element_type=F32)

    rows = pl.BlockSpec((TM, D_MODEL), lambda i: (i, 0))
    const = lambda shape: pl.BlockSpec(shape, lambda i: (0, 0))
    return pl.pallas_call(
        body, name=name, grid=(n // TM,), in_specs=[rows, rows, const((1, D_MODEL)), const((D_MODEL, D_MODEL))],
        out_specs=[rows, rows, const((1, D_MODEL)), const((1, D_MODEL))],
        out_shape=[jax.ShapeDtypeStruct((n, D_MODEL), F32), jax.ShapeDtypeStruct((n, D_MODEL), BF16),
                   jax.ShapeDtypeStruct((1, D_MODEL), F32), jax.ShapeDtypeStruct((1, D_MODEL), F32)],
        compiler_params=_params(("arbitrary",), 48),
    )(dxo, y, qw, w)


def _nt_prenorm_bwd(dps, w, x, pw, dxo, has_bias, name):
    n = x.shape[0]
    widths = [d.shape[1] for d in dps]
    m = sum(widths)
    npieces = len(dps)

    def body(*refs):
        dp_refs = refs[:npieces]
        w_ref, x_ref, pw_ref, dxo_ref, dx_ref, dpw_ref, db_ref = refs[npieces:]

        @pl.when(pl.program_id(0) == 0)
        def _():
            dpw_ref[...] = jnp.zeros_like(dpw_ref)
            db_ref[...] = jnp.zeros_like(db_ref)

        dh = jnp.zeros((TM, D_MODEL), F32)
        off = 0
        for dp_ref, wd in zip(dp_refs, widths):
            cn = _col_chunk(wd)
            for j in range(0, wd, cn):
                dpc = dp_ref[:, j:j + cn]
                if has_bias:
                    db_ref[:, off + j:off + j + cn] += jnp.sum(dpc, axis=0, keepdims=True)
                dh = dh + lax.dot_general(dpc.astype(BF16), w_ref[:, off + j:off + j + cn], NT_DIMS, preferred_element_type=F32)
            off += wd
        xv = x_ref[...]
        r = lax.rsqrt(jnp.mean(xv * xv, axis=-1, keepdims=True) + NORM_EPS)
        xn = xv * r
        dpw_ref[...] += jnp.sum(dh * xn, axis=0, keepdims=True)
        dxn = dh * pw_ref[...]
        dx_ref[...] = dxo_ref[...] + r * (dxn - xn * jnp.mean(dxn * xn, axis=-1, keepdims=True))

    rows = pl.BlockSpec((TM, D_MODEL), lambda i: (i, 0))
    const = lambda shape: pl.BlockSpec(shape, lambda i: (0, 0))
    in_specs = [pl.BlockSpec((TM, wd), lambda i: (i, 0)) for wd in widths] + [const((D_MODEL, m)), rows, const((1, D_MODEL)), rows]
    return pl.pallas_call(
        body, name=name, grid=(n // TM,), in_specs=in_specs,
        out_specs=[rows, const((1, D_MODEL)), const((1, m))],
        out_shape=[jax.ShapeDtypeStruct((n, D_MODEL), F32), jax.ShapeDtypeStruct((1, D_MODEL), F32),
                   jax.ShapeDtypeStruct((1, m), F32)],
        compiler_params=_params(("arbitrary",), 56),
    )(*dps, w, x, pw, dxo)


def _matmul_tn(a, b, name):
    n, k = a.shape
    m = b.shape[1]
    tk, tm, tn = 512, _col_chunk(m), 1024 if n % 1024 == 0 else n

    def body(a_ref, b_ref, o_ref):
        @pl.when(pl.program_id(2) == 0)
        def _():
            o_ref[...] = jnp.zeros_like(o_ref)

        o_ref[...] += lax.dot_general(a_ref[...], b_ref[...].astype(BF16), TN_DIMS, preferred_element_type=F32)

    return pl.pallas_call(
        body, name=name, grid=(k // tk, m // tm, n // tn),
        in_specs=[pl.BlockSpec((tn, tk), lambda i, j, l: (l, i)), pl.BlockSpec((tn, tm), lambda i, j, l: (l, j))],
        out_specs=pl.BlockSpec((tk, tm), lambda i, j, l: (i, j)),
        out_shape=jax.ShapeDtypeStruct((k, m), F32),
        compiler_params=_params(("arbitrary", "arbitrary", "arbitrary"), 48),
    )(a, b)


def _by_owner_cols(dw):
    k, m = dw.shape
    return dw.reshape(k, N_DEV, m // N_DEV).transpose(1, 0, 2)


def _own_and_bf16(part):
    return lax.dynamic_index_in_dim(part, _my_id(), 0, keepdims=False), part.astype(BF16)


def _step(x, pos_col, tgt, pre_w, post_w, wa_in, ba_in, sinks, wa_out, ba_out, wr_in_shard, lb_logits, gnorm_w, wr_out_shard, b_loc, t_len):
    nb = t_len // BLK
    ct, st = _rope_tables(pos_col)
    lb = _lower_bound(lb_logits)
    p0, h0 = _norm_matmul(x, pre_w[0:1], wa_in, ba_in, "attn_in_proj")
    o0, g0, gr_in, gr_out = _attn_fwd(p0, ct, st, sinks, b_loc, nb, [wr_in_shard, wr_out_shard])
    wr_in = gr_in.transpose(1, 0, 2).reshape(D_MODEL, REC_IN)
    wr_out = gr_out.reshape(1024, D_MODEL)
    y0, x1 = _outproj_postnorm(g0, wa_out, ba_out, x, post_w[0:1], None, "attn_out_proj")
    p1, h1 = _norm_matmul(x1, pre_w[1:2], wr_in, None, "rec_in_proj")
    o1, g1, sh = _hgrn_fwd(p1, lb, gnorm_w, b_loc, t_len)
    y1, dx2, loss_tile = _outproj_postnorm(g1, wr_out, None, x1, post_w[1:2], tgt, "rec_out_proj_loss")
    dg1, dy1, dpost1, _ = _postnorm_bwd_nt(dx2, y1, post_w[1:2], wr_out, False, "rec_out_bwd")
    d_wr_out = _matmul_tn(g1, dy1, "rec_w_out_grad")
    dq1, df1, di1, dz1, dlb, dgw = _hgrn_bwd(p1, lb, gnorm_w, o1, sh, dg1, b_loc, t_len)
    dps1 = [dq1, df1, di1, dz1]
    dx1, dpre1, _ = _nt_prenorm_bwd(dps1, wr_in, x1, pre_w[1:2], dx2, False, "rec_in_bwd")
    d_wr_in = [_matmul_tn(h1, dpk, f"rec_w_in_grad_{k}") for k, dpk in enumerate(dps1)]
    dg0, dy0, dpost0, dba_out = _postnorm_bwd_nt(dx1, y0, post_w[0:1], wa_out, True, "attn_out_bwd")
    d_wa_out = _matmul_tn(g0, dy0, "attn_w_out_grad")
    owns, wires = zip(*[_own_and_bf16(part) for part in (
        _by_owner_cols(jnp.concatenate(d_wr_in, axis=1)), d_wr_out.reshape(N_DEV, 1024 // N_DEV, D_MODEL),
        d_wa_out.reshape(N_DEV, ATTN_WIDTH // N_DEV, D_MODEL))])
    dp0, dsink_tile, *lands = _attn_bwd(p0, ct, st, sinks, o0, dg0, b_loc, nb, list(wires))
    dx0, dpre0, dba_in = _nt_prenorm_bwd([dp0], wa_in, x, pre_w[0:1], dx1, True, "attn_in_bwd")
    d_wa_in = _matmul_tn(h0, dp0, "attn_w_in_grad")
    small = dict(pre=jnp.concatenate([dpre0, dpre1], axis=0), post=jnp.concatenate([dpost0, dpost1], axis=0),
                 ba_in=dba_in, sinks=dsink_tile[0:1, 0:N_HEADS], ba_out=dba_out, lb=dlb, gnorm=jnp.sum(dgw, axis=0))
    return loss_tile, dx0, d_wa_in, list(zip(lands, owns)), small


def _my_id():
    return lax.axis_index("x") * 4 + lax.axis_index("y") * 2 + lax.axis_index("c")


def _peer(k):
    x, y, c = lax.axis_index("x"), lax.axis_index("y"), lax.axis_index("c")
    return (x ^ ((k >> 2) & 1), y ^ ((k >> 1) & 1), c ^ (k & 1))


def _peer_id(k):
    return _my_id() ^ k


def _all_gather(shards):
    nsh = len(shards)

    def body(*refs):
        ins, outs, sems = refs[:nsh], refs[nsh:2 * nsh], refs[2 * nsh:]
        _gather_start(ins, outs, sems)
        _gather_wait(ins, outs, sems)

    hbm = pl.BlockSpec(memory_space=pl.ANY)
    return pl.pallas_call(
        body, name="comm_all_gather", in_specs=[hbm] * nsh, out_specs=[hbm] * nsh,
        out_shape=_gather_shapes(shards), scratch_shapes=_gather_sems(nsh),
    )(*shards)


def _gather_shapes(shards):
    return [jax.ShapeDtypeStruct((N_DEV,) + s.shape, s.dtype) for s in shards]


def _gather_sems(nsh):
    return [pltpu.SemaphoreType.DMA((nsh, N_DEV - 1)), pltpu.SemaphoreType.DMA((nsh, N_DEV - 1)), pltpu.SemaphoreType.DMA((nsh,))]


def _gather_copies(ins, outs, sems):
    send_sems, recv_sems, local_sems = sems
    me = _my_id()
    local, sends, recvs = [], [], []
    for a in range(len(ins)):
        local.append(pltpu.make_async_copy(ins[a], outs[a].at[me], local_sems.at[a]))
        for k in range(1, N_DEV):
            for row, group in ((me, sends), (_peer_id(k), recvs)):
                group.append(pltpu.make_async_remote_copy(
                    src_ref=ins[a], dst_ref=outs[a].at[row], send_sem=send_sems.at[a, k - 1], recv_sem=recv_sems.at[a, k - 1],
                    device_id=_peer(k), device_id_type=MESH))
    return local, sends, recvs


def _gather_start(ins, outs, sems):
    local, sends, _ = _gather_copies(ins, outs, sems)
    for cp in local + sends:
        cp.start()


def _gather_wait(ins, outs, sems):
    local, sends, recvs = _gather_copies(ins, outs, sems)
    for cp in recvs:
        cp.wait_recv()
    for cp in sends:
        cp.wait_send()
    for cp in local:
        cp.wait()


def _scatter_lands(parts):
    return [jax.ShapeDtypeStruct((N_DEV - 1,) + p.shape[1:], p.dtype) for p in parts]


def _scatter_sems(nparts):
    return [pltpu.SemaphoreType.DMA((nparts, N_DEV - 1)), pltpu.SemaphoreType.DMA((nparts, N_DEV - 1))]


def _scatter_copies(parts, lands, sems):
    send_sems, recv_sems = sems
    return [pltpu.make_async_remote_copy(
        src_ref=parts[a].at[_peer_id(k)], dst_ref=lands[a].at[k - 1], send_sem=send_sems.at[a, k - 1],
        recv_sem=recv_sems.at[a, k - 1], device_id=_peer(k), device_id_type=MESH)
        for a in range(len(parts)) for k in range(1, N_DEV)]


def _scatter_start(parts, lands, sems):
    for cp in _scatter_copies(parts, lands, sems):
        cp.start()


def _scatter_wait(parts, lands, sems):
    copies = _scatter_copies(parts, lands, sems)
    for cp in copies:
        cp.wait_recv()
    for cp in copies:
        cp.wait_send()


def _adamw(w, g, m, v):
    m2 = ADAM_B1 * m + (1.0 - ADAM_B1) * g
    v2 = ADAM_B2 * v + (1.0 - ADAM_B2) * (g * g)
    m_hat = m2 / (1.0 - ADAM_B1 ** ADAM_STEP)
    v_hat = v2 / (1.0 - ADAM_B2 ** ADAM_STEP)
    delta = -ADAM_LR * (m_hat / (jnp.sqrt(v_hat) + ADAM_EPS) + ADAM_WD * w)
    return delta, m2, v2


def _reduce_scatter_adamw(part_bf, own, w, m, v, name):
    r, c = own.shape

    def body(part_ref, own_ref, w_ref, m_ref, v_ref, g_ref, d_ref, m2_ref, v2_ref, land_ref, send_sems, recv_sems):
        _scatter_start([part_ref], [land_ref], (send_sems, recv_sems))
        _scatter_wait([part_ref], [land_ref], (send_sems, recv_sems))
        _sum_adamw_rows(land_ref, own_ref, w_ref, m_ref, v_ref, (g_ref, d_ref, m2_ref, v2_ref))

    vmem = pl.BlockSpec(memory_space=pltpu.VMEM)
    return pl.pallas_call(
        body, name=name, in_specs=[pl.BlockSpec(memory_space=pl.ANY)] + [vmem] * 4, out_specs=[vmem] * 4,
        out_shape=[jax.ShapeDtypeStruct((r, c), F32)] * 4,
        scratch_shapes=[pltpu.VMEM((N_DEV - 1, r, c), BF16)] + _scatter_sems(1),
        compiler_params=_params(None, 56),
    )(part_bf, own, w, m, v)


def _sum_adamw_rows(land_ref, own_ref, w_ref, m_ref, v_ref, out_refs):
    r, c = own_ref.shape
    rc = 64 if r % 64 == 0 else r
    me = _my_id()
    g_ref, d_ref, m2_ref, v2_ref = out_refs

    def rows(i, carry):
        rs = pl.ds(pl.multiple_of(i * rc, rc), rc)
        g = jnp.zeros((rc, c), F32)
        for dev in range(N_DEV):
            k = dev ^ me
            g = g + jnp.where(k == 0, own_ref[rs, :], land_ref[jnp.maximum(k - 1, 0), rs, :].astype(F32))
        delta, m2, v2 = _adamw(w_ref[rs, :], g, m_ref[rs, :], v_ref[rs, :])
        g_ref[rs, :] = g
        d_ref[rs, :] = delta
        m2_ref[rs, :] = m2
        v2_ref[rs, :] = v2
        return carry

    lax.fori_loop(0, r // rc, rows, 0)


def _sum_adamw(land, own, w, m, v, name):
    r, c = own.shape

    def body(land_ref, own_ref, w_ref, m_ref, v_ref, g_ref, d_ref, m2_ref, v2_ref):
        _sum_adamw_rows(land_ref, own_ref, w_ref, m_ref, v_ref, (g_ref, d_ref, m2_ref, v2_ref))

    vmem = pl.BlockSpec(memory_space=pltpu.VMEM)
    return pl.pallas_call(
        body, name=name, in_specs=[vmem] * 5, out_specs=[vmem] * 4, out_shape=[jax.ShapeDtypeStruct((r, c), F32)] * 4,
        compiler_params=_params(None, 56),
    )(land, own, w, m, v)


_SMALL = [("pre_norm_w", 2048), ("post_norm_w", 2048), ("attn_b_in", 2304), ("attn_sinks", 16), ("attn_b_out", 1024),
          ("rec_lb_logits", 2048), ("rec_gnorm_w", 128)]
_TILE = SUBLANES * LANES


def _small_rows(size):
    return -(-size // _TILE) * SUBLANES


_SMALL_OFF = {}
_r = 0
for _name, _size in _SMALL:
    _SMALL_OFF[_name] = _r
    _r += _small_rows(_size)
_SMALL_ROWS = _r


def _pack_small(pieces):
    out = []
    for name, size in _SMALL:
        flat = pieces[name].reshape(-1).astype(F32)
        out.append(jnp.pad(flat, (0, _small_rows(size) * LANES - size)).reshape(-1, LANES))
    return jnp.concatenate(out, axis=0)


def _unpack_small(packed, shapes):
    return {name: packed[_SMALL_OFF[name]:_SMALL_OFF[name] + _small_rows(size)].reshape(-1)[:size].reshape(shapes[name])
            for name, size in _SMALL}


def _small_allreduce_adamw(gpart, w, m, v):
    lb0 = _SMALL_OFF["rec_lb_logits"]

    def body(gp_ref, w_ref, m_ref, v_ref, g_ref, d_ref, m2_ref, v2_ref, land_ref, send_sems, recv_sems):
        me = _my_id()
        sent = []
        for k in range(1, N_DEV):
            cp = pltpu.make_async_remote_copy(src_ref=gp_ref, dst_ref=land_ref.at[k - 1], send_sem=send_sems.at[k - 1],
                                              recv_sem=recv_sems.at[k - 1], device_id=_peer(k), device_id_type=MESH)
            cp.start()
            sent.append(cp)
        for cp in sent:
            cp.wait_recv()
        for cp in sent:
            cp.wait_send()
        g = jnp.zeros((_SMALL_ROWS, LANES), F32)
        for dev in range(N_DEV):
            k = dev ^ me
            g = g + jnp.where(k == 0, gp_ref[...], land_ref[jnp.maximum(k - 1, 0)])
        g_ref[...] = g
        l0, l1 = w_ref[lb0:lb0 + SUBLANES, :], w_ref[lb0 + SUBLANES:lb0 + 2 * SUBLANES, :]
        mx = jnp.maximum(l0, l1)
        e0, e1 = jnp.exp(l0 - mx), jnp.exp(l1 - mx)
        p1 = e1 / (e0 + e1)
        dl1 = (1.0 - p1) * p1 * g[lb0:lb0 + SUBLANES, :]
        g_ref[lb0:lb0 + SUBLANES, :] = -dl1
        g_ref[lb0 + SUBLANES:lb0 + 2 * SUBLANES, :] = dl1
        delta, m2, v2 = _adamw(w_ref[...], g_ref[...], m_ref[...], v_ref[...])
        d_ref[...] = delta
        m2_ref[...] = m2
        v2_ref[...] = v2

    vmem = pl.BlockSpec(memory_space=pltpu.VMEM)
    return pl.pallas_call(
        body, name="comm_small_allreduce_adamw", in_specs=[vmem] * 4, out_specs=[vmem] * 4,
        out_shape=[jax.ShapeDtypeStruct((_SMALL_ROWS, LANES), F32)] * 4,
        scratch_shapes=[pltpu.VMEM((N_DEV - 1, _SMALL_ROWS, LANES), F32), pltpu.SemaphoreType.DMA((N_DEV - 1,)),
                        pltpu.SemaphoreType.DMA((N_DEV - 1,))],
    )(gpart, w, m, v)


def _qzkv(a):
    return jnp.concatenate([a[..., :1024], a[..., 1280:], a[..., 1024:1280]], axis=-1)


def _qkvz(a):
    return jnp.concatenate([a[..., :1024], a[..., 2048:], a[..., 1024:2048]], axis=-1)


def kernel(x, positions, pre_norm_w, post_norm_w, attn_w_in, attn_b_in, attn_sinks, attn_w_out, attn_b_out, rec_w_in, rec_lb_logits, rec_gnorm_w, rec_w_out, loss_target, m_pre_norm_w, m_post_norm_w, m_attn_w_in, m_attn_b_in, m_attn_sinks, m_attn_w_out, m_attn_b_out, m_rec_w_in, m_rec_lb_logits, m_rec_gnorm_w, m_rec_w_out, v_pre_norm_w, v_post_norm_w, v_attn_w_in, v_attn_b_in, v_attn_sinks, v_attn_w_out, v_attn_b_out, v_rec_w_in, v_rec_lb_logits, v_rec_gnorm_w, v_rec_w_out):
    b_loc, t_len, _ = x.shape
    n = b_loc * t_len
    ga_in, ga_out = _all_gather([attn_w_in[0].astype(BF16), attn_w_out[0].astype(BF16)])
    wa_in = _qzkv(ga_in.transpose(1, 0, 2).reshape(D_MODEL, ATTN_IN))
    wa_out = ga_out.reshape(ATTN_WIDTH, D_MODEL)

    loss_tile, dx, d_wa_in, landed, small = _step(
        x.reshape(n, D_MODEL), positions.reshape(n, 1).astype(F32), loss_target.reshape(n, D_MODEL),
        pre_norm_w, post_norm_w, wa_in, _qzkv(attn_b_in), attn_sinks, wa_out, attn_b_out, rec_w_in[0].astype(BF16),
        rec_lb_logits, rec_gnorm_w, rec_w_out[0].astype(BF16), b_loc, t_len)
    loss = lax.psum(loss_tile[0, 0], ("x", "y", "c"))

    lift = lambda outs: tuple(a[None] for a in outs)
    own, wire = _own_and_bf16(_by_owner_cols(_qkvz(d_wa_in)))
    r_a_in = lift(_reduce_scatter_adamw(wire, own, attn_w_in[0], m_attn_w_in[0], v_attn_w_in[0], "comm_rs_attn_w_in"))
    (l_r_in, o_r_in), (l_r_out, o_r_out), (l_a_out, o_a_out) = landed
    r_r_in = lift(_sum_adamw(l_r_in, o_r_in, rec_w_in[0], m_rec_w_in[0], v_rec_w_in[0], "adamw_rec_w_in"))
    r_r_out = lift(_sum_adamw(l_r_out, o_r_out, rec_w_out[0], m_rec_w_out[0], v_rec_w_out[0], "adamw_rec_w_out"))
    r_a_out = lift(_sum_adamw(l_a_out, o_a_out, attn_w_out[0], m_attn_w_out[0], v_attn_w_out[0], "adamw_attn_w_out"))

    gsmall = dict(pre_norm_w=small["pre"], post_norm_w=small["post"], attn_b_in=_qkvz(small["ba_in"]), attn_sinks=small["sinks"],
                  attn_b_out=small["ba_out"], rec_lb_logits=jnp.concatenate([small["lb"], jnp.zeros_like(small["lb"])], axis=0),
                  rec_gnorm_w=small["gnorm"])
    wsmall = dict(pre_norm_w=pre_norm_w, post_norm_w=post_norm_w, attn_b_in=attn_b_in, attn_sinks=attn_sinks,
                  attn_b_out=attn_b_out, rec_lb_logits=rec_lb_logits, rec_gnorm_w=rec_gnorm_w)
    msmall = dict(pre_norm_w=m_pre_norm_w, post_norm_w=m_post_norm_w, attn_b_in=m_attn_b_in, attn_sinks=m_attn_sinks,
                  attn_b_out=m_attn_b_out, rec_lb_logits=m_rec_lb_logits, rec_gnorm_w=m_rec_gnorm_w)
    vsmall = dict(pre_norm_w=v_pre_norm_w, post_norm_w=v_post_norm_w, attn_b_in=v_attn_b_in, attn_sinks=v_attn_sinks,
                  attn_b_out=v_attn_b_out, rec_lb_logits=v_rec_lb_logits, rec_gnorm_w=v_rec_gnorm_w)
    shapes = {k: a.shape for k, a in wsmall.items()}
    packed = _small_allreduce_adamw(_pack_small(gsmall), _pack_small(wsmall), _pack_small(msmall), _pack_small(vsmall))
    sg, sd, sm, sv = [_unpack_small(a, shapes) for a in packed]

    big = {"attn_w_in": r_a_in, "attn_w_out": r_a_out, "rec_w_in": r_r_in, "rec_w_out": r_r_out}
    order = ["pre_norm_w", "post_norm_w", "attn_w_in", "attn_b_in", "attn_sinks", "attn_w_out", "attn_b_out", "rec_w_in",
             "rec_lb_logits", "rec_gnorm_w", "rec_w_out"]
    outs = [loss, dx.reshape(b_loc, t_len, D_MODEL)]
    for idx, small_set in enumerate((sg, sd, sm, sv)):
        outs += [big[nm][idx] if nm in big else small_set[nm] for nm in order]
    return tuple(outs)
```

```python
import numpy as np
import jax
import jax.numpy as jnp
from jax import lax
from jax.experimental import pallas as pl
from jax.experimental.pallas import tpu as pltpu

F32, BF16 = jnp.float32, jnp.bfloat16
MESH = pl.DeviceIdType.MESH
N_DEV = 8

D_MODEL = 1024
N_HEADS, HEAD_DIM, N_KV, GROUP = 16, 64, 2, 8
ATTN_WIDTH, KV_WIDTH = 1024, 128
ATTN_IN = 2 * ATTN_WIDTH + 2 * KV_WIDTH
BLK = 128
ROPE_THETA, ROPE_HALF = 500000.0, 8
REC_HEADS, REC_K = 8, 128
REC_IN = 4 * 1024
CH = 32
NORM_EPS = 1e-6
ADAM_LR, ADAM_B1, ADAM_B2, ADAM_EPS, ADAM_WD, ADAM_STEP = 0.001, 0.9, 0.999, 1e-08, 0.01, 10

LANES, SUBLANES = 128, 8
TM = 512
NT_DIMS = (((1,), (1,)), ((), ()))
TN_DIMS = (((0,), (0,)), ((), ()))
MB = 2 ** 20


def _params(sem=None, vmem_mb=48, **kw):
    return pltpu.CompilerParams(dimension_semantics=sem, vmem_limit_bytes=vmem_mb * MB, **kw)


def _col_chunk(m):
    return 768 if m % 1024 else 1024


def _sigmoid(x):
    return 1.0 / (1.0 + jnp.exp(-x))


def _split3(x):
    hi = x.astype(BF16)
    r1 = x - hi.astype(F32)
    mid = r1.astype(BF16)
    lo = (r1 - mid.astype(F32)).astype(BF16)
    return hi, mid, lo


def _dot3(l_bf, x):
    hi, mid, lo = _split3(x)
    return (jnp.dot(l_bf, hi, preferred_element_type=F32) + jnp.dot(l_bf, mid, preferred_element_type=F32)
            + jnp.dot(l_bf, lo, preferred_element_type=F32))


def _rope_tables(pos_col):
    n = pos_col.shape[0]
    lane = np.arange(LANES) % HEAD_DIM
    inv = np.float32(ROPE_THETA) ** (-(np.arange(ROPE_HALF, dtype=np.float32) * np.float32(2.0) / np.float32(2 * ROPE_HALF)))
    freq = np.where(lane < 2 * ROPE_HALF, inv[lane % ROPE_HALF], 0.0).astype(np.float32)[None, :]
    sign = np.where(lane < ROPE_HALF, -1.0, np.where(lane < 2 * ROPE_HALF, 1.0, 0.0)).astype(np.float32)[None, :]

    def body(p_ref, f_ref, s_ref, c_out, s_out):
        ang = p_ref[...] * f_ref[...]
        c_out[...] = jnp.cos(ang)
        s_out[...] = jnp.sin(ang) * s_ref[...]

    row = pl.BlockSpec((TM, 1), lambda i: (i, 0))
    vec = pl.BlockSpec((1, LANES), lambda i: (0, 0))
    out = pl.BlockSpec((TM, LANES), lambda i: (i, 0))
    return pl.pallas_call(
        body, name="rope_tables", grid=(n // TM,), in_specs=[row, vec, vec], out_specs=[out, out],
        out_shape=[jax.ShapeDtypeStruct((n, LANES), F32)] * 2, compiler_params=_params(("arbitrary",)),
    )(pos_col, jnp.asarray(freq), jnp.asarray(sign))


def _rope_apply(xv, c, s, lm):
    partner = jnp.where(lm < ROPE_HALF, pltpu.roll(xv, LANES - ROPE_HALF, 1), pltpu.roll(xv, ROPE_HALF, 1))
    return xv * c + partner * s


def _rope_bwd(dy, c, s, lm):
    t = dy * s
    partner = jnp.where(lm < ROPE_HALF, pltpu.roll(t, LANES - ROPE_HALF, 1),
                        jnp.where(lm < 2 * ROPE_HALF, pltpu.roll(t, ROPE_HALF, 1), 0.0))
    return dy * c + partner


def _lower_bound(lb_logits):
    def body(l_ref, o_ref):
        l0, l1 = l_ref[0:1, :], l_ref[1:2, :]
        m = jnp.maximum(l0, l1)
        e0, e1 = jnp.exp(l0 - m), jnp.exp(l1 - m)
        o_ref[...] = e1 / (e0 + e1)

    return pl.pallas_call(body, name="lower_bound", out_shape=jax.ShapeDtypeStruct((1, lb_logits.shape[1]), F32))(lb_logits)


def _norm_matmul(x, pw, w, bias, name, shards=()):
    n, m = x.shape[0], w.shape[1]
    cn = _col_chunk(m)
    has_bias = bias is not None
    nsh, steps = len(shards), n // TM

    def body(*refs):
        refs = list(refs)
        x_ref, pw_ref, w_ref = refs[:3]
        b_ref = refs[3] if has_bias else None
        refs = refs[4 if has_bias else 3:]
        sh_in, (p_ref, h_ref), sh_out, sems = refs[:nsh], refs[nsh:nsh + 2], refs[nsh + 2:2 * nsh + 2], refs[2 * nsh + 2:]
        if nsh:
            @pl.when(pl.program_id(0) == 0)
            def _():
                _gather_start(sh_in, sh_out, sems)

        xv = x_ref[...]
        r = lax.rsqrt(jnp.mean(xv * xv, axis=-1, keepdims=True) + NORM_EPS)
        h = ((xv * r) * pw_ref[...]).astype(BF16)
        h_ref[...] = h
        for j in range(0, m, cn):
            acc = jnp.dot(h, w_ref[:, j:j + cn], preferred_element_type=F32)
            if has_bias:
                acc = acc + b_ref[:, j:j + cn]
            p_ref[:, j:j + cn] = acc

        if nsh:
            @pl.when(pl.program_id(0) == steps - 1)
            def _():
                _gather_wait(sh_in, sh_out, sems)

    rows = pl.BlockSpec((TM, D_MODEL), lambda i: (i, 0))
    const = lambda shape: pl.BlockSpec(shape, lambda i: (0, 0))
    hbm = pl.BlockSpec(memory_space=pl.ANY)
    in_specs = [rows, const((1, D_MODEL)), const((D_MODEL, m))] + ([const((1, m))] if has_bias else []) + [hbm] * nsh
    args = (x, pw, w) + ((bias,) if has_bias else ()) + tuple(shards)
    return pl.pallas_call(
        body, name=name, grid=(steps,), in_specs=in_specs,
        out_specs=[pl.BlockSpec((TM, m), lambda i: (i, 0)), rows] + [hbm] * nsh,
        out_shape=[jax.ShapeDtypeStruct((n, m), F32), jax.ShapeDtypeStruct((n, D_MODEL), BF16)] + _gather_shapes(shards),
        scratch_shapes=_gather_sems(nsh) if nsh else [],
        compiler_params=_params(("arbitrary",), 56),
    )(*args)


def _outproj_postnorm(g, w, bias, xres, qw, tgt, name):
    n = g.shape[0]
    has_bias, has_loss = bias is not None, tgt is not None
    steps = n // TM

    def body(*refs):
        refs = list(refs)
        g_ref, w_ref = refs.pop(0), refs.pop(0)
        b_ref = refs.pop(0) if has_bias else None
        x_ref, qw_ref = refs.pop(0), refs.pop(0)
        t_ref = refs.pop(0) if has_loss else None
        y_ref, o_ref = refs.pop(0), refs.pop(0)
        y = jnp.dot(g_ref[...], w_ref[...], preferred_element_type=F32)
        if has_bias:
            y = y + b_ref[...]
        y_ref[...] = y
        r = lax.rsqrt(jnp.mean(y * y, axis=-1, keepdims=True) + NORM_EPS)
        xn = x_ref[...] + (y * r) * qw_ref[...]
        if not has_loss:
            o_ref[...] = xn
        else:
            loss_ref, acc_ref = refs
            i = pl.program_id(0)
            e = xn - t_ref[...]
            o_ref[...] = e * (1.0 / D_MODEL)

            @pl.when(i == 0)
            def _():
                acc_ref[...] = jnp.zeros_like(acc_ref)

            acc_ref[...] += jnp.sum(e * e, axis=0, keepdims=True)

            @pl.when(i == steps - 1)
            def _():
                loss_ref[...] = jnp.full(loss_ref.shape, jnp.sum(acc_ref[...]) * (0.5 / D_MODEL), F32)

    rows = pl.BlockSpec((TM, D_MODEL), lambda i: (i, 0))
    const = lambda shape: pl.BlockSpec(shape, lambda i: (0, 0))
    in_specs = [rows, const((D_MODEL, D_MODEL))] + ([const((1, D_MODEL))] if has_bias else []) + [rows, const((1, D_MODEL))]
    args = [g, w] + ([bias] if has_bias else []) + [xres, qw]
    out_specs = [rows, rows]
    out_shape = [jax.ShapeDtypeStruct((n, D_MODEL), F32)] * 2
    scratch = []
    if has_loss:
        in_specs.append(rows)
        args.append(tgt)
        out_specs.append(const((SUBLANES, LANES)))
        out_shape.append(jax.ShapeDtypeStruct((SUBLANES, LANES), F32))
        scratch = [pltpu.VMEM((1, D_MODEL), F32)]
    return pl.pallas_call(
        body, name=name, grid=(steps,), in_specs=in_specs, out_specs=out_specs, out_shape=out_shape,
        scratch_shapes=scratch, compiler_params=_params(("arbitrary",), 48),
    )(*args)


_QCOL, _ZCOL, _KCOL, _VCOL = 0, 1024, 2048, 2176


def _attn_probs(qr, kcat, h, sink_ref, valid):
    qst = jnp.concatenate(
        [qr[4 * h + g // 2][:, (g % 2) * HEAD_DIM:(g % 2 + 1) * HEAD_DIM] for g in range(GROUP)], axis=0).astype(BF16)
    kh = kcat[:, h * HEAD_DIM:(h + 1) * HEAD_DIM].astype(BF16)
    s = lax.dot_general(qst, kh, NT_DIMS, preferred_element_type=F32)
    s = jnp.where(valid, s, -1e30)
    sk = jnp.concatenate([jnp.full((BLK, 1), sink_ref[0, h * GROUP + g], F32) for g in range(GROUP)], axis=0)
    m = jnp.maximum(jnp.max(s, axis=-1, keepdims=True), sk)
    p = jnp.exp(s - m)
    esk = jnp.exp(sk - m)
    inv = 1.0 / (jnp.sum(p, axis=-1, keepdims=True) + esk)
    return qst, kh, p * inv, esk * inv


def _attn_mask(i):
    row = lax.broadcasted_iota(jnp.int32, (GROUP * BLK, 2 * BLK), 0) & (BLK - 1)
    col = lax.broadcasted_iota(jnp.int32, (GROUP * BLK, 2 * BLK), 1)
    return (col > row) & (col <= row + BLK) & ((col >= BLK) | (i > 0))


def _unstack_heads(xh, c2):
    return jnp.concatenate([xh[(2 * c2) * BLK:(2 * c2 + 1) * BLK], xh[(2 * c2 + 1) * BLK:(2 * c2 + 2) * BLK]], axis=1)


def _stack_heads(chunks, h):
    return jnp.concatenate(
        [chunks[4 * h + g // 2][:, (g % 2) * HEAD_DIM:(g % 2 + 1) * HEAD_DIM] for g in range(GROUP)], axis=0)


def _attn_fwd(p, ct, st, sinks, b_loc, nb, shards):
    n = p.shape[0]
    nsh = len(shards)

    def body(sink_ref, q_ref, z_ref, kc_ref, kp_ref, vc_ref, vp_ref, cc_ref, sc_ref, cp_ref, sp_ref, *rest):
        sh_in, (o_ref, g_ref), sh_out, sems = rest[:nsh], rest[nsh:nsh + 2], rest[nsh + 2:2 * nsh + 2], rest[2 * nsh + 2:]
        b, i = pl.program_id(0), pl.program_id(1)

        @pl.when((b == 0) & (i == 0))
        def _():
            _gather_start(sh_in, sh_out, sems)

        lane = lax.broadcasted_iota(jnp.int32, (BLK, LANES), 1)
        lm = lane & (HEAD_DIM - 1)
        cc, sc = cc_ref[...], sc_ref[...]
        kcat = jnp.concatenate([_rope_apply(kp_ref[...], cp_ref[...], sp_ref[...], lm),
                                _rope_apply(kc_ref[...], cc, sc, lm)], axis=0)
        vcat = jnp.concatenate([vp_ref[...], vc_ref[...]], axis=0)
        qr = [_rope_apply(q_ref[:, c * LANES:(c + 1) * LANES], cc, sc, lm) * (HEAD_DIM ** -0.5) for c in range(8)]
        valid = _attn_mask(i)
        for h in range(N_KV):
            _, _, pn, _ = _attn_probs(qr, kcat, h, sink_ref, valid)
            vh = vcat[:, h * HEAD_DIM:(h + 1) * HEAD_DIM].astype(BF16)
            oh = jnp.dot(pn.astype(BF16), vh, preferred_element_type=F32)
            for c2 in range(4):
                oc = _unstack_heads(oh, c2)
                cols = slice((4 * h + c2) * LANES, (4 * h + c2 + 1) * LANES)
                zc = z_ref[:, cols]
                o_ref[:, cols] = oc
                g_ref[:, cols] = (oc * (zc * _sigmoid(zc))).astype(BF16)

        @pl.when((b == b_loc - 1) & (i == nb - 1))
        def _():
            _gather_wait(sh_in, sh_out, sems)

    cur = lambda b, i: b * nb + i
    prev = lambda b, i: b * nb + jnp.maximum(i - 1, 0)
    wide = lambda cb: pl.BlockSpec((BLK, ATTN_WIDTH), lambda b, i: (cur(b, i), cb))
    kv = lambda rowf, cb: pl.BlockSpec((BLK, LANES), lambda b, i: (rowf(b, i), cb))
    hbm = pl.BlockSpec(memory_space=pl.ANY)
    in_specs = [pl.BlockSpec(memory_space=pltpu.SMEM), wide(0), wide(1),
                kv(cur, _KCOL // LANES), kv(prev, _KCOL // LANES), kv(cur, _VCOL // LANES), kv(prev, _VCOL // LANES),
                kv(cur, 0), kv(cur, 0), kv(prev, 0), kv(prev, 0)] + [hbm] * nsh
    return pl.pallas_call(
        body, name="attn_fwd", grid=(b_loc, nb), in_specs=in_specs, out_specs=[wide(0), wide(0)] + [hbm] * nsh,
        out_shape=[jax.ShapeDtypeStruct((n, ATTN_WIDTH), F32), jax.ShapeDtypeStruct((n, ATTN_WIDTH), BF16)] + _gather_shapes(shards),
        scratch_shapes=_gather_sems(nsh), compiler_params=_params(("arbitrary", "arbitrary"), 48),
    )(sinks, p, p, p, p, p, p, ct, st, ct, st, *shards)


def _attn_bwd(p, ct, st, sinks, o, dg, b_loc, nb, parts):
    n = p.shape[0]
    nparts = len(parts)

    def body(sink_ref, q_ref, z_ref, kc_ref, kp_ref, vc_ref, vp_ref, cc_ref, sc_ref, cp_ref, sp_ref, o_ref, dg_ref, *rest):
        part_refs, (dp_ref, ds_ref), land_refs = rest[:nparts], rest[nparts:nparts + 2], rest[nparts + 2:2 * nparts + 2]
        dq_s, dz_s, dk_s, dv_s = rest[2 * nparts + 2:2 * nparts + 6]
        sems = rest[2 * nparts + 6:]
        b, i = pl.program_id(0), pl.program_id(1)

        @pl.when((b == 0) & (i == 0))
        def _():
            _scatter_start(part_refs, land_refs, sems)

        @pl.when((b == b_loc - 1) & (i == nb))
        def _():
            _scatter_wait(part_refs, land_refs, sems)

        lane = lax.broadcasted_iota(jnp.int32, (BLK, LANES), 1)
        lm = lane & (HEAD_DIM - 1)

        @pl.when((b == 0) & (i == 0))
        def _():
            ds_ref[...] = jnp.zeros_like(ds_ref)

        @pl.when(i < nb)
        def _compute():
            cc, sc = cc_ref[...], sc_ref[...]
            kcat = jnp.concatenate([_rope_apply(kp_ref[...], cp_ref[...], sp_ref[...], lm),
                                    _rope_apply(kc_ref[...], cc, sc, lm)], axis=0)
            vcat = jnp.concatenate([vp_ref[...], vc_ref[...]], axis=0)
            qr = [_rope_apply(q_ref[:, c * LANES:(c + 1) * LANES], cc, sc, lm) * (HEAD_DIM ** -0.5) for c in range(8)]
            valid = _attn_mask(i)
            do_chunks, o_chunks, dz_chunks = [], [], []
            for c in range(8):
                cols = slice(c * LANES, (c + 1) * LANES)
                zc, oc, dgc = z_ref[:, cols], o_ref[:, cols], dg_ref[:, cols]
                sg = _sigmoid(zc)
                do_chunks.append(dgc * (zc * sg))
                dz_chunks.append(dgc * oc * (sg * (1.0 + zc * (1.0 - sg))))
                o_chunks.append(oc)
            dq_chunks = [None] * 8
            dk_h, dv_h = [], []
            ds_acc = jnp.zeros((SUBLANES, LANES), F32)
            tile_lane = lax.broadcasted_iota(jnp.int32, (SUBLANES, LANES), 1)
            tile_row = lax.broadcasted_iota(jnp.int32, (SUBLANES, LANES), 0)
            for h in range(N_KV):
                qst, kh, pn, psink = _attn_probs(qr, kcat, h, sink_ref, valid)
                vh = vcat[:, h * HEAD_DIM:(h + 1) * HEAD_DIM].astype(BF16)
                do_st = _stack_heads(do_chunks, h)
                delta = jnp.sum(do_st * _stack_heads(o_chunks, h), axis=-1, keepdims=True)
                do_bf = do_st.astype(BF16)
                dpm = lax.dot_general(do_bf, vh, NT_DIMS, preferred_element_type=F32)
                dsm = (pn * (dpm - delta)).astype(BF16)
                sink_term = psink * delta
                for g in range(GROUP):
                    val = -jnp.sum(sink_term[g * BLK:(g + 1) * BLK])
                    ds_acc = ds_acc + jnp.where((tile_lane == h * GROUP + g) & (tile_row == 0), val, 0.0)
                dq_st = jnp.dot(dsm, kh, preferred_element_type=F32) * (HEAD_DIM ** -0.5)
                dk_h.append(lax.dot_general(dsm, qst, TN_DIMS, preferred_element_type=F32))
                dv_h.append(lax.dot_general(pn.astype(BF16), do_bf, TN_DIMS, preferred_element_type=F32))
                for c2 in range(4):
                    dq_chunks[4 * h + c2] = _rope_bwd(_unstack_heads(dq_st, c2), cc, sc, lm)
            ds_ref[...] += ds_acc
            dk_full = jnp.concatenate(dk_h, axis=1)
            dv_full = jnp.concatenate(dv_h, axis=1)

            @pl.when(i >= 1)
            def _emit():
                dp_ref[:, _QCOL:_QCOL + ATTN_WIDTH] = dq_s[...]
                dp_ref[:, _ZCOL:_ZCOL + ATTN_WIDTH] = dz_s[...]
                dp_ref[:, _KCOL:_KCOL + KV_WIDTH] = _rope_bwd(dk_s[...] + dk_full[:BLK], cp_ref[...], sp_ref[...], lm)
                dp_ref[:, _VCOL:_VCOL + KV_WIDTH] = dv_s[...] + dv_full[:BLK]

            for c in range(8):
                dq_s[:, c * LANES:(c + 1) * LANES] = dq_chunks[c]
                dz_s[:, c * LANES:(c + 1) * LANES] = dz_chunks[c]
            dk_s[...] = dk_full[BLK:]
            dv_s[...] = dv_full[BLK:]

        @pl.when(i == nb)
        def _final():
            dp_ref[:, _QCOL:_QCOL + ATTN_WIDTH] = dq_s[...]
            dp_ref[:, _ZCOL:_ZCOL + ATTN_WIDTH] = dz_s[...]
            dp_ref[:, _KCOL:_KCOL + KV_WIDTH] = _rope_bwd(dk_s[...], cc_ref[...], sc_ref[...], lm)
            dp_ref[:, _VCOL:_VCOL + KV_WIDTH] = dv_s[...]

    cur = lambda b, i: b * nb + jnp.minimum(i, nb - 1)
    prev = lambda b, i: b * nb + jnp.maximum(jnp.minimum(i, nb - 1) - 1, 0)
    emit = lambda b, i: b * nb + jnp.maximum(i - 1, 0)
    hbm = pl.BlockSpec(memory_space=pl.ANY)
    wide = lambda cb: pl.BlockSpec((BLK, ATTN_WIDTH), lambda b, i: (cur(b, i), cb))
    kv = lambda rowf, cb: pl.BlockSpec((BLK, LANES), lambda b, i: (rowf(b, i), cb))
    in_specs = [pl.BlockSpec(memory_space=pltpu.SMEM), wide(0), wide(1),
                kv(cur, _KCOL // LANES), kv(prev, _KCOL // LANES), kv(cur, _VCOL // LANES), kv(prev, _VCOL // LANES),
                kv(cur, 0), kv(cur, 0), kv(prev, 0), kv(prev, 0), wide(0), wide(0)] + [hbm] * nparts
    out_specs = [pl.BlockSpec((BLK, ATTN_IN), lambda b, i: (emit(b, i), 0)),
                 pl.BlockSpec((SUBLANES, LANES), lambda b, i: (0, 0))] + [hbm] * nparts
    return pl.pallas_call(
        body, name="attn_bwd", grid=(b_loc, nb + 1), in_specs=in_specs, out_specs=out_specs,
        out_shape=[jax.ShapeDtypeStruct((n, ATTN_IN), F32), jax.ShapeDtypeStruct((SUBLANES, LANES), F32)] + _scatter_lands(parts),
        scratch_shapes=[pltpu.VMEM((BLK, ATTN_WIDTH), F32), pltpu.VMEM((BLK, ATTN_WIDTH), F32),
                        pltpu.VMEM((BLK, KV_WIDTH), F32), pltpu.VMEM((BLK, KV_WIDTH), F32)] + _scatter_sems(nparts),
        compiler_params=_params(("arbitrary", "arbitrary"), 48),
    )(sinks, p, p, p, p, p, p, ct, st, ct, st, o, dg, *parts)


_CUM_ROWS = 256
_ROWS_OF = [CH - SUBLANES * (s // SUBLANES) for s in range(CH)]
_OFF_OF = [sum(_ROWS_OF[:s]) for s in range(CH)]
_PAIR_ROWS = sum(_ROWS_OF)


def _tri(lower):
    r = lax.broadcasted_iota(jnp.int32, (_CUM_ROWS, _CUM_ROWS), 0)
    c = lax.broadcasted_iota(jnp.int32, (_CUM_ROWS, _CUM_ROWS), 1)
    same = (r ^ c) < CH
    return (same & ((c <= r) if lower else (c >= r))).astype(BF16)


def _gates(qp, fp, lb):
    e = jnp.exp(-jnp.abs(fp))
    log_sig = jnp.minimum(fp, 0.0) - jnp.log(1.0 + e)
    a = jnp.log(lb)
    c = jnp.log(1.0 - lb) + log_sig
    g = jnp.maximum(a, c) + jnp.log(1.0 + jnp.exp(-jnp.abs(a - c)))
    sig_neg = jnp.where(fp >= 0, e, 1.0) / (1.0 + e)
    return qp * _sigmoid(qp), g, (1.0 - lb) * sig_neg, sig_neg


def _pair_rows(bc, s):
    r0 = SUBLANES * (s // SUBLANES)
    return jnp.exp(jnp.minimum(bc[r0:, :] - bc[s:s + 1, :], 0.0))


HP = 4
_HW = HP * REC_K


def _hgrn_specs(tb, nt, reverse):
    tmap = (lambda t: nt - 1 - t) if reverse else (lambda t: t)
    groups = REC_HEADS // HP
    blk = lambda cb: pl.BlockSpec((tb, _HW), lambda h, b, t: (b * nt + tmap(t), cb * groups + h))
    head = pl.BlockSpec((tb, _HW), lambda h, b, t: (b * nt + tmap(t), h))
    lbs = pl.BlockSpec((1, _HW), lambda h, b, t: (0, h))
    gws = pl.BlockSpec((1, REC_K), lambda h, b, t: (0, 0))
    hist = pl.BlockSpec((HP, 1, tb // CH, REC_K, REC_K), lambda h, b, t: (h, b, tmap(t), 0, 0))
    return blk, head, lbs, gws, hist


def _cumsum_chunks(tri, x, out_ref, tb):
    for r in range(0, tb, _CUM_ROWS):
        out_ref[r:r + _CUM_ROWS, :] = _dot3(tri, x[r:r + _CUM_ROWS, :])


def _hgrn_fwd(p, lb, gw, b_loc, t_len):
    n = p.shape[0]
    tb = min(TM, t_len)
    nt, nck = t_len // tb, tb // CH

    def body(qp_ref, fp_ref, i_ref, z_ref, lb_ref, gw_ref, oraw_ref, g_ref, sh_ref, q_s, k_s, b_s, o_s, st_ref):
        @pl.when(pl.program_id(2) == 0)
        def _():
            st_ref[...] = jnp.zeros_like(st_ref)

        qv, g, kk, _ = _gates(qp_ref[...], fp_ref[...], lb_ref[...])
        q_s[...] = qv
        k_s[...] = kk
        _cumsum_chunks(_tri(True), g, b_s, tb)
        ones = jnp.ones((REC_K, REC_K), BF16)
        sub = lax.broadcasted_iota(jnp.int32, (SUBLANES, REC_K), 0)

        def chunk(c, carry):
            for hp in range(HP):
                head_chunk(c, hp)
            return carry

        def head_chunk(c, hp):
            rs = pl.ds(pl.multiple_of(c * CH, CH), CH)
            cs = slice(hp * REC_K, (hp + 1) * REC_K)
            q, k, bc, v = q_s[rs, cs], k_s[rs, cs], b_s[rs, cs], i_ref[rs, cs]
            bl = bc[CH - 1:CH, :]
            st = st_ref[hp]
            sh_ref[hp, 0, c] = st
            o = lax.dot_general((q * jnp.exp(bc)).astype(BF16), st.astype(BF16), NT_DIMS, preferred_element_type=F32)
            w = jnp.concatenate([q[SUBLANES * (s // SUBLANES):, :] * _pair_rows(bc, s) * k[s:s + 1, :] for s in range(CH)], axis=0)
            a = jnp.dot(w.astype(BF16), ones, preferred_element_type=F32)
            acc = [jnp.zeros((SUBLANES, REC_K), F32) for _ in range(CH // SUBLANES)]
            for s in range(CH):
                j = s // SUBLANES
                vs = v[s:s + 1, :]
                for jj in range(j, CH // SUBLANES):
                    blk = a[_OFF_OF[s] + (jj - j) * SUBLANES:_OFF_OF[s] + (jj - j + 1) * SUBLANES, :]
                    if jj == j:
                        blk = jnp.where(sub >= s - j * SUBLANES, blk, 0.0)
                    acc[jj] = acc[jj] + blk * vs
            o_s[rs, cs] = o + jnp.concatenate(acc, axis=0)
            kd = k * jnp.exp(bl - bc)
            st_ref[hp] = st * jnp.exp(bl) + lax.dot_general(v.astype(BF16), kd.astype(BF16), TN_DIMS, preferred_element_type=F32)

        lax.fori_loop(0, nck, chunk, 0)
        oraw_ref[...] = o_s[...]
        for hp in range(HP):
            cs = slice(hp * REC_K, (hp + 1) * REC_K)
            o, zc = o_s[:, cs], z_ref[:, cs]
            on = (o * lax.rsqrt(jnp.mean(o * o, axis=-1, keepdims=True) + NORM_EPS)) * gw_ref[...]
            g_ref[:, cs] = (on * (zc * _sigmoid(zc))).astype(BF16)

    blk, head, lbs, gws, hist = _hgrn_specs(tb, nt, False)
    return pl.pallas_call(
        body, name="hgrn_fwd", grid=(REC_HEADS // HP, b_loc, nt),
        in_specs=[blk(0), blk(1), blk(2), blk(3), lbs, gws], out_specs=[head, head, hist],
        out_shape=[jax.ShapeDtypeStruct((n, 1024), F32), jax.ShapeDtypeStruct((n, 1024), BF16),
                   jax.ShapeDtypeStruct((REC_HEADS, b_loc, t_len // CH, REC_K, REC_K), F32)],
        scratch_shapes=[pltpu.VMEM((tb, _HW), F32)] * 4 + [pltpu.VMEM((HP, REC_K, REC_K), F32)],
        compiler_params=_params(("arbitrary", "arbitrary", "arbitrary"), 48),
    )(p, p, p, p, lb, gw)


def _hgrn_bwd(p, lb, gw, oraw, sh, dg, b_loc, t_len):
    n = p.shape[0]
    tb = min(TM, t_len)
    nt, nck = t_len // tb, tb // CH

    def body(qp_ref, fp_ref, i_ref, z_ref, lb_ref, gw_ref, oraw_ref, dg_ref, sh_ref,
             dq_ref, df_ref, di_ref, dz_ref, dlb_ref, dgw_ref,
             q_s, k_s, b_s, do_s, dqv_s, dk_s, db_s, rowk_s, rowv_s, dst_ref):
        b, t = pl.program_id(1), pl.program_id(2)

        @pl.when(t == 0)
        def _():
            dst_ref[...] = jnp.zeros_like(dst_ref)

        @pl.when((b == 0) & (t == 0))
        def _():
            dlb_ref[...] = jnp.zeros_like(dlb_ref)
            dgw_ref[...] = jnp.zeros_like(dgw_ref)

        lbv, qp, fp = lb_ref[...], qp_ref[...], fp_ref[...]
        qv, g, kk, sig_neg = _gates(qp, fp, lbv)
        q_s[...] = qv
        k_s[...] = kk
        _cumsum_chunks(_tri(True), g, b_s, tb)
        gwv = gw_ref[...]
        for hp in range(HP):
            cs = slice(hp * REC_K, (hp + 1) * REC_K)
            o, zc, dgv = oraw_ref[:, cs], z_ref[:, cs], dg_ref[:, cs]
            rn = lax.rsqrt(jnp.mean(o * o, axis=-1, keepdims=True) + NORM_EPS)
            on = o * rn
            sgz = _sigmoid(zc)
            dz_ref[:, cs] = dgv * (on * gwv) * (sgz * (1.0 + zc * (1.0 - sgz)))
            dpre = dgv * (zc * sgz)
            dgw_ref[hp] += jnp.sum(dpre * on, axis=0, keepdims=True)
            don = dpre * gwv
            do_s[:, cs] = rn * (don - on * jnp.mean(don * on, axis=-1, keepdims=True))

        ones = jnp.ones((REC_K, REC_K), BF16)
        sub = lax.broadcasted_iota(jnp.int32, (SUBLANES, REC_K), 0)
        rowid = lax.broadcasted_iota(jnp.int32, (CH, REC_K), 0)
        ngrp = CH // SUBLANES

        def chunk(ci, carry):
            for hp in range(HP):
                head_chunk(nck - 1 - ci, hp)
            return carry

        def head_chunk(c, hp):
            rs = pl.ds(pl.multiple_of(c * CH, CH), CH)
            cs = slice(hp * REC_K, (hp + 1) * REC_K)
            rowk, rowv = rowk_s.at[hp], rowv_s.at[hp]
            q, k, bc, v, do = q_s[rs, cs], k_s[rs, cs], b_s[rs, cs], i_ref[rs, cs], do_s[rs, cs]
            bl = bc[CH - 1:CH, :]
            st, dst = sh_ref[hp, 0, c], dst_ref[hp]
            eb, ebl, ekd = jnp.exp(bc), jnp.exp(bl), jnp.exp(bl - bc)
            qe, kd = q * eb, k * ekd
            do_bf, dst_bf = do.astype(BF16), dst.astype(BF16)
            dqe = jnp.dot(do_bf, st.astype(BF16), preferred_element_type=F32)
            dkd = jnp.dot(v.astype(BF16), dst_bf, preferred_element_type=F32)
            dv = lax.dot_general(kd.astype(BF16), dst_bf, NT_DIMS, preferred_element_type=F32)
            dbl = jnp.sum(dkd * kd, axis=0, keepdims=True) + ebl * jnp.sum(st * dst, axis=0, keepdims=True)
            dst_ref[hp] = dst * ebl + lax.dot_general(do_bf, qe.astype(BF16), TN_DIMS, preferred_element_type=F32)
            dec = [_pair_rows(bc, s) for s in range(CH)]
            w = jnp.concatenate([q[SUBLANES * (s // SUBLANES):, :] * dec[s] * k[s:s + 1, :] for s in range(CH)], axis=0)
            x = jnp.concatenate([do[SUBLANES * (s // SUBLANES):, :] * v[s:s + 1, :] for s in range(CH)], axis=0)
            a = jnp.dot(w.astype(BF16), ones, preferred_element_type=F32)
            da = jnp.dot(x.astype(BF16), ones, preferred_element_type=F32)
            dq_acc = [jnp.zeros((SUBLANES, REC_K), F32) for _ in range(ngrp)]
            for s in range(CH):
                j = s // SUBLANES
                r0 = j * SUBLANES
                ks = k[s:s + 1, :]
                uk = jnp.zeros((SUBLANES, REC_K), F32)
                uv = jnp.zeros((SUBLANES, REC_K), F32)
                for jj in range(j, ngrp):
                    lo, hi = _OFF_OF[s] + (jj - j) * SUBLANES, _OFF_OF[s] + (jj - j + 1) * SUBLANES
                    a_blk, da_blk = a[lo:hi, :], da[lo:hi, :]
                    if jj == j:
                        keep = sub >= s - r0
                        a_blk, da_blk = jnp.where(keep, a_blk, 0.0), jnp.where(keep, da_blk, 0.0)
                    rows = slice(jj * SUBLANES, (jj + 1) * SUBLANES)
                    tt = da_blk * dec[s][(jj - j) * SUBLANES:(jj - j + 1) * SUBLANES, :]
                    dq_acc[jj] = dq_acc[jj] + tt * ks
                    uk = uk + tt * q[rows, :]
                    uv = uv + a_blk * do[rows, :]
                rowk[s:s + 1, :] = jnp.sum(uk, axis=0, keepdims=True)
                rowv[s:s + 1, :] = jnp.sum(uv, axis=0, keepdims=True)
            dq_in = jnp.concatenate(dq_acc, axis=0)
            dk_in = rowk[...]
            dqv_s[rs, cs] = dqe * eb + dq_in
            dk_s[rs, cs] = dkd * ekd + dk_in
            di_ref[rs, cs] = dv + rowv[...]
            db = dqe * qe - dkd * kd + q * dq_in - k * dk_in
            db_s[rs, cs] = db + jnp.where(rowid == CH - 1, dbl, 0.0)

        lax.fori_loop(0, nck, chunk, 0)
        up = _tri(False)
        sgq = _sigmoid(qp)
        dq_ref[...] = dqv_s[...] * (sgq * (1.0 + qp * (1.0 - sgq)))
        dlb_acc = jnp.zeros((1, _HW), F32)
        for r in range(0, tb, _CUM_ROWS):
            rows = slice(r, r + _CUM_ROWS)
            dgl = _dot3(up, db_s[rows, :])
            dfg = dgl * jnp.exp(-g[rows, :]) - dk_s[rows, :]
            sn = sig_neg[rows, :]
            df_ref[rows, :] = dfg * (1.0 - lbv) * (1.0 - sn) * sn
            dlb_acc = dlb_acc + jnp.sum(dfg * sn, axis=0, keepdims=True)
        dlb_ref[...] += dlb_acc

    blk, head, lbs, gws, hist = _hgrn_specs(tb, nt, True)
    out_specs = [head, head, head, head, lbs, pl.BlockSpec((HP, 1, REC_K), lambda h, b, t: (h, 0, 0))]
    out_shape = [jax.ShapeDtypeStruct((n, 1024), F32)] * 4 + [
        jax.ShapeDtypeStruct((1, 1024), F32), jax.ShapeDtypeStruct((REC_HEADS, 1, REC_K), F32)]
    return pl.pallas_call(
        body, name="hgrn_bwd", grid=(REC_HEADS // HP, b_loc, nt),
        in_specs=[blk(0), blk(1), blk(2), blk(3), lbs, gws, head, head, hist], out_specs=out_specs, out_shape=out_shape,
        scratch_shapes=[pltpu.VMEM((tb, _HW), F32)] * 7 + [pltpu.VMEM((HP, CH, REC_K), F32)] * 2 + [pltpu.VMEM((HP, REC_K, REC_K), F32)],
        compiler_params=_params(("arbitrary", "arbitrary", "arbitrary"), 48),
    )(p, p, p, p, lb, gw, oraw, dg, sh)


def _postnorm_bwd_nt(dxo, y, qw, w, has_bias, name):
    n = dxo.shape[0]

    def body(dx_ref, y_ref, qw_ref, w_ref, dg_ref, dy_ref, dqw_ref, db_ref):
        @pl.when(pl.program_id(0) == 0)
        def _():
            dqw_ref[...] = jnp.zeros_like(dqw_ref)
            db_ref[...] = jnp.zeros_like(db_ref)

        yv, dxv = y_ref[...], dx_ref[...]
        r = lax.rsqrt(jnp.mean(yv * yv, axis=-1, keepdims=True) + NORM_EPS)
        u = yv * r
        du = dxv * qw_ref[...]
        dy = r * (du - u * jnp.mean(du * u, axis=-1, keepdims=True))
        dqw_ref[...] += jnp.sum(dxv * u, axis=0, keepdims=True)
        if has_bias:
            db_ref[...] += jnp.sum(dy, axis=0, keepdims=True)
        dyb = dy.astype(BF16)
        dy_ref[...] = dyb
        dg_ref[...] = lax.dot_general(dyb, w_ref[...], NT_DIMS, preferred_element_type=F32)

    rows = pl.BlockSpec((TM, D_MODEL), lambda i: (i, 0))
    const = lambda shape: pl.BlockSpec(shape, lambda i: (0, 0))
    return pl.pallas_call(
        body, name=name, grid=(n // TM,), in_specs=[rows, rows, const((1, D_MODEL)), const((D_MODEL, D_MODEL))],
        out_specs=[rows, rows, const((1, D_MODEL)), const((1, D_MODEL))],
        out_shape=[jax.ShapeDtypeStruct((n, D_MODEL), F32), jax.ShapeDtypeStruct((n, D_MODEL), BF16),
                   jax.ShapeDtypeStruct((1, D_MODEL), F32), jax.ShapeDtypeStruct((1, D_MODEL), F32)],
        compiler_params=_params(("arbitrary",), 48),
    )(dxo, y, qw, w)


def _nt_prenorm_bwd(dps, w, x, pw, dxo, has_bias, name, parts=()):
    n = x.shape[0]
    widths = [d.shape[1] for d in dps]
    m = sum(widths)
    npieces, nparts, steps = len(dps), len(parts), n // TM

    def body(*refs):
        dp_refs = refs[:npieces]
        w_ref, x_ref, pw_ref, dxo_ref = refs[npieces:npieces + 4]
        part_refs = refs[npieces + 4:npieces + 4 + nparts]
        dx_ref, dpw_ref, db_ref = refs[npieces + 4 + nparts:npieces + 7 + nparts]
        land_refs = refs[npieces + 7 + nparts:npieces + 7 + 2 * nparts]
        sems = refs[npieces + 7 + 2 * nparts:]

        @pl.when(pl.program_id(0) == 0)
        def _():
            dpw_ref[...] = jnp.zeros_like(dpw_ref)
            db_ref[...] = jnp.zeros_like(db_ref)
            if nparts:
                _scatter_start(part_refs, land_refs, sems)

        dh = jnp.zeros((TM, D_MODEL), F32)
        off = 0
        for dp_ref, wd in zip(dp_refs, widths):
            cn = _col_chunk(wd)
            for j in range(0, wd, cn):
                dpc = dp_ref[:, j:j + cn]
                if has_bias:
                    db_ref[:, off + j:off + j + cn] += jnp.sum(dpc, axis=0, keepdims=True)
                dh = dh + lax.dot_general(dpc.astype(BF16), w_ref[:, off + j:off + j + cn], NT_DIMS, preferred_element_type=F32)
            off += wd
        xv = x_ref[...]
        r = lax.rsqrt(jnp.mean(xv * xv, axis=-1, keepdims=True) + NORM_EPS)
        xn = xv * r
        dpw_ref[...] += jnp.sum(dh * xn, axis=0, keepdims=True)
        dxn = dh * pw_ref[...]
        dx_ref[...] = dxo_ref[...] + r * (dxn - xn * jnp.mean(dxn * xn, axis=-1, keepdims=True))

        if nparts:
            @pl.when(pl.program_id(0) == steps - 1)
            def _():
                _scatter_wait(part_refs, land_refs, sems)

    rows = pl.BlockSpec((TM, D_MODEL), lambda i: (i, 0))
    const = lambda shape: pl.BlockSpec(shape, lambda i: (0, 0))
    hbm = pl.BlockSpec(memory_space=pl.ANY)
    in_specs = ([pl.BlockSpec((TM, wd), lambda i: (i, 0)) for wd in widths] + [const((D_MODEL, m)), rows, const((1, D_MODEL)), rows]
                + [hbm] * nparts)
    return pl.pallas_call(
        body, name=name, grid=(steps,), in_specs=in_specs,
        out_specs=[rows, const((1, D_MODEL)), const((1, m))] + [hbm] * nparts,
        out_shape=[jax.ShapeDtypeStruct((n, D_MODEL), F32), jax.ShapeDtypeStruct((1, D_MODEL), F32),
                   jax.ShapeDtypeStruct((1, m), F32)] + _scatter_lands(parts),
        scratch_shapes=_scatter_sems(nparts) if nparts else [],
        compiler_params=_params(("arbitrary",), 56),
    )(*dps, w, x, pw, dxo, *parts)


def _matmul_tn(a, b, name):
    n, k = a.shape
    m = b.shape[1]
    tk, tm, tn = 512, _col_chunk(m), 1024 if n % 1024 == 0 else n

    def body(a_ref, b_ref, o_ref):
        @pl.when(pl.program_id(2) == 0)
        def _():
            o_ref[...] = jnp.zeros_like(o_ref)

        o_ref[...] += lax.dot_general(a_ref[...], b_ref[...].astype(BF16), TN_DIMS, preferred_element_type=F32)

    return pl.pallas_call(
        body, name=name, grid=(k // tk, m // tm, n // tn),
        in_specs=[pl.BlockSpec((tn, tk), lambda i, j, l: (l, i)), pl.BlockSpec((tn, tm), lambda i, j, l: (l, j))],
        out_specs=pl.BlockSpec((tk, tm), lambda i, j, l: (i, j)),
        out_shape=jax.ShapeDtypeStruct((k, m), F32),
        compiler_params=_params(("arbitrary", "arbitrary", "arbitrary"), 48),
    )(a, b)


def _by_owner_cols(dw):
    k, m = dw.shape
    return dw.reshape(k, N_DEV, m // N_DEV).transpose(1, 0, 2)


def _own_and_bf16(part):
    return lax.dynamic_index_in_dim(part, _my_id(), 0, keepdims=False), part.astype(BF16)


def _step(x, pos_col, tgt, pre_w, post_w, wa_in, ba_in, sinks, wa_out_shard, ba_out, wr_in_shard, lb_logits, gnorm_w, wr_out_shard, b_loc, t_len):
    nb = t_len // BLK
    ct, st = _rope_tables(pos_col)
    lb = _lower_bound(lb_logits)
    p0, h0, ga_out = _norm_matmul(x, pre_w[0:1], wa_in, ba_in, "attn_in_proj", [wa_out_shard])
    wa_out = ga_out.reshape(ATTN_WIDTH, D_MODEL)
    o0, g0, gr_in, gr_out = _attn_fwd(p0, ct, st, sinks, b_loc, nb, [wr_in_shard, wr_out_shard])
    wr_in = gr_in.transpose(1, 0, 2).reshape(D_MODEL, REC_IN)
    wr_out = gr_out.reshape(1024, D_MODEL)
    y0, x1 = _outproj_postnorm(g0, wa_out, ba_out, x, post_w[0:1], None, "attn_out_proj")
    p1, h1 = _norm_matmul(x1, pre_w[1:2], wr_in, None, "rec_in_proj")
    o1, g1, sh = _hgrn_fwd(p1, lb, gnorm_w, b_loc, t_len)
    y1, dx2, loss_tile = _outproj_postnorm(g1, wr_out, None, x1, post_w[1:2], tgt, "rec_out_proj_loss")
    dg1, dy1, dpost1, _ = _postnorm_bwd_nt(dx2, y1, post_w[1:2], wr_out, False, "rec_out_bwd")
    d_wr_out = _matmul_tn(g1, dy1, "rec_w_out_grad")
    dq1, df1, di1, dz1, dlb, dgw = _hgrn_bwd(p1, lb, gnorm_w, o1, sh, dg1, b_loc, t_len)
    dps1 = [dq1, df1, di1, dz1]
    dx1, dpre1, _ = _nt_prenorm_bwd(dps1, wr_in, x1, pre_w[1:2], dx2, False, "rec_in_bwd")
    d_wr_in = [_matmul_tn(h1, dpk, f"rec_w_in_grad_{k}") for k, dpk in enumerate(dps1)]
    dg0, dy0, dpost0, dba_out = _postnorm_bwd_nt(dx1, y0, post_w[0:1], wa_out, True, "attn_out_bwd")
    d_wa_out = _matmul_tn(g0, dy0, "attn_w_out_grad")
    owns, wires = zip(*[_own_and_bf16(part) for part in (
        _by_owner_cols(jnp.concatenate(d_wr_in, axis=1)), d_wr_out.reshape(N_DEV, 1024 // N_DEV, D_MODEL),
        d_wa_out.reshape(N_DEV, ATTN_WIDTH // N_DEV, D_MODEL))])
    dp0, dsink_tile, *lands = _attn_bwd(p0, ct, st, sinks, o0, dg0, b_loc, nb, list(wires))
    d_wa_in = _matmul_tn(h0, dp0, "attn_w_in_grad")
    own_a_in, wire_a_in = _own_and_bf16(_by_owner_cols(_qkvz(d_wa_in)))
    dx0, dpre0, dba_in, land_a_in = _nt_prenorm_bwd([dp0], wa_in, x, pre_w[0:1], dx1, True, "attn_in_bwd", [wire_a_in])
    small = dict(pre=jnp.concatenate([dpre0, dpre1], axis=0), post=jnp.concatenate([dpost0, dpost1], axis=0),
                 ba_in=dba_in, sinks=dsink_tile[0:1, 0:N_HEADS], ba_out=dba_out, lb=dlb, gnorm=jnp.sum(dgw, axis=0))
    return loss_tile, dx0, list(zip(lands, owns)) + [(land_a_in, own_a_in)], small


def _my_id():
    return lax.axis_index("x") * 4 + lax.axis_index("y") * 2 + lax.axis_index("c")


def _peer(k):
    x, y, c = lax.axis_index("x"), lax.axis_index("y"), lax.axis_index("c")
    return (x ^ ((k >> 2) & 1), y ^ ((k >> 1) & 1), c ^ (k & 1))


def _peer_id(k):
    return _my_id() ^ k


def _all_gather(shards):
    nsh = len(shards)

    def body(*refs):
        ins, outs, sems = refs[:nsh], refs[nsh:2 * nsh], refs[2 * nsh:]
        _gather_start(ins, outs, sems)
        _gather_wait(ins, outs, sems)

    hbm = pl.BlockSpec(memory_space=pl.ANY)
    return pl.pallas_call(
        body, name="comm_all_gather", in_specs=[hbm] * nsh, out_specs=[hbm] * nsh,
        out_shape=_gather_shapes(shards), scratch_shapes=_gather_sems(nsh),
    )(*shards)


def _gather_shapes(shards):
    return [jax.ShapeDtypeStruct((N_DEV,) + s.shape, s.dtype) for s in shards]


def _gather_sems(nsh):
    return [pltpu.SemaphoreType.DMA((nsh, N_DEV - 1)), pltpu.SemaphoreType.DMA((nsh, N_DEV - 1)), pltpu.SemaphoreType.DMA((nsh,))]


def _gather_copies(ins, outs, sems):
    send_sems, recv_sems, local_sems = sems
    me = _my_id()
    local, sends, recvs = [], [], []
    for a in range(len(ins)):
        local.append(pltpu.make_async_copy(ins[a], outs[a].at[me], local_sems.at[a]))
        for k in range(1, N_DEV):
            for row, group in ((me, sends), (_peer_id(k), recvs)):
                group.append(pltpu.make_async_remote_copy(
                    src_ref=ins[a], dst_ref=outs[a].at[row], send_sem=send_sems.at[a, k - 1], recv_sem=recv_sems.at[a, k - 1],
                    device_id=_peer(k), device_id_type=MESH))
    return local, sends, recvs


def _gather_start(ins, outs, sems):
    local, sends, _ = _gather_copies(ins, outs, sems)
    for cp in local + sends:
        cp.start()


def _gather_wait(ins, outs, sems):
    local, sends, recvs = _gather_copies(ins, outs, sems)
    for cp in recvs:
        cp.wait_recv()
    for cp in sends:
        cp.wait_send()
    for cp in local:
        cp.wait()


def _scatter_lands(parts):
    return [jax.ShapeDtypeStruct((N_DEV - 1,) + p.shape[1:], p.dtype) for p in parts]


def _scatter_sems(nparts):
    return [pltpu.SemaphoreType.DMA((nparts, N_DEV - 1)), pltpu.SemaphoreType.DMA((nparts, N_DEV - 1))]


def _scatter_copies(parts, lands, sems):
    send_sems, recv_sems = sems
    return [pltpu.make_async_remote_copy(
        src_ref=parts[a].at[_peer_id(k)], dst_ref=lands[a].at[k - 1], send_sem=send_sems.at[a, k - 1],
        recv_sem=recv_sems.at[a, k - 1], device_id=_peer(k), device_id_type=MESH)
        for a in range(len(parts)) for k in range(1, N_DEV)]


def _scatter_start(parts, lands, sems):
    for cp in _scatter_copies(parts, lands, sems):
        cp.start()


def _scatter_wait(parts, lands, sems):
    copies = _scatter_copies(parts, lands, sems)
    for cp in copies:
        cp.wait_recv()
    for cp in copies:
        cp.wait_send()


def _adamw(w, g, m, v):
    m2 = ADAM_B1 * m + (1.0 - ADAM_B1) * g
    v2 = ADAM_B2 * v + (1.0 - ADAM_B2) * (g * g)
    m_hat = m2 / (1.0 - ADAM_B1 ** ADAM_STEP)
    v_hat = v2 / (1.0 - ADAM_B2 ** ADAM_STEP)
    delta = -ADAM_LR * (m_hat / (jnp.sqrt(v_hat) + ADAM_EPS) + ADAM_WD * w)
    return delta, m2, v2


def _sum_adamw_rows(land_ref, own_ref, w_ref, m_ref, v_ref, out_refs):
    r, c = own_ref.shape
    rc = 64 if r % 64 == 0 else r
    me = _my_id()
    g_ref, d_ref, m2_ref, v2_ref = out_refs

    def rows(i, carry):
        rs = pl.ds(pl.multiple_of(i * rc, rc), rc)
        g = jnp.zeros((rc, c), F32)
        for dev in range(N_DEV):
            k = dev ^ me
            g = g + jnp.where(k == 0, own_ref[rs, :], land_ref[jnp.maximum(k - 1, 0), rs, :].astype(F32))
        delta, m2, v2 = _adamw(w_ref[rs, :], g, m_ref[rs, :], v_ref[rs, :])
        g_ref[rs, :] = g
        d_ref[rs, :] = delta
        m2_ref[rs, :] = m2
        v2_ref[rs, :] = v2
        return carry

    lax.fori_loop(0, r // rc, rows, 0)


def _sum_adamw(land, own, w, m, v, name):
    r, c = own.shape

    def body(land_ref, own_ref, w_ref, m_ref, v_ref, g_ref, d_ref, m2_ref, v2_ref):
        _sum_adamw_rows(land_ref, own_ref, w_ref, m_ref, v_ref, (g_ref, d_ref, m2_ref, v2_ref))

    vmem = pl.BlockSpec(memory_space=pltpu.VMEM)
    return pl.pallas_call(
        body, name=name, in_specs=[vmem] * 5, out_specs=[vmem] * 4, out_shape=[jax.ShapeDtypeStruct((r, c), F32)] * 4,
        compiler_params=_params(None, 56),
    )(land, own, w, m, v)


_SMALL = [("pre_norm_w", 2048), ("post_norm_w", 2048), ("attn_b_in", 2304), ("attn_sinks", 16), ("attn_b_out", 1024),
          ("rec_lb_logits", 2048), ("rec_gnorm_w", 128)]
_TILE = SUBLANES * LANES


def _small_rows(size):
    return -(-size // _TILE) * SUBLANES


_SMALL_OFF = {}
_r = 0
for _name, _size in _SMALL:
    _SMALL_OFF[_name] = _r
    _r += _small_rows(_size)
_SMALL_ROWS = _r


def _pack_small(pieces):
    out = []
    for name, size in _SMALL:
        flat = pieces[name].reshape(-1).astype(F32)
        out.append(jnp.pad(flat, (0, _small_rows(size) * LANES - size)).reshape(-1, LANES))
    return jnp.concatenate(out, axis=0)


def _unpack_small(packed, shapes):
    return {name: packed[_SMALL_OFF[name]:_SMALL_OFF[name] + _small_rows(size)].reshape(-1)[:size].reshape(shapes[name])
            for name, size in _SMALL}


def _small_allreduce_adamw(gpart, w, m, v):
    lb0 = _SMALL_OFF["rec_lb_logits"]

    def body(gp_ref, w_ref, m_ref, v_ref, g_ref, d_ref, m2_ref, v2_ref, land_ref, send_sems, recv_sems):
        me = _my_id()
        sent = []
        for k in range(1, N_DEV):
            cp = pltpu.make_async_remote_copy(src_ref=gp_ref, dst_ref=land_ref.at[k - 1], send_sem=send_sems.at[k - 1],
                                              recv_sem=recv_sems.at[k - 1], device_id=_peer(k), device_id_type=MESH)
            cp.start()
            sent.append(cp)
        for cp in sent:
            cp.wait_recv()
        for cp in sent:
            cp.wait_send()
        g = jnp.zeros((_SMALL_ROWS, LANES), F32)
        for dev in range(N_DEV):
            k = dev ^ me
            g = g + jnp.where(k == 0, gp_ref[...], land_ref[jnp.maximum(k - 1, 0)])
        g_ref[...] = g
        l0, l1 = w_ref[lb0:lb0 + SUBLANES, :], w_ref[lb0 + SUBLANES:lb0 + 2 * SUBLANES, :]
        mx = jnp.maximum(l0, l1)
        e0, e1 = jnp.exp(l0 - mx), jnp.exp(l1 - mx)
        p1 = e1 / (e0 + e1)
        dl1 = (1.0 - p1) * p1 * g[lb0:lb0 + SUBLANES, :]
        g_ref[lb0:lb0 + SUBLANES, :] = -dl1
        g_ref[lb0 + SUBLANES:lb0 + 2 * SUBLANES, :] = dl1
        delta, m2, v2 = _adamw(w_ref[...], g_ref[...], m_ref[...], v_ref[...])
        d_ref[...] = delta
        m2_ref[...] = m2
        v2_ref[...] = v2

    vmem = pl.BlockSpec(memory_space=pltpu.VMEM)
    return pl.pallas_call(
        body, name="comm_small_allreduce_adamw", in_specs=[vmem] * 4, out_specs=[vmem] * 4,
        out_shape=[jax.ShapeDtypeStruct((_SMALL_ROWS, LANES), F32)] * 4,
        scratch_shapes=[pltpu.VMEM((N_DEV - 1, _SMALL_ROWS, LANES), F32), pltpu.SemaphoreType.DMA((N_DEV - 1,)),
                        pltpu.SemaphoreType.DMA((N_DEV - 1,))],
    )(gpart, w, m, v)


def _qzkv(a):
    return jnp.concatenate([a[..., :1024], a[..., 1280:], a[..., 1024:1280]], axis=-1)


def _qkvz(a):
    return jnp.concatenate([a[..., :1024], a[..., 2048:], a[..., 1024:2048]], axis=-1)


def kernel(x, positions, pre_norm_w, post_norm_w, attn_w_in, attn_b_in, attn_sinks, attn_w_out, attn_b_out, rec_w_in, rec_lb_logits, rec_gnorm_w, rec_w_out, loss_target, m_pre_norm_w, m_post_norm_w, m_attn_w_in, m_attn_b_in, m_attn_sinks, m_attn_w_out, m_attn_b_out, m_rec_w_in, m_rec_lb_logits, m_rec_gnorm_w, m_rec_w_out, v_pre_norm_w, v_post_norm_w, v_attn_w_in, v_attn_b_in, v_attn_sinks, v_attn_w_out, v_attn_b_out, v_rec_w_in, v_rec_lb_logits, v_rec_gnorm_w, v_rec_w_out):
    b_loc, t_len, _ = x.shape
    n = b_loc * t_len
    ga_in, = _all_gather([attn_w_in[0].astype(BF16)])
    wa_in = _qzkv(ga_in.transpose(1, 0, 2).reshape(D_MODEL, ATTN_IN))

    loss_tile, dx, landed, small = _step(
        x.reshape(n, D_MODEL), positions.reshape(n, 1).astype(F32), loss_target.reshape(n, D_MODEL),
        pre_norm_w, post_norm_w, wa_in, _qzkv(attn_b_in), attn_sinks, attn_w_out[0].astype(BF16), attn_b_out,
        rec_w_in[0].astype(BF16), rec_lb_logits, rec_gnorm_w, rec_w_out[0].astype(BF16), b_loc, t_len)
    loss = lax.psum(loss_tile[0, 0], ("x", "y", "c"))

    lift = lambda outs: tuple(a[None] for a in outs)
    (l_r_in, o_r_in), (l_r_out, o_r_out), (l_a_out, o_a_out), (l_a_in, o_a_in) = landed
    r_a_in = lift(_sum_adamw(l_a_in, o_a_in, attn_w_in[0], m_attn_w_in[0], v_attn_w_in[0], "adamw_attn_w_in"))
    r_r_in = lift(_sum_adamw(l_r_in, o_r_in, rec_w_in[0], m_rec_w_in[0], v_rec_w_in[0], "adamw_rec_w_in"))
    r_r_out = lift(_sum_adamw(l_r_out, o_r_out, rec_w_out[0], m_rec_w_out[0], v_rec_w_out[0], "adamw_rec_w_out"))
    r_a_out = lift(_sum_adamw(l_a_out, o_a_out, attn_w_out[0], m_attn_w_out[0], v_attn_w_out[0], "adamw_attn_w_out"))

    gsmall = dict(pre_norm_w=small["pre"], post_norm_w=small["post"], attn_b_in=_qkvz(small["ba_in"]), attn_sinks=small["sinks"],
                  attn_b_out=small["ba_out"], rec_lb_logits=jnp.concatenate([small["lb"], jnp.zeros_like(small["lb"])], axis=0),
                  rec_gnorm_w=small["gnorm"])
    wsmall = dict(pre_norm_w=pre_norm_w, post_norm_w=post_norm_w, attn_b_in=attn_b_in, attn_sinks=attn_sinks,
                  attn_b_out=attn_b_out, rec_lb_logits=rec_lb_logits, rec_gnorm_w=rec_gnorm_w)
    msmall = dict(pre_norm_w=m_pre_norm_w, post_norm_w=m_post_norm_w, attn_b_in=m_attn_b_in, attn_sinks=m_attn_sinks,
                  attn_b_out=m_attn_b_out, rec_lb_logits=m_rec_lb_logits, rec_gnorm_w=m_rec_gnorm_w)
    vsmall = dict(pre_norm_w=v_pre_norm_w, post_norm_w=v_post_norm_w, attn_b_in=v_attn_b_in, attn_sinks=v_attn_sinks,
                  attn_b_out=v_attn_b_out, rec_lb_logits=v_rec_lb_logits, rec_gnorm_w=v_rec_gnorm_w)
    shapes = {k: a.shape for k, a in wsmall.items()}
    packed = _small_allreduce_adamw(_pack_small(gsmall), _pack_small(wsmall), _pack_small(msmall), _pack_small(vsmall))
    sg, sd, sm, sv = [_unpack_small(a, shapes) for a in packed]

    big = {"attn_w_in": r_a_in, "attn_w_out": r_a_out, "rec_w_in": r_r_in, "rec_w_out": r_r_out}
    order = ["pre_norm_w", "post_norm_w", "attn_w_in", "attn_b_in", "attn_sinks", "attn_w_out", "attn_b_out", "rec_w_in",
             "rec_lb_logits", "rec_gnorm_w", "rec_w_out"]
    outs = [loss, dx.reshape(b_loc, t_len, D_MODEL)]
    for idx, small_set in enumerate((sg, sd, sm, sv)):
        outs += [big[nm][idx] if nm in big else small_set[nm] for nm in order]
    return tuple(outs)
```

```python
import numpy as np
import jax
import jax.numpy as jnp
from jax import lax
from jax.experimental import pallas as pl
from jax.experimental.pallas import tpu as pltpu

F32, BF16 = jnp.float32, jnp.bfloat16
MESH = pl.DeviceIdType.MESH
N_DEV = 8

D_MODEL = 1024
N_HEADS, HEAD_DIM, N_KV, GROUP = 16, 64, 2, 8
ATTN_WIDTH, KV_WIDTH = 1024, 128
ATTN_IN = 2 * ATTN_WIDTH + 2 * KV_WIDTH
BLK = 128
ROPE_THETA, ROPE_HALF = 500000.0, 8
REC_HEADS, REC_K = 8, 128
REC_IN = 4 * 1024
CH = 32
NORM_EPS = 1e-6
ADAM_LR, ADAM_B1, ADAM_B2, ADAM_EPS, ADAM_WD, ADAM_STEP = 0.001, 0.9, 0.999, 1e-08, 0.01, 10

LANES, SUBLANES = 128, 8
TM = 512
NT_DIMS = (((1,), (1,)), ((), ()))
TN_DIMS = (((0,), (0,)), ((), ()))
MB = 2 ** 20


def _params(sem=None, vmem_mb=48, **kw):
    return pltpu.CompilerParams(dimension_semantics=sem, vmem_limit_bytes=vmem_mb * MB, **kw)


def _col_chunk(m):
    return 768 if m % 1024 else 1024


def _sigmoid(x):
    return 1.0 / (1.0 + jnp.exp(-x))


def _split3(x):
    hi = x.astype(BF16)
    r1 = x - hi.astype(F32)
    mid = r1.astype(BF16)
    lo = (r1 - mid.astype(F32)).astype(BF16)
    return hi, mid, lo


def _dot3(l_bf, x):
    hi, mid, lo = _split3(x)
    return (jnp.dot(l_bf, hi, preferred_element_type=F32) + jnp.dot(l_bf, mid, preferred_element_type=F32)
            + jnp.dot(l_bf, lo, preferred_element_type=F32))


def _rope_tables(pos_col):
    n = pos_col.shape[0]
    lane = np.arange(LANES) % HEAD_DIM
    inv = np.float32(ROPE_THETA) ** (-(np.arange(ROPE_HALF, dtype=np.float32) * np.float32(2.0) / np.float32(2 * ROPE_HALF)))
    freq = np.where(lane < 2 * ROPE_HALF, inv[lane % ROPE_HALF], 0.0).astype(np.float32)[None, :]
    sign = np.where(lane < ROPE_HALF, -1.0, np.where(lane < 2 * ROPE_HALF, 1.0, 0.0)).astype(np.float32)[None, :]

    def body(p_ref, f_ref, s_ref, c_out, s_out):
        ang = p_ref[...] * f_ref[...]
        c_out[...] = jnp.cos(ang)
        s_out[...] = jnp.sin(ang) * s_ref[...]

    row = pl.BlockSpec((TM, 1), lambda i: (i, 0))
    vec = pl.BlockSpec((1, LANES), lambda i: (0, 0))
    out = pl.BlockSpec((TM, LANES), lambda i: (i, 0))
    return pl.pallas_call(
        body, name="rope_tables", grid=(n // TM,), in_specs=[row, vec, vec], out_specs=[out, out],
        out_shape=[jax.ShapeDtypeStruct((n, LANES), F32)] * 2, compiler_params=_params(("arbitrary",)),
    )(pos_col, jnp.asarray(freq), jnp.asarray(sign))


def _rope_apply(xv, c, s, lm):
    partner = jnp.where(lm < ROPE_HALF, pltpu.roll(xv, LANES - ROPE_HALF, 1), pltpu.roll(xv, ROPE_HALF, 1))
    return xv * c + partner * s


def _rope_bwd(dy, c, s, lm):
    t = dy * s
    partner = jnp.where(lm < ROPE_HALF, pltpu.roll(t, LANES - ROPE_HALF, 1),
                        jnp.where(lm < 2 * ROPE_HALF, pltpu.roll(t, ROPE_HALF, 1), 0.0))
    return dy * c + partner


def _lower_bound(lb_logits):
    def body(l_ref, o_ref):
        l0, l1 = l_ref[0:1, :], l_ref[1:2, :]
        m = jnp.maximum(l0, l1)
        e0, e1 = jnp.exp(l0 - m), jnp.exp(l1 - m)
        o_ref[...] = e1 / (e0 + e1)

    return pl.pallas_call(body, name="lower_bound", out_shape=jax.ShapeDtypeStruct((1, lb_logits.shape[1]), F32))(lb_logits)


def _norm_matmul(x, pw, w, bias, name, shards=()):
    n, m = x.shape[0], w.shape[1]
    cn = _col_chunk(m)
    has_bias = bias is not None
    nsh, steps = len(shards), n // TM

    def body(*refs):
        refs = list(refs)
        x_ref, pw_ref, w_ref = refs[:3]
        b_ref = refs[3] if has_bias else None
        refs = refs[4 if has_bias else 3:]
        sh_in, (p_ref, h_ref), sh_out, sems = refs[:nsh], refs[nsh:nsh + 2], refs[nsh + 2:2 * nsh + 2], refs[2 * nsh + 2:]
        if nsh:
            @pl.when(pl.program_id(0) == 0)
            def _():
                _gather_start(sh_in, sh_out, sems)

        xv = x_ref[...]
        r = lax.rsqrt(jnp.mean(xv * xv, axis=-1, keepdims=True) + NORM_EPS)
        h = ((xv * r) * pw_ref[...]).astype(BF16)
        h_ref[...] = h
        for j in range(0, m, cn):
            acc = jnp.dot(h, w_ref[:, j:j + cn], preferred_element_type=F32)
            if has_bias:
                acc = acc + b_ref[:, j:j + cn]
            p_ref[:, j:j + cn] = acc

        if nsh:
            @pl.when(pl.program_id(0) == steps - 1)
            def _():
                _gather_wait(sh_in, sh_out, sems)

    rows = pl.BlockSpec((TM, D_MODEL), lambda i: (i, 0))
    const = lambda shape: pl.BlockSpec(shape, lambda i: (0, 0))
    hbm = pl.BlockSpec(memory_space=pl.ANY)
    in_specs = [rows, const((1, D_MODEL)), const((D_MODEL, m))] + ([const((1, m))] if has_bias else []) + [hbm] * nsh
    args = (x, pw, w) + ((bias,) if has_bias else ()) + tuple(shards)
    return pl.pallas_call(
        body, name=name, grid=(steps,), in_specs=in_specs,
        out_specs=[pl.BlockSpec((TM, m), lambda i: (i, 0)), rows] + [hbm] * nsh,
        out_shape=[jax.ShapeDtypeStruct((n, m), F32), jax.ShapeDtypeStruct((n, D_MODEL), BF16)] + _gather_shapes(shards),
        scratch_shapes=_gather_sems(nsh) if nsh else [],
        compiler_params=_params(("arbitrary",), 56),
    )(*args)


def _outproj_postnorm(g, w, bias, xres, qw, tgt, name):
    n = g.shape[0]
    has_bias, has_loss = bias is not None, tgt is not None
    steps = n // TM

    def body(*refs):
        refs = list(refs)
        g_ref, w_ref = refs.pop(0), refs.pop(0)
        b_ref = refs.pop(0) if has_bias else None
        x_ref, qw_ref = refs.pop(0), refs.pop(0)
        t_ref = refs.pop(0) if has_loss else None
        y_ref, o_ref = refs.pop(0), refs.pop(0)
        y = jnp.dot(g_ref[...], w_ref[...], preferred_element_type=F32)
        if has_bias:
            y = y + b_ref[...]
        y_ref[...] = y
        r = lax.rsqrt(jnp.mean(y * y, axis=-1, keepdims=True) + NORM_EPS)
        xn = x_ref[...] + (y * r) * qw_ref[...]
        if not has_loss:
            o_ref[...] = xn
        else:
            loss_ref, acc_ref = refs
            i = pl.program_id(0)
            e = xn - t_ref[...]
            o_ref[...] = e * (1.0 / D_MODEL)

            @pl.when(i == 0)
            def _():
                acc_ref[...] = jnp.zeros_like(acc_ref)

            acc_ref[...] += jnp.sum(e * e, axis=0, keepdims=True)

            @pl.when(i == steps - 1)
            def _():
                loss_ref[...] = jnp.full(loss_ref.shape, jnp.sum(acc_ref[...]) * (0.5 / D_MODEL), F32)

    rows = pl.BlockSpec((TM, D_MODEL), lambda i: (i, 0))
    const = lambda shape: pl.BlockSpec(shape, lambda i: (0, 0))
    in_specs = [rows, const((D_MODEL, D_MODEL))] + ([const((1, D_MODEL))] if has_bias else []) + [rows, const((1, D_MODEL))]
    args = [g, w] + ([bias] if has_bias else []) + [xres, qw]
    out_specs = [rows, rows]
    out_shape = [jax.ShapeDtypeStruct((n, D_MODEL), F32)] * 2
    scratch = []
    if has_loss:
        in_specs.append(rows)
        args.append(tgt)
        out_specs.append(const((SUBLANES, LANES)))
        out_shape.append(jax.ShapeDtypeStruct((SUBLANES, LANES), F32))
        scratch = [pltpu.VMEM((1, D_MODEL), F32)]
    return pl.pallas_call(
        body, name=name, grid=(steps,), in_specs=in_specs, out_specs=out_specs, out_shape=out_shape,
        scratch_shapes=scratch, compiler_params=_params(("arbitrary",), 48),
    )(*args)


_QCOL, _ZCOL, _KCOL, _VCOL = 0, 1024, 2048, 2176


def _head_stack(chunks, h, lt64):
    return jnp.concatenate(
        [jnp.where(lt64 if g % 2 == 0 else ~lt64, chunks[4 * h + g // 2], 0.0) for g in range(GROUP)], axis=0)


def _dup_half(x, h, lt64):
    r = pltpu.roll(x, HEAD_DIM, 1)
    return jnp.where(lt64, x, r) if h == 0 else jnp.where(lt64, r, x)


def _pair_chunk(xt, c2):
    a, b = 2 * c2, 2 * c2 + 1
    return jnp.concatenate([xt[:HEAD_DIM, a * BLK:(a + 1) * BLK], xt[HEAD_DIM:, b * BLK:(b + 1) * BLK]], axis=0).T


def _attn_mask_t(i):
    key = lax.broadcasted_iota(jnp.int32, (2 * BLK, BLK), 0)
    qry = lax.broadcasted_iota(jnp.int32, (2 * BLK, BLK), 1)
    valid = (key > qry) & (key <= qry + BLK) & ((key >= BLK) | (i > 0))
    return jnp.tile(jnp.where(valid, 0.0, -1e30), (1, GROUP))


def _attn_probs_t(qst, kext, h, sink_ref, mask):
    s = lax.dot_general(kext, qst, NT_DIMS, preferred_element_type=F32) + mask
    head = lax.broadcasted_iota(jnp.int32, (1, GROUP * BLK), 1) >> 7
    sk = jnp.zeros((1, GROUP * BLK), F32)
    for g in range(GROUP):
        sk = jnp.where(head == g, sink_ref[0, h * GROUP + g], sk)
    m = jnp.maximum(jnp.max(s, axis=0, keepdims=True), sk)
    p = jnp.exp(s - m)
    esk = jnp.exp(sk - m)
    inv = 1.0 / (jnp.sum(p, axis=0, keepdims=True) + esk)
    return p * inv, esk * inv


def _attn_fwd(p, ct, st, sinks, b_loc, nb, shards):
    n = p.shape[0]
    nsh = len(shards)

    def body(sink_ref, q_ref, z_ref, kc_ref, kp_ref, vc_ref, vp_ref, cc_ref, sc_ref, cp_ref, sp_ref, *rest):
        sh_in, (o_ref, g_ref), sh_out, sems = rest[:nsh], rest[nsh:nsh + 2], rest[nsh + 2:2 * nsh + 2], rest[2 * nsh + 2:]
        b, i = pl.program_id(0), pl.program_id(1)

        @pl.when((b == 0) & (i == 0))
        def _():
            _gather_start(sh_in, sh_out, sems)

        lane = lax.broadcasted_iota(jnp.int32, (BLK, LANES), 1)
        lm = lane & (HEAD_DIM - 1)
        cc, sc = cc_ref[...], sc_ref[...]
        kcat = jnp.concatenate([_rope_apply(kp_ref[...], cp_ref[...], sp_ref[...], lm),
                                _rope_apply(kc_ref[...], cc, sc, lm)], axis=0)
        vcat = jnp.concatenate([vp_ref[...], vc_ref[...]], axis=0)
        qr = [_rope_apply(q_ref[:, c * LANES:(c + 1) * LANES], cc, sc, lm) * (HEAD_DIM ** -0.5) for c in range(8)]
        valid = _attn_mask_t(i)
        lt64, lt64k = lane < HEAD_DIM, lax.broadcasted_iota(jnp.int32, (2 * BLK, LANES), 1) < HEAD_DIM
        for h in range(N_KV):
            qst = _head_stack(qr, h, lt64).astype(BF16)
            pn, _ = _attn_probs_t(qst, _dup_half(kcat, h, lt64k).astype(BF16), h, sink_ref, valid)
            ot = lax.dot_general(_dup_half(vcat, h, lt64k).astype(BF16), pn.astype(BF16), TN_DIMS, preferred_element_type=F32)
            for c2 in range(4):
                oc = _pair_chunk(ot, c2)
                cols = slice((4 * h + c2) * LANES, (4 * h + c2 + 1) * LANES)
                zc = z_ref[:, cols]
                o_ref[:, cols] = oc
                g_ref[:, cols] = (oc * (zc * _sigmoid(zc))).astype(BF16)

        @pl.when((b == b_loc - 1) & (i == nb - 1))
        def _():
            _gather_wait(sh_in, sh_out, sems)

    cur = lambda b, i: b * nb + i
    prev = lambda b, i: b * nb + jnp.maximum(i - 1, 0)
    wide = lambda cb: pl.BlockSpec((BLK, ATTN_WIDTH), lambda b, i: (cur(b, i), cb))
    kv = lambda rowf, cb: pl.BlockSpec((BLK, LANES), lambda b, i: (rowf(b, i), cb))
    hbm = pl.BlockSpec(memory_space=pl.ANY)
    in_specs = [pl.BlockSpec(memory_space=pltpu.SMEM), wide(0), wide(1),
                kv(cur, _KCOL // LANES), kv(prev, _KCOL // LANES), kv(cur, _VCOL // LANES), kv(prev, _VCOL // LANES),
                kv(cur, 0), kv(cur, 0), kv(prev, 0), kv(prev, 0)] + [hbm] * nsh
    return pl.pallas_call(
        body, name="attn_fwd", grid=(b_loc, nb), in_specs=in_specs, out_specs=[wide(0), wide(0)] + [hbm] * nsh,
        out_shape=[jax.ShapeDtypeStruct((n, ATTN_WIDTH), F32), jax.ShapeDtypeStruct((n, ATTN_WIDTH), BF16)] + _gather_shapes(shards),
        scratch_shapes=_gather_sems(nsh), compiler_params=_params(("arbitrary", "arbitrary"), 48),
    )(sinks, p, p, p, p, p, p, ct, st, ct, st, *shards)


def _attn_bwd(p, ct, st, sinks, o, dg, b_loc, nb, parts):
    n = p.shape[0]
    nparts = len(parts)

    def body(sink_ref, q_ref, z_ref, kc_ref, kp_ref, vc_ref, vp_ref, cc_ref, sc_ref, cp_ref, sp_ref, o_ref, dg_ref, *rest):
        part_refs, (dp_ref, ds_ref), land_refs = rest[:nparts], rest[nparts:nparts + 2], rest[nparts + 2:2 * nparts + 2]
        dq_s, dz_s, dk_s, dv_s = rest[2 * nparts + 2:2 * nparts + 6]
        sems = rest[2 * nparts + 6:]
        b, i = pl.program_id(0), pl.program_id(1)

        @pl.when((b == 0) & (i == 0))
        def _():
            _scatter_start(part_refs, land_refs, sems)

        @pl.when((b == b_loc - 1) & (i == nb))
        def _():
            _scatter_wait(part_refs, land_refs, sems)

        lane = lax.broadcasted_iota(jnp.int32, (BLK, LANES), 1)
        lm = lane & (HEAD_DIM - 1)

        @pl.when((b == 0) & (i == 0))
        def _():
            ds_ref[...] = jnp.zeros_like(ds_ref)

        @pl.when(i < nb)
        def _compute():
            cc, sc = cc_ref[...], sc_ref[...]
            kcat = jnp.concatenate([_rope_apply(kp_ref[...], cp_ref[...], sp_ref[...], lm),
                                    _rope_apply(kc_ref[...], cc, sc, lm)], axis=0)
            vcat = jnp.concatenate([vp_ref[...], vc_ref[...]], axis=0)
            qr = [_rope_apply(q_ref[:, c * LANES:(c + 1) * LANES], cc, sc, lm) * (HEAD_DIM ** -0.5) for c in range(8)]
            valid = _attn_mask_t(i)
            lt64, lt64k = lane < HEAD_DIM, lax.broadcasted_iota(jnp.int32, (2 * BLK, LANES), 1) < HEAD_DIM
            do_chunks, doo_chunks, dz_chunks = [], [], []
            for c in range(8):
                cols = slice(c * LANES, (c + 1) * LANES)
                zc, oc, dgc = z_ref[:, cols], o_ref[:, cols], dg_ref[:, cols]
                sg = _sigmoid(zc)
                do_chunks.append(dgc * (zc * sg))
                dz_chunks.append(dgc * oc * (sg * (1.0 + zc * (1.0 - sg))))
                doo_chunks.append(do_chunks[c] * oc)
            dq_chunks = [None] * 8
            dk_h, dv_h = [], []
            ds_acc = jnp.zeros((SUBLANES, LANES), F32)
            tile_lane = lax.broadcasted_iota(jnp.int32, (SUBLANES, LANES), 1)
            tile_row = lax.broadcasted_iota(jnp.int32, (SUBLANES, LANES), 0)
            ones8 = jnp.ones((SUBLANES, LANES), BF16)
            for h in range(N_KV):
                qst = _head_stack(qr, h, lt64).astype(BF16)
                kext = _dup_half(kcat, h, lt64k)
                vext = _dup_half(vcat, h, lt64k).astype(BF16)
                pn, psink = _attn_probs_t(qst, kext.astype(BF16), h, sink_ref, valid)
                do_bf = _head_stack(do_chunks, h, lt64).astype(BF16)
                delta = sum(lax.dot_general(ones8, part, NT_DIMS, preferred_element_type=F32)
                            for part in _split3(_head_stack(doo_chunks, h, lt64)))[0:1, :]
                dpt = lax.dot_general(vext, do_bf, NT_DIMS, preferred_element_type=F32)
                dst = (pn * (dpt - delta)).astype(BF16)
                sink_term = psink * delta
                for g in range(GROUP):
                    val = -jnp.sum(sink_term[:, g * BLK:(g + 1) * BLK])
                    ds_acc = ds_acc + jnp.where((tile_lane == h * GROUP + g) & (tile_row == 0), val, 0.0)
                dqt = jnp.dot(kext.T.astype(BF16), dst, preferred_element_type=F32) * (HEAD_DIM ** -0.5)
                dk_ext = jnp.dot(dst, qst, preferred_element_type=F32)
                dv_ext = jnp.dot(pn.astype(BF16), do_bf, preferred_element_type=F32)
                dk_h.append(dk_ext + pltpu.roll(dk_ext, HEAD_DIM, 1))
                dv_h.append(dv_ext + pltpu.roll(dv_ext, HEAD_DIM, 1))
                for c2 in range(4):
                    dq_chunks[4 * h + c2] = _rope_bwd(_pair_chunk(dqt, c2), cc, sc, lm)
            ds_ref[...] += ds_acc
            dk_full = jnp.where(lt64k, dk_h[0], dk_h[1])
            dv_full = jnp.where(lt64k, dv_h[0], dv_h[1])

            @pl.when(i >= 1)
            def _emit():
                dp_ref[:, _QCOL:_QCOL + ATTN_WIDTH] = dq_s[...]
                dp_ref[:, _ZCOL:_ZCOL + ATTN_WIDTH] = dz_s[...]
                dp_ref[:, _KCOL:_KCOL + KV_WIDTH] = _rope_bwd(dk_s[...] + dk_full[:BLK], cp_ref[...], sp_ref[...], lm)
                dp_ref[:, _VCOL:_VCOL + KV_WIDTH] = dv_s[...] + dv_full[:BLK]

            for c in range(8):
                dq_s[:, c * LANES:(c + 1) * LANES] = dq_chunks[c]
                dz_s[:, c * LANES:(c + 1) * LANES] = dz_chunks[c]
            dk_s[...] = dk_full[BLK:]
            dv_s[...] = dv_full[BLK:]

        @pl.when(i == nb)
        def _final():
            dp_ref[:, _QCOL:_QCOL + ATTN_WIDTH] = dq_s[...]
            dp_ref[:, _ZCOL:_ZCOL + ATTN_WIDTH] = dz_s[...]
            dp_ref[:, _KCOL:_KCOL + KV_WIDTH] = _rope_bwd(dk_s[...], cc_ref[...], sc_ref[...], lm)
            dp_ref[:, _VCOL:_VCOL + KV_WIDTH] = dv_s[...]

    cur = lambda b, i: b * nb + jnp.minimum(i, nb - 1)
    prev = lambda b, i: b * nb + jnp.maximum(jnp.minimum(i, nb - 1) - 1, 0)
    emit = lambda b, i: b * nb + jnp.maximum(i - 1, 0)
    hbm = pl.BlockSpec(memory_space=pl.ANY)
    wide = lambda cb: pl.BlockSpec((BLK, ATTN_WIDTH), lambda b, i: (cur(b, i), cb))
    kv = lambda rowf, cb: pl.BlockSpec((BLK, LANES), lambda b, i: (rowf(b, i), cb))
    in_specs = [pl.BlockSpec(memory_space=pltpu.SMEM), wide(0), wide(1),
                kv(cur, _KCOL // LANES), kv(prev, _KCOL // LANES), kv(cur, _VCOL // LANES), kv(prev, _VCOL // LANES),
                kv(cur, 0), kv(cur, 0), kv(prev, 0), kv(prev, 0), wide(0), wide(0)] + [hbm] * nparts
    out_specs = [pl.BlockSpec((BLK, ATTN_IN), lambda b, i: (emit(b, i), 0)),
                 pl.BlockSpec((SUBLANES, LANES), lambda b, i: (0, 0))] + [hbm] * nparts
    return pl.pallas_call(
        body, name="attn_bwd", grid=(b_loc, nb + 1), in_specs=in_specs, out_specs=out_specs,
        out_shape=[jax.ShapeDtypeStruct((n, ATTN_IN), F32), jax.ShapeDtypeStruct((SUBLANES, LANES), F32)] + _scatter_lands(parts),
        scratch_shapes=[pltpu.VMEM((BLK, ATTN_WIDTH), F32), pltpu.VMEM((BLK, ATTN_WIDTH), F32),
                        pltpu.VMEM((BLK, KV_WIDTH), F32), pltpu.VMEM((BLK, KV_WIDTH), F32)] + _scatter_sems(nparts),
        compiler_params=_params(("arbitrary", "arbitrary"), 48),
    )(sinks, p, p, p, p, p, p, ct, st, ct, st, o, dg, *parts)


_CUM_ROWS = 256
_ROWS_OF = [CH - SUBLANES * (s // SUBLANES) for s in range(CH)]
_OFF_OF = [sum(_ROWS_OF[:s]) for s in range(CH)]
_PAIR_ROWS = sum(_ROWS_OF)


def _tri(lower):
    r = lax.broadcasted_iota(jnp.int32, (_CUM_ROWS, _CUM_ROWS), 0)
    c = lax.broadcasted_iota(jnp.int32, (_CUM_ROWS, _CUM_ROWS), 1)
    same = (r ^ c) < CH
    return (same & ((c <= r) if lower else (c >= r))).astype(BF16)


def _gates(qp, fp, lb):
    e = jnp.exp(-jnp.abs(fp))
    log_sig = jnp.minimum(fp, 0.0) - jnp.log(1.0 + e)
    a = jnp.log(lb)
    c = jnp.log(1.0 - lb) + log_sig
    g = jnp.maximum(a, c) + jnp.log(1.0 + jnp.exp(-jnp.abs(a - c)))
    sig_neg = jnp.where(fp >= 0, e, 1.0) / (1.0 + e)
    return qp * _sigmoid(qp), g, (1.0 - lb) * sig_neg, sig_neg


def _pair_rows(bc, s):
    r0 = SUBLANES * (s // SUBLANES)
    return jnp.exp(jnp.minimum(bc[r0:, :] - bc[s:s + 1, :], 0.0))


HP = 4
_HW = HP * REC_K


def _hgrn_specs(tb, nt, reverse):
    tmap = (lambda t: nt - 1 - t) if reverse else (lambda t: t)
    groups = REC_HEADS // HP
    blk = lambda cb: pl.BlockSpec((tb, _HW), lambda h, b, t: (b * nt + tmap(t), cb * groups + h))
    head = pl.BlockSpec((tb, _HW), lambda h, b, t: (b * nt + tmap(t), h))
    lbs = pl.BlockSpec((1, _HW), lambda h, b, t: (0, h))
    gws = pl.BlockSpec((1, REC_K), lambda h, b, t: (0, 0))
    hist = pl.BlockSpec((HP, 1, tb // CH, REC_K, REC_K), lambda h, b, t: (h, b, tmap(t), 0, 0))
    return blk, head, lbs, gws, hist


def _cumsum_chunks(tri, x, out_ref, tb):
    for r in range(0, tb, _CUM_ROWS):
        out_ref[r:r + _CUM_ROWS, :] = _dot3(tri, x[r:r + _CUM_ROWS, :])


def _hgrn_fwd(p, lb, gw, b_loc, t_len):
    n = p.shape[0]
    tb = min(TM, t_len)
    nt, nck = t_len // tb, tb // CH

    def body(qp_ref, fp_ref, i_ref, z_ref, lb_ref, gw_ref, oraw_ref, g_ref, sh_ref, q_s, k_s, b_s, o_s, st_ref):
        @pl.when(pl.program_id(2) == 0)
        def _():
            st_ref[...] = jnp.zeros_like(st_ref)

        qv, g, kk, _ = _gates(qp_ref[...], fp_ref[...], lb_ref[...])
        q_s[...] = qv
        k_s[...] = kk
        _cumsum_chunks(_tri(True), g, b_s, tb)
        ones = jnp.ones((REC_K, REC_K), BF16)
        sub = lax.broadcasted_iota(jnp.int32, (SUBLANES, REC_K), 0)

        def chunk(c, carry):
            for hp in range(HP):
                head_chunk(c, hp)
            return carry

        def head_chunk(c, hp):
            rs = pl.ds(pl.multiple_of(c * CH, CH), CH)
            cs = slice(hp * REC_K, (hp + 1) * REC_K)
            q, k, bc, v = q_s[rs, cs], k_s[rs, cs], b_s[rs, cs], i_ref[rs, cs]
            bl = bc[CH - 1:CH, :]
            st = st_ref[hp]
            sh_ref[hp, 0, c] = st
            o = lax.dot_general((q * jnp.exp(bc)).astype(BF16), st.astype(BF16), NT_DIMS, preferred_element_type=F32)
            w = jnp.concatenate([q[SUBLANES * (s // SUBLANES):, :] * _pair_rows(bc, s) * k[s:s + 1, :] for s in range(CH)], axis=0)
            a = jnp.dot(w.astype(BF16), ones, preferred_element_type=F32)
            acc = [jnp.zeros((SUBLANES, REC_K), F32) for _ in range(CH // SUBLANES)]
            for s in range(CH):
                j = s // SUBLANES
                vs = v[s:s + 1, :]
                for jj in range(j, CH // SUBLANES):
                    blk = a[_OFF_OF[s] + (jj - j) * SUBLANES:_OFF_OF[s] + (jj - j + 1) * SUBLANES, :]
                    if jj == j:
                        blk = jnp.where(sub >= s - j * SUBLANES, blk, 0.0)
                    acc[jj] = acc[jj] + blk * vs
            o_s[rs, cs] = o + jnp.concatenate(acc, axis=0)
            kd = k * jnp.exp(bl - bc)
            st_ref[hp] = st * jnp.exp(bl) + lax.dot_general(v.astype(BF16), kd.astype(BF16), TN_DIMS, preferred_element_type=F32)

        lax.fori_loop(0, nck, chunk, 0)
        oraw_ref[...] = o_s[...]
        for hp in range(HP):
            cs = slice(hp * REC_K, (hp + 1) * REC_K)
            o, zc = o_s[:, cs], z_ref[:, cs]
            on = (o * lax.rsqrt(jnp.mean(o * o, axis=-1, keepdims=True) + NORM_EPS)) * gw_ref[...]
            g_ref[:, cs] = (on * (zc * _sigmoid(zc))).astype(BF16)

    blk, head, lbs, gws, hist = _hgrn_specs(tb, nt, False)
    return pl.pallas_call(
        body, name="hgrn_fwd", grid=(REC_HEADS // HP, b_loc, nt),
        in_specs=[blk(0), blk(1), blk(2), blk(3), lbs, gws], out_specs=[head, head, hist],
        out_shape=[jax.ShapeDtypeStruct((n, 1024), F32), jax.ShapeDtypeStruct((n, 1024), BF16),
                   jax.ShapeDtypeStruct((REC_HEADS, b_loc, t_len // CH, REC_K, REC_K), F32)],
        scratch_shapes=[pltpu.VMEM((tb, _HW), F32)] * 4 + [pltpu.VMEM((HP, REC_K, REC_K), F32)],
        compiler_params=_params(("arbitrary", "arbitrary", "arbitrary"), 48),
    )(p, p, p, p, lb, gw)


def _hgrn_bwd(p, lb, gw, oraw, sh, dg, b_loc, t_len):
    n = p.shape[0]
    tb = min(TM, t_len)
    nt, nck = t_len // tb, tb // CH

    def body(qp_ref, fp_ref, i_ref, z_ref, lb_ref, gw_ref, oraw_ref, dg_ref, sh_ref,
             dq_ref, df_ref, di_ref, dz_ref, dlb_ref, dgw_ref,
             q_s, k_s, b_s, do_s, dqv_s, dk_s, db_s, rowk_s, rowv_s, dst_ref):
        b, t = pl.program_id(1), pl.program_id(2)

        @pl.when(t == 0)
        def _():
            dst_ref[...] = jnp.zeros_like(dst_ref)

        @pl.when((b == 0) & (t == 0))
        def _():
            dlb_ref[...] = jnp.zeros_like(dlb_ref)
            dgw_ref[...] = jnp.zeros_like(dgw_ref)

        lbv, qp, fp = lb_ref[...], qp_ref[...], fp_ref[...]
        qv, g, kk, sig_neg = _gates(qp, fp, lbv)
        q_s[...] = qv
        k_s[...] = kk
        _cumsum_chunks(_tri(True), g, b_s, tb)
        gwv = gw_ref[...]
        for hp in range(HP):
            cs = slice(hp * REC_K, (hp + 1) * REC_K)
            o, zc, dgv = oraw_ref[:, cs], z_ref[:, cs], dg_ref[:, cs]
            rn = lax.rsqrt(jnp.mean(o * o, axis=-1, keepdims=True) + NORM_EPS)
            on = o * rn
            sgz = _sigmoid(zc)
            dz_ref[:, cs] = dgv * (on * gwv) * (sgz * (1.0 + zc * (1.0 - sgz)))
            dpre = dgv * (zc * sgz)
            dgw_ref[hp] += jnp.sum(dpre * on, axis=0, keepdims=True)
            don = dpre * gwv
            do_s[:, cs] = rn * (don - on * jnp.mean(don * on, axis=-1, keepdims=True))

        ones = jnp.ones((REC_K, REC_K), BF16)
        sub = lax.broadcasted_iota(jnp.int32, (SUBLANES, REC_K), 0)
        rowid = lax.broadcasted_iota(jnp.int32, (CH, REC_K), 0)
        ngrp = CH // SUBLANES

        def chunk(ci, carry):
            for hp in range(HP):
                head_chunk(nck - 1 - ci, hp)
            return carry

        def head_chunk(c, hp):
            rs = pl.ds(pl.multiple_of(c * CH, CH), CH)
            cs = slice(hp * REC_K, (hp + 1) * REC_K)
            rowk, rowv = rowk_s.at[hp], rowv_s.at[hp]
            q, k, bc, v, do = q_s[rs, cs], k_s[rs, cs], b_s[rs, cs], i_ref[rs, cs], do_s[rs, cs]
            bl = bc[CH - 1:CH, :]
            st, dst = sh_ref[hp, 0, c], dst_ref[hp]
            eb, ebl, ekd = jnp.exp(bc), jnp.exp(bl), jnp.exp(bl - bc)
            qe, kd = q * eb, k * ekd
            do_bf, dst_bf = do.astype(BF16), dst.astype(BF16)
            dqe = jnp.dot(do_bf, st.astype(BF16), preferred_element_type=F32)
            dkd = jnp.dot(v.astype(BF16), dst_bf, preferred_element_type=F32)
            dv = lax.dot_general(kd.astype(BF16), dst_bf, NT_DIMS, preferred_element_type=F32)
            dbl = jnp.sum(dkd * kd, axis=0, keepdims=True) + ebl * jnp.sum(st * dst, axis=0, keepdims=True)
            dst_ref[hp] = dst * ebl + lax.dot_general(do_bf, qe.astype(BF16), TN_DIMS, preferred_element_type=F32)
            dec = [_pair_rows(bc, s) for s in range(CH)]
            w = jnp.concatenate([q[SUBLANES * (s // SUBLANES):, :] * dec[s] * k[s:s + 1, :] for s in range(CH)], axis=0)
            x = jnp.concatenate([do[SUBLANES * (s // SUBLANES):, :] * v[s:s + 1, :] for s in range(CH)], axis=0)
            a = jnp.dot(w.astype(BF16), ones, preferred_element_type=F32)
            da = jnp.dot(x.astype(BF16), ones, preferred_element_type=F32)
            dq_acc = [jnp.zeros((SUBLANES, REC_K), F32) for _ in range(ngrp)]
            for s in range(CH):
                j = s // SUBLANES
                r0 = j * SUBLANES
                ks = k[s:s + 1, :]
                uk = jnp.zeros((SUBLANES, REC_K), F32)
                uv = jnp.zeros((SUBLANES, REC_K), F32)
                for jj in range(j, ngrp):
                    lo, hi = _OFF_OF[s] + (jj - j) * SUBLANES, _OFF_OF[s] + (jj - j + 1) * SUBLANES
                    a_blk, da_blk = a[lo:hi, :], da[lo:hi, :]
                    if jj == j:
                        keep = sub >= s - r0
                        a_blk, da_blk = jnp.where(keep, a_blk, 0.0), jnp.where(keep, da_blk, 0.0)
                    rows = slice(jj * SUBLANES, (jj + 1) * SUBLANES)
                    tt = da_blk * dec[s][(jj - j) * SUBLANES:(jj - j + 1) * SUBLANES, :]
                    dq_acc[jj] = dq_acc[jj] + tt * ks
                    uk = uk + tt * q[rows, :]
                    uv = uv + a_blk * do[rows, :]
                rowk[s:s + 1, :] = jnp.sum(uk, axis=0, keepdims=True)
                rowv[s:s + 1, :] = jnp.sum(uv, axis=0, keepdims=True)
            dq_in = jnp.concatenate(dq_acc, axis=0)
            dk_in = rowk[...]
            dqv_s[rs, cs] = dqe * eb + dq_in
            dk_s[rs, cs] = dkd * ekd + dk_in
            di_ref[rs, cs] = dv + rowv[...]
            db = dqe * qe - dkd * kd + q * dq_in - k * dk_in
            db_s[rs, cs] = db + jnp.where(rowid == CH - 1, dbl, 0.0)

        lax.fori_loop(0, nck, chunk, 0)
        up = _tri(False)
        sgq = _sigmoid(qp)
        dq_ref[...] = dqv_s[...] * (sgq * (1.0 + qp * (1.0 - sgq)))
        dlb_acc = jnp.zeros((1, _HW), F32)
        for r in range(0, tb, _CUM_ROWS):
            rows = slice(r, r + _CUM_ROWS)
            dgl = _dot3(up, db_s[rows, :])
            dfg = dgl * jnp.exp(-g[rows, :]) - dk_s[rows, :]
            sn = sig_neg[rows, :]
            df_ref[rows, :] = dfg * (1.0 - lbv) * (1.0 - sn) * sn
            dlb_acc = dlb_acc + jnp.sum(dfg * sn, axis=0, keepdims=True)
        dlb_ref[...] += dlb_acc

    blk, head, lbs, gws, hist = _hgrn_specs(tb, nt, True)
    out_specs = [head, head, head, head, lbs, pl.BlockSpec((HP, 1, REC_K), lambda h, b, t: (h, 0, 0))]
    out_shape = [jax.ShapeDtypeStruct((n, 1024), F32)] * 4 + [
        jax.ShapeDtypeStruct((1, 1024), F32), jax.ShapeDtypeStruct((REC_HEADS, 1, REC_K), F32)]
    return pl.pallas_call(
        body, name="hgrn_bwd", grid=(REC_HEADS // HP, b_loc, nt),
        in_specs=[blk(0), blk(1), blk(2), blk(3), lbs, gws, head, head, hist], out_specs=out_specs, out_shape=out_shape,
        scratch_shapes=[pltpu.VMEM((tb, _HW), F32)] * 7 + [pltpu.VMEM((HP, CH, REC_K), F32)] * 2 + [pltpu.VMEM((HP, REC_K, REC_K), F32)],
        compiler_params=_params(("arbitrary", "arbitrary", "arbitrary"), 48),
    )(p, p, p, p, lb, gw, oraw, dg, sh)


def _postnorm_bwd_nt(dxo, y, qw, w, has_bias, name):
    n = dxo.shape[0]

    def body(dx_ref, y_ref, qw_ref, w_ref, dg_ref, dy_ref, dqw_ref, db_ref):
        @pl.when(pl.program_id(0) == 0)
        def _():
            dqw_ref[...] = jnp.zeros_like(dqw_ref)
            db_ref[...] = jnp.zeros_like(db_ref)

        yv, dxv = y_ref[...], dx_ref[...]
        r = lax.rsqrt(jnp.mean(yv * yv, axis=-1, keepdims=True) + NORM_EPS)
        u = yv * r
        du = dxv * qw_ref[...]
        dy = r * (du - u * jnp.mean(du * u, axis=-1, keepdims=True))
        dqw_ref[...] += jnp.sum(dxv * u, axis=0, keepdims=True)
        if has_bias:
            db_ref[...] += jnp.sum(dy, axis=0, keepdims=True)
        dyb = dy.astype(BF16)
        dy_ref[...] = dyb
        dg_ref[...] = lax.dot_general(dyb, w_ref[...], NT_DIMS, preferred_element_type=F32)

    rows = pl.BlockSpec((TM, D_MODEL), lambda i: (i, 0))
    const = lambda shape: pl.BlockSpec(shape, lambda i: (0, 0))
    return pl.pallas_call(
        body, name=name, grid=(n // TM,), in_specs=[rows, rows, const((1, D_MODEL)), const((D_MODEL, D_MODEL))],
        out_specs=[rows, rows, const((1, D_MODEL)), const((1, D_MODEL))],
        out_shape=[jax.ShapeDtypeStruct((n, D_MODEL), F32), jax.ShapeDtypeStruct((n, D_MODEL), BF16),
                   jax.ShapeDtypeStruct((1, D_MODEL), F32), jax.ShapeDtypeStruct((1, D_MODEL), F32)],
        compiler_params=_params(("arbitrary",), 48),
    )(dxo, y, qw, w)


def _nt_prenorm_bwd(dps, w, x, pw, dxo, has_bias, name, parts=()):
    n = x.shape[0]
    widths = [d.shape[1] for d in dps]
    m = sum(widths)
    npieces, nparts, steps = len(dps), len(parts), n // TM

    def body(*refs):
        dp_refs = refs[:npieces]
        w_ref, x_ref, pw_ref, dxo_ref = refs[npieces:npieces + 4]
        part_refs = refs[npieces + 4:npieces + 4 + nparts]
        dx_ref, dpw_ref, db_ref = refs[npieces + 4 + nparts:npieces + 7 + nparts]
        land_refs = refs[npieces + 7 + nparts:npieces + 7 + 2 * nparts]
        sems = refs[npieces + 7 + 2 * nparts:]

        @pl.when(pl.program_id(0) == 0)
        def _():
            dpw_ref[...] = jnp.zeros_like(dpw_ref)
            db_ref[...] = jnp.zeros_like(db_ref)
            if nparts:
                _scatter_start(part_refs, land_refs, sems)

        dh = jnp.zeros((TM, D_MODEL), F32)
        off = 0
        for dp_ref, wd in zip(dp_refs, widths):
            cn = _col_chunk(wd)
            for j in range(0, wd, cn):
                dpc = dp_ref[:, j:j + cn]
                if has_bias:
                    db_ref[:, off + j:off + j + cn] += jnp.sum(dpc, axis=0, keepdims=True)
                dh = dh + lax.dot_general(dpc.astype(BF16), w_ref[:, off + j:off + j + cn], NT_DIMS, preferred_element_type=F32)
            off += wd
        xv = x_ref[...]
        r = lax.rsqrt(jnp.mean(xv * xv, axis=-1, keepdims=True) + NORM_EPS)
        xn = xv * r
        dpw_ref[...] += jnp.sum(dh * xn, axis=0, keepdims=True)
        dxn = dh * pw_ref[...]
        dx_ref[...] = dxo_ref[...] + r * (dxn - xn * jnp.mean(dxn * xn, axis=-1, keepdims=True))

        if nparts:
            @pl.when(pl.program_id(0) == steps - 1)
            def _():
                _scatter_wait(part_refs, land_refs, sems)

    rows = pl.BlockSpec((TM, D_MODEL), lambda i: (i, 0))
    const = lambda shape: pl.BlockSpec(shape, lambda i: (0, 0))
    hbm = pl.BlockSpec(memory_space=pl.ANY)
    in_specs = ([pl.BlockSpec((TM, wd), lambda i: (i, 0)) for wd in widths] + [const((D_MODEL, m)), rows, const((1, D_MODEL)), rows]
                + [hbm] * nparts)
    return pl.pallas_call(
        body, name=name, grid=(steps,), in_specs=in_specs,
        out_specs=[rows, const((1, D_MODEL)), const((1, m))] + [hbm] * nparts,
        out_shape=[jax.ShapeDtypeStruct((n, D_MODEL), F32), jax.ShapeDtypeStruct((1, D_MODEL), F32),
                   jax.ShapeDtypeStruct((1, m), F32)] + _scatter_lands(parts),
        scratch_shapes=_scatter_sems(nparts) if nparts else [],
        compiler_params=_params(("arbitrary",), 56),
    )(*dps, w, x, pw, dxo, *parts)


def _matmul_tn(a, b, name):
    n, k = a.shape
    m = b.shape[1]
    tk, tm, tn = 512, _col_chunk(m), 1024 if n % 1024 == 0 else n

    def body(a_ref, b_ref, o_ref):
        @pl.when(pl.program_id(2) == 0)
        def _():
            o_ref[...] = jnp.zeros_like(o_ref)

        o_ref[...] += lax.dot_general(a_ref[...], b_ref[...].astype(BF16), TN_DIMS, preferred_element_type=F32)

    return pl.pallas_call(
        body, name=name, grid=(k // tk, m // tm, n // tn),
        in_specs=[pl.BlockSpec((tn, tk), lambda i, j, l: (l, i)), pl.BlockSpec((tn, tm), lambda i, j, l: (l, j))],
        out_specs=pl.BlockSpec((tk, tm), lambda i, j, l: (i, j)),
        out_shape=jax.ShapeDtypeStruct((k, m), F32),
        compiler_params=_params(("arbitrary", "arbitrary", "arbitrary"), 48),
    )(a, b)


def _by_owner_cols(dw):
    k, m = dw.shape
    return dw.reshape(k, N_DEV, m // N_DEV).transpose(1, 0, 2)


def _own_and_bf16(part):
    return lax.dynamic_index_in_dim(part, _my_id(), 0, keepdims=False), part.astype(BF16)


def _step(x, pos_col, tgt, pre_w, post_w, wa_in, ba_in, sinks, wa_out_shard, ba_out, wr_in_shard, lb_logits, gnorm_w, wr_out_shard, b_loc, t_len):
    nb = t_len // BLK
    ct, st = _rope_tables(pos_col)
    lb = _lower_bound(lb_logits)
    p0, h0, ga_out = _norm_matmul(x, pre_w[0:1], wa_in, ba_in, "attn_in_proj", [wa_out_shard])
    wa_out = ga_out.reshape(ATTN_WIDTH, D_MODEL)
    o0, g0, gr_in, gr_out = _attn_fwd(p0, ct, st, sinks, b_loc, nb, [wr_in_shard, wr_out_shard])
    wr_in = gr_in.transpose(1, 0, 2).reshape(D_MODEL, REC_IN)
    wr_out = gr_out.reshape(1024, D_MODEL)
    y0, x1 = _outproj_postnorm(g0, wa_out, ba_out, x, post_w[0:1], None, "attn_out_proj")
    p1, h1 = _norm_matmul(x1, pre_w[1:2], wr_in, None, "rec_in_proj")
    o1, g1, sh = _hgrn_fwd(p1, lb, gnorm_w, b_loc, t_len)
    y1, dx2, loss_tile = _outproj_postnorm(g1, wr_out, None, x1, post_w[1:2], tgt, "rec_out_proj_loss")
    dg1, dy1, dpost1, _ = _postnorm_bwd_nt(dx2, y1, post_w[1:2], wr_out, False, "rec_out_bwd")
    d_wr_out = _matmul_tn(g1, dy1, "rec_w_out_grad")
    dq1, df1, di1, dz1, dlb, dgw = _hgrn_bwd(p1, lb, gnorm_w, o1, sh, dg1, b_loc, t_len)
    dps1 = [dq1, df1, di1, dz1]
    dx1, dpre1, _ = _nt_prenorm_bwd(dps1, wr_in, x1, pre_w[1:2], dx2, False, "rec_in_bwd")
    d_wr_in = [_matmul_tn(h1, dpk, f"rec_w_in_grad_{k}") for k, dpk in enumerate(dps1)]
    dg0, dy0, dpost0, dba_out = _postnorm_bwd_nt(dx1, y0, post_w[0:1], wa_out, True, "attn_out_bwd")
    d_wa_out = _matmul_tn(g0, dy0, "attn_w_out_grad")
    owns, wires = zip(*[_own_and_bf16(part) for part in (
        _by_owner_cols(jnp.concatenate(d_wr_in, axis=1)), d_wr_out.reshape(N_DEV, 1024 // N_DEV, D_MODEL),
        d_wa_out.reshape(N_DEV, ATTN_WIDTH // N_DEV, D_MODEL))])
    dp0, dsink_tile, *lands = _attn_bwd(p0, ct, st, sinks, o0, dg0, b_loc, nb, list(wires))
    d_wa_in = _matmul_tn(h0, dp0, "attn_w_in_grad")
    own_a_in, wire_a_in = _own_and_bf16(_by_owner_cols(_qkvz(d_wa_in)))
    dx0, dpre0, dba_in, land_a_in = _nt_prenorm_bwd([dp0], wa_in, x, pre_w[0:1], dx1, True, "attn_in_bwd", [wire_a_in])
    small = dict(pre=jnp.concatenate([dpre0, dpre1], axis=0), post=jnp.concatenate([dpost0, dpost1], axis=0),
                 ba_in=dba_in, sinks=dsink_tile[0:1, 0:N_HEADS], ba_out=dba_out, lb=dlb, gnorm=jnp.sum(dgw, axis=0))
    return loss_tile, dx0, list(zip(lands, owns)) + [(land_a_in, own_a_in)], small


def _my_id():
    return lax.axis_index("x") * 4 + lax.axis_index("y") * 2 + lax.axis_index("c")


def _peer(k):
    x, y, c = lax.axis_index("x"), lax.axis_index("y"), lax.axis_index("c")
    return (x ^ ((k >> 2) & 1), y ^ ((k >> 1) & 1), c ^ (k & 1))


def _peer_id(k):
    return _my_id() ^ k


def _all_gather(shards):
    nsh = len(shards)

    def body(*refs):
        ins, outs, sems = refs[:nsh], refs[nsh:2 * nsh], refs[2 * nsh:]
        _gather_start(ins, outs, sems)
        _gather_wait(ins, outs, sems)

    hbm = pl.BlockSpec(memory_space=pl.ANY)
    return pl.pallas_call(
        body, name="comm_all_gather", in_specs=[hbm] * nsh, out_specs=[hbm] * nsh,
        out_shape=_gather_shapes(shards), scratch_shapes=_gather_sems(nsh),
    )(*shards)


def _gather_shapes(shards):
    return [jax.ShapeDtypeStruct((N_DEV,) + s.shape, s.dtype) for s in shards]


def _gather_sems(nsh):
    return [pltpu.SemaphoreType.DMA((nsh, N_DEV - 1)), pltpu.SemaphoreType.DMA((nsh, N_DEV - 1)), pltpu.SemaphoreType.DMA((nsh,))]


def _gather_copies(ins, outs, sems, received):
    send_sems, recv_sems, local_sems = sems
    me = _my_id()
    local = [pltpu.make_async_copy(ins[a], outs[a].at[me], local_sems.at[a]) for a in range(len(ins))]
    remote = [pltpu.make_async_remote_copy(
        src_ref=ins[a], dst_ref=outs[a].at[_peer_id(k) if received else me], send_sem=send_sems.at[a, k - 1],
        recv_sem=recv_sems.at[a, k - 1], device_id=_peer(k), device_id_type=MESH)
        for a in range(len(ins)) for k in range(1, N_DEV)]
    return local, remote


def _gather_start(ins, outs, sems):
    local, sends = _gather_copies(ins, outs, sems, False)
    for cp in local + sends:
        cp.start()


def _gather_wait(ins, outs, sems):
    local, recvs = _gather_copies(ins, outs, sems, True)
    for cp in recvs:
        cp.wait_recv()
    for cp in recvs:
        cp.wait_send()
    for cp in local:
        cp.wait()


def _scatter_lands(parts):
    return [jax.ShapeDtypeStruct((N_DEV - 1,) + p.shape[1:], p.dtype) for p in parts]


def _scatter_sems(nparts):
    return [pltpu.SemaphoreType.DMA((nparts, N_DEV - 1)), pltpu.SemaphoreType.DMA((nparts, N_DEV - 1))]


def _scatter_copies(parts, lands, sems):
    send_sems, recv_sems = sems
    return [pltpu.make_async_remote_copy(
        src_ref=parts[a].at[_peer_id(k)], dst_ref=lands[a].at[k - 1], send_sem=send_sems.at[a, k - 1],
        recv_sem=recv_sems.at[a, k - 1], device_id=_peer(k), device_id_type=MESH)
        for a in range(len(parts)) for k in range(1, N_DEV)]


def _scatter_start(parts, lands, sems):
    for cp in _scatter_copies(parts, lands, sems):
        cp.start()


def _scatter_wait(parts, lands, sems):
    copies = _scatter_copies(parts, lands, sems)
    for cp in copies:
        cp.wait_recv()
    for cp in copies:
        cp.wait_send()


def _adamw(w, g, m, v):
    m2 = ADAM_B1 * m + (1.0 - ADAM_B1) * g
    v2 = ADAM_B2 * v + (1.0 - ADAM_B2) * (g * g)
    m_hat = m2 / (1.0 - ADAM_B1 ** ADAM_STEP)
    v_hat = v2 / (1.0 - ADAM_B2 ** ADAM_STEP)
    delta = -ADAM_LR * (m_hat / (jnp.sqrt(v_hat) + ADAM_EPS) + ADAM_WD * w)
    return delta, m2, v2


def _sum_adamw_rows(land_ref, own_ref, w_ref, m_ref, v_ref, out_refs):
    r, c = own_ref.shape
    rc = 64 if r % 64 == 0 else r
    me = _my_id()
    g_ref, d_ref, m2_ref, v2_ref = out_refs

    def rows(i, carry):
        rs = pl.ds(pl.multiple_of(i * rc, rc), rc)
        g = jnp.zeros((rc, c), F32)
        for dev in range(N_DEV):
            k = dev ^ me
            g = g + jnp.where(k == 0, own_ref[rs, :], land_ref[jnp.maximum(k - 1, 0), rs, :].astype(F32))
        delta, m2, v2 = _adamw(w_ref[rs, :], g, m_ref[rs, :], v_ref[rs, :])
        g_ref[rs, :] = g
        d_ref[rs, :] = delta
        m2_ref[rs, :] = m2
        v2_ref[rs, :] = v2
        return carry

    lax.fori_loop(0, r // rc, rows, 0)


def _sum_adamw(land, own, w, m, v, name):
    r, c = own.shape

    def body(land_ref, own_ref, w_ref, m_ref, v_ref, g_ref, d_ref, m2_ref, v2_ref):
        _sum_adamw_rows(land_ref, own_ref, w_ref, m_ref, v_ref, (g_ref, d_ref, m2_ref, v2_ref))

    vmem = pl.BlockSpec(memory_space=pltpu.VMEM)
    return pl.pallas_call(
        body, name=name, in_specs=[vmem] * 5, out_specs=[vmem] * 4, out_shape=[jax.ShapeDtypeStruct((r, c), F32)] * 4,
        compiler_params=_params(None, 56),
    )(land, own, w, m, v)


_SMALL = [("pre_norm_w", 2048), ("post_norm_w", 2048), ("attn_b_in", 2304), ("attn_sinks", 16), ("attn_b_out", 1024),
          ("rec_lb_logits", 2048), ("rec_gnorm_w", 128)]
_TILE = SUBLANES * LANES


def _small_rows(size):
    return -(-size // _TILE) * SUBLANES


_SMALL_OFF = {}
_r = 0
for _name, _size in _SMALL:
    _SMALL_OFF[_name] = _r
    _r += _small_rows(_size)
_SMALL_ROWS = _r


def _pack_small(pieces):
    out = []
    for name, size in _SMALL:
        flat = pieces[name].reshape(-1).astype(F32)
        out.append(jnp.pad(flat, (0, _small_rows(size) * LANES - size)).reshape(-1, LANES))
    return jnp.concatenate(out, axis=0)


def _unpack_small(packed, shapes):
    return {name: packed[_SMALL_OFF[name]:_SMALL_OFF[name] + _small_rows(size)].reshape(-1)[:size].reshape(shapes[name])
            for name, size in _SMALL}


def _small_allreduce_adamw(gpart, w, m, v):
    lb0 = _SMALL_OFF["rec_lb_logits"]

    def body(gp_ref, w_ref, m_ref, v_ref, g_ref, d_ref, m2_ref, v2_ref, land_ref, send_sems, recv_sems):
        me = _my_id()
        sent = []
        for k in range(1, N_DEV):
            cp = pltpu.make_async_remote_copy(src_ref=gp_ref, dst_ref=land_ref.at[k - 1], send_sem=send_sems.at[k - 1],
                                              recv_sem=recv_sems.at[k - 1], device_id=_peer(k), device_id_type=MESH)
            cp.start()
            sent.append(cp)
        for cp in sent:
            cp.wait_recv()
        for cp in sent:
            cp.wait_send()
        g = jnp.zeros((_SMALL_ROWS, LANES), F32)
        for dev in range(N_DEV):
            k = dev ^ me
            g = g + jnp.where(k == 0, gp_ref[...], land_ref[jnp.maximum(k - 1, 0)])
        g_ref[...] = g
        l0, l1 = w_ref[lb0:lb0 + SUBLANES, :], w_ref[lb0 + SUBLANES:lb0 + 2 * SUBLANES, :]
        mx = jnp.maximum(l0, l1)
        e0, e1 = jnp.exp(l0 - mx), jnp.exp(l1 - mx)
        p1 = e1 / (e0 + e1)
        dl1 = (1.0 - p1) * p1 * g[lb0:lb0 + SUBLANES, :]
        g_ref[lb0:lb0 + SUBLANES, :] = -dl1
        g_ref[lb0 + SUBLANES:lb0 + 2 * SUBLANES, :] = dl1
        delta, m2, v2 = _adamw(w_ref[...], g_ref[...], m_ref[...], v_ref[...])
        d_ref[...] = delta
        m2_ref[...] = m2
        v2_ref[...] = v2

    vmem = pl.BlockSpec(memory_space=pltpu.VMEM)
    return pl.pallas_call(
        body, name="comm_small_allreduce_adamw", in_specs=[vmem] * 4, out_specs=[vmem] * 4,
        out_shape=[jax.ShapeDtypeStruct((_SMALL_ROWS, LANES), F32)] * 4,
        scratch_shapes=[pltpu.VMEM((N_DEV - 1, _SMALL_ROWS, LANES), F32), pltpu.SemaphoreType.DMA((N_DEV - 1,)),
                        pltpu.SemaphoreType.DMA((N_DEV - 1,))],
    )(gpart, w, m, v)


def _qzkv(a):
    return jnp.concatenate([a[..., :1024], a[..., 1280:], a[..., 1024:1280]], axis=-1)


def _qkvz(a):
    return jnp.concatenate([a[..., :1024], a[..., 2048:], a[..., 1024:2048]], axis=-1)


def kernel(x, positions, pre_norm_w, post_norm_w, attn_w_in, attn_b_in, attn_sinks, attn_w_out, attn_b_out, rec_w_in, rec_lb_logits, rec_gnorm_w, rec_w_out, loss_target, m_pre_norm_w, m_post_norm_w, m_attn_w_in, m_attn_b_in, m_attn_sinks, m_attn_w_out, m_attn_b_out, m_rec_w_in, m_rec_lb_logits, m_rec_gnorm_w, m_rec_w_out, v_pre_norm_w, v_post_norm_w, v_attn_w_in, v_attn_b_in, v_attn_sinks, v_attn_w_out, v_attn_b_out, v_rec_w_in, v_rec_lb_logits, v_rec_gnorm_w, v_rec_w_out):
    b_loc, t_len, _ = x.shape
    n = b_loc * t_len
    ga_in, = _all_gather([attn_w_in[0].astype(BF16)])
    wa_in = _qzkv(ga_in.transpose(1, 0, 2).reshape(D_MODEL, ATTN_IN))

    loss_tile, dx, landed, small = _step(
        x.reshape(n, D_MODEL), positions.reshape(n, 1).astype(F32), loss_target.reshape(n, D_MODEL),
        pre_norm_w, post_norm_w, wa_in, _qzkv(attn_b_in), attn_sinks, attn_w_out[0].astype(BF16), attn_b_out,
        rec_w_in[0].astype(BF16), rec_lb_logits, rec_gnorm_w, rec_w_out[0].astype(BF16), b_loc, t_len)
    loss = lax.psum(loss_tile[0, 0], ("x", "y", "c"))

    lift = lambda outs: tuple(a[None] for a in outs)
    (l_r_in, o_r_in), (l_r_out, o_r_out), (l_a_out, o_a_out), (l_a_in, o_a_in) = landed
    r_a_in = lift(_sum_adamw(l_a_in, o_a_in, attn_w_in[0], m_attn_w_in[0], v_attn_w_in[0], "adamw_attn_w_in"))
    r_r_in = lift(_sum_adamw(l_r_in, o_r_in, rec_w_in[0], m_rec_w_in[0], v_rec_w_in[0], "adamw_rec_w_in"))
    r_r_out = lift(_sum_adamw(l_r_out, o_r_out, rec_w_out[0], m_rec_w_out[0], v_rec_w_out[0], "adamw_rec_w_out"))
    r_a_out = lift(_sum_adamw(l_a_out, o_a_out, attn_w_out[0], m_attn_w_out[0], v_attn_w_out[0], "adamw_attn_w_out"))

    gsmall = dict(pre_norm_w=small["pre"], post_norm_w=small["post"], attn_b_in=_qkvz(small["ba_in"]), attn_sinks=small["sinks"],
                  attn_b_out=small["ba_out"], rec_lb_logits=jnp.concatenate([small["lb"], jnp.zeros_like(small["lb"])], axis=0),
                  rec_gnorm_w=small["gnorm"])
    wsmall = dict(pre_norm_w=pre_norm_w, post_norm_w=post_norm_w, attn_b_in=attn_b_in, attn_sinks=attn_sinks,
                  attn_b_out=attn_b_out, rec_lb_logits=rec_lb_logits, rec_gnorm_w=rec_gnorm_w)
    msmall = dict(pre_norm_w=m_pre_norm_w, post_norm_w=m_post_norm_w, attn_b_in=m_attn_b_in, attn_sinks=m_attn_sinks,
                  attn_b_out=m_attn_b_out, rec_lb_logits=m_rec_lb_logits, rec_gnorm_w=m_rec_gnorm_w)
    vsmall = dict(pre_norm_w=v_pre_norm_w, post_norm_w=v_post_norm_w, attn_b_in=v_attn_b_in, attn_sinks=v_attn_sinks,
                  attn_b_out=v_attn_b_out, rec_lb_logits=v_rec_lb_logits, rec_gnorm_w=v_rec_gnorm_w)
    shapes = {k: a.shape for k, a in wsmall.items()}
    packed = _small_allreduce_adamw(_pack_small(gsmall), _pack_small(wsmall), _pack_small(msmall), _pack_small(vsmall))
    sg, sd, sm, sv = [_unpack_small(a, shapes) for a in packed]

    big = {"attn_w_in": r_a_in, "attn_w_out": r_a_out, "rec_w_in": r_r_in, "rec_w_out": r_r_out}
    order = ["pre_norm_w", "post_norm_w", "attn_w_in", "attn_b_in", "attn_sinks", "attn_w_out", "attn_b_out", "rec_w_in",
             "rec_lb_logits", "rec_gnorm_w", "rec_w_out"]
    outs = [loss, dx.reshape(b_loc, t_len, D_MODEL)]
    for idx, small_set in enumerate((sg, sd, sm, sv)):
        outs += [big[nm][idx] if nm in big else small_set[nm] for nm in order]
    return tuple(outs)
```

```python
import numpy as np
import jax
import jax.numpy as jnp
from jax import lax
from jax.experimental import pallas as pl
from jax.experimental.pallas import tpu as pltpu

F32, BF16 = jnp.float32, jnp.bfloat16
MESH = pl.DeviceIdType.MESH
N_DEV = 8

D_MODEL = 1024
N_HEADS, HEAD_DIM, N_KV, GROUP = 16, 64, 2, 8
ATTN_WIDTH, KV_WIDTH = 1024, 128
ATTN_IN = 2 * ATTN_WIDTH + 2 * KV_WIDTH
BLK = 128
ROPE_THETA, ROPE_HALF = 500000.0, 8
REC_HEADS, REC_K = 8, 128
REC_IN = 4 * 1024
CH = 32
NORM_EPS = 1e-6
ADAM_LR, ADAM_B1, ADAM_B2, ADAM_EPS, ADAM_WD, ADAM_STEP = 0.001, 0.9, 0.999, 1e-08, 0.01, 10

LANES, SUBLANES = 128, 8
TM = 512
NT_DIMS = (((1,), (1,)), ((), ()))
TN_DIMS = (((0,), (0,)), ((), ()))
MB = 2 ** 20


def _params(sem=None, vmem_mb=48, **kw):
    return pltpu.CompilerParams(dimension_semantics=sem, vmem_limit_bytes=vmem_mb * MB, **kw)


def _col_chunk(m):
    return 768 if m % 1024 else 1024


def _sigmoid(x):
    return 1.0 / (1.0 + jnp.exp(-x))


def _split3(x):
    hi = x.astype(BF16)
    r1 = x - hi.astype(F32)
    mid = r1.astype(BF16)
    lo = (r1 - mid.astype(F32)).astype(BF16)
    return hi, mid, lo


def _dot3(l_bf, x):
    hi, mid, lo = _split3(x)
    return (jnp.dot(l_bf, hi, preferred_element_type=F32) + jnp.dot(l_bf, mid, preferred_element_type=F32)
            + jnp.dot(l_bf, lo, preferred_element_type=F32))


def _rope_tables(pos_col):
    n = pos_col.shape[0]
    lane = np.arange(LANES) % HEAD_DIM
    inv = np.float32(ROPE_THETA) ** (-(np.arange(ROPE_HALF, dtype=np.float32) * np.float32(2.0) / np.float32(2 * ROPE_HALF)))
    freq = np.where(lane < 2 * ROPE_HALF, inv[lane % ROPE_HALF], 0.0).astype(np.float32)[None, :]
    sign = np.where(lane < ROPE_HALF, -1.0, np.where(lane < 2 * ROPE_HALF, 1.0, 0.0)).astype(np.float32)[None, :]

    def body(p_ref, f_ref, s_ref, c_out, s_out):
        ang = p_ref[...] * f_ref[...]
        c_out[...] = jnp.cos(ang)
        s_out[...] = jnp.sin(ang) * s_ref[...]

    row = pl.BlockSpec((TM, 1), lambda i: (i, 0))
    vec = pl.BlockSpec((1, LANES), lambda i: (0, 0))
    out = pl.BlockSpec((TM, LANES), lambda i: (i, 0))
    return pl.pallas_call(
        body, name="rope_tables", grid=(n // TM,), in_specs=[row, vec, vec], out_specs=[out, out],
        out_shape=[jax.ShapeDtypeStruct((n, LANES), F32)] * 2, compiler_params=_params(("arbitrary",)),
    )(pos_col, jnp.asarray(freq), jnp.asarray(sign))


def _rope_apply(xv, c, s, lm):
    partner = jnp.where(lm < ROPE_HALF, pltpu.roll(xv, LANES - ROPE_HALF, 1), pltpu.roll(xv, ROPE_HALF, 1))
    return xv * c + partner * s


def _rope_bwd(dy, c, s, lm):
    t = dy * s
    partner = jnp.where(lm < ROPE_HALF, pltpu.roll(t, LANES - ROPE_HALF, 1),
                        jnp.where(lm < 2 * ROPE_HALF, pltpu.roll(t, ROPE_HALF, 1), 0.0))
    return dy * c + partner


def _lower_bound(lb_logits):
    def body(l_ref, o_ref):
        l0, l1 = l_ref[0:1, :], l_ref[1:2, :]
        m = jnp.maximum(l0, l1)
        e0, e1 = jnp.exp(l0 - m), jnp.exp(l1 - m)
        o_ref[...] = e1 / (e0 + e1)

    return pl.pallas_call(body, name="lower_bound", out_shape=jax.ShapeDtypeStruct((1, lb_logits.shape[1]), F32))(lb_logits)


def _norm_matmul(x, pw, w, bias, name, shards=()):
    n, m = x.shape[0], w.shape[1]
    cn = _col_chunk(m)
    has_bias = bias is not None
    nsh, steps = len(shards), n // TM

    def body(*refs):
        refs = list(refs)
        x_ref, pw_ref, w_ref = refs[:3]
        b_ref = refs[3] if has_bias else None
        refs = refs[4 if has_bias else 3:]
        sh_in, (p_ref, h_ref), sh_out, sems = refs[:nsh], refs[nsh:nsh + 2], refs[nsh + 2:2 * nsh + 2], refs[2 * nsh + 2:]
        if nsh:
            @pl.when(pl.program_id(0) == 0)
            def _():
                _gather_start(sh_in, sh_out, sems)

        xv = x_ref[...]
        r = lax.rsqrt(jnp.mean(xv * xv, axis=-1, keepdims=True) + NORM_EPS)
        h = ((xv * r) * pw_ref[...]).astype(BF16)
        h_ref[...] = h
        for j in range(0, m, cn):
            acc = jnp.dot(h, w_ref[:, j:j + cn], preferred_element_type=F32)
            if has_bias:
                acc = acc + b_ref[:, j:j + cn]
            p_ref[:, j:j + cn] = acc

        if nsh:
            @pl.when(pl.program_id(0) == steps - 1)
            def _():
                _gather_wait(sh_in, sh_out, sems)

    rows = pl.BlockSpec((TM, D_MODEL), lambda i: (i, 0))
    const = lambda shape: pl.BlockSpec(shape, lambda i: (0, 0))
    hbm = pl.BlockSpec(memory_space=pl.ANY)
    in_specs = [rows, const((1, D_MODEL)), const((D_MODEL, m))] + ([const((1, m))] if has_bias else []) + [hbm] * nsh
    args = (x, pw, w) + ((bias,) if has_bias else ()) + tuple(shards)
    return pl.pallas_call(
        body, name=name, grid=(steps,), in_specs=in_specs,
        out_specs=[pl.BlockSpec((TM, m), lambda i: (i, 0)), rows] + [hbm] * nsh,
        out_shape=[jax.ShapeDtypeStruct((n, m), F32), jax.ShapeDtypeStruct((n, D_MODEL), BF16)] + _gather_shapes(shards),
        scratch_shapes=_gather_sems(nsh) if nsh else [],
        compiler_params=_params(("arbitrary",), 56),
    )(*args)


def _outproj_postnorm(g, w, bias, xres, qw, tgt, name):
    n = g.shape[0]
    has_bias, has_loss = bias is not None, tgt is not None
    steps = n // TM

    def body(*refs):
        refs = list(refs)
        g_ref, w_ref = refs.pop(0), refs.pop(0)
        b_ref = refs.pop(0) if has_bias else None
        x_ref, qw_ref = refs.pop(0), refs.pop(0)
        t_ref = refs.pop(0) if has_loss else None
        y_ref, o_ref = refs.pop(0), refs.pop(0)
        y = jnp.dot(g_ref[...], w_ref[...], preferred_element_type=F32)
        if has_bias:
            y = y + b_ref[...]
        y_ref[...] = y
        r = lax.rsqrt(jnp.mean(y * y, axis=-1, keepdims=True) + NORM_EPS)
        xn = x_ref[...] + (y * r) * qw_ref[...]
        if not has_loss:
            o_ref[...] = xn
        else:
            loss_ref, acc_ref = refs
            i = pl.program_id(0)
            e = xn - t_ref[...]
            o_ref[...] = e * (1.0 / D_MODEL)

            @pl.when(i == 0)
            def _():
                acc_ref[...] = jnp.zeros_like(acc_ref)

            acc_ref[...] += jnp.sum(e * e, axis=0, keepdims=True)

            @pl.when(i == steps - 1)
            def _():
                loss_ref[...] = jnp.full(loss_ref.shape, jnp.sum(acc_ref[...]) * (0.5 / D_MODEL), F32)

    rows = pl.BlockSpec((TM, D_MODEL), lambda i: (i, 0))
    const = lambda shape: pl.BlockSpec(shape, lambda i: (0, 0))
    in_specs = [rows, const((D_MODEL, D_MODEL))] + ([const((1, D_MODEL))] if has_bias else []) + [rows, const((1, D_MODEL))]
    args = [g, w] + ([bias] if has_bias else []) + [xres, qw]
    out_specs = [rows, rows]
    out_shape = [jax.ShapeDtypeStruct((n, D_MODEL), F32)] * 2
    scratch = []
    if has_loss:
        in_specs.append(rows)
        args.append(tgt)
        out_specs.append(const((SUBLANES, LANES)))
        out_shape.append(jax.ShapeDtypeStruct((SUBLANES, LANES), F32))
        scratch = [pltpu.VMEM((1, D_MODEL), F32)]
    return pl.pallas_call(
        body, name=name, grid=(steps,), in_specs=in_specs, out_specs=out_specs, out_shape=out_shape,
        scratch_shapes=scratch, compiler_params=_params(("arbitrary",), 48),
    )(*args)


_QCOL, _ZCOL, _KCOL, _VCOL = 0, 1024, 2048, 2176


def _head_stack(chunks, h, lt64):
    return jnp.concatenate(
        [jnp.where(lt64 if g % 2 == 0 else ~lt64, chunks[4 * h + g // 2], 0.0) for g in range(GROUP)], axis=0)


def _dup_half(x, h, lt64):
    r = pltpu.roll(x, HEAD_DIM, 1)
    return jnp.where(lt64, x, r) if h == 0 else jnp.where(lt64, r, x)


def _pair_chunk(xt, c2):
    a, b = 2 * c2, 2 * c2 + 1
    return jnp.concatenate([xt[:HEAD_DIM, a * BLK:(a + 1) * BLK], xt[HEAD_DIM:, b * BLK:(b + 1) * BLK]], axis=0).T


def _attn_mask_t(i):
    key = lax.broadcasted_iota(jnp.int32, (2 * BLK, BLK), 0)
    qry = lax.broadcasted_iota(jnp.int32, (2 * BLK, BLK), 1)
    valid = (key > qry) & (key <= qry + BLK) & ((key >= BLK) | (i > 0))
    return jnp.tile(jnp.where(valid, 0.0, -1e30), (1, GROUP))


def _attn_probs_t(qst, kext, h, sink_ref, mask):
    s = lax.dot_general(kext, qst, NT_DIMS, preferred_element_type=F32) + mask
    head = lax.broadcasted_iota(jnp.int32, (1, GROUP * BLK), 1) >> 7
    sk = jnp.zeros((1, GROUP * BLK), F32)
    for g in range(GROUP):
        sk = jnp.where(head == g, sink_ref[0, h * GROUP + g], sk)
    m = jnp.maximum(jnp.max(s, axis=0, keepdims=True), sk)
    p = jnp.exp(s - m)
    esk = jnp.exp(sk - m)
    inv = 1.0 / (jnp.sum(p, axis=0, keepdims=True) + esk)
    return p * inv, esk * inv


def _attn_fwd(p, ct, st, sinks, b_loc, nb, shards):
    n = p.shape[0]
    nsh = len(shards)

    def body(sink_ref, q_ref, z_ref, kc_ref, kp_ref, vc_ref, vp_ref, cc_ref, sc_ref, cp_ref, sp_ref, *rest):
        sh_in, (o_ref, g_ref), sh_out, sems = rest[:nsh], rest[nsh:nsh + 2], rest[nsh + 2:2 * nsh + 2], rest[2 * nsh + 2:]
        b, i = pl.program_id(0), pl.program_id(1)

        @pl.when((b == 0) & (i == 0))
        def _():
            _gather_start(sh_in, sh_out, sems)

        lane = lax.broadcasted_iota(jnp.int32, (BLK, LANES), 1)
        lm = lane & (HEAD_DIM - 1)
        cc, sc = cc_ref[...], sc_ref[...]
        kcat = jnp.concatenate([_rope_apply(kp_ref[...], cp_ref[...], sp_ref[...], lm),
                                _rope_apply(kc_ref[...], cc, sc, lm)], axis=0)
        vcat = jnp.concatenate([vp_ref[...], vc_ref[...]], axis=0)
        qr = [_rope_apply(q_ref[:, c * LANES:(c + 1) * LANES], cc, sc, lm) * (HEAD_DIM ** -0.5) for c in range(8)]
        valid = _attn_mask_t(i)
        lt64, lt64k = lane < HEAD_DIM, lax.broadcasted_iota(jnp.int32, (2 * BLK, LANES), 1) < HEAD_DIM
        for h in range(N_KV):
            qst = _head_stack(qr, h, lt64).astype(BF16)
            pn, _ = _attn_probs_t(qst, _dup_half(kcat, h, lt64k).astype(BF16), h, sink_ref, valid)
            ot = lax.dot_general(_dup_half(vcat, h, lt64k).astype(BF16), pn.astype(BF16), TN_DIMS, preferred_element_type=F32)
            for c2 in range(4):
                oc = _pair_chunk(ot, c2)
                cols = slice((4 * h + c2) * LANES, (4 * h + c2 + 1) * LANES)
                zc = z_ref[:, cols]
                o_ref[:, cols] = oc
                g_ref[:, cols] = (oc * (zc * _sigmoid(zc))).astype(BF16)

        @pl.when((b == b_loc - 1) & (i == nb - 1))
        def _():
            _gather_wait(sh_in, sh_out, sems)

    cur = lambda b, i: b * nb + i
    prev = lambda b, i: b * nb + jnp.maximum(i - 1, 0)
    wide = lambda cb: pl.BlockSpec((BLK, ATTN_WIDTH), lambda b, i: (cur(b, i), cb))
    kv = lambda rowf, cb: pl.BlockSpec((BLK, LANES), lambda b, i: (rowf(b, i), cb))
    hbm = pl.BlockSpec(memory_space=pl.ANY)
    in_specs = [pl.BlockSpec(memory_space=pltpu.SMEM), wide(0), wide(1),
                kv(cur, _KCOL // LANES), kv(prev, _KCOL // LANES), kv(cur, _VCOL // LANES), kv(prev, _VCOL // LANES),
                kv(cur, 0), kv(cur, 0), kv(prev, 0), kv(prev, 0)] + [hbm] * nsh
    return pl.pallas_call(
        body, name="attn_fwd", grid=(b_loc, nb), in_specs=in_specs, out_specs=[wide(0), wide(0)] + [hbm] * nsh,
        out_shape=[jax.ShapeDtypeStruct((n, ATTN_WIDTH), F32), jax.ShapeDtypeStruct((n, ATTN_WIDTH), BF16)] + _gather_shapes(shards),
        scratch_shapes=_gather_sems(nsh), compiler_params=_params(("arbitrary", "arbitrary"), 48),
    )(sinks, p, p, p, p, p, p, ct, st, ct, st, *shards)


def _attn_bwd(p, ct, st, sinks, o, dg, b_loc, nb, parts):
    n = p.shape[0]
    nparts = len(parts)

    def body(sink_ref, q_ref, z_ref, kc_ref, kp_ref, vc_ref, vp_ref, cc_ref, sc_ref, cp_ref, sp_ref, o_ref, dg_ref, *rest):
        part_refs, (dp_ref, ds_ref), land_refs = rest[:nparts], rest[nparts:nparts + 2], rest[nparts + 2:2 * nparts + 2]
        dq_s, dz_s, dk_s, dv_s = rest[2 * nparts + 2:2 * nparts + 6]
        sems = rest[2 * nparts + 6:]
        b, i = pl.program_id(0), pl.program_id(1)

        @pl.when((b == 0) & (i == 0))
        def _():
            _scatter_start(part_refs, land_refs, sems)

        @pl.when((b == b_loc - 1) & (i == nb))
        def _():
            _scatter_wait(part_refs, land_refs, sems)

        lane = lax.broadcasted_iota(jnp.int32, (BLK, LANES), 1)
        lm = lane & (HEAD_DIM - 1)

        @pl.when((b == 0) & (i == 0))
        def _():
            ds_ref[...] = jnp.zeros_like(ds_ref)

        @pl.when(i < nb)
        def _compute():
            cc, sc = cc_ref[...], sc_ref[...]
            kcat = jnp.concatenate([_rope_apply(kp_ref[...], cp_ref[...], sp_ref[...], lm),
                                    _rope_apply(kc_ref[...], cc, sc, lm)], axis=0)
            vcat = jnp.concatenate([vp_ref[...], vc_ref[...]], axis=0)
            qr = [_rope_apply(q_ref[:, c * LANES:(c + 1) * LANES], cc, sc, lm) * (HEAD_DIM ** -0.5) for c in range(8)]
            valid = _attn_mask_t(i)
            lt64, lt64k = lane < HEAD_DIM, lax.broadcasted_iota(jnp.int32, (2 * BLK, LANES), 1) < HEAD_DIM
            do_chunks, doo_chunks, dz_chunks = [], [], []
            for c in range(8):
                cols = slice(c * LANES, (c + 1) * LANES)
                zc, oc, dgc = z_ref[:, cols], o_ref[:, cols], dg_ref[:, cols]
                sg = _sigmoid(zc)
                do_chunks.append(dgc * (zc * sg))
                dz_chunks.append(dgc * oc * (sg * (1.0 + zc * (1.0 - sg))))
                doo_chunks.append(do_chunks[c] * oc)
            dq_chunks = [None] * 8
            dk_h, dv_h = [], []
            ds_acc = jnp.zeros((SUBLANES, LANES), F32)
            tile_lane = lax.broadcasted_iota(jnp.int32, (SUBLANES, LANES), 1)
            tile_row = lax.broadcasted_iota(jnp.int32, (SUBLANES, LANES), 0)
            ones8 = jnp.ones((SUBLANES, LANES), BF16)
            for h in range(N_KV):
                qst = _head_stack(qr, h, lt64).astype(BF16)
                kext = _dup_half(kcat, h, lt64k)
                vext = _dup_half(vcat, h, lt64k).astype(BF16)
                pn, psink = _attn_probs_t(qst, kext.astype(BF16), h, sink_ref, valid)
                do_bf = _head_stack(do_chunks, h, lt64).astype(BF16)
                delta = sum(lax.dot_general(ones8, part, NT_DIMS, preferred_element_type=F32)
                            for part in _split3(_head_stack(doo_chunks, h, lt64)))[0:1, :]
                dpt = lax.dot_general(vext, do_bf, NT_DIMS, preferred_element_type=F32)
                dst = (pn * (dpt - delta)).astype(BF16)
                sink_term = psink * delta
                for g in range(GROUP):
                    val = -jnp.sum(sink_term[:, g * BLK:(g + 1) * BLK])
                    ds_acc = ds_acc + jnp.where((tile_lane == h * GROUP + g) & (tile_row == 0), val, 0.0)
                dqt = jnp.dot(kext.T.astype(BF16), dst, preferred_element_type=F32) * (HEAD_DIM ** -0.5)
                dk_ext = jnp.dot(dst, qst, preferred_element_type=F32)
                dv_ext = jnp.dot(pn.astype(BF16), do_bf, preferred_element_type=F32)
                dk_h.append(dk_ext + pltpu.roll(dk_ext, HEAD_DIM, 1))
                dv_h.append(dv_ext + pltpu.roll(dv_ext, HEAD_DIM, 1))
                for c2 in range(4):
                    dq_chunks[4 * h + c2] = _rope_bwd(_pair_chunk(dqt, c2), cc, sc, lm)
            ds_ref[...] += ds_acc
            dk_full = jnp.where(lt64k, dk_h[0], dk_h[1])
            dv_full = jnp.where(lt64k, dv_h[0], dv_h[1])

            @pl.when(i >= 1)
            def _emit():
                dp_ref[:, _QCOL:_QCOL + ATTN_WIDTH] = dq_s[...]
                dp_ref[:, _ZCOL:_ZCOL + ATTN_WIDTH] = dz_s[...]
                dp_ref[:, _KCOL:_KCOL + KV_WIDTH] = _rope_bwd(dk_s[...] + dk_full[:BLK], cp_ref[...], sp_ref[...], lm)
                dp_ref[:, _VCOL:_VCOL + KV_WIDTH] = dv_s[...] + dv_full[:BLK]

            for c in range(8):
                dq_s[:, c * LANES:(c + 1) * LANES] = dq_chunks[c]
                dz_s[:, c * LANES:(c + 1) * LANES] = dz_chunks[c]
            dk_s[...] = dk_full[BLK:]
            dv_s[...] = dv_full[BLK:]

        @pl.when(i == nb)
        def _final():
            dp_ref[:, _QCOL:_QCOL + ATTN_WIDTH] = dq_s[...]
            dp_ref[:, _ZCOL:_ZCOL + ATTN_WIDTH] = dz_s[...]
            dp_ref[:, _KCOL:_KCOL + KV_WIDTH] = _rope_bwd(dk_s[...], cc_ref[...], sc_ref[...], lm)
            dp_ref[:, _VCOL:_VCOL + KV_WIDTH] = dv_s[...]

    cur = lambda b, i: b * nb + jnp.minimum(i, nb - 1)
    prev = lambda b, i: b * nb + jnp.maximum(jnp.minimum(i, nb - 1) - 1, 0)
    emit = lambda b, i: b * nb + jnp.maximum(i - 1, 0)
    hbm = pl.BlockSpec(memory_space=pl.ANY)
    wide = lambda cb: pl.BlockSpec((BLK, ATTN_WIDTH), lambda b, i: (cur(b, i), cb))
    kv = lambda rowf, cb: pl.BlockSpec((BLK, LANES), lambda b, i: (rowf(b, i), cb))
    in_specs = [pl.BlockSpec(memory_space=pltpu.SMEM), wide(0), wide(1),
                kv(cur, _KCOL // LANES), kv(prev, _KCOL // LANES), kv(cur, _VCOL // LANES), kv(prev, _VCOL // LANES),
                kv(cur, 0), kv(cur, 0), kv(prev, 0), kv(prev, 0), wide(0), wide(0)] + [hbm] * nparts
    out_specs = [pl.BlockSpec((BLK, ATTN_IN), lambda b, i: (emit(b, i), 0)),
                 pl.BlockSpec((SUBLANES, LANES), lambda b, i: (0, 0))] + [hbm] * nparts
    return pl.pallas_call(
        body, name="attn_bwd", grid=(b_loc, nb + 1), in_specs=in_specs, out_specs=out_specs,
        out_shape=[jax.ShapeDtypeStruct((n, ATTN_IN), F32), jax.ShapeDtypeStruct((SUBLANES, LANES), F32)] + _scatter_lands(parts),
        scratch_shapes=[pltpu.VMEM((BLK, ATTN_WIDTH), F32), pltpu.VMEM((BLK, ATTN_WIDTH), F32),
                        pltpu.VMEM((BLK, KV_WIDTH), F32), pltpu.VMEM((BLK, KV_WIDTH), F32)] + _scatter_sems(nparts),
        compiler_params=_params(("arbitrary", "arbitrary"), 48),
    )(sinks, p, p, p, p, p, p, ct, st, ct, st, o, dg, *parts)


_CUM_ROWS = 256
HALF = CH // 2
_ROW0 = [SUBLANES * (s // SUBLANES) for s in range(CH)]
_ROW1 = [HALF * (s // HALF + 1) for s in range(CH)]
_ROWS_OF = [_ROW1[s] - _ROW0[s] for s in range(CH)]
_OFF_OF = [sum(_ROWS_OF[:s]) for s in range(CH)]


def _tri(lower):
    r = lax.broadcasted_iota(jnp.int32, (_CUM_ROWS, _CUM_ROWS), 0)
    c = lax.broadcasted_iota(jnp.int32, (_CUM_ROWS, _CUM_ROWS), 1)
    same = (r ^ c) < CH
    return (same & ((c <= r) if lower else (c >= r))).astype(BF16)


def _gates(qp, fp, lb):
    e = jnp.exp(-jnp.abs(fp))
    log_sig = jnp.minimum(fp, 0.0) - jnp.log(1.0 + e)
    a = jnp.log(lb)
    c = jnp.log(1.0 - lb) + log_sig
    g = jnp.maximum(a, c) + jnp.log(1.0 + jnp.exp(-jnp.abs(a - c)))
    sig_neg = jnp.where(fp >= 0, e, 1.0) / (1.0 + e)
    return qp * _sigmoid(qp), g, (1.0 - lb) * sig_neg, sig_neg


def _pair_rows(bc, s):
    return jnp.exp(jnp.minimum(bc[_ROW0[s]:_ROW1[s], :] - bc[s:s + 1, :], 0.0))


def _cross_half(q, k, bc):
    r = bc[HALF - 1:HALF, :]
    e1, e0 = jnp.exp(bc[HALF:, :] - r), jnp.exp(r - bc[:HALF, :])
    return q[HALF:, :] * e1, e1, k[:HALF, :] * e0, e0


HP = 4
_HW = HP * REC_K


def _hgrn_specs(tb, nt, reverse):
    tmap = (lambda t: nt - 1 - t) if reverse else (lambda t: t)
    groups = REC_HEADS // HP
    blk = lambda cb: pl.BlockSpec((tb, _HW), lambda h, b, t: (b * nt + tmap(t), cb * groups + h))
    head = pl.BlockSpec((tb, _HW), lambda h, b, t: (b * nt + tmap(t), h))
    lbs = pl.BlockSpec((1, _HW), lambda h, b, t: (0, h))
    gws = pl.BlockSpec((1, REC_K), lambda h, b, t: (0, 0))
    hist = pl.BlockSpec((HP, 1, tb // CH, REC_K, REC_K), lambda h, b, t: (h, b, tmap(t), 0, 0))
    return blk, head, lbs, gws, hist


def _in_stages(heads):
    live = list(heads)
    while live:
        live = [g for g in live if next(g, live) is not live]


def _cumsum_chunks(tri, x, out_ref, tb):
    for r in range(0, tb, _CUM_ROWS):
        out_ref[r:r + _CUM_ROWS, :] = _dot3(tri, x[r:r + _CUM_ROWS, :])


def _hgrn_fwd(p, lb, gw, b_loc, t_len):
    n = p.shape[0]
    tb = min(TM, t_len)
    nt, nck = t_len // tb, tb // CH

    def body(qp_ref, fp_ref, i_ref, z_ref, lb_ref, gw_ref, oraw_ref, g_ref, sh_ref, q_s, k_s, b_s, o_s, st_ref):
        @pl.when(pl.program_id(2) == 0)
        def _():
            st_ref[...] = jnp.zeros_like(st_ref)

        qv, g, kk, _ = _gates(qp_ref[...], fp_ref[...], lb_ref[...])
        q_s[...] = qv
        k_s[...] = kk
        _cumsum_chunks(_tri(True), g, b_s, tb)
        ones = jnp.ones((REC_K, REC_K), BF16)
        sub = lax.broadcasted_iota(jnp.int32, (SUBLANES, REC_K), 0)

        def chunk(c, carry):
            _in_stages([head_chunk(c, hp) for hp in range(HP)])
            return carry

        def head_chunk(c, hp):
            rs = pl.ds(pl.multiple_of(c * CH, CH), CH)
            cs = slice(hp * REC_K, (hp + 1) * REC_K)
            q, k, bc, v = q_s[rs, cs], k_s[rs, cs], b_s[rs, cs], i_ref[rs, cs]
            bl = bc[CH - 1:CH, :]
            st = st_ref[hp]
            sh_ref[hp, 0, c] = st
            o = lax.dot_general((q * jnp.exp(bc)).astype(BF16), st.astype(BF16), NT_DIMS, preferred_element_type=F32)
            w = jnp.concatenate([q[_ROW0[s]:_ROW1[s], :] * _pair_rows(bc, s) * k[s:s + 1, :] for s in range(CH)], axis=0)
            a = jnp.dot(w.astype(BF16), ones, preferred_element_type=F32)
            qe1, _, ke0, _ = _cross_half(q, k, bc)
            s10 = lax.dot_general(qe1.astype(BF16), ke0.astype(BF16), NT_DIMS, preferred_element_type=F32)
            kd = k * jnp.exp(bl - bc)
            st_new = lax.dot_general(v.astype(BF16), kd.astype(BF16), TN_DIMS, preferred_element_type=F32)
            yield
            o_cross = jnp.dot(s10.astype(BF16), v[:HALF, :].astype(BF16), preferred_element_type=F32)
            yield
            st_ref[hp] = st * jnp.exp(bl) + st_new
            acc = [jnp.zeros((SUBLANES, REC_K), F32) for _ in range(CH // SUBLANES)]
            for s in range(CH):
                j = s // SUBLANES
                vs = v[s:s + 1, :]
                for jj in range(j, _ROW1[s] // SUBLANES):
                    blk = a[_OFF_OF[s] + (jj - j) * SUBLANES:_OFF_OF[s] + (jj - j + 1) * SUBLANES, :]
                    if jj == j:
                        blk = jnp.where(sub >= s - j * SUBLANES, blk, 0.0)
                    acc[jj] = acc[jj] + blk * vs
            o_s[rs, cs] = o + jnp.concatenate(acc, axis=0) + jnp.concatenate([jnp.zeros((HALF, REC_K), F32), o_cross], axis=0)

        lax.fori_loop(0, nck, chunk, 0)
        oraw_ref[...] = o_s[...]
        for hp in range(HP):
            cs = slice(hp * REC_K, (hp + 1) * REC_K)
            o, zc = o_s[:, cs], z_ref[:, cs]
            on = (o * lax.rsqrt(jnp.mean(o * o, axis=-1, keepdims=True) + NORM_EPS)) * gw_ref[...]
            g_ref[:, cs] = (on * (zc * _sigmoid(zc))).astype(BF16)

    blk, head, lbs, gws, hist = _hgrn_specs(tb, nt, False)
    return pl.pallas_call(
        body, name="hgrn_fwd", grid=(REC_HEADS // HP, b_loc, nt),
        in_specs=[blk(0), blk(1), blk(2), blk(3), lbs, gws], out_specs=[head, head, hist],
        out_shape=[jax.ShapeDtypeStruct((n, 1024), F32), jax.ShapeDtypeStruct((n, 1024), BF16),
                   jax.ShapeDtypeStruct((REC_HEADS, b_loc, t_len // CH, REC_K, REC_K), F32)],
        scratch_shapes=[pltpu.VMEM((tb, _HW), F32)] * 4 + [pltpu.VMEM((HP, REC_K, REC_K), F32)],
        compiler_params=_params(("arbitrary", "arbitrary", "arbitrary"), 48),
    )(p, p, p, p, lb, gw)


def _hgrn_bwd(p, lb, gw, oraw, sh, dg, b_loc, t_len):
    n = p.shape[0]
    tb = min(TM, t_len)
    nt, nck = t_len // tb, tb // CH

    def body(qp_ref, fp_ref, i_ref, z_ref, lb_ref, gw_ref, oraw_ref, dg_ref, sh_ref,
             dq_ref, df_ref, di_ref, dz_ref, dlb_ref, dgw_ref,
             q_s, k_s, b_s, do_s, dqv_s, dk_s, db_s, rowk_s, rowv_s, dst_ref):
        b, t = pl.program_id(1), pl.program_id(2)

        @pl.when(t == 0)
        def _():
            dst_ref[...] = jnp.zeros_like(dst_ref)

        @pl.when((b == 0) & (t == 0))
        def _():
            dlb_ref[...] = jnp.zeros_like(dlb_ref)
            dgw_ref[...] = jnp.zeros_like(dgw_ref)

        lbv, qp, fp = lb_ref[...], qp_ref[...], fp_ref[...]
        qv, g, kk, sig_neg = _gates(qp, fp, lbv)
        q_s[...] = qv
        k_s[...] = kk
        _cumsum_chunks(_tri(True), g, b_s, tb)
        gwv = gw_ref[...]
        for hp in range(HP):
            cs = slice(hp * REC_K, (hp + 1) * REC_K)
            o, zc, dgv = oraw_ref[:, cs], z_ref[:, cs], dg_ref[:, cs]
            rn = lax.rsqrt(jnp.mean(o * o, axis=-1, keepdims=True) + NORM_EPS)
            on = o * rn
            sgz = _sigmoid(zc)
            dz_ref[:, cs] = (dgv * (on * gwv) * (sgz * (1.0 + zc * (1.0 - sgz)))).astype(BF16)
            dpre = dgv * (zc * sgz)
            dgw_ref[hp] += jnp.sum(dpre * on, axis=0, keepdims=True)
            don = dpre * gwv
            do_s[:, cs] = rn * (don - on * jnp.mean(don * on, axis=-1, keepdims=True))

        ones = jnp.ones((REC_K, REC_K), BF16)
        sub = lax.broadcasted_iota(jnp.int32, (SUBLANES, REC_K), 0)
        rowid = lax.broadcasted_iota(jnp.int32, (CH, REC_K), 0)
        ngrp = CH // SUBLANES

        def chunk(ci, carry):
            _in_stages([head_chunk(nck - 1 - ci, hp) for hp in range(HP)])
            return carry

        def head_chunk(c, hp):
            rs = pl.ds(pl.multiple_of(c * CH, CH), CH)
            cs = slice(hp * REC_K, (hp + 1) * REC_K)
            rowk, rowv = rowk_s.at[hp], rowv_s.at[hp]
            q, k, bc, v, do = q_s[rs, cs], k_s[rs, cs], b_s[rs, cs], i_ref[rs, cs], do_s[rs, cs]
            bl = bc[CH - 1:CH, :]
            st, dst = sh_ref[hp, 0, c], dst_ref[hp]
            eb, ebl, ekd = jnp.exp(bc), jnp.exp(bl), jnp.exp(bl - bc)
            qe, kd = q * eb, k * ekd
            do_bf, dst_bf = do.astype(BF16), dst.astype(BF16)
            dqe = jnp.dot(do_bf, st.astype(BF16), preferred_element_type=F32)
            dkd = jnp.dot(v.astype(BF16), dst_bf, preferred_element_type=F32)
            dv = lax.dot_general(kd.astype(BF16), dst_bf, NT_DIMS, preferred_element_type=F32)
            dst_new = lax.dot_general(do_bf, qe.astype(BF16), TN_DIMS, preferred_element_type=F32)
            dec = [_pair_rows(bc, s) for s in range(CH)]
            w = jnp.concatenate([q[_ROW0[s]:_ROW1[s], :] * dec[s] * k[s:s + 1, :] for s in range(CH)], axis=0)
            x = jnp.concatenate([do[_ROW0[s]:_ROW1[s], :] * v[s:s + 1, :] for s in range(CH)], axis=0)
            a = jnp.dot(w.astype(BF16), ones, preferred_element_type=F32)
            da = jnp.dot(x.astype(BF16), ones, preferred_element_type=F32)
            qe1, e1, ke0, e0 = _cross_half(q, k, bc)
            qe1_bf, ke0_bf = qe1.astype(BF16), ke0.astype(BF16)
            do1_bf, v0_bf = do[HALF:, :].astype(BF16), v[:HALF, :].astype(BF16)
            s_t = lax.dot_general(ke0_bf, qe1_bf, NT_DIMS, preferred_element_type=F32)
            ds_x = lax.dot_general(do1_bf, v0_bf, NT_DIMS, preferred_element_type=F32)
            ds_t = lax.dot_general(v0_bf, do1_bf, NT_DIMS, preferred_element_type=F32)
            yield
            dqe1 = jnp.dot(ds_x.astype(BF16), ke0_bf, preferred_element_type=F32)
            dke0 = jnp.dot(ds_t.astype(BF16), qe1_bf, preferred_element_type=F32)
            dv1 = jnp.dot(s_t.astype(BF16), do1_bf, preferred_element_type=F32)
            yield
            dbl = jnp.sum(dkd * kd, axis=0, keepdims=True) + ebl * jnp.sum(st * dst, axis=0, keepdims=True)
            dst_ref[hp] = dst * ebl + dst_new
            dq_acc = [jnp.zeros((SUBLANES, REC_K), F32) for _ in range(ngrp)]
            for s in range(CH):
                j = s // SUBLANES
                r0 = j * SUBLANES
                ks = k[s:s + 1, :]
                uk = jnp.zeros((SUBLANES, REC_K), F32)
                uv = jnp.zeros((SUBLANES, REC_K), F32)
                for jj in range(j, _ROW1[s] // SUBLANES):
                    lo, hi = _OFF_OF[s] + (jj - j) * SUBLANES, _OFF_OF[s] + (jj - j + 1) * SUBLANES
                    a_blk, da_blk = a[lo:hi, :], da[lo:hi, :]
                    if jj == j:
                        keep = sub >= s - r0
                        a_blk, da_blk = jnp.where(keep, a_blk, 0.0), jnp.where(keep, da_blk, 0.0)
                    rows = slice(jj * SUBLANES, (jj + 1) * SUBLANES)
                    tt = da_blk * dec[s][(jj - j) * SUBLANES:(jj - j + 1) * SUBLANES, :]
                    dq_acc[jj] = dq_acc[jj] + tt * ks
                    uk = uk + tt * q[rows, :]
                    uv = uv + a_blk * do[rows, :]
                rowk[s:s + 1, :] = jnp.sum(uk, axis=0, keepdims=True)
                rowv[s:s + 1, :] = jnp.sum(uv, axis=0, keepdims=True)
            zero_half = jnp.zeros((HALF, REC_K), F32)
            dq_x = jnp.concatenate([zero_half, dqe1 * e1], axis=0)
            dk_x = jnp.concatenate([dke0 * e0, zero_half], axis=0)
            dv_x = jnp.concatenate([dv1, zero_half], axis=0)
            db_x = jnp.concatenate([-(dke0 * ke0), dqe1 * qe1], axis=0)
            dq_in = jnp.concatenate(dq_acc, axis=0)
            dk_in = rowk[...]
            dqv_s[rs, cs] = dqe * eb + dq_in + dq_x
            dk_s[rs, cs] = dkd * ekd + dk_in + dk_x
            di_ref[rs, cs] = (dv + rowv[...] + dv_x).astype(BF16)
            db = dqe * qe - dkd * kd + q * dq_in - k * dk_in + db_x
            db_s[rs, cs] = db + jnp.where(rowid == CH - 1, dbl, 0.0)

        lax.fori_loop(0, nck, chunk, 0)
        up = _tri(False)
        sgq = _sigmoid(qp)
        dq_ref[...] = (dqv_s[...] * (sgq * (1.0 + qp * (1.0 - sgq)))).astype(BF16)
        dlb_acc = jnp.zeros((1, _HW), F32)
        for r in range(0, tb, _CUM_ROWS):
            rows = slice(r, r + _CUM_ROWS)
            dgl = _dot3(up, db_s[rows, :])
            dfg = dgl * jnp.exp(-g[rows, :]) - dk_s[rows, :]
            sn = sig_neg[rows, :]
            df_ref[rows, :] = (dfg * (1.0 - lbv) * (1.0 - sn) * sn).astype(BF16)
            dlb_acc = dlb_acc + jnp.sum(dfg * sn, axis=0, keepdims=True)
        dlb_ref[...] += dlb_acc

    blk, head, lbs, gws, hist = _hgrn_specs(tb, nt, True)
    out_specs = [head, head, head, head, lbs, pl.BlockSpec((HP, 1, REC_K), lambda h, b, t: (h, 0, 0))]
    out_shape = [jax.ShapeDtypeStruct((n, 1024), BF16)] * 4 + [
        jax.ShapeDtypeStruct((1, 1024), F32), jax.ShapeDtypeStruct((REC_HEADS, 1, REC_K), F32)]
    return pl.pallas_call(
        body, name="hgrn_bwd", grid=(REC_HEADS // HP, b_loc, nt),
        in_specs=[blk(0), blk(1), blk(2), blk(3), lbs, gws, head, head, hist], out_specs=out_specs, out_shape=out_shape,
        scratch_shapes=[pltpu.VMEM((tb, _HW), F32)] * 7 + [pltpu.VMEM((HP, CH, REC_K), F32)] * 2 + [pltpu.VMEM((HP, REC_K, REC_K), F32)],
        compiler_params=_params(("arbitrary", "arbitrary", "arbitrary"), 48),
    )(p, p, p, p, lb, gw, oraw, dg, sh)


def _postnorm_bwd_nt(dxo, y, qw, w, has_bias, name):
    n = dxo.shape[0]

    def body(dx_ref, y_ref, qw_ref, w_ref, dg_ref, dy_ref, dqw_ref, db_ref):
        @pl.when(pl.program_id(0) == 0)
        def _():
            dqw_ref[...] = jnp.zeros_like(dqw_ref)
            db_ref[...] = jnp.zeros_like(db_ref)

        yv, dxv = y_ref[...], dx_ref[...]
        r = lax.rsqrt(jnp.mean(yv * yv, axis=-1, keepdims=True) + NORM_EPS)
        u = yv * r
        du = dxv * qw_ref[...]
        dy = r * (du - u * jnp.mean(du * u, axis=-1, keepdims=True))
        dqw_ref[...] += jnp.sum(dxv * u, axis=0, keepdims=True)
        if has_bias:
            db_ref[...] += jnp.sum(dy, axis=0, keepdims=True)
        dyb = dy.astype(BF16)
        dy_ref[...] = dyb
        dg_ref[...] = lax.dot_general(dyb, w_ref[...], NT_DIMS, preferred_element_type=F32)

    rows = pl.BlockSpec((TM, D_MODEL), lambda i: (i, 0))
    const = lambda shape: pl.BlockSpec(shape, lambda i: (0, 0))
    return pl.pallas_call(
        body, name=name, grid=(n // TM,), in_specs=[rows, rows, const((1, D_MODEL)), const((D_MODEL, D_MODEL))],
        out_specs=[rows, rows, const((1, D_MODEL)), const((1, D_MODEL))],
        out_shape=[jax.ShapeDtypeStruct((n, D_MODEL), F32), jax.ShapeDtypeStruct((n, D_MODEL), BF16),
                   jax.ShapeDtypeStruct((1, D_MODEL), F32), jax.ShapeDtypeStruct((1, D_MODEL), F32)],
        compiler_params=_params(("arbitrary",), 48),
    )(dxo, y, qw, w)


def _nt_prenorm_bwd(dps, w, x, pw, dxo, has_bias, name, parts=()):
    n = x.shape[0]
    widths = [d.shape[1] for d in dps]
    m = sum(widths)
    npieces, nparts, steps = len(dps), len(parts), n // TM

    def body(*refs):
        dp_refs = refs[:npieces]
        w_ref, x_ref, pw_ref, dxo_ref = refs[npieces:npieces + 4]
        part_refs = refs[npieces + 4:npieces + 4 + nparts]
        dx_ref, dpw_ref, db_ref = refs[npieces + 4 + nparts:npieces + 7 + nparts]
        land_refs = refs[npieces + 7 + nparts:npieces + 7 + 2 * nparts]
        sems = refs[npieces + 7 + 2 * nparts:]

        @pl.when(pl.program_id(0) == 0)
        def _():
            dpw_ref[...] = jnp.zeros_like(dpw_ref)
            db_ref[...] = jnp.zeros_like(db_ref)
            if nparts:
                _scatter_start(part_refs, land_refs, sems)

        dh = jnp.zeros((TM, D_MODEL), F32)
        off = 0
        for dp_ref, wd in zip(dp_refs, widths):
            cn = _col_chunk(wd)
            for j in range(0, wd, cn):
                dpc = dp_ref[:, j:j + cn]
                if has_bias:
                    db_ref[:, off + j:off + j + cn] += jnp.sum(dpc, axis=0, keepdims=True)
                dh = dh + lax.dot_general(dpc.astype(BF16), w_ref[:, off + j:off + j + cn], NT_DIMS, preferred_element_type=F32)
            off += wd
        xv = x_ref[...]
        r = lax.rsqrt(jnp.mean(xv * xv, axis=-1, keepdims=True) + NORM_EPS)
        xn = xv * r
        dpw_ref[...] += jnp.sum(dh * xn, axis=0, keepdims=True)
        dxn = dh * pw_ref[...]
        dx_ref[...] = dxo_ref[...] + r * (dxn - xn * jnp.mean(dxn * xn, axis=-1, keepdims=True))

        if nparts:
            @pl.when(pl.program_id(0) == steps - 1)
            def _():
                _scatter_wait(part_refs, land_refs, sems)

    rows = pl.BlockSpec((TM, D_MODEL), lambda i: (i, 0))
    const = lambda shape: pl.BlockSpec(shape, lambda i: (0, 0))
    hbm = pl.BlockSpec(memory_space=pl.ANY)
    in_specs = ([pl.BlockSpec((TM, wd), lambda i: (i, 0)) for wd in widths] + [const((D_MODEL, m)), rows, const((1, D_MODEL)), rows]
                + [hbm] * nparts)
    return pl.pallas_call(
        body, name=name, grid=(steps,), in_specs=in_specs,
        out_specs=[rows, const((1, D_MODEL)), const((1, m))] + [hbm] * nparts,
        out_shape=[jax.ShapeDtypeStruct((n, D_MODEL), F32), jax.ShapeDtypeStruct((1, D_MODEL), F32),
                   jax.ShapeDtypeStruct((1, m), F32)] + _scatter_lands(parts),
        scratch_shapes=_scatter_sems(nparts) if nparts else [],
        compiler_params=_params(("arbitrary",), 56),
    )(*dps, w, x, pw, dxo, *parts)


def _matmul_tn(a, b, name):
    n, k = a.shape
    m = b.shape[1]
    tk, tm, tn = k, _col_chunk(m), 1024 if n % 1024 == 0 else n

    def body(a_ref, b_ref, o_ref):
        @pl.when(pl.program_id(2) == 0)
        def _():
            o_ref[...] = jnp.zeros_like(o_ref)

        o_ref[...] += lax.dot_general(a_ref[...], b_ref[...].astype(BF16), TN_DIMS, preferred_element_type=F32)

    return pl.pallas_call(
        body, name=name, grid=(k // tk, m // tm, n // tn),
        in_specs=[pl.BlockSpec((tn, tk), lambda i, j, l: (l, i)), pl.BlockSpec((tn, tm), lambda i, j, l: (l, j))],
        out_specs=pl.BlockSpec((tk, tm), lambda i, j, l: (i, j)),
        out_shape=jax.ShapeDtypeStruct((k, m), F32),
        compiler_params=_params(("arbitrary", "arbitrary", "arbitrary"), 48),
    )(a, b)


def _by_owner_cols(dw):
    k, m = dw.shape
    return dw.reshape(k, N_DEV, m // N_DEV).transpose(1, 0, 2)


def _own_and_bf16(part):
    return lax.dynamic_index_in_dim(part, _my_id(), 0, keepdims=False), part.astype(BF16)


def _step(x, pos_col, tgt, pre_w, post_w, wa_in, ba_in, sinks, wa_out_shard, ba_out, wr_in_shard, lb_logits, gnorm_w, wr_out_shard, b_loc, t_len):
    nb = t_len // BLK
    ct, st = _rope_tables(pos_col)
    lb = _lower_bound(lb_logits)
    p0, h0, ga_out = _norm_matmul(x, pre_w[0:1], wa_in, ba_in, "attn_in_proj", [wa_out_shard])
    wa_out = ga_out.reshape(ATTN_WIDTH, D_MODEL)
    o0, g0, gr_in, gr_out = _attn_fwd(p0, ct, st, sinks, b_loc, nb, [wr_in_shard, wr_out_shard])
    wr_in = gr_in.transpose(1, 0, 2).reshape(D_MODEL, REC_IN)
    wr_out = gr_out.reshape(1024, D_MODEL)
    y0, x1 = _outproj_postnorm(g0, wa_out, ba_out, x, post_w[0:1], None, "attn_out_proj")
    p1, h1 = _norm_matmul(x1, pre_w[1:2], wr_in, None, "rec_in_proj")
    o1, g1, sh = _hgrn_fwd(p1, lb, gnorm_w, b_loc, t_len)
    y1, dx2, loss_tile = _outproj_postnorm(g1, wr_out, None, x1, post_w[1:2], tgt, "rec_out_proj_loss")
    dg1, dy1, dpost1, _ = _postnorm_bwd_nt(dx2, y1, post_w[1:2], wr_out, False, "rec_out_bwd")
    d_wr_out = _matmul_tn(g1, dy1, "rec_w_out_grad")
    dq1, df1, di1, dz1, dlb, dgw = _hgrn_bwd(p1, lb, gnorm_w, o1, sh, dg1, b_loc, t_len)
    dps1 = [dq1, df1, di1, dz1]
    dx1, dpre1, _ = _nt_prenorm_bwd(dps1, wr_in, x1, pre_w[1:2], dx2, False, "rec_in_bwd")
    d_wr_in = [_matmul_tn(h1, dpk, f"rec_w_in_grad_{k}") for k, dpk in enumerate(dps1)]
    dg0, dy0, dpost0, dba_out = _postnorm_bwd_nt(dx1, y0, post_w[0:1], wa_out, True, "attn_out_bwd")
    d_wa_out = _matmul_tn(g0, dy0, "attn_w_out_grad")
    owns, wires = zip(*[_own_and_bf16(part) for part in (
        _by_owner_cols(jnp.concatenate(d_wr_in, axis=1)), d_wr_out.reshape(N_DEV, 1024 // N_DEV, D_MODEL),
        d_wa_out.reshape(N_DEV, ATTN_WIDTH // N_DEV, D_MODEL))])
    dp0, dsink_tile, *lands = _attn_bwd(p0, ct, st, sinks, o0, dg0, b_loc, nb, list(wires))
    d_wa_in = _matmul_tn(h0, dp0, "attn_w_in_grad")
    own_a_in, wire_a_in = _own_and_bf16(_by_owner_cols(_qkvz(d_wa_in)))
    dx0, dpre0, dba_in, land_a_in = _nt_prenorm_bwd([dp0], wa_in, x, pre_w[0:1], dx1, True, "attn_in_bwd", [wire_a_in])
    small = dict(pre=jnp.concatenate([dpre0, dpre1], axis=0), post=jnp.concatenate([dpost0, dpost1], axis=0),
                 ba_in=dba_in, sinks=dsink_tile[0:1, 0:N_HEADS], ba_out=dba_out, lb=dlb, gnorm=jnp.sum(dgw, axis=0))
    return loss_tile, dx0, list(zip(lands, owns)) + [(land_a_in, own_a_in)], small


def _my_id():
    return lax.axis_index("x") * 4 + lax.axis_index("y") * 2 + lax.axis_index("c")


def _peer(k):
    x, y, c = lax.axis_index("x"), lax.axis_index("y"), lax.axis_index("c")
    return (x ^ ((k >> 2) & 1), y ^ ((k >> 1) & 1), c ^ (k & 1))


def _peer_id(k):
    return _my_id() ^ k


def _all_gather(shards):
    nsh = len(shards)

    def body(*refs):
        ins, outs, sems = refs[:nsh], refs[nsh:2 * nsh], refs[2 * nsh:]
        _gather_start(ins, outs, sems)
        _gather_wait(ins, outs, sems)

    hbm = pl.BlockSpec(memory_space=pl.ANY)
    return pl.pallas_call(
        body, name="comm_all_gather", in_specs=[hbm] * nsh, out_specs=[hbm] * nsh,
        out_shape=_gather_shapes(shards), scratch_shapes=_gather_sems(nsh),
    )(*shards)


def _gather_shapes(shards):
    return [jax.ShapeDtypeStruct((N_DEV,) + s.shape, s.dtype) for s in shards]


def _gather_sems(nsh):
    return [pltpu.SemaphoreType.DMA((nsh, N_DEV - 1)), pltpu.SemaphoreType.DMA((nsh, N_DEV - 1)), pltpu.SemaphoreType.DMA((nsh,))]


def _gather_copies(ins, outs, sems, received):
    send_sems, recv_sems, local_sems = sems
    me = _my_id()
    local = [pltpu.make_async_copy(ins[a], outs[a].at[me], local_sems.at[a]) for a in range(len(ins))]
    remote = [pltpu.make_async_remote_copy(
        src_ref=ins[a], dst_ref=outs[a].at[_peer_id(k) if received else me], send_sem=send_sems.at[a, k - 1],
        recv_sem=recv_sems.at[a, k - 1], device_id=_peer(k), device_id_type=MESH)
        for a in range(len(ins)) for k in range(1, N_DEV)]
    return local, remote


def _gather_start(ins, outs, sems):
    local, sends = _gather_copies(ins, outs, sems, False)
    for cp in local + sends:
        cp.start()


def _gather_wait(ins, outs, sems):
    local, recvs = _gather_copies(ins, outs, sems, True)
    for cp in recvs:
        cp.wait_recv()
    for cp in recvs:
        cp.wait_send()
    for cp in local:
        cp.wait()


def _scatter_lands(parts):
    return [jax.ShapeDtypeStruct((N_DEV - 1,) + p.shape[1:], p.dtype) for p in parts]


def _scatter_sems(nparts):
    return [pltpu.SemaphoreType.DMA((nparts, N_DEV - 1)), pltpu.SemaphoreType.DMA((nparts, N_DEV - 1))]


def _scatter_copies(parts, lands, sems):
    send_sems, recv_sems = sems
    return [pltpu.make_async_remote_copy(
        src_ref=parts[a].at[_peer_id(k)], dst_ref=lands[a].at[k - 1], send_sem=send_sems.at[a, k - 1],
        recv_sem=recv_sems.at[a, k - 1], device_id=_peer(k), device_id_type=MESH)
        for a in range(len(parts)) for k in range(1, N_DEV)]


def _scatter_start(parts, lands, sems):
    for cp in _scatter_copies(parts, lands, sems):
        cp.start()


def _scatter_wait(parts, lands, sems):
    copies = _scatter_copies(parts, lands, sems)
    for cp in copies:
        cp.wait_recv()
    for cp in copies:
        cp.wait_send()


def _adamw(w, g, m, v):
    m2 = ADAM_B1 * m + (1.0 - ADAM_B1) * g
    v2 = ADAM_B2 * v + (1.0 - ADAM_B2) * (g * g)
    m_hat = m2 / (1.0 - ADAM_B1 ** ADAM_STEP)
    v_hat = v2 / (1.0 - ADAM_B2 ** ADAM_STEP)
    delta = -ADAM_LR * (m_hat / (jnp.sqrt(v_hat) + ADAM_EPS) + ADAM_WD * w)
    return delta, m2, v2


def _sum_adamw_rows(land_ref, own_ref, w_ref, m_ref, v_ref, out_refs):
    r, c = own_ref.shape
    rc = 64 if r % 64 == 0 else r
    me = _my_id()
    g_ref, d_ref, m2_ref, v2_ref = out_refs

    def rows(i, carry):
        rs = pl.ds(pl.multiple_of(i * rc, rc), rc)
        g = jnp.zeros((rc, c), F32)
        for dev in range(N_DEV):
            k = dev ^ me
            g = g + jnp.where(k == 0, own_ref[rs, :], land_ref[jnp.maximum(k - 1, 0), rs, :].astype(F32))
        delta, m2, v2 = _adamw(w_ref[rs, :], g, m_ref[rs, :], v_ref[rs, :])
        g_ref[rs, :] = g
        d_ref[rs, :] = delta
        m2_ref[rs, :] = m2
        v2_ref[rs, :] = v2
        return carry

    lax.fori_loop(0, r // rc, rows, 0)


def _sum_adamw(land, own, w, m, v, name):
    r, c = own.shape

    def body(land_ref, own_ref, w_ref, m_ref, v_ref, g_ref, d_ref, m2_ref, v2_ref):
        _sum_adamw_rows(land_ref, own_ref, w_ref, m_ref, v_ref, (g_ref, d_ref, m2_ref, v2_ref))

    vmem = pl.BlockSpec(memory_space=pltpu.VMEM)
    return pl.pallas_call(
        body, name=name, in_specs=[vmem] * 5, out_specs=[vmem] * 4, out_shape=[jax.ShapeDtypeStruct((r, c), F32)] * 4,
        compiler_params=_params(None, 56),
    )(land, own, w, m, v)


_SMALL = [("pre_norm_w", 2048), ("post_norm_w", 2048), ("attn_b_in", 2304), ("attn_sinks", 16), ("attn_b_out", 1024),
          ("rec_lb_logits", 2048), ("rec_gnorm_w", 128)]
_TILE = SUBLANES * LANES


def _small_rows(size):
    return -(-size // _TILE) * SUBLANES


_SMALL_OFF = {}
_r = 0
for _name, _size in _SMALL:
    _SMALL_OFF[_name] = _r
    _r += _small_rows(_size)
_SMALL_ROWS = _r


def _pack_small(pieces):
    out = []
    for name, size in _SMALL:
        flat = pieces[name].reshape(-1).astype(F32)
        out.append(jnp.pad(flat, (0, _small_rows(size) * LANES - size)).reshape(-1, LANES))
    return jnp.concatenate(out, axis=0)


def _unpack_small(packed, shapes):
    return {name: packed[_SMALL_OFF[name]:_SMALL_OFF[name] + _small_rows(size)].reshape(-1)[:size].reshape(shapes[name])
            for name, size in _SMALL}


def _small_allreduce_adamw(gpart, w, m, v):
    lb0 = _SMALL_OFF["rec_lb_logits"]

    def body(gp_ref, w_ref, m_ref, v_ref, g_ref, d_ref, m2_ref, v2_ref, land_ref, send_sems, recv_sems):
        me = _my_id()
        sent = []
        for k in range(1, N_DEV):
            cp = pltpu.make_async_remote_copy(src_ref=gp_ref, dst_ref=land_ref.at[k - 1], send_sem=send_sems.at[k - 1],
                                              recv_sem=recv_sems.at[k - 1], device_id=_peer(k), device_id_type=MESH)
            cp.start()
            sent.append(cp)
        for cp in sent:
            cp.wait_recv()
        for cp in sent:
            cp.wait_send()
        g = jnp.zeros((_SMALL_ROWS, LANES), F32)
        for dev in range(N_DEV):
            k = dev ^ me
            g = g + jnp.where(k == 0, gp_ref[...], land_ref[jnp.maximum(k - 1, 0)])
        g_ref[...] = g
        l0, l1 = w_ref[lb0:lb0 + SUBLANES, :], w_ref[lb0 + SUBLANES:lb0 + 2 * SUBLANES, :]
        mx = jnp.maximum(l0, l1)
        e0, e1 = jnp.exp(l0 - mx), jnp.exp(l1 - mx)
        p1 = e1 / (e0 + e1)
        dl1 = (1.0 - p1) * p1 * g[lb0:lb0 + SUBLANES, :]
        g_ref[lb0:lb0 + SUBLANES, :] = -dl1
        g_ref[lb0 + SUBLANES:lb0 + 2 * SUBLANES, :] = dl1
        delta, m2, v2 = _adamw(w_ref[...], g_ref[...], m_ref[...], v_ref[...])
        d_ref[...] = delta
        m2_ref[...] = m2
        v2_ref[...] = v2

    vmem = pl.BlockSpec(memory_space=pltpu.VMEM)
    return pl.pallas_call(
        body, name="comm_small_allreduce_adamw", in_specs=[vmem] * 4, out_specs=[vmem] * 4,
        out_shape=[jax.ShapeDtypeStruct((_SMALL_ROWS, LANES), F32)] * 4,
        scratch_shapes=[pltpu.VMEM((N_DEV - 1, _SMALL_ROWS, LANES), F32), pltpu.SemaphoreType.DMA((N_DEV - 1,)),
                        pltpu.SemaphoreType.DMA((N_DEV - 1,))],
    )(gpart, w, m, v)


def _qzkv(a):
    return jnp.concatenate([a[..., :1024], a[..., 1280:], a[..., 1024:1280]], axis=-1)


def _qkvz(a):
    return jnp.concatenate([a[..., :1024], a[..., 2048:], a[..., 1024:2048]], axis=-1)


def kernel(x, positions, pre_norm_w, post_norm_w, attn_w_in, attn_b_in, attn_sinks, attn_w_out, attn_b_out, rec_w_in, rec_lb_logits, rec_gnorm_w, rec_w_out, loss_target, m_pre_norm_w, m_post_norm_w, m_attn_w_in, m_attn_b_in, m_attn_sinks, m_attn_w_out, m_attn_b_out, m_rec_w_in, m_rec_lb_logits, m_rec_gnorm_w, m_rec_w_out, v_pre_norm_w, v_post_norm_w, v_attn_w_in, v_attn_b_in, v_attn_sinks, v_attn_w_out, v_attn_b_out, v_rec_w_in, v_rec_lb_logits, v_rec_gnorm_w, v_rec_w_out):
    b_loc, t_len, _ = x.shape
    n = b_loc * t_len
    ga_in, = _all_gather([attn_w_in[0].astype(BF16)])
    wa_in = _qzkv(ga_in.transpose(1, 0, 2).reshape(D_MODEL, ATTN_IN))

    loss_tile, dx, landed, small = _step(
        x.reshape(n, D_MODEL), positions.reshape(n, 1).astype(F32), loss_target.reshape(n, D_MODEL),
        pre_norm_w, post_norm_w, wa_in, _qzkv(attn_b_in), attn_sinks, attn_w_out[0].astype(BF16), attn_b_out,
        rec_w_in[0].astype(BF16), rec_lb_logits, rec_gnorm_w, rec_w_out[0].astype(BF16), b_loc, t_len)
    loss = lax.psum(loss_tile[0, 0], ("x", "y", "c"))

    lift = lambda outs: tuple(a[None] for a in outs)
    (l_r_in, o_r_in), (l_r_out, o_r_out), (l_a_out, o_a_out), (l_a_in, o_a_in) = landed
    r_a_in = lift(_sum_adamw(l_a_in, o_a_in, attn_w_in[0], m_attn_w_in[0], v_attn_w_in[0], "adamw_attn_w_in"))
    r_r_in = lift(_sum_adamw(l_r_in, o_r_in, rec_w_in[0], m_rec_w_in[0], v_rec_w_in[0], "adamw_rec_w_in"))
    r_r_out = lift(_sum_adamw(l_r_out, o_r_out, rec_w_out[0], m_rec_w_out[0], v_rec_w_out[0], "adamw_rec_w_out"))
    r_a_out = lift(_sum_adamw(l_a_out, o_a_out, attn_w_out[0], m_attn_w_out[0], v_attn_w_out[0], "adamw_attn_w_out"))

    gsmall = dict(pre_norm_w=small["pre"], post_norm_w=small["post"], attn_b_in=_qkvz(small["ba_in"]), attn_sinks=small["sinks"],
                  attn_b_out=small["ba_out"], rec_lb_logits=jnp.concatenate([small["lb"], jnp.zeros_like(small["lb"])], axis=0),
                  rec_gnorm_w=small["gnorm"])
    wsmall = dict(pre_norm_w=pre_norm_w, post_norm_w=post_norm_w, attn_b_in=attn_b_in, attn_sinks=attn_sinks,
                  attn_b_out=attn_b_out, rec_lb_logits=rec_lb_logits, rec_gnorm_w=rec_gnorm_w)
    msmall = dict(pre_norm_w=m_pre_norm_w, post_norm_w=m_post_norm_w, attn_b_in=m_attn_b_in, attn_sinks=m_attn_sinks,
                  attn_b_out=m_attn_b_out, rec_lb_logits=m_rec_lb_logits, rec_gnorm_w=m_rec_gnorm_w)
    vsmall = dict(pre_norm_w=v_pre_norm_w, post_norm_w=v_post_norm_w, attn_b_in=v_attn_b_in, attn_sinks=v_attn_sinks,
                  attn_b_out=v_attn_b_out, rec_lb_logits=v_rec_lb_logits, rec_gnorm_w=v_rec_gnorm_w)
    shapes = {k: a.shape for k, a in wsmall.items()}
    packed = _small_allreduce_adamw(_pack_small(gsmall), _pack_small(wsmall), _pack_small(msmall), _pack_small(vsmall))
    sg, sd, sm, sv = [_unpack_small(a, shapes) for a in packed]

    big = {"attn_w_in": r_a_in, "attn_w_out": r_a_out, "rec_w_in": r_r_in, "rec_w_out": r_r_out}
    order = ["pre_norm_w", "post_norm_w", "attn_w_in", "attn_b_in", "attn_sinks", "attn_w_out", "attn_b_out", "rec_w_in",
             "rec_lb_logits", "rec_gnorm_w", "rec_w_out"]
    outs = [loss, dx.reshape(b_loc, t_len, D_MODEL)]
    for idx, small_set in enumerate((sg, sd, sm, sv)):
        outs += [big[nm][idx] if nm in big else small_set[nm] for nm in order]
    return tuple(outs)
```

```python
import numpy as np
import jax
import jax.numpy as jnp
from jax import lax
from jax.experimental import pallas as pl
from jax.experimental.pallas import tpu as pltpu

F32, BF16 = jnp.float32, jnp.bfloat16
MESH = pl.DeviceIdType.MESH
N_DEV = 8

D_MODEL = 1024
N_HEADS, HEAD_DIM, N_KV, GROUP = 16, 64, 2, 8
ATTN_WIDTH, KV_WIDTH = 1024, 128
ATTN_IN = 2 * ATTN_WIDTH + 2 * KV_WIDTH
BLK = 128
ROPE_THETA, ROPE_HALF = 500000.0, 8
REC_HEADS, REC_K = 8, 128
REC_IN = 4 * 1024
CH = 32
NORM_EPS = 1e-6
ADAM_LR, ADAM_B1, ADAM_B2, ADAM_EPS, ADAM_WD, ADAM_STEP = 0.001, 0.9, 0.999, 1e-08, 0.01, 10

LANES, SUBLANES = 128, 8
TM = 512
NT_DIMS = (((1,), (1,)), ((), ()))
TN_DIMS = (((0,), (0,)), ((), ()))
MB = 2 ** 20


def _params(sem=None, vmem_mb=48, **kw):
    return pltpu.CompilerParams(dimension_semantics=sem, vmem_limit_bytes=vmem_mb * MB, **kw)


def _col_chunk(m):
    return 768 if m % 1024 else 1024


def _sigmoid(x):
    return 1.0 / (1.0 + jnp.exp(-x))


def _split3(x):
    hi = x.astype(BF16)
    r1 = x - hi.astype(F32)
    mid = r1.astype(BF16)
    lo = (r1 - mid.astype(F32)).astype(BF16)
    return hi, mid, lo


def _dot3(l_bf, x):
    hi, mid, lo = _split3(x)
    return (jnp.dot(l_bf, hi, preferred_element_type=F32) + jnp.dot(l_bf, mid, preferred_element_type=F32)
            + jnp.dot(l_bf, lo, preferred_element_type=F32))


def _rope_tables(pos_col):
    n = pos_col.shape[0]
    lane = np.arange(LANES) % HEAD_DIM
    inv = np.float32(ROPE_THETA) ** (-(np.arange(ROPE_HALF, dtype=np.float32) * np.float32(2.0) / np.float32(2 * ROPE_HALF)))
    freq = np.where(lane < 2 * ROPE_HALF, inv[lane % ROPE_HALF], 0.0).astype(np.float32)[None, :]
    sign = np.where(lane < ROPE_HALF, -1.0, np.where(lane < 2 * ROPE_HALF, 1.0, 0.0)).astype(np.float32)[None, :]

    def body(p_ref, f_ref, s_ref, c_out, s_out):
        ang = p_ref[...] * f_ref[...]
        c_out[...] = jnp.cos(ang)
        s_out[...] = jnp.sin(ang) * s_ref[...]

    row = pl.BlockSpec((TM, 1), lambda i: (i, 0))
    vec = pl.BlockSpec((1, LANES), lambda i: (0, 0))
    out = pl.BlockSpec((TM, LANES), lambda i: (i, 0))
    return pl.pallas_call(
        body, name="rope_tables", grid=(n // TM,), in_specs=[row, vec, vec], out_specs=[out, out],
        out_shape=[jax.ShapeDtypeStruct((n, LANES), F32)] * 2, compiler_params=_params(("arbitrary",)),
    )(pos_col, jnp.asarray(freq), jnp.asarray(sign))


def _rope_apply(xv, c, s, lm):
    partner = jnp.where(lm < ROPE_HALF, pltpu.roll(xv, LANES - ROPE_HALF, 1), pltpu.roll(xv, ROPE_HALF, 1))
    return xv * c + partner * s


def _rope_bwd(dy, c, s, lm):
    t = dy * s
    partner = jnp.where(lm < ROPE_HALF, pltpu.roll(t, LANES - ROPE_HALF, 1),
                        jnp.where(lm < 2 * ROPE_HALF, pltpu.roll(t, ROPE_HALF, 1), 0.0))
    return dy * c + partner


def _lower_bound(lb_logits):
    def body(l_ref, o_ref):
        l0, l1 = l_ref[0:1, :], l_ref[1:2, :]
        m = jnp.maximum(l0, l1)
        e0, e1 = jnp.exp(l0 - m), jnp.exp(l1 - m)
        o_ref[...] = e1 / (e0 + e1)

    return pl.pallas_call(body, name="lower_bound", out_shape=jax.ShapeDtypeStruct((1, lb_logits.shape[1]), F32))(lb_logits)


def _norm_matmul(x, pw, w, bias, name, shards=()):
    n, m = x.shape[0], w.shape[1]
    cn = _col_chunk(m)
    has_bias = bias is not None
    nsh, steps = len(shards), n // TM

    def body(*refs):
        refs = list(refs)
        x_ref, pw_ref, w_ref = refs[:3]
        b_ref = refs[3] if has_bias else None
        refs = refs[4 if has_bias else 3:]
        sh_in, (p_ref, h_ref), sh_out, sems = refs[:nsh], refs[nsh:nsh + 2], refs[nsh + 2:2 * nsh + 2], refs[2 * nsh + 2:]
        if nsh:
            @pl.when(pl.program_id(0) == 0)
            def _():
                _gather_start(sh_in, sh_out, sems)

        xv = x_ref[...]
        r = lax.rsqrt(jnp.mean(xv * xv, axis=-1, keepdims=True) + NORM_EPS)
        h = ((xv * r) * pw_ref[...]).astype(BF16)
        h_ref[...] = h
        for j in range(0, m, cn):
            acc = jnp.dot(h, w_ref[:, j:j + cn], preferred_element_type=F32)
            if has_bias:
                acc = acc + b_ref[:, j:j + cn]
            p_ref[:, j:j + cn] = acc

        if nsh:
            @pl.when(pl.program_id(0) == steps - 1)
            def _():
                _gather_wait(sh_in, sh_out, sems)

    rows = pl.BlockSpec((TM, D_MODEL), lambda i: (i, 0))
    const = lambda shape: pl.BlockSpec(shape, lambda i: (0, 0))
    hbm = pl.BlockSpec(memory_space=pl.ANY)
    in_specs = [rows, const((1, D_MODEL)), const((D_MODEL, m))] + ([const((1, m))] if has_bias else []) + [hbm] * nsh
    args = (x, pw, w) + ((bias,) if has_bias else ()) + tuple(shards)
    return pl.pallas_call(
        body, name=name, grid=(steps,), in_specs=in_specs,
        out_specs=[pl.BlockSpec((TM, m), lambda i: (i, 0)), rows] + [hbm] * nsh,
        out_shape=[jax.ShapeDtypeStruct((n, m), F32), jax.ShapeDtypeStruct((n, D_MODEL), BF16)] + _gather_shapes(shards),
        scratch_shapes=_gather_sems(nsh) if nsh else [],
        compiler_params=_params(("arbitrary",), 56),
    )(*args)


def _outproj_postnorm(g, w, bias, xres, qw, tgt, name):
    n = g.shape[0]
    has_bias, has_loss = bias is not None, tgt is not None
    steps = n // TM

    def body(*refs):
        refs = list(refs)
        g_ref, w_ref = refs.pop(0), refs.pop(0)
        b_ref = refs.pop(0) if has_bias else None
        x_ref, qw_ref = refs.pop(0), refs.pop(0)
        t_ref = refs.pop(0) if has_loss else None
        y_ref, o_ref = refs.pop(0), refs.pop(0)
        y = jnp.dot(g_ref[...], w_ref[...], preferred_element_type=F32)
        if has_bias:
            y = y + b_ref[...]
        y_ref[...] = y
        r = lax.rsqrt(jnp.mean(y * y, axis=-1, keepdims=True) + NORM_EPS)
        xn = x_ref[...] + (y * r) * qw_ref[...]
        if not has_loss:
            o_ref[...] = xn
        else:
            loss_ref, acc_ref = refs
            i = pl.program_id(0)
            e = xn - t_ref[...]
            o_ref[...] = e * (1.0 / D_MODEL)

            @pl.when(i == 0)
            def _():
                acc_ref[...] = jnp.zeros_like(acc_ref)

            acc_ref[...] += jnp.sum(e * e, axis=0, keepdims=True)

            @pl.when(i == steps - 1)
            def _():
                loss_ref[...] = jnp.full(loss_ref.shape, jnp.sum(acc_ref[...]) * (0.5 / D_MODEL), F32)

    rows = pl.BlockSpec((TM, D_MODEL), lambda i: (i, 0))
    const = lambda shape: pl.BlockSpec(shape, lambda i: (0, 0))
    in_specs = [rows, const((D_MODEL, D_MODEL))] + ([const((1, D_MODEL))] if has_bias else []) + [rows, const((1, D_MODEL))]
    args = [g, w] + ([bias] if has_bias else []) + [xres, qw]
    out_specs = [rows, rows]
    out_shape = [jax.ShapeDtypeStruct((n, D_MODEL), F32)] * 2
    scratch = []
    if has_loss:
        in_specs.append(rows)
        args.append(tgt)
        out_specs.append(const((SUBLANES, LANES)))
        out_shape.append(jax.ShapeDtypeStruct((SUBLANES, LANES), F32))
        scratch = [pltpu.VMEM((1, D_MODEL), F32)]
    return pl.pallas_call(
        body, name=name, grid=(steps,), in_specs=in_specs, out_specs=out_specs, out_shape=out_shape,
        scratch_shapes=scratch, compiler_params=_params(("arbitrary",), 48),
    )(*args)


_QCOL, _ZCOL, _KCOL, _VCOL = 0, 1024, 2048, 2176


def _head_stack(chunks, heads, lt64):
    return jnp.concatenate([jnp.where(lt64 if n % 2 == 0 else ~lt64, chunks[n // 2], 0.0) for n in heads], axis=0)


def _dup_half(x, h, lt64):
    r = pltpu.roll(x, HEAD_DIM, 1)
    return jnp.where(lt64, x, r) if h == 0 else jnp.where(lt64, r, x)


def _pair_chunk(xt, c2):
    a, b = 2 * c2, 2 * c2 + 1
    return jnp.concatenate([xt[:HEAD_DIM, a * BLK:(a + 1) * BLK], xt[HEAD_DIM:, b * BLK:(b + 1) * BLK]], axis=0).T


def _attn_mask_t(i):
    key = lax.broadcasted_iota(jnp.int32, (2 * BLK, BLK), 0)
    qry = lax.broadcasted_iota(jnp.int32, (2 * BLK, BLK), 1)
    valid = (key > qry) & (key <= qry + BLK) & ((key >= BLK) | (i > 0))
    return jnp.tile(jnp.where(valid, 0.0, -1e30), (1, GROUP))


def _attn_probs_t(s, heads, sink_ref, mask):
    s = s + mask
    head = lax.broadcasted_iota(jnp.int32, (1, len(heads) * BLK), 1) >> 7
    sk = jnp.zeros((1, len(heads) * BLK), F32)
    for j, n in enumerate(heads):
        sk = jnp.where(head == j, sink_ref[0, n], sk)
    m = jnp.maximum(jnp.max(s, axis=0, keepdims=True), sk)
    p = jnp.exp(s - m)
    esk = jnp.exp(sk - m)
    inv = 1.0 / (jnp.sum(p, axis=0, keepdims=True) + esk)
    return p * inv, esk * inv


def _attn_fwd(p, ct, st, sinks, b_loc, nb, shards):
    n = p.shape[0]
    nsh = len(shards)

    def body(sink_ref, q_ref, z_ref, kc_ref, kp_ref, vc_ref, vp_ref, cc_ref, sc_ref, cp_ref, sp_ref, *rest):
        sh_in, (o_ref, g_ref), sh_out, sems = rest[:nsh], rest[nsh:nsh + 2], rest[nsh + 2:2 * nsh + 2], rest[2 * nsh + 2:]
        b, i = pl.program_id(0), pl.program_id(1)

        @pl.when((b == 0) & (i == 0))
        def _():
            _gather_start(sh_in, sh_out, sems)

        lane = lax.broadcasted_iota(jnp.int32, (BLK, LANES), 1)
        lm = lane & (HEAD_DIM - 1)
        cc, sc = cc_ref[...], sc_ref[...]
        kcat = jnp.concatenate([_rope_apply(kp_ref[...], cp_ref[...], sp_ref[...], lm),
                                _rope_apply(kc_ref[...], cc, sc, lm)], axis=0)
        vcat = jnp.concatenate([vp_ref[...], vc_ref[...]], axis=0)
        qr = [_rope_apply(q_ref[:, c * LANES:(c + 1) * LANES], cc, sc, lm) * (HEAD_DIM ** -0.5) for c in range(8)]
        valid = _attn_mask_t(i)
        lt64, lt64k = lane < HEAD_DIM, lax.broadcasted_iota(jnp.int32, (2 * BLK, LANES), 1) < HEAD_DIM
        def kv_head(h):
            heads = list(range(h * GROUP, (h + 1) * GROUP))
            kext, vext = _dup_half(kcat, h, lt64k).astype(BF16), _dup_half(vcat, h, lt64k).astype(BF16)
            qst = _head_stack(qr, heads, lt64).astype(BF16)
            s = lax.dot_general(kext, qst, NT_DIMS, preferred_element_type=F32)
            yield
            pn, _ = _attn_probs_t(s, heads, sink_ref, valid)
            ot = lax.dot_general(vext, pn.astype(BF16), TN_DIMS, preferred_element_type=F32)
            yield
            for c2 in range(GROUP // 2):
                oc = _pair_chunk(ot, c2)
                cols = slice((4 * h + c2) * LANES, (4 * h + c2 + 1) * LANES)
                zc = z_ref[:, cols]
                o_ref[:, cols] = oc
                g_ref[:, cols] = (oc * (zc * _sigmoid(zc))).astype(BF16)

        _in_stages([kv_head(h) for h in range(N_KV)])

        @pl.when((b == b_loc - 1) & (i == nb - 1))
        def _():
            _gather_wait(sh_in, sh_out, sems)

    cur = lambda b, i: b * nb + i
    prev = lambda b, i: b * nb + jnp.maximum(i - 1, 0)
    wide = lambda cb: pl.BlockSpec((BLK, ATTN_WIDTH), lambda b, i: (cur(b, i), cb))
    kv = lambda rowf, cb: pl.BlockSpec((BLK, LANES), lambda b, i: (rowf(b, i), cb))
    hbm = pl.BlockSpec(memory_space=pl.ANY)
    in_specs = [pl.BlockSpec(memory_space=pltpu.SMEM), wide(0), wide(1),
                kv(cur, _KCOL // LANES), kv(prev, _KCOL // LANES), kv(cur, _VCOL // LANES), kv(prev, _VCOL // LANES),
                kv(cur, 0), kv(cur, 0), kv(prev, 0), kv(prev, 0)] + [hbm] * nsh
    return pl.pallas_call(
        body, name="attn_fwd", grid=(b_loc, nb), in_specs=in_specs, out_specs=[wide(0), wide(0)] + [hbm] * nsh,
        out_shape=[jax.ShapeDtypeStruct((n, ATTN_WIDTH), F32), jax.ShapeDtypeStruct((n, ATTN_WIDTH), BF16)] + _gather_shapes(shards),
        scratch_shapes=_gather_sems(nsh), compiler_params=_params(("arbitrary", "arbitrary"), 48),
    )(sinks, p, p, p, p, p, p, ct, st, ct, st, *shards)


def _attn_bwd(p, ct, st, sinks, o, dg, b_loc, nb, parts):
    n = p.shape[0]
    nparts = len(parts)

    def body(sink_ref, q_ref, z_ref, kc_ref, kp_ref, vc_ref, vp_ref, cc_ref, sc_ref, cp_ref, sp_ref, o_ref, dg_ref, *rest):
        part_refs, (dp_ref, ds_ref), land_refs = rest[:nparts], rest[nparts:nparts + 2], rest[nparts + 2:2 * nparts + 2]
        dq_s, dz_s, dk_s, dv_s = rest[2 * nparts + 2:2 * nparts + 6]
        sems = rest[2 * nparts + 6:]
        b, i = pl.program_id(0), pl.program_id(1)

        @pl.when((b == 0) & (i == 0))
        def _():
            _scatter_start(part_refs, land_refs, sems)

        @pl.when((b == b_loc - 1) & (i == nb))
        def _():
            _scatter_wait(part_refs, land_refs, sems)

        lane = lax.broadcasted_iota(jnp.int32, (BLK, LANES), 1)
        lm = lane & (HEAD_DIM - 1)

        @pl.when((b == 0) & (i == 0))
        def _():
            ds_ref[...] = jnp.zeros_like(ds_ref)

        @pl.when(i < nb)
        def _compute():
            cc, sc = cc_ref[...], sc_ref[...]
            kcat = jnp.concatenate([_rope_apply(kp_ref[...], cp_ref[...], sp_ref[...], lm),
                                    _rope_apply(kc_ref[...], cc, sc, lm)], axis=0)
            vcat = jnp.concatenate([vp_ref[...], vc_ref[...]], axis=0)
            qr = [_rope_apply(q_ref[:, c * LANES:(c + 1) * LANES], cc, sc, lm) * (HEAD_DIM ** -0.5) for c in range(8)]
            valid = _attn_mask_t(i)
            lt64, lt64k = lane < HEAD_DIM, lax.broadcasted_iota(jnp.int32, (2 * BLK, LANES), 1) < HEAD_DIM
            do_chunks, doo_chunks, dz_chunks = [], [], []
            for c in range(8):
                cols = slice(c * LANES, (c + 1) * LANES)
                zc, oc, dgc = z_ref[:, cols], o_ref[:, cols], dg_ref[:, cols]
                sg = _sigmoid(zc)
                do_chunks.append(dgc * (zc * sg))
                dz_chunks.append(dgc * oc * (sg * (1.0 + zc * (1.0 - sg))))
                doo_chunks.append(do_chunks[c] * oc)
            dq_chunks = [None] * 8
            dk_h, dv_h, ds_parts = [None] * N_KV, [None] * N_KV, [None] * N_KV
            tile_lane = lax.broadcasted_iota(jnp.int32, (SUBLANES, LANES), 1)
            tile_row = lax.broadcasted_iota(jnp.int32, (SUBLANES, LANES), 0)
            ones8 = jnp.ones((SUBLANES, LANES), BF16)

            def kv_head(h):
                heads = list(range(h * GROUP, (h + 1) * GROUP))
                kext = _dup_half(kcat, h, lt64k)
                kext_bf, kext_t = kext.astype(BF16), kext.T.astype(BF16)
                vext = _dup_half(vcat, h, lt64k).astype(BF16)
                qst = _head_stack(qr, heads, lt64).astype(BF16)
                do_bf = _head_stack(do_chunks, heads, lt64).astype(BF16)
                s = lax.dot_general(kext_bf, qst, NT_DIMS, preferred_element_type=F32)
                dpt = lax.dot_general(vext, do_bf, NT_DIMS, preferred_element_type=F32)
                delta = sum(lax.dot_general(ones8, part, NT_DIMS, preferred_element_type=F32)
                            for part in _split3(_head_stack(doo_chunks, heads, lt64)))[0:1, :]
                yield
                pn, psink = _attn_probs_t(s, heads, sink_ref, valid)
                dst = (pn * (dpt - delta)).astype(BF16)
                dqt = jnp.dot(kext_t, dst, preferred_element_type=F32) * (HEAD_DIM ** -0.5)
                dk_ext = jnp.dot(dst, qst, preferred_element_type=F32)
                dv_ext = jnp.dot(pn.astype(BF16), do_bf, preferred_element_type=F32)
                yield
                sink_term = psink * delta
                ds_acc = jnp.zeros((SUBLANES, LANES), F32)
                for j, n in enumerate(heads):
                    val = -jnp.sum(sink_term[:, j * BLK:(j + 1) * BLK])
                    ds_acc = ds_acc + jnp.where((tile_lane == n) & (tile_row == 0), val, 0.0)
                ds_parts[h] = ds_acc
                for c2 in range(GROUP // 2):
                    dq_chunks[4 * h + c2] = _rope_bwd(_pair_chunk(dqt, c2), cc, sc, lm)
                dk_h[h] = dk_ext + pltpu.roll(dk_ext, HEAD_DIM, 1)
                dv_h[h] = dv_ext + pltpu.roll(dv_ext, HEAD_DIM, 1)

            _in_stages([kv_head(h) for h in range(N_KV)])
            ds_ref[...] += ds_parts[0] + ds_parts[1]
            dk_full = jnp.where(lt64k, dk_h[0], dk_h[1])
            dv_full = jnp.where(lt64k, dv_h[0], dv_h[1])

            @pl.when(i >= 1)
            def _emit():
                dp_ref[:, _QCOL:_QCOL + ATTN_WIDTH] = dq_s[...]
                dp_ref[:, _ZCOL:_ZCOL + ATTN_WIDTH] = dz_s[...]
                dp_ref[:, _KCOL:_KCOL + KV_WIDTH] = _rope_bwd(dk_s[...] + dk_full[:BLK], cp_ref[...], sp_ref[...], lm)
                dp_ref[:, _VCOL:_VCOL + KV_WIDTH] = dv_s[...] + dv_full[:BLK]

            for c in range(8):
                dq_s[:, c * LANES:(c + 1) * LANES] = dq_chunks[c]
                dz_s[:, c * LANES:(c + 1) * LANES] = dz_chunks[c]
            dk_s[...] = dk_full[BLK:]
            dv_s[...] = dv_full[BLK:]

        @pl.when(i == nb)
        def _final():
            dp_ref[:, _QCOL:_QCOL + ATTN_WIDTH] = dq_s[...]
            dp_ref[:, _ZCOL:_ZCOL + ATTN_WIDTH] = dz_s[...]
            dp_ref[:, _KCOL:_KCOL + KV_WIDTH] = _rope_bwd(dk_s[...], cc_ref[...], sc_ref[...], lm)
            dp_ref[:, _VCOL:_VCOL + KV_WIDTH] = dv_s[...]

    cur = lambda b, i: b * nb + jnp.minimum(i, nb - 1)
    prev = lambda b, i: b * nb + jnp.maximum(jnp.minimum(i, nb - 1) - 1, 0)
    emit = lambda b, i: b * nb + jnp.maximum(i - 1, 0)
    hbm = pl.BlockSpec(memory_space=pl.ANY)
    wide = lambda cb: pl.BlockSpec((BLK, ATTN_WIDTH), lambda b, i: (cur(b, i), cb))
    kv = lambda rowf, cb: pl.BlockSpec((BLK, LANES), lambda b, i: (rowf(b, i), cb))
    in_specs = [pl.BlockSpec(memory_space=pltpu.SMEM), wide(0), wide(1),
                kv(cur, _KCOL // LANES), kv(prev, _KCOL // LANES), kv(cur, _VCOL // LANES), kv(prev, _VCOL // LANES),
                kv(cur, 0), kv(cur, 0), kv(prev, 0), kv(prev, 0), wide(0), wide(0)] + [hbm] * nparts
    out_specs = [pl.BlockSpec((BLK, ATTN_IN), lambda b, i: (emit(b, i), 0)),
                 pl.BlockSpec((SUBLANES, LANES), lambda b, i: (0, 0))] + [hbm] * nparts
    return pl.pallas_call(
        body, name="attn_bwd", grid=(b_loc, nb + 1), in_specs=in_specs, out_specs=out_specs,
        out_shape=[jax.ShapeDtypeStruct((n, ATTN_IN), F32), jax.ShapeDtypeStruct((SUBLANES, LANES), F32)] + _scatter_lands(parts),
        scratch_shapes=[pltpu.VMEM((BLK, ATTN_WIDTH), F32), pltpu.VMEM((BLK, ATTN_WIDTH), F32),
                        pltpu.VMEM((BLK, KV_WIDTH), F32), pltpu.VMEM((BLK, KV_WIDTH), F32)] + _scatter_sems(nparts),
        compiler_params=_params(("arbitrary", "arbitrary"), 48),
    )(sinks, p, p, p, p, p, p, ct, st, ct, st, o, dg, *parts)


_CUM_ROWS = 256
HALF = CH // 2
_ROW0 = [SUBLANES * (s // SUBLANES) for s in range(CH)]
_ROW1 = [HALF * (s // HALF + 1) for s in range(CH)]
_ROWS_OF = [_ROW1[s] - _ROW0[s] for s in range(CH)]
_OFF_OF = [sum(_ROWS_OF[:s]) for s in range(CH)]


def _tri(lower):
    r = lax.broadcasted_iota(jnp.int32, (_CUM_ROWS, _CUM_ROWS), 0)
    c = lax.broadcasted_iota(jnp.int32, (_CUM_ROWS, _CUM_ROWS), 1)
    same = (r ^ c) < CH
    return (same & ((c <= r) if lower else (c >= r))).astype(BF16)


def _gates(qp, fp, lb):
    e = jnp.exp(-jnp.abs(fp))
    log_sig = jnp.minimum(fp, 0.0) - jnp.log(1.0 + e)
    a = jnp.log(lb)
    c = jnp.log(1.0 - lb) + log_sig
    g = jnp.maximum(a, c) + jnp.log(1.0 + jnp.exp(-jnp.abs(a - c)))
    sig_neg = jnp.where(fp >= 0, e, 1.0) / (1.0 + e)
    return qp * _sigmoid(qp), g, (1.0 - lb) * sig_neg, sig_neg


def _pair_rows(bc, s):
    return jnp.exp(jnp.minimum(bc[_ROW0[s]:_ROW1[s], :] - bc[s:s + 1, :], 0.0))


def _cross_half(q, k, bc):
    r = bc[HALF - 1:HALF, :]
    e1, e0 = jnp.exp(bc[HALF:, :] - r), jnp.exp(r - bc[:HALF, :])
    return q[HALF:, :] * e1, e1, k[:HALF, :] * e0, e0


HP = 8
REC_TB = 256
_HW = HP * REC_K


def _hgrn_specs(tb, nt, reverse):
    tmap = (lambda t: nt - 1 - t) if reverse else (lambda t: t)
    groups = REC_HEADS // HP
    blk = lambda cb: pl.BlockSpec((tb, _HW), lambda h, b, t: (b * nt + tmap(t), cb * groups + h))
    head = pl.BlockSpec((tb, _HW), lambda h, b, t: (b * nt + tmap(t), h))
    lbs = pl.BlockSpec((1, _HW), lambda h, b, t: (0, h))
    gws = pl.BlockSpec((1, REC_K), lambda h, b, t: (0, 0))
    hist = pl.BlockSpec((HP, 1, tb // CH, REC_K, REC_K), lambda h, b, t: (h, b, tmap(t), 0, 0))
    return blk, head, lbs, gws, hist


def _in_stages(heads):
    live = list(heads)
    while live:
        live = [g for g in live if next(g, live) is not live]


def _cumsum_chunks(tri, x, out_ref, tb):
    for r in range(0, tb, _CUM_ROWS):
        out_ref[r:r + _CUM_ROWS, :] = _dot3(tri, x[r:r + _CUM_ROWS, :])


def _hgrn_fwd(p, lb, gw, b_loc, t_len):
    n = p.shape[0]
    tb = min(REC_TB, t_len)
    nt, nck = t_len // tb, tb // CH

    def body(qp_ref, fp_ref, i_ref, z_ref, lb_ref, gw_ref, oraw_ref, g_ref, sh_ref, q_s, k_s, b_s, o_s, st_ref):
        @pl.when(pl.program_id(2) == 0)
        def _():
            st_ref[...] = jnp.zeros_like(st_ref)

        qv, g, kk, _ = _gates(qp_ref[...], fp_ref[...], lb_ref[...])
        q_s[...] = qv
        k_s[...] = kk
        _cumsum_chunks(_tri(True), g, b_s, tb)
        ones = jnp.ones((REC_K, REC_K), BF16)
        sub = lax.broadcasted_iota(jnp.int32, (SUBLANES, REC_K), 0)

        def chunk(c, carry):
            _in_stages([head_chunk(c, hp) for hp in range(HP)])
            return carry

        def head_chunk(c, hp):
            rs = pl.ds(pl.multiple_of(c * CH, CH), CH)
            cs = slice(hp * REC_K, (hp + 1) * REC_K)
            q, k, bc, v = q_s[rs, cs], k_s[rs, cs], b_s[rs, cs], i_ref[rs, cs]
            bl = bc[CH - 1:CH, :]
            st = st_ref[hp]
            sh_ref[hp, 0, c] = st
            o = lax.dot_general((q * jnp.exp(bc)).astype(BF16), st.astype(BF16), NT_DIMS, preferred_element_type=F32)
            w = jnp.concatenate([q[_ROW0[s]:_ROW1[s], :] * _pair_rows(bc, s) * k[s:s + 1, :] for s in range(CH)], axis=0)
            a = jnp.dot(w.astype(BF16), ones, preferred_element_type=F32)
            qe1, _, ke0, _ = _cross_half(q, k, bc)
            s10 = lax.dot_general(qe1.astype(BF16), ke0.astype(BF16), NT_DIMS, preferred_element_type=F32)
            kd = k * jnp.exp(bl - bc)
            st_new = lax.dot_general(v.astype(BF16), kd.astype(BF16), TN_DIMS, preferred_element_type=F32)
            yield
            o_cross = jnp.dot(s10.astype(BF16), v[:HALF, :].astype(BF16), preferred_element_type=F32)
            yield
            st_ref[hp] = st * jnp.exp(bl) + st_new
            acc = [jnp.zeros((SUBLANES, REC_K), F32) for _ in range(CH // SUBLANES)]
            for s in range(CH):
                j = s // SUBLANES
                vs = v[s:s + 1, :]
                for jj in range(j, _ROW1[s] // SUBLANES):
                    blk = a[_OFF_OF[s] + (jj - j) * SUBLANES:_OFF_OF[s] + (jj - j + 1) * SUBLANES, :]
                    if jj == j:
                        blk = jnp.where(sub >= s - j * SUBLANES, blk, 0.0)
                    acc[jj] = acc[jj] + blk * vs
            o_s[rs, cs] = o + jnp.concatenate(acc, axis=0) + jnp.concatenate([jnp.zeros((HALF, REC_K), F32), o_cross], axis=0)

        lax.fori_loop(0, nck, chunk, 0)
        oraw_ref[...] = o_s[...]
        for hp in range(HP):
            cs = slice(hp * REC_K, (hp + 1) * REC_K)
            o, zc = o_s[:, cs], z_ref[:, cs]
            on = (o * lax.rsqrt(jnp.mean(o * o, axis=-1, keepdims=True) + NORM_EPS)) * gw_ref[...]
            g_ref[:, cs] = (on * (zc * _sigmoid(zc))).astype(BF16)

    blk, head, lbs, gws, hist = _hgrn_specs(tb, nt, False)
    return pl.pallas_call(
        body, name="hgrn_fwd", grid=(REC_HEADS // HP, b_loc, nt),
        in_specs=[blk(0), blk(1), blk(2), blk(3), lbs, gws], out_specs=[head, head, hist],
        out_shape=[jax.ShapeDtypeStruct((n, 1024), F32), jax.ShapeDtypeStruct((n, 1024), BF16),
                   jax.ShapeDtypeStruct((REC_HEADS, b_loc, t_len // CH, REC_K, REC_K), F32)],
        scratch_shapes=[pltpu.VMEM((tb, _HW), F32)] * 4 + [pltpu.VMEM((HP, REC_K, REC_K), F32)],
        compiler_params=_params(("arbitrary", "arbitrary", "arbitrary"), 48),
    )(p, p, p, p, lb, gw)


def _hgrn_bwd(p, lb, gw, oraw, sh, dg, b_loc, t_len):
    n = p.shape[0]
    tb = min(REC_TB, t_len)
    nt, nck = t_len // tb, tb // CH

    def body(qp_ref, fp_ref, i_ref, z_ref, lb_ref, gw_ref, oraw_ref, dg_ref, sh_ref,
             dq_ref, df_ref, di_ref, dz_ref, dlb_ref, dgw_ref,
             q_s, k_s, b_s, do_s, dqv_s, dk_s, db_s, rowk_s, rowv_s, dst_ref):
        b, t = pl.program_id(1), pl.program_id(2)

        @pl.when(t == 0)
        def _():
            dst_ref[...] = jnp.zeros_like(dst_ref)

        @pl.when((b == 0) & (t == 0))
        def _():
            dlb_ref[...] = jnp.zeros_like(dlb_ref)
            dgw_ref[...] = jnp.zeros_like(dgw_ref)

        lbv, qp, fp = lb_ref[...], qp_ref[...], fp_ref[...]
        qv, g, kk, sig_neg = _gates(qp, fp, lbv)
        q_s[...] = qv
        k_s[...] = kk
        _cumsum_chunks(_tri(True), g, b_s, tb)
        gwv = gw_ref[...]
        for hp in range(HP):
            cs = slice(hp * REC_K, (hp + 1) * REC_K)
            o, zc, dgv = oraw_ref[:, cs], z_ref[:, cs], dg_ref[:, cs]
            rn = lax.rsqrt(jnp.mean(o * o, axis=-1, keepdims=True) + NORM_EPS)
            on = o * rn
            sgz = _sigmoid(zc)
            dz_ref[:, cs] = (dgv * (on * gwv) * (sgz * (1.0 + zc * (1.0 - sgz)))).astype(BF16)
            dpre = dgv * (zc * sgz)
            dgw_ref[hp] += jnp.sum(dpre * on, axis=0, keepdims=True)
            don = dpre * gwv
            do_s[:, cs] = rn * (don - on * jnp.mean(don * on, axis=-1, keepdims=True))

        ones = jnp.ones((REC_K, REC_K), BF16)
        sub = lax.broadcasted_iota(jnp.int32, (SUBLANES, REC_K), 0)
        rowid = lax.broadcasted_iota(jnp.int32, (CH, REC_K), 0)
        ngrp = CH // SUBLANES

        def chunk(ci, carry):
            _in_stages([head_chunk(nck - 1 - ci, hp) for hp in range(HP)])
            return carry

        def head_chunk(c, hp):
            rs = pl.ds(pl.multiple_of(c * CH, CH), CH)
            cs = slice(hp * REC_K, (hp + 1) * REC_K)
            rowk, rowv = rowk_s.at[hp], rowv_s.at[hp]
            q, k, bc, v, do = q_s[rs, cs], k_s[rs, cs], b_s[rs, cs], i_ref[rs, cs], do_s[rs, cs]
            bl = bc[CH - 1:CH, :]
            st, dst = sh_ref[hp, 0, c], dst_ref[hp]
            eb, ebl, ekd = jnp.exp(bc), jnp.exp(bl), jnp.exp(bl - bc)
            qe, kd = q * eb, k * ekd
            do_bf, dst_bf = do.astype(BF16), dst.astype(BF16)
            dqe = jnp.dot(do_bf, st.astype(BF16), preferred_element_type=F32)
            dkd = jnp.dot(v.astype(BF16), dst_bf, preferred_element_type=F32)
            dv = lax.dot_general(kd.astype(BF16), dst_bf, NT_DIMS, preferred_element_type=F32)
            dst_new = lax.dot_general(do_bf, qe.astype(BF16), TN_DIMS, preferred_element_type=F32)
            dec = [_pair_rows(bc, s) for s in range(CH)]
            w = jnp.concatenate([q[_ROW0[s]:_ROW1[s], :] * dec[s] * k[s:s + 1, :] for s in range(CH)], axis=0)
            x = jnp.concatenate([do[_ROW0[s]:_ROW1[s], :] * v[s:s + 1, :] for s in range(CH)], axis=0)
            a = jnp.dot(w.astype(BF16), ones, preferred_element_type=F32)
            da = jnp.dot(x.astype(BF16), ones, preferred_element_type=F32)
            qe1, e1, ke0, e0 = _cross_half(q, k, bc)
            qe1_bf, ke0_bf = qe1.astype(BF16), ke0.astype(BF16)
            do1_bf, v0_bf = do[HALF:, :].astype(BF16), v[:HALF, :].astype(BF16)
            s_t = lax.dot_general(ke0_bf, qe1_bf, NT_DIMS, preferred_element_type=F32)
            ds_x = lax.dot_general(do1_bf, v0_bf, NT_DIMS, preferred_element_type=F32)
            ds_t = lax.dot_general(v0_bf, do1_bf, NT_DIMS, preferred_element_type=F32)
            yield
            dqe1 = jnp.dot(ds_x.astype(BF16), ke0_bf, preferred_element_type=F32)
            dke0 = jnp.dot(ds_t.astype(BF16), qe1_bf, preferred_element_type=F32)
            dv1 = jnp.dot(s_t.astype(BF16), do1_bf, preferred_element_type=F32)
            yield
            dbl = jnp.sum(dkd * kd, axis=0, keepdims=True) + ebl * jnp.sum(st * dst, axis=0, keepdims=True)
            dst_ref[hp] = dst * ebl + dst_new
            dq_acc = [jnp.zeros((SUBLANES, REC_K), F32) for _ in range(ngrp)]
            for s in range(CH):
                j = s // SUBLANES
                r0 = j * SUBLANES
                ks = k[s:s + 1, :]
                uk = jnp.zeros((SUBLANES, REC_K), F32)
                uv = jnp.zeros((SUBLANES, REC_K), F32)
                for jj in range(j, _ROW1[s] // SUBLANES):
                    lo, hi = _OFF_OF[s] + (jj - j) * SUBLANES, _OFF_OF[s] + (jj - j + 1) * SUBLANES
                    a_blk, da_blk = a[lo:hi, :], da[lo:hi, :]
                    if jj == j:
                        keep = sub >= s - r0
                        a_blk, da_blk = jnp.where(keep, a_blk, 0.0), jnp.where(keep, da_blk, 0.0)
                    rows = slice(jj * SUBLANES, (jj + 1) * SUBLANES)
                    tt = da_blk * dec[s][(jj - j) * SUBLANES:(jj - j + 1) * SUBLANES, :]
                    dq_acc[jj] = dq_acc[jj] + tt * ks
                    uk = uk + tt * q[rows, :]
                    uv = uv + a_blk * do[rows, :]
                rowk[s:s + 1, :] = jnp.sum(uk, axis=0, keepdims=True)
                rowv[s:s + 1, :] = jnp.sum(uv, axis=0, keepdims=True)
            zero_half = jnp.zeros((HALF, REC_K), F32)
            dq_x = jnp.concatenate([zero_half, dqe1 * e1], axis=0)
            dk_x = jnp.concatenate([dke0 * e0, zero_half], axis=0)
            dv_x = jnp.concatenate([dv1, zero_half], axis=0)
            db_x = jnp.concatenate([-(dke0 * ke0), dqe1 * qe1], axis=0)
            dq_in = jnp.concatenate(dq_acc, axis=0)
            dk_in = rowk[...]
            dqv_s[rs, cs] = dqe * eb + dq_in + dq_x
            dk_s[rs, cs] = dkd * ekd + dk_in + dk_x
            di_ref[rs, cs] = (dv + rowv[...] + dv_x).astype(BF16)
            db = dqe * qe - dkd * kd + q * dq_in - k * dk_in + db_x
            db_s[rs, cs] = db + jnp.where(rowid == CH - 1, dbl, 0.0)

        lax.fori_loop(0, nck, chunk, 0)
        up = _tri(False)
        sgq = _sigmoid(qp)
        dq_ref[...] = (dqv_s[...] * (sgq * (1.0 + qp * (1.0 - sgq)))).astype(BF16)
        dlb_acc = jnp.zeros((1, _HW), F32)
        for r in range(0, tb, _CUM_ROWS):
            rows = slice(r, r + _CUM_ROWS)
            dgl = _dot3(up, db_s[rows, :])
            dfg = dgl * jnp.exp(-g[rows, :]) - dk_s[rows, :]
            sn = sig_neg[rows, :]
            df_ref[rows, :] = (dfg * (1.0 - lbv) * (1.0 - sn) * sn).astype(BF16)
            dlb_acc = dlb_acc + jnp.sum(dfg * sn, axis=0, keepdims=True)
        dlb_ref[...] += dlb_acc

    blk, head, lbs, gws, hist = _hgrn_specs(tb, nt, True)
    out_specs = [head, head, head, head, lbs, pl.BlockSpec((HP, 1, REC_K), lambda h, b, t: (h, 0, 0))]
    out_shape = [jax.ShapeDtypeStruct((n, 1024), BF16)] * 4 + [
        jax.ShapeDtypeStruct((1, 1024), F32), jax.ShapeDtypeStruct((REC_HEADS, 1, REC_K), F32)]
    return pl.pallas_call(
        body, name="hgrn_bwd", grid=(REC_HEADS // HP, b_loc, nt),
        in_specs=[blk(0), blk(1), blk(2), blk(3), lbs, gws, head, head, hist], out_specs=out_specs, out_shape=out_shape,
        scratch_shapes=[pltpu.VMEM((tb, _HW), F32)] * 7 + [pltpu.VMEM((HP, CH, REC_K), F32)] * 2 + [pltpu.VMEM((HP, REC_K, REC_K), F32)],
        compiler_params=_params(("arbitrary", "arbitrary", "arbitrary"), 48),
    )(p, p, p, p, lb, gw, oraw, dg, sh)


def _postnorm_bwd_nt(dxo, y, qw, w, has_bias, name):
    n = dxo.shape[0]

    def body(dx_ref, y_ref, qw_ref, w_ref, dg_ref, dy_ref, dqw_ref, db_ref):
        @pl.when(pl.program_id(0) == 0)
        def _():
            dqw_ref[...] = jnp.zeros_like(dqw_ref)
            db_ref[...] = jnp.zeros_like(db_ref)

        yv, dxv = y_ref[...], dx_ref[...]
        r = lax.rsqrt(jnp.mean(yv * yv, axis=-1, keepdims=True) + NORM_EPS)
        u = yv * r
        du = dxv * qw_ref[...]
        dy = r * (du - u * jnp.mean(du * u, axis=-1, keepdims=True))
        dqw_ref[...] += jnp.sum(dxv * u, axis=0, keepdims=True)
        if has_bias:
            db_ref[...] += jnp.sum(dy, axis=0, keepdims=True)
        dyb = dy.astype(BF16)
        dy_ref[...] = dyb
        dg_ref[...] = lax.dot_general(dyb, w_ref[...], NT_DIMS, preferred_element_type=F32)

    rows = pl.BlockSpec((TM, D_MODEL), lambda i: (i, 0))
    const = lambda shape: pl.BlockSpec(shape, lambda i: (0, 0))
    return pl.pallas_call(
        body, name=name, grid=(n // TM,), in_specs=[rows, rows, const((1, D_MODEL)), const((D_MODEL, D_MODEL))],
        out_specs=[rows, rows, const((1, D_MODEL)), const((1, D_MODEL))],
        out_shape=[jax.ShapeDtypeStruct((n, D_MODEL), F32), jax.ShapeDtypeStruct((n, D_MODEL), BF16),
                   jax.ShapeDtypeStruct((1, D_MODEL), F32), jax.ShapeDtypeStruct((1, D_MODEL), F32)],
        compiler_params=_params(("arbitrary",), 48),
    )(dxo, y, qw, w)


def _nt_prenorm_bwd(dps, w, x, pw, dxo, has_bias, name, parts=()):
    n = x.shape[0]
    widths = [d.shape[1] for d in dps]
    m = sum(widths)
    npieces, nparts, steps = len(dps), len(parts), n // TM

    def body(*refs):
        dp_refs = refs[:npieces]
        w_ref, x_ref, pw_ref, dxo_ref = refs[npieces:npieces + 4]
        part_refs = refs[npieces + 4:npieces + 4 + nparts]
        dx_ref, dpw_ref, db_ref = refs[npieces + 4 + nparts:npieces + 7 + nparts]
        land_refs = refs[npieces + 7 + nparts:npieces + 7 + 2 * nparts]
        sems = refs[npieces + 7 + 2 * nparts:]

        @pl.when(pl.program_id(0) == 0)
        def _():
            dpw_ref[...] = jnp.zeros_like(dpw_ref)
            db_ref[...] = jnp.zeros_like(db_ref)
            if nparts:
                _scatter_start(part_refs, land_refs, sems)

        dh = jnp.zeros((TM, D_MODEL), F32)
        off = 0
        for dp_ref, wd in zip(dp_refs, widths):
            cn = _col_chunk(wd)
            for j in range(0, wd, cn):
                dpc = dp_ref[:, j:j + cn]
                if has_bias:
                    db_ref[:, off + j:off + j + cn] += jnp.sum(dpc, axis=0, keepdims=True)
                dh = dh + lax.dot_general(dpc.astype(BF16), w_ref[:, off + j:off + j + cn], NT_DIMS, preferred_element_type=F32)
            off += wd
        xv = x_ref[...]
        r = lax.rsqrt(jnp.mean(xv * xv, axis=-1, keepdims=True) + NORM_EPS)
        xn = xv * r
        dpw_ref[...] += jnp.sum(dh * xn, axis=0, keepdims=True)
        dxn = dh * pw_ref[...]
        dx_ref[...] = dxo_ref[...] + r * (dxn - xn * jnp.mean(dxn * xn, axis=-1, keepdims=True))

        if nparts:
            @pl.when(pl.program_id(0) == steps - 1)
            def _():
                _scatter_wait(part_refs, land_refs, sems)

    rows = pl.BlockSpec((TM, D_MODEL), lambda i: (i, 0))
    const = lambda shape: pl.BlockSpec(shape, lambda i: (0, 0))
    hbm = pl.BlockSpec(memory_space=pl.ANY)
    in_specs = ([pl.BlockSpec((TM, wd), lambda i: (i, 0)) for wd in widths] + [const((D_MODEL, m)), rows, const((1, D_MODEL)), rows]
                + [hbm] * nparts)
    return pl.pallas_call(
        body, name=name, grid=(steps,), in_specs=in_specs,
        out_specs=[rows, const((1, D_MODEL)), const((1, m))] + [hbm] * nparts,
        out_shape=[jax.ShapeDtypeStruct((n, D_MODEL), F32), jax.ShapeDtypeStruct((1, D_MODEL), F32),
                   jax.ShapeDtypeStruct((1, m), F32)] + _scatter_lands(parts),
        scratch_shapes=_scatter_sems(nparts) if nparts else [],
        compiler_params=_params(("arbitrary",), 56),
    )(*dps, w, x, pw, dxo, *parts)


def _matmul_tn(a, b, name):
    n, k = a.shape
    m = b.shape[1]
    tk, tm, tn = k, _col_chunk(m), 1024 if n % 1024 == 0 else n

    def body(a_ref, b_ref, o_ref):
        @pl.when(pl.program_id(2) == 0)
        def _():
            o_ref[...] = jnp.zeros_like(o_ref)

        o_ref[...] += lax.dot_general(a_ref[...], b_ref[...].astype(BF16), TN_DIMS, preferred_element_type=F32)

    return pl.pallas_call(
        body, name=name, grid=(k // tk, m // tm, n // tn),
        in_specs=[pl.BlockSpec((tn, tk), lambda i, j, l: (l, i)), pl.BlockSpec((tn, tm), lambda i, j, l: (l, j))],
        out_specs=pl.BlockSpec((tk, tm), lambda i, j, l: (i, j)),
        out_shape=jax.ShapeDtypeStruct((k, m), F32),
        compiler_params=_params(("arbitrary", "arbitrary", "arbitrary"), 48),
    )(a, b)


def _by_owner_cols(dw):
    k, m = dw.shape
    return dw.reshape(k, N_DEV, m // N_DEV).transpose(1, 0, 2)


def _own_and_bf16(part):
    return lax.dynamic_index_in_dim(part, _my_id(), 0, keepdims=False), part.astype(BF16)


def _step(x, pos_col, tgt, pre_w, post_w, wa_in, ba_in, sinks, wa_out_shard, ba_out, wr_in_shard, lb_logits, gnorm_w, wr_out_shard, b_loc, t_len):
    nb = t_len // BLK
    ct, st = _rope_tables(pos_col)
    lb = _lower_bound(lb_logits)
    p0, h0, ga_out = _norm_matmul(x, pre_w[0:1], wa_in, ba_in, "attn_in_proj", [wa_out_shard])
    wa_out = ga_out.reshape(ATTN_WIDTH, D_MODEL)
    o0, g0, gr_in, gr_out = _attn_fwd(p0, ct, st, sinks, b_loc, nb, [wr_in_shard, wr_out_shard])
    wr_in = gr_in.transpose(1, 0, 2).reshape(D_MODEL, REC_IN)
    wr_out = gr_out.reshape(1024, D_MODEL)
    y0, x1 = _outproj_postnorm(g0, wa_out, ba_out, x, post_w[0:1], None, "attn_out_proj")
    p1, h1 = _norm_matmul(x1, pre_w[1:2], wr_in, None, "rec_in_proj")
    o1, g1, sh = _hgrn_fwd(p1, lb, gnorm_w, b_loc, t_len)
    y1, dx2, loss_tile = _outproj_postnorm(g1, wr_out, None, x1, post_w[1:2], tgt, "rec_out_proj_loss")
    dg1, dy1, dpost1, _ = _postnorm_bwd_nt(dx2, y1, post_w[1:2], wr_out, False, "rec_out_bwd")
    d_wr_out = _matmul_tn(g1, dy1, "rec_w_out_grad")
    dq1, df1, di1, dz1, dlb, dgw = _hgrn_bwd(p1, lb, gnorm_w, o1, sh, dg1, b_loc, t_len)
    dps1 = [dq1, df1, di1, dz1]
    dx1, dpre1, _ = _nt_prenorm_bwd(dps1, wr_in, x1, pre_w[1:2], dx2, False, "rec_in_bwd")
    d_wr_in = [_matmul_tn(h1, dpk, f"rec_w_in_grad_{k}") for k, dpk in enumerate(dps1)]
    dg0, dy0, dpost0, dba_out = _postnorm_bwd_nt(dx1, y0, post_w[0:1], wa_out, True, "attn_out_bwd")
    d_wa_out = _matmul_tn(g0, dy0, "attn_w_out_grad")
    owns, wires = zip(*[_own_and_bf16(part) for part in (
        _by_owner_cols(jnp.concatenate(d_wr_in, axis=1)), d_wr_out.reshape(N_DEV, 1024 // N_DEV, D_MODEL),
        d_wa_out.reshape(N_DEV, ATTN_WIDTH // N_DEV, D_MODEL))])
    dp0, dsink_tile, *lands = _attn_bwd(p0, ct, st, sinks, o0, dg0, b_loc, nb, list(wires))
    d_wa_in = _matmul_tn(h0, dp0, "attn_w_in_grad")
    own_a_in, wire_a_in = _own_and_bf16(_by_owner_cols(_qkvz(d_wa_in)))
    dx0, dpre0, dba_in, land_a_in = _nt_prenorm_bwd([dp0], wa_in, x, pre_w[0:1], dx1, True, "attn_in_bwd", [wire_a_in])
    small = dict(pre=jnp.concatenate([dpre0, dpre1], axis=0), post=jnp.concatenate([dpost0, dpost1], axis=0),
                 ba_in=dba_in, sinks=dsink_tile[0:1, 0:N_HEADS], ba_out=dba_out, lb=dlb, gnorm=jnp.sum(dgw, axis=0))
    return loss_tile, dx0, list(zip(lands, owns)) + [(land_a_in, own_a_in)], small


def _my_id():
    return lax.axis_index("x") * 4 + lax.axis_index("y") * 2 + lax.axis_index("c")


def _peer(k):
    x, y, c = lax.axis_index("x"), lax.axis_index("y"), lax.axis_index("c")
    return (x ^ ((k >> 2) & 1), y ^ ((k >> 1) & 1), c ^ (k & 1))


def _peer_id(k):
    return _my_id() ^ k


def _all_gather(shards):
    nsh = len(shards)

    def body(*refs):
        ins, outs, sems = refs[:nsh], refs[nsh:2 * nsh], refs[2 * nsh:]
        _gather_start(ins, outs, sems)
        _gather_wait(ins, outs, sems)

    hbm = pl.BlockSpec(memory_space=pl.ANY)
    return pl.pallas_call(
        body, name="comm_all_gather", in_specs=[hbm] * nsh, out_specs=[hbm] * nsh,
        out_shape=_gather_shapes(shards), scratch_shapes=_gather_sems(nsh),
    )(*shards)


def _gather_shapes(shards):
    return [jax.ShapeDtypeStruct((N_DEV,) + s.shape, s.dtype) for s in shards]


def _gather_sems(nsh):
    return [pltpu.SemaphoreType.DMA((nsh, N_DEV - 1)), pltpu.SemaphoreType.DMA((nsh, N_DEV - 1)), pltpu.SemaphoreType.DMA((nsh,))]


def _gather_copies(ins, outs, sems, received):
    send_sems, recv_sems, local_sems = sems
    me = _my_id()
    local = [pltpu.make_async_copy(ins[a], outs[a].at[me], local_sems.at[a]) for a in range(len(ins))]
    remote = [pltpu.make_async_remote_copy(
        src_ref=ins[a], dst_ref=outs[a].at[_peer_id(k) if received else me], send_sem=send_sems.at[a, k - 1],
        recv_sem=recv_sems.at[a, k - 1], device_id=_peer(k), device_id_type=MESH)
        for a in range(len(ins)) for k in range(1, N_DEV)]
    return local, remote


def _gather_start(ins, outs, sems):
    local, sends = _gather_copies(ins, outs, sems, False)
    for cp in local + sends:
        cp.start()


def _gather_wait(ins, outs, sems):
    local, recvs = _gather_copies(ins, outs, sems, True)
    for cp in recvs:
        cp.wait_recv()
    for cp in recvs:
        cp.wait_send()
    for cp in local:
        cp.wait()


def _scatter_lands(parts):
    return [jax.ShapeDtypeStruct((N_DEV - 1,) + p.shape[1:], p.dtype) for p in parts]


def _scatter_sems(nparts):
    return [pltpu.SemaphoreType.DMA((nparts, N_DEV - 1)), pltpu.SemaphoreType.DMA((nparts, N_DEV - 1))]


def _scatter_copies(parts, lands, sems):
    send_sems, recv_sems = sems
    return [pltpu.make_async_remote_copy(
        src_ref=parts[a].at[_peer_id(k)], dst_ref=lands[a].at[k - 1], send_sem=send_sems.at[a, k - 1],
        recv_sem=recv_sems.at[a, k - 1], device_id=_peer(k), device_id_type=MESH)
        for a in range(len(parts)) for k in range(1, N_DEV)]


def _scatter_start(parts, lands, sems):
    for cp in _scatter_copies(parts, lands, sems):
        cp.start()


def _scatter_wait(parts, lands, sems):
    copies = _scatter_copies(parts, lands, sems)
    for cp in copies:
        cp.wait_recv()
    for cp in copies:
        cp.wait_send()


def _adamw(w, g, m, v):
    m2 = ADAM_B1 * m + (1.0 - ADAM_B1) * g
    v2 = ADAM_B2 * v + (1.0 - ADAM_B2) * (g * g)
    m_hat = m2 / (1.0 - ADAM_B1 ** ADAM_STEP)
    v_hat = v2 / (1.0 - ADAM_B2 ** ADAM_STEP)
    delta = -ADAM_LR * (m_hat / (jnp.sqrt(v_hat) + ADAM_EPS) + ADAM_WD * w)
    return delta, m2, v2


def _sum_adamw_rows(land_ref, own_ref, w_ref, m_ref, v_ref, out_refs):
    r, c = own_ref.shape
    rc = 64 if r % 64 == 0 else r
    me = _my_id()
    g_ref, d_ref, m2_ref, v2_ref = out_refs

    def rows(i, carry):
        rs = pl.ds(pl.multiple_of(i * rc, rc), rc)
        g = jnp.zeros((rc, c), F32)
        for dev in range(N_DEV):
            k = dev ^ me
            g = g + jnp.where(k == 0, own_ref[rs, :], land_ref[jnp.maximum(k - 1, 0), rs, :].astype(F32))
        delta, m2, v2 = _adamw(w_ref[rs, :], g, m_ref[rs, :], v_ref[rs, :])
        g_ref[rs, :] = g
        d_ref[rs, :] = delta
        m2_ref[rs, :] = m2
        v2_ref[rs, :] = v2
        return carry

    lax.fori_loop(0, r // rc, rows, 0)


def _sum_adamw(land, own, w, m, v, name):
    r, c = own.shape

    def body(land_ref, own_ref, w_ref, m_ref, v_ref, g_ref, d_ref, m2_ref, v2_ref):
        _sum_adamw_rows(land_ref, own_ref, w_ref, m_ref, v_ref, (g_ref, d_ref, m2_ref, v2_ref))

    vmem = pl.BlockSpec(memory_space=pltpu.VMEM)
    return pl.pallas_call(
        body, name=name, in_specs=[vmem] * 5, out_specs=[vmem] * 4, out_shape=[jax.ShapeDtypeStruct((r, c), F32)] * 4,
        compiler_params=_params(None, 56),
    )(land, own, w, m, v)


_SMALL = [("pre_norm_w", 2048), ("post_norm_w", 2048), ("attn_b_in", 2304), ("attn_sinks", 16), ("attn_b_out", 1024),
          ("rec_lb_logits", 2048), ("rec_gnorm_w", 128)]
_TILE = SUBLANES * LANES


def _small_rows(size):
    return -(-size // _TILE) * SUBLANES


_SMALL_OFF = {}
_r = 0
for _name, _size in _SMALL:
    _SMALL_OFF[_name] = _r
    _r += _small_rows(_size)
_SMALL_ROWS = _r


def _pack_small(pieces):
    out = []
    for name, size in _SMALL:
        flat = pieces[name].reshape(-1).astype(F32)
        out.append(jnp.pad(flat, (0, _small_rows(size) * LANES - size)).reshape(-1, LANES))
    return jnp.concatenate(out, axis=0)


def _unpack_small(packed, shapes):
    return {name: packed[_SMALL_OFF[name]:_SMALL_OFF[name] + _small_rows(size)].reshape(-1)[:size].reshape(shapes[name])
            for name, size in _SMALL}


def _small_allreduce_adamw(gpart, w, m, v):
    lb0 = _SMALL_OFF["rec_lb_logits"]

    def body(gp_ref, w_ref, m_ref, v_ref, g_ref, d_ref, m2_ref, v2_ref, land_ref, send_sems, recv_sems):
        me = _my_id()
        sent = []
        for k in range(1, N_DEV):
            cp = pltpu.make_async_remote_copy(src_ref=gp_ref, dst_ref=land_ref.at[k - 1], send_sem=send_sems.at[k - 1],
                                              recv_sem=recv_sems.at[k - 1], device_id=_peer(k), device_id_type=MESH)
            cp.start()
            sent.append(cp)
        for cp in sent:
            cp.wait_recv()
        for cp in sent:
            cp.wait_send()
        g = jnp.zeros((_SMALL_ROWS, LANES), F32)
        for dev in range(N_DEV):
            k = dev ^ me
            g = g + jnp.where(k == 0, gp_ref[...], land_ref[jnp.maximum(k - 1, 0)])
        g_ref[...] = g
        l0, l1 = w_ref[lb0:lb0 + SUBLANES, :], w_ref[lb0 + SUBLANES:lb0 + 2 * SUBLANES, :]
        mx = jnp.maximum(l0, l1)
        e0, e1 = jnp.exp(l0 - mx), jnp.exp(l1 - mx)
        p1 = e1 / (e0 + e1)
        dl1 = (1.0 - p1) * p1 * g[lb0:lb0 + SUBLANES, :]
        g_ref[lb0:lb0 + SUBLANES, :] = -dl1
        g_ref[lb0 + SUBLANES:lb0 + 2 * SUBLANES, :] = dl1
        delta, m2, v2 = _adamw(w_ref[...], g_ref[...], m_ref[...], v_ref[...])
        d_ref[...] = delta
        m2_ref[...] = m2
        v2_ref[...] = v2

    vmem = pl.BlockSpec(memory_space=pltpu.VMEM)
    return pl.pallas_call(
        body, name="comm_small_allreduce_adamw", in_specs=[vmem] * 4, out_specs=[vmem] * 4,
        out_shape=[jax.ShapeDtypeStruct((_SMALL_ROWS, LANES), F32)] * 4,
        scratch_shapes=[pltpu.VMEM((N_DEV - 1, _SMALL_ROWS, LANES), F32), pltpu.SemaphoreType.DMA((N_DEV - 1,)),
                        pltpu.SemaphoreType.DMA((N_DEV - 1,))],
    )(gpart, w, m, v)


def _qzkv(a):
    return jnp.concatenate([a[..., :1024], a[..., 1280:], a[..., 1024:1280]], axis=-1)


def _qkvz(a):
    return jnp.concatenate([a[..., :1024], a[..., 2048:], a[..., 1024:2048]], axis=-1)


def kernel(x, positions, pre_norm_w, post_norm_w, attn_w_in, attn_b_in, attn_sinks, attn_w_out, attn_b_out, rec_w_in, rec_lb_logits, rec_gnorm_w, rec_w_out, loss_target, m_pre_norm_w, m_post_norm_w, m_attn_w_in, m_attn_b_in, m_attn_sinks, m_attn_w_out, m_attn_b_out, m_rec_w_in, m_rec_lb_logits, m_rec_gnorm_w, m_rec_w_out, v_pre_norm_w, v_post_norm_w, v_attn_w_in, v_attn_b_in, v_attn_sinks, v_attn_w_out, v_attn_b_out, v_rec_w_in, v_rec_lb_logits, v_rec_gnorm_w, v_rec_w_out):
    b_loc, t_len, _ = x.shape
    n = b_loc * t_len
    ga_in, = _all_gather([attn_w_in[0].astype(BF16)])
    wa_in = _qzkv(ga_in.transpose(1, 0, 2).reshape(D_MODEL, ATTN_IN))

    loss_tile, dx, landed, small = _step(
        x.reshape(n, D_MODEL), positions.reshape(n, 1).astype(F32), loss_target.reshape(n, D_MODEL),
        pre_norm_w, post_norm_w, wa_in, _qzkv(attn_b_in), attn_sinks, attn_w_out[0].astype(BF16), attn_b_out,
        rec_w_in[0].astype(BF16), rec_lb_logits, rec_gnorm_w, rec_w_out[0].astype(BF16), b_loc, t_len)
    loss = lax.psum(loss_tile[0, 0], ("x", "y", "c"))

    lift = lambda outs: tuple(a[None] for a in outs)
    (l_r_in, o_r_in), (l_r_out, o_r_out), (l_a_out, o_a_out), (l_a_in, o_a_in) = landed
    r_a_in = lift(_sum_adamw(l_a_in, o_a_in, attn_w_in[0], m_attn_w_in[0], v_attn_w_in[0], "adamw_attn_w_in"))
    r_r_in = lift(_sum_adamw(l_r_in, o_r_in, rec_w_in[0], m_rec_w_in[0], v_rec_w_in[0], "adamw_rec_w_in"))
    r_r_out = lift(_sum_adamw(l_r_out, o_r_out, rec_w_out[0], m_rec_w_out[0], v_rec_w_out[0], "adamw_rec_w_out"))
    r_a_out = lift(_sum_adamw(l_a_out, o_a_out, attn_w_out[0], m_attn_w_out[0], v_attn_w_out[0], "adamw_attn_w_out"))

    gsmall = dict(pre_norm_w=small["pre"], post_norm_w=small["post"], attn_b_in=_qkvz(small["ba_in"]), attn_sinks=small["sinks"],
                  attn_b_out=small["ba_out"], rec_lb_logits=jnp.concatenate([small["lb"], jnp.zeros_like(small["lb"])], axis=0),
                  rec_gnorm_w=small["gnorm"])
    wsmall = dict(pre_norm_w=pre_norm_w, post_norm_w=post_norm_w, attn_b_in=attn_b_in, attn_sinks=attn_sinks,
                  attn_b_out=attn_b_out, rec_lb_logits=rec_lb_logits, rec_gnorm_w=rec_gnorm_w)
    msmall = dict(pre_norm_w=m_pre_norm_w, post_norm_w=m_post_norm_w, attn_b_in=m_attn_b_in, attn_sinks=m_attn_sinks,
                  attn_b_out=m_attn_b_out, rec_lb_logits=m_rec_lb_logits, rec_gnorm_w=m_rec_gnorm_w)
    vsmall = dict(pre_norm_w=v_pre_norm_w, post_norm_w=v_post_norm_w, attn_b_in=v_attn_b_in, attn_sinks=v_attn_sinks,
                  attn_b_out=v_attn_b_out, rec_lb_logits=v_rec_lb_logits, rec_gnorm_w=v_rec_gnorm_w)
    shapes = {k: a.shape for k, a in wsmall.items()}
    packed = _small_allreduce_adamw(_pack_small(gsmall), _pack_small(wsmall), _pack_small(msmall), _pack_small(vsmall))
    sg, sd, sm, sv = [_unpack_small(a, shapes) for a in packed]

    big = {"attn_w_in": r_a_in, "attn_w_out": r_a_out, "rec_w_in": r_r_in, "rec_w_out": r_r_out}
    order = ["pre_norm_w", "post_norm_w", "attn_w_in", "attn_b_in", "attn_sinks", "attn_w_out", "attn_b_out", "rec_w_in",
             "rec_lb_logits", "rec_gnorm_w", "rec_w_out"]
    outs = [loss, dx.reshape(b_loc, t_len, D_MODEL)]
    for idx, small_set in enumerate((sg, sd, sm, sv)):
        outs += [big[nm][idx] if nm in big else small_set[nm] for nm in order]
    return tuple(outs)
```

```python
import numpy as np
import jax
import jax.numpy as jnp
from jax import lax
from jax.experimental import pallas as pl
from jax.experimental.pallas import tpu as pltpu

F32, BF16 = jnp.float32, jnp.bfloat16
MESH = pl.DeviceIdType.MESH
N_DEV = 8

D_MODEL = 1024
N_HEADS, HEAD_DIM, N_KV, GROUP = 16, 64, 2, 8
ATTN_WIDTH, KV_WIDTH = 1024, 128
ATTN_IN = 2 * ATTN_WIDTH + 2 * KV_WIDTH
BLK = 128
ROPE_THETA, ROPE_HALF = 500000.0, 8
REC_HEADS, REC_K = 8, 128
REC_IN = 4 * 1024
CH = 32
NORM_EPS = 1e-6
ADAM_LR, ADAM_B1, ADAM_B2, ADAM_EPS, ADAM_WD, ADAM_STEP = 0.001, 0.9, 0.999, 1e-08, 0.01, 10

LANES, SUBLANES = 128, 8
TM = 512
NT_DIMS = (((1,), (1,)), ((), ()))
TN_DIMS = (((0,), (0,)), ((), ()))
MB = 2 ** 20


def _params(sem=None, vmem_mb=48, **kw):
    return pltpu.CompilerParams(dimension_semantics=sem, vmem_limit_bytes=vmem_mb * MB, **kw)


def _col_chunk(m):
    return 768 if m % 1024 else 1024


def _sigmoid(x):
    return 1.0 / (1.0 + jnp.exp(-x))


def _split3(x):
    hi = x.astype(BF16)
    r1 = x - hi.astype(F32)
    mid = r1.astype(BF16)
    lo = (r1 - mid.astype(F32)).astype(BF16)
    return hi, mid, lo


def _dot3(l_bf, x):
    hi, mid, lo = _split3(x)
    return (jnp.dot(l_bf, hi, preferred_element_type=F32) + jnp.dot(l_bf, mid, preferred_element_type=F32)
            + jnp.dot(l_bf, lo, preferred_element_type=F32))


def _rope_tables(pos_col):
    n = pos_col.shape[0]
    lane = np.arange(LANES) % HEAD_DIM
    inv = np.float32(ROPE_THETA) ** (-(np.arange(ROPE_HALF, dtype=np.float32) * np.float32(2.0) / np.float32(2 * ROPE_HALF)))
    freq = np.where(lane < 2 * ROPE_HALF, inv[lane % ROPE_HALF], 0.0).astype(np.float32)[None, :]
    sign = np.where(lane < ROPE_HALF, -1.0, np.where(lane < 2 * ROPE_HALF, 1.0, 0.0)).astype(np.float32)[None, :]

    def body(p_ref, f_ref, s_ref, c_out, s_out):
        ang = p_ref[...] * f_ref[...]
        c_out[...] = jnp.cos(ang)
        s_out[...] = jnp.sin(ang) * s_ref[...]

    row = pl.BlockSpec((TM, 1), lambda i: (i, 0))
    vec = pl.BlockSpec((1, LANES), lambda i: (0, 0))
    out = pl.BlockSpec((TM, LANES), lambda i: (i, 0))
    return pl.pallas_call(
        body, name="rope_tables", grid=(n // TM,), in_specs=[row, vec, vec], out_specs=[out, out],
        out_shape=[jax.ShapeDtypeStruct((n, LANES), F32)] * 2, compiler_params=_params(("arbitrary",)),
    )(pos_col, jnp.asarray(freq), jnp.asarray(sign))


def _rope_apply(xv, c, s, lm):
    partner = jnp.where(lm < ROPE_HALF, pltpu.roll(xv, LANES - ROPE_HALF, 1), pltpu.roll(xv, ROPE_HALF, 1))
    return xv * c + partner * s


def _rope_bwd(dy, c, s, lm):
    t = dy * s
    partner = jnp.where(lm < ROPE_HALF, pltpu.roll(t, LANES - ROPE_HALF, 1),
                        jnp.where(lm < 2 * ROPE_HALF, pltpu.roll(t, ROPE_HALF, 1), 0.0))
    return dy * c + partner


def _lower_bound(lb_logits):
    def body(l_ref, o_ref):
        l0, l1 = l_ref[0:1, :], l_ref[1:2, :]
        m = jnp.maximum(l0, l1)
        e0, e1 = jnp.exp(l0 - m), jnp.exp(l1 - m)
        o_ref[...] = e1 / (e0 + e1)

    return pl.pallas_call(body, name="lower_bound", out_shape=jax.ShapeDtypeStruct((1, lb_logits.shape[1]), F32))(lb_logits)


def _norm_matmul(x, pw, w, bias, name, shards=()):
    n, m = x.shape[0], w.shape[1]
    cn = _col_chunk(m)
    has_bias = bias is not None
    nsh, steps = len(shards), n // TM

    def body(*refs):
        refs = list(refs)
        x_ref, pw_ref, w_ref = refs[:3]
        b_ref = refs[3] if has_bias else None
        refs = refs[4 if has_bias else 3:]
        sh_in, (p_ref, h_ref), sh_out, sems = refs[:nsh], refs[nsh:nsh + 2], refs[nsh + 2:2 * nsh + 2], refs[2 * nsh + 2:]
        if nsh:
            @pl.when(pl.program_id(0) == 0)
            def _():
                _gather_start(sh_in, sh_out, sems)

        xv = x_ref[...]
        r = lax.rsqrt(jnp.mean(xv * xv, axis=-1, keepdims=True) + NORM_EPS)
        h = ((xv * r) * pw_ref[...]).astype(BF16)
        h_ref[...] = h
        for j in range(0, m, cn):
            acc = jnp.dot(h, w_ref[:, j:j + cn], preferred_element_type=F32)
            if has_bias:
                acc = acc + b_ref[:, j:j + cn]
            p_ref[:, j:j + cn] = acc

        if nsh:
            @pl.when(pl.program_id(0) == steps - 1)
            def _():
                _gather_wait(sh_in, sh_out, sems)

    rows = pl.BlockSpec((TM, D_MODEL), lambda i: (i, 0))
    const = lambda shape: pl.BlockSpec(shape, lambda i: (0, 0))
    hbm = pl.BlockSpec(memory_space=pl.ANY)
    in_specs = [rows, const((1, D_MODEL)), const((D_MODEL, m))] + ([const((1, m))] if has_bias else []) + [hbm] * nsh
    args = (x, pw, w) + ((bias,) if has_bias else ()) + tuple(shards)
    return pl.pallas_call(
        body, name=name, grid=(steps,), in_specs=in_specs,
        out_specs=[pl.BlockSpec((TM, m), lambda i: (i, 0)), rows] + [hbm] * nsh,
        out_shape=[jax.ShapeDtypeStruct((n, m), F32), jax.ShapeDtypeStruct((n, D_MODEL), BF16)] + _gather_shapes(shards),
        scratch_shapes=_gather_sems(nsh) if nsh else [],
        compiler_params=_params(("arbitrary",), 56),
    )(*args)


def _outproj_postnorm(g, w, bias, xres, qw, tgt, name):
    n = g.shape[0]
    has_bias, has_loss = bias is not None, tgt is not None
    steps = n // TM

    def body(*refs):
        refs = list(refs)
        g_ref, w_ref = refs.pop(0), refs.pop(0)
        b_ref = refs.pop(0) if has_bias else None
        x_ref, qw_ref = refs.pop(0), refs.pop(0)
        t_ref = refs.pop(0) if has_loss else None
        y_ref, o_ref = refs.pop(0), refs.pop(0)
        y = jnp.dot(g_ref[...], w_ref[...], preferred_element_type=F32)
        if has_bias:
            y = y + b_ref[...]
        y_ref[...] = y
        r = lax.rsqrt(jnp.mean(y * y, axis=-1, keepdims=True) + NORM_EPS)
        xn = x_ref[...] + (y * r) * qw_ref[...]
        if not has_loss:
            o_ref[...] = xn
        else:
            loss_ref, acc_ref = refs
            i = pl.program_id(0)
            e = xn - t_ref[...]
            o_ref[...] = e * (1.0 / D_MODEL)

            @pl.when(i == 0)
            def _():
                acc_ref[...] = jnp.zeros_like(acc_ref)

            acc_ref[...] += jnp.sum(e * e, axis=0, keepdims=True)

            @pl.when(i == steps - 1)
            def _():
                loss_ref[...] = jnp.full(loss_ref.shape, jnp.sum(acc_ref[...]) * (0.5 / D_MODEL), F32)

    rows = pl.BlockSpec((TM, D_MODEL), lambda i: (i, 0))
    const = lambda shape: pl.BlockSpec(shape, lambda i: (0, 0))
    in_specs = [rows, const((D_MODEL, D_MODEL))] + ([const((1, D_MODEL))] if has_bias else []) + [rows, const((1, D_MODEL))]
    args = [g, w] + ([bias] if has_bias else []) + [xres, qw]
    out_specs = [rows, rows]
    out_shape = [jax.ShapeDtypeStruct((n, D_MODEL), F32)] * 2
    scratch = []
    if has_loss:
        in_specs.append(rows)
        args.append(tgt)
        out_specs.append(const((SUBLANES, LANES)))
        out_shape.append(jax.ShapeDtypeStruct((SUBLANES, LANES), F32))
        scratch = [pltpu.VMEM((1, D_MODEL), F32)]
    return pl.pallas_call(
        body, name=name, grid=(steps,), in_specs=in_specs, out_specs=out_specs, out_shape=out_shape,
        scratch_shapes=scratch, compiler_params=_params(("arbitrary",), 48),
    )(*args)


_QCOL, _ZCOL, _KCOL, _VCOL = 0, 1024, 2048, 2176


def _head_stack(chunks, heads, lt64):
    return jnp.concatenate([jnp.where(lt64 if n % 2 == 0 else ~lt64, chunks[n // 2], 0.0) for n in heads], axis=0)


def _dup_half(x, h, lt64):
    r = pltpu.roll(x, HEAD_DIM, 1)
    return jnp.where(lt64, x, r) if h == 0 else jnp.where(lt64, r, x)


def _pair_chunk(xt, c2):
    a, b = 2 * c2, 2 * c2 + 1
    return jnp.concatenate([xt[:HEAD_DIM, a * BLK:(a + 1) * BLK], xt[HEAD_DIM:, b * BLK:(b + 1) * BLK]], axis=0).T


def _attn_mask_t(i):
    key = lax.broadcasted_iota(jnp.int32, (2 * BLK, BLK), 0)
    qry = lax.broadcasted_iota(jnp.int32, (2 * BLK, BLK), 1)
    valid = (key > qry) & (key <= qry + BLK) & ((key >= BLK) | (i > 0))
    return jnp.tile(jnp.where(valid, 0.0, -1e30), (1, GROUP))


def _attn_probs_t(s, heads, sink_ref, mask):
    s = s + mask
    head = lax.broadcasted_iota(jnp.int32, (1, len(heads) * BLK), 1) >> 7
    sk = jnp.zeros((1, len(heads) * BLK), F32)
    for j, n in enumerate(heads):
        sk = jnp.where(head == j, sink_ref[0, n], sk)
    m = jnp.maximum(jnp.max(s, axis=0, keepdims=True), sk)
    p = jnp.exp(s - m)
    esk = jnp.exp(sk - m)
    inv = 1.0 / (jnp.sum(p, axis=0, keepdims=True) + esk)
    return p * inv, esk * inv


def _attn_fwd(p, ct, st, sinks, b_loc, nb, shards):
    n = p.shape[0]
    nsh = len(shards)

    def body(sink_ref, q_ref, z_ref, kc_ref, kp_ref, vc_ref, vp_ref, cc_ref, sc_ref, cp_ref, sp_ref, *rest):
        sh_in, (o_ref, g_ref), sh_out, sems = rest[:nsh], rest[nsh:nsh + 2], rest[nsh + 2:2 * nsh + 2], rest[2 * nsh + 2:]
        b, i = pl.program_id(0), pl.program_id(1)

        @pl.when((b == 0) & (i == 0))
        def _():
            _gather_start(sh_in, sh_out, sems)

        lane = lax.broadcasted_iota(jnp.int32, (BLK, LANES), 1)
        lm = lane & (HEAD_DIM - 1)
        cc, sc = cc_ref[...], sc_ref[...]
        kcat = jnp.concatenate([_rope_apply(kp_ref[...], cp_ref[...], sp_ref[...], lm),
                                _rope_apply(kc_ref[...], cc, sc, lm)], axis=0)
        vcat = jnp.concatenate([vp_ref[...], vc_ref[...]], axis=0)
        qr = [_rope_apply(q_ref[:, c * LANES:(c + 1) * LANES], cc, sc, lm) * (HEAD_DIM ** -0.5) for c in range(8)]
        valid = _attn_mask_t(i)
        lt64, lt64k = lane < HEAD_DIM, lax.broadcasted_iota(jnp.int32, (2 * BLK, LANES), 1) < HEAD_DIM
        def kv_head(h):
            heads = list(range(h * GROUP, (h + 1) * GROUP))
            kext, vext = _dup_half(kcat, h, lt64k).astype(BF16), _dup_half(vcat, h, lt64k).astype(BF16)
            qst = _head_stack(qr, heads, lt64).astype(BF16)
            s = lax.dot_general(kext, qst, NT_DIMS, preferred_element_type=F32)
            yield
            pn, _ = _attn_probs_t(s, heads, sink_ref, valid)
            ot = lax.dot_general(vext, pn.astype(BF16), TN_DIMS, preferred_element_type=F32)
            yield
            for c2 in range(GROUP // 2):
                oc = _pair_chunk(ot, c2)
                cols = slice((4 * h + c2) * LANES, (4 * h + c2 + 1) * LANES)
                zc = z_ref[:, cols]
                o_ref[:, cols] = oc
                g_ref[:, cols] = (oc * (zc * _sigmoid(zc))).astype(BF16)

        _in_stages([kv_head(h) for h in range(N_KV)])

        @pl.when((b == b_loc - 1) & (i == nb - 1))
        def _():
            _gather_wait(sh_in, sh_out, sems)

    cur = lambda b, i: b * nb + i
    prev = lambda b, i: b * nb + jnp.maximum(i - 1, 0)
    wide = lambda cb: pl.BlockSpec((BLK, ATTN_WIDTH), lambda b, i: (cur(b, i), cb))
    kv = lambda rowf, cb: pl.BlockSpec((BLK, LANES), lambda b, i: (rowf(b, i), cb))
    hbm = pl.BlockSpec(memory_space=pl.ANY)
    in_specs = [pl.BlockSpec(memory_space=pltpu.SMEM), wide(0), wide(1),
                kv(cur, _KCOL // LANES), kv(prev, _KCOL // LANES), kv(cur, _VCOL // LANES), kv(prev, _VCOL // LANES),
                kv(cur, 0), kv(cur, 0), kv(prev, 0), kv(prev, 0)] + [hbm] * nsh
    return pl.pallas_call(
        body, name="attn_fwd", grid=(b_loc, nb), in_specs=in_specs, out_specs=[wide(0), wide(0)] + [hbm] * nsh,
        out_shape=[jax.ShapeDtypeStruct((n, ATTN_WIDTH), F32), jax.ShapeDtypeStruct((n, ATTN_WIDTH), BF16)] + _gather_shapes(shards),
        scratch_shapes=_gather_sems(nsh), compiler_params=_params(("arbitrary", "arbitrary"), 48),
    )(sinks, p, p, p, p, p, p, ct, st, ct, st, *shards)


def _attn_bwd(p, ct, st, sinks, o, dg, b_loc, nb, parts):
    n = p.shape[0]
    nparts = len(parts)

    def body(sink_ref, q_ref, z_ref, kc_ref, kp_ref, vc_ref, vp_ref, cc_ref, sc_ref, cp_ref, sp_ref, o_ref, dg_ref, *rest):
        part_refs, (dp_ref, ds_ref), land_refs = rest[:nparts], rest[nparts:nparts + 2], rest[nparts + 2:2 * nparts + 2]
        dq_s, dz_s, dk_s, dv_s = rest[2 * nparts + 2:2 * nparts + 6]
        sems = rest[2 * nparts + 6:]
        b, i = pl.program_id(0), pl.program_id(1)

        @pl.when((b == 0) & (i == 0))
        def _():
            _scatter_start(part_refs, land_refs, sems)

        @pl.when((b == b_loc - 1) & (i == nb))
        def _():
            _scatter_wait(part_refs, land_refs, sems)

        lane = lax.broadcasted_iota(jnp.int32, (BLK, LANES), 1)
        lm = lane & (HEAD_DIM - 1)

        @pl.when((b == 0) & (i == 0))
        def _():
            ds_ref[...] = jnp.zeros_like(ds_ref)

        @pl.when(i < nb)
        def _compute():
            cc, sc = cc_ref[...], sc_ref[...]
            kcat = jnp.concatenate([_rope_apply(kp_ref[...], cp_ref[...], sp_ref[...], lm),
                                    _rope_apply(kc_ref[...], cc, sc, lm)], axis=0)
            vcat = jnp.concatenate([vp_ref[...], vc_ref[...]], axis=0)
            qr = [_rope_apply(q_ref[:, c * LANES:(c + 1) * LANES], cc, sc, lm) * (HEAD_DIM ** -0.5) for c in range(8)]
            valid = _attn_mask_t(i)
            lt64, lt64k = lane < HEAD_DIM, lax.broadcasted_iota(jnp.int32, (2 * BLK, LANES), 1) < HEAD_DIM
            do_chunks, doo_chunks, dz_chunks = [], [], []
            for c in range(8):
                cols = slice(c * LANES, (c + 1) * LANES)
                zc, oc, dgc = z_ref[:, cols], o_ref[:, cols], dg_ref[:, cols]
                sg = _sigmoid(zc)
                do_chunks.append(dgc * (zc * sg))
                dz_chunks.append(dgc * oc * (sg * (1.0 + zc * (1.0 - sg))))
                doo_chunks.append(do_chunks[c] * oc)
            dq_chunks = [None] * 8
            dk_h, dv_h, ds_parts = [None] * N_KV, [None] * N_KV, [None] * N_KV
            tile_lane = lax.broadcasted_iota(jnp.int32, (SUBLANES, LANES), 1)
            tile_row = lax.broadcasted_iota(jnp.int32, (SUBLANES, LANES), 0)
            ones8 = jnp.ones((SUBLANES, LANES), BF16)

            def kv_head(h):
                heads = list(range(h * GROUP, (h + 1) * GROUP))
                kext = _dup_half(kcat, h, lt64k)
                kext_bf, kext_t = kext.astype(BF16), kext.T.astype(BF16)
                vext = _dup_half(vcat, h, lt64k).astype(BF16)
                qst = _head_stack(qr, heads, lt64).astype(BF16)
                do_bf = _head_stack(do_chunks, heads, lt64).astype(BF16)
                s = lax.dot_general(kext_bf, qst, NT_DIMS, preferred_element_type=F32)
                dpt = lax.dot_general(vext, do_bf, NT_DIMS, preferred_element_type=F32)
                delta = sum(lax.dot_general(ones8, part, NT_DIMS, preferred_element_type=F32)
                            for part in _split3(_head_stack(doo_chunks, heads, lt64)))[0:1, :]
                yield
                pn, psink = _attn_probs_t(s, heads, sink_ref, valid)
                dst = (pn * (dpt - delta)).astype(BF16)
                dqt = jnp.dot(kext_t, dst, preferred_element_type=F32) * (HEAD_DIM ** -0.5)
                dk_ext = jnp.dot(dst, qst, preferred_element_type=F32)
                dv_ext = jnp.dot(pn.astype(BF16), do_bf, preferred_element_type=F32)
                yield
                sink_term = psink * delta
                ds_acc = jnp.zeros((SUBLANES, LANES), F32)
                for j, n in enumerate(heads):
                    val = -jnp.sum(sink_term[:, j * BLK:(j + 1) * BLK])
                    ds_acc = ds_acc + jnp.where((tile_lane == n) & (tile_row == 0), val, 0.0)
                ds_parts[h] = ds_acc
                for c2 in range(GROUP // 2):
                    dq_chunks[4 * h + c2] = _rope_bwd(_pair_chunk(dqt, c2), cc, sc, lm)
                dk_h[h] = dk_ext + pltpu.roll(dk_ext, HEAD_DIM, 1)
                dv_h[h] = dv_ext + pltpu.roll(dv_ext, HEAD_DIM, 1)

            _in_stages([kv_head(h) for h in range(N_KV)])
            ds_ref[...] += ds_parts[0] + ds_parts[1]
            dk_full = jnp.where(lt64k, dk_h[0], dk_h[1])
            dv_full = jnp.where(lt64k, dv_h[0], dv_h[1])

            @pl.when(i >= 1)
            def _emit():
                dp_ref[:, _QCOL:_QCOL + ATTN_WIDTH] = dq_s[...]
                dp_ref[:, _ZCOL:_ZCOL + ATTN_WIDTH] = dz_s[...]
                dp_ref[:, _KCOL:_KCOL + KV_WIDTH] = _rope_bwd(dk_s[...] + dk_full[:BLK], cp_ref[...], sp_ref[...], lm)
                dp_ref[:, _VCOL:_VCOL + KV_WIDTH] = dv_s[...] + dv_full[:BLK]

            for c in range(8):
                dq_s[:, c * LANES:(c + 1) * LANES] = dq_chunks[c]
                dz_s[:, c * LANES:(c + 1) * LANES] = dz_chunks[c]
            dk_s[...] = dk_full[BLK:]
            dv_s[...] = dv_full[BLK:]

        @pl.when(i == nb)
        def _final():
            dp_ref[:, _QCOL:_QCOL + ATTN_WIDTH] = dq_s[...]
            dp_ref[:, _ZCOL:_ZCOL + ATTN_WIDTH] = dz_s[...]
            dp_ref[:, _KCOL:_KCOL + KV_WIDTH] = _rope_bwd(dk_s[...], cc_ref[...], sc_ref[...], lm)
            dp_ref[:, _VCOL:_VCOL + KV_WIDTH] = dv_s[...]

    cur = lambda b, i: b * nb + jnp.minimum(i, nb - 1)
    prev = lambda b, i: b * nb + jnp.maximum(jnp.minimum(i, nb - 1) - 1, 0)
    emit = lambda b, i: b * nb + jnp.maximum(i - 1, 0)
    hbm = pl.BlockSpec(memory_space=pl.ANY)
    wide = lambda cb: pl.BlockSpec((BLK, ATTN_WIDTH), lambda b, i: (cur(b, i), cb))
    kv = lambda rowf, cb: pl.BlockSpec((BLK, LANES), lambda b, i: (rowf(b, i), cb))
    in_specs = [pl.BlockSpec(memory_space=pltpu.SMEM), wide(0), wide(1),
                kv(cur, _KCOL // LANES), kv(prev, _KCOL // LANES), kv(cur, _VCOL // LANES), kv(prev, _VCOL // LANES),
                kv(cur, 0), kv(cur, 0), kv(prev, 0), kv(prev, 0), wide(0), wide(0)] + [hbm] * nparts
    out_specs = [pl.BlockSpec((BLK, ATTN_IN), lambda b, i: (emit(b, i), 0)),
                 pl.BlockSpec((SUBLANES, LANES), lambda b, i: (0, 0))] + [hbm] * nparts
    return pl.pallas_call(
        body, name="attn_bwd", grid=(b_loc, nb + 1), in_specs=in_specs, out_specs=out_specs,
        out_shape=[jax.ShapeDtypeStruct((n, ATTN_IN), F32), jax.ShapeDtypeStruct((SUBLANES, LANES), F32)] + _scatter_lands(parts),
        scratch_shapes=[pltpu.VMEM((BLK, ATTN_WIDTH), F32), pltpu.VMEM((BLK, ATTN_WIDTH), F32),
                        pltpu.VMEM((BLK, KV_WIDTH), F32), pltpu.VMEM((BLK, KV_WIDTH), F32)] + _scatter_sems(nparts),
        compiler_params=_params(("arbitrary", "arbitrary"), 48),
    )(sinks, p, p, p, p, p, p, ct, st, ct, st, o, dg, *parts)


_CUM_ROWS = 256
HALF = CH // 2
_ROW0 = [SUBLANES * (s // SUBLANES) for s in range(CH)]
_ROW1 = [HALF * (s // HALF + 1) for s in range(CH)]
_ROWS_OF = [_ROW1[s] - _ROW0[s] for s in range(CH)]
_OFF_OF = [sum(_ROWS_OF[:s]) for s in range(CH)]


def _tri(lower):
    r = lax.broadcasted_iota(jnp.int32, (_CUM_ROWS, _CUM_ROWS), 0)
    c = lax.broadcasted_iota(jnp.int32, (_CUM_ROWS, _CUM_ROWS), 1)
    same = (r ^ c) < CH
    return (same & ((c <= r) if lower else (c >= r))).astype(BF16)


def _gates(qp, fp, lb):
    e = jnp.exp(-jnp.abs(fp))
    log_sig = jnp.minimum(fp, 0.0) - jnp.log(1.0 + e)
    a = jnp.log(lb)
    c = jnp.log(1.0 - lb) + log_sig
    g = jnp.maximum(a, c) + jnp.log(1.0 + jnp.exp(-jnp.abs(a - c)))
    sig_neg = jnp.where(fp >= 0, e, 1.0) / (1.0 + e)
    return qp * _sigmoid(qp), g, (1.0 - lb) * sig_neg, sig_neg


def _pair_rows(bc, s):
    return jnp.exp(jnp.minimum(bc[_ROW0[s]:_ROW1[s], :] - bc[s:s + 1, :], 0.0))


def _cross_half(q, k, bc):
    r = bc[HALF - 1:HALF, :]
    e1, e0 = jnp.exp(bc[HALF:, :] - r), jnp.exp(r - bc[:HALF, :])
    return q[HALF:, :] * e1, e1, k[:HALF, :] * e0, e0


HP = 8
REC_TB = 256
_HW = HP * REC_K


def _hgrn_specs(tb, nt, reverse):
    tmap = (lambda t: nt - 1 - t) if reverse else (lambda t: t)
    groups = REC_HEADS // HP
    blk = lambda cb: pl.BlockSpec((tb, _HW), lambda h, b, t: (b * nt + tmap(t), cb * groups + h))
    head = pl.BlockSpec((tb, _HW), lambda h, b, t: (b * nt + tmap(t), h))
    lbs = pl.BlockSpec((1, _HW), lambda h, b, t: (0, h))
    gws = pl.BlockSpec((1, REC_K), lambda h, b, t: (0, 0))
    hist = pl.BlockSpec((HP, 1, tb // CH, REC_K, REC_K), lambda h, b, t: (h, b, tmap(t), 0, 0))
    return blk, head, lbs, gws, hist


def _chunk_rows(c, first=0, size=CH):
    start = c * CH + first
    return pl.ds(start if isinstance(start, int) else pl.multiple_of(start, CH if first % CH == 0 else SUBLANES), size)


def _in_stages(heads):
    live = list(heads)
    while live:
        live = [g for g in live if next(g, live) is not live]


def _cumsum_chunks(tri, x, out_ref, tb):
    for r in range(0, tb, _CUM_ROWS):
        out_ref[r:r + _CUM_ROWS, :] = _dot3(tri, x[r:r + _CUM_ROWS, :])


def _hgrn_fwd(p, lb, gw, b_loc, t_len):
    n = p.shape[0]
    tb = min(REC_TB, t_len)
    nt, nck = t_len // tb, tb // CH

    def body(qp_ref, fp_ref, i_ref, z_ref, lb_ref, gw_ref, oraw_ref, g_ref, sh_ref, q_s, k_s, b_s, o_s, st_ref,
             car_o, car_a, car_s, car_st):
        @pl.when(pl.program_id(2) == 0)
        def _():
            st_ref[...] = jnp.zeros_like(st_ref)

        qv, g, kk, _ = _gates(qp_ref[...], fp_ref[...], lb_ref[...])
        q_s[...] = qv
        k_s[...] = kk
        _cumsum_chunks(_tri(True), g, b_s, tb)
        ones = jnp.ones((REC_K, REC_K), BF16)
        sub = lax.broadcasted_iota(jnp.int32, (SUBLANES, REC_K), 0)

        rows_of = _chunk_rows

        def issue(c, hp):
            rs, cs = rows_of(c), slice(hp * REC_K, (hp + 1) * REC_K)
            q, k, bc, v = q_s[rs, cs], k_s[rs, cs], b_s[rs, cs], i_ref[rs, cs]
            st = st_ref[hp]
            sh_ref[hp, 0, c] = st
            o = lax.dot_general((q * jnp.exp(bc)).astype(BF16), st.astype(BF16), NT_DIMS, preferred_element_type=F32)
            w = jnp.concatenate([q[_ROW0[s]:_ROW1[s], :] * _pair_rows(bc, s) * k[s:s + 1, :] for s in range(CH)], axis=0)
            a = jnp.dot(w.astype(BF16), ones, preferred_element_type=F32)
            qe1, _, ke0, _ = _cross_half(q, k, bc)
            s10 = lax.dot_general(qe1.astype(BF16), ke0.astype(BF16), NT_DIMS, preferred_element_type=F32)
            kd = k * jnp.exp(bc[CH - 1:CH, :] - bc)
            st_new = lax.dot_general(v.astype(BF16), kd.astype(BF16), TN_DIMS, preferred_element_type=F32)
            return o, a, s10, st_new

        def advance_state(c, hp, st_new):
            bl = b_s[_chunk_rows(c, CH - SUBLANES, SUBLANES), hp * REC_K:(hp + 1) * REC_K][SUBLANES - 1:, :]
            st_ref[hp] = st_ref[hp] * jnp.exp(bl) + st_new

        def cross(c, hp, s10):
            v0 = i_ref[_chunk_rows(c, 0, HALF), hp * REC_K:(hp + 1) * REC_K]
            return jnp.dot(s10.astype(BF16), v0.astype(BF16), preferred_element_type=F32)

        def finish(c, hp, o, a, o_cross):
            rs, cs = rows_of(c), slice(hp * REC_K, (hp + 1) * REC_K)
            v = i_ref[rs, cs]
            acc = [jnp.zeros((SUBLANES, REC_K), F32) for _ in range(CH // SUBLANES)]
            for s in range(CH):
                j = s // SUBLANES
                vs = v[s:s + 1, :]
                for jj in range(j, _ROW1[s] // SUBLANES):
                    blk = a[_OFF_OF[s] + (jj - j) * SUBLANES:_OFF_OF[s] + (jj - j + 1) * SUBLANES, :]
                    if jj == j:
                        blk = jnp.where(sub >= s - j * SUBLANES, blk, 0.0)
                    acc[jj] = acc[jj] + blk * vs
            o_s[rs, cs] = o + jnp.concatenate(acc, axis=0) + jnp.concatenate([jnp.zeros((HALF, REC_K), F32), o_cross], axis=0)

        def park(slot, results):
            for hp, (o, a, s10, st_new) in enumerate(results):
                car_o[slot, hp], car_a[slot, hp], car_s[slot, hp], car_st[slot, hp] = o, a, s10, st_new

        def retire(c, slot):
            for hp in range(HP):
                advance_state(c, hp, car_st[slot, hp])
            yield
            crosses = [cross(c, hp, car_s[slot, hp]) for hp in range(HP)]
            for hp in range(HP):
                finish(c, hp, car_o[slot, hp], car_a[slot, hp], crosses[hp])

        def step(c, slot):
            closing = retire(c - 1, slot)
            next(closing)
            park(1 - slot, [issue(c, hp) for hp in range(HP)])
            next(closing, None)

        def trip(j, carry):
            step(2 * j + 1, 0)
            step(2 * j + 2, 1)
            return carry

        assert nck % 2 == 0
        park(0, [issue(0, hp) for hp in range(HP)])
        lax.fori_loop(0, nck // 2 - 1, trip, 0)
        step(nck - 1, 0)
        for _ in retire(nck - 1, 1):
            pass
        oraw_ref[...] = o_s[...]
        for hp in range(HP):
            cs = slice(hp * REC_K, (hp + 1) * REC_K)
            o, zc = o_s[:, cs], z_ref[:, cs]
            on = (o * lax.rsqrt(jnp.mean(o * o, axis=-1, keepdims=True) + NORM_EPS)) * gw_ref[...]
            g_ref[:, cs] = (on * (zc * _sigmoid(zc))).astype(BF16)

    blk, head, lbs, gws, hist = _hgrn_specs(tb, nt, False)
    return pl.pallas_call(
        body, name="hgrn_fwd", grid=(REC_HEADS // HP, b_loc, nt),
        in_specs=[blk(0), blk(1), blk(2), blk(3), lbs, gws], out_specs=[head, head, hist],
        out_shape=[jax.ShapeDtypeStruct((n, 1024), F32), jax.ShapeDtypeStruct((n, 1024), BF16),
                   jax.ShapeDtypeStruct((REC_HEADS, b_loc, t_len // CH, REC_K, REC_K), F32)],
        scratch_shapes=[pltpu.VMEM((tb, _HW), F32)] * 4 + [pltpu.VMEM((HP, REC_K, REC_K), F32)] + [
            pltpu.VMEM((2, HP, CH, REC_K), F32), pltpu.VMEM((2, HP, sum(_ROWS_OF), REC_K), F32),
            pltpu.VMEM((2, HP, HALF, HALF), F32), pltpu.VMEM((2, HP, REC_K, REC_K), F32)],
        compiler_params=_params(("arbitrary", "arbitrary", "arbitrary"), 48),
    )(p, p, p, p, lb, gw)


def _hgrn_bwd(p, lb, gw, oraw, sh, dg, b_loc, t_len):
    n = p.shape[0]
    tb = min(REC_TB, t_len)
    nt, nck = t_len // tb, tb // CH

    def body(qp_ref, fp_ref, i_ref, z_ref, lb_ref, gw_ref, oraw_ref, dg_ref, sh_ref,
             dq_ref, df_ref, di_ref, dz_ref, dlb_ref, dgw_ref,
             q_s, k_s, b_s, do_s, dqv_s, dk_s, db_s, rowk_s, rowv_s, dst_ref, car_r, car_dst, car_dec, car_a, car_da, car_x):
        b, t = pl.program_id(1), pl.program_id(2)

        @pl.when(t == 0)
        def _():
            dst_ref[...] = jnp.zeros_like(dst_ref)

        @pl.when((b == 0) & (t == 0))
        def _():
            dlb_ref[...] = jnp.zeros_like(dlb_ref)
            dgw_ref[...] = jnp.zeros_like(dgw_ref)

        lbv, qp, fp = lb_ref[...], qp_ref[...], fp_ref[...]
        qv, g, kk, sig_neg = _gates(qp, fp, lbv)
        q_s[...] = qv
        k_s[...] = kk
        _cumsum_chunks(_tri(True), g, b_s, tb)
        gwv = gw_ref[...]
        for hp in range(HP):
            cs = slice(hp * REC_K, (hp + 1) * REC_K)
            o, zc, dgv = oraw_ref[:, cs], z_ref[:, cs], dg_ref[:, cs]
            rn = lax.rsqrt(jnp.mean(o * o, axis=-1, keepdims=True) + NORM_EPS)
            on = o * rn
            sgz = _sigmoid(zc)
            dz_ref[:, cs] = (dgv * (on * gwv) * (sgz * (1.0 + zc * (1.0 - sgz)))).astype(BF16)
            dpre = dgv * (zc * sgz)
            dgw_ref[hp] += jnp.sum(dpre * on, axis=0, keepdims=True)
            don = dpre * gwv
            do_s[:, cs] = rn * (don - on * jnp.mean(don * on, axis=-1, keepdims=True))

        ones = jnp.ones((REC_K, REC_K), BF16)
        sub = lax.broadcasted_iota(jnp.int32, (SUBLANES, REC_K), 0)
        rowid = lax.broadcasted_iota(jnp.int32, (CH, REC_K), 0)
        ngrp = CH // SUBLANES

        def operands(c, hp):
            rs, cs = _chunk_rows(c), slice(hp * REC_K, (hp + 1) * REC_K)
            return rs, cs, q_s[rs, cs], k_s[rs, cs], b_s[rs, cs], i_ref[rs, cs], do_s[rs, cs]

        def issue(c, hp, slot):
            _, _, q, k, bc, v, do = operands(c, hp)
            st, dst = sh_ref[hp, 0, c], dst_ref[hp]
            qe, kd = q * jnp.exp(bc), k * jnp.exp(bc[CH - 1:CH, :] - bc)
            do_bf, dst_bf = do.astype(BF16), dst.astype(BF16)
            car_r[slot, hp, 0:CH] = jnp.dot(do_bf, st.astype(BF16), preferred_element_type=F32)
            car_r[slot, hp, CH:2 * CH] = jnp.dot(v.astype(BF16), dst_bf, preferred_element_type=F32)
            car_r[slot, hp, 2 * CH:3 * CH] = lax.dot_general(kd.astype(BF16), dst_bf, NT_DIMS, preferred_element_type=F32)
            car_dst[slot, hp] = lax.dot_general(do_bf, qe.astype(BF16), TN_DIMS, preferred_element_type=F32)
            dec = jnp.concatenate([_pair_rows(bc, s) for s in range(CH)], axis=0)
            qk = jnp.concatenate([q[_ROW0[s]:_ROW1[s], :] * k[s:s + 1, :] for s in range(CH)], axis=0)
            x = jnp.concatenate([do[_ROW0[s]:_ROW1[s], :] * v[s:s + 1, :] for s in range(CH)], axis=0)
            car_dec[slot, hp] = dec
            car_a[slot, hp] = jnp.dot((qk * dec).astype(BF16), ones, preferred_element_type=F32)
            car_da[slot, hp] = jnp.dot(x.astype(BF16), ones, preferred_element_type=F32)
            qe1, _, ke0, _ = _cross_half(q, k, bc)
            qe1_bf, ke0_bf = qe1.astype(BF16), ke0.astype(BF16)
            do1_bf, v0_bf = do[HALF:, :].astype(BF16), v[:HALF, :].astype(BF16)
            car_x[slot, hp, 0:HALF] = lax.dot_general(ke0_bf, qe1_bf, NT_DIMS, preferred_element_type=F32)
            car_x[slot, hp, HALF:2 * HALF] = lax.dot_general(do1_bf, v0_bf, NT_DIMS, preferred_element_type=F32)
            car_x[slot, hp, 2 * HALF:3 * HALF] = lax.dot_general(v0_bf, do1_bf, NT_DIMS, preferred_element_type=F32)

        def advance_state(c, hp, slot):
            ebl = jnp.exp(b_s[_chunk_rows(c, CH - SUBLANES, SUBLANES), hp * REC_K:(hp + 1) * REC_K][SUBLANES - 1:, :])
            st, dst = sh_ref[hp, 0, c], dst_ref[hp]
            dst_ref[hp] = dst * ebl + car_dst[slot, hp]
            return ebl * jnp.sum(st * dst, axis=0, keepdims=True)

        def cross(c, hp, slot):
            _, _, q, k, bc, v, do = operands(c, hp)
            qe1, _, ke0, _ = _cross_half(q, k, bc)
            xs = car_x[slot, hp]
            dqe1 = jnp.dot(xs[HALF:2 * HALF].astype(BF16), ke0.astype(BF16), preferred_element_type=F32)
            dke0 = jnp.dot(xs[2 * HALF:].astype(BF16), qe1.astype(BF16), preferred_element_type=F32)
            dv1 = jnp.dot(xs[:HALF].astype(BF16), do[HALF:, :].astype(BF16), preferred_element_type=F32)
            return dqe1, dke0, dv1

        def retire(c, slot):
            dbl_state = [advance_state(c, hp, slot) for hp in range(HP)]
            yield
            crossed = [cross(c, hp, slot) for hp in range(HP)]
            for hp in range(HP):
                finish(c, hp, slot, dbl_state[hp], *crossed[hp])

        def step(c, slot):
            closing = retire(c + 1, slot)
            next(closing)
            for hp in range(HP):
                issue(c, hp, 1 - slot)
            next(closing, None)

        def trip(j, carry):
            step(nck - 2 - 2 * j, 0)
            step(nck - 3 - 2 * j, 1)
            return carry

        def finish(c, hp, slot, dbl_state, dqe1, dke0, dv1):
            rs, cs, q, k, bc, v, do = operands(c, hp)
            rowk, rowv = rowk_s.at[hp], rowv_s.at[hp]
            eb, ekd = jnp.exp(bc), jnp.exp(bc[CH - 1:CH, :] - bc)
            qe, kd = q * eb, k * ekd
            qe1, e1, ke0, e0 = _cross_half(q, k, bc)
            dqe, dkd, dv = car_r[slot, hp, 0:CH], car_r[slot, hp, CH:2 * CH], car_r[slot, hp, 2 * CH:3 * CH]
            a, da, decs = car_a[slot, hp], car_da[slot, hp], car_dec[slot, hp]
            dec = [decs[_OFF_OF[s]:_OFF_OF[s] + _ROWS_OF[s], :] for s in range(CH)]
            dbl = jnp.sum(dkd * kd, axis=0, keepdims=True) + dbl_state
            dq_acc = [jnp.zeros((SUBLANES, REC_K), F32) for _ in range(ngrp)]
            for s in range(CH):
                j = s // SUBLANES
                r0 = j * SUBLANES
                ks = k[s:s + 1, :]
                uk = jnp.zeros((SUBLANES, REC_K), F32)
                uv = jnp.zeros((SUBLANES, REC_K), F32)
                for jj in range(j, _ROW1[s] // SUBLANES):
                    lo, hi = _OFF_OF[s] + (jj - j) * SUBLANES, _OFF_OF[s] + (jj - j + 1) * SUBLANES
                    a_blk, da_blk = a[lo:hi, :], da[lo:hi, :]
                    if jj == j:
                        keep = sub >= s - r0
                        a_blk, da_blk = jnp.where(keep, a_blk, 0.0), jnp.where(keep, da_blk, 0.0)
                    rows = slice(jj * SUBLANES, (jj + 1) * SUBLANES)
                    tt = da_blk * dec[s][(jj - j) * SUBLANES:(jj - j + 1) * SUBLANES, :]
                    dq_acc[jj] = dq_acc[jj] + tt * ks
                    uk = uk + tt * q[rows, :]
                    uv = uv + a_blk * do[rows, :]
                rowk[s:s + 1, :] = jnp.sum(uk, axis=0, keepdims=True)
                rowv[s:s + 1, :] = jnp.sum(uv, axis=0, keepdims=True)
            zero_half = jnp.zeros((HALF, REC_K), F32)
            dq_x = jnp.concatenate([zero_half, dqe1 * e1], axis=0)
            dk_x = jnp.concatenate([dke0 * e0, zero_half], axis=0)
            dv_x = jnp.concatenate([dv1, zero_half], axis=0)
            db_x = jnp.concatenate([-(dke0 * ke0), dqe1 * qe1], axis=0)
            dq_in = jnp.concatenate(dq_acc, axis=0)
            dk_in = rowk[...]
            dqv_s[rs, cs] = dqe * eb + dq_in + dq_x
            dk_s[rs, cs] = dkd * ekd + dk_in + dk_x
            di_ref[rs, cs] = (dv + rowv[...] + dv_x).astype(BF16)
            db = dqe * qe - dkd * kd + q * dq_in - k * dk_in + db_x
            db_s[rs, cs] = db + jnp.where(rowid == CH - 1, dbl, 0.0)

        assert nck % 2 == 0
        for hp in range(HP):
            issue(nck - 1, hp, 0)
        lax.fori_loop(0, nck // 2 - 1, trip, 0)
        step(0, 0)
        for _ in retire(0, 1):
            pass
        up = _tri(False)
        sgq = _sigmoid(qp)
        dq_ref[...] = (dqv_s[...] * (sgq * (1.0 + qp * (1.0 - sgq)))).astype(BF16)
        dlb_acc = jnp.zeros((1, _HW), F32)
        for r in range(0, tb, _CUM_ROWS):
            rows = slice(r, r + _CUM_ROWS)
            dgl = _dot3(up, db_s[rows, :])
            dfg = dgl * jnp.exp(-g[rows, :]) - dk_s[rows, :]
            sn = sig_neg[rows, :]
            df_ref[rows, :] = (dfg * (1.0 - lbv) * (1.0 - sn) * sn).astype(BF16)
            dlb_acc = dlb_acc + jnp.sum(dfg * sn, axis=0, keepdims=True)
        dlb_ref[...] += dlb_acc

    blk, head, lbs, gws, hist = _hgrn_specs(tb, nt, True)
    out_specs = [head, head, head, head, lbs, pl.BlockSpec((HP, 1, REC_K), lambda h, b, t: (h, 0, 0))]
    out_shape = [jax.ShapeDtypeStruct((n, 1024), BF16)] * 4 + [
        jax.ShapeDtypeStruct((1, 1024), F32), jax.ShapeDtypeStruct((REC_HEADS, 1, REC_K), F32)]
    return pl.pallas_call(
        body, name="hgrn_bwd", grid=(REC_HEADS // HP, b_loc, nt),
        in_specs=[blk(0), blk(1), blk(2), blk(3), lbs, gws, head, head, hist], out_specs=out_specs, out_shape=out_shape,
        scratch_shapes=[pltpu.VMEM((tb, _HW), F32)] * 7 + [pltpu.VMEM((HP, CH, REC_K), F32)] * 2 + [pltpu.VMEM((HP, REC_K, REC_K), F32)] + [
            pltpu.VMEM((2, HP, 3 * CH, REC_K), F32), pltpu.VMEM((2, HP, REC_K, REC_K), F32)] + [
            pltpu.VMEM((2, HP, sum(_ROWS_OF), REC_K), F32)] * 3 + [pltpu.VMEM((2, HP, 3 * HALF, HALF), F32)],
        compiler_params=_params(("arbitrary", "arbitrary", "arbitrary"), 56),
    )(p, p, p, p, lb, gw, oraw, dg, sh)


def _postnorm_bwd_nt(dxo, y, qw, w, has_bias, name):
    n = dxo.shape[0]

    def body(dx_ref, y_ref, qw_ref, w_ref, dg_ref, dy_ref, dqw_ref, db_ref):
        @pl.when(pl.program_id(0) == 0)
        def _():
            dqw_ref[...] = jnp.zeros_like(dqw_ref)
            db_ref[...] = jnp.zeros_like(db_ref)

        yv, dxv = y_ref[...], dx_ref[...]
        r = lax.rsqrt(jnp.mean(yv * yv, axis=-1, keepdims=True) + NORM_EPS)
        u = yv * r
        du = dxv * qw_ref[...]
        dy = r * (du - u * jnp.mean(du * u, axis=-1, keepdims=True))
        dqw_ref[...] += jnp.sum(dxv * u, axis=0, keepdims=True)
        if has_bias:
            db_ref[...] += jnp.sum(dy, axis=0, keepdims=True)
        dyb = dy.astype(BF16)
        dy_ref[...] = dyb
        dg_ref[...] = lax.dot_general(dyb, w_ref[...], NT_DIMS, preferred_element_type=F32)

    rows = pl.BlockSpec((TM, D_MODEL), lambda i: (i, 0))
    const = lambda shape: pl.BlockSpec(shape, lambda i: (0, 0))
    return pl.pallas_call(
        body, name=name, grid=(n // TM,), in_specs=[rows, rows, const((1, D_MODEL)), const((D_MODEL, D_MODEL))],
        out_specs=[rows, rows, const((1, D_MODEL)), const((1, D_MODEL))],
        out_shape=[jax.ShapeDtypeStruct((n, D_MODEL), F32), jax.ShapeDtypeStruct((n, D_MODEL), BF16),
                   jax.ShapeDtypeStruct((1, D_MODEL), F32), jax.ShapeDtypeStruct((1, D_MODEL), F32)],
        compiler_params=_params(("arbitrary",), 48),
    )(dxo, y, qw, w)


def _nt_prenorm_bwd(dps, w, x, pw, dxo, has_bias, name, parts=()):
    n = x.shape[0]
    widths = [d.shape[1] for d in dps]
    m = sum(widths)
    npieces, nparts, steps = len(dps), len(parts), n // TM

    def body(*refs):
        dp_refs = refs[:npieces]
        w_ref, x_ref, pw_ref, dxo_ref = refs[npieces:npieces + 4]
        part_refs = refs[npieces + 4:npieces + 4 + nparts]
        dx_ref, dpw_ref, db_ref = refs[npieces + 4 + nparts:npieces + 7 + nparts]
        land_refs = refs[npieces + 7 + nparts:npieces + 7 + 2 * nparts]
        sems = refs[npieces + 7 + 2 * nparts:]

        @pl.when(pl.program_id(0) == 0)
        def _():
            dpw_ref[...] = jnp.zeros_like(dpw_ref)
            db_ref[...] = jnp.zeros_like(db_ref)
            if nparts:
                _scatter_start(part_refs, land_refs, sems)

        dh = jnp.zeros((TM, D_MODEL), F32)
        off = 0
        for dp_ref, wd in zip(dp_refs, widths):
            cn = _col_chunk(wd)
            for j in range(0, wd, cn):
                dpc = dp_ref[:, j:j + cn]
                if has_bias:
                    db_ref[:, off + j:off + j + cn] += jnp.sum(dpc, axis=0, keepdims=True)
                dh = dh + lax.dot_general(dpc.astype(BF16), w_ref[:, off + j:off + j + cn], NT_DIMS, preferred_element_type=F32)
            off += wd
        xv = x_ref[...]
        r = lax.rsqrt(jnp.mean(xv * xv, axis=-1, keepdims=True) + NORM_EPS)
        xn = xv * r
        dpw_ref[...] += jnp.sum(dh * xn, axis=0, keepdims=True)
        dxn = dh * pw_ref[...]
        dx_ref[...] = dxo_ref[...] + r * (dxn - xn * jnp.mean(dxn * xn, axis=-1, keepdims=True))

        if nparts:
            @pl.when(pl.program_id(0) == steps - 1)
            def _():
                _scatter_wait(part_refs, land_refs, sems)

    rows = pl.BlockSpec((TM, D_MODEL), lambda i: (i, 0))
    const = lambda shape: pl.BlockSpec(shape, lambda i: (0, 0))
    hbm = pl.BlockSpec(memory_space=pl.ANY)
    in_specs = ([pl.BlockSpec((TM, wd), lambda i: (i, 0)) for wd in widths] + [const((D_MODEL, m)), rows, const((1, D_MODEL)), rows]
                + [hbm] * nparts)
    return pl.pallas_call(
        body, name=name, grid=(steps,), in_specs=in_specs,
        out_specs=[rows, const((1, D_MODEL)), const((1, m))] + [hbm] * nparts,
        out_shape=[jax.ShapeDtypeStruct((n, D_MODEL), F32), jax.ShapeDtypeStruct((1, D_MODEL), F32),
                   jax.ShapeDtypeStruct((1, m), F32)] + _scatter_lands(parts),
        scratch_shapes=_scatter_sems(nparts) if nparts else [],
        compiler_params=_params(("arbitrary",), 56),
    )(*dps, w, x, pw, dxo, *parts)


def _matmul_tn(a, b, name):
    n, k = a.shape
    m = b.shape[1]
    tk, tm, tn = k, _col_chunk(m), 1024 if n % 1024 == 0 else n

    def body(a_ref, b_ref, o_ref):
        @pl.when(pl.program_id(2) == 0)
        def _():
            o_ref[...] = jnp.zeros_like(o_ref)

        o_ref[...] += lax.dot_general(a_ref[...], b_ref[...].astype(BF16), TN_DIMS, preferred_element_type=F32)

    return pl.pallas_call(
        body, name=name, grid=(k // tk, m // tm, n // tn),
        in_specs=[pl.BlockSpec((tn, tk), lambda i, j, l: (l, i)), pl.BlockSpec((tn, tm), lambda i, j, l: (l, j))],
        out_specs=pl.BlockSpec((tk, tm), lambda i, j, l: (i, j)),
        out_shape=jax.ShapeDtypeStruct((k, m), F32),
        compiler_params=_params(("arbitrary", "arbitrary", "arbitrary"), 48),
    )(a, b)


def _by_owner_cols(dw):
    k, m = dw.shape
    return dw.reshape(k, N_DEV, m // N_DEV).transpose(1, 0, 2)


def _own_and_bf16(part):
    return lax.dynamic_index_in_dim(part, _my_id(), 0, keepdims=False), part.astype(BF16)


def _step(x, pos_col, tgt, pre_w, post_w, wa_in, ba_in, sinks, wa_out_shard, ba_out, wr_in_shard, lb_logits, gnorm_w, wr_out_shard, b_loc, t_len):
    nb = t_len // BLK
    ct, st = _rope_tables(pos_col)
    lb = _lower_bound(lb_logits)
    p0, h0, ga_out = _norm_matmul(x, pre_w[0:1], wa_in, ba_in, "attn_in_proj", [wa_out_shard])
    wa_out = ga_out.reshape(ATTN_WIDTH, D_MODEL)
    o0, g0, gr_in, gr_out = _attn_fwd(p0, ct, st, sinks, b_loc, nb, [wr_in_shard, wr_out_shard])
    wr_in = gr_in.transpose(1, 0, 2).reshape(D_MODEL, REC_IN)
    wr_out = gr_out.reshape(1024, D_MODEL)
    y0, x1 = _outproj_postnorm(g0, wa_out, ba_out, x, post_w[0:1], None, "attn_out_proj")
    p1, h1 = _norm_matmul(x1, pre_w[1:2], wr_in, None, "rec_in_proj")
    o1, g1, sh = _hgrn_fwd(p1, lb, gnorm_w, b_loc, t_len)
    y1, dx2, loss_tile = _outproj_postnorm(g1, wr_out, None, x1, post_w[1:2], tgt, "rec_out_proj_loss")
    dg1, dy1, dpost1, _ = _postnorm_bwd_nt(dx2, y1, post_w[1:2], wr_out, False, "rec_out_bwd")
    d_wr_out = _matmul_tn(g1, dy1, "rec_w_out_grad")
    dq1, df1, di1, dz1, dlb, dgw = _hgrn_bwd(p1, lb, gnorm_w, o1, sh, dg1, b_loc, t_len)
    dps1 = [dq1, df1, di1, dz1]
    dx1, dpre1, _ = _nt_prenorm_bwd(dps1, wr_in, x1, pre_w[1:2], dx2, False, "rec_in_bwd")
    d_wr_in = [_matmul_tn(h1, dpk, f"rec_w_in_grad_{k}") for k, dpk in enumerate(dps1)]
    dg0, dy0, dpost0, dba_out = _postnorm_bwd_nt(dx1, y0, post_w[0:1], wa_out, True, "attn_out_bwd")
    d_wa_out = _matmul_tn(g0, dy0, "attn_w_out_grad")
    owns, wires = zip(*[_own_and_bf16(part) for part in (
        _by_owner_cols(jnp.concatenate(d_wr_in, axis=1)), d_wr_out.reshape(N_DEV, 1024 // N_DEV, D_MODEL),
        d_wa_out.reshape(N_DEV, ATTN_WIDTH // N_DEV, D_MODEL))])
    dp0, dsink_tile, *lands = _attn_bwd(p0, ct, st, sinks, o0, dg0, b_loc, nb, list(wires))
    d_wa_in = _matmul_tn(h0, dp0, "attn_w_in_grad")
    own_a_in, wire_a_in = _own_and_bf16(_by_owner_cols(_qkvz(d_wa_in)))
    dx0, dpre0, dba_in, land_a_in = _nt_prenorm_bwd([dp0], wa_in, x, pre_w[0:1], dx1, True, "attn_in_bwd", [wire_a_in])
    small = dict(pre=jnp.concatenate([dpre0, dpre1], axis=0), post=jnp.concatenate([dpost0, dpost1], axis=0),
                 ba_in=dba_in, sinks=dsink_tile[0:1, 0:N_HEADS], ba_out=dba_out, lb=dlb, gnorm=jnp.sum(dgw, axis=0))
    return loss_tile, dx0, list(zip(lands, owns)) + [(land_a_in, own_a_in)], small


def _my_id():
    return lax.axis_index("x") * 4 + lax.axis_index("y") * 2 + lax.axis_index("c")


def _peer(k):
    x, y, c = lax.axis_index("x"), lax.axis_index("y"), lax.axis_index("c")
    return (x ^ ((k >> 2) & 1), y ^ ((k >> 1) & 1), c ^ (k & 1))


def _peer_id(k):
    return _my_id() ^ k


def _all_gather(shards):
    nsh = len(shards)

    def body(*refs):
        ins, outs, sems = refs[:nsh], refs[nsh:2 * nsh], refs[2 * nsh:]
        _gather_start(ins, outs, sems)
        _gather_wait(ins, outs, sems)

    hbm = pl.BlockSpec(memory_space=pl.ANY)
    return pl.pallas_call(
        body, name="comm_all_gather", in_specs=[hbm] * nsh, out_specs=[hbm] * nsh,
        out_shape=_gather_shapes(shards), scratch_shapes=_gather_sems(nsh),
    )(*shards)


def _gather_shapes(shards):
    return [jax.ShapeDtypeStruct((N_DEV,) + s.shape, s.dtype) for s in shards]


def _gather_sems(nsh):
    return [pltpu.SemaphoreType.DMA((nsh, N_DEV - 1)), pltpu.SemaphoreType.DMA((nsh, N_DEV - 1)), pltpu.SemaphoreType.DMA((nsh,))]


def _gather_copies(ins, outs, sems, received):
    send_sems, recv_sems, local_sems = sems
    me = _my_id()
    local = [pltpu.make_async_copy(ins[a], outs[a].at[me], local_sems.at[a]) for a in range(len(ins))]
    remote = [pltpu.make_async_remote_copy(
        src_ref=ins[a], dst_ref=outs[a].at[_peer_id(k) if received else me], send_sem=send_sems.at[a, k - 1],
        recv_sem=recv_sems.at[a, k - 1], device_id=_peer(k), device_id_type=MESH)
        for a in range(len(ins)) for k in range(1, N_DEV)]
    return local, remote


def _gather_start(ins, outs, sems):
    local, sends = _gather_copies(ins, outs, sems, False)
    for cp in local + sends:
        cp.start()


def _gather_wait(ins, outs, sems):
    local, recvs = _gather_copies(ins, outs, sems, True)
    for cp in recvs:
        cp.wait_recv()
    for cp in recvs:
        cp.wait_send()
    for cp in local:
        cp.wait()


def _scatter_lands(parts):
    return [jax.ShapeDtypeStruct((N_DEV - 1,) + p.shape[1:], p.dtype) for p in parts]


def _scatter_sems(nparts):
    return [pltpu.SemaphoreType.DMA((nparts, N_DEV - 1)), pltpu.SemaphoreType.DMA((nparts, N_DEV - 1))]


def _scatter_copies(parts, lands, sems):
    send_sems, recv_sems = sems
    return [pltpu.make_async_remote_copy(
        src_ref=parts[a].at[_peer_id(k)], dst_ref=lands[a].at[k - 1], send_sem=send_sems.at[a, k - 1],
        recv_sem=recv_sems.at[a, k - 1], device_id=_peer(k), device_id_type=MESH)
        for a in range(len(parts)) for k in range(1, N_DEV)]


def _scatter_start(parts, lands, sems):
    for cp in _scatter_copies(parts, lands, sems):
        cp.start()


def _scatter_wait(parts, lands, sems):
    copies = _scatter_copies(parts, lands, sems)
    for cp in copies:
        cp.wait_recv()
    for cp in copies:
        cp.wait_send()


def _adamw(w, g, m, v):
    m2 = ADAM_B1 * m + (1.0 - ADAM_B1) * g
    v2 = ADAM_B2 * v + (1.0 - ADAM_B2) * (g * g)
    m_hat = m2 / (1.0 - ADAM_B1 ** ADAM_STEP)
    v_hat = v2 / (1.0 - ADAM_B2 ** ADAM_STEP)
    delta = -ADAM_LR * (m_hat / (jnp.sqrt(v_hat) + ADAM_EPS) + ADAM_WD * w)
    return delta, m2, v2


def _sum_adamw_rows(land_ref, own_ref, w_ref, m_ref, v_ref, out_refs):
    r, c = own_ref.shape
    rc = 64 if r % 64 == 0 else r
    me = _my_id()
    g_ref, d_ref, m2_ref, v2_ref = out_refs

    def rows(i, carry):
        rs = pl.ds(pl.multiple_of(i * rc, rc), rc)
        g = jnp.zeros((rc, c), F32)
        for dev in range(N_DEV):
            k = dev ^ me
            g = g + jnp.where(k == 0, own_ref[rs, :], land_ref[jnp.maximum(k - 1, 0), rs, :].astype(F32))
        delta, m2, v2 = _adamw(w_ref[rs, :], g, m_ref[rs, :], v_ref[rs, :])
        g_ref[rs, :] = g
        d_ref[rs, :] = delta
        m2_ref[rs, :] = m2
        v2_ref[rs, :] = v2
        return carry

    lax.fori_loop(0, r // rc, rows, 0)


def _sum_adamw(land, own, w, m, v, name):
    r, c = own.shape

    def body(land_ref, own_ref, w_ref, m_ref, v_ref, g_ref, d_ref, m2_ref, v2_ref):
        _sum_adamw_rows(land_ref, own_ref, w_ref, m_ref, v_ref, (g_ref, d_ref, m2_ref, v2_ref))

    vmem = pl.BlockSpec(memory_space=pltpu.VMEM)
    return pl.pallas_call(
        body, name=name, in_specs=[vmem] * 5, out_specs=[vmem] * 4, out_shape=[jax.ShapeDtypeStruct((r, c), F32)] * 4,
        compiler_params=_params(None, 56),
    )(land, own, w, m, v)


_SMALL = [("pre_norm_w", 2048), ("post_norm_w", 2048), ("attn_b_in", 2304), ("attn_sinks", 16), ("attn_b_out", 1024),
          ("rec_lb_logits", 2048), ("rec_gnorm_w", 128)]
_TILE = SUBLANES * LANES


def _small_rows(size):
    return -(-size // _TILE) * SUBLANES


_SMALL_OFF = {}
_r = 0
for _name, _size in _SMALL:
    _SMALL_OFF[_name] = _r
    _r += _small_rows(_size)
_SMALL_ROWS = _r


def _pack_small(pieces):
    out = []
    for name, size in _SMALL:
        flat = pieces[name].reshape(-1).astype(F32)
        out.append(jnp.pad(flat, (0, _small_rows(size) * LANES - size)).reshape(-1, LANES))
    return jnp.concatenate(out, axis=0)


def _unpack_small(packed, shapes):
    return {name: packed[_SMALL_OFF[name]:_SMALL_OFF[name] + _small_rows(size)].reshape(-1)[:size].reshape(shapes[name])
            for name, size in _SMALL}


def _small_allreduce_adamw(gpart, w, m, v):
    lb0 = _SMALL_OFF["rec_lb_logits"]

    def body(gp_ref, w_ref, m_ref, v_ref, g_ref, d_ref, m2_ref, v2_ref, land_ref, send_sems, recv_sems):
        me = _my_id()
        sent = []
        for k in range(1, N_DEV):
            cp = pltpu.make_async_remote_copy(src_ref=gp_ref, dst_ref=land_ref.at[k - 1], send_sem=send_sems.at[k - 1],
                                              recv_sem=recv_sems.at[k - 1], device_id=_peer(k), device_id_type=MESH)
            cp.start()
            sent.append(cp)
        for cp in sent:
            cp.wait_recv()
        for cp in sent:
            cp.wait_send()
        g = jnp.zeros((_SMALL_ROWS, LANES), F32)
        for dev in range(N_DEV):
            k = dev ^ me
            g = g + jnp.where(k == 0, gp_ref[...], land_ref[jnp.maximum(k - 1, 0)])
        g_ref[...] = g
        l0, l1 = w_ref[lb0:lb0 + SUBLANES, :], w_ref[lb0 + SUBLANES:lb0 + 2 * SUBLANES, :]
        mx = jnp.maximum(l0, l1)
        e0, e1 = jnp.exp(l0 - mx), jnp.exp(l1 - mx)
        p1 = e1 / (e0 + e1)
        dl1 = (1.0 - p1) * p1 * g[lb0:lb0 + SUBLANES, :]
        g_ref[lb0:lb0 + SUBLANES, :] = -dl1
        g_ref[lb0 + SUBLANES:lb0 + 2 * SUBLANES, :] = dl1
        delta, m2, v2 = _adamw(w_ref[...], g_ref[...], m_ref[...], v_ref[...])
        d_ref[...] = delta
        m2_ref[...] = m2
        v2_ref[...] = v2

    vmem = pl.BlockSpec(memory_space=pltpu.VMEM)
    return pl.pallas_call(
        body, name="comm_small_allreduce_adamw", in_specs=[vmem] * 4, out_specs=[vmem] * 4,
        out_shape=[jax.ShapeDtypeStruct((_SMALL_ROWS, LANES), F32)] * 4,
        scratch_shapes=[pltpu.VMEM((N_DEV - 1, _SMALL_ROWS, LANES), F32), pltpu.SemaphoreType.DMA((N_DEV - 1,)),
                        pltpu.SemaphoreType.DMA((N_DEV - 1,))],
    )(gpart, w, m, v)


def _qzkv(a):
    return jnp.concatenate([a[..., :1024], a[..., 1280:], a[..., 1024:1280]], axis=-1)


def _qkvz(a):
    return jnp.concatenate([a[..., :1024], a[..., 2048:], a[..., 1024:2048]], axis=-1)


def kernel(x, positions, pre_norm_w, post_norm_w, attn_w_in, attn_b_in, attn_sinks, attn_w_out, attn_b_out, rec_w_in, rec_lb_logits, rec_gnorm_w, rec_w_out, loss_target, m_pre_norm_w, m_post_norm_w, m_attn_w_in, m_attn_b_in, m_attn_sinks, m_attn_w_out, m_attn_b_out, m_rec_w_in, m_rec_lb_logits, m_rec_gnorm_w, m_rec_w_out, v_pre_norm_w, v_post_norm_w, v_attn_w_in, v_attn_b_in, v_attn_sinks, v_attn_w_out, v_attn_b_out, v_rec_w_in, v_rec_lb_logits, v_rec_gnorm_w, v_rec_w_out):
    b_loc, t_len, _ = x.shape
    n = b_loc * t_len
    ga_in, = _all_gather([attn_w_in[0].astype(BF16)])
    wa_in = _qzkv(ga_in.transpose(1, 0, 2).reshape(D_MODEL, ATTN_IN))

    loss_tile, dx, landed, small = _step(
        x.reshape(n, D_MODEL), positions.reshape(n, 1).astype(F32), loss_target.reshape(n, D_MODEL),
        pre_norm_w, post_norm_w, wa_in, _qzkv(attn_b_in), attn_sinks, attn_w_out[0].astype(BF16), attn_b_out,
        rec_w_in[0].astype(BF16), rec_lb_logits, rec_gnorm_w, rec_w_out[0].astype(BF16), b_loc, t_len)
    loss = lax.psum(loss_tile[0, 0], ("x", "y", "c"))

    lift = lambda outs: tuple(a[None] for a in outs)
    (l_r_in, o_r_in), (l_r_out, o_r_out), (l_a_out, o_a_out), (l_a_in, o_a_in) = landed
    r_a_in = lift(_sum_adamw(l_a_in, o_a_in, attn_w_in[0], m_attn_w_in[0], v_attn_w_in[0], "adamw_attn_w_in"))
    r_r_in = lift(_sum_adamw(l_r_in, o_r_in, rec_w_in[0], m_rec_w_in[0], v_rec_w_in[0], "adamw_rec_w_in"))
    r_r_out = lift(_sum_adamw(l_r_out, o_r_out, rec_w_out[0], m_rec_w_out[0], v_rec_w_out[0], "adamw_rec_w_out"))
    r_a_out = lift(_sum_adamw(l_a_out, o_a_out, attn_w_out[0], m_attn_w_out[0], v_attn_w_out[0], "adamw_attn_w_out"))

    gsmall = dict(pre_norm_w=small["pre"], post_norm_w=small["post"], attn_b_in=_qkvz(small["ba_in"]), attn_sinks=small["sinks"],
                  attn_b_out=small["ba_out"], rec_lb_logits=jnp.concatenate([small["lb"], jnp.zeros_like(small["lb"])], axis=0),
                  rec_gnorm_w=small["gnorm"])
    wsmall = dict(pre_norm_w=pre_norm_w, post_norm_w=post_norm_w, attn_b_in=attn_b_in, attn_sinks=attn_sinks,
                  attn_b_out=attn_b_out, rec_lb_logits=rec_lb_logits, rec_gnorm_w=rec_gnorm_w)
    msmall = dict(pre_norm_w=m_pre_norm_w, post_norm_w=m_post_norm_w, attn_b_in=m_attn_b_in, attn_sinks=m_attn_sinks,
                  attn_b_out=m_attn_b_out, rec_lb_logits=m_rec_lb_logits, rec_gnorm_w=m_rec_gnorm_w)
    vsmall = dict(pre_norm_w=v_pre_norm_w, post_norm_w=v_post_norm_w, attn_b_in=v_attn_b_in, attn_sinks=v_attn_sinks,
                  attn_b_out=v_attn_b_out, rec_lb_logits=v_rec_lb_logits, rec_gnorm_w=v_rec_gnorm_w)
    shapes = {k: a.shape for k, a in wsmall.items()}
    packed = _small_allreduce_adamw(_pack_small(gsmall), _pack_small(wsmall), _pack_small(msmall), _pack_small(vsmall))
    sg, sd, sm, sv = [_unpack_small(a, shapes) for a in packed]

    big = {"attn_w_in": r_a_in, "attn_w_out": r_a_out, "rec_w_in": r_r_in, "rec_w_out": r_r_out}
    order = ["pre_norm_w", "post_norm_w", "attn_w_in", "attn_b_in", "attn_sinks", "attn_w_out", "attn_b_out", "rec_w_in",
             "rec_lb_logits", "rec_gnorm_w", "rec_w_out"]
    outs = [loss, dx.reshape(b_loc, t_len, D_MODEL)]
    for idx, small_set in enumerate((sg, sd, sm, sv)):
        outs += [big[nm][idx] if nm in big else small_set[nm] for nm in order]
    return tuple(outs)
```

```python
import numpy as np
import jax
import jax.numpy as jnp
from jax import lax
from jax.experimental import pallas as pl
from jax.experimental.pallas import tpu as pltpu

F32, BF16 = jnp.float32, jnp.bfloat16
MESH = pl.DeviceIdType.MESH
N_DEV = 8

D_MODEL = 1024
N_HEADS, HEAD_DIM, N_KV, GROUP = 16, 64, 2, 8
ATTN_WIDTH, KV_WIDTH = 1024, 128
ATTN_IN = 2 * ATTN_WIDTH + 2 * KV_WIDTH
BLK = 128
ROPE_THETA, ROPE_HALF = 500000.0, 8
REC_HEADS, REC_K = 8, 128
REC_IN = 4 * 1024
CH = 32
NORM_EPS = 1e-6
ADAM_LR, ADAM_B1, ADAM_B2, ADAM_EPS, ADAM_WD, ADAM_STEP = 0.001, 0.9, 0.999, 1e-08, 0.01, 10

LANES, SUBLANES = 128, 8
TM = 512
NT_DIMS = (((1,), (1,)), ((), ()))
TN_DIMS = (((0,), (0,)), ((), ()))
MB = 2 ** 20


def _params(sem=None, vmem_mb=48, **kw):
    return pltpu.CompilerParams(dimension_semantics=sem, vmem_limit_bytes=vmem_mb * MB, **kw)


def _col_chunk(m):
    return 768 if m % 1024 else 1024


def _sigmoid(x):
    return 1.0 / (1.0 + jnp.exp(-x))


def _split3(x):
    hi = x.astype(BF16)
    r1 = x - hi.astype(F32)
    mid = r1.astype(BF16)
    lo = (r1 - mid.astype(F32)).astype(BF16)
    return hi, mid, lo


def _dot3(l_bf, x):
    hi, mid, lo = _split3(x)
    return (jnp.dot(l_bf, hi, preferred_element_type=F32) + jnp.dot(l_bf, mid, preferred_element_type=F32)
            + jnp.dot(l_bf, lo, preferred_element_type=F32))


def _rope_tables(pos_col):
    n = pos_col.shape[0]
    lane = np.arange(LANES) % HEAD_DIM
    inv = np.float32(ROPE_THETA) ** (-(np.arange(ROPE_HALF, dtype=np.float32) * np.float32(2.0) / np.float32(2 * ROPE_HALF)))
    freq = np.where(lane < 2 * ROPE_HALF, inv[lane % ROPE_HALF], 0.0).astype(np.float32)[None, :]
    sign = np.where(lane < ROPE_HALF, -1.0, np.where(lane < 2 * ROPE_HALF, 1.0, 0.0)).astype(np.float32)[None, :]

    def body(p_ref, f_ref, s_ref, c_out, s_out):
        ang = p_ref[...] * f_ref[...]
        c_out[...] = jnp.cos(ang)
        s_out[...] = jnp.sin(ang) * s_ref[...]

    row = pl.BlockSpec((TM, 1), lambda i: (i, 0))
    vec = pl.BlockSpec((1, LANES), lambda i: (0, 0))
    out = pl.BlockSpec((TM, LANES), lambda i: (i, 0))
    return pl.pallas_call(
        body, name="rope_tables", grid=(n // TM,), in_specs=[row, vec, vec], out_specs=[out, out],
        out_shape=[jax.ShapeDtypeStruct((n, LANES), F32)] * 2, compiler_params=_params(("arbitrary",)),
    )(pos_col, jnp.asarray(freq), jnp.asarray(sign))


def _rope_apply(xv, c, s, lm):
    partner = jnp.where(lm < ROPE_HALF, pltpu.roll(xv, LANES - ROPE_HALF, 1), pltpu.roll(xv, ROPE_HALF, 1))
    return xv * c + partner * s


def _rope_bwd(dy, c, s, lm):
    t = dy * s
    partner = jnp.where(lm < ROPE_HALF, pltpu.roll(t, LANES - ROPE_HALF, 1),
                        jnp.where(lm < 2 * ROPE_HALF, pltpu.roll(t, ROPE_HALF, 1), 0.0))
    return dy * c + partner


def _lower_bound(lb_logits):
    def body(l_ref, o_ref):
        l0, l1 = l_ref[0:1, :], l_ref[1:2, :]
        m = jnp.maximum(l0, l1)
        e0, e1 = jnp.exp(l0 - m), jnp.exp(l1 - m)
        o_ref[...] = e1 / (e0 + e1)

    return pl.pallas_call(body, name="lower_bound", out_shape=jax.ShapeDtypeStruct((1, lb_logits.shape[1]), F32))(lb_logits)


def _norm_matmul(x, pw, w, bias, name, shards=()):
    n, m = x.shape[0], w.shape[1]
    cn = _col_chunk(m)
    has_bias = bias is not None
    nsh, steps = len(shards), n // TM

    def body(*refs):
        refs = list(refs)
        x_ref, pw_ref, w_ref = refs[:3]
        b_ref = refs[3] if has_bias else None
        refs = refs[4 if has_bias else 3:]
        sh_in, (p_ref, h_ref), sh_out, sems = refs[:nsh], refs[nsh:nsh + 2], refs[nsh + 2:2 * nsh + 2], refs[2 * nsh + 2:]
        if nsh:
            @pl.when(pl.program_id(0) == 0)
            def _():
                _gather_start(sh_in, sh_out, sems)

        xv = x_ref[...]
        r = lax.rsqrt(jnp.mean(xv * xv, axis=-1, keepdims=True) + NORM_EPS)
        h = ((xv * r) * pw_ref[...]).astype(BF16)
        h_ref[...] = h
        for j in range(0, m, cn):
            acc = jnp.dot(h, w_ref[:, j:j + cn], preferred_element_type=F32)
            if has_bias:
                acc = acc + b_ref[:, j:j + cn]
            p_ref[:, j:j + cn] = acc

        if nsh:
            @pl.when(pl.program_id(0) == steps - 1)
            def _():
                _gather_wait(sh_in, sh_out, sems)

    rows = pl.BlockSpec((TM, D_MODEL), lambda i: (i, 0))
    const = lambda shape: pl.BlockSpec(shape, lambda i: (0, 0))
    hbm = pl.BlockSpec(memory_space=pl.ANY)
    in_specs = [rows, const((1, D_MODEL)), const((D_MODEL, m))] + ([const((1, m))] if has_bias else []) + [hbm] * nsh
    args = (x, pw, w) + ((bias,) if has_bias else ()) + tuple(shards)
    return pl.pallas_call(
        body, name=name, grid=(steps,), in_specs=in_specs,
        out_specs=[pl.BlockSpec((TM, m), lambda i: (i, 0)), rows] + [hbm] * nsh,
        out_shape=[jax.ShapeDtypeStruct((n, m), F32), jax.ShapeDtypeStruct((n, D_MODEL), BF16)] + _gather_shapes(shards),
        scratch_shapes=_gather_sems(nsh) if nsh else [],
        compiler_params=_params(("arbitrary",), 56),
    )(*args)


def _outproj_postnorm(g, w, bias, xres, qw, tgt, name):
    n = g.shape[0]
    has_bias, has_loss = bias is not None, tgt is not None
    steps = n // TM

    def body(*refs):
        refs = list(refs)
        g_ref, w_ref = refs.pop(0), refs.pop(0)
        b_ref = refs.pop(0) if has_bias else None
        x_ref, qw_ref = refs.pop(0), refs.pop(0)
        t_ref = refs.pop(0) if has_loss else None
        y_ref, o_ref = refs.pop(0), refs.pop(0)
        y = jnp.dot(g_ref[...], w_ref[...], preferred_element_type=F32)
        if has_bias:
            y = y + b_ref[...]
        y_ref[...] = y
        r = lax.rsqrt(jnp.mean(y * y, axis=-1, keepdims=True) + NORM_EPS)
        xn = x_ref[...] + (y * r) * qw_ref[...]
        if not has_loss:
            o_ref[...] = xn
        else:
            loss_ref, acc_ref = refs
            i = pl.program_id(0)
            e = xn - t_ref[...]
            o_ref[...] = e * (1.0 / D_MODEL)

            @pl.when(i == 0)
            def _():
                acc_ref[...] = jnp.zeros_like(acc_ref)

            acc_ref[...] += jnp.sum(e * e, axis=0, keepdims=True)

            @pl.when(i == steps - 1)
            def _():
                loss_ref[...] = jnp.full(loss_ref.shape, jnp.sum(acc_ref[...]) * (0.5 / D_MODEL), F32)

    rows = pl.BlockSpec((TM, D_MODEL), lambda i: (i, 0))
    const = lambda shape: pl.BlockSpec(shape, lambda i: (0, 0))
    in_specs = [rows, const((D_MODEL, D_MODEL))] + ([const((1, D_MODEL))] if has_bias else []) + [rows, const((1, D_MODEL))]
    args = [g, w] + ([bias] if has_bias else []) + [xres, qw]
    out_specs = [rows, rows]
    out_shape = [jax.ShapeDtypeStruct((n, D_MODEL), F32)] * 2
    scratch = []
    if has_loss:
        in_specs.append(rows)
        args.append(tgt)
        out_specs.append(const((SUBLANES, LANES)))
        out_shape.append(jax.ShapeDtypeStruct((SUBLANES, LANES), F32))
        scratch = [pltpu.VMEM((1, D_MODEL), F32)]
    return pl.pallas_call(
        body, name=name, grid=(steps,), in_specs=in_specs, out_specs=out_specs, out_shape=out_shape,
        scratch_shapes=scratch, compiler_params=_params(("arbitrary",), 48),
    )(*args)


_QCOL, _ZCOL, _KCOL, _VCOL = 0, 1024, 2048, 2176


def _head_stack(chunks, heads, lt64):
    return jnp.concatenate([jnp.where(lt64 if n % 2 == 0 else ~lt64, chunks[n // 2], 0.0) for n in heads], axis=0)


def _dup_half(x, h, lt64):
    r = pltpu.roll(x, HEAD_DIM, 1)
    return jnp.where(lt64, x, r) if h == 0 else jnp.where(lt64, r, x)


def _pair_chunk(xt, c2):
    a, b = 2 * c2, 2 * c2 + 1
    return jnp.concatenate([xt[:HEAD_DIM, a * BLK:(a + 1) * BLK], xt[HEAD_DIM:, b * BLK:(b + 1) * BLK]], axis=0).T


def _attn_mask_t(i):
    key = lax.broadcasted_iota(jnp.int32, (2 * BLK, BLK), 0)
    qry = lax.broadcasted_iota(jnp.int32, (2 * BLK, BLK), 1)
    valid = (key > qry) & (key <= qry + BLK) & ((key >= BLK) | (i > 0))
    return jnp.tile(jnp.where(valid, 0.0, -1e30), (1, GROUP))


def _attn_probs_t(s, heads, sink_ref, mask):
    s = s + mask
    head = lax.broadcasted_iota(jnp.int32, (1, len(heads) * BLK), 1) >> 7
    sk = jnp.zeros((1, len(heads) * BLK), F32)
    for j, n in enumerate(heads):
        sk = jnp.where(head == j, sink_ref[0, n], sk)
    m = jnp.maximum(jnp.max(s, axis=0, keepdims=True), sk)
    p = jnp.exp(s - m)
    esk = jnp.exp(sk - m)
    inv = 1.0 / (jnp.sum(p, axis=0, keepdims=True) + esk)
    return p * inv, esk * inv


def _attn_fwd(p, ct, st, sinks, b_loc, nb, shards):
    n = p.shape[0]
    nsh = len(shards)

    def body(sink_ref, q_ref, z_ref, kc_ref, kp_ref, vc_ref, vp_ref, cc_ref, sc_ref, cp_ref, sp_ref, *rest):
        sh_in, (o_ref, g_ref), sh_out, sems = rest[:nsh], rest[nsh:nsh + 2], rest[nsh + 2:2 * nsh + 2], rest[2 * nsh + 2:]
        b, i = pl.program_id(0), pl.program_id(1)

        @pl.when((b == 0) & (i == 0))
        def _():
            _gather_start(sh_in, sh_out, sems)

        lane = lax.broadcasted_iota(jnp.int32, (BLK, LANES), 1)
        lm = lane & (HEAD_DIM - 1)
        cc, sc = cc_ref[...], sc_ref[...]
        kcat = jnp.concatenate([_rope_apply(kp_ref[...], cp_ref[...], sp_ref[...], lm),
                                _rope_apply(kc_ref[...], cc, sc, lm)], axis=0)
        vcat = jnp.concatenate([vp_ref[...], vc_ref[...]], axis=0)
        qr = [_rope_apply(q_ref[:, c * LANES:(c + 1) * LANES], cc, sc, lm) * (HEAD_DIM ** -0.5) for c in range(8)]
        valid = _attn_mask_t(i)
        lt64, lt64k = lane < HEAD_DIM, lax.broadcasted_iota(jnp.int32, (2 * BLK, LANES), 1) < HEAD_DIM
        def kv_head(h):
            heads = list(range(h * GROUP, (h + 1) * GROUP))
            kext, vext = _dup_half(kcat, h, lt64k).astype(BF16), _dup_half(vcat, h, lt64k).astype(BF16)
            qst = _head_stack(qr, heads, lt64).astype(BF16)
            s = lax.dot_general(kext, qst, NT_DIMS, preferred_element_type=F32)
            yield
            pn, _ = _attn_probs_t(s, heads, sink_ref, valid)
            ot = lax.dot_general(vext, pn.astype(BF16), TN_DIMS, preferred_element_type=F32)
            yield
            for c2 in range(GROUP // 2):
                oc = _pair_chunk(ot, c2)
                cols = slice((4 * h + c2) * LANES, (4 * h + c2 + 1) * LANES)
                zc = z_ref[:, cols]
                o_ref[:, cols] = oc
                g_ref[:, cols] = (oc * (zc * _sigmoid(zc))).astype(BF16)

        _in_stages([kv_head(h) for h in range(N_KV)])

        @pl.when((b == b_loc - 1) & (i == nb - 1))
        def _():
            _gather_wait(sh_in, sh_out, sems)

    cur = lambda b, i: b * nb + i
    prev = lambda b, i: b * nb + jnp.maximum(i - 1, 0)
    wide = lambda cb: pl.BlockSpec((BLK, ATTN_WIDTH), lambda b, i: (cur(b, i), cb))
    kv = lambda rowf, cb: pl.BlockSpec((BLK, LANES), lambda b, i: (rowf(b, i), cb))
    hbm = pl.BlockSpec(memory_space=pl.ANY)
    in_specs = [pl.BlockSpec(memory_space=pltpu.SMEM), wide(0), wide(1),
                kv(cur, _KCOL // LANES), kv(prev, _KCOL // LANES), kv(cur, _VCOL // LANES), kv(prev, _VCOL // LANES),
                kv(cur, 0), kv(cur, 0), kv(prev, 0), kv(prev, 0)] + [hbm] * nsh
    return pl.pallas_call(
        body, name="attn_fwd", grid=(b_loc, nb), in_specs=in_specs, out_specs=[wide(0), wide(0)] + [hbm] * nsh,
        out_shape=[jax.ShapeDtypeStruct((n, ATTN_WIDTH), F32), jax.ShapeDtypeStruct((n, ATTN_WIDTH), BF16)] + _gather_shapes(shards),
        scratch_shapes=_gather_sems(nsh), compiler_params=_params(("arbitrary", "arbitrary"), 48),
    )(sinks, p, p, p, p, p, p, ct, st, ct, st, *shards)


def _attn_bwd(p, ct, st, sinks, o, dg, b_loc, nb, parts):
    n = p.shape[0]
    nparts = len(parts)

    def body(sink_ref, q_ref, z_ref, kc_ref, kp_ref, vc_ref, vp_ref, cc_ref, sc_ref, cp_ref, sp_ref, o_ref, dg_ref, *rest):
        part_refs, (dp_ref, ds_ref), land_refs = rest[:nparts], rest[nparts:nparts + 2], rest[nparts + 2:2 * nparts + 2]
        dq_s, dz_s, dk_s, dv_s = rest[2 * nparts + 2:2 * nparts + 6]
        sems = rest[2 * nparts + 6:]
        b, i = pl.program_id(0), pl.program_id(1)

        @pl.when((b == 0) & (i == 0))
        def _():
            _scatter_start(part_refs, land_refs, sems)

        @pl.when((b == b_loc - 1) & (i == nb))
        def _():
            _scatter_wait(part_refs, land_refs, sems)

        lane = lax.broadcasted_iota(jnp.int32, (BLK, LANES), 1)
        lm = lane & (HEAD_DIM - 1)

        @pl.when((b == 0) & (i == 0))
        def _():
            ds_ref[...] = jnp.zeros_like(ds_ref)

        @pl.when(i < nb)
        def _compute():
            cc, sc = cc_ref[...], sc_ref[...]
            kcat = jnp.concatenate([_rope_apply(kp_ref[...], cp_ref[...], sp_ref[...], lm),
                                    _rope_apply(kc_ref[...], cc, sc, lm)], axis=0)
            vcat = jnp.concatenate([vp_ref[...], vc_ref[...]], axis=0)
            qr = [_rope_apply(q_ref[:, c * LANES:(c + 1) * LANES], cc, sc, lm) * (HEAD_DIM ** -0.5) for c in range(8)]
            valid = _attn_mask_t(i)
            lt64, lt64k = lane < HEAD_DIM, lax.broadcasted_iota(jnp.int32, (2 * BLK, LANES), 1) < HEAD_DIM
            do_chunks, doo_chunks, dz_chunks = [], [], []
            for c in range(8):
                cols = slice(c * LANES, (c + 1) * LANES)
                zc, oc, dgc = z_ref[:, cols], o_ref[:, cols], dg_ref[:, cols]
                sg = _sigmoid(zc)
                do_chunks.append(dgc * (zc * sg))
                dz_chunks.append(dgc * oc * (sg * (1.0 + zc * (1.0 - sg))))
                doo_chunks.append(do_chunks[c] * oc)
            dq_chunks = [None] * 8
            dk_h, dv_h, ds_parts = [None] * N_KV, [None] * N_KV, [None] * N_KV
            tile_lane = lax.broadcasted_iota(jnp.int32, (SUBLANES, LANES), 1)
            tile_row = lax.broadcasted_iota(jnp.int32, (SUBLANES, LANES), 0)
            ones8 = jnp.ones((SUBLANES, LANES), BF16)

            def kv_head(h):
                heads = list(range(h * GROUP, (h + 1) * GROUP))
                kext = _dup_half(kcat, h, lt64k)
                kext_bf, kext_t = kext.astype(BF16), kext.T.astype(BF16)
                vext = _dup_half(vcat, h, lt64k).astype(BF16)
                qst = _head_stack(qr, heads, lt64).astype(BF16)
                do_bf = _head_stack(do_chunks, heads, lt64).astype(BF16)
                s = lax.dot_general(kext_bf, qst, NT_DIMS, preferred_element_type=F32)
                dpt = lax.dot_general(vext, do_bf, NT_DIMS, preferred_element_type=F32)
                delta = sum(lax.dot_general(ones8, part, NT_DIMS, preferred_element_type=F32)
                            for part in _split3(_head_stack(doo_chunks, heads, lt64)))[0:1, :]
                yield
                pn, psink = _attn_probs_t(s, heads, sink_ref, valid)
                dst = (pn * (dpt - delta)).astype(BF16)
                dqt = jnp.dot(kext_t, dst, preferred_element_type=F32) * (HEAD_DIM ** -0.5)
                dk_ext = jnp.dot(dst, qst, preferred_element_type=F32)
                dv_ext = jnp.dot(pn.astype(BF16), do_bf, preferred_element_type=F32)
                yield
                sink_term = psink * delta
                ds_acc = jnp.zeros((SUBLANES, LANES), F32)
                for j, n in enumerate(heads):
                    val = -jnp.sum(sink_term[:, j * BLK:(j + 1) * BLK])
                    ds_acc = ds_acc + jnp.where((tile_lane == n) & (tile_row == 0), val, 0.0)
                ds_parts[h] = ds_acc
                for c2 in range(GROUP // 2):
                    dq_chunks[4 * h + c2] = _rope_bwd(_pair_chunk(dqt, c2), cc, sc, lm)
                dk_h[h] = dk_ext + pltpu.roll(dk_ext, HEAD_DIM, 1)
                dv_h[h] = dv_ext + pltpu.roll(dv_ext, HEAD_DIM, 1)

            for h in range(N_KV):
                for _ in kv_head(h):
                    pass
            ds_ref[...] += ds_parts[0] + ds_parts[1]
            dk_full = jnp.where(lt64k, dk_h[0], dk_h[1])
            dv_full = jnp.where(lt64k, dv_h[0], dv_h[1])

            @pl.when(i >= 1)
            def _emit():
                dp_ref[:, _QCOL:_QCOL + ATTN_WIDTH] = dq_s[...]
                dp_ref[:, _ZCOL:_ZCOL + ATTN_WIDTH] = dz_s[...]
                dp_ref[:, _KCOL:_KCOL + KV_WIDTH] = _rope_bwd(dk_s[...] + dk_full[:BLK], cp_ref[...], sp_ref[...], lm)
                dp_ref[:, _VCOL:_VCOL + KV_WIDTH] = dv_s[...] + dv_full[:BLK]

            for c in range(8):
                dq_s[:, c * LANES:(c + 1) * LANES] = dq_chunks[c]
                dz_s[:, c * LANES:(c + 1) * LANES] = dz_chunks[c]
            dk_s[...] = dk_full[BLK:]
            dv_s[...] = dv_full[BLK:]

        @pl.when(i == nb)
        def _final():
            dp_ref[:, _QCOL:_QCOL + ATTN_WIDTH] = dq_s[...]
            dp_ref[:, _ZCOL:_ZCOL + ATTN_WIDTH] = dz_s[...]
            dp_ref[:, _KCOL:_KCOL + KV_WIDTH] = _rope_bwd(dk_s[...], cc_ref[...], sc_ref[...], lm)
            dp_ref[:, _VCOL:_VCOL + KV_WIDTH] = dv_s[...]

    cur = lambda b, i: b * nb + jnp.minimum(i, nb - 1)
    prev = lambda b, i: b * nb + jnp.maximum(jnp.minimum(i, nb - 1) - 1, 0)
    emit = lambda b, i: b * nb + jnp.maximum(i - 1, 0)
    hbm = pl.BlockSpec(memory_space=pl.ANY)
    wide = lambda cb: pl.BlockSpec((BLK, ATTN_WIDTH), lambda b, i: (cur(b, i), cb))
    kv = lambda rowf, cb: pl.BlockSpec((BLK, LANES), lambda b, i: (rowf(b, i), cb))
    in_specs = [pl.BlockSpec(memory_space=pltpu.SMEM), wide(0), wide(1),
                kv(cur, _KCOL // LANES), kv(prev, _KCOL // LANES), kv(cur, _VCOL // LANES), kv(prev, _VCOL // LANES),
                kv(cur, 0), kv(cur, 0), kv(prev, 0), kv(prev, 0), wide(0), wide(0)] + [hbm] * nparts
    out_specs = [pl.BlockSpec((BLK, ATTN_IN), lambda b, i: (emit(b, i), 0)),
                 pl.BlockSpec((SUBLANES, LANES), lambda b, i: (0, 0))] + [hbm] * nparts
    return pl.pallas_call(
        body, name="attn_bwd", grid=(b_loc, nb + 1), in_specs=in_specs, out_specs=out_specs,
        out_shape=[jax.ShapeDtypeStruct((n, ATTN_IN), F32), jax.ShapeDtypeStruct((SUBLANES, LANES), F32)] + _scatter_lands(parts),
        scratch_shapes=[pltpu.VMEM((BLK, ATTN_WIDTH), F32), pltpu.VMEM((BLK, ATTN_WIDTH), F32),
                        pltpu.VMEM((BLK, KV_WIDTH), F32), pltpu.VMEM((BLK, KV_WIDTH), F32)] + _scatter_sems(nparts),
        compiler_params=_params(("arbitrary", "arbitrary"), 48),
    )(sinks, p, p, p, p, p, p, ct, st, ct, st, o, dg, *parts)


_CUM_ROWS = 256
HALF = CH // 2
_ROW0 = [SUBLANES * (s // SUBLANES) for s in range(CH)]
_ROW1 = [HALF * (s // HALF + 1) for s in range(CH)]
_ROWS_OF = [_ROW1[s] - _ROW0[s] for s in range(CH)]
_OFF_OF = [sum(_ROWS_OF[:s]) for s in range(CH)]


def _tri(lower):
    r = lax.broadcasted_iota(jnp.int32, (_CUM_ROWS, _CUM_ROWS), 0)
    c = lax.broadcasted_iota(jnp.int32, (_CUM_ROWS, _CUM_ROWS), 1)
    same = (r ^ c) < CH
    return (same & ((c <= r) if lower else (c >= r))).astype(BF16)


def _gates(qp, fp, lb):
    e = jnp.exp(-jnp.abs(fp))
    log_sig = jnp.minimum(fp, 0.0) - jnp.log(1.0 + e)
    a = jnp.log(lb)
    c = jnp.log(1.0 - lb) + log_sig
    g = jnp.maximum(a, c) + jnp.log(1.0 + jnp.exp(-jnp.abs(a - c)))
    sig_neg = jnp.where(fp >= 0, e, 1.0) / (1.0 + e)
    return qp * _sigmoid(qp), g, (1.0 - lb) * sig_neg, sig_neg


def _pair_rows(bc, s):
    return jnp.exp(jnp.minimum(bc[_ROW0[s]:_ROW1[s], :] - bc[s:s + 1, :], 0.0))


def _cross_half(q, k, bc):
    r = bc[HALF - 1:HALF, :]
    e1, e0 = jnp.exp(bc[HALF:, :] - r), jnp.exp(r - bc[:HALF, :])
    return q[HALF:, :] * e1, e1, k[:HALF, :] * e0, e0


HP = 8
REC_TB = 256
_HW = HP * REC_K


def _hgrn_specs(tb, nt, reverse):
    tmap = (lambda t: nt - 1 - t) if reverse else (lambda t: t)
    groups = REC_HEADS // HP
    blk = lambda cb: pl.BlockSpec((tb, _HW), lambda h, b, t: (b * nt + tmap(t), cb * groups + h))
    head = pl.BlockSpec((tb, _HW), lambda h, b, t: (b * nt + tmap(t), h))
    lbs = pl.BlockSpec((1, _HW), lambda h, b, t: (0, h))
    gws = pl.BlockSpec((1, REC_K), lambda h, b, t: (0, 0))
    hist = pl.BlockSpec((HP, 1, tb // CH, REC_K, REC_K), lambda h, b, t: (h, b, tmap(t), 0, 0))
    return blk, head, lbs, gws, hist


def _chunk_rows(c, first=0, size=CH):
    start = c * CH + first
    return pl.ds(start if isinstance(start, int) else pl.multiple_of(start, CH if first % CH == 0 else SUBLANES), size)


def _in_stages(heads):
    live = list(heads)
    while live:
        live = [g for g in live if next(g, live) is not live]


def _cumsum_chunks(tri, x, out_ref, tb):
    for r in range(0, tb, _CUM_ROWS):
        out_ref[r:r + _CUM_ROWS, :] = _dot3(tri, x[r:r + _CUM_ROWS, :])


def _hgrn_fwd(p, lb, gw, b_loc, t_len):
    n = p.shape[0]
    tb = min(REC_TB, t_len)
    nt, nck = t_len // tb, tb // CH

    def body(qp_ref, fp_ref, i_ref, z_ref, lb_ref, gw_ref, oraw_ref, g_ref, sh_ref, q_s, k_s, b_s, o_s, st_ref,
             car_o, car_a, car_s, car_st):
        @pl.when(pl.program_id(2) == 0)
        def _():
            st_ref[...] = jnp.zeros_like(st_ref)

        qv, g, kk, _ = _gates(qp_ref[...], fp_ref[...], lb_ref[...])
        q_s[...] = qv
        k_s[...] = kk
        _cumsum_chunks(_tri(True), g, b_s, tb)
        ones = jnp.ones((REC_K, REC_K), BF16)
        sub = lax.broadcasted_iota(jnp.int32, (SUBLANES, REC_K), 0)

        rows_of = _chunk_rows

        def issue(c, hp):
            rs, cs = rows_of(c), slice(hp * REC_K, (hp + 1) * REC_K)
            q, k, bc, v = q_s[rs, cs], k_s[rs, cs], b_s[rs, cs], i_ref[rs, cs]
            st = st_ref[hp]
            sh_ref[hp, 0, c] = st
            o = lax.dot_general((q * jnp.exp(bc)).astype(BF16), st.astype(BF16), NT_DIMS, preferred_element_type=F32)
            w = jnp.concatenate([q[_ROW0[s]:_ROW1[s], :] * _pair_rows(bc, s) * k[s:s + 1, :] for s in range(CH)], axis=0)
            a = jnp.dot(w.astype(BF16), ones, preferred_element_type=F32)
            qe1, _, ke0, _ = _cross_half(q, k, bc)
            s10 = lax.dot_general(qe1.astype(BF16), ke0.astype(BF16), NT_DIMS, preferred_element_type=F32)
            kd = k * jnp.exp(bc[CH - 1:CH, :] - bc)
            st_new = lax.dot_general(v.astype(BF16), kd.astype(BF16), TN_DIMS, preferred_element_type=F32)
            return o, a, s10, st_new

        def advance_state(c, hp, st_new):
            bl = b_s[_chunk_rows(c, CH - SUBLANES, SUBLANES), hp * REC_K:(hp + 1) * REC_K][SUBLANES - 1:, :]
            st_ref[hp] = st_ref[hp] * jnp.exp(bl) + st_new

        def cross(c, hp, s10):
            v0 = i_ref[_chunk_rows(c, 0, HALF), hp * REC_K:(hp + 1) * REC_K]
            return jnp.dot(s10.astype(BF16), v0.astype(BF16), preferred_element_type=F32)

        def finish(c, hp, o, a, o_cross):
            rs, cs = rows_of(c), slice(hp * REC_K, (hp + 1) * REC_K)
            v = i_ref[rs, cs]
            acc = [jnp.zeros((SUBLANES, REC_K), F32) for _ in range(CH // SUBLANES)]
            for s in range(CH):
                j = s // SUBLANES
                vs = v[s:s + 1, :]
                for jj in range(j, _ROW1[s] // SUBLANES):
                    blk = a[_OFF_OF[s] + (jj - j) * SUBLANES:_OFF_OF[s] + (jj - j + 1) * SUBLANES, :]
                    if jj == j:
                        blk = jnp.where(sub >= s - j * SUBLANES, blk, 0.0)
                    acc[jj] = acc[jj] + blk * vs
            o_s[rs, cs] = o + jnp.concatenate(acc, axis=0) + jnp.concatenate([jnp.zeros((HALF, REC_K), F32), o_cross], axis=0)

        def park(slot, results):
            for hp, (o, a, s10, st_new) in enumerate(results):
                car_o[slot, hp], car_a[slot, hp], car_s[slot, hp], car_st[slot, hp] = o, a, s10, st_new

        def retire(c, slot):
            for hp in range(HP):
                advance_state(c, hp, car_st[slot, hp])
            yield
            crosses = [cross(c, hp, car_s[slot, hp]) for hp in range(HP)]
            for hp in range(HP):
                finish(c, hp, car_o[slot, hp], car_a[slot, hp], crosses[hp])

        def step(c, slot):
            closing = retire(c - 1, slot)
            next(closing)
            park(1 - slot, [issue(c, hp) for hp in range(HP)])
            next(closing, None)

        def trip(j, carry):
            step(2 * j + 1, 0)
            step(2 * j + 2, 1)
            return carry

        assert nck % 2 == 0
        park(0, [issue(0, hp) for hp in range(HP)])
        lax.fori_loop(0, nck // 2 - 1, trip, 0)
        step(nck - 1, 0)
        for _ in retire(nck - 1, 1):
            pass
        oraw_ref[...] = o_s[...]
        for hp in range(HP):
            cs = slice(hp * REC_K, (hp + 1) * REC_K)
            o, zc = o_s[:, cs], z_ref[:, cs]
            on = (o * lax.rsqrt(jnp.mean(o * o, axis=-1, keepdims=True) + NORM_EPS)) * gw_ref[...]
            g_ref[:, cs] = (on * (zc * _sigmoid(zc))).astype(BF16)

    blk, head, lbs, gws, hist = _hgrn_specs(tb, nt, False)
    return pl.pallas_call(
        body, name="hgrn_fwd", grid=(REC_HEADS // HP, b_loc, nt),
        in_specs=[blk(0), blk(1), blk(2), blk(3), lbs, gws], out_specs=[head, head, hist],
        out_shape=[jax.ShapeDtypeStruct((n, 1024), F32), jax.ShapeDtypeStruct((n, 1024), BF16),
                   jax.ShapeDtypeStruct((REC_HEADS, b_loc, t_len // CH, REC_K, REC_K), F32)],
        scratch_shapes=[pltpu.VMEM((tb, _HW), F32)] * 4 + [pltpu.VMEM((HP, REC_K, REC_K), F32)] + [
            pltpu.VMEM((2, HP, CH, REC_K), F32), pltpu.VMEM((2, HP, sum(_ROWS_OF), REC_K), F32),
            pltpu.VMEM((2, HP, HALF, HALF), F32), pltpu.VMEM((2, HP, REC_K, REC_K), F32)],
        compiler_params=_params(("arbitrary", "arbitrary", "arbitrary"), 48),
    )(p, p, p, p, lb, gw)


def _hgrn_bwd(p, lb, gw, oraw, sh, dg, b_loc, t_len):
    n = p.shape[0]
    tb = min(REC_TB, t_len)
    nt, nck = t_len // tb, tb // CH

    def body(qp_ref, fp_ref, i_ref, z_ref, lb_ref, gw_ref, oraw_ref, dg_ref, sh_ref,
             dq_ref, df_ref, di_ref, dz_ref, dlb_ref, dgw_ref,
             q_s, k_s, b_s, do_s, dqv_s, dk_s, db_s, dst_ref, car_r, car_dst, car_dec, car_a, car_da, car_x):
        b, t = pl.program_id(1), pl.program_id(2)

        @pl.when(t == 0)
        def _():
            dst_ref[...] = jnp.zeros_like(dst_ref)

        @pl.when((b == 0) & (t == 0))
        def _():
            dlb_ref[...] = jnp.zeros_like(dlb_ref)
            dgw_ref[...] = jnp.zeros_like(dgw_ref)

        lbv, qp, fp = lb_ref[...], qp_ref[...], fp_ref[...]
        qv, g, kk, sig_neg = _gates(qp, fp, lbv)
        q_s[...] = qv
        k_s[...] = kk
        _cumsum_chunks(_tri(True), g, b_s, tb)
        gwv = gw_ref[...]
        for hp in range(HP):
            cs = slice(hp * REC_K, (hp + 1) * REC_K)
            o, zc, dgv = oraw_ref[:, cs], z_ref[:, cs], dg_ref[:, cs]
            rn = lax.rsqrt(jnp.mean(o * o, axis=-1, keepdims=True) + NORM_EPS)
            on = o * rn
            sgz = _sigmoid(zc)
            dz_ref[:, cs] = (dgv * (on * gwv) * (sgz * (1.0 + zc * (1.0 - sgz)))).astype(BF16)
            dpre = dgv * (zc * sgz)
            dgw_ref[hp] += jnp.sum(dpre * on, axis=0, keepdims=True)
            don = dpre * gwv
            do_s[:, cs] = rn * (don - on * jnp.mean(don * on, axis=-1, keepdims=True))

        ones = jnp.ones((REC_K, REC_K), BF16)
        sub = lax.broadcasted_iota(jnp.int32, (SUBLANES, REC_K), 0)
        rowid = lax.broadcasted_iota(jnp.int32, (CH, REC_K), 0)
        ngrp = CH // SUBLANES
        piece_row = lax.broadcasted_iota(jnp.int32, (1, sum(_ROWS_OF)), 1)
        key_of = jnp.zeros((1, sum(_ROWS_OF)), jnp.int32)
        for s in range(1, CH):
            key_of = jnp.where(piece_row >= _OFF_OF[s], s, key_of)
        pick = (key_of == lax.broadcasted_iota(jnp.int32, (CH, sum(_ROWS_OF)), 0)).astype(BF16)

        def operands(c, hp):
            rs, cs = _chunk_rows(c), slice(hp * REC_K, (hp + 1) * REC_K)
            return rs, cs, q_s[rs, cs], k_s[rs, cs], b_s[rs, cs], i_ref[rs, cs], do_s[rs, cs]

        def issue(c, hp, slot):
            _, _, q, k, bc, v, do = operands(c, hp)
            st, dst = sh_ref[hp, 0, c], dst_ref[hp]
            qe, kd = q * jnp.exp(bc), k * jnp.exp(bc[CH - 1:CH, :] - bc)
            do_bf, dst_bf = do.astype(BF16), dst.astype(BF16)
            car_r[slot, hp, 0:CH] = jnp.dot(do_bf, st.astype(BF16), preferred_element_type=F32)
            car_r[slot, hp, CH:2 * CH] = jnp.dot(v.astype(BF16), dst_bf, preferred_element_type=F32)
            car_r[slot, hp, 2 * CH:3 * CH] = lax.dot_general(kd.astype(BF16), dst_bf, NT_DIMS, preferred_element_type=F32)
            car_dst[slot, hp] = lax.dot_general(do_bf, qe.astype(BF16), TN_DIMS, preferred_element_type=F32)
            dec = jnp.concatenate([_pair_rows(bc, s) for s in range(CH)], axis=0)
            qk = jnp.concatenate([q[_ROW0[s]:_ROW1[s], :] * k[s:s + 1, :] for s in range(CH)], axis=0)
            x = jnp.concatenate([do[_ROW0[s]:_ROW1[s], :] * v[s:s + 1, :] for s in range(CH)], axis=0)
            car_dec[slot, hp] = dec
            car_a[slot, hp] = jnp.dot((qk * dec).astype(BF16), ones, preferred_element_type=F32)
            car_da[slot, hp] = jnp.dot(x.astype(BF16), ones, preferred_element_type=F32)
            qe1, _, ke0, _ = _cross_half(q, k, bc)
            qe1_bf, ke0_bf = qe1.astype(BF16), ke0.astype(BF16)
            do1_bf, v0_bf = do[HALF:, :].astype(BF16), v[:HALF, :].astype(BF16)
            car_x[slot, hp, 0:HALF] = lax.dot_general(ke0_bf, qe1_bf, NT_DIMS, preferred_element_type=F32)
            car_x[slot, hp, HALF:2 * HALF] = lax.dot_general(do1_bf, v0_bf, NT_DIMS, preferred_element_type=F32)
            car_x[slot, hp, 2 * HALF:3 * HALF] = lax.dot_general(v0_bf, do1_bf, NT_DIMS, preferred_element_type=F32)

        def advance_state(c, hp, slot):
            ebl = jnp.exp(b_s[_chunk_rows(c, CH - SUBLANES, SUBLANES), hp * REC_K:(hp + 1) * REC_K][SUBLANES - 1:, :])
            st, dst = sh_ref[hp, 0, c], dst_ref[hp]
            dst_ref[hp] = dst * ebl + car_dst[slot, hp]
            return ebl * jnp.sum(st * dst, axis=0, keepdims=True)

        def cross(c, hp, slot):
            _, _, q, k, bc, v, do = operands(c, hp)
            qe1, _, ke0, _ = _cross_half(q, k, bc)
            xs = car_x[slot, hp]
            dqe1 = jnp.dot(xs[HALF:2 * HALF].astype(BF16), ke0.astype(BF16), preferred_element_type=F32)
            dke0 = jnp.dot(xs[2 * HALF:].astype(BF16), qe1.astype(BF16), preferred_element_type=F32)
            dv1 = jnp.dot(xs[:HALF].astype(BF16), do[HALF:, :].astype(BF16), preferred_element_type=F32)
            return dqe1, dke0, dv1

        def retire(c, slot):
            dbl_state = [advance_state(c, hp, slot) for hp in range(HP)]
            yield
            crossed = [cross(c, hp, slot) for hp in range(HP)]
            for hp in range(HP):
                finish(c, hp, slot, dbl_state[hp], *crossed[hp])

        def step(c, slot):
            closing = retire(c + 1, slot)
            next(closing)
            for hp in range(HP):
                issue(c, hp, 1 - slot)
            next(closing, None)

        def trip(j, carry):
            step(nck - 2 - 2 * j, 0)
            step(nck - 3 - 2 * j, 1)
            return carry

        def finish(c, hp, slot, dbl_state, dqe1, dke0, dv1):
            rs, cs, q, k, bc, v, do = operands(c, hp)
            eb, ekd = jnp.exp(bc), jnp.exp(bc[CH - 1:CH, :] - bc)
            qe, kd = q * eb, k * ekd
            qe1, e1, ke0, e0 = _cross_half(q, k, bc)
            dqe, dkd, dv = car_r[slot, hp, 0:CH], car_r[slot, hp, CH:2 * CH], car_r[slot, hp, 2 * CH:3 * CH]
            a, da, decs = car_a[slot, hp], car_da[slot, hp], car_dec[slot, hp]
            dec = [decs[_OFF_OF[s]:_OFF_OF[s] + _ROWS_OF[s], :] for s in range(CH)]
            dbl = jnp.sum(dkd * kd, axis=0, keepdims=True) + dbl_state
            dq_acc = [jnp.zeros((SUBLANES, REC_K), F32) for _ in range(ngrp)]
            uk, uv = [], []
            for s in range(CH):
                j = s // SUBLANES
                r0 = j * SUBLANES
                ks = k[s:s + 1, :]
                for jj in range(j, _ROW1[s] // SUBLANES):
                    lo, hi = _OFF_OF[s] + (jj - j) * SUBLANES, _OFF_OF[s] + (jj - j + 1) * SUBLANES
                    a_blk, da_blk = a[lo:hi, :], da[lo:hi, :]
                    if jj == j:
                        keep = sub >= s - r0
                        a_blk, da_blk = jnp.where(keep, a_blk, 0.0), jnp.where(keep, da_blk, 0.0)
                    rows = slice(jj * SUBLANES, (jj + 1) * SUBLANES)
                    tt = da_blk * dec[s][(jj - j) * SUBLANES:(jj - j + 1) * SUBLANES, :]
                    dq_acc[jj] = dq_acc[jj] + tt * ks
                    uk.append(tt * q[rows, :])
                    uv.append(a_blk * do[rows, :])
            dk_in = jnp.dot(pick, jnp.concatenate(uk, axis=0).astype(BF16), preferred_element_type=F32)
            dv_in = jnp.dot(pick, jnp.concatenate(uv, axis=0).astype(BF16), preferred_element_type=F32)
            zero_half = jnp.zeros((HALF, REC_K), F32)
            dq_x = jnp.concatenate([zero_half, dqe1 * e1], axis=0)
            dk_x = jnp.concatenate([dke0 * e0, zero_half], axis=0)
            dv_x = jnp.concatenate([dv1, zero_half], axis=0)
            db_x = jnp.concatenate([-(dke0 * ke0), dqe1 * qe1], axis=0)
            dq_in = jnp.concatenate(dq_acc, axis=0)
            dqv_s[rs, cs] = dqe * eb + dq_in + dq_x
            dk_s[rs, cs] = dkd * ekd + dk_in + dk_x
            di_ref[rs, cs] = (dv + dv_in + dv_x).astype(BF16)
            db = dqe * qe - dkd * kd + q * dq_in - k * dk_in + db_x
            db_s[rs, cs] = db + jnp.where(rowid == CH - 1, dbl, 0.0)

        assert nck % 2 == 0
        for hp in range(HP):
            issue(nck - 1, hp, 0)
        lax.fori_loop(0, nck // 2 - 1, trip, 0)
        step(0, 0)
        for _ in retire(0, 1):
            pass
        up = _tri(False)
        sgq = _sigmoid(qp)
        dq_ref[...] = (dqv_s[...] * (sgq * (1.0 + qp * (1.0 - sgq)))).astype(BF16)
        dlb_acc = jnp.zeros((1, _HW), F32)
        for r in range(0, tb, _CUM_ROWS):
            rows = slice(r, r + _CUM_ROWS)
            dgl = _dot3(up, db_s[rows, :])
            dfg = dgl * jnp.exp(-g[rows, :]) - dk_s[rows, :]
            sn = sig_neg[rows, :]
            df_ref[rows, :] = (dfg * (1.0 - lbv) * (1.0 - sn) * sn).astype(BF16)
            dlb_acc = dlb_acc + jnp.sum(dfg * sn, axis=0, keepdims=True)
        dlb_ref[...] += dlb_acc

    blk, head, lbs, gws, hist = _hgrn_specs(tb, nt, True)
    out_specs = [head, head, head, head, lbs, pl.BlockSpec((HP, 1, REC_K), lambda h, b, t: (h, 0, 0))]
    out_shape = [jax.ShapeDtypeStruct((n, 1024), BF16)] * 4 + [
        jax.ShapeDtypeStruct((1, 1024), F32), jax.ShapeDtypeStruct((REC_HEADS, 1, REC_K), F32)]
    return pl.pallas_call(
        body, name="hgrn_bwd", grid=(REC_HEADS // HP, b_loc, nt),
        in_specs=[blk(0), blk(1), blk(2), blk(3), lbs, gws, head, head, hist], out_specs=out_specs, out_shape=out_shape,
        scratch_shapes=[pltpu.VMEM((tb, _HW), F32)] * 7 + [pltpu.VMEM((HP, REC_K, REC_K), F32)] + [
            pltpu.VMEM((2, HP, 3 * CH, REC_K), F32), pltpu.VMEM((2, HP, REC_K, REC_K), F32)] + [
            pltpu.VMEM((2, HP, sum(_ROWS_OF), REC_K), F32)] * 3 + [pltpu.VMEM((2, HP, 3 * HALF, HALF), F32)],
        compiler_params=_params(("arbitrary", "arbitrary", "arbitrary"), 56),
    )(p, p, p, p, lb, gw, oraw, dg, sh)


def _postnorm_bwd_nt(dxo, y, qw, w, has_bias, name):
    n = dxo.shape[0]

    def body(dx_ref, y_ref, qw_ref, w_ref, dg_ref, dy_ref, dqw_ref, db_ref):
        @pl.when(pl.program_id(0) == 0)
        def _():
            dqw_ref[...] = jnp.zeros_like(dqw_ref)
            db_ref[...] = jnp.zeros_like(db_ref)

        yv, dxv = y_ref[...], dx_ref[...]
        r = lax.rsqrt(jnp.mean(yv * yv, axis=-1, keepdims=True) + NORM_EPS)
        u = yv * r
        du = dxv * qw_ref[...]
        dy = r * (du - u * jnp.mean(du * u, axis=-1, keepdims=True))
        dqw_ref[...] += jnp.sum(dxv * u, axis=0, keepdims=True)
        if has_bias:
            db_ref[...] += jnp.sum(dy, axis=0, keepdims=True)
        dyb = dy.astype(BF16)
        dy_ref[...] = dyb
        dg_ref[...] = lax.dot_general(dyb, w_ref[...], NT_DIMS, preferred_element_type=F32)

    rows = pl.BlockSpec((TM, D_MODEL), lambda i: (i, 0))
    const = lambda shape: pl.BlockSpec(shape, lambda i: (0, 0))
    return pl.pallas_call(
        body, name=name, grid=(n // TM,), in_specs=[rows, rows, const((1, D_MODEL)), const((D_MODEL, D_MODEL))],
        out_specs=[rows, rows, const((1, D_MODEL)), const((1, D_MODEL))],
        out_shape=[jax.ShapeDtypeStruct((n, D_MODEL), F32), jax.ShapeDtypeStruct((n, D_MODEL), BF16),
                   jax.ShapeDtypeStruct((1, D_MODEL), F32), jax.ShapeDtypeStruct((1, D_MODEL), F32)],
        compiler_params=_params(("arbitrary",), 48),
    )(dxo, y, qw, w)


def _nt_prenorm_bwd(dps, w, x, pw, dxo, has_bias, name, parts=()):
    n = x.shape[0]
    widths = [d.shape[1] for d in dps]
    m = sum(widths)
    npieces, nparts, steps = len(dps), len(parts), n // TM

    def body(*refs):
        dp_refs = refs[:npieces]
        w_ref, x_ref, pw_ref, dxo_ref = refs[npieces:npieces + 4]
        part_refs = refs[npieces + 4:npieces + 4 + nparts]
        dx_ref, dpw_ref, db_ref = refs[npieces + 4 + nparts:npieces + 7 + nparts]
        land_refs = refs[npieces + 7 + nparts:npieces + 7 + 2 * nparts]
        sems = refs[npieces + 7 + 2 * nparts:]

        @pl.when(pl.program_id(0) == 0)
        def _():
            dpw_ref[...] = jnp.zeros_like(dpw_ref)
            db_ref[...] = jnp.zeros_like(db_ref)
            if nparts:
                _scatter_start(part_refs, land_refs, sems)

        dh = jnp.zeros((TM, D_MODEL), F32)
        off = 0
        for dp_ref, wd in zip(dp_refs, widths):
            cn = _col_chunk(wd)
            for j in range(0, wd, cn):
                dpc = dp_ref[:, j:j + cn]
                if has_bias:
                    db_ref[:, off + j:off + j + cn] += jnp.sum(dpc, axis=0, keepdims=True)
                dh = dh + lax.dot_general(dpc.astype(BF16), w_ref[:, off + j:off + j + cn], NT_DIMS, preferred_element_type=F32)
            off += wd
        xv = x_ref[...]
        r = lax.rsqrt(jnp.mean(xv * xv, axis=-1, keepdims=True) + NORM_EPS)
        xn = xv * r
        dpw_ref[...] += jnp.sum(dh * xn, axis=0, keepdims=True)
        dxn = dh * pw_ref[...]
        dx_ref[...] = dxo_ref[...] + r * (dxn - xn * jnp.mean(dxn * xn, axis=-1, keepdims=True))

        if nparts:
            @pl.when(pl.program_id(0) == steps - 1)
            def _():
                _scatter_wait(part_refs, land_refs, sems)

    rows = pl.BlockSpec((TM, D_MODEL), lambda i: (i, 0))
    const = lambda shape: pl.BlockSpec(shape, lambda i: (0, 0))
    hbm = pl.BlockSpec(memory_space=pl.ANY)
    in_specs = ([pl.BlockSpec((TM, wd), lambda i: (i, 0)) for wd in widths] + [const((D_MODEL, m)), rows, const((1, D_MODEL)), rows]
                + [hbm] * nparts)
    return pl.pallas_call(
        body, name=name, grid=(steps,), in_specs=in_specs,
        out_specs=[rows, const((1, D_MODEL)), const((1, m))] + [hbm] * nparts,
        out_shape=[jax.ShapeDtypeStruct((n, D_MODEL), F32), jax.ShapeDtypeStruct((1, D_MODEL), F32),
                   jax.ShapeDtypeStruct((1, m), F32)] + _scatter_lands(parts),
        scratch_shapes=_scatter_sems(nparts) if nparts else [],
        compiler_params=_params(("arbitrary",), 56),
    )(*dps, w, x, pw, dxo, *parts)


def _matmul_tn(a, b, name):
    n, k = a.shape
    m = b.shape[1]
    tk, tm, tn = k, _col_chunk(m), 1024 if n % 1024 == 0 else n

    def body(a_ref, b_ref, o_ref):
        @pl.when(pl.program_id(2) == 0)
        def _():
            o_ref[...] = jnp.zeros_like(o_ref)

        o_ref[...] += lax.dot_general(a_ref[...], b_ref[...].astype(BF16), TN_DIMS, preferred_element_type=F32)

    return pl.pallas_call(
        body, name=name, grid=(k // tk, m // tm, n // tn),
        in_specs=[pl.BlockSpec((tn, tk), lambda i, j, l: (l, i)), pl.BlockSpec((tn, tm), lambda i, j, l: (l, j))],
        out_specs=pl.BlockSpec((tk, tm), lambda i, j, l: (i, j)),
        out_shape=jax.ShapeDtypeStruct((k, m), F32),
        compiler_params=_params(("arbitrary", "arbitrary", "arbitrary"), 48),
    )(a, b)


def _by_owner_cols(dw):
    k, m = dw.shape
    return dw.reshape(k, N_DEV, m // N_DEV).transpose(1, 0, 2)


def _own_and_bf16(part):
    return lax.dynamic_index_in_dim(part, _my_id(), 0, keepdims=False), part.astype(BF16)


def _step(x, pos_col, tgt, pre_w, post_w, wa_in, ba_in, sinks, wa_out_shard, ba_out, wr_in_shard, lb_logits, gnorm_w, wr_out_shard, b_loc, t_len):
    nb = t_len // BLK
    ct, st = _rope_tables(pos_col)
    lb = _lower_bound(lb_logits)
    p0, h0, ga_out = _norm_matmul(x, pre_w[0:1], wa_in, ba_in, "attn_in_proj", [wa_out_shard])
    wa_out = ga_out.reshape(ATTN_WIDTH, D_MODEL)
    o0, g0, gr_in, gr_out = _attn_fwd(p0, ct, st, sinks, b_loc, nb, [wr_in_shard, wr_out_shard])
    wr_in = gr_in.transpose(1, 0, 2).reshape(D_MODEL, REC_IN)
    wr_out = gr_out.reshape(1024, D_MODEL)
    y0, x1 = _outproj_postnorm(g0, wa_out, ba_out, x, post_w[0:1], None, "attn_out_proj")
    p1, h1 = _norm_matmul(x1, pre_w[1:2], wr_in, None, "rec_in_proj")
    o1, g1, sh = _hgrn_fwd(p1, lb, gnorm_w, b_loc, t_len)
    y1, dx2, loss_tile = _outproj_postnorm(g1, wr_out, None, x1, post_w[1:2], tgt, "rec_out_proj_loss")
    dg1, dy1, dpost1, _ = _postnorm_bwd_nt(dx2, y1, post_w[1:2], wr_out, False, "rec_out_bwd")
    d_wr_out = _matmul_tn(g1, dy1, "rec_w_out_grad")
    dq1, df1, di1, dz1, dlb, dgw = _hgrn_bwd(p1, lb, gnorm_w, o1, sh, dg1, b_loc, t_len)
    dps1 = [dq1, df1, di1, dz1]
    dx1, dpre1, _ = _nt_prenorm_bwd(dps1, wr_in, x1, pre_w[1:2], dx2, False, "rec_in_bwd")
    d_wr_in = [_matmul_tn(h1, dpk, f"rec_w_in_grad_{k}") for k, dpk in enumerate(dps1)]
    dg0, dy0, dpost0, dba_out = _postnorm_bwd_nt(dx1, y0, post_w[0:1], wa_out, True, "attn_out_bwd")
    d_wa_out = _matmul_tn(g0, dy0, "attn_w_out_grad")
    owns, wires = zip(*[_own_and_bf16(part) for part in (
        _by_owner_cols(jnp.concatenate(d_wr_in, axis=1)), d_wr_out.reshape(N_DEV, 1024 // N_DEV, D_MODEL),
        d_wa_out.reshape(N_DEV, ATTN_WIDTH // N_DEV, D_MODEL))])
    dp0, dsink_tile, *lands = _attn_bwd(p0, ct, st, sinks, o0, dg0, b_loc, nb, list(wires))
    d_wa_in = _matmul_tn(h0, dp0, "attn_w_in_grad")
    own_a_in, wire_a_in = _own_and_bf16(_by_owner_cols(_qkvz(d_wa_in)))
    dx0, dpre0, dba_in, land_a_in = _nt_prenorm_bwd([dp0], wa_in, x, pre_w[0:1], dx1, True, "attn_in_bwd", [wire_a_in])
    small = dict(pre=jnp.concatenate([dpre0, dpre1], axis=0), post=jnp.concatenate([dpost0, dpost1], axis=0),
                 ba_in=dba_in, sinks=dsink_tile[0:1, 0:N_HEADS], ba_out=dba_out, lb=dlb, gnorm=jnp.sum(dgw, axis=0))
    return loss_tile, dx0, list(zip(lands, owns)) + [(land_a_in, own_a_in)], small


def _my_id():
    return lax.axis_index("x") * 4 + lax.axis_index("y") * 2 + lax.axis_index("c")


def _peer(k):
    x, y, c = lax.axis_index("x"), lax.axis_index("y"), lax.axis_index("c")
    return (x ^ ((k >> 2) & 1), y ^ ((k >> 1) & 1), c ^ (k & 1))


def _peer_id(k):
    return _my_id() ^ k


def _all_gather(shards):
    nsh = len(shards)

    def body(*refs):
        ins, outs, sems = refs[:nsh], refs[nsh:2 * nsh], refs[2 * nsh:]
        _gather_start(ins, outs, sems)
        _gather_wait(ins, outs, sems)

    hbm = pl.BlockSpec(memory_space=pl.ANY)
    return pl.pallas_call(
        body, name="comm_all_gather", in_specs=[hbm] * nsh, out_specs=[hbm] * nsh,
        out_shape=_gather_shapes(shards), scratch_shapes=_gather_sems(nsh),
    )(*shards)


def _gather_shapes(shards):
    return [jax.ShapeDtypeStruct((N_DEV,) + s.shape, s.dtype) for s in shards]


def _gather_sems(nsh):
    return [pltpu.SemaphoreType.DMA((nsh, N_DEV - 1)), pltpu.SemaphoreType.DMA((nsh, N_DEV - 1)), pltpu.SemaphoreType.DMA((nsh,))]


def _gather_copies(ins, outs, sems, received):
    send_sems, recv_sems, local_sems = sems
    me = _my_id()
    local = [pltpu.make_async_copy(ins[a], outs[a].at[me], local_sems.at[a]) for a in range(len(ins))]
    remote = [pltpu.make_async_remote_copy(
        src_ref=ins[a], dst_ref=outs[a].at[_peer_id(k) if received else me], send_sem=send_sems.at[a, k - 1],
        recv_sem=recv_sems.at[a, k - 1], device_id=_peer(k), device_id_type=MESH)
        for a in range(len(ins)) for k in range(1, N_DEV)]
    return local, remote


def _gather_start(ins, outs, sems):
    local, sends = _gather_copies(ins, outs, sems, False)
    for cp in local + sends:
        cp.start()


def _gather_wait(ins, outs, sems):
    local, recvs = _gather_copies(ins, outs, sems, True)
    for cp in recvs:
        cp.wait_recv()
    for cp in recvs:
        cp.wait_send()
    for cp in local:
        cp.wait()


def _scatter_lands(parts):
    return [jax.ShapeDtypeStruct((N_DEV - 1,) + p.shape[1:], p.dtype) for p in parts]


def _scatter_sems(nparts):
    return [pltpu.SemaphoreType.DMA((nparts, N_DEV - 1)), pltpu.SemaphoreType.DMA((nparts, N_DEV - 1))]


def _scatter_copies(parts, lands, sems):
    send_sems, recv_sems = sems
    return [pltpu.make_async_remote_copy(
        src_ref=parts[a].at[_peer_id(k)], dst_ref=lands[a].at[k - 1], send_sem=send_sems.at[a, k - 1],
        recv_sem=recv_sems.at[a, k - 1], device_id=_peer(k), device_id_type=MESH)
        for a in range(len(parts)) for k in range(1, N_DEV)]


def _scatter_start(parts, lands, sems):
    for cp in _scatter_copies(parts, lands, sems):
        cp.start()


def _scatter_wait(parts, lands, sems):
    copies = _scatter_copies(parts, lands, sems)
    for cp in copies:
        cp.wait_recv()
    for cp in copies:
        cp.wait_send()


def _adamw(w, g, m, v):
    m2 = ADAM_B1 * m + (1.0 - ADAM_B1) * g
    v2 = ADAM_B2 * v + (1.0 - ADAM_B2) * (g * g)
    m_hat = m2 / (1.0 - ADAM_B1 ** ADAM_STEP)
    v_hat = v2 / (1.0 - ADAM_B2 ** ADAM_STEP)
    delta = -ADAM_LR * (m_hat / (jnp.sqrt(v_hat) + ADAM_EPS) + ADAM_WD * w)
    return delta, m2, v2


def _sum_adamw_rows(land_ref, own_ref, w_ref, m_ref, v_ref, out_refs):
    r, c = own_ref.shape
    rc = 64 if r % 64 == 0 else r
    me = _my_id()
    g_ref, d_ref, m2_ref, v2_ref = out_refs

    def rows(i, carry):
        rs = pl.ds(pl.multiple_of(i * rc, rc), rc)
        g = jnp.zeros((rc, c), F32)
        for dev in range(N_DEV):
            k = dev ^ me
            g = g + jnp.where(k == 0, own_ref[rs, :], land_ref[jnp.maximum(k - 1, 0), rs, :].astype(F32))
        delta, m2, v2 = _adamw(w_ref[rs, :], g, m_ref[rs, :], v_ref[rs, :])
        g_ref[rs, :] = g
        d_ref[rs, :] = delta
        m2_ref[rs, :] = m2
        v2_ref[rs, :] = v2
        return carry

    lax.fori_loop(0, r // rc, rows, 0)


def _sum_adamw(land, own, w, m, v, name):
    r, c = own.shape

    def body(land_ref, own_ref, w_ref, m_ref, v_ref, g_ref, d_ref, m2_ref, v2_ref):
        _sum_adamw_rows(land_ref, own_ref, w_ref, m_ref, v_ref, (g_ref, d_ref, m2_ref, v2_ref))

    vmem = pl.BlockSpec(memory_space=pltpu.VMEM)
    return pl.pallas_call(
        body, name=name, in_specs=[vmem] * 5, out_specs=[vmem] * 4, out_shape=[jax.ShapeDtypeStruct((r, c), F32)] * 4,
        compiler_params=_params(None, 56),
    )(land, own, w, m, v)


_SMALL = [("pre_norm_w", 2048), ("post_norm_w", 2048), ("attn_b_in", 2304), ("attn_sinks", 16), ("attn_b_out", 1024),
          ("rec_lb_logits", 2048), ("rec_gnorm_w", 128)]
_TILE = SUBLANES * LANES


def _small_rows(size):
    return -(-size // _TILE) * SUBLANES


_SMALL_OFF = {}
_r = 0
for _name, _size in _SMALL:
    _SMALL_OFF[_name] = _r
    _r += _small_rows(_size)
_SMALL_ROWS = _r


def _pack_small(pieces):
    out = []
    for name, size in _SMALL:
        flat = pieces[name].reshape(-1).astype(F32)
        out.append(jnp.pad(flat, (0, _small_rows(size) * LANES - size)).reshape(-1, LANES))
    return jnp.concatenate(out, axis=0)


def _unpack_small(packed, shapes):
    return {name: packed[_SMALL_OFF[name]:_SMALL_OFF[name] + _small_rows(size)].reshape(-1)[:size].reshape(shapes[name])
            for name, size in _SMALL}


def _small_allreduce_adamw(gpart, w, m, v):
    lb0 = _SMALL_OFF["rec_lb_logits"]

    def body(gp_ref, w_ref, m_ref, v_ref, g_ref, d_ref, m2_ref, v2_ref, land_ref, send_sems, recv_sems):
        me = _my_id()
        sent = []
        for k in range(1, N_DEV):
            cp = pltpu.make_async_remote_copy(src_ref=gp_ref, dst_ref=land_ref.at[k - 1], send_sem=send_sems.at[k - 1],
                                              recv_sem=recv_sems.at[k - 1], device_id=_peer(k), device_id_type=MESH)
            cp.start()
            sent.append(cp)
        for cp in sent:
            cp.wait_recv()
        for cp in sent:
            cp.wait_send()
        g = jnp.zeros((_SMALL_ROWS, LANES), F32)
        for dev in range(N_DEV):
            k = dev ^ me
            g = g + jnp.where(k == 0, gp_ref[...], land_ref[jnp.maximum(k - 1, 0)])
        g_ref[...] = g
        l0, l1 = w_ref[lb0:lb0 + SUBLANES, :], w_ref[lb0 + SUBLANES:lb0 + 2 * SUBLANES, :]
        mx = jnp.maximum(l0, l1)
        e0, e1 = jnp.exp(l0 - mx), jnp.exp(l1 - mx)
        p1 = e1 / (e0 + e1)
        dl1 = (1.0 - p1) * p1 * g[lb0:lb0 + SUBLANES, :]
        g_ref[lb0:lb0 + SUBLANES, :] = -dl1
        g_ref[lb0 + SUBLANES:lb0 + 2 * SUBLANES, :] = dl1
        delta, m2, v2 = _adamw(w_ref[...], g_ref[...], m_ref[...], v_ref[...])
        d_ref[...] = delta
        m2_ref[...] = m2
        v2_ref[...] = v2

    vmem = pl.BlockSpec(memory_space=pltpu.VMEM)
    return pl.pallas_call(
        body, name="comm_small_allreduce_adamw", in_specs=[vmem] * 4, out_specs=[vmem] * 4,
        out_shape=[jax.ShapeDtypeStruct((_SMALL_ROWS, LANES), F32)] * 4,
        scratch_shapes=[pltpu.VMEM((N_DEV - 1, _SMALL_ROWS, LANES), F32), pltpu.SemaphoreType.DMA((N_DEV - 1,)),
                        pltpu.SemaphoreType.DMA((N_DEV - 1,))],
    )(gpart, w, m, v)


def _qzkv(a):
    return jnp.concatenate([a[..., :1024], a[..., 1280:], a[..., 1024:1280]], axis=-1)


def _qkvz(a):
    return jnp.concatenate([a[..., :1024], a[..., 2048:], a[..., 1024:2048]], axis=-1)


def kernel(x, positions, pre_norm_w, post_norm_w, attn_w_in, attn_b_in, attn_sinks, attn_w_out, attn_b_out, rec_w_in, rec_lb_logits, rec_gnorm_w, rec_w_out, loss_target, m_pre_norm_w, m_post_norm_w, m_attn_w_in, m_attn_b_in, m_attn_sinks, m_attn_w_out, m_attn_b_out, m_rec_w_in, m_rec_lb_logits, m_rec_gnorm_w, m_rec_w_out, v_pre_norm_w, v_post_norm_w, v_attn_w_in, v_attn_b_in, v_attn_sinks, v_attn_w_out, v_attn_b_out, v_rec_w_in, v_rec_lb_logits, v_rec_gnorm_w, v_rec_w_out):
    b_loc, t_len, _ = x.shape
    n = b_loc * t_len
    ga_in, = _all_gather([attn_w_in[0].astype(BF16)])
    wa_in = _qzkv(ga_in.transpose(1, 0, 2).reshape(D_MODEL, ATTN_IN))

    loss_tile, dx, landed, small = _step(
        x.reshape(n, D_MODEL), positions.reshape(n, 1).astype(F32), loss_target.reshape(n, D_MODEL),
        pre_norm_w, post_norm_w, wa_in, _qzkv(attn_b_in), attn_sinks, attn_w_out[0].astype(BF16), attn_b_out,
        rec_w_in[0].astype(BF16), rec_lb_logits, rec_gnorm_w, rec_w_out[0].astype(BF16), b_loc, t_len)
    loss = lax.psum(loss_tile[0, 0], ("x", "y", "c"))

    lift = lambda outs: tuple(a[None] for a in outs)
    (l_r_in, o_r_in), (l_r_out, o_r_out), (l_a_out, o_a_out), (l_a_in, o_a_in) = landed
    r_a_in = lift(_sum_adamw(l_a_in, o_a_in, attn_w_in[0], m_attn_w_in[0], v_attn_w_in[0], "adamw_attn_w_in"))
    r_r_in = lift(_sum_adamw(l_r_in, o_r_in, rec_w_in[0], m_rec_w_in[0], v_rec_w_in[0], "adamw_rec_w_in"))
    r_r_out = lift(_sum_adamw(l_r_out, o_r_out, rec_w_out[0], m_rec_w_out[0], v_rec_w_out[0], "adamw_rec_w_out"))
    r_a_out = lift(_sum_adamw(l_a_out, o_a_out, attn_w_out[0], m_attn_w_out[0], v_attn_w_out[0], "adamw_attn_w_out"))

    gsmall = dict(pre_norm_w=small["pre"], post_norm_w=small["post"], attn_b_in=_qkvz(small["ba_in"]), attn_sinks=small["sinks"],
                  attn_b_out=small["ba_out"], rec_lb_logits=jnp.concatenate([small["lb"], jnp.zeros_like(small["lb"])], axis=0),
                  rec_gnorm_w=small["gnorm"])
    wsmall = dict(pre_norm_w=pre_norm_w, post_norm_w=post_norm_w, attn_b_in=attn_b_in, attn_sinks=attn_sinks,
                  attn_b_out=attn_b_out, rec_lb_logits=rec_lb_logits, rec_gnorm_w=rec_gnorm_w)
    msmall = dict(pre_norm_w=m_pre_norm_w, post_norm_w=m_post_norm_w, attn_b_in=m_attn_b_in, attn_sinks=m_attn_sinks,
                  attn_b_out=m_attn_b_out, rec_lb_logits=m_rec_lb_logits, rec_gnorm_w=m_rec_gnorm_w)
    vsmall = dict(pre_norm_w=v_pre_norm_w, post_norm_w=v_post_norm_w, attn_b_in=v_attn_b_in, attn_sinks=v_attn_sinks,
                  attn_b_out=v_attn_b_out, rec_lb_logits=v_rec_lb_logits, rec_gnorm_w=v_rec_gnorm_w)
    shapes = {k: a.shape for k, a in wsmall.items()}
    packed = _small_allreduce_adamw(_pack_small(gsmall), _pack_small(wsmall), _pack_small(msmall), _pack_small(vsmall))
    sg, sd, sm, sv = [_unpack_small(a, shapes) for a in packed]

    big = {"attn_w_in": r_a_in, "attn_w_out": r_a_out, "rec_w_in": r_r_in, "rec_w_out": r_r_out}
    order = ["pre_norm_w", "post_norm_w", "attn_w_in", "attn_b_in", "attn_sinks", "attn_w_out", "attn_b_out", "rec_w_in",
             "rec_lb_logits", "rec_gnorm_w", "rec_w_out"]
    outs = [loss, dx.reshape(b_loc, t_len, D_MODEL)]
    for idx, small_set in enumerate((sg, sd, sm, sv)):
        outs += [big[nm][idx] if nm in big else small_set[nm] for nm in order]
    return tuple(outs)
```

```python
import numpy as np
import jax
import jax.numpy as jnp
from jax import lax
from jax.experimental import pallas as pl
from jax.experimental.pallas import tpu as pltpu

F32, BF16 = jnp.float32, jnp.bfloat16
MESH = pl.DeviceIdType.MESH
N_DEV = 8

D_MODEL = 1024
N_HEADS, HEAD_DIM, N_KV, GROUP = 16, 64, 2, 8
ATTN_WIDTH, KV_WIDTH = 1024, 128
ATTN_IN = 2 * ATTN_WIDTH + 2 * KV_WIDTH
BLK = 128
ROPE_THETA, ROPE_HALF = 500000.0, 8
REC_HEADS, REC_K = 8, 128
REC_IN = 4 * 1024
CH = 32
NORM_EPS = 1e-6
ADAM_LR, ADAM_B1, ADAM_B2, ADAM_EPS, ADAM_WD, ADAM_STEP = 0.001, 0.9, 0.999, 1e-08, 0.01, 10

LANES, SUBLANES = 128, 8
TM = 512
NT_DIMS = (((1,), (1,)), ((), ()))
TN_DIMS = (((0,), (0,)), ((), ()))
MB = 2 ** 20


def _params(sem=None, vmem_mb=48, **kw):
    return pltpu.CompilerParams(dimension_semantics=sem, vmem_limit_bytes=vmem_mb * MB, **kw)


def _col_chunk(m):
    return 768 if m % 1024 else 1024


def _sigmoid(x):
    return 1.0 / (1.0 + jnp.exp(-x))


def _split3(x):
    hi = x.astype(BF16)
    r1 = x - hi.astype(F32)
    mid = r1.astype(BF16)
    lo = (r1 - mid.astype(F32)).astype(BF16)
    return hi, mid, lo


def _dot3(l_bf, x):
    hi, mid, lo = _split3(x)
    return (jnp.dot(l_bf, hi, preferred_element_type=F32) + jnp.dot(l_bf, mid, preferred_element_type=F32)
            + jnp.dot(l_bf, lo, preferred_element_type=F32))


def _rope_tables(pos_col):
    n = pos_col.shape[0]
    lane = np.arange(LANES) % HEAD_DIM
    inv = np.float32(ROPE_THETA) ** (-(np.arange(ROPE_HALF, dtype=np.float32) * np.float32(2.0) / np.float32(2 * ROPE_HALF)))
    freq = np.where(lane < 2 * ROPE_HALF, inv[lane % ROPE_HALF], 0.0).astype(np.float32)[None, :]
    sign = np.where(lane < ROPE_HALF, -1.0, np.where(lane < 2 * ROPE_HALF, 1.0, 0.0)).astype(np.float32)[None, :]

    def body(p_ref, f_ref, s_ref, c_out, s_out):
        ang = p_ref[...] * f_ref[...]
        c_out[...] = jnp.cos(ang)
        s_out[...] = jnp.sin(ang) * s_ref[...]

    row = pl.BlockSpec((TM, 1), lambda i: (i, 0))
    vec = pl.BlockSpec((1, LANES), lambda i: (0, 0))
    out = pl.BlockSpec((TM, LANES), lambda i: (i, 0))
    return pl.pallas_call(
        body, name="rope_tables", grid=(n // TM,), in_specs=[row, vec, vec], out_specs=[out, out],
        out_shape=[jax.ShapeDtypeStruct((n, LANES), F32)] * 2, compiler_params=_params(("arbitrary",)),
    )(pos_col, jnp.asarray(freq), jnp.asarray(sign))


def _rope_apply(xv, c, s, lm):
    partner = jnp.where(lm < ROPE_HALF, pltpu.roll(xv, LANES - ROPE_HALF, 1), pltpu.roll(xv, ROPE_HALF, 1))
    return xv * c + partner * s


def _rope_bwd(dy, c, s, lm):
    t = dy * s
    partner = jnp.where(lm < ROPE_HALF, pltpu.roll(t, LANES - ROPE_HALF, 1),
                        jnp.where(lm < 2 * ROPE_HALF, pltpu.roll(t, ROPE_HALF, 1), 0.0))
    return dy * c + partner


def _lower_bound(lb_logits):
    def body(l_ref, o_ref):
        l0, l1 = l_ref[0:1, :], l_ref[1:2, :]
        m = jnp.maximum(l0, l1)
        e0, e1 = jnp.exp(l0 - m), jnp.exp(l1 - m)
        o_ref[...] = e1 / (e0 + e1)

    return pl.pallas_call(body, name="lower_bound", out_shape=jax.ShapeDtypeStruct((1, lb_logits.shape[1]), F32))(lb_logits)


def _norm_matmul(x, pw, w, bias, name, shards=()):
    n, m = x.shape[0], w.shape[1]
    cn = _col_chunk(m)
    has_bias = bias is not None
    nsh, steps = len(shards), n // TM

    def body(*refs):
        refs = list(refs)
        x_ref, pw_ref, w_ref = refs[:3]
        b_ref = refs[3] if has_bias else None
        refs = refs[4 if has_bias else 3:]
        sh_in, (p_ref, h_ref), sh_out, sems = refs[:nsh], refs[nsh:nsh + 2], refs[nsh + 2:2 * nsh + 2], refs[2 * nsh + 2:]
        if nsh:
            @pl.when(pl.program_id(0) == 0)
            def _():
                _gather_start(sh_in, sh_out, sems)

        xv = x_ref[...]
        r = lax.rsqrt(jnp.mean(xv * xv, axis=-1, keepdims=True) + NORM_EPS)
        h = ((xv * r) * pw_ref[...]).astype(BF16)
        h_ref[...] = h
        for j in range(0, m, cn):
            acc = jnp.dot(h, w_ref[:, j:j + cn], preferred_element_type=F32)
            if has_bias:
                acc = acc + b_ref[:, j:j + cn]
            p_ref[:, j:j + cn] = acc

        if nsh:
            @pl.when(pl.program_id(0) == steps - 1)
            def _():
                _gather_wait(sh_in, sh_out, sems)

    rows = pl.BlockSpec((TM, D_MODEL), lambda i: (i, 0))
    const = lambda shape: pl.BlockSpec(shape, lambda i: (0, 0))
    hbm = pl.BlockSpec(memory_space=pl.ANY)
    in_specs = [rows, const((1, D_MODEL)), const((D_MODEL, m))] + ([const((1, m))] if has_bias else []) + [hbm] * nsh
    args = (x, pw, w) + ((bias,) if has_bias else ()) + tuple(shards)
    return pl.pallas_call(
        body, name=name, grid=(steps,), in_specs=in_specs,
        out_specs=[pl.BlockSpec((TM, m), lambda i: (i, 0)), rows] + [hbm] * nsh,
        out_shape=[jax.ShapeDtypeStruct((n, m), F32), jax.ShapeDtypeStruct((n, D_MODEL), BF16)] + _gather_shapes(shards),
        scratch_shapes=_gather_sems(nsh) if nsh else [],
        compiler_params=_params(("arbitrary",), 56),
    )(*args)


def _outproj_postnorm(g, w, bias, xres, qw, tgt, name):
    n = g.shape[0]
    has_bias, has_loss = bias is not None, tgt is not None
    steps = n // TM

    def body(*refs):
        refs = list(refs)
        g_ref, w_ref = refs.pop(0), refs.pop(0)
        b_ref = refs.pop(0) if has_bias else None
        x_ref, qw_ref = refs.pop(0), refs.pop(0)
        t_ref = refs.pop(0) if has_loss else None
        y_ref, o_ref = refs.pop(0), refs.pop(0)
        y = jnp.dot(g_ref[...], w_ref[...], preferred_element_type=F32)
        if has_bias:
            y = y + b_ref[...]
        y_ref[...] = y
        r = lax.rsqrt(jnp.mean(y * y, axis=-1, keepdims=True) + NORM_EPS)
        xn = x_ref[...] + (y * r) * qw_ref[...]
        if not has_loss:
            o_ref[...] = xn
        else:
            loss_ref, acc_ref = refs
            i = pl.program_id(0)
            e = xn - t_ref[...]
            o_ref[...] = e * (1.0 / D_MODEL)

            @pl.when(i == 0)
            def _():
                acc_ref[...] = jnp.zeros_like(acc_ref)

            acc_ref[...] += jnp.sum(e * e, axis=0, keepdims=True)

            @pl.when(i == steps - 1)
            def _():
                loss_ref[...] = jnp.full(loss_ref.shape, jnp.sum(acc_ref[...]) * (0.5 / D_MODEL), F32)

    rows = pl.BlockSpec((TM, D_MODEL), lambda i: (i, 0))
    const = lambda shape: pl.BlockSpec(shape, lambda i: (0, 0))
    in_specs = [rows, const((D_MODEL, D_MODEL))] + ([const((1, D_MODEL))] if has_bias else []) + [rows, const((1, D_MODEL))]
    args = [g, w] + ([bias] if has_bias else []) + [xres, qw]
    out_specs = [rows, rows]
    out_shape = [jax.ShapeDtypeStruct((n, D_MODEL), F32)] * 2
    scratch = []
    if has_loss:
        in_specs.append(rows)
        args.append(tgt)
        out_specs.append(const((SUBLANES, LANES)))
        out_shape.append(jax.ShapeDtypeStruct((SUBLANES, LANES), F32))
        scratch = [pltpu.VMEM((1, D_MODEL), F32)]
    return pl.pallas_call(
        body, name=name, grid=(steps,), in_specs=in_specs, out_specs=out_specs, out_shape=out_shape,
        scratch_shapes=scratch, compiler_params=_params(("arbitrary",), 48),
    )(*args)


_QCOL, _ZCOL, _KCOL, _VCOL = 0, 1024, 2048, 2176


def _head_stack(chunks, heads, lt64):
    return jnp.concatenate([jnp.where(lt64 if n % 2 == 0 else ~lt64, chunks[n // 2], 0.0) for n in heads], axis=0)


def _dup_half(x, h, lt64):
    r = pltpu.roll(x, HEAD_DIM, 1)
    return jnp.where(lt64, x, r) if h == 0 else jnp.where(lt64, r, x)


def _pair_chunk(xt, c2):
    a, b = 2 * c2, 2 * c2 + 1
    return jnp.concatenate([xt[:HEAD_DIM, a * BLK:(a + 1) * BLK], xt[HEAD_DIM:, b * BLK:(b + 1) * BLK]], axis=0).T


def _attn_mask_t(i):
    key = lax.broadcasted_iota(jnp.int32, (2 * BLK, BLK), 0)
    qry = lax.broadcasted_iota(jnp.int32, (2 * BLK, BLK), 1)
    valid = (key > qry) & (key <= qry + BLK) & ((key >= BLK) | (i > 0))
    return jnp.tile(jnp.where(valid, 0.0, -1e30), (1, GROUP))


def _attn_probs_t(s, heads, sink_ref, mask):
    s = s + mask
    head = lax.broadcasted_iota(jnp.int32, (1, len(heads) * BLK), 1) >> 7
    sk = jnp.zeros((1, len(heads) * BLK), F32)
    for j, n in enumerate(heads):
        sk = jnp.where(head == j, sink_ref[0, n], sk)
    m = jnp.maximum(jnp.max(s, axis=0, keepdims=True), sk)
    p = jnp.exp(s - m)
    esk = jnp.exp(sk - m)
    inv = 1.0 / (jnp.sum(p, axis=0, keepdims=True) + esk)
    return p * inv, esk * inv


def _attn_fwd(p, ct, st, sinks, b_loc, nb, shards):
    n = p.shape[0]
    nsh = len(shards)

    def body(sink_ref, q_ref, z_ref, kc_ref, kp_ref, vc_ref, vp_ref, cc_ref, sc_ref, cp_ref, sp_ref, *rest):
        sh_in, (o_ref, g_ref), sh_out, sems = rest[:nsh], rest[nsh:nsh + 2], rest[nsh + 2:2 * nsh + 2], rest[2 * nsh + 2:]
        b, i = pl.program_id(0), pl.program_id(1)

        @pl.when((b == 0) & (i == 0))
        def _():
            _gather_start(sh_in, sh_out, sems)

        lane = lax.broadcasted_iota(jnp.int32, (BLK, LANES), 1)
        lm = lane & (HEAD_DIM - 1)
        cc, sc = cc_ref[...], sc_ref[...]
        kcat = jnp.concatenate([_rope_apply(kp_ref[...], cp_ref[...], sp_ref[...], lm),
                                _rope_apply(kc_ref[...], cc, sc, lm)], axis=0)
        vcat = jnp.concatenate([vp_ref[...], vc_ref[...]], axis=0)
        qr = [_rope_apply(q_ref[:, c * LANES:(c + 1) * LANES], cc, sc, lm) * (HEAD_DIM ** -0.5) for c in range(8)]
        valid = _attn_mask_t(i)
        lt64, lt64k = lane < HEAD_DIM, lax.broadcasted_iota(jnp.int32, (2 * BLK, LANES), 1) < HEAD_DIM
        def kv_head(h):
            heads = list(range(h * GROUP, (h + 1) * GROUP))
            kext, vext = _dup_half(kcat, h, lt64k).astype(BF16), _dup_half(vcat, h, lt64k).astype(BF16)
            qst = _head_stack(qr, heads, lt64).astype(BF16)
            s = lax.dot_general(kext, qst, NT_DIMS, preferred_element_type=F32)
            yield
            pn, _ = _attn_probs_t(s, heads, sink_ref, valid)
            ot = lax.dot_general(vext, pn.astype(BF16), TN_DIMS, preferred_element_type=F32)
            yield
            for c2 in range(GROUP // 2):
                oc = _pair_chunk(ot, c2)
                cols = slice((4 * h + c2) * LANES, (4 * h + c2 + 1) * LANES)
                zc = z_ref[:, cols]
                o_ref[:, cols] = oc
                g_ref[:, cols] = (oc * (zc * _sigmoid(zc))).astype(BF16)

        _in_stages([kv_head(h) for h in range(N_KV)])

        @pl.when((b == b_loc - 1) & (i == nb - 1))
        def _():
            _gather_wait(sh_in, sh_out, sems)

    cur = lambda b, i: b * nb + i
    prev = lambda b, i: b * nb + jnp.maximum(i - 1, 0)
    wide = lambda cb: pl.BlockSpec((BLK, ATTN_WIDTH), lambda b, i: (cur(b, i), cb))
    kv = lambda rowf, cb: pl.BlockSpec((BLK, LANES), lambda b, i: (rowf(b, i), cb))
    hbm = pl.BlockSpec(memory_space=pl.ANY)
    in_specs = [pl.BlockSpec(memory_space=pltpu.SMEM), wide(0), wide(1),
                kv(cur, _KCOL // LANES), kv(prev, _KCOL // LANES), kv(cur, _VCOL // LANES), kv(prev, _VCOL // LANES),
                kv(cur, 0), kv(cur, 0), kv(prev, 0), kv(prev, 0)] + [hbm] * nsh
    return pl.pallas_call(
        body, name="attn_fwd", grid=(b_loc, nb), in_specs=in_specs, out_specs=[wide(0), wide(0)] + [hbm] * nsh,
        out_shape=[jax.ShapeDtypeStruct((n, ATTN_WIDTH), F32), jax.ShapeDtypeStruct((n, ATTN_WIDTH), BF16)] + _gather_shapes(shards),
        scratch_shapes=_gather_sems(nsh), compiler_params=_params(("arbitrary", "arbitrary"), 48),
    )(sinks, p, p, p, p, p, p, ct, st, ct, st, *shards)


def _attn_bwd(p, ct, st, sinks, o, dg, b_loc, nb, parts):
    n = p.shape[0]
    nparts = len(parts)

    def body(sink_ref, q_ref, z_ref, kc_ref, kp_ref, vc_ref, vp_ref, cc_ref, sc_ref, cp_ref, sp_ref, o_ref, dg_ref, *rest):
        part_refs, (dp_ref, ds_ref), land_refs = rest[:nparts], rest[nparts:nparts + 2], rest[nparts + 2:2 * nparts + 2]
        dq_s, dz_s, dk_s, dv_s = rest[2 * nparts + 2:2 * nparts + 6]
        sems = rest[2 * nparts + 6:]
        b, i = pl.program_id(0), pl.program_id(1)

        @pl.when((b == 0) & (i == 0))
        def _():
            _scatter_start(part_refs, land_refs, sems)

        @pl.when((b == b_loc - 1) & (i == nb))
        def _():
            _scatter_wait(part_refs, land_refs, sems)

        lane = lax.broadcasted_iota(jnp.int32, (BLK, LANES), 1)
        lm = lane & (HEAD_DIM - 1)

        @pl.when((b == 0) & (i == 0))
        def _():
            ds_ref[...] = jnp.zeros_like(ds_ref)

        @pl.when(i < nb)
        def _compute():
            cc, sc = cc_ref[...], sc_ref[...]
            kcat = jnp.concatenate([_rope_apply(kp_ref[...], cp_ref[...], sp_ref[...], lm),
                                    _rope_apply(kc_ref[...], cc, sc, lm)], axis=0)
            vcat = jnp.concatenate([vp_ref[...], vc_ref[...]], axis=0)
            qr = [_rope_apply(q_ref[:, c * LANES:(c + 1) * LANES], cc, sc, lm) * (HEAD_DIM ** -0.5) for c in range(8)]
            valid = _attn_mask_t(i)
            lt64, lt64k = lane < HEAD_DIM, lax.broadcasted_iota(jnp.int32, (2 * BLK, LANES), 1) < HEAD_DIM
            do_chunks, doo_chunks, dz_chunks = [], [], []
            for c in range(8):
                cols = slice(c * LANES, (c + 1) * LANES)
                zc, oc, dgc = z_ref[:, cols], o_ref[:, cols], dg_ref[:, cols]
                sg = _sigmoid(zc)
                do_chunks.append(dgc * (zc * sg))
                dz_chunks.append(dgc * oc * (sg * (1.0 + zc * (1.0 - sg))))
                doo_chunks.append(do_chunks[c] * oc)
            dq_chunks = [None] * 8
            dk_h, dv_h, ds_parts = [None] * N_KV, [None] * N_KV, [None] * N_KV
            tile_lane = lax.broadcasted_iota(jnp.int32, (SUBLANES, LANES), 1)
            tile_row = lax.broadcasted_iota(jnp.int32, (SUBLANES, LANES), 0)
            ones8 = jnp.ones((SUBLANES, LANES), BF16)

            def kv_head(h):
                heads = list(range(h * GROUP, (h + 1) * GROUP))
                kext = _dup_half(kcat, h, lt64k)
                kext_bf, kext_t = kext.astype(BF16), kext.T.astype(BF16)
                vext = _dup_half(vcat, h, lt64k).astype(BF16)
                qst = _head_stack(qr, heads, lt64).astype(BF16)
                pn, psink = _attn_probs_t(lax.dot_general(kext_bf, qst, NT_DIMS, preferred_element_type=F32), heads, sink_ref, valid)
                do_bf = _head_stack(do_chunks, heads, lt64).astype(BF16)
                delta = sum(lax.dot_general(ones8, part, NT_DIMS, preferred_element_type=F32)
                            for part in _split3(_head_stack(doo_chunks, heads, lt64)))[0:1, :]
                dpt = lax.dot_general(vext, do_bf, NT_DIMS, preferred_element_type=F32)
                dst = (pn * (dpt - delta)).astype(BF16)
                sink_term = psink * delta
                ds_acc = jnp.zeros((SUBLANES, LANES), F32)
                for j, n in enumerate(heads):
                    val = -jnp.sum(sink_term[:, j * BLK:(j + 1) * BLK])
                    ds_acc = ds_acc + jnp.where((tile_lane == n) & (tile_row == 0), val, 0.0)
                ds_parts[h] = ds_acc
                dqt = jnp.dot(kext_t, dst, preferred_element_type=F32) * (HEAD_DIM ** -0.5)
                dk_ext = jnp.dot(dst, qst, preferred_element_type=F32)
                dv_ext = jnp.dot(pn.astype(BF16), do_bf, preferred_element_type=F32)
                dk_h[h] = dk_ext + pltpu.roll(dk_ext, HEAD_DIM, 1)
                dv_h[h] = dv_ext + pltpu.roll(dv_ext, HEAD_DIM, 1)
                for c2 in range(GROUP // 2):
                    dq_chunks[4 * h + c2] = _rope_bwd(_pair_chunk(dqt, c2), cc, sc, lm)

            for h in range(N_KV):
                kv_head(h)
            ds_ref[...] += ds_parts[0] + ds_parts[1]
            dk_full = jnp.where(lt64k, dk_h[0], dk_h[1])
            dv_full = jnp.where(lt64k, dv_h[0], dv_h[1])

            @pl.when(i >= 1)
            def _emit():
                dp_ref[:, _QCOL:_QCOL + ATTN_WIDTH] = dq_s[...]
                dp_ref[:, _ZCOL:_ZCOL + ATTN_WIDTH] = dz_s[...]
                dp_ref[:, _KCOL:_KCOL + KV_WIDTH] = _rope_bwd(dk_s[...] + dk_full[:BLK], cp_ref[...], sp_ref[...], lm)
                dp_ref[:, _VCOL:_VCOL + KV_WIDTH] = dv_s[...] + dv_full[:BLK]

            for c in range(8):
                dq_s[:, c * LANES:(c + 1) * LANES] = dq_chunks[c]
                dz_s[:, c * LANES:(c + 1) * LANES] = dz_chunks[c]
            dk_s[...] = dk_full[BLK:]
            dv_s[...] = dv_full[BLK:]

        @pl.when(i == nb)
        def _final():
            dp_ref[:, _QCOL:_QCOL + ATTN_WIDTH] = dq_s[...]
            dp_ref[:, _ZCOL:_ZCOL + ATTN_WIDTH] = dz_s[...]
            dp_ref[:, _KCOL:_KCOL + KV_WIDTH] = _rope_bwd(dk_s[...], cc_ref[...], sc_ref[...], lm)
            dp_ref[:, _VCOL:_VCOL + KV_WIDTH] = dv_s[...]

    cur = lambda b, i: b * nb + jnp.minimum(i, nb - 1)
    prev = lambda b, i: b * nb + jnp.maximum(jnp.minimum(i, nb - 1) - 1, 0)
    emit = lambda b, i: b * nb + jnp.maximum(i - 1, 0)
    hbm = pl.BlockSpec(memory_space=pl.ANY)
    wide = lambda cb: pl.BlockSpec((BLK, ATTN_WIDTH), lambda b, i: (cur(b, i), cb))
    kv = lambda rowf, cb: pl.BlockSpec((BLK, LANES), lambda b, i: (rowf(b, i), cb))
    in_specs = [pl.BlockSpec(memory_space=pltpu.SMEM), wide(0), wide(1),
                kv(cur, _KCOL // LANES), kv(prev, _KCOL // LANES), kv(cur, _VCOL // LANES), kv(prev, _VCOL // LANES),
                kv(cur, 0), kv(cur, 0), kv(prev, 0), kv(prev, 0), wide(0), wide(0)] + [hbm] * nparts
    out_specs = [pl.BlockSpec((BLK, ATTN_IN), lambda b, i: (emit(b, i), 0)),
                 pl.BlockSpec((SUBLANES, LANES), lambda b, i: (0, 0))] + [hbm] * nparts
    return pl.pallas_call(
        body, name="attn_bwd", grid=(b_loc, nb + 1), in_specs=in_specs, out_specs=out_specs,
        out_shape=[jax.ShapeDtypeStruct((n, ATTN_IN), F32), jax.ShapeDtypeStruct((SUBLANES, LANES), F32)] + _scatter_lands(parts),
        scratch_shapes=[pltpu.VMEM((BLK, ATTN_WIDTH), F32), pltpu.VMEM((BLK, ATTN_WIDTH), F32),
                        pltpu.VMEM((BLK, KV_WIDTH), F32), pltpu.VMEM((BLK, KV_WIDTH), F32)] + _scatter_sems(nparts),
        compiler_params=_params(("arbitrary", "arbitrary"), 48),
    )(sinks, p, p, p, p, p, p, ct, st, ct, st, o, dg, *parts)


_CUM_ROWS = 256
HALF = CH // 2
_ROW0 = [SUBLANES * (s // SUBLANES) for s in range(CH)]
_ROW1 = [HALF * (s // HALF + 1) for s in range(CH)]
_ROWS_OF = [_ROW1[s] - _ROW0[s] for s in range(CH)]
_OFF_OF = [sum(_ROWS_OF[:s]) for s in range(CH)]


def _tri(lower):
    r = lax.broadcasted_iota(jnp.int32, (_CUM_ROWS, _CUM_ROWS), 0)
    c = lax.broadcasted_iota(jnp.int32, (_CUM_ROWS, _CUM_ROWS), 1)
    same = (r ^ c) < CH
    return (same & ((c <= r) if lower else (c >= r))).astype(BF16)


def _gates(qp, fp, lb):
    e = jnp.exp(-jnp.abs(fp))
    log_sig = jnp.minimum(fp, 0.0) - jnp.log(1.0 + e)
    a = jnp.log(lb)
    c = jnp.log(1.0 - lb) + log_sig
    g = jnp.maximum(a, c) + jnp.log(1.0 + jnp.exp(-jnp.abs(a - c)))
    sig_neg = jnp.where(fp >= 0, e, 1.0) / (1.0 + e)
    return qp * _sigmoid(qp), g, (1.0 - lb) * sig_neg, sig_neg


def _pair_rows(bc, s):
    return jnp.exp(jnp.minimum(bc[_ROW0[s]:_ROW1[s], :] - bc[s:s + 1, :], 0.0))


def _cross_half(q, k, bc):
    r = bc[HALF - 1:HALF, :]
    e1, e0 = jnp.exp(bc[HALF:, :] - r), jnp.exp(r - bc[:HALF, :])
    return q[HALF:, :] * e1, e1, k[:HALF, :] * e0, e0


HP = 8
REC_TB = 256
_HW = HP * REC_K


def _hgrn_specs(tb, nt, reverse):
    tmap = (lambda t: nt - 1 - t) if reverse else (lambda t: t)
    groups = REC_HEADS // HP
    blk = lambda cb: pl.BlockSpec((tb, _HW), lambda h, b, t: (b * nt + tmap(t), cb * groups + h))
    head = pl.BlockSpec((tb, _HW), lambda h, b, t: (b * nt + tmap(t), h))
    lbs = pl.BlockSpec((1, _HW), lambda h, b, t: (0, h))
    gws = pl.BlockSpec((1, REC_K), lambda h, b, t: (0, 0))
    hist = pl.BlockSpec((HP, 1, tb // CH, REC_K, REC_K), lambda h, b, t: (h, b, tmap(t), 0, 0))
    return blk, head, lbs, gws, hist


def _chunk_rows(c, first=0, size=CH):
    start = c * CH + first
    return pl.ds(start if isinstance(start, int) else pl.multiple_of(start, CH if first % CH == 0 else SUBLANES), size)


def _in_stages(heads):
    live = list(heads)
    while live:
        live = [g for g in live if next(g, live) is not live]


def _cumsum_chunks(tri, x, out_ref, tb):
    for r in range(0, tb, _CUM_ROWS):
        out_ref[r:r + _CUM_ROWS, :] = _dot3(tri, x[r:r + _CUM_ROWS, :])


def _hgrn_fwd(p, lb, gw, b_loc, t_len):
    n = p.shape[0]
    tb = min(REC_TB, t_len)
    nt, nck = t_len // tb, tb // CH

    def body(qp_ref, fp_ref, i_ref, z_ref, lb_ref, gw_ref, oraw_ref, g_ref, sh_ref, q_s, k_s, b_s, o_s, st_ref,
             car_o, car_a, car_s, car_st):
        @pl.when(pl.program_id(2) == 0)
        def _():
            st_ref[...] = jnp.zeros_like(st_ref)

        qv, g, kk, _ = _gates(qp_ref[...], fp_ref[...], lb_ref[...])
        q_s[...] = qv
        k_s[...] = kk
        _cumsum_chunks(_tri(True), g, b_s, tb)
        ones = jnp.ones((REC_K, REC_K), BF16)
        sub = lax.broadcasted_iota(jnp.int32, (SUBLANES, REC_K), 0)

        rows_of = _chunk_rows

        def issue(c, hp):
            rs, cs = rows_of(c), slice(hp * REC_K, (hp + 1) * REC_K)
            q, k, bc, v = q_s[rs, cs], k_s[rs, cs], b_s[rs, cs], i_ref[rs, cs]
            st = st_ref[hp]
            sh_ref[hp, 0, c] = st
            o = lax.dot_general((q * jnp.exp(bc)).astype(BF16), st.astype(BF16), NT_DIMS, preferred_element_type=F32)
            w = jnp.concatenate([q[_ROW0[s]:_ROW1[s], :] * _pair_rows(bc, s) * k[s:s + 1, :] for s in range(CH)], axis=0)
            a = jnp.dot(w.astype(BF16), ones, preferred_element_type=F32)
            qe1, _, ke0, _ = _cross_half(q, k, bc)
            s10 = lax.dot_general(qe1.astype(BF16), ke0.astype(BF16), NT_DIMS, preferred_element_type=F32)
            kd = k * jnp.exp(bc[CH - 1:CH, :] - bc)
            st_new = lax.dot_general(v.astype(BF16), kd.astype(BF16), TN_DIMS, preferred_element_type=F32)
            return o, a, s10, st_new

        def advance_state(c, hp, st_new):
            bl = b_s[_chunk_rows(c, CH - SUBLANES, SUBLANES), hp * REC_K:(hp + 1) * REC_K][SUBLANES - 1:, :]
            st_ref[hp] = st_ref[hp] * jnp.exp(bl) + st_new

        def cross(c, hp, s10):
            v0 = i_ref[_chunk_rows(c, 0, HALF), hp * REC_K:(hp + 1) * REC_K]
            return jnp.dot(s10.astype(BF16), v0.astype(BF16), preferred_element_type=F32)

        def finish(c, hp, o, a, o_cross):
            rs, cs = rows_of(c), slice(hp * REC_K, (hp + 1) * REC_K)
            v = i_ref[rs, cs]
            acc = [jnp.zeros((SUBLANES, REC_K), F32) for _ in range(CH // SUBLANES)]
            for s in range(CH):
                j = s // SUBLANES
                vs = v[s:s + 1, :]
                for jj in range(j, _ROW1[s] // SUBLANES):
                    blk = a[_OFF_OF[s] + (jj - j) * SUBLANES:_OFF_OF[s] + (jj - j + 1) * SUBLANES, :]
                    if jj == j:
                        blk = jnp.where(sub >= s - j * SUBLANES, blk, 0.0)
                    acc[jj] = acc[jj] + blk * vs
            o_s[rs, cs] = o + jnp.concatenate(acc, axis=0) + jnp.concatenate([jnp.zeros((HALF, REC_K), F32), o_cross], axis=0)

        def park(slot, results):
            for hp, (o, a, s10, st_new) in enumerate(results):
                car_o[slot, hp], car_a[slot, hp], car_s[slot, hp], car_st[slot, hp] = o, a, s10, st_new

        def retire(c, slot):
            for hp in range(HP):
                advance_state(c, hp, car_st[slot, hp])
            yield
            crosses = [cross(c, hp, car_s[slot, hp]) for hp in range(HP)]
            for hp in range(HP):
                finish(c, hp, car_o[slot, hp], car_a[slot, hp], crosses[hp])

        def step(c, slot):
            closing = retire(c - 1, slot)
            next(closing)
            park(1 - slot, [issue(c, hp) for hp in range(HP)])
            next(closing, None)

        def trip(j, carry):
            step(2 * j + 1, 0)
            step(2 * j + 2, 1)
            return carry

        assert nck % 2 == 0
        park(0, [issue(0, hp) for hp in range(HP)])
        lax.fori_loop(0, nck // 2 - 1, trip, 0)
        step(nck - 1, 0)
        for _ in retire(nck - 1, 1):
            pass
        oraw_ref[...] = o_s[...]
        for hp in range(HP):
            cs = slice(hp * REC_K, (hp + 1) * REC_K)
            o, zc = o_s[:, cs], z_ref[:, cs]
            on = (o * lax.rsqrt(jnp.mean(o * o, axis=-1, keepdims=True) + NORM_EPS)) * gw_ref[...]
            g_ref[:, cs] = (on * (zc * _sigmoid(zc))).astype(BF16)

    blk, head, lbs, gws, hist = _hgrn_specs(tb, nt, False)
    return pl.pallas_call(
        body, name="hgrn_fwd", grid=(REC_HEADS // HP, b_loc, nt),
        in_specs=[blk(0), blk(1), blk(2), blk(3), lbs, gws], out_specs=[head, head, hist, head, head, head],
        out_shape=[jax.ShapeDtypeStruct((n, 1024), F32), jax.ShapeDtypeStruct((n, 1024), BF16),
                   jax.ShapeDtypeStruct((REC_HEADS, b_loc, t_len // CH, REC_K, REC_K), F32)]
        + [jax.ShapeDtypeStruct((n, 1024), F32)] * 3,
        scratch_shapes=[pltpu.VMEM((tb, _HW), F32)] + [pltpu.VMEM((HP, REC_K, REC_K), F32)] + [
            pltpu.VMEM((2, HP, CH, REC_K), F32), pltpu.VMEM((2, HP, sum(_ROWS_OF), REC_K), F32),
            pltpu.VMEM((2, HP, HALF, HALF), F32), pltpu.VMEM((2, HP, REC_K, REC_K), F32)],
        compiler_params=_params(("arbitrary", "arbitrary", "arbitrary"), 48),
    )(p, p, p, p, lb, gw)


def _hgrn_bwd(p, lb, gw, oraw, sh, dg, gates, b_loc, t_len):
    n = p.shape[0]
    tb = min(REC_TB, t_len)
    nt, nck = t_len // tb, tb // CH

    def body(qp_ref, fp_ref, i_ref, z_ref, lb_ref, gw_ref, oraw_ref, dg_ref, sh_ref, q_s, k_s, b_s,
             dq_ref, df_ref, di_ref, dz_ref, dlb_ref, dgw_ref,
             do_s, dqv_s, dk_s, db_s, dst_ref, car_r, car_dst, car_dec, car_a, car_da, car_x):
        b, t = pl.program_id(1), pl.program_id(2)

        @pl.when(t == 0)
        def _():
            dst_ref[...] = jnp.zeros_like(dst_ref)

        @pl.when((b == 0) & (t == 0))
        def _():
            dlb_ref[...] = jnp.zeros_like(dlb_ref)
            dgw_ref[...] = jnp.zeros_like(dgw_ref)

        lbv, qp, fp = lb_ref[...], qp_ref[...], fp_ref[...]
        e_f = jnp.exp(-jnp.abs(fp))
        r_f = 1.0 / (1.0 + e_f)
        sig_neg = jnp.where(fp >= 0, e_f, 1.0) * r_f
        fg_inv = 1.0 / (lbv + (1.0 - lbv) * (jnp.where(fp >= 0, 1.0, e_f) * r_f))
        gwv = gw_ref[...]
        for hp in range(HP):
            cs = slice(hp * REC_K, (hp + 1) * REC_K)
            o, zc, dgv = oraw_ref[:, cs], z_ref[:, cs], dg_ref[:, cs]
            rn = lax.rsqrt(jnp.mean(o * o, axis=-1, keepdims=True) + NORM_EPS)
            on = o * rn
            sgz = _sigmoid(zc)
            dz_ref[:, cs] = (dgv * (on * gwv) * (sgz * (1.0 + zc * (1.0 - sgz)))).astype(BF16)
            dpre = dgv * (zc * sgz)
            dgw_ref[hp] += jnp.sum(dpre * on, axis=0, keepdims=True)
            don = dpre * gwv
            do_s[:, cs] = rn * (don - on * jnp.mean(don * on, axis=-1, keepdims=True))

        ones = jnp.ones((REC_K, REC_K), BF16)
        sub = lax.broadcasted_iota(jnp.int32, (SUBLANES, REC_K), 0)
        rowid = lax.broadcasted_iota(jnp.int32, (CH, REC_K), 0)
        ngrp = CH // SUBLANES
        piece_row = lax.broadcasted_iota(jnp.int32, (1, sum(_ROWS_OF)), 1)
        key_of = jnp.zeros((1, sum(_ROWS_OF)), jnp.int32)
        for s in range(1, CH):
            key_of = jnp.where(piece_row >= _OFF_OF[s], s, key_of)
        pick = (key_of == lax.broadcasted_iota(jnp.int32, (CH, sum(_ROWS_OF)), 0)).astype(BF16)

        def operands(c, hp):
            rs, cs = _chunk_rows(c), slice(hp * REC_K, (hp + 1) * REC_K)
            return rs, cs, q_s[rs, cs], k_s[rs, cs], b_s[rs, cs], i_ref[rs, cs], do_s[rs, cs]

        def issue(c, hp, slot):
            _, _, q, k, bc, v, do = operands(c, hp)
            st, dst = sh_ref[hp, 0, c], dst_ref[hp]
            qe, kd = q * jnp.exp(bc), k * jnp.exp(bc[CH - 1:CH, :] - bc)
            do_bf, dst_bf = do.astype(BF16), dst.astype(BF16)
            car_r[slot, hp, 0:CH] = jnp.dot(do_bf, st.astype(BF16), preferred_element_type=F32)
            car_r[slot, hp, CH:2 * CH] = jnp.dot(v.astype(BF16), dst_bf, preferred_element_type=F32)
            car_r[slot, hp, 2 * CH:3 * CH] = lax.dot_general(kd.astype(BF16), dst_bf, NT_DIMS, preferred_element_type=F32)
            car_dst[slot, hp] = lax.dot_general(do_bf, qe.astype(BF16), TN_DIMS, preferred_element_type=F32)
            dec = jnp.concatenate([_pair_rows(bc, s) for s in range(CH)], axis=0)
            qk = jnp.concatenate([q[_ROW0[s]:_ROW1[s], :] * k[s:s + 1, :] for s in range(CH)], axis=0)
            x = jnp.concatenate([do[_ROW0[s]:_ROW1[s], :] * v[s:s + 1, :] for s in range(CH)], axis=0)
            car_dec[slot, hp] = dec
            car_a[slot, hp] = jnp.dot((qk * dec).astype(BF16), ones, preferred_element_type=F32)
            car_da[slot, hp] = jnp.dot(x.astype(BF16), ones, preferred_element_type=F32)
            qe1, _, ke0, _ = _cross_half(q, k, bc)
            qe1_bf, ke0_bf = qe1.astype(BF16), ke0.astype(BF16)
            do1_bf, v0_bf = do[HALF:, :].astype(BF16), v[:HALF, :].astype(BF16)
            car_x[slot, hp, 0:HALF] = lax.dot_general(ke0_bf, qe1_bf, NT_DIMS, preferred_element_type=F32)
            car_x[slot, hp, HALF:2 * HALF] = lax.dot_general(do1_bf, v0_bf, NT_DIMS, preferred_element_type=F32)
            car_x[slot, hp, 2 * HALF:3 * HALF] = lax.dot_general(v0_bf, do1_bf, NT_DIMS, preferred_element_type=F32)

        def advance_state(c, hp, slot):
            ebl = jnp.exp(b_s[_chunk_rows(c, CH - SUBLANES, SUBLANES), hp * REC_K:(hp + 1) * REC_K][SUBLANES - 1:, :])
            st, dst = sh_ref[hp, 0, c], dst_ref[hp]
            dst_ref[hp] = dst * ebl + car_dst[slot, hp]
            return ebl * jnp.sum(st * dst, axis=0, keepdims=True)

        def cross(c, hp, slot):
            _, _, q, k, bc, v, do = operands(c, hp)
            qe1, _, ke0, _ = _cross_half(q, k, bc)
            xs = car_x[slot, hp]
            dqe1 = jnp.dot(xs[HALF:2 * HALF].astype(BF16), ke0.astype(BF16), preferred_element_type=F32)
            dke0 = jnp.dot(xs[2 * HALF:].astype(BF16), qe1.astype(BF16), preferred_element_type=F32)
            dv1 = jnp.dot(xs[:HALF].astype(BF16), do[HALF:, :].astype(BF16), preferred_element_type=F32)
            return dqe1, dke0, dv1

        def retire(c, slot):
            dbl_state = [advance_state(c, hp, slot) for hp in range(HP)]
            yield
            crossed = [cross(c, hp, slot) for hp in range(HP)]
            for hp in range(HP):
                finish(c, hp, slot, dbl_state[hp], *crossed[hp])

        def step(c, slot):
            closing = retire(c + 1, slot)
            next(closing)
            for hp in range(HP):
                issue(c, hp, 1 - slot)
            next(closing, None)

        def trip(j, carry):
            step(nck - 2 - 2 * j, 0)
            step(nck - 3 - 2 * j, 1)
            return carry

        def finish(c, hp, slot, dbl_state, dqe1, dke0, dv1):
            rs, cs, q, k, bc, v, do = operands(c, hp)
            eb, ekd = jnp.exp(bc), jnp.exp(bc[CH - 1:CH, :] - bc)
            qe, kd = q * eb, k * ekd
            qe1, e1, ke0, e0 = _cross_half(q, k, bc)
            dqe, dkd, dv = car_r[slot, hp, 0:CH], car_r[slot, hp, CH:2 * CH], car_r[slot, hp, 2 * CH:3 * CH]
            a, da, decs = car_a[slot, hp], car_da[slot, hp], car_dec[slot, hp]
            dec = [decs[_OFF_OF[s]:_OFF_OF[s] + _ROWS_OF[s], :] for s in range(CH)]
            dbl = jnp.sum(dkd * kd, axis=0, keepdims=True) + dbl_state
            dq_acc = [jnp.zeros((SUBLANES, REC_K), F32) for _ in range(ngrp)]
            uk, uv = [], []
            for s in range(CH):
                j = s // SUBLANES
                r0 = j * SUBLANES
                ks = k[s:s + 1, :]
                for jj in range(j, _ROW1[s] // SUBLANES):
                    lo, hi = _OFF_OF[s] + (jj - j) * SUBLANES, _OFF_OF[s] + (jj - j + 1) * SUBLANES
                    a_blk, da_blk = a[lo:hi, :], da[lo:hi, :]
                    if jj == j:
                        keep = sub >= s - r0
                        a_blk, da_blk = jnp.where(keep, a_blk, 0.0), jnp.where(keep, da_blk, 0.0)
                    rows = slice(jj * SUBLANES, (jj + 1) * SUBLANES)
                    tt = da_blk * dec[s][(jj - j) * SUBLANES:(jj - j + 1) * SUBLANES, :]
                    dq_acc[jj] = dq_acc[jj] + tt * ks
                    uk.append(tt * q[rows, :])
                    uv.append(a_blk * do[rows, :])
            dk_in = jnp.dot(pick, jnp.concatenate(uk, axis=0).astype(BF16), preferred_element_type=F32)
            dv_in = jnp.dot(pick, jnp.concatenate(uv, axis=0).astype(BF16), preferred_element_type=F32)
            zero_half = jnp.zeros((HALF, REC_K), F32)
            dq_x = jnp.concatenate([zero_half, dqe1 * e1], axis=0)
            dk_x = jnp.concatenate([dke0 * e0, zero_half], axis=0)
            dv_x = jnp.concatenate([dv1, zero_half], axis=0)
            db_x = jnp.concatenate([-(dke0 * ke0), dqe1 * qe1], axis=0)
            dq_in = jnp.concatenate(dq_acc, axis=0)
            dqv_s[rs, cs] = dqe * eb + dq_in + dq_x
            dk_s[rs, cs] = dkd * ekd + dk_in + dk_x
            di_ref[rs, cs] = (dv + dv_in + dv_x).astype(BF16)
            db = dqe * qe - dkd * kd + q * dq_in - k * dk_in + db_x
            db_s[rs, cs] = db + jnp.where(rowid == CH - 1, dbl, 0.0)

        assert nck % 2 == 0
        for hp in range(HP):
            issue(nck - 1, hp, 0)
        lax.fori_loop(0, nck // 2 - 1, trip, 0)
        step(0, 0)
        for _ in retire(0, 1):
            pass
        up = _tri(False)
        sgq = _sigmoid(qp)
        dq_ref[...] = (dqv_s[...] * (sgq * (1.0 + qp * (1.0 - sgq)))).astype(BF16)
        dlb_acc = jnp.zeros((1, _HW), F32)
        for r in range(0, tb, _CUM_ROWS):
            rows = slice(r, r + _CUM_ROWS)
            dgl = _dot3(up, db_s[rows, :])
            dfg = dgl * fg_inv[rows, :] - dk_s[rows, :]
            sn = sig_neg[rows, :]
            df_ref[rows, :] = (dfg * (1.0 - lbv) * (1.0 - sn) * sn).astype(BF16)
            dlb_acc = dlb_acc + jnp.sum(dfg * sn, axis=0, keepdims=True)
        dlb_ref[...] += dlb_acc

    blk, head, lbs, gws, hist = _hgrn_specs(tb, nt, True)
    out_specs = [head, head, head, head, lbs, pl.BlockSpec((HP, 1, REC_K), lambda h, b, t: (h, 0, 0))]
    out_shape = [jax.ShapeDtypeStruct((n, 1024), BF16)] * 4 + [
        jax.ShapeDtypeStruct((1, 1024), F32), jax.ShapeDtypeStruct((REC_HEADS, 1, REC_K), F32)]
    return pl.pallas_call(
        body, name="hgrn_bwd", grid=(REC_HEADS // HP, b_loc, nt),
        in_specs=[blk(0), blk(1), blk(2), blk(3), lbs, gws, head, head, hist, head, head, head],
        out_specs=out_specs, out_shape=out_shape,
        scratch_shapes=[pltpu.VMEM((tb, _HW), F32)] * 4 + [pltpu.VMEM((HP, REC_K, REC_K), F32)] + [
            pltpu.VMEM((2, HP, 3 * CH, REC_K), F32), pltpu.VMEM((2, HP, REC_K, REC_K), F32)] + [
            pltpu.VMEM((2, HP, sum(_ROWS_OF), REC_K), F32)] * 3 + [pltpu.VMEM((2, HP, 3 * HALF, HALF), F32)],
        compiler_params=_params(("arbitrary", "arbitrary", "arbitrary"), 56),
    )(p, p, p, p, lb, gw, oraw, dg, sh, *gates)


def _postnorm_bwd_nt(dxo, y, qw, w, has_bias, name):
    n = dxo.shape[0]

    def body(dx_ref, y_ref, qw_ref, w_ref, dg_ref, dy_ref, dqw_ref, db_ref):
        @pl.when(pl.program_id(0) == 0)
        def _():
            dqw_ref[...] = jnp.zeros_like(dqw_ref)
            db_ref[...] = jnp.zeros_like(db_ref)

        yv, dxv = y_ref[...], dx_ref[...]
        r = lax.rsqrt(jnp.mean(yv * yv, axis=-1, keepdims=True) + NORM_EPS)
        u = yv * r
        du = dxv * qw_ref[...]
        dy = r * (du - u * jnp.mean(du * u, axis=-1, keepdims=True))
        dqw_ref[...] += jnp.sum(dxv * u, axis=0, keepdims=True)
        if has_bias:
            db_ref[...] += jnp.sum(dy, axis=0, keepdims=True)
        dyb = dy.astype(BF16)
        dy_ref[...] = dyb
        dg_ref[...] = lax.dot_general(dyb, w_ref[...], NT_DIMS, preferred_element_type=F32)

    rows = pl.BlockSpec((TM, D_MODEL), lambda i: (i, 0))
    const = lambda shape: pl.BlockSpec(shape, lambda i: (0, 0))
    return pl.pallas_call(
        body, name=name, grid=(n // TM,), in_specs=[rows, rows, const((1, D_MODEL)), const((D_MODEL, D_MODEL))],
        out_specs=[rows, rows, const((1, D_MODEL)), const((1, D_MODEL))],
        out_shape=[jax.ShapeDtypeStruct((n, D_MODEL), F32), jax.ShapeDtypeStruct((n, D_MODEL), BF16),
                   jax.ShapeDtypeStruct((1, D_MODEL), F32), jax.ShapeDtypeStruct((1, D_MODEL), F32)],
        compiler_params=_params(("arbitrary",), 48),
    )(dxo, y, qw, w)


def _nt_prenorm_bwd(dps, w, x, pw, dxo, has_bias, name, parts=()):
    n = x.shape[0]
    widths = [d.shape[1] for d in dps]
    m = sum(widths)
    npieces, nparts, steps = len(dps), len(parts), n // TM

    def body(*refs):
        dp_refs = refs[:npieces]
        w_ref, x_ref, pw_ref, dxo_ref = refs[npieces:npieces + 4]
        part_refs = refs[npieces + 4:npieces + 4 + nparts]
        dx_ref, dpw_ref, db_ref = refs[npieces + 4 + nparts:npieces + 7 + nparts]
        land_refs = refs[npieces + 7 + nparts:npieces + 7 + 2 * nparts]
        sems = refs[npieces + 7 + 2 * nparts:]

        @pl.when(pl.program_id(0) == 0)
        def _():
            dpw_ref[...] = jnp.zeros_like(dpw_ref)
            db_ref[...] = jnp.zeros_like(db_ref)
            if nparts:
                _scatter_start(part_refs, land_refs, sems)

        dh = jnp.zeros((TM, D_MODEL), F32)
        off = 0
        for dp_ref, wd in zip(dp_refs, widths):
            cn = _col_chunk(wd)
            for j in range(0, wd, cn):
                dpc = dp_ref[:, j:j + cn]
                if has_bias:
                    db_ref[:, off + j:off + j + cn] += jnp.sum(dpc, axis=0, keepdims=True)
                dh = dh + lax.dot_general(dpc.astype(BF16), w_ref[:, off + j:off + j + cn], NT_DIMS, preferred_element_type=F32)
            off += wd
        xv = x_ref[...]
        r = lax.rsqrt(jnp.mean(xv * xv, axis=-1, keepdims=True) + NORM_EPS)
        xn = xv * r
        dpw_ref[...] += jnp.sum(dh * xn, axis=0, keepdims=True)
        dxn = dh * pw_ref[...]
        dx_ref[...] = dxo_ref[...] + r * (dxn - xn * jnp.mean(dxn * xn, axis=-1, keepdims=True))

        if nparts:
            @pl.when(pl.program_id(0) == steps - 1)
            def _():
                _scatter_wait(part_refs, land_refs, sems)

    rows = pl.BlockSpec((TM, D_MODEL), lambda i: (i, 0))
    const = lambda shape: pl.BlockSpec(shape, lambda i: (0, 0))
    hbm = pl.BlockSpec(memory_space=pl.ANY)
    in_specs = ([pl.BlockSpec((TM, wd), lambda i: (i, 0)) for wd in widths] + [const((D_MODEL, m)), rows, const((1, D_MODEL)), rows]
                + [hbm] * nparts)
    return pl.pallas_call(
        body, name=name, grid=(steps,), in_specs=in_specs,
        out_specs=[rows, const((1, D_MODEL)), const((1, m))] + [hbm] * nparts,
        out_shape=[jax.ShapeDtypeStruct((n, D_MODEL), F32), jax.ShapeDtypeStruct((1, D_MODEL), F32),
                   jax.ShapeDtypeStruct((1, m), F32)] + _scatter_lands(parts),
        scratch_shapes=_scatter_sems(nparts) if nparts else [],
        compiler_params=_params(("arbitrary",), 56),
    )(*dps, w, x, pw, dxo, *parts)


def _matmul_tn(a, b, name):
    n, k = a.shape
    m = b.shape[1]
    tk, tm, tn = k, _col_chunk(m), 1024 if n % 1024 == 0 else n

    def body(a_ref, b_ref, o_ref):
        @pl.when(pl.program_id(2) == 0)
        def _():
            o_ref[...] = jnp.zeros_like(o_ref)

        o_ref[...] += lax.dot_general(a_ref[...], b_ref[...].astype(BF16), TN_DIMS, preferred_element_type=F32)

    return pl.pallas_call(
        body, name=name, grid=(k // tk, m // tm, n // tn),
        in_specs=[pl.BlockSpec((tn, tk), lambda i, j, l: (l, i)), pl.BlockSpec((tn, tm), lambda i, j, l: (l, j))],
        out_specs=pl.BlockSpec((tk, tm), lambda i, j, l: (i, j)),
        out_shape=jax.ShapeDtypeStruct((k, m), F32),
        compiler_params=_params(("arbitrary", "arbitrary", "arbitrary"), 48),
    )(a, b)


def _by_owner_cols(dw):
    k, m = dw.shape
    return dw.reshape(k, N_DEV, m // N_DEV).transpose(1, 0, 2)


def _own_and_bf16(part):
    return lax.dynamic_index_in_dim(part, _my_id(), 0, keepdims=False), part.astype(BF16)


def _step(x, pos_col, tgt, pre_w, post_w, wa_in, ba_in, sinks, wa_out_shard, ba_out, wr_in_shard, lb_logits, gnorm_w, wr_out_shard, b_loc, t_len):
    nb = t_len // BLK
    ct, st = _rope_tables(pos_col)
    lb = _lower_bound(lb_logits)
    p0, h0, ga_out = _norm_matmul(x, pre_w[0:1], wa_in, ba_in, "attn_in_proj", [wa_out_shard])
    wa_out = ga_out.reshape(ATTN_WIDTH, D_MODEL)
    o0, g0, gr_in, gr_out = _attn_fwd(p0, ct, st, sinks, b_loc, nb, [wr_in_shard, wr_out_shard])
    wr_in = gr_in.transpose(1, 0, 2).reshape(D_MODEL, REC_IN)
    wr_out = gr_out.reshape(1024, D_MODEL)
    y0, x1 = _outproj_postnorm(g0, wa_out, ba_out, x, post_w[0:1], None, "attn_out_proj")
    p1, h1 = _norm_matmul(x1, pre_w[1:2], wr_in, None, "rec_in_proj")
    o1, g1, sh, *gates = _hgrn_fwd(p1, lb, gnorm_w, b_loc, t_len)
    y1, dx2, loss_tile = _outproj_postnorm(g1, wr_out, None, x1, post_w[1:2], tgt, "rec_out_proj_loss")
    dg1, dy1, dpost1, _ = _postnorm_bwd_nt(dx2, y1, post_w[1:2], wr_out, False, "rec_out_bwd")
    d_wr_out = _matmul_tn(g1, dy1, "rec_w_out_grad")
    dq1, df1, di1, dz1, dlb, dgw = _hgrn_bwd(p1, lb, gnorm_w, o1, sh, dg1, gates, b_loc, t_len)
    dps1 = [dq1, df1, di1, dz1]
    dx1, dpre1, _ = _nt_prenorm_bwd(dps1, wr_in, x1, pre_w[1:2], dx2, False, "rec_in_bwd")
    d_wr_in = [_matmul_tn(h1, dpk, f"rec_w_in_grad_{k}") for k, dpk in enumerate(dps1)]
    dg0, dy0, dpost0, dba_out = _postnorm_bwd_nt(dx1, y0, post_w[0:1], wa_out, True, "attn_out_bwd")
    d_wa_out = _matmul_tn(g0, dy0, "attn_w_out_grad")
    owns, wires = zip(*[_own_and_bf16(part) for part in (
        _by_owner_cols(jnp.concatenate(d_wr_in, axis=1)), d_wr_out.reshape(N_DEV, 1024 // N_DEV, D_MODEL),
        d_wa_out.reshape(N_DEV, ATTN_WIDTH // N_DEV, D_MODEL))])
    dp0, dsink_tile, *lands = _attn_bwd(p0, ct, st, sinks, o0, dg0, b_loc, nb, list(wires))
    d_wa_in = _matmul_tn(h0, dp0, "attn_w_in_grad")
    own_a_in, wire_a_in = _own_and_bf16(_by_owner_cols(_qkvz(d_wa_in)))
    dx0, dpre0, dba_in, land_a_in = _nt_prenorm_bwd([dp0], wa_in, x, pre_w[0:1], dx1, True, "attn_in_bwd", [wire_a_in])
    small = dict(pre=jnp.concatenate([dpre0, dpre1], axis=0), post=jnp.concatenate([dpost0, dpost1], axis=0),
                 ba_in=dba_in, sinks=dsink_tile[0:1, 0:N_HEADS], ba_out=dba_out, lb=dlb, gnorm=jnp.sum(dgw, axis=0))
    return loss_tile, dx0, list(zip(lands, owns)) + [(land_a_in, own_a_in)], small


def _my_id():
    return lax.axis_index("x") * 4 + lax.axis_index("y") * 2 + lax.axis_index("c")


def _peer(k):
    x, y, c = lax.axis_index("x"), lax.axis_index("y"), lax.axis_index("c")
    return (x ^ ((k >> 2) & 1), y ^ ((k >> 1) & 1), c ^ (k & 1))


def _peer_id(k):
    return _my_id() ^ k


def _all_gather_by_chip(shard):
    def body(x_ref, out_ref, send_sems, recv_sems, local_sem):
        x, y, c = lax.axis_index("x"), lax.axis_index("y"), lax.axis_index("c")
        me, sibling = (x, y, c), (x, y, 1 - c)
        chips = [(1 - x, y), (x, 1 - y), (1 - x, 1 - y)]

        def rows(px, py, pc):
            return out_ref.at[4 * px + 2 * py + pc]

        def copy(k, block, to, src=None):
            return pltpu.make_async_remote_copy(src_ref=rows(*block) if src is None else src, dst_ref=rows(*block),
                                                send_sem=send_sems.at[k], recv_sem=recv_sems.at[k], device_id=to, device_id_type=MESH)

        mine = pltpu.make_async_copy(x_ref, rows(*me), local_sem)
        mine.start()
        first = [copy(0, me, sibling, src=x_ref)] + [copy(1 + j, me, (*chip, c), src=x_ref) for j, chip in enumerate(chips)]
        for cp in first:
            cp.start()
        passed = [copy(4 + j, (*chip, c), sibling) for j, chip in enumerate(chips)]
        for j, chip in enumerate(chips):
            copy(1 + j, (*chip, c), me).wait_recv()
            passed[j].start()
        copy(0, sibling, me).wait_recv()
        for j, chip in enumerate(chips):
            copy(4 + j, (*chip, 1 - c), me).wait_recv()
        for cp in first + passed:
            cp.wait_send()
        mine.wait()

    hbm = pl.BlockSpec(memory_space=pl.ANY)
    return pl.pallas_call(
        body, name="comm_all_gather_by_chip", in_specs=[hbm], out_specs=hbm,
        out_shape=jax.ShapeDtypeStruct((N_DEV,) + shard.shape, shard.dtype),
        scratch_shapes=[pltpu.SemaphoreType.DMA((N_DEV - 1,)), pltpu.SemaphoreType.DMA((N_DEV - 1,)), pltpu.SemaphoreType.DMA],
    )(shard)


def _gather_shapes(shards):
    return [jax.ShapeDtypeStruct((N_DEV,) + s.shape, s.dtype) for s in shards]


def _gather_sems(nsh):
    return [pltpu.SemaphoreType.DMA((nsh, N_DEV - 1)), pltpu.SemaphoreType.DMA((nsh, N_DEV - 1)), pltpu.SemaphoreType.DMA((nsh,))]


def _gather_copies(ins, outs, sems, received):
    send_sems, recv_sems, local_sems = sems
    me = _my_id()
    local = [pltpu.make_async_copy(ins[a], outs[a].at[me], local_sems.at[a]) for a in range(len(ins))]
    remote = [pltpu.make_async_remote_copy(
        src_ref=ins[a], dst_ref=outs[a].at[_peer_id(k) if received else me], send_sem=send_sems.at[a, k - 1],
        recv_sem=recv_sems.at[a, k - 1], device_id=_peer(k), device_id_type=MESH)
        for a in range(len(ins)) for k in range(1, N_DEV)]
    return local, remote


def _gather_start(ins, outs, sems):
    local, sends = _gather_copies(ins, outs, sems, False)
    for cp in local + sends:
        cp.start()


def _gather_wait(ins, outs, sems):
    local, recvs = _gather_copies(ins, outs, sems, True)
    for cp in recvs:
        cp.wait_recv()
    for cp in recvs:
        cp.wait_send()
    for cp in local:
        cp.wait()


def _scatter_lands(parts):
    return [jax.ShapeDtypeStruct((N_DEV - 1,) + p.shape[1:], p.dtype) for p in parts]


def _scatter_sems(nparts):
    return [pltpu.SemaphoreType.DMA((nparts, N_DEV - 1)), pltpu.SemaphoreType.DMA((nparts, N_DEV - 1))]


def _scatter_copies(parts, lands, sems):
    send_sems, recv_sems = sems
    return [pltpu.make_async_remote_copy(
        src_ref=parts[a].at[_peer_id(k)], dst_ref=lands[a].at[k - 1], send_sem=send_sems.at[a, k - 1],
        recv_sem=recv_sems.at[a, k - 1], device_id=_peer(k), device_id_type=MESH)
        for a in range(len(parts)) for k in range(1, N_DEV)]


def _scatter_start(parts, lands, sems):
    for cp in _scatter_copies(parts, lands, sems):
        cp.start()


def _scatter_wait(parts, lands, sems):
    copies = _scatter_copies(parts, lands, sems)
    for cp in copies:
        cp.wait_recv()
    for cp in copies:
        cp.wait_send()


def _adamw(w, g, m, v):
    m2 = ADAM_B1 * m + (1.0 - ADAM_B1) * g
    v2 = ADAM_B2 * v + (1.0 - ADAM_B2) * (g * g)
    m_hat = m2 / (1.0 - ADAM_B1 ** ADAM_STEP)
    v_hat = v2 / (1.0 - ADAM_B2 ** ADAM_STEP)
    delta = -ADAM_LR * (m_hat / (jnp.sqrt(v_hat) + ADAM_EPS) + ADAM_WD * w)
    return delta, m2, v2


def _sum_adamw_rows(land_ref, own_ref, w_ref, m_ref, v_ref, out_refs):
    r, c = own_ref.shape
    rc = 64 if r % 64 == 0 else r
    me = _my_id()
    g_ref, d_ref, m2_ref, v2_ref = out_refs

    def rows(i, carry):
        rs = pl.ds(pl.multiple_of(i * rc, rc), rc)
        g = jnp.zeros((rc, c), F32)
        for dev in range(N_DEV):
            k = dev ^ me
            g = g + jnp.where(k == 0, own_ref[rs, :], land_ref[jnp.maximum(k - 1, 0), rs, :].astype(F32))
        delta, m2, v2 = _adamw(w_ref[rs, :], g, m_ref[rs, :], v_ref[rs, :])
        g_ref[rs, :] = g
        d_ref[rs, :] = delta
        m2_ref[rs, :] = m2
        v2_ref[rs, :] = v2
        return carry

    lax.fori_loop(0, r // rc, rows, 0)


def _sum_adamw(land, own, w, m, v, name):
    r, c = own.shape

    def body(land_ref, own_ref, w_ref, m_ref, v_ref, g_ref, d_ref, m2_ref, v2_ref):
        _sum_adamw_rows(land_ref, own_ref, w_ref, m_ref, v_ref, (g_ref, d_ref, m2_ref, v2_ref))

    vmem = pl.BlockSpec(memory_space=pltpu.VMEM)
    return pl.pallas_call(
        body, name=name, in_specs=[vmem] * 5, out_specs=[vmem] * 4, out_shape=[jax.ShapeDtypeStruct((r, c), F32)] * 4,
        compiler_params=_params(None, 56),
    )(land, own, w, m, v)


_SMALL = [("pre_norm_w", 2048), ("post_norm_w", 2048), ("attn_b_in", 2304), ("attn_sinks", 16), ("attn_b_out", 1024),
          ("rec_lb_logits", 2048), ("rec_gnorm_w", 128), ("loss", 1)]
_TILE = SUBLANES * LANES


def _small_rows(size):
    return -(-size // _TILE) * SUBLANES


_SMALL_OFF = {}
_r = 0
for _name, _size in _SMALL:
    _SMALL_OFF[_name] = _r
    _r += _small_rows(_size)
_SMALL_ROWS = _r


def _pack_small(pieces):
    out = []
    for name, size in _SMALL:
        flat = pieces[name].reshape(-1).astype(F32)
        out.append(jnp.pad(flat, (0, _small_rows(size) * LANES - size)).reshape(-1, LANES))
    return jnp.concatenate(out, axis=0)


def _unpack_small(packed, shapes):
    return {name: packed[_SMALL_OFF[name]:_SMALL_OFF[name] + _small_rows(size)].reshape(-1)[:size].reshape(shapes[name])
            for name, size in _SMALL}


def _small_allreduce_adamw(gpart, w, m, v):
    lb0 = _SMALL_OFF["rec_lb_logits"]

    def body(gp_ref, w_ref, m_ref, v_ref, g_ref, d_ref, m2_ref, v2_ref, land_ref, send_sems, recv_sems):
        me = _my_id()
        sent = []
        for k in range(1, N_DEV):
            cp = pltpu.make_async_remote_copy(src_ref=gp_ref, dst_ref=land_ref.at[k - 1], send_sem=send_sems.at[k - 1],
                                              recv_sem=recv_sems.at[k - 1], device_id=_peer(k), device_id_type=MESH)
            cp.start()
            sent.append(cp)
        for cp in sent:
            cp.wait_recv()
        for cp in sent:
            cp.wait_send()
        g = jnp.zeros((_SMALL_ROWS, LANES), F32)
        for dev in range(N_DEV):
            k = dev ^ me
            g = g + jnp.where(k == 0, gp_ref[...], land_ref[jnp.maximum(k - 1, 0)])
        g_ref[...] = g
        l0, l1 = w_ref[lb0:lb0 + SUBLANES, :], w_ref[lb0 + SUBLANES:lb0 + 2 * SUBLANES, :]
        mx = jnp.maximum(l0, l1)
        e0, e1 = jnp.exp(l0 - mx), jnp.exp(l1 - mx)
        p1 = e1 / (e0 + e1)
        dl1 = (1.0 - p1) * p1 * g[lb0:lb0 + SUBLANES, :]
        g_ref[lb0:lb0 + SUBLANES, :] = -dl1
        g_ref[lb0 + SUBLANES:lb0 + 2 * SUBLANES, :] = dl1
        delta, m2, v2 = _adamw(w_ref[...], g_ref[...], m_ref[...], v_ref[...])
        d_ref[...] = delta
        m2_ref[...] = m2
        v2_ref[...] = v2

    vmem = pl.BlockSpec(memory_space=pltpu.VMEM)
    return pl.pallas_call(
        body, name="comm_small_allreduce_adamw", in_specs=[vmem] * 4, out_specs=[vmem] * 4,
        out_shape=[jax.ShapeDtypeStruct((_SMALL_ROWS, LANES), F32)] * 4,
        scratch_shapes=[pltpu.VMEM((N_DEV - 1, _SMALL_ROWS, LANES), F32), pltpu.SemaphoreType.DMA((N_DEV - 1,)),
                        pltpu.SemaphoreType.DMA((N_DEV - 1,))],
    )(gpart, w, m, v)


def _qzkv(a):
    return jnp.concatenate([a[..., :1024], a[..., 1280:], a[..., 1024:1280]], axis=-1)


def _qkvz(a):
    return jnp.concatenate([a[..., :1024], a[..., 2048:], a[..., 1024:2048]], axis=-1)


def kernel(x, positions, pre_norm_w, post_norm_w, attn_w_in, attn_b_in, attn_sinks, attn_w_out, attn_b_out, rec_w_in, rec_lb_logits, rec_gnorm_w, rec_w_out, loss_target, m_pre_norm_w, m_post_norm_w, m_attn_w_in, m_attn_b_in, m_attn_sinks, m_attn_w_out, m_attn_b_out, m_rec_w_in, m_rec_lb_logits, m_rec_gnorm_w, m_rec_w_out, v_pre_norm_w, v_post_norm_w, v_attn_w_in, v_attn_b_in, v_attn_sinks, v_attn_w_out, v_attn_b_out, v_rec_w_in, v_rec_lb_logits, v_rec_gnorm_w, v_rec_w_out):
    b_loc, t_len, _ = x.shape
    n = b_loc * t_len
    ga_in = _all_gather_by_chip(attn_w_in[0].astype(BF16))
    wa_in = _qzkv(ga_in.transpose(1, 0, 2).reshape(D_MODEL, ATTN_IN))

    loss_tile, dx, landed, small = _step(
        x.reshape(n, D_MODEL), positions.reshape(n, 1).astype(F32), loss_target.reshape(n, D_MODEL),
        pre_norm_w, post_norm_w, wa_in, _qzkv(attn_b_in), attn_sinks, attn_w_out[0].astype(BF16), attn_b_out,
        rec_w_in[0].astype(BF16), rec_lb_logits, rec_gnorm_w, rec_w_out[0].astype(BF16), b_loc, t_len)

    lift = lambda outs: tuple(a[None] for a in outs)
    (l_r_in, o_r_in), (l_r_out, o_r_out), (l_a_out, o_a_out), (l_a_in, o_a_in) = landed
    r_a_in = lift(_sum_adamw(l_a_in, o_a_in, attn_w_in[0], m_attn_w_in[0], v_attn_w_in[0], "adamw_attn_w_in"))
    r_r_in = lift(_sum_adamw(l_r_in, o_r_in, rec_w_in[0], m_rec_w_in[0], v_rec_w_in[0], "adamw_rec_w_in"))
    r_r_out = lift(_sum_adamw(l_r_out, o_r_out, rec_w_out[0], m_rec_w_out[0], v_rec_w_out[0], "adamw_rec_w_out"))
    r_a_out = lift(_sum_adamw(l_a_out, o_a_out, attn_w_out[0], m_attn_w_out[0], v_attn_w_out[0], "adamw_attn_w_out"))

    gsmall = dict(pre_norm_w=small["pre"], post_norm_w=small["post"], attn_b_in=_qkvz(small["ba_in"]), attn_sinks=small["sinks"],
                  attn_b_out=small["ba_out"], rec_lb_logits=jnp.concatenate([small["lb"], jnp.zeros_like(small["lb"])], axis=0),
                  rec_gnorm_w=small["gnorm"], loss=loss_tile[0:1, 0:1])
    nil = jnp.zeros((1, 1), F32)
    wsmall = dict(pre_norm_w=pre_norm_w, post_norm_w=post_norm_w, attn_b_in=attn_b_in, attn_sinks=attn_sinks,
                  attn_b_out=attn_b_out, rec_lb_logits=rec_lb_logits, rec_gnorm_w=rec_gnorm_w, loss=nil)
    msmall = dict(pre_norm_w=m_pre_norm_w, post_norm_w=m_post_norm_w, attn_b_in=m_attn_b_in, attn_sinks=m_attn_sinks,
                  attn_b_out=m_attn_b_out, rec_lb_logits=m_rec_lb_logits, rec_gnorm_w=m_rec_gnorm_w, loss=nil)
    vsmall = dict(pre_norm_w=v_pre_norm_w, post_norm_w=v_post_norm_w, attn_b_in=v_attn_b_in, attn_sinks=v_attn_sinks,
                  attn_b_out=v_attn_b_out, rec_lb_logits=v_rec_lb_logits, rec_gnorm_w=v_rec_gnorm_w, loss=nil)
    shapes = {k: a.shape for k, a in wsmall.items()}
    packed = _small_allreduce_adamw(_pack_small(gsmall), _pack_small(wsmall), _pack_small(msmall), _pack_small(vsmall))
    sg, sd, sm, sv = [_unpack_small(a, shapes) for a in packed]

    big = {"attn_w_in": r_a_in, "attn_w_out": r_a_out, "rec_w_in": r_r_in, "rec_w_out": r_r_out}
    order = ["pre_norm_w", "post_norm_w", "attn_w_in", "attn_b_in", "attn_sinks", "attn_w_out", "attn_b_out", "rec_w_in",
             "rec_lb_logits", "rec_gnorm_w", "rec_w_out"]
    outs = [sg["loss"][0, 0], dx.reshape(b_loc, t_len, D_MODEL)]
    for idx, small_set in enumerate((sg, sd, sm, sv)):
        outs += [big[nm][idx] if nm in big else small_set[nm] for nm in order]
    return tuple(outs)
```

```python
import numpy as np
import jax
import jax.numpy as jnp
from jax import lax
from jax.experimental import pallas as pl
from jax.experimental.pallas import tpu as pltpu

F32, BF16 = jnp.float32, jnp.bfloat16
MESH = pl.DeviceIdType.MESH
N_DEV = 8

D_MODEL = 1024
N_HEADS, HEAD_DIM, N_KV, GROUP = 16, 64, 2, 8
ATTN_WIDTH, KV_WIDTH = 1024, 128
ATTN_IN = 2 * ATTN_WIDTH + 2 * KV_WIDTH
BLK = 128
ROPE_THETA, ROPE_HALF = 500000.0, 8
REC_HEADS, REC_K = 8, 128
REC_IN = 4 * 1024
CH = 32
NORM_EPS = 1e-6
ADAM_LR, ADAM_B1, ADAM_B2, ADAM_EPS, ADAM_WD, ADAM_STEP = 0.001, 0.9, 0.999, 1e-08, 0.01, 10

LANES, SUBLANES = 128, 8
TM = 512
NT_DIMS = (((1,), (1,)), ((), ()))
TN_DIMS = (((0,), (0,)), ((), ()))
MB = 2 ** 20


def _params(sem=None, vmem_mb=48, **kw):
    return pltpu.CompilerParams(dimension_semantics=sem, vmem_limit_bytes=vmem_mb * MB, **kw)


def _col_chunk(m):
    return 768 if m % 1024 else 1024


def _sigmoid(x):
    return 1.0 / (1.0 + jnp.exp(-x))


def _split3(x):
    hi = x.astype(BF16)
    r1 = x - hi.astype(F32)
    mid = r1.astype(BF16)
    lo = (r1 - mid.astype(F32)).astype(BF16)
    return hi, mid, lo


def _dot3(l_bf, x):
    hi, mid, lo = _split3(x)
    return (jnp.dot(l_bf, hi, preferred_element_type=F32) + jnp.dot(l_bf, mid, preferred_element_type=F32)
            + jnp.dot(l_bf, lo, preferred_element_type=F32))


def _rope_tables(pos_col):
    n = pos_col.shape[0]
    lane = np.arange(LANES) % HEAD_DIM
    inv = np.float32(ROPE_THETA) ** (-(np.arange(ROPE_HALF, dtype=np.float32) * np.float32(2.0) / np.float32(2 * ROPE_HALF)))
    freq = np.where(lane < 2 * ROPE_HALF, inv[lane % ROPE_HALF], 0.0).astype(np.float32)[None, :]
    sign = np.where(lane < ROPE_HALF, -1.0, np.where(lane < 2 * ROPE_HALF, 1.0, 0.0)).astype(np.float32)[None, :]

    def body(p_ref, f_ref, s_ref, c_out, s_out):
        ang = p_ref[...] * f_ref[...]
        c_out[...] = jnp.cos(ang)
        s_out[...] = jnp.sin(ang) * s_ref[...]

    row = pl.BlockSpec((TM, 1), lambda i: (i, 0))
    vec = pl.BlockSpec((1, LANES), lambda i: (0, 0))
    out = pl.BlockSpec((TM, LANES), lambda i: (i, 0))
    return pl.pallas_call(
        body, name="rope_tables", grid=(n // TM,), in_specs=[row, vec, vec], out_specs=[out, out],
        out_shape=[jax.ShapeDtypeStruct((n, LANES), F32)] * 2, compiler_params=_params(("arbitrary",)),
    )(pos_col, jnp.asarray(freq), jnp.asarray(sign))


def _rope_apply(xv, c, s, lm):
    partner = jnp.where(lm < ROPE_HALF, pltpu.roll(xv, LANES - ROPE_HALF, 1), pltpu.roll(xv, ROPE_HALF, 1))
    return xv * c + partner * s


def _rope_bwd(dy, c, s, lm):
    t = dy * s
    partner = jnp.where(lm < ROPE_HALF, pltpu.roll(t, LANES - ROPE_HALF, 1),
                        jnp.where(lm < 2 * ROPE_HALF, pltpu.roll(t, ROPE_HALF, 1), 0.0))
    return dy * c + partner


def _lower_bound(lb_logits):
    def body(l_ref, o_ref):
        l0, l1 = l_ref[0:1, :], l_ref[1:2, :]
        m = jnp.maximum(l0, l1)
        e0, e1 = jnp.exp(l0 - m), jnp.exp(l1 - m)
        o_ref[...] = e1 / (e0 + e1)

    return pl.pallas_call(body, name="lower_bound", out_shape=jax.ShapeDtypeStruct((1, lb_logits.shape[1]), F32))(lb_logits)


def _norm_matmul(x, pw, w, bias, name, shards=()):
    n, m = x.shape[0], w.shape[1]
    cn = _col_chunk(m)
    has_bias = bias is not None
    nsh, steps = len(shards), n // TM

    def body(*refs):
        refs = list(refs)
        x_ref, pw_ref, w_ref = refs[:3]
        b_ref = refs[3] if has_bias else None
        refs = refs[4 if has_bias else 3:]
        sh_in, (p_ref, h_ref), sh_out, sems = refs[:nsh], refs[nsh:nsh + 2], refs[nsh + 2:2 * nsh + 2], refs[2 * nsh + 2:]
        if nsh:
            @pl.when(pl.program_id(0) == 0)
            def _():
                _gather_start(sh_in, sh_out, sems)

        xv = x_ref[...]
        r = lax.rsqrt(jnp.mean(xv * xv, axis=-1, keepdims=True) + NORM_EPS)
        h = ((xv * r) * pw_ref[...]).astype(BF16)
        h_ref[...] = h
        for j in range(0, m, cn):
            acc = jnp.dot(h, w_ref[:, j:j + cn], preferred_element_type=F32)
            if has_bias:
                acc = acc + b_ref[:, j:j + cn]
            p_ref[:, j:j + cn] = acc

        if nsh:
            @pl.when(pl.program_id(0) == steps - 1)
            def _():
                _gather_wait(sh_in, sh_out, sems)

    rows = pl.BlockSpec((TM, D_MODEL), lambda i: (i, 0))
    const = lambda shape: pl.BlockSpec(shape, lambda i: (0, 0))
    hbm = pl.BlockSpec(memory_space=pl.ANY)
    in_specs = [rows, const((1, D_MODEL)), const((D_MODEL, m))] + ([const((1, m))] if has_bias else []) + [hbm] * nsh
    args = (x, pw, w) + ((bias,) if has_bias else ()) + tuple(shards)
    return pl.pallas_call(
        body, name=name, grid=(steps,), in_specs=in_specs,
        out_specs=[pl.BlockSpec((TM, m), lambda i: (i, 0)), rows] + [hbm] * nsh,
        out_shape=[jax.ShapeDtypeStruct((n, m), F32), jax.ShapeDtypeStruct((n, D_MODEL), BF16)] + _gather_shapes(shards),
        scratch_shapes=_gather_sems(nsh) if nsh else [],
        compiler_params=_params(("arbitrary",), 56),
    )(*args)


def _outproj_postnorm(g, w, bias, xres, qw, name):
    n = g.shape[0]

    def body(g_ref, w_ref, b_ref, x_ref, qw_ref, y_ref, o_ref):
        y = jnp.dot(g_ref[...], w_ref[...], preferred_element_type=F32) + b_ref[...]
        y_ref[...] = y
        r = lax.rsqrt(jnp.mean(y * y, axis=-1, keepdims=True) + NORM_EPS)
        o_ref[...] = x_ref[...] + (y * r) * qw_ref[...]

    rows = pl.BlockSpec((TM, D_MODEL), lambda i: (i, 0))
    const = lambda shape: pl.BlockSpec(shape, lambda i: (0, 0))
    return pl.pallas_call(
        body, name=name, grid=(n // TM,),
        in_specs=[rows, const((D_MODEL, D_MODEL)), const((1, D_MODEL)), rows, const((1, D_MODEL))],
        out_specs=[rows, rows], out_shape=[jax.ShapeDtypeStruct((n, D_MODEL), F32)] * 2,
        compiler_params=_params(("arbitrary",), 48),
    )(g, w, bias, xres, qw)


def _outproj_loss_bwd(g, w, xres, qw, tgt, name):
    n = g.shape[0]
    steps = n // TM

    def body(g_ref, w_ref, x_ref, qw_ref, t_ref, dx_ref, dg_ref, dy_ref, dqw_ref, loss_ref, acc_ref):
        i = pl.program_id(0)

        @pl.when(i == 0)
        def _():
            acc_ref[...] = jnp.zeros_like(acc_ref)
            dqw_ref[...] = jnp.zeros_like(dqw_ref)

        y = jnp.dot(g_ref[...], w_ref[...], preferred_element_type=F32)
        r = lax.rsqrt(jnp.mean(y * y, axis=-1, keepdims=True) + NORM_EPS)
        u = y * r
        e = (x_ref[...] + u * qw_ref[...]) - t_ref[...]
        dxn = e * (1.0 / D_MODEL)
        dx_ref[...] = dxn
        acc_ref[...] += jnp.sum(e * e, axis=0, keepdims=True)
        du = dxn * qw_ref[...]
        dy = (r * (du - u * jnp.mean(du * u, axis=-1, keepdims=True))).astype(BF16)
        dqw_ref[...] += jnp.sum(dxn * u, axis=0, keepdims=True)
        dy_ref[...] = dy
        dg_ref[...] = lax.dot_general(dy, w_ref[...], NT_DIMS, preferred_element_type=F32)

        @pl.when(i == steps - 1)
        def _():
            loss_ref[...] = jnp.full(loss_ref.shape, jnp.sum(acc_ref[...]) * (0.5 / D_MODEL), F32)

    rows = pl.BlockSpec((TM, D_MODEL), lambda i: (i, 0))
    const = lambda shape: pl.BlockSpec(shape, lambda i: (0, 0))
    return pl.pallas_call(
        body, name=name, grid=(steps,),
        in_specs=[rows, const((D_MODEL, D_MODEL)), rows, const((1, D_MODEL)), rows],
        out_specs=[rows, rows, rows, const((1, D_MODEL)), const((SUBLANES, LANES))],
        out_shape=[jax.ShapeDtypeStruct((n, D_MODEL), F32), jax.ShapeDtypeStruct((n, D_MODEL), F32),
                   jax.ShapeDtypeStruct((n, D_MODEL), BF16), jax.ShapeDtypeStruct((1, D_MODEL), F32),
                   jax.ShapeDtypeStruct((SUBLANES, LANES), F32)],
        scratch_shapes=[pltpu.VMEM((1, D_MODEL), F32)], compiler_params=_params(("arbitrary",), 48),
    )(g, w, xres, qw, tgt)


_QCOL, _ZCOL, _KCOL, _VCOL = 0, 1024, 2048, 2176


def _head_stack(chunks, heads, lt64):
    return jnp.concatenate([jnp.where(lt64 if n % 2 == 0 else ~lt64, chunks[n // 2], 0.0) for n in heads], axis=0)


def _dup_half(x, h, lt64):
    r = pltpu.roll(x, HEAD_DIM, 1)
    return jnp.where(lt64, x, r) if h == 0 else jnp.where(lt64, r, x)


def _pair_chunk(xt, c2):
    a, b = 2 * c2, 2 * c2 + 1
    return jnp.concatenate([xt[:HEAD_DIM, a * BLK:(a + 1) * BLK], xt[HEAD_DIM:, b * BLK:(b + 1) * BLK]], axis=0).T


def _attn_mask_t(i):
    key = lax.broadcasted_iota(jnp.int32, (2 * BLK, BLK), 0)
    qry = lax.broadcasted_iota(jnp.int32, (2 * BLK, BLK), 1)
    valid = (key > qry) & (key <= qry + BLK) & ((key >= BLK) | (i > 0))
    return jnp.tile(jnp.where(valid, 0.0, -1e30), (1, GROUP))


def _attn_probs_t(s, heads, sink_ref, mask):
    s = s + mask
    head = lax.broadcasted_iota(jnp.int32, (1, len(heads) * BLK), 1) >> 7
    sk = jnp.zeros((1, len(heads) * BLK), F32)
    for j, n in enumerate(heads):
        sk = jnp.where(head == j, sink_ref[0, n], sk)
    m = jnp.maximum(jnp.max(s, axis=0, keepdims=True), sk)
    p = jnp.exp(s - m)
    esk = jnp.exp(sk - m)
    inv = 1.0 / (jnp.sum(p, axis=0, keepdims=True) + esk)
    return p * inv, esk * inv


def _attn_fwd(p, ct, st, sinks, b_loc, nb, shards):
    n = p.shape[0]
    nsh = len(shards)

    def body(sink_ref, q_ref, z_ref, kc_ref, kp_ref, vc_ref, vp_ref, cc_ref, sc_ref, cp_ref, sp_ref, *rest):
        sh_in, (o_ref, g_ref), sh_out, sems = rest[:nsh], rest[nsh:nsh + 2], rest[nsh + 2:2 * nsh + 2], rest[2 * nsh + 2:]
        b, i = pl.program_id(0), pl.program_id(1)

        @pl.when((b == 0) & (i == 0))
        def _():
            _gather_start(sh_in, sh_out, sems)

        lane = lax.broadcasted_iota(jnp.int32, (BLK, LANES), 1)
        lm = lane & (HEAD_DIM - 1)
        cc, sc = cc_ref[...], sc_ref[...]
        kcat = jnp.concatenate([_rope_apply(kp_ref[...], cp_ref[...], sp_ref[...], lm),
                                _rope_apply(kc_ref[...], cc, sc, lm)], axis=0)
        vcat = jnp.concatenate([vp_ref[...], vc_ref[...]], axis=0)
        qr = [_rope_apply(q_ref[:, c * LANES:(c + 1) * LANES], cc, sc, lm) * (HEAD_DIM ** -0.5) for c in range(8)]
        valid = _attn_mask_t(i)
        lt64, lt64k = lane < HEAD_DIM, lax.broadcasted_iota(jnp.int32, (2 * BLK, LANES), 1) < HEAD_DIM
        def kv_head(h):
            heads = list(range(h * GROUP, (h + 1) * GROUP))
            kext, vext = _dup_half(kcat, h, lt64k).astype(BF16), _dup_half(vcat, h, lt64k).astype(BF16)
            qst = _head_stack(qr, heads, lt64).astype(BF16)
            s = lax.dot_general(kext, qst, NT_DIMS, preferred_element_type=F32)
            yield
            pn, _ = _attn_probs_t(s, heads, sink_ref, valid)
            ot = lax.dot_general(vext, pn.astype(BF16), TN_DIMS, preferred_element_type=F32)
            yield
            for c2 in range(GROUP // 2):
                oc = _pair_chunk(ot, c2)
                cols = slice((4 * h + c2) * LANES, (4 * h + c2 + 1) * LANES)
                zc = z_ref[:, cols]
                o_ref[:, cols] = oc
                g_ref[:, cols] = (oc * (zc * _sigmoid(zc))).astype(BF16)

        _in_stages([kv_head(h) for h in range(N_KV)])

        @pl.when((b == b_loc - 1) & (i == nb - 1))
        def _():
            _gather_wait(sh_in, sh_out, sems)

    cur = lambda b, i: b * nb + i
    prev = lambda b, i: b * nb + jnp.maximum(i - 1, 0)
    wide = lambda cb: pl.BlockSpec((BLK, ATTN_WIDTH), lambda b, i: (cur(b, i), cb))
    kv = lambda rowf, cb: pl.BlockSpec((BLK, LANES), lambda b, i: (rowf(b, i), cb))
    hbm = pl.BlockSpec(memory_space=pl.ANY)
    in_specs = [pl.BlockSpec(memory_space=pltpu.SMEM), wide(0), wide(1),
                kv(cur, _KCOL // LANES), kv(prev, _KCOL // LANES), kv(cur, _VCOL // LANES), kv(prev, _VCOL // LANES),
                kv(cur, 0), kv(cur, 0), kv(prev, 0), kv(prev, 0)] + [hbm] * nsh
    return pl.pallas_call(
        body, name="attn_fwd", grid=(b_loc, nb), in_specs=in_specs, out_specs=[wide(0), wide(0)] + [hbm] * nsh,
        out_shape=[jax.ShapeDtypeStruct((n, ATTN_WIDTH), F32), jax.ShapeDtypeStruct((n, ATTN_WIDTH), BF16)] + _gather_shapes(shards),
        scratch_shapes=_gather_sems(nsh), compiler_params=_params(("arbitrary", "arbitrary"), 48),
    )(sinks, p, p, p, p, p, p, ct, st, ct, st, *shards)


def _attn_bwd(p, ct, st, sinks, o, dg, b_loc, nb, parts):
    n = p.shape[0]
    nparts = len(parts)

    def body(sink_ref, q_ref, z_ref, kc_ref, kp_ref, vc_ref, vp_ref, cc_ref, sc_ref, cp_ref, sp_ref, o_ref, dg_ref, *rest):
        part_refs, (dp_ref, ds_ref), land_refs = rest[:nparts], rest[nparts:nparts + 2], rest[nparts + 2:2 * nparts + 2]
        dq_s, dz_s, dk_s, dv_s = rest[2 * nparts + 2:2 * nparts + 6]
        sems = rest[2 * nparts + 6:]
        b, i = pl.program_id(0), pl.program_id(1)

        @pl.when((b == 0) & (i == 0))
        def _():
            _scatter_start(part_refs, land_refs, sems)

        @pl.when((b == b_loc - 1) & (i == nb))
        def _():
            _scatter_wait(part_refs, land_refs, sems)

        lane = lax.broadcasted_iota(jnp.int32, (BLK, LANES), 1)
        lm = lane & (HEAD_DIM - 1)

        @pl.when((b == 0) & (i == 0))
        def _():
            ds_ref[...] = jnp.zeros_like(ds_ref)

        @pl.when(i < nb)
        def _compute():
            cc, sc = cc_ref[...], sc_ref[...]
            kcat = jnp.concatenate([_rope_apply(kp_ref[...], cp_ref[...], sp_ref[...], lm),
                                    _rope_apply(kc_ref[...], cc, sc, lm)], axis=0)
            vcat = jnp.concatenate([vp_ref[...], vc_ref[...]], axis=0)
            qr = [_rope_apply(q_ref[:, c * LANES:(c + 1) * LANES], cc, sc, lm) * (HEAD_DIM ** -0.5) for c in range(8)]
            valid = _attn_mask_t(i)
            lt64, lt64k = lane < HEAD_DIM, lax.broadcasted_iota(jnp.int32, (2 * BLK, LANES), 1) < HEAD_DIM
            do_chunks, doo_chunks, dz_chunks = [], [], []
            for c in range(8):
                cols = slice(c * LANES, (c + 1) * LANES)
                zc, oc, dgc = z_ref[:, cols], o_ref[:, cols], dg_ref[:, cols]
                sg = _sigmoid(zc)
                do_chunks.append(dgc * (zc * sg))
                dz_chunks.append(dgc * oc * (sg * (1.0 + zc * (1.0 - sg))))
                doo_chunks.append(do_chunks[c] * oc)
            dq_chunks = [None] * 8
            dk_h, dv_h, ds_parts = [None] * N_KV, [None] * N_KV, [None] * N_KV
            tile_lane = lax.broadcasted_iota(jnp.int32, (SUBLANES, LANES), 1)
            tile_row = lax.broadcasted_iota(jnp.int32, (SUBLANES, LANES), 0)
            ones8 = jnp.ones((SUBLANES, LANES), BF16)

            def kv_head(h):
                heads = list(range(h * GROUP, (h + 1) * GROUP))
                kext = _dup_half(kcat, h, lt64k)
                kext_bf, kext_t = kext.astype(BF16), kext.T.astype(BF16)
                vext = _dup_half(vcat, h, lt64k).astype(BF16)
                qst = _head_stack(qr, heads, lt64).astype(BF16)
                pn, psink = _attn_probs_t(lax.dot_general(kext_bf, qst, NT_DIMS, preferred_element_type=F32), heads, sink_ref, valid)
                do_bf = _head_stack(do_chunks, heads, lt64).astype(BF16)
                delta = sum(lax.dot_general(ones8, part, NT_DIMS, preferred_element_type=F32)
                            for part in _split3(_head_stack(doo_chunks, heads, lt64)))[0:1, :]
                dpt = lax.dot_general(vext, do_bf, NT_DIMS, preferred_element_type=F32)
                dst = (pn * (dpt - delta)).astype(BF16)
                sink_term = psink * delta
                ds_acc = jnp.zeros((SUBLANES, LANES), F32)
                for j, n in enumerate(heads):
                    val = -jnp.sum(sink_term[:, j * BLK:(j + 1) * BLK])
                    ds_acc = ds_acc + jnp.where((tile_lane == n) & (tile_row == 0), val, 0.0)
                ds_parts[h] = ds_acc
                dqt = jnp.dot(kext_t, dst, preferred_element_type=F32) * (HEAD_DIM ** -0.5)
                dk_ext = jnp.dot(dst, qst, preferred_element_type=F32)
                dv_ext = jnp.dot(pn.astype(BF16), do_bf, preferred_element_type=F32)
                dk_h[h] = dk_ext + pltpu.roll(dk_ext, HEAD_DIM, 1)
                dv_h[h] = dv_ext + pltpu.roll(dv_ext, HEAD_DIM, 1)
                for c2 in range(GROUP // 2):
                    dq_chunks[4 * h + c2] = _rope_bwd(_pair_chunk(dqt, c2), cc, sc, lm)

            for h in range(N_KV):
                kv_head(h)
            ds_ref[...] += ds_parts[0] + ds_parts[1]
            dk_full = jnp.where(lt64k, dk_h[0], dk_h[1])
            dv_full = jnp.where(lt64k, dv_h[0], dv_h[1])

            @pl.when(i >= 1)
            def _emit():
                dp_ref[:, _QCOL:_QCOL + ATTN_WIDTH] = dq_s[...]
                dp_ref[:, _ZCOL:_ZCOL + ATTN_WIDTH] = dz_s[...]
                dp_ref[:, _KCOL:_KCOL + KV_WIDTH] = _rope_bwd(dk_s[...] + dk_full[:BLK], cp_ref[...], sp_ref[...], lm)
                dp_ref[:, _VCOL:_VCOL + KV_WIDTH] = dv_s[...] + dv_full[:BLK]

            for c in range(8):
                dq_s[:, c * LANES:(c + 1) * LANES] = dq_chunks[c]
                dz_s[:, c * LANES:(c + 1) * LANES] = dz_chunks[c]
            dk_s[...] = dk_full[BLK:]
            dv_s[...] = dv_full[BLK:]

        @pl.when(i == nb)
        def _final():
            dp_ref[:, _QCOL:_QCOL + ATTN_WIDTH] = dq_s[...]
            dp_ref[:, _ZCOL:_ZCOL + ATTN_WIDTH] = dz_s[...]
            dp_ref[:, _KCOL:_KCOL + KV_WIDTH] = _rope_bwd(dk_s[...], cc_ref[...], sc_ref[...], lm)
            dp_ref[:, _VCOL:_VCOL + KV_WIDTH] = dv_s[...]

    cur = lambda b, i: b * nb + jnp.minimum(i, nb - 1)
    prev = lambda b, i: b * nb + jnp.maximum(jnp.minimum(i, nb - 1) - 1, 0)
    emit = lambda b, i: b * nb + jnp.maximum(i - 1, 0)
    hbm = pl.BlockSpec(memory_space=pl.ANY)
    wide = lambda cb: pl.BlockSpec((BLK, ATTN_WIDTH), lambda b, i: (cur(b, i), cb))
    kv = lambda rowf, cb: pl.BlockSpec((BLK, LANES), lambda b, i: (rowf(b, i), cb))
    in_specs = [pl.BlockSpec(memory_space=pltpu.SMEM), wide(0), wide(1),
                kv(cur, _KCOL // LANES), kv(prev, _KCOL // LANES), kv(cur, _VCOL // LANES), kv(prev, _VCOL // LANES),
                kv(cur, 0), kv(cur, 0), kv(prev, 0), kv(prev, 0), wide(0), wide(0)] + [hbm] * nparts
    out_specs = [pl.BlockSpec((BLK, ATTN_IN), lambda b, i: (emit(b, i), 0)),
                 pl.BlockSpec((SUBLANES, LANES), lambda b, i: (0, 0))] + [hbm] * nparts
    return pl.pallas_call(
        body, name="attn_bwd", grid=(b_loc, nb + 1), in_specs=in_specs, out_specs=out_specs,
        out_shape=[jax.ShapeDtypeStruct((n, ATTN_IN), F32), jax.ShapeDtypeStruct((SUBLANES, LANES), F32)] + _scatter_lands(parts),
        scratch_shapes=[pltpu.VMEM((BLK, ATTN_WIDTH), F32), pltpu.VMEM((BLK, ATTN_WIDTH), F32),
                        pltpu.VMEM((BLK, KV_WIDTH), F32), pltpu.VMEM((BLK, KV_WIDTH), F32)] + _scatter_sems(nparts),
        compiler_params=_params(("arbitrary", "arbitrary"), 48),
    )(sinks, p, p, p, p, p, p, ct, st, ct, st, o, dg, *parts)


_CUM_ROWS = 256
HALF = CH // 2
_ROW0 = [SUBLANES * (s // SUBLANES) for s in range(CH)]
_ROW1 = [HALF * (s // HALF + 1) for s in range(CH)]
_ROWS_OF = [_ROW1[s] - _ROW0[s] for s in range(CH)]
_OFF_OF = [sum(_ROWS_OF[:s]) for s in range(CH)]


def _tri(lower):
    r = lax.broadcasted_iota(jnp.int32, (_CUM_ROWS, _CUM_ROWS), 0)
    c = lax.broadcasted_iota(jnp.int32, (_CUM_ROWS, _CUM_ROWS), 1)
    same = (r ^ c) < CH
    return (same & ((c <= r) if lower else (c >= r))).astype(BF16)


def _gates(qp, fp, lb):
    e = jnp.exp(-jnp.abs(fp))
    log_sig = jnp.minimum(fp, 0.0) - jnp.log(1.0 + e)
    a = jnp.log(lb)
    c = jnp.log(1.0 - lb) + log_sig
    g = jnp.maximum(a, c) + jnp.log(1.0 + jnp.exp(-jnp.abs(a - c)))
    sig_neg = jnp.where(fp >= 0, e, 1.0) / (1.0 + e)
    return qp * _sigmoid(qp), g, (1.0 - lb) * sig_neg, sig_neg


def _pair_rows(bc, s):
    return jnp.exp(jnp.minimum(bc[_ROW0[s]:_ROW1[s], :] - bc[s:s + 1, :], 0.0))


def _cross_half(q, k, bc):
    r = bc[HALF - 1:HALF, :]
    e1, e0 = jnp.exp(bc[HALF:, :] - r), jnp.exp(r - bc[:HALF, :])
    return q[HALF:, :] * e1, e1, k[:HALF, :] * e0, e0


HP = 8
REC_TB = 256
_HW = HP * REC_K


def _hgrn_specs(tb, nt, reverse):
    tmap = (lambda t: nt - 1 - t) if reverse else (lambda t: t)
    groups = REC_HEADS // HP
    blk = lambda cb: pl.BlockSpec((tb, _HW), lambda h, b, t: (b * nt + tmap(t), cb * groups + h))
    head = pl.BlockSpec((tb, _HW), lambda h, b, t: (b * nt + tmap(t), h))
    lbs = pl.BlockSpec((1, _HW), lambda h, b, t: (0, h))
    gws = pl.BlockSpec((1, REC_K), lambda h, b, t: (0, 0))
    hist = pl.BlockSpec((HP, 1, tb // CH, REC_K, REC_K), lambda h, b, t: (h, b, tmap(t), 0, 0))
    return blk, head, lbs, gws, hist


def _chunk_rows(c, first=0, size=CH):
    start = c * CH + first
    return pl.ds(start if isinstance(start, int) else pl.multiple_of(start, CH if first % CH == 0 else SUBLANES), size)


def _in_stages(heads):
    live = list(heads)
    while live:
        live = [g for g in live if next(g, live) is not live]


def _cumsum_chunks(tri, x, out_ref, tb):
    for r in range(0, tb, _CUM_ROWS):
        out_ref[r:r + _CUM_ROWS, :] = _dot3(tri, x[r:r + _CUM_ROWS, :])


def _hgrn_fwd(p, lb, gw, b_loc, t_len):
    n = p.shape[0]
    tb = min(REC_TB, t_len)
    nt, nck = t_len // tb, tb // CH

    def body(qp_ref, fp_ref, i_ref, z_ref, lb_ref, gw_ref, oraw_ref, g_ref, sh_ref, q_s, k_s, b_s, o_s, st_ref,
             car_o, car_a, car_s, car_st):
        @pl.when(pl.program_id(2) == 0)
        def _():
            st_ref[...] = jnp.zeros_like(st_ref)

        qv, g, kk, _ = _gates(qp_ref[...], fp_ref[...], lb_ref[...])
        q_s[...] = qv
        k_s[...] = kk
        _cumsum_chunks(_tri(True), g, b_s, tb)
        ones = jnp.ones((REC_K, REC_K), BF16)
        sub = lax.broadcasted_iota(jnp.int32, (SUBLANES, REC_K), 0)

        rows_of = _chunk_rows

        def issue(c, hp):
            rs, cs = rows_of(c), slice(hp * REC_K, (hp + 1) * REC_K)
            q, k, bc, v = q_s[rs, cs], k_s[rs, cs], b_s[rs, cs], i_ref[rs, cs]
            st = st_ref[hp]
            sh_ref[hp, 0, c] = st
            o = lax.dot_general((q * jnp.exp(bc)).astype(BF16), st.astype(BF16), NT_DIMS, preferred_element_type=F32)
            w = jnp.concatenate([q[_ROW0[s]:_ROW1[s], :] * _pair_rows(bc, s) * k[s:s + 1, :] for s in range(CH)], axis=0)
            a = jnp.dot(w.astype(BF16), ones, preferred_element_type=F32)
            qe1, _, ke0, _ = _cross_half(q, k, bc)
            s10 = lax.dot_general(qe1.astype(BF16), ke0.astype(BF16), NT_DIMS, preferred_element_type=F32)
            kd = k * jnp.exp(bc[CH - 1:CH, :] - bc)
            st_new = lax.dot_general(v.astype(BF16), kd.astype(BF16), TN_DIMS, preferred_element_type=F32)
            return o, a, s10, st_new

        def advance_state(c, hp, st_new):
            bl = b_s[_chunk_rows(c, CH - SUBLANES, SUBLANES), hp * REC_K:(hp + 1) * REC_K][SUBLANES - 1:, :]
            st_ref[hp] = st_ref[hp] * jnp.exp(bl) + st_new

        def cross(c, hp, s10):
            v0 = i_ref[_chunk_rows(c, 0, HALF), hp * REC_K:(hp + 1) * REC_K]
            return jnp.dot(s10.astype(BF16), v0.astype(BF16), preferred_element_type=F32)

        def finish(c, hp, o, a, o_cross):
            rs, cs = rows_of(c), slice(hp * REC_K, (hp + 1) * REC_K)
            v = i_ref[rs, cs]
            acc = [jnp.zeros((SUBLANES, REC_K), F32) for _ in range(CH // SUBLANES)]
            for s in range(CH):
                j = s // SUBLANES
                vs = v[s:s + 1, :]
                for jj in range(j, _ROW1[s] // SUBLANES):
                    blk = a[_OFF_OF[s] + (jj - j) * SUBLANES:_OFF_OF[s] + (jj - j + 1) * SUBLANES, :]
                    if jj == j:
                        blk = jnp.where(sub >= s - j * SUBLANES, blk, 0.0)
                    acc[jj] = acc[jj] + blk * vs
            o_s[rs, cs] = o + jnp.concatenate(acc, axis=0) + jnp.concatenate([jnp.zeros((HALF, REC_K), F32), o_cross], axis=0)

        def park(slot, results):
            for hp, (o, a, s10, st_new) in enumerate(results):
                car_o[slot, hp], car_a[slot, hp], car_s[slot, hp], car_st[slot, hp] = o, a, s10, st_new

        def retire(c, slot):
            for hp in range(HP):
                advance_state(c, hp, car_st[slot, hp])
            yield
            crosses = [cross(c, hp, car_s[slot, hp]) for hp in range(HP)]
            for hp in range(HP):
                finish(c, hp, car_o[slot, hp], car_a[slot, hp], crosses[hp])

        def step(c, slot):
            closing = retire(c - 1, slot)
            next(closing)
            park(1 - slot, [issue(c, hp) for hp in range(HP)])
            next(closing, None)

        def trip(j, carry):
            step(2 * j + 1, 0)
            step(2 * j + 2, 1)
            return carry

        assert nck % 2 == 0
        park(0, [issue(0, hp) for hp in range(HP)])
        lax.fori_loop(0, nck // 2 - 1, trip, 0)
        step(nck - 1, 0)
        for _ in retire(nck - 1, 1):
            pass
        oraw_ref[...] = o_s[...]
        for hp in range(HP):
            cs = slice(hp * REC_K, (hp + 1) * REC_K)
            o, zc = o_s[:, cs], z_ref[:, cs]
            on = (o * lax.rsqrt(jnp.mean(o * o, axis=-1, keepdims=True) + NORM_EPS)) * gw_ref[...]
            g_ref[:, cs] = (on * (zc * _sigmoid(zc))).astype(BF16)

    blk, head, lbs, gws, hist = _hgrn_specs(tb, nt, False)
    return pl.pallas_call(
        body, name="hgrn_fwd", grid=(REC_HEADS // HP, b_loc, nt),
        in_specs=[blk(0), blk(1), blk(2), blk(3), lbs, gws], out_specs=[head, head, hist],
        out_shape=[jax.ShapeDtypeStruct((n, 1024), F32), jax.ShapeDtypeStruct((n, 1024), BF16),
                   jax.ShapeDtypeStruct((REC_HEADS, b_loc, t_len // CH, REC_K, REC_K), F32)],
        scratch_shapes=[pltpu.VMEM((tb, _HW), F32)] * 4 + [pltpu.VMEM((HP, REC_K, REC_K), F32)] + [
            pltpu.VMEM((2, HP, CH, REC_K), F32), pltpu.VMEM((2, HP, sum(_ROWS_OF), REC_K), F32),
            pltpu.VMEM((2, HP, HALF, HALF), F32), pltpu.VMEM((2, HP, REC_K, REC_K), F32)],
        compiler_params=_params(("arbitrary", "arbitrary", "arbitrary"), 48),
    )(p, p, p, p, lb, gw)


def _hgrn_bwd(p, lb, gw, oraw, sh, dg, b_loc, t_len):
    n = p.shape[0]
    tb = min(REC_TB, t_len)
    nt, nck = t_len // tb, tb // CH
    assert HP == REC_HEADS

    def body(qp_ref, fp_ref, i_ref, z_ref, lb_ref, gw_ref, oraw_ref, dg_ref, sh_ref,
             dp_ref, dlb_ref, dgw_ref,
             q_s, k_s, b_s, do_s, dqv_s, dk_s, db_s, dst_ref, car_r, car_dst, car_dec, car_a, car_da, car_x):
        dq_ref, df_ref, di_ref, dz_ref = (dp_ref.at[:, part * 1024:(part + 1) * 1024] for part in range(4))
        b, t = pl.program_id(1), pl.program_id(2)

        @pl.when(t == 0)
        def _():
            dst_ref[...] = jnp.zeros_like(dst_ref)

        @pl.when((b == 0) & (t == 0))
        def _():
            dlb_ref[...] = jnp.zeros_like(dlb_ref)
            dgw_ref[...] = jnp.zeros_like(dgw_ref)

        lbv, qp, fp = lb_ref[...], qp_ref[...], fp_ref[...]
        qv, g, kk, sig_neg = _gates(qp, fp, lbv)
        q_s[...] = qv
        k_s[...] = kk
        _cumsum_chunks(_tri(True), g, b_s, tb)
        gwv = gw_ref[...]
        for hp in range(HP):
            cs = slice(hp * REC_K, (hp + 1) * REC_K)
            o, zc, dgv = oraw_ref[:, cs], z_ref[:, cs], dg_ref[:, cs]
            rn = lax.rsqrt(jnp.mean(o * o, axis=-1, keepdims=True) + NORM_EPS)
            on = o * rn
            sgz = _sigmoid(zc)
            dz_ref[:, cs] = (dgv * (on * gwv) * (sgz * (1.0 + zc * (1.0 - sgz)))).astype(BF16)
            dpre = dgv * (zc * sgz)
            dgw_ref[hp] += jnp.sum(dpre * on, axis=0, keepdims=True)
            don = dpre * gwv
            do_s[:, cs] = rn * (don - on * jnp.mean(don * on, axis=-1, keepdims=True))

        ones = jnp.ones((REC_K, REC_K), BF16)
        sub = lax.broadcasted_iota(jnp.int32, (SUBLANES, REC_K), 0)
        rowid = lax.broadcasted_iota(jnp.int32, (CH, REC_K), 0)
        ngrp = CH // SUBLANES
        piece_row = lax.broadcasted_iota(jnp.int32, (1, sum(_ROWS_OF)), 1)
        key_of = jnp.zeros((1, sum(_ROWS_OF)), jnp.int32)
        for s in range(1, CH):
            key_of = jnp.where(piece_row >= _OFF_OF[s], s, key_of)
        pick = (key_of == lax.broadcasted_iota(jnp.int32, (CH, sum(_ROWS_OF)), 0)).astype(BF16)

        def operands(c, hp):
            rs, cs = _chunk_rows(c), slice(hp * REC_K, (hp + 1) * REC_K)
            return rs, cs, q_s[rs, cs], k_s[rs, cs], b_s[rs, cs], i_ref[rs, cs], do_s[rs, cs]

        def issue(c, hp, slot):
            _, _, q, k, bc, v, do = operands(c, hp)
            st, dst = sh_ref[hp, 0, c], dst_ref[hp]
            qe, kd = q * jnp.exp(bc), k * jnp.exp(bc[CH - 1:CH, :] - bc)
            do_bf, dst_bf = do.astype(BF16), dst.astype(BF16)
            car_r[slot, hp, 0:CH] = jnp.dot(do_bf, st.astype(BF16), preferred_element_type=F32)
            car_r[slot, hp, CH:2 * CH] = jnp.dot(v.astype(BF16), dst_bf, preferred_element_type=F32)
            car_r[slot, hp, 2 * CH:3 * CH] = lax.dot_general(kd.astype(BF16), dst_bf, NT_DIMS, preferred_element_type=F32)
            car_dst[slot, hp] = lax.dot_general(do_bf, qe.astype(BF16), TN_DIMS, preferred_element_type=F32)
            dec = jnp.concatenate([_pair_rows(bc, s) for s in range(CH)], axis=0)
            qk = jnp.concatenate([q[_ROW0[s]:_ROW1[s], :] * k[s:s + 1, :] for s in range(CH)], axis=0)
            x = jnp.concatenate([do[_ROW0[s]:_ROW1[s], :] * v[s:s + 1, :] for s in range(CH)], axis=0)
            car_dec[slot, hp] = dec
            car_a[slot, hp] = jnp.dot((qk * dec).astype(BF16), ones, preferred_element_type=F32)
            car_da[slot, hp] = jnp.dot(x.astype(BF16), ones, preferred_element_type=F32)
            qe1, _, ke0, _ = _cross_half(q, k, bc)
            qe1_bf, ke0_bf = qe1.astype(BF16), ke0.astype(BF16)
            do1_bf, v0_bf = do[HALF:, :].astype(BF16), v[:HALF, :].astype(BF16)
            car_x[slot, hp, 0:HALF] = lax.dot_general(ke0_bf, qe1_bf, NT_DIMS, preferred_element_type=F32)
            car_x[slot, hp, HALF:2 * HALF] = lax.dot_general(do1_bf, v0_bf, NT_DIMS, preferred_element_type=F32)
            car_x[slot, hp, 2 * HALF:3 * HALF] = lax.dot_general(v0_bf, do1_bf, NT_DIMS, preferred_element_type=F32)

        def advance_state(c, hp, slot):
            ebl = jnp.exp(b_s[_chunk_rows(c, CH - SUBLANES, SUBLANES), hp * REC_K:(hp + 1) * REC_K][SUBLANES - 1:, :])
            st, dst = sh_ref[hp, 0, c], dst_ref[hp]
            dst_ref[hp] = dst * ebl + car_dst[slot, hp]
            return ebl * jnp.sum(st * dst, axis=0, keepdims=True)

        def cross(c, hp, slot):
            _, _, q, k, bc, v, do = operands(c, hp)
            qe1, _, ke0, _ = _cross_half(q, k, bc)
            xs = car_x[slot, hp]
            dqe1 = jnp.dot(xs[HALF:2 * HALF].astype(BF16), ke0.astype(BF16), preferred_element_type=F32)
            dke0 = jnp.dot(xs[2 * HALF:].astype(BF16), qe1.astype(BF16), preferred_element_type=F32)
            dv1 = jnp.dot(xs[:HALF].astype(BF16), do[HALF:, :].astype(BF16), preferred_element_type=F32)
            return dqe1, dke0, dv1

        def retire(c, slot):
            dbl_state = [advance_state(c, hp, slot) for hp in range(HP)]
            yield
            crossed = [cross(c, hp, slot) for hp in range(HP)]
            for hp in range(HP):
                finish(c, hp, slot, dbl_state[hp], *crossed[hp])

        def step(c, slot):
            closing = retire(c + 1, slot)
            next(closing)
            for hp in range(HP):
                issue(c, hp, 1 - slot)
            next(closing, None)

        def trip(j, carry):
            step(nck - 2 - 2 * j, 0)
            step(nck - 3 - 2 * j, 1)
            return carry

        def finish(c, hp, slot, dbl_state, dqe1, dke0, dv1):
            rs, cs, q, k, bc, v, do = operands(c, hp)
            eb, ekd = jnp.exp(bc), jnp.exp(bc[CH - 1:CH, :] - bc)
            qe, kd = q * eb, k * ekd
            qe1, e1, ke0, e0 = _cross_half(q, k, bc)
            dqe, dkd, dv = car_r[slot, hp, 0:CH], car_r[slot, hp, CH:2 * CH], car_r[slot, hp, 2 * CH:3 * CH]
            a, da, decs = car_a[slot, hp], car_da[slot, hp], car_dec[slot, hp]
            dec = [decs[_OFF_OF[s]:_OFF_OF[s] + _ROWS_OF[s], :] for s in range(CH)]
            dbl = jnp.sum(dkd * kd, axis=0, keepdims=True) + dbl_state
            dq_acc = [jnp.zeros((SUBLANES, REC_K), F32) for _ in range(ngrp)]
            uk, uv = [], []
            for s in range(CH):
                j = s // SUBLANES
                r0 = j * SUBLANES
                ks = k[s:s + 1, :]
                for jj in range(j, _ROW1[s] // SUBLANES):
                    lo, hi = _OFF_OF[s] + (jj - j) * SUBLANES, _OFF_OF[s] + (jj - j + 1) * SUBLANES
                    a_blk, da_blk = a[lo:hi, :], da[lo:hi, :]
                    if jj == j:
                        keep = sub >= s - r0
                        a_blk, da_blk = jnp.where(keep, a_blk, 0.0), jnp.where(keep, da_blk, 0.0)
                    rows = slice(jj * SUBLANES, (jj + 1) * SUBLANES)
                    tt = da_blk * dec[s][(jj - j) * SUBLANES:(jj - j + 1) * SUBLANES, :]
                    dq_acc[jj] = dq_acc[jj] + tt * ks
                    uk.append(tt * q[rows, :])
                    uv.append(a_blk * do[rows, :])
            dk_in = jnp.dot(pick, jnp.concatenate(uk, axis=0).astype(BF16), preferred_element_type=F32)
            dv_in = jnp.dot(pick, jnp.concatenate(uv, axis=0).astype(BF16), preferred_element_type=F32)
            zero_half = jnp.zeros((HALF, REC_K), F32)
            dq_x = jnp.concatenate([zero_half, dqe1 * e1], axis=0)
            dk_x = jnp.concatenate([dke0 * e0, zero_half], axis=0)
            dv_x = jnp.concatenate([dv1, zero_half], axis=0)
            db_x = jnp.concatenate([-(dke0 * ke0), dqe1 * qe1], axis=0)
            dq_in = jnp.concatenate(dq_acc, axis=0)
            dqv_s[rs, cs] = dqe * eb + dq_in + dq_x
            dk_s[rs, cs] = dkd * ekd + dk_in + dk_x
            di_ref[rs, cs] = (dv + dv_in + dv_x).astype(BF16)
            db = dqe * qe - dkd * kd + q * dq_in - k * dk_in + db_x
            db_s[rs, cs] = db + jnp.where(rowid == CH - 1, dbl, 0.0)

        assert nck % 2 == 0
        for hp in range(HP):
            issue(nck - 1, hp, 0)
        lax.fori_loop(0, nck // 2 - 1, trip, 0)
        step(0, 0)
        for _ in retire(0, 1):
            pass
        up = _tri(False)
        sgq = _sigmoid(qp)
        dq_ref[...] = (dqv_s[...] * (sgq * (1.0 + qp * (1.0 - sgq)))).astype(BF16)
        dlb_acc = jnp.zeros((1, _HW), F32)
        for r in range(0, tb, _CUM_ROWS):
            rows = slice(r, r + _CUM_ROWS)
            dgl = _dot3(up, db_s[rows, :])
            dfg = dgl * jnp.exp(-g[rows, :]) - dk_s[rows, :]
            sn = sig_neg[rows, :]
            df_ref[rows, :] = (dfg * (1.0 - lbv) * (1.0 - sn) * sn).astype(BF16)
            dlb_acc = dlb_acc + jnp.sum(dfg * sn, axis=0, keepdims=True)
        dlb_ref[...] += dlb_acc

    blk, head, lbs, gws, hist = _hgrn_specs(tb, nt, True)
    out_specs = [pl.BlockSpec((tb, REC_IN), lambda h, b, t: (b * nt + nt - 1 - t, 0)), lbs,
                 pl.BlockSpec((HP, 1, REC_K), lambda h, b, t: (h, 0, 0))]
    out_shape = [jax.ShapeDtypeStruct((n, REC_IN), BF16),
                 jax.ShapeDtypeStruct((1, 1024), F32), jax.ShapeDtypeStruct((REC_HEADS, 1, REC_K), F32)]
    return pl.pallas_call(
        body, name="hgrn_bwd", grid=(REC_HEADS // HP, b_loc, nt),
        in_specs=[blk(0), blk(1), blk(2), blk(3), lbs, gws, head, head, hist],
        out_specs=out_specs, out_shape=out_shape,
        scratch_shapes=[pltpu.VMEM((tb, _HW), F32)] * 7 + [pltpu.VMEM((HP, REC_K, REC_K), F32)] + [
            pltpu.VMEM((2, HP, 3 * CH, REC_K), F32), pltpu.VMEM((2, HP, REC_K, REC_K), F32)] + [
            pltpu.VMEM((2, HP, sum(_ROWS_OF), REC_K), F32)] * 3 + [pltpu.VMEM((2, HP, 3 * HALF, HALF), F32)],
        compiler_params=_params(("arbitrary", "arbitrary", "arbitrary"), 56),
    )(p, p, p, p, lb, gw, oraw, dg, sh)


def _postnorm_bwd_nt(dxo, y, qw, w, name):
    n = dxo.shape[0]

    def body(dx_ref, y_ref, qw_ref, w_ref, dg_ref, dy_ref, dqw_ref, db_ref):
        @pl.when(pl.program_id(0) == 0)
        def _():
            dqw_ref[...] = jnp.zeros_like(dqw_ref)
            db_ref[...] = jnp.zeros_like(db_ref)

        yv, dxv = y_ref[...], dx_ref[...]
        r = lax.rsqrt(jnp.mean(yv * yv, axis=-1, keepdims=True) + NORM_EPS)
        u = yv * r
        du = dxv * qw_ref[...]
        dy = r * (du - u * jnp.mean(du * u, axis=-1, keepdims=True))
        dqw_ref[...] += jnp.sum(dxv * u, axis=0, keepdims=True)
        db_ref[...] += jnp.sum(dy, axis=0, keepdims=True)
        dyb = dy.astype(BF16)
        dy_ref[...] = dyb
        dg_ref[...] = lax.dot_general(dyb, w_ref[...], NT_DIMS, preferred_element_type=F32)

    rows = pl.BlockSpec((TM, D_MODEL), lambda i: (i, 0))
    const = lambda shape: pl.BlockSpec(shape, lambda i: (0, 0))
    return pl.pallas_call(
        body, name=name, grid=(n // TM,), in_specs=[rows, rows, const((1, D_MODEL)), const((D_MODEL, D_MODEL))],
        out_specs=[rows, rows, const((1, D_MODEL)), const((1, D_MODEL))],
        out_shape=[jax.ShapeDtypeStruct((n, D_MODEL), F32), jax.ShapeDtypeStruct((n, D_MODEL), BF16),
                   jax.ShapeDtypeStruct((1, D_MODEL), F32), jax.ShapeDtypeStruct((1, D_MODEL), F32)],
        compiler_params=_params(("arbitrary",), 48),
    )(dxo, y, qw, w)


def _nt_prenorm_bwd(dps, w, x, pw, dxo, has_bias, name, parts=()):
    n = x.shape[0]
    widths = [d.shape[1] for d in dps]
    m = sum(widths)
    npieces, nparts, steps = len(dps), len(parts), n // TM

    def body(*refs):
        dp_refs = refs[:npieces]
        w_ref, x_ref, pw_ref, dxo_ref = refs[npieces:npieces + 4]
        part_refs = refs[npieces + 4:npieces + 4 + nparts]
        dx_ref, dpw_ref, db_ref = refs[npieces + 4 + nparts:npieces + 7 + nparts]
        land_refs = refs[npieces + 7 + nparts:npieces + 7 + 2 * nparts]
        sems = refs[npieces + 7 + 2 * nparts:]

        @pl.when(pl.program_id(0) == 0)
        def _():
            dpw_ref[...] = jnp.zeros_like(dpw_ref)
            db_ref[...] = jnp.zeros_like(db_ref)
            if nparts:
                _scatter_start(part_refs, land_refs, sems)

        dh = jnp.zeros((TM, D_MODEL), F32)
        off = 0
        for dp_ref, wd in zip(dp_refs, widths):
            cn = _col_chunk(wd)
            for j in range(0, wd, cn):
                dpc = dp_ref[:, j:j + cn]
                if has_bias:
                    db_ref[:, off + j:off + j + cn] += jnp.sum(dpc, axis=0, keepdims=True)
                dh = dh + lax.dot_general(dpc.astype(BF16), w_ref[:, off + j:off + j + cn], NT_DIMS, preferred_element_type=F32)
            off += wd
        xv = x_ref[...]
        r = lax.rsqrt(jnp.mean(xv * xv, axis=-1, keepdims=True) + NORM_EPS)
        xn = xv * r
        dpw_ref[...] += jnp.sum(dh * xn, axis=0, keepdims=True)
        dxn = dh * pw_ref[...]
        dx_ref[...] = dxo_ref[...] + r * (dxn - xn * jnp.mean(dxn * xn, axis=-1, keepdims=True))

        if nparts:
            @pl.when(pl.program_id(0) == steps - 1)
            def _():
                _scatter_wait(part_refs, land_refs, sems)

    rows = pl.BlockSpec((TM, D_MODEL), lambda i: (i, 0))
    const = lambda shape: pl.BlockSpec(shape, lambda i: (0, 0))
    hbm = pl.BlockSpec(memory_space=pl.ANY)
    in_specs = ([pl.BlockSpec((TM, wd), lambda i: (i, 0)) for wd in widths] + [const((D_MODEL, m)), rows, const((1, D_MODEL)), rows]
                + [hbm] * nparts)
    return pl.pallas_call(
        body, name=name, grid=(steps,), in_specs=in_specs,
        out_specs=[rows, const((1, D_MODEL)), const((1, m))] + [hbm] * nparts,
        out_shape=[jax.ShapeDtypeStruct((n, D_MODEL), F32), jax.ShapeDtypeStruct((1, D_MODEL), F32),
                   jax.ShapeDtypeStruct((1, m), F32)] + _scatter_lands(parts),
        scratch_shapes=_scatter_sems(nparts) if nparts else [],
        compiler_params=_params(("arbitrary",), 56),
    )(*dps, w, x, pw, dxo, *parts)


def _matmul_tn(a, b, name):
    n, k = a.shape
    m = b.shape[1]
    tk, tm, tn = k, _col_chunk(m), 1024 if n % 1024 == 0 else n

    def body(a_ref, b_ref, o_ref):
        @pl.when(pl.program_id(2) == 0)
        def _():
            o_ref[...] = jnp.zeros_like(o_ref)

        o_ref[...] += lax.dot_general(a_ref[...], b_ref[...].astype(BF16), TN_DIMS, preferred_element_type=F32)

    return pl.pallas_call(
        body, name=name, grid=(k // tk, m // tm, n // tn),
        in_specs=[pl.BlockSpec((tn, tk), lambda i, j, l: (l, i)), pl.BlockSpec((tn, tm), lambda i, j, l: (l, j))],
        out_specs=pl.BlockSpec((tk, tm), lambda i, j, l: (i, j)),
        out_shape=jax.ShapeDtypeStruct((k, m), F32),
        compiler_params=_params(("arbitrary", "arbitrary", "arbitrary"), 48),
    )(a, b)


def _matmul_tn_by_owner(a, b, name):
    n, k = a.shape
    c = b.shape[1] // N_DEV
    tn = 1024 if n % 1024 == 0 else n
    steps = n // tn

    def body(a_ref, b_ref, o_ref, w_ref):
        @pl.when(pl.program_id(1) == 0)
        def _():
            o_ref[...] = jnp.zeros_like(o_ref)

        o_ref[0] += lax.dot_general(a_ref[...], b_ref[...], TN_DIMS, preferred_element_type=F32)

        @pl.when(pl.program_id(1) == steps - 1)
        def _():
            w_ref[...] = o_ref[...].astype(BF16)

    out = pl.BlockSpec((1, k, c), lambda j, l: (j, 0, 0))
    return pl.pallas_call(
        body, name=name, grid=(N_DEV, steps),
        in_specs=[pl.BlockSpec((tn, k), lambda j, l: (l, 0)), pl.BlockSpec((tn, c), lambda j, l: (l, j))],
        out_specs=[out, out],
        out_shape=[jax.ShapeDtypeStruct((N_DEV, k, c), F32), jax.ShapeDtypeStruct((N_DEV, k, c), BF16)],
        compiler_params=_params(("arbitrary", "arbitrary"), 48),
    )(a, b)


def _by_owner_cols(dw):
    k, m = dw.shape
    return dw.reshape(k, N_DEV, m // N_DEV).transpose(1, 0, 2)


def _own_and_bf16(part):
    return lax.dynamic_index_in_dim(part, _my_id(), 0, keepdims=False), part.astype(BF16)


def _step(x, pos_col, tgt, pre_w, post_w, wa_in, ba_in, sinks, wa_out_shard, ba_out, wr_in_shard, lb_logits, gnorm_w, wr_out_shard, b_loc, t_len):
    nb = t_len // BLK
    ct, st = _rope_tables(pos_col)
    lb = _lower_bound(lb_logits)
    p0, h0, ga_out = _norm_matmul(x, pre_w[0:1], wa_in, ba_in, "attn_in_proj", [wa_out_shard])
    wa_out = ga_out.reshape(ATTN_WIDTH, D_MODEL)
    o0, g0, gr_in, gr_out = _attn_fwd(p0, ct, st, sinks, b_loc, nb, [wr_in_shard, wr_out_shard])
    wr_in = gr_in.transpose(1, 0, 2).reshape(D_MODEL, REC_IN)
    wr_out = gr_out.reshape(1024, D_MODEL)
    y0, x1 = _outproj_postnorm(g0, wa_out, ba_out, x, post_w[0:1], "attn_out_proj")
    p1, h1 = _norm_matmul(x1, pre_w[1:2], wr_in, None, "rec_in_proj")
    o1, g1, sh = _hgrn_fwd(p1, lb, gnorm_w, b_loc, t_len)
    dx2, dg1, dy1, dpost1, loss_tile = _outproj_loss_bwd(g1, wr_out, x1, post_w[1:2], tgt, "rec_out_proj_loss_bwd")
    d_wr_out = _matmul_tn(g1, dy1, "rec_w_out_grad")
    dp1, dlb, dgw = _hgrn_bwd(p1, lb, gnorm_w, o1, sh, dg1, b_loc, t_len)
    dx1, dpre1, _ = _nt_prenorm_bwd([dp1], wr_in, x1, pre_w[1:2], dx2, False, "rec_in_bwd")
    part_r_in, wire_r_in = _matmul_tn_by_owner(h1, dp1, "rec_w_in_grad")
    own_r_in = lax.dynamic_index_in_dim(part_r_in, _my_id(), 0, keepdims=False)
    dg0, dy0, dpost0, dba_out = _postnorm_bwd_nt(dx1, y0, post_w[0:1], wa_out, "attn_out_bwd")
    d_wa_out = _matmul_tn(g0, dy0, "attn_w_out_grad")
    owns, wires = zip(*[_own_and_bf16(part) for part in (
        d_wr_out.reshape(N_DEV, 1024 // N_DEV, D_MODEL), d_wa_out.reshape(N_DEV, ATTN_WIDTH // N_DEV, D_MODEL))])
    owns, wires = (own_r_in,) + owns, (wire_r_in,) + wires
    dp0, dsink_tile, *lands = _attn_bwd(p0, ct, st, sinks, o0, dg0, b_loc, nb, list(wires))
    d_wa_in = _matmul_tn(h0, dp0, "attn_w_in_grad")
    own_a_in, wire_a_in = _own_and_bf16(_by_owner_cols(_qkvz(d_wa_in)))
    dx0, dpre0, dba_in, land_a_in = _nt_prenorm_bwd([dp0], wa_in, x, pre_w[0:1], dx1, True, "attn_in_bwd", [wire_a_in])
    small = dict(pre=jnp.concatenate([dpre0, dpre1], axis=0), post=jnp.concatenate([dpost0, dpost1], axis=0),
                 ba_in=dba_in, sinks=dsink_tile[0:1, 0:N_HEADS], ba_out=dba_out, lb=dlb, gnorm=jnp.sum(dgw, axis=0))
    return loss_tile, dx0, list(zip(lands, owns)) + [(land_a_in, own_a_in)], small


def _my_id():
    return lax.axis_index("x") * 4 + lax.axis_index("y") * 2 + lax.axis_index("c")


def _peer(k):
    x, y, c = lax.axis_index("x"), lax.axis_index("y"), lax.axis_index("c")
    return (x ^ ((k >> 2) & 1), y ^ ((k >> 1) & 1), c ^ (k & 1))


def _peer_id(k):
    return _my_id() ^ k


def _all_gather_by_chip(shard):
    def body(x_ref, out_ref, send_sems, recv_sems, local_sem):
        x, y, c = lax.axis_index("x"), lax.axis_index("y"), lax.axis_index("c")
        me, sibling = (x, y, c), (x, y, 1 - c)
        chips = [(1 - x, y), (x, 1 - y), (1 - x, 1 - y)]

        def rows(px, py, pc):
            return out_ref.at[4 * px + 2 * py + pc]

        def copy(k, block, to, src=None):
            return pltpu.make_async_remote_copy(src_ref=rows(*block) if src is None else src, dst_ref=rows(*block),
                                                send_sem=send_sems.at[k], recv_sem=recv_sems.at[k], device_id=to, device_id_type=MESH)

        mine = pltpu.make_async_copy(x_ref, rows(*me), local_sem)
        mine.start()
        first = [copy(0, me, sibling, src=x_ref)] + [copy(1 + j, me, (*chip, c), src=x_ref) for j, chip in enumerate(chips)]
        for cp in first:
            cp.start()
        passed = [copy(4 + j, (*chip, c), sibling) for j, chip in enumerate(chips)]
        for j, chip in enumerate(chips):
            copy(1 + j, (*chip, c), me).wait_recv()
            passed[j].start()
        copy(0, sibling, me).wait_recv()
        for j, chip in enumerate(chips):
            copy(4 + j, (*chip, 1 - c), me).wait_recv()
        for cp in first + passed:
            cp.wait_send()
        mine.wait()

    hbm = pl.BlockSpec(memory_space=pl.ANY)
    return pl.pallas_call(
        body, name="comm_all_gather_by_chip", in_specs=[hbm], out_specs=hbm,
        out_shape=jax.ShapeDtypeStruct((N_DEV,) + shard.shape, shard.dtype),
        scratch_shapes=[pltpu.SemaphoreType.DMA((N_DEV - 1,)), pltpu.SemaphoreType.DMA((N_DEV - 1,)), pltpu.SemaphoreType.DMA],
    )(shard)


def _gather_shapes(shards):
    return [jax.ShapeDtypeStruct((N_DEV,) + s.shape, s.dtype) for s in shards]


def _gather_sems(nsh):
    return [pltpu.SemaphoreType.DMA((nsh, N_DEV - 1)), pltpu.SemaphoreType.DMA((nsh, N_DEV - 1)), pltpu.SemaphoreType.DMA((nsh,))]


def _gather_copies(ins, outs, sems, received):
    send_sems, recv_sems, local_sems = sems
    me = _my_id()
    local = [pltpu.make_async_copy(ins[a], outs[a].at[me], local_sems.at[a]) for a in range(len(ins))]
    remote = [pltpu.make_async_remote_copy(
        src_ref=ins[a], dst_ref=outs[a].at[_peer_id(k) if received else me], send_sem=send_sems.at[a, k - 1],
        recv_sem=recv_sems.at[a, k - 1], device_id=_peer(k), device_id_type=MESH)
        for a in range(len(ins)) for k in range(1, N_DEV)]
    return local, remote


def _gather_start(ins, outs, sems):
    local, sends = _gather_copies(ins, outs, sems, False)
    for cp in local + sends:
        cp.start()


def _gather_wait(ins, outs, sems):
    local, recvs = _gather_copies(ins, outs, sems, True)
    for cp in recvs:
        cp.wait_recv()
    for cp in recvs:
        cp.wait_send()
    for cp in local:
        cp.wait()


def _scatter_lands(parts):
    return [jax.ShapeDtypeStruct((N_DEV - 1,) + p.shape[1:], p.dtype) for p in parts]


def _scatter_sems(nparts):
    return [pltpu.SemaphoreType.DMA((nparts, N_DEV - 1)), pltpu.SemaphoreType.DMA((nparts, N_DEV - 1))]


def _scatter_copies(parts, lands, sems):
    send_sems, recv_sems = sems
    return [pltpu.make_async_remote_copy(
        src_ref=parts[a].at[_peer_id(k)], dst_ref=lands[a].at[k - 1], send_sem=send_sems.at[a, k - 1],
        recv_sem=recv_sems.at[a, k - 1], device_id=_peer(k), device_id_type=MESH)
        for a in range(len(parts)) for k in range(1, N_DEV)]


def _scatter_start(parts, lands, sems):
    for cp in _scatter_copies(parts, lands, sems):
        cp.start()


def _scatter_wait(parts, lands, sems):
    copies = _scatter_copies(parts, lands, sems)
    for cp in copies:
        cp.wait_recv()
    for cp in copies:
        cp.wait_send()


def _adamw(w, g, m, v):
    m2 = ADAM_B1 * m + (1.0 - ADAM_B1) * g
    v2 = ADAM_B2 * v + (1.0 - ADAM_B2) * (g * g)
    m_hat = m2 / (1.0 - ADAM_B1 ** ADAM_STEP)
    v_hat = v2 / (1.0 - ADAM_B2 ** ADAM_STEP)
    delta = -ADAM_LR * (m_hat / (jnp.sqrt(v_hat) + ADAM_EPS) + ADAM_WD * w)
    return delta, m2, v2


def _sum_adamw_rows(land_ref, own_ref, w_ref, m_ref, v_ref, out_refs):
    r, c = own_ref.shape
    rc = 64 if r % 64 == 0 else r
    me = _my_id()
    g_ref, d_ref, m2_ref, v2_ref = out_refs

    def rows(i, carry):
        rs = pl.ds(pl.multiple_of(i * rc, rc), rc)
        g = jnp.zeros((rc, c), F32)
        for dev in range(N_DEV):
            k = dev ^ me
            g = g + jnp.where(k == 0, own_ref[rs, :], land_ref[jnp.maximum(k - 1, 0), rs, :].astype(F32))
        delta, m2, v2 = _adamw(w_ref[rs, :], g, m_ref[rs, :], v_ref[rs, :])
        g_ref[rs, :] = g
        d_ref[rs, :] = delta
        m2_ref[rs, :] = m2
        v2_ref[rs, :] = v2
        return carry

    lax.fori_loop(0, r // rc, rows, 0)


def _sum_adamw(land, own, w, m, v, name):
    r, c = own.shape

    def body(land_ref, own_ref, w_ref, m_ref, v_ref, g_ref, d_ref, m2_ref, v2_ref):
        _sum_adamw_rows(land_ref, own_ref, w_ref, m_ref, v_ref, (g_ref, d_ref, m2_ref, v2_ref))

    vmem = pl.BlockSpec(memory_space=pltpu.VMEM)
    return pl.pallas_call(
        body, name=name, in_specs=[vmem] * 5, out_specs=[vmem] * 4, out_shape=[jax.ShapeDtypeStruct((r, c), F32)] * 4,
        compiler_params=_params(None, 56),
    )(land, own, w, m, v)


_SMALL = [("pre_norm_w", 2048), ("post_norm_w", 2048), ("attn_b_in", 2304), ("attn_sinks", 16), ("attn_b_out", 1024),
          ("rec_lb_logits", 2048), ("rec_gnorm_w", 128), ("loss", 1)]
_TILE = SUBLANES * LANES


def _small_rows(size):
    return -(-size // _TILE) * SUBLANES


_SMALL_OFF = {}
_r = 0
for _name, _size in _SMALL:
    _SMALL_OFF[_name] = _r
    _r += _small_rows(_size)
_SMALL_ROWS = _r


def _pack_small(pieces):
    out = []
    for name, size in _SMALL:
        flat = pieces[name].reshape(-1).astype(F32)
        out.append(jnp.pad(flat, (0, _small_rows(size) * LANES - size)).reshape(-1, LANES))
    return jnp.concatenate(out, axis=0)


def _unpack_small(packed, shapes):
    return {name: packed[_SMALL_OFF[name]:_SMALL_OFF[name] + _small_rows(size)].reshape(-1)[:size].reshape(shapes[name])
            for name, size in _SMALL}


def _small_allreduce_adamw(gpart, w, m, v):
    lb0 = _SMALL_OFF["rec_lb_logits"]

    def body(gp_ref, w_ref, m_ref, v_ref, g_ref, d_ref, m2_ref, v2_ref, land_ref, send_sems, recv_sems):
        me = _my_id()
        sent = []
        for k in range(1, N_DEV):
            cp = pltpu.make_async_remote_copy(src_ref=gp_ref, dst_ref=land_ref.at[k - 1], send_sem=send_sems.at[k - 1],
                                              recv_sem=recv_sems.at[k - 1], device_id=_peer(k), device_id_type=MESH)
            cp.start()
            sent.append(cp)
        for cp in sent:
            cp.wait_recv()
        for cp in sent:
            cp.wait_send()
        g = jnp.zeros((_SMALL_ROWS, LANES), F32)
        for dev in range(N_DEV):
            k = dev ^ me
            g = g + jnp.where(k == 0, gp_ref[...], land_ref[jnp.maximum(k - 1, 0)])
        g_ref[...] = g
        l0, l1 = w_ref[lb0:lb0 + SUBLANES, :], w_ref[lb0 + SUBLANES:lb0 + 2 * SUBLANES, :]
        mx = jnp.maximum(l0, l1)
        e0, e1 = jnp.exp(l0 - mx), jnp.exp(l1 - mx)
        p1 = e1 / (e0 + e1)
        dl1 = (1.0 - p1) * p1 * g[lb0:lb0 + SUBLANES, :]
        g_ref[lb0:lb0 + SUBLANES, :] = -dl1
        g_ref[lb0 + SUBLANES:lb0 + 2 * SUBLANES, :] = dl1
        delta, m2, v2 = _adamw(w_ref[...], g_ref[...], m_ref[...], v_ref[...])
        d_ref[...] = delta
        m2_ref[...] = m2
        v2_ref[...] = v2

    vmem = pl.BlockSpec(memory_space=pltpu.VMEM)
    return pl.pallas_call(
        body, name="comm_small_allreduce_adamw", in_specs=[vmem] * 4, out_specs=[vmem] * 4,
        out_shape=[jax.ShapeDtypeStruct((_SMALL_ROWS, LANES), F32)] * 4,
        scratch_shapes=[pltpu.VMEM((N_DEV - 1, _SMALL_ROWS, LANES), F32), pltpu.SemaphoreType.DMA((N_DEV - 1,)),
                        pltpu.SemaphoreType.DMA((N_DEV - 1,))],
    )(gpart, w, m, v)


def _qzkv(a):
    return jnp.concatenate([a[..., :1024], a[..., 1280:], a[..., 1024:1280]], axis=-1)


def _qkvz(a):
    return jnp.concatenate([a[..., :1024], a[..., 2048:], a[..., 1024:2048]], axis=-1)


def kernel(x, positions, pre_norm_w, post_norm_w, attn_w_in, attn_b_in, attn_sinks, attn_w_out, attn_b_out, rec_w_in, rec_lb_logits, rec_gnorm_w, rec_w_out, loss_target, m_pre_norm_w, m_post_norm_w, m_attn_w_in, m_attn_b_in, m_attn_sinks, m_attn_w_out, m_attn_b_out, m_rec_w_in, m_rec_lb_logits, m_rec_gnorm_w, m_rec_w_out, v_pre_norm_w, v_post_norm_w, v_attn_w_in, v_attn_b_in, v_attn_sinks, v_attn_w_out, v_attn_b_out, v_rec_w_in, v_rec_lb_logits, v_rec_gnorm_w, v_rec_w_out):
    b_loc, t_len, _ = x.shape
    n = b_loc * t_len
    ga_in = _all_gather_by_chip(attn_w_in[0].astype(BF16))
    wa_in = _qzkv(ga_in.transpose(1, 0, 2).reshape(D_MODEL, ATTN_IN))

    loss_tile, dx, landed, small = _step(
        x.reshape(n, D_MODEL), positions.reshape(n, 1).astype(F32), loss_target.reshape(n, D_MODEL),
        pre_norm_w, post_norm_w, wa_in, _qzkv(attn_b_in), attn_sinks, attn_w_out[0].astype(BF16), attn_b_out,
        rec_w_in[0].astype(BF16), rec_lb_logits, rec_gnorm_w, rec_w_out[0].astype(BF16), b_loc, t_len)

    lift = lambda outs: tuple(a[None] for a in outs)
    (l_r_in, o_r_in), (l_r_out, o_r_out), (l_a_out, o_a_out), (l_a_in, o_a_in) = landed
    r_a_in = lift(_sum_adamw(l_a_in, o_a_in, attn_w_in[0], m_attn_w_in[0], v_attn_w_in[0], "adamw_attn_w_in"))
    r_r_in = lift(_sum_adamw(l_r_in, o_r_in, rec_w_in[0], m_rec_w_in[0], v_rec_w_in[0], "adamw_rec_w_in"))
    r_r_out = lift(_sum_adamw(l_r_out, o_r_out, rec_w_out[0], m_rec_w_out[0], v_rec_w_out[0], "adamw_rec_w_out"))
    r_a_out = lift(_sum_adamw(l_a_out, o_a_out, attn_w_out[0], m_attn_w_out[0], v_attn_w_out[0], "adamw_attn_w_out"))

    gsmall = dict(pre_norm_w=small["pre"], post_norm_w=small["post"], attn_b_in=_qkvz(small["ba_in"]), attn_sinks=small["sinks"],
                  attn_b_out=small["ba_out"], rec_lb_logits=jnp.concatenate([small["lb"], jnp.zeros_like(small["lb"])], axis=0),
                  rec_gnorm_w=small["gnorm"], loss=loss_tile[0:1, 0:1])
    nil = jnp.zeros((1, 1), F32)
    wsmall = dict(pre_norm_w=pre_norm_w, post_norm_w=post_norm_w, attn_b_in=attn_b_in, attn_sinks=attn_sinks,
                  attn_b_out=attn_b_out, rec_lb_logits=rec_lb_logits, rec_gnorm_w=rec_gnorm_w, loss=nil)
    msmall = dict(pre_norm_w=m_pre_norm_w, post_norm_w=m_post_norm_w, attn_b_in=m_attn_b_in, attn_sinks=m_attn_sinks,
                  attn_b_out=m_attn_b_out, rec_lb_logits=m_rec_lb_logits, rec_gnorm_w=m_rec_gnorm_w, loss=nil)
    vsmall = dict(pre_norm_w=v_pre_norm_w, post_norm_w=v_post_norm_w, attn_b_in=v_attn_b_in, attn_sinks=v_attn_sinks,
                  attn_b_out=v_attn_b_out, rec_lb_logits=v_rec_lb_logits, rec_gnorm_w=v_rec_gnorm_w, loss=nil)
    shapes = {k: a.shape for k, a in wsmall.items()}
    packed = _small_allreduce_adamw(_pack_small(gsmall), _pack_small(wsmall), _pack_small(msmall), _pack_small(vsmall))
    sg, sd, sm, sv = [_unpack_small(a, shapes) for a in packed]

    big = {"attn_w_in": r_a_in, "attn_w_out": r_a_out, "rec_w_in": r_r_in, "rec_w_out": r_r_out}
    order = ["pre_norm_w", "post_norm_w", "attn_w_in", "attn_b_in", "attn_sinks", "attn_w_out", "attn_b_out", "rec_w_in",
             "rec_lb_logits", "rec_gnorm_w", "rec_w_out"]
    outs = [sg["loss"][0, 0], dx.reshape(b_loc, t_len, D_MODEL)]
    for idx, small_set in enumerate((sg, sd, sm, sv)):
        outs += [big[nm][idx] if nm in big else small_set[nm] for nm in order]
    return tuple(outs)
```

```python
import numpy as np
import jax
import jax.numpy as jnp
from jax import lax
from jax.experimental import pallas as pl
from jax.experimental.pallas import tpu as pltpu

F32, BF16 = jnp.float32, jnp.bfloat16
MESH = pl.DeviceIdType.MESH
N_DEV = 8

D_MODEL = 1024
N_HEADS, HEAD_DIM, N_KV, GROUP = 16, 64, 2, 8
ATTN_WIDTH, KV_WIDTH = 1024, 128
ATTN_IN = 2 * ATTN_WIDTH + 2 * KV_WIDTH
BLK = 128
ROPE_THETA, ROPE_HALF = 500000.0, 8
REC_HEADS, REC_K = 8, 128
REC_IN = 4 * 1024
CH = 32
NORM_EPS = 1e-6
ADAM_LR, ADAM_B1, ADAM_B2, ADAM_EPS, ADAM_WD, ADAM_STEP = 0.001, 0.9, 0.999, 1e-08, 0.01, 10

LANES, SUBLANES = 128, 8
TM = 512
NT_DIMS = (((1,), (1,)), ((), ()))
TN_DIMS = (((0,), (0,)), ((), ()))
MB = 2 ** 20


def _params(sem=None, vmem_mb=48, **kw):
    return pltpu.CompilerParams(dimension_semantics=sem, vmem_limit_bytes=vmem_mb * MB, **kw)


def _col_chunk(m):
    return 768 if m % 1024 else 1024


def _sigmoid(x):
    return 1.0 / (1.0 + jnp.exp(-x))


def _split3(x):
    hi = x.astype(BF16)
    r1 = x - hi.astype(F32)
    mid = r1.astype(BF16)
    lo = (r1 - mid.astype(F32)).astype(BF16)
    return hi, mid, lo


def _dot3(l_bf, x):
    hi, mid, lo = _split3(x)
    return (jnp.dot(l_bf, hi, preferred_element_type=F32) + jnp.dot(l_bf, mid, preferred_element_type=F32)
            + jnp.dot(l_bf, lo, preferred_element_type=F32))


def _rope_tables(pos_col):
    n = pos_col.shape[0]
    lane = np.arange(LANES) % HEAD_DIM
    inv = np.float32(ROPE_THETA) ** (-(np.arange(ROPE_HALF, dtype=np.float32) * np.float32(2.0) / np.float32(2 * ROPE_HALF)))
    freq = np.where(lane < 2 * ROPE_HALF, inv[lane % ROPE_HALF], 0.0).astype(np.float32)[None, :]
    sign = np.where(lane < ROPE_HALF, -1.0, np.where(lane < 2 * ROPE_HALF, 1.0, 0.0)).astype(np.float32)[None, :]

    def body(p_ref, f_ref, s_ref, c_out, s_out):
        ang = p_ref[...] * f_ref[...]
        c_out[...] = jnp.cos(ang)
        s_out[...] = jnp.sin(ang) * s_ref[...]

    row = pl.BlockSpec((TM, 1), lambda i: (i, 0))
    vec = pl.BlockSpec((1, LANES), lambda i: (0, 0))
    out = pl.BlockSpec((TM, LANES), lambda i: (i, 0))
    return pl.pallas_call(
        body, name="rope_tables", grid=(n // TM,), in_specs=[row, vec, vec], out_specs=[out, out],
        out_shape=[jax.ShapeDtypeStruct((n, LANES), F32)] * 2, compiler_params=_params(("arbitrary",)),
    )(pos_col, jnp.asarray(freq), jnp.asarray(sign))


def _rope_apply(xv, c, s, lm):
    partner = jnp.where(lm < ROPE_HALF, pltpu.roll(xv, LANES - ROPE_HALF, 1), pltpu.roll(xv, ROPE_HALF, 1))
    return xv * c + partner * s


def _rope_bwd(dy, c, s, lm):
    t = dy * s
    partner = jnp.where(lm < ROPE_HALF, pltpu.roll(t, LANES - ROPE_HALF, 1),
                        jnp.where(lm < 2 * ROPE_HALF, pltpu.roll(t, ROPE_HALF, 1), 0.0))
    return dy * c + partner


def _lower_bound(lb_logits):
    def body(l_ref, o_ref):
        l0, l1 = l_ref[0:1, :], l_ref[1:2, :]
        m = jnp.maximum(l0, l1)
        e0, e1 = jnp.exp(l0 - m), jnp.exp(l1 - m)
        o_ref[...] = e1 / (e0 + e1)

    return pl.pallas_call(body, name="lower_bound", out_shape=jax.ShapeDtypeStruct((1, lb_logits.shape[1]), F32))(lb_logits)


def _norm_matmul(x, pw, w, bias, name, shards=()):
    n, m = x.shape[0], w.shape[1]
    cn = _col_chunk(m)
    has_bias = bias is not None
    nsh, steps = len(shards), n // TM

    def body(*refs):
        refs = list(refs)
        x_ref, pw_ref, w_ref = refs[:3]
        b_ref = refs[3] if has_bias else None
        refs = refs[4 if has_bias else 3:]
        sh_in, (p_ref, h_ref), sh_out, sems = refs[:nsh], refs[nsh:nsh + 2], refs[nsh + 2:2 * nsh + 2], refs[2 * nsh + 2:]
        if nsh:
            @pl.when(pl.program_id(0) == 0)
            def _():
                _gather_start(sh_in, sh_out, sems)

        xv = x_ref[...]
        r = lax.rsqrt(jnp.mean(xv * xv, axis=-1, keepdims=True) + NORM_EPS)
        h = ((xv * r) * pw_ref[...]).astype(BF16)
        h_ref[...] = h
        for j in range(0, m, cn):
            acc = jnp.dot(h, w_ref[:, j:j + cn], preferred_element_type=F32)
            if has_bias:
                acc = acc + b_ref[:, j:j + cn]
            p_ref[:, j:j + cn] = acc

        if nsh:
            @pl.when(pl.program_id(0) == steps - 1)
            def _():
                _gather_wait(sh_in, sh_out, sems)

    rows = pl.BlockSpec((TM, D_MODEL), lambda i: (i, 0))
    const = lambda shape: pl.BlockSpec(shape, lambda i: (0, 0))
    hbm = pl.BlockSpec(memory_space=pl.ANY)
    in_specs = [rows, const((1, D_MODEL)), const((D_MODEL, m))] + ([const((1, m))] if has_bias else []) + [hbm] * nsh
    args = (x, pw, w) + ((bias,) if has_bias else ()) + tuple(shards)
    return pl.pallas_call(
        body, name=name, grid=(steps,), in_specs=in_specs,
        out_specs=[pl.BlockSpec((TM, m), lambda i: (i, 0)), rows] + [hbm] * nsh,
        out_shape=[jax.ShapeDtypeStruct((n, m), F32), jax.ShapeDtypeStruct((n, D_MODEL), BF16)] + _gather_shapes(shards),
        scratch_shapes=_gather_sems(nsh) if nsh else [],
        compiler_params=_params(("arbitrary",), 56),
    )(*args)


def _outproj_postnorm(g, w, bias, xres, qw, name):
    n = g.shape[0]

    def body(g_ref, w_ref, b_ref, x_ref, qw_ref, y_ref, o_ref):
        y = jnp.dot(g_ref[...], w_ref[...], preferred_element_type=F32) + b_ref[...]
        y_ref[...] = y
        r = lax.rsqrt(jnp.mean(y * y, axis=-1, keepdims=True) + NORM_EPS)
        o_ref[...] = x_ref[...] + (y * r) * qw_ref[...]

    rows = pl.BlockSpec((TM, D_MODEL), lambda i: (i, 0))
    const = lambda shape: pl.BlockSpec(shape, lambda i: (0, 0))
    return pl.pallas_call(
        body, name=name, grid=(n // TM,),
        in_specs=[rows, const((D_MODEL, D_MODEL)), const((1, D_MODEL)), rows, const((1, D_MODEL))],
        out_specs=[rows, rows], out_shape=[jax.ShapeDtypeStruct((n, D_MODEL), F32)] * 2,
        compiler_params=_params(("arbitrary",), 48),
    )(g, w, bias, xres, qw)


def _outproj_loss_bwd(g, w, xres, qw, tgt, name):
    n = g.shape[0]
    steps = n // TM

    def body(g_ref, w_ref, x_ref, qw_ref, t_ref, dx_ref, dg_ref, dy_ref, dqw_ref, loss_ref, acc_ref):
        i = pl.program_id(0)

        @pl.when(i == 0)
        def _():
            acc_ref[...] = jnp.zeros_like(acc_ref)
            dqw_ref[...] = jnp.zeros_like(dqw_ref)

        y = jnp.dot(g_ref[...], w_ref[...], preferred_element_type=F32)
        r = lax.rsqrt(jnp.mean(y * y, axis=-1, keepdims=True) + NORM_EPS)
        u = y * r
        e = (x_ref[...] + u * qw_ref[...]) - t_ref[...]
        dxn = e * (1.0 / D_MODEL)
        dx_ref[...] = dxn
        acc_ref[...] += jnp.sum(e * e, axis=0, keepdims=True)
        du = dxn * qw_ref[...]
        dy = (r * (du - u * jnp.mean(du * u, axis=-1, keepdims=True))).astype(BF16)
        dqw_ref[...] += jnp.sum(dxn * u, axis=0, keepdims=True)
        dy_ref[...] = dy
        dg_ref[...] = lax.dot_general(dy, w_ref[...], NT_DIMS, preferred_element_type=F32)

        @pl.when(i == steps - 1)
        def _():
            loss_ref[...] = jnp.full(loss_ref.shape, jnp.sum(acc_ref[...]) * (0.5 / D_MODEL), F32)

    rows = pl.BlockSpec((TM, D_MODEL), lambda i: (i, 0))
    const = lambda shape: pl.BlockSpec(shape, lambda i: (0, 0))
    return pl.pallas_call(
        body, name=name, grid=(steps,),
        in_specs=[rows, const((D_MODEL, D_MODEL)), rows, const((1, D_MODEL)), rows],
        out_specs=[rows, rows, rows, const((1, D_MODEL)), const((SUBLANES, LANES))],
        out_shape=[jax.ShapeDtypeStruct((n, D_MODEL), F32), jax.ShapeDtypeStruct((n, D_MODEL), F32),
                   jax.ShapeDtypeStruct((n, D_MODEL), BF16), jax.ShapeDtypeStruct((1, D_MODEL), F32),
                   jax.ShapeDtypeStruct((SUBLANES, LANES), F32)],
        scratch_shapes=[pltpu.VMEM((1, D_MODEL), F32)], compiler_params=_params(("arbitrary",), 48),
    )(g, w, xres, qw, tgt)


_QCOL, _ZCOL, _KCOL, _VCOL = 0, 1024, 2048, 2176


def _head_stack(chunks, heads, lt64):
    return jnp.concatenate([jnp.where(lt64 if n % 2 == 0 else ~lt64, chunks[n // 2], 0.0) for n in heads], axis=0)


def _dup_half(x, h, lt64):
    r = pltpu.roll(x, HEAD_DIM, 1)
    return jnp.where(lt64, x, r) if h == 0 else jnp.where(lt64, r, x)


def _pair_chunk(xt, c2):
    a, b = 2 * c2, 2 * c2 + 1
    return jnp.concatenate([xt[:HEAD_DIM, a * BLK:(a + 1) * BLK], xt[HEAD_DIM:, b * BLK:(b + 1) * BLK]], axis=0).T


def _attn_mask_t(i):
    key = lax.broadcasted_iota(jnp.int32, (2 * BLK, BLK), 0)
    qry = lax.broadcasted_iota(jnp.int32, (2 * BLK, BLK), 1)
    valid = (key > qry) & (key <= qry + BLK) & ((key >= BLK) | (i > 0))
    return jnp.tile(jnp.where(valid, 0.0, -1e30), (1, GROUP))


def _attn_probs_t(s, heads, sink_ref, mask):
    s = s + mask
    head = lax.broadcasted_iota(jnp.int32, (1, len(heads) * BLK), 1) >> 7
    sk = jnp.zeros((1, len(heads) * BLK), F32)
    for j, n in enumerate(heads):
        sk = jnp.where(head == j, sink_ref[0, n], sk)
    m = jnp.maximum(jnp.max(s, axis=0, keepdims=True), sk)
    p = jnp.exp(s - m)
    esk = jnp.exp(sk - m)
    inv = 1.0 / (jnp.sum(p, axis=0, keepdims=True) + esk)
    return p * inv, esk * inv


def _attn_fwd(p, ct, st, sinks, b_loc, nb, shards):
    n = p.shape[0]
    nsh = len(shards)

    def body(sink_ref, q_ref, z_ref, kc_ref, kp_ref, vc_ref, vp_ref, cc_ref, sc_ref, cp_ref, sp_ref, *rest):
        sh_in, (o_ref, g_ref), sh_out, sems = rest[:nsh], rest[nsh:nsh + 2], rest[nsh + 2:2 * nsh + 2], rest[2 * nsh + 2:]
        b, i = pl.program_id(0), pl.program_id(1)

        @pl.when((b == 0) & (i == 0))
        def _():
            _gather_start(sh_in, sh_out, sems)

        lane = lax.broadcasted_iota(jnp.int32, (BLK, LANES), 1)
        lm = lane & (HEAD_DIM - 1)
        cc, sc = cc_ref[...], sc_ref[...]
        kcat = jnp.concatenate([_rope_apply(kp_ref[...], cp_ref[...], sp_ref[...], lm),
                                _rope_apply(kc_ref[...], cc, sc, lm)], axis=0)
        vcat = jnp.concatenate([vp_ref[...], vc_ref[...]], axis=0)
        qr = [_rope_apply(q_ref[:, c * LANES:(c + 1) * LANES], cc, sc, lm) * (HEAD_DIM ** -0.5) for c in range(8)]
        valid = _attn_mask_t(i)
        lt64, lt64k = lane < HEAD_DIM, lax.broadcasted_iota(jnp.int32, (2 * BLK, LANES), 1) < HEAD_DIM
        def kv_head(h):
            heads = list(range(h * GROUP, (h + 1) * GROUP))
            kext, vext = _dup_half(kcat, h, lt64k).astype(BF16), _dup_half(vcat, h, lt64k).astype(BF16)
            qst = _head_stack(qr, heads, lt64).astype(BF16)
            s = lax.dot_general(kext, qst, NT_DIMS, preferred_element_type=F32)
            yield
            pn, _ = _attn_probs_t(s, heads, sink_ref, valid)
            ot = lax.dot_general(vext, pn.astype(BF16), TN_DIMS, preferred_element_type=F32)
            yield
            for c2 in range(GROUP // 2):
                oc = _pair_chunk(ot, c2)
                cols = slice((4 * h + c2) * LANES, (4 * h + c2 + 1) * LANES)
                zc = z_ref[:, cols]
                o_ref[:, cols] = oc
                g_ref[:, cols] = (oc * (zc * _sigmoid(zc))).astype(BF16)

        _in_stages([kv_head(h) for h in range(N_KV)])

        @pl.when((b == b_loc - 1) & (i == nb - 1))
        def _():
            _gather_wait(sh_in, sh_out, sems)

    cur = lambda b, i: b * nb + i
    prev = lambda b, i: b * nb + jnp.maximum(i - 1, 0)
    wide = lambda cb: pl.BlockSpec((BLK, ATTN_WIDTH), lambda b, i: (cur(b, i), cb))
    kv = lambda rowf, cb: pl.BlockSpec((BLK, LANES), lambda b, i: (rowf(b, i), cb))
    hbm = pl.BlockSpec(memory_space=pl.ANY)
    in_specs = [pl.BlockSpec(memory_space=pltpu.SMEM), wide(0), wide(1),
                kv(cur, _KCOL // LANES), kv(prev, _KCOL // LANES), kv(cur, _VCOL // LANES), kv(prev, _VCOL // LANES),
                kv(cur, 0), kv(cur, 0), kv(prev, 0), kv(prev, 0)] + [hbm] * nsh
    return pl.pallas_call(
        body, name="attn_fwd", grid=(b_loc, nb), in_specs=in_specs, out_specs=[wide(0), wide(0)] + [hbm] * nsh,
        out_shape=[jax.ShapeDtypeStruct((n, ATTN_WIDTH), F32), jax.ShapeDtypeStruct((n, ATTN_WIDTH), BF16)] + _gather_shapes(shards),
        scratch_shapes=_gather_sems(nsh), compiler_params=_params(("arbitrary", "arbitrary"), 48),
    )(sinks, p, p, p, p, p, p, ct, st, ct, st, *shards)


def _attn_bwd(p, ct, st, sinks, o, dg, b_loc, nb, parts):
    n = p.shape[0]
    nparts = len(parts)

    def body(sink_ref, q_ref, z_ref, kc_ref, kp_ref, vc_ref, vp_ref, cc_ref, sc_ref, cp_ref, sp_ref, o_ref, dg_ref, *rest):
        part_refs, (dp_ref, ds_ref), land_refs = rest[:nparts], rest[nparts:nparts + 2], rest[nparts + 2:2 * nparts + 2]
        dq_s, dz_s, dk_s, dv_s = rest[2 * nparts + 2:2 * nparts + 6]
        sems = rest[2 * nparts + 6:]
        b, i = pl.program_id(0), pl.program_id(1)

        @pl.when((b == 0) & (i == 0))
        def _():
            _scatter_start(part_refs, land_refs, sems)

        @pl.when((b == b_loc - 1) & (i == nb))
        def _():
            _scatter_wait(part_refs, land_refs, sems)

        lane = lax.broadcasted_iota(jnp.int32, (BLK, LANES), 1)
        lm = lane & (HEAD_DIM - 1)

        @pl.when((b == 0) & (i == 0))
        def _():
            ds_ref[...] = jnp.zeros_like(ds_ref)

        @pl.when(i < nb)
        def _compute():
            cc, sc = cc_ref[...], sc_ref[...]
            kcat = jnp.concatenate([_rope_apply(kp_ref[...], cp_ref[...], sp_ref[...], lm),
                                    _rope_apply(kc_ref[...], cc, sc, lm)], axis=0)
            vcat = jnp.concatenate([vp_ref[...], vc_ref[...]], axis=0)
            qr = [_rope_apply(q_ref[:, c * LANES:(c + 1) * LANES], cc, sc, lm) * (HEAD_DIM ** -0.5) for c in range(8)]
            valid = _attn_mask_t(i)
            lt64, lt64k = lane < HEAD_DIM, lax.broadcasted_iota(jnp.int32, (2 * BLK, LANES), 1) < HEAD_DIM
            do_chunks, doo_chunks, dz_chunks = [], [], []
            for c in range(8):
                cols = slice(c * LANES, (c + 1) * LANES)
                zc, oc, dgc = z_ref[:, cols], o_ref[:, cols], dg_ref[:, cols]
                sg = _sigmoid(zc)
                do_chunks.append(dgc * (zc * sg))
                dz_chunks.append(dgc * oc * (sg * (1.0 + zc * (1.0 - sg))))
                doo_chunks.append(do_chunks[c] * oc)
            dq_chunks = [None] * 8
            dk_h, dv_h, ds_parts = [None] * N_KV, [None] * N_KV, [None] * N_KV
            tile_lane = lax.broadcasted_iota(jnp.int32, (SUBLANES, LANES), 1)
            tile_row = lax.broadcasted_iota(jnp.int32, (SUBLANES, LANES), 0)
            ones8 = jnp.ones((SUBLANES, LANES), BF16)

            def kv_head(h):
                heads = list(range(h * GROUP, (h + 1) * GROUP))
                kext = _dup_half(kcat, h, lt64k)
                kext_bf, kext_t = kext.astype(BF16), kext.T.astype(BF16)
                vext = _dup_half(vcat, h, lt64k).astype(BF16)
                qst = _head_stack(qr, heads, lt64).astype(BF16)
                pn, psink = _attn_probs_t(lax.dot_general(kext_bf, qst, NT_DIMS, preferred_element_type=F32), heads, sink_ref, valid)
                do_bf = _head_stack(do_chunks, heads, lt64).astype(BF16)
                delta = sum(lax.dot_general(ones8, part, NT_DIMS, preferred_element_type=F32)
                            for part in _split3(_head_stack(doo_chunks, heads, lt64)))[0:1, :]
                dpt = lax.dot_general(vext, do_bf, NT_DIMS, preferred_element_type=F32)
                dst = (pn * (dpt - delta)).astype(BF16)
                sink_term = psink * delta
                ds_acc = jnp.zeros((SUBLANES, LANES), F32)
                for j, n in enumerate(heads):
                    val = -jnp.sum(sink_term[:, j * BLK:(j + 1) * BLK])
                    ds_acc = ds_acc + jnp.where((tile_lane == n) & (tile_row == 0), val, 0.0)
                ds_parts[h] = ds_acc
                dqt = jnp.dot(kext_t, dst, preferred_element_type=F32) * (HEAD_DIM ** -0.5)
                dk_ext = jnp.dot(dst, qst, preferred_element_type=F32)
                dv_ext = jnp.dot(pn.astype(BF16), do_bf, preferred_element_type=F32)
                dk_h[h] = dk_ext + pltpu.roll(dk_ext, HEAD_DIM, 1)
                dv_h[h] = dv_ext + pltpu.roll(dv_ext, HEAD_DIM, 1)
                for c2 in range(GROUP // 2):
                    dq_chunks[4 * h + c2] = _rope_bwd(_pair_chunk(dqt, c2), cc, sc, lm)

            for h in range(N_KV):
                kv_head(h)
            ds_ref[...] += ds_parts[0] + ds_parts[1]
            dk_full = jnp.where(lt64k, dk_h[0], dk_h[1])
            dv_full = jnp.where(lt64k, dv_h[0], dv_h[1])

            @pl.when(i >= 1)
            def _emit():
                dp_ref[:, _QCOL:_QCOL + ATTN_WIDTH] = dq_s[...]
                dp_ref[:, _ZCOL:_ZCOL + ATTN_WIDTH] = dz_s[...]
                dp_ref[:, _KCOL:_KCOL + KV_WIDTH] = _rope_bwd(dk_s[...] + dk_full[:BLK], cp_ref[...], sp_ref[...], lm)
                dp_ref[:, _VCOL:_VCOL + KV_WIDTH] = dv_s[...] + dv_full[:BLK]

            for c in range(8):
                dq_s[:, c * LANES:(c + 1) * LANES] = dq_chunks[c]
                dz_s[:, c * LANES:(c + 1) * LANES] = dz_chunks[c]
            dk_s[...] = dk_full[BLK:]
            dv_s[...] = dv_full[BLK:]

        @pl.when(i == nb)
        def _final():
            dp_ref[:, _QCOL:_QCOL + ATTN_WIDTH] = dq_s[...]
            dp_ref[:, _ZCOL:_ZCOL + ATTN_WIDTH] = dz_s[...]
            dp_ref[:, _KCOL:_KCOL + KV_WIDTH] = _rope_bwd(dk_s[...], cc_ref[...], sc_ref[...], lm)
            dp_ref[:, _VCOL:_VCOL + KV_WIDTH] = dv_s[...]

    cur = lambda b, i: b * nb + jnp.minimum(i, nb - 1)
    prev = lambda b, i: b * nb + jnp.maximum(jnp.minimum(i, nb - 1) - 1, 0)
    emit = lambda b, i: b * nb + jnp.maximum(i - 1, 0)
    hbm = pl.BlockSpec(memory_space=pl.ANY)
    wide = lambda cb: pl.BlockSpec((BLK, ATTN_WIDTH), lambda b, i: (cur(b, i), cb))
    kv = lambda rowf, cb: pl.BlockSpec((BLK, LANES), lambda b, i: (rowf(b, i), cb))
    in_specs = [pl.BlockSpec(memory_space=pltpu.SMEM), wide(0), wide(1),
                kv(cur, _KCOL // LANES), kv(prev, _KCOL // LANES), kv(cur, _VCOL // LANES), kv(prev, _VCOL // LANES),
                kv(cur, 0), kv(cur, 0), kv(prev, 0), kv(prev, 0), wide(0), wide(0)] + [hbm] * nparts
    out_specs = [pl.BlockSpec((BLK, ATTN_IN), lambda b, i: (emit(b, i), 0)),
                 pl.BlockSpec((SUBLANES, LANES), lambda b, i: (0, 0))] + [hbm] * nparts
    return pl.pallas_call(
        body, name="attn_bwd", grid=(b_loc, nb + 1), in_specs=in_specs, out_specs=out_specs,
        out_shape=[jax.ShapeDtypeStruct((n, ATTN_IN), F32), jax.ShapeDtypeStruct((SUBLANES, LANES), F32)] + _scatter_lands(parts),
        scratch_shapes=[pltpu.VMEM((BLK, ATTN_WIDTH), F32), pltpu.VMEM((BLK, ATTN_WIDTH), F32),
                        pltpu.VMEM((BLK, KV_WIDTH), F32), pltpu.VMEM((BLK, KV_WIDTH), F32)] + _scatter_sems(nparts),
        compiler_params=_params(("arbitrary", "arbitrary"), 48),
    )(sinks, p, p, p, p, p, p, ct, st, ct, st, o, dg, *parts)


_CUM_ROWS = 256
HALF = CH // 2
_ROW0 = [SUBLANES * (s // SUBLANES) for s in range(CH)]
_ROW1 = [HALF * (s // HALF + 1) for s in range(CH)]
_ROWS_OF = [_ROW1[s] - _ROW0[s] for s in range(CH)]
_OFF_OF = [sum(_ROWS_OF[:s]) for s in range(CH)]


def _tri(lower):
    r = lax.broadcasted_iota(jnp.int32, (_CUM_ROWS, _CUM_ROWS), 0)
    c = lax.broadcasted_iota(jnp.int32, (_CUM_ROWS, _CUM_ROWS), 1)
    same = (r ^ c) < CH
    return (same & ((c <= r) if lower else (c >= r))).astype(BF16)


def _gates(qp, fp, lb):
    e = jnp.exp(-jnp.abs(fp))
    log_sig = jnp.minimum(fp, 0.0) - jnp.log(1.0 + e)
    a = jnp.log(lb)
    c = jnp.log(1.0 - lb) + log_sig
    g = jnp.maximum(a, c) + jnp.log(1.0 + jnp.exp(-jnp.abs(a - c)))
    sig_neg = jnp.where(fp >= 0, e, 1.0) / (1.0 + e)
    return qp * _sigmoid(qp), g, (1.0 - lb) * sig_neg, sig_neg


def _pair_rows(bc, s):
    return jnp.exp(jnp.minimum(bc[_ROW0[s]:_ROW1[s], :] - bc[s:s + 1, :], 0.0))


def _cross_half(q, k, bc):
    r = bc[HALF - 1:HALF, :]
    e1, e0 = jnp.exp(bc[HALF:, :] - r), jnp.exp(r - bc[:HALF, :])
    return q[HALF:, :] * e1, e1, k[:HALF, :] * e0, e0


HP = 8
REC_TB = 256
_HW = HP * REC_K


def _hgrn_specs(tb, nt, reverse):
    tmap = (lambda t: nt - 1 - t) if reverse else (lambda t: t)
    groups = REC_HEADS // HP
    blk = lambda cb: pl.BlockSpec((tb, _HW), lambda h, b, t: (b * nt + tmap(t), cb * groups + h))
    head = pl.BlockSpec((tb, _HW), lambda h, b, t: (b * nt + tmap(t), h))
    lbs = pl.BlockSpec((1, _HW), lambda h, b, t: (0, h))
    gws = pl.BlockSpec((1, REC_K), lambda h, b, t: (0, 0))
    hist = pl.BlockSpec((HP, 1, tb // CH, REC_K, REC_K), lambda h, b, t: (h, b, tmap(t), 0, 0))
    return blk, head, lbs, gws, hist


def _chunk_rows(c, first=0, size=CH):
    start = c * CH + first
    return pl.ds(start if isinstance(start, int) else pl.multiple_of(start, CH if first % CH == 0 else SUBLANES), size)


def _in_stages(heads):
    live = list(heads)
    while live:
        live = [g for g in live if next(g, live) is not live]


def _cumsum_chunks(tri, x, out_ref, tb):
    for r in range(0, tb, _CUM_ROWS):
        out_ref[r:r + _CUM_ROWS, :] = _dot3(tri, x[r:r + _CUM_ROWS, :])


def _hgrn_fwd(p, lb, gw, b_loc, t_len):
    n = p.shape[0]
    tb = min(REC_TB, t_len)
    nt, nck = t_len // tb, tb // CH

    def body(qp_ref, fp_ref, i_ref, z_ref, lb_ref, gw_ref, oraw_ref, g_ref, sh_ref, q_s, k_s, b_s, o_s, st_ref,
             car_o, car_a, car_s, car_st):
        @pl.when(pl.program_id(2) == 0)
        def _():
            st_ref[...] = jnp.zeros_like(st_ref)

        qv, g, kk, _ = _gates(qp_ref[...], fp_ref[...], lb_ref[...])
        q_s[...] = qv
        k_s[...] = kk
        _cumsum_chunks(_tri(True), g, b_s, tb)
        ones = jnp.ones((REC_K, REC_K), BF16)
        sub = lax.broadcasted_iota(jnp.int32, (SUBLANES, REC_K), 0)

        rows_of = _chunk_rows

        def issue(c, hp):
            rs, cs = rows_of(c), slice(hp * REC_K, (hp + 1) * REC_K)
            q, k, bc, v = q_s[rs, cs], k_s[rs, cs], b_s[rs, cs], i_ref[rs, cs]
            st = st_ref[hp]
            sh_ref[hp, 0, c] = st
            o = lax.dot_general((q * jnp.exp(bc)).astype(BF16), st.astype(BF16), NT_DIMS, preferred_element_type=F32)
            w = jnp.concatenate([q[_ROW0[s]:_ROW1[s], :] * _pair_rows(bc, s) * k[s:s + 1, :] for s in range(CH)], axis=0)
            a = jnp.dot(w.astype(BF16), ones, preferred_element_type=F32)
            qe1, _, ke0, _ = _cross_half(q, k, bc)
            s10 = lax.dot_general(qe1.astype(BF16), ke0.astype(BF16), NT_DIMS, preferred_element_type=F32)
            kd = k * jnp.exp(bc[CH - 1:CH, :] - bc)
            st_new = lax.dot_general(v.astype(BF16), kd.astype(BF16), TN_DIMS, preferred_element_type=F32)
            return o, a, s10, st_new

        def advance_state(c, hp, st_new):
            bl = b_s[_chunk_rows(c, CH - SUBLANES, SUBLANES), hp * REC_K:(hp + 1) * REC_K][SUBLANES - 1:, :]
            st_ref[hp] = st_ref[hp] * jnp.exp(bl) + st_new

        def cross(c, hp, s10):
            v0 = i_ref[_chunk_rows(c, 0, HALF), hp * REC_K:(hp + 1) * REC_K]
            return jnp.dot(s10.astype(BF16), v0.astype(BF16), preferred_element_type=F32)

        def finish(c, hp, o, a, o_cross):
            rs, cs = rows_of(c), slice(hp * REC_K, (hp + 1) * REC_K)
            v = i_ref[rs, cs]
            acc = [jnp.zeros((SUBLANES, REC_K), F32) for _ in range(CH // SUBLANES)]
            for s in range(CH):
                j = s // SUBLANES
                vs = v[s:s + 1, :]
                for jj in range(j, _ROW1[s] // SUBLANES):
                    blk = a[_OFF_OF[s] + (jj - j) * SUBLANES:_OFF_OF[s] + (jj - j + 1) * SUBLANES, :]
                    if jj == j:
                        blk = jnp.where(sub >= s - j * SUBLANES, blk, 0.0)
                    acc[jj] = acc[jj] + blk * vs
            o_s[rs, cs] = o + jnp.concatenate(acc, axis=0) + jnp.concatenate([jnp.zeros((HALF, REC_K), F32), o_cross], axis=0)

        def park(slot, results):
            for hp, (o, a, s10, st_new) in enumerate(results):
                car_o[slot, hp], car_a[slot, hp], car_s[slot, hp], car_st[slot, hp] = o, a, s10, st_new

        def retire(c, slot):
            for hp in range(HP):
                advance_state(c, hp, car_st[slot, hp])
            yield
            crosses = [cross(c, hp, car_s[slot, hp]) for hp in range(HP)]
            for hp in range(HP):
                finish(c, hp, car_o[slot, hp], car_a[slot, hp], crosses[hp])

        def step(c, slot):
            closing = retire(c - 1, slot)
            next(closing)
            park(1 - slot, [issue(c, hp) for hp in range(HP)])
            next(closing, None)

        def trip(j, carry):
            step(2 * j + 1, 0)
            step(2 * j + 2, 1)
            return carry

        assert nck % 2 == 0
        park(0, [issue(0, hp) for hp in range(HP)])
        lax.fori_loop(0, nck // 2 - 1, trip, 0)
        step(nck - 1, 0)
        for _ in retire(nck - 1, 1):
            pass
        oraw_ref[...] = o_s[...]
        for hp in range(HP):
            cs = slice(hp * REC_K, (hp + 1) * REC_K)
            o, zc = o_s[:, cs], z_ref[:, cs]
            on = (o * lax.rsqrt(jnp.mean(o * o, axis=-1, keepdims=True) + NORM_EPS)) * gw_ref[...]
            g_ref[:, cs] = (on * (zc * _sigmoid(zc))).astype(BF16)

    blk, head, lbs, gws, hist = _hgrn_specs(tb, nt, False)
    return pl.pallas_call(
        body, name="hgrn_fwd", grid=(REC_HEADS // HP, b_loc, nt),
        in_specs=[blk(0), blk(1), blk(2), blk(3), lbs, gws], out_specs=[head, head, hist],
        out_shape=[jax.ShapeDtypeStruct((n, 1024), F32), jax.ShapeDtypeStruct((n, 1024), BF16),
                   jax.ShapeDtypeStruct((REC_HEADS, b_loc, t_len // CH, REC_K, REC_K), F32)],
        scratch_shapes=[pltpu.VMEM((tb, _HW), F32)] * 4 + [pltpu.VMEM((HP, REC_K, REC_K), F32)] + [
            pltpu.VMEM((2, HP, CH, REC_K), F32), pltpu.VMEM((2, HP, sum(_ROWS_OF), REC_K), F32),
            pltpu.VMEM((2, HP, HALF, HALF), F32), pltpu.VMEM((2, HP, REC_K, REC_K), F32)],
        compiler_params=_params(("arbitrary", "arbitrary", "arbitrary"), 48),
    )(p, p, p, p, lb, gw)


def _hgrn_bwd(p, lb, gw, oraw, sh, dg, b_loc, t_len):
    n = p.shape[0]
    tb = min(REC_TB, t_len)
    nt, nck = t_len // tb, tb // CH
    assert HP == REC_HEADS

    def body(qp_ref, fp_ref, i_ref, z_ref, lb_ref, gw_ref, oraw_ref, dg_ref, sh_ref,
             dp_ref, dlb_ref, dgw_ref,
             q_s, k_s, b_s, do_s, dqv_s, dk_s, db_s, dst_ref, car_r, car_dst, car_dec, car_a, car_da, car_x):
        dq_ref, df_ref, di_ref, dz_ref = (dp_ref.at[:, part * 1024:(part + 1) * 1024] for part in range(4))
        b, t = pl.program_id(1), pl.program_id(2)

        @pl.when(t == 0)
        def _():
            dst_ref[...] = jnp.zeros_like(dst_ref)

        @pl.when((b == 0) & (t == 0))
        def _():
            dlb_ref[...] = jnp.zeros_like(dlb_ref)
            dgw_ref[...] = jnp.zeros_like(dgw_ref)

        lbv, qp, fp = lb_ref[...], qp_ref[...], fp_ref[...]
        qv, g, kk, sig_neg = _gates(qp, fp, lbv)
        q_s[...] = qv
        k_s[...] = kk
        _cumsum_chunks(_tri(True), g, b_s, tb)
        gwv = gw_ref[...]
        for hp in range(HP):
            cs = slice(hp * REC_K, (hp + 1) * REC_K)
            o, zc, dgv = oraw_ref[:, cs], z_ref[:, cs], dg_ref[:, cs]
            rn = lax.rsqrt(jnp.mean(o * o, axis=-1, keepdims=True) + NORM_EPS)
            on = o * rn
            sgz = _sigmoid(zc)
            dz_ref[:, cs] = (dgv * (on * gwv) * (sgz * (1.0 + zc * (1.0 - sgz)))).astype(BF16)
            dpre = dgv * (zc * sgz)
            dgw_ref[hp] += jnp.sum(dpre * on, axis=0, keepdims=True)
            don = dpre * gwv
            do_s[:, cs] = rn * (don - on * jnp.mean(don * on, axis=-1, keepdims=True))

        ones = jnp.ones((REC_K, REC_K), BF16)
        sub = lax.broadcasted_iota(jnp.int32, (SUBLANES, REC_K), 0)
        rowid = lax.broadcasted_iota(jnp.int32, (CH, REC_K), 0)
        ngrp = CH // SUBLANES
        piece_row = lax.broadcasted_iota(jnp.int32, (1, sum(_ROWS_OF)), 1)
        key_of = jnp.zeros((1, sum(_ROWS_OF)), jnp.int32)
        for s in range(1, CH):
            key_of = jnp.where(piece_row >= _OFF_OF[s], s, key_of)
        pick = (key_of == lax.broadcasted_iota(jnp.int32, (CH, sum(_ROWS_OF)), 0)).astype(BF16)

        def operands(c, hp):
            rs, cs = _chunk_rows(c), slice(hp * REC_K, (hp + 1) * REC_K)
            return rs, cs, q_s[rs, cs], k_s[rs, cs], b_s[rs, cs], i_ref[rs, cs], do_s[rs, cs]

        def issue(c, hp, slot):
            _, _, q, k, bc, v, do = operands(c, hp)
            st, dst = sh_ref[hp, 0, c], dst_ref[hp]
            qe, kd = q * jnp.exp(bc), k * jnp.exp(bc[CH - 1:CH, :] - bc)
            do_bf, dst_bf = do.astype(BF16), dst.astype(BF16)
            car_r[slot, hp, 0:CH] = jnp.dot(do_bf, st.astype(BF16), preferred_element_type=F32)
            car_r[slot, hp, CH:2 * CH] = jnp.dot(v.astype(BF16), dst_bf, preferred_element_type=F32)
            car_r[slot, hp, 2 * CH:3 * CH] = lax.dot_general(kd.astype(BF16), dst_bf, NT_DIMS, preferred_element_type=F32)
            car_dst[slot, hp] = lax.dot_general(do_bf, qe.astype(BF16), TN_DIMS, preferred_element_type=F32)
            dec = jnp.concatenate([_pair_rows(bc, s) for s in range(CH)], axis=0)
            qk = jnp.concatenate([q[_ROW0[s]:_ROW1[s], :] * k[s:s + 1, :] for s in range(CH)], axis=0)
            x = jnp.concatenate([do[_ROW0[s]:_ROW1[s], :] * v[s:s + 1, :] for s in range(CH)], axis=0)
            car_dec[slot, hp] = dec
            car_a[slot, hp] = jnp.dot((qk * dec).astype(BF16), ones, preferred_element_type=F32)
            car_da[slot, hp] = jnp.dot(x.astype(BF16), ones, preferred_element_type=F32)
            qe1, _, ke0, _ = _cross_half(q, k, bc)
            qe1_bf, ke0_bf = qe1.astype(BF16), ke0.astype(BF16)
            do1_bf, v0_bf = do[HALF:, :].astype(BF16), v[:HALF, :].astype(BF16)
            car_x[slot, hp, 0:HALF] = lax.dot_general(ke0_bf, qe1_bf, NT_DIMS, preferred_element_type=F32)
            car_x[slot, hp, HALF:2 * HALF] = lax.dot_general(do1_bf, v0_bf, NT_DIMS, preferred_element_type=F32)
            car_x[slot, hp, 2 * HALF:3 * HALF] = lax.dot_general(v0_bf, do1_bf, NT_DIMS, preferred_element_type=F32)

        def advance_state(c, hp, slot):
            ebl = jnp.exp(b_s[_chunk_rows(c, CH - SUBLANES, SUBLANES), hp * REC_K:(hp + 1) * REC_K][SUBLANES - 1:, :])
            st, dst = sh_ref[hp, 0, c], dst_ref[hp]
            dst_ref[hp] = dst * ebl + car_dst[slot, hp]
            return ebl * jnp.sum(st * dst, axis=0, keepdims=True)

        def cross(c, hp, slot):
            _, _, q, k, bc, v, do = operands(c, hp)
            qe1, _, ke0, _ = _cross_half(q, k, bc)
            xs = car_x[slot, hp]
            dqe1 = jnp.dot(xs[HALF:2 * HALF].astype(BF16), ke0.astype(BF16), preferred_element_type=F32)
            dke0 = jnp.dot(xs[2 * HALF:].astype(BF16), qe1.astype(BF16), preferred_element_type=F32)
            dv1 = jnp.dot(xs[:HALF].astype(BF16), do[HALF:, :].astype(BF16), preferred_element_type=F32)
            return dqe1, dke0, dv1

        def retire(c, slot):
            dbl_state = [advance_state(c, hp, slot) for hp in range(HP)]
            yield
            crossed = [cross(c, hp, slot) for hp in range(HP)]
            for hp in range(HP):
                finish(c, hp, slot, dbl_state[hp], *crossed[hp])

        def step(c, slot):
            closing = retire(c + 1, slot)
            next(closing)
            for hp in range(HP):
                issue(c, hp, 1 - slot)
            next(closing, None)

        def trip(j, carry):
            step(nck - 2 - 2 * j, 0)
            step(nck - 3 - 2 * j, 1)
            return carry

        def finish(c, hp, slot, dbl_state, dqe1, dke0, dv1):
            rs, cs, q, k, bc, v, do = operands(c, hp)
            eb, ekd = jnp.exp(bc), jnp.exp(bc[CH - 1:CH, :] - bc)
            qe, kd = q * eb, k * ekd
            qe1, e1, ke0, e0 = _cross_half(q, k, bc)
            dqe, dkd, dv = car_r[slot, hp, 0:CH], car_r[slot, hp, CH:2 * CH], car_r[slot, hp, 2 * CH:3 * CH]
            a, da, decs = car_a[slot, hp], car_da[slot, hp], car_dec[slot, hp]
            dec = [decs[_OFF_OF[s]:_OFF_OF[s] + _ROWS_OF[s], :] for s in range(CH)]
            dbl = jnp.sum(dkd * kd, axis=0, keepdims=True) + dbl_state
            dq_acc = [jnp.zeros((SUBLANES, REC_K), F32) for _ in range(ngrp)]
            uk, uv = [], []
            for s in range(CH):
                j = s // SUBLANES
                r0 = j * SUBLANES
                ks = k[s:s + 1, :]
                for jj in range(j, _ROW1[s] // SUBLANES):
                    lo, hi = _OFF_OF[s] + (jj - j) * SUBLANES, _OFF_OF[s] + (jj - j + 1) * SUBLANES
                    a_blk, da_blk = a[lo:hi, :], da[lo:hi, :]
                    if jj == j:
                        keep = sub >= s - r0
                        a_blk, da_blk = jnp.where(keep, a_blk, 0.0), jnp.where(keep, da_blk, 0.0)
                    rows = slice(jj * SUBLANES, (jj + 1) * SUBLANES)
                    tt = da_blk * dec[s][(jj - j) * SUBLANES:(jj - j + 1) * SUBLANES, :]
                    dq_acc[jj] = dq_acc[jj] + tt * ks
                    uk.append(tt * q[rows, :])
                    uv.append(a_blk * do[rows, :])
            dk_in = jnp.dot(pick, jnp.concatenate(uk, axis=0).astype(BF16), preferred_element_type=F32)
            dv_in = jnp.dot(pick, jnp.concatenate(uv, axis=0).astype(BF16), preferred_element_type=F32)
            zero_half = jnp.zeros((HALF, REC_K), F32)
            dq_x = jnp.concatenate([zero_half, dqe1 * e1], axis=0)
            dk_x = jnp.concatenate([dke0 * e0, zero_half], axis=0)
            dv_x = jnp.concatenate([dv1, zero_half], axis=0)
            db_x = jnp.concatenate([-(dke0 * ke0), dqe1 * qe1], axis=0)
            dq_in = jnp.concatenate(dq_acc, axis=0)
            dqv_s[rs, cs] = dqe * eb + dq_in + dq_x
            dk_s[rs, cs] = dkd * ekd + dk_in + dk_x
            di_ref[rs, cs] = (dv + dv_in + dv_x).astype(BF16)
            db = dqe * qe - dkd * kd + q * dq_in - k * dk_in + db_x
            db_s[rs, cs] = db + jnp.where(rowid == CH - 1, dbl, 0.0)

        assert nck % 2 == 0
        for hp in range(HP):
            issue(nck - 1, hp, 0)
        lax.fori_loop(0, nck // 2 - 1, trip, 0)
        step(0, 0)
        for _ in retire(0, 1):
            pass
        up = _tri(False)
        sgq = _sigmoid(qp)
        dq_ref[...] = (dqv_s[...] * (sgq * (1.0 + qp * (1.0 - sgq)))).astype(BF16)
        dlb_acc = jnp.zeros((1, _HW), F32)
        for r in range(0, tb, _CUM_ROWS):
            rows = slice(r, r + _CUM_ROWS)
            dgl = _dot3(up, db_s[rows, :])
            dfg = dgl * jnp.exp(-g[rows, :]) - dk_s[rows, :]
            sn = sig_neg[rows, :]
            df_ref[rows, :] = (dfg * (1.0 - lbv) * (1.0 - sn) * sn).astype(BF16)
            dlb_acc = dlb_acc + jnp.sum(dfg * sn, axis=0, keepdims=True)
        dlb_ref[...] += dlb_acc

    blk, head, lbs, gws, hist = _hgrn_specs(tb, nt, True)
    out_specs = [pl.BlockSpec((tb, REC_IN), lambda h, b, t: (b * nt + nt - 1 - t, 0)), lbs,
                 pl.BlockSpec((HP, 1, REC_K), lambda h, b, t: (h, 0, 0))]
    out_shape = [jax.ShapeDtypeStruct((n, REC_IN), BF16),
                 jax.ShapeDtypeStruct((1, 1024), F32), jax.ShapeDtypeStruct((REC_HEADS, 1, REC_K), F32)]
    return pl.pallas_call(
        body, name="hgrn_bwd", grid=(REC_HEADS // HP, b_loc, nt),
        in_specs=[blk(0), blk(1), blk(2), blk(3), lbs, gws, head, head, hist],
        out_specs=out_specs, out_shape=out_shape,
        scratch_shapes=[pltpu.VMEM((tb, _HW), F32)] * 7 + [pltpu.VMEM((HP, REC_K, REC_K), F32)] + [
            pltpu.VMEM((2, HP, 3 * CH, REC_K), F32), pltpu.VMEM((2, HP, REC_K, REC_K), F32)] + [
            pltpu.VMEM((2, HP, sum(_ROWS_OF), REC_K), F32)] * 3 + [pltpu.VMEM((2, HP, 3 * HALF, HALF), F32)],
        compiler_params=_params(("arbitrary", "arbitrary", "arbitrary"), 56),
    )(p, p, p, p, lb, gw, oraw, dg, sh)


def _postnorm_bwd_nt(dxo, y, qw, w, name):
    n = dxo.shape[0]

    def body(dx_ref, y_ref, qw_ref, w_ref, dg_ref, dy_ref, dqw_ref, db_ref):
        @pl.when(pl.program_id(0) == 0)
        def _():
            dqw_ref[...] = jnp.zeros_like(dqw_ref)
            db_ref[...] = jnp.zeros_like(db_ref)

        yv, dxv = y_ref[...], dx_ref[...]
        r = lax.rsqrt(jnp.mean(yv * yv, axis=-1, keepdims=True) + NORM_EPS)
        u = yv * r
        du = dxv * qw_ref[...]
        dy = r * (du - u * jnp.mean(du * u, axis=-1, keepdims=True))
        dqw_ref[...] += jnp.sum(dxv * u, axis=0, keepdims=True)
        db_ref[...] += jnp.sum(dy, axis=0, keepdims=True)
        dyb = dy.astype(BF16)
        dy_ref[...] = dyb
        dg_ref[...] = lax.dot_general(dyb, w_ref[...], NT_DIMS, preferred_element_type=F32)

    rows = pl.BlockSpec((TM, D_MODEL), lambda i: (i, 0))
    const = lambda shape: pl.BlockSpec(shape, lambda i: (0, 0))
    return pl.pallas_call(
        body, name=name, grid=(n // TM,), in_specs=[rows, rows, const((1, D_MODEL)), const((D_MODEL, D_MODEL))],
        out_specs=[rows, rows, const((1, D_MODEL)), const((1, D_MODEL))],
        out_shape=[jax.ShapeDtypeStruct((n, D_MODEL), F32), jax.ShapeDtypeStruct((n, D_MODEL), BF16),
                   jax.ShapeDtypeStruct((1, D_MODEL), F32), jax.ShapeDtypeStruct((1, D_MODEL), F32)],
        compiler_params=_params(("arbitrary",), 48),
    )(dxo, y, qw, w)


def _nt_prenorm_bwd(dps, w, x, pw, dxo, has_bias, name, parts=()):
    n = x.shape[0]
    widths = [d.shape[1] for d in dps]
    m = sum(widths)
    npieces, nparts, steps = len(dps), len(parts), n // TM

    def body(*refs):
        dp_refs = refs[:npieces]
        w_ref, x_ref, pw_ref, dxo_ref = refs[npieces:npieces + 4]
        part_refs = refs[npieces + 4:npieces + 4 + nparts]
        dx_ref, dpw_ref, db_ref = refs[npieces + 4 + nparts:npieces + 7 + nparts]
        land_refs = refs[npieces + 7 + nparts:npieces + 7 + 2 * nparts]
        sems = refs[npieces + 7 + 2 * nparts:]

        @pl.when(pl.program_id(0) == 0)
        def _():
            dpw_ref[...] = jnp.zeros_like(dpw_ref)
            db_ref[...] = jnp.zeros_like(db_ref)
            if nparts:
                _scatter_start(part_refs, land_refs, sems)

        dh = jnp.zeros((TM, D_MODEL), F32)
        off = 0
        for dp_ref, wd in zip(dp_refs, widths):
            cn = _col_chunk(wd)
            for j in range(0, wd, cn):
                dpc = dp_ref[:, j:j + cn]
                if has_bias:
                    db_ref[:, off + j:off + j + cn] += jnp.sum(dpc, axis=0, keepdims=True)
                dh = dh + lax.dot_general(dpc.astype(BF16), w_ref[:, off + j:off + j + cn], NT_DIMS, preferred_element_type=F32)
            off += wd
        xv = x_ref[...]
        r = lax.rsqrt(jnp.mean(xv * xv, axis=-1, keepdims=True) + NORM_EPS)
        xn = xv * r
        dpw_ref[...] += jnp.sum(dh * xn, axis=0, keepdims=True)
        dxn = dh * pw_ref[...]
        dx_ref[...] = dxo_ref[...] + r * (dxn - xn * jnp.mean(dxn * xn, axis=-1, keepdims=True))

        if nparts:
            @pl.when(pl.program_id(0) == steps - 1)
            def _():
                _scatter_wait(part_refs, land_refs, sems)

    rows = pl.BlockSpec((TM, D_MODEL), lambda i: (i, 0))
    const = lambda shape: pl.BlockSpec(shape, lambda i: (0, 0))
    hbm = pl.BlockSpec(memory_space=pl.ANY)
    in_specs = ([pl.BlockSpec((TM, wd), lambda i: (i, 0)) for wd in widths] + [const((D_MODEL, m)), rows, const((1, D_MODEL)), rows]
                + [hbm] * nparts)
    return pl.pallas_call(
        body, name=name, grid=(steps,), in_specs=in_specs,
        out_specs=[rows, const((1, D_MODEL)), const((1, m))] + [hbm] * nparts,
        out_shape=[jax.ShapeDtypeStruct((n, D_MODEL), F32), jax.ShapeDtypeStruct((1, D_MODEL), F32),
                   jax.ShapeDtypeStruct((1, m), F32)] + _scatter_lands(parts),
        scratch_shapes=_scatter_sems(nparts) if nparts else [],
        compiler_params=_params(("arbitrary",), 56),
    )(*dps, w, x, pw, dxo, *parts)


def _matmul_tn(a, b, name):
    n, k = a.shape
    m = b.shape[1]
    tk, tm, tn = k, _col_chunk(m), 1024 if n % 1024 == 0 else n

    def body(a_ref, b_ref, o_ref):
        @pl.when(pl.program_id(2) == 0)
        def _():
            o_ref[...] = jnp.zeros_like(o_ref)

        o_ref[...] += lax.dot_general(a_ref[...], b_ref[...].astype(BF16), TN_DIMS, preferred_element_type=F32)

    return pl.pallas_call(
        body, name=name, grid=(k // tk, m // tm, n // tn),
        in_specs=[pl.BlockSpec((tn, tk), lambda i, j, l: (l, i)), pl.BlockSpec((tn, tm), lambda i, j, l: (l, j))],
        out_specs=pl.BlockSpec((tk, tm), lambda i, j, l: (i, j)),
        out_shape=jax.ShapeDtypeStruct((k, m), F32),
        compiler_params=_params(("arbitrary", "arbitrary", "arbitrary"), 48),
    )(a, b)


def _matmul_tn_by_owner(a, b, name):
    n, k = a.shape
    c = b.shape[1] // N_DEV
    tn = 1024 if n % 1024 == 0 else n
    steps = n // tn
    per = 2

    def body(a_ref, b_ref, o_ref, w_ref):
        @pl.when(pl.program_id(1) == 0)
        def _():
            o_ref[...] = jnp.zeros_like(o_ref)

        r = lax.dot_general(a_ref[...], b_ref[...], TN_DIMS, preferred_element_type=F32)
        for j in range(per):
            o_ref[j] += r[:, j * c:(j + 1) * c]

        @pl.when(pl.program_id(1) == steps - 1)
        def _():
            w_ref[...] = o_ref[...].astype(BF16)

    out = pl.BlockSpec((per, k, c), lambda j, l: (j, 0, 0))
    return pl.pallas_call(
        body, name=name, grid=(N_DEV // per, steps),
        in_specs=[pl.BlockSpec((tn, k), lambda j, l: (l, 0)), pl.BlockSpec((tn, per * c), lambda j, l: (l, j))],
        out_specs=[out, out],
        out_shape=[jax.ShapeDtypeStruct((N_DEV, k, c), F32), jax.ShapeDtypeStruct((N_DEV, k, c), BF16)],
        compiler_params=_params(("arbitrary", "arbitrary"), 48),
    )(a, b)


def _by_owner_cols(dw):
    k, m = dw.shape
    return dw.reshape(k, N_DEV, m // N_DEV).transpose(1, 0, 2)


def _own_and_bf16(part):
    return lax.dynamic_index_in_dim(part, _my_id(), 0, keepdims=False), part.astype(BF16)


def _step(x, pos_col, tgt, pre_w, post_w, wa_in, ba_in, sinks, wa_out_shard, ba_out, wr_in_shard, lb_logits, gnorm_w, wr_out_shard, b_loc, t_len):
    nb = t_len // BLK
    ct, st = _rope_tables(pos_col)
    lb = _lower_bound(lb_logits)
    p0, h0, ga_out = _norm_matmul(x, pre_w[0:1], wa_in, ba_in, "attn_in_proj", [wa_out_shard])
    wa_out = ga_out.reshape(ATTN_WIDTH, D_MODEL)
    o0, g0, gr_in, gr_out = _attn_fwd(p0, ct, st, sinks, b_loc, nb, [wr_in_shard, wr_out_shard])
    wr_in = gr_in.transpose(1, 0, 2).reshape(D_MODEL, REC_IN)
    wr_out = gr_out.reshape(1024, D_MODEL)
    y0, x1 = _outproj_postnorm(g0, wa_out, ba_out, x, post_w[0:1], "attn_out_proj")
    p1, h1 = _norm_matmul(x1, pre_w[1:2], wr_in, None, "rec_in_proj")
    o1, g1, sh = _hgrn_fwd(p1, lb, gnorm_w, b_loc, t_len)
    dx2, dg1, dy1, dpost1, loss_tile = _outproj_loss_bwd(g1, wr_out, x1, post_w[1:2], tgt, "rec_out_proj_loss_bwd")
    d_wr_out = _matmul_tn(g1, dy1, "rec_w_out_grad")
    dp1, dlb, dgw = _hgrn_bwd(p1, lb, gnorm_w, o1, sh, dg1, b_loc, t_len)
    dx1, dpre1, _ = _nt_prenorm_bwd([dp1], wr_in, x1, pre_w[1:2], dx2, False, "rec_in_bwd")
    part_r_in, wire_r_in = _matmul_tn_by_owner(h1, dp1, "rec_w_in_grad")
    own_r_in = lax.dynamic_index_in_dim(part_r_in, _my_id(), 0, keepdims=False)
    dg0, dy0, dpost0, dba_out = _postnorm_bwd_nt(dx1, y0, post_w[0:1], wa_out, "attn_out_bwd")
    d_wa_out = _matmul_tn(g0, dy0, "attn_w_out_grad")
    owns, wires = zip(*[_own_and_bf16(part) for part in (
        d_wr_out.reshape(N_DEV, 1024 // N_DEV, D_MODEL), d_wa_out.reshape(N_DEV, ATTN_WIDTH // N_DEV, D_MODEL))])
    owns, wires = (own_r_in,) + owns, (wire_r_in,) + wires
    dp0, dsink_tile, *lands = _attn_bwd(p0, ct, st, sinks, o0, dg0, b_loc, nb, list(wires))
    d_wa_in = _matmul_tn(h0, dp0, "attn_w_in_grad")
    own_a_in, wire_a_in = _own_and_bf16(_by_owner_cols(_qkvz(d_wa_in)))
    dx0, dpre0, dba_in, land_a_in = _nt_prenorm_bwd([dp0], wa_in, x, pre_w[0:1], dx1, True, "attn_in_bwd", [wire_a_in])
    small = dict(pre=jnp.concatenate([dpre0, dpre1], axis=0), post=jnp.concatenate([dpost0, dpost1], axis=0),
                 ba_in=dba_in, sinks=dsink_tile[0:1, 0:N_HEADS], ba_out=dba_out, lb=dlb, gnorm=jnp.sum(dgw, axis=0))
    return loss_tile, dx0, list(zip(lands, owns)) + [(land_a_in, own_a_in)], small


def _my_id():
    return lax.axis_index("x") * 4 + lax.axis_index("y") * 2 + lax.axis_index("c")


def _peer(k):
    x, y, c = lax.axis_index("x"), lax.axis_index("y"), lax.axis_index("c")
    return (x ^ ((k >> 2) & 1), y ^ ((k >> 1) & 1), c ^ (k & 1))


def _peer_id(k):
    return _my_id() ^ k


def _all_gather_by_chip(shard):
    def body(x_ref, out_ref, send_sems, recv_sems, local_sem):
        x, y, c = lax.axis_index("x"), lax.axis_index("y"), lax.axis_index("c")
        me, sibling = (x, y, c), (x, y, 1 - c)
        chips = [(1 - x, y), (x, 1 - y), (1 - x, 1 - y)]

        def rows(px, py, pc):
            return out_ref.at[4 * px + 2 * py + pc]

        def copy(k, block, to, src=None):
            return pltpu.make_async_remote_copy(src_ref=rows(*block) if src is None else src, dst_ref=rows(*block),
                                                send_sem=send_sems.at[k], recv_sem=recv_sems.at[k], device_id=to, device_id_type=MESH)

        mine = pltpu.make_async_copy(x_ref, rows(*me), local_sem)
        mine.start()
        first = [copy(0, me, sibling, src=x_ref)] + [copy(1 + j, me, (*chip, c), src=x_ref) for j, chip in enumerate(chips)]
        for cp in first:
            cp.start()
        passed = [copy(4 + j, (*chip, c), sibling) for j, chip in enumerate(chips)]
        for j, chip in enumerate(chips):
            copy(1 + j, (*chip, c), me).wait_recv()
            passed[j].start()
        copy(0, sibling, me).wait_recv()
        for j, chip in enumerate(chips):
            copy(4 + j, (*chip, 1 - c), me).wait_recv()
        for cp in first + passed:
            cp.wait_send()
        mine.wait()

    hbm = pl.BlockSpec(memory_space=pl.ANY)
    return pl.pallas_call(
        body, name="comm_all_gather_by_chip", in_specs=[hbm], out_specs=hbm,
        out_shape=jax.ShapeDtypeStruct((N_DEV,) + shard.shape, shard.dtype),
        scratch_shapes=[pltpu.SemaphoreType.DMA((N_DEV - 1,)), pltpu.SemaphoreType.DMA((N_DEV - 1,)), pltpu.SemaphoreType.DMA],
    )(shard)


def _gather_shapes(shards):
    return [jax.ShapeDtypeStruct((N_DEV,) + s.shape, s.dtype) for s in shards]


def _gather_sems(nsh):
    return [pltpu.SemaphoreType.DMA((nsh, N_DEV - 1)), pltpu.SemaphoreType.DMA((nsh, N_DEV - 1)), pltpu.SemaphoreType.DMA((nsh,))]


def _gather_copies(ins, outs, sems, received):
    send_sems, recv_sems, local_sems = sems
    me = _my_id()
    local = [pltpu.make_async_copy(ins[a], outs[a].at[me], local_sems.at[a]) for a in range(len(ins))]
    remote = [pltpu.make_async_remote_copy(
        src_ref=ins[a], dst_ref=outs[a].at[_peer_id(k) if received else me], send_sem=send_sems.at[a, k - 1],
        recv_sem=recv_sems.at[a, k - 1], device_id=_peer(k), device_id_type=MESH)
        for a in range(len(ins)) for k in range(1, N_DEV)]
    return local, remote


def _gather_start(ins, outs, sems):
    local, sends = _gather_copies(ins, outs, sems, False)
    for cp in local + sends:
        cp.start()


def _gather_wait(ins, outs, sems):
    local, recvs = _gather_copies(ins, outs, sems, True)
    for cp in recvs:
        cp.wait_recv()
    for cp in recvs:
        cp.wait_send()
    for cp in local:
        cp.wait()


def _scatter_lands(parts):
    return [jax.ShapeDtypeStruct((N_DEV - 1,) + p.shape[1:], p.dtype) for p in parts]


def _scatter_sems(nparts):
    return [pltpu.SemaphoreType.DMA((nparts, N_DEV - 1)), pltpu.SemaphoreType.DMA((nparts, N_DEV - 1))]


def _scatter_copies(parts, lands, sems):
    send_sems, recv_sems = sems
    return [pltpu.make_async_remote_copy(
        src_ref=parts[a].at[_peer_id(k)], dst_ref=lands[a].at[k - 1], send_sem=send_sems.at[a, k - 1],
        recv_sem=recv_sems.at[a, k - 1], device_id=_peer(k), device_id_type=MESH)
        for a in range(len(parts)) for k in range(1, N_DEV)]


def _scatter_start(parts, lands, sems):
    for cp in _scatter_copies(parts, lands, sems):
        cp.start()


def _scatter_wait(parts, lands, sems):
    copies = _scatter_copies(parts, lands, sems)
    for cp in copies:
        cp.wait_recv()
    for cp in copies:
        cp.wait_send()


def _adamw(w, g, m, v):
    m2 = ADAM_B1 * m + (1.0 - ADAM_B1) * g
    v2 = ADAM_B2 * v + (1.0 - ADAM_B2) * (g * g)
    m_hat = m2 / (1.0 - ADAM_B1 ** ADAM_STEP)
    v_hat = v2 / (1.0 - ADAM_B2 ** ADAM_STEP)
    delta = -ADAM_LR * (m_hat / (jnp.sqrt(v_hat) + ADAM_EPS) + ADAM_WD * w)
    return delta, m2, v2


def _sum_adamw(land, own, w, m, v, name):
    r, c = own.shape
    rb = min(r, 256)

    def body(land_ref, own_ref, w_ref, m_ref, v_ref, g_ref, d_ref, m2_ref, v2_ref):
        me = _my_id()
        g = jnp.zeros((rb, c), F32)
        for dev in range(N_DEV):
            k = dev ^ me
            g = g + jnp.where(k == 0, own_ref[...], land_ref[jnp.maximum(k - 1, 0)].astype(F32))
        delta, m2, v2 = _adamw(w_ref[...], g, m_ref[...], v_ref[...])
        g_ref[...] = g
        d_ref[...] = delta
        m2_ref[...] = m2
        v2_ref[...] = v2

    rows = pl.BlockSpec((rb, c), lambda i: (i, 0))
    return pl.pallas_call(
        body, name=name, grid=(r // rb,), in_specs=[pl.BlockSpec((N_DEV - 1, rb, c), lambda i: (0, i, 0))] + [rows] * 4,
        out_specs=[rows] * 4, out_shape=[jax.ShapeDtypeStruct((r, c), F32)] * 4,
        compiler_params=_params(("arbitrary",), 32),
    )(land, own, w, m, v)


_SMALL = [("pre_norm_w", 2048), ("post_norm_w", 2048), ("attn_b_in", 2304), ("attn_sinks", 16), ("attn_b_out", 1024),
          ("rec_lb_logits", 2048), ("rec_gnorm_w", 128), ("loss", 1)]
_TILE = SUBLANES * LANES


def _small_rows(size):
    return -(-size // _TILE) * SUBLANES


_SMALL_OFF = {}
_r = 0
for _name, _size in _SMALL:
    _SMALL_OFF[_name] = _r
    _r += _small_rows(_size)
_SMALL_ROWS = _r


def _pack_small(pieces):
    out = []
    for name, size in _SMALL:
        flat = pieces[name].reshape(-1).astype(F32)
        out.append(jnp.pad(flat, (0, _small_rows(size) * LANES - size)).reshape(-1, LANES))
    return jnp.concatenate(out, axis=0)


def _unpack_small(packed, shapes):
    return {name: packed[_SMALL_OFF[name]:_SMALL_OFF[name] + _small_rows(size)].reshape(-1)[:size].reshape(shapes[name])
            for name, size in _SMALL}


def _small_allreduce_adamw(gpart, w, m, v):
    lb0 = _SMALL_OFF["rec_lb_logits"]

    def body(gp_ref, w_ref, m_ref, v_ref, g_ref, d_ref, m2_ref, v2_ref, land_ref, send_sems, recv_sems):
        me = _my_id()
        sent = []
        for k in range(1, N_DEV):
            cp = pltpu.make_async_remote_copy(src_ref=gp_ref, dst_ref=land_ref.at[k - 1], send_sem=send_sems.at[k - 1],
                                              recv_sem=recv_sems.at[k - 1], device_id=_peer(k), device_id_type=MESH)
            cp.start()
            sent.append(cp)
        for cp in sent:
            cp.wait_recv()
        for cp in sent:
            cp.wait_send()
        g = jnp.zeros((_SMALL_ROWS, LANES), F32)
        for dev in range(N_DEV):
            k = dev ^ me
            g = g + jnp.where(k == 0, gp_ref[...], land_ref[jnp.maximum(k - 1, 0)])
        g_ref[...] = g
        l0, l1 = w_ref[lb0:lb0 + SUBLANES, :], w_ref[lb0 + SUBLANES:lb0 + 2 * SUBLANES, :]
        mx = jnp.maximum(l0, l1)
        e0, e1 = jnp.exp(l0 - mx), jnp.exp(l1 - mx)
        p1 = e1 / (e0 + e1)
        dl1 = (1.0 - p1) * p1 * g[lb0:lb0 + SUBLANES, :]
        g_ref[lb0:lb0 + SUBLANES, :] = -dl1
        g_ref[lb0 + SUBLANES:lb0 + 2 * SUBLANES, :] = dl1
        delta, m2, v2 = _adamw(w_ref[...], g_ref[...], m_ref[...], v_ref[...])
        d_ref[...] = delta
        m2_ref[...] = m2
        v2_ref[...] = v2

    vmem = pl.BlockSpec(memory_space=pltpu.VMEM)
    return pl.pallas_call(
        body, name="comm_small_allreduce_adamw", in_specs=[vmem] * 4, out_specs=[vmem] * 4,
        out_shape=[jax.ShapeDtypeStruct((_SMALL_ROWS, LANES), F32)] * 4,
        scratch_shapes=[pltpu.VMEM((N_DEV - 1, _SMALL_ROWS, LANES), F32), pltpu.SemaphoreType.DMA((N_DEV - 1,)),
                        pltpu.SemaphoreType.DMA((N_DEV - 1,))],
    )(gpart, w, m, v)


def _qzkv(a):
    return jnp.concatenate([a[..., :1024], a[..., 1280:], a[..., 1024:1280]], axis=-1)


def _qkvz(a):
    return jnp.concatenate([a[..., :1024], a[..., 2048:], a[..., 1024:2048]], axis=-1)


def kernel(x, positions, pre_norm_w, post_norm_w, attn_w_in, attn_b_in, attn_sinks, attn_w_out, attn_b_out, rec_w_in, rec_lb_logits, rec_gnorm_w, rec_w_out, loss_target, m_pre_norm_w, m_post_norm_w, m_attn_w_in, m_attn_b_in, m_attn_sinks, m_attn_w_out, m_attn_b_out, m_rec_w_in, m_rec_lb_logits, m_rec_gnorm_w, m_rec_w_out, v_pre_norm_w, v_post_norm_w, v_attn_w_in, v_attn_b_in, v_attn_sinks, v_attn_w_out, v_attn_b_out, v_rec_w_in, v_rec_lb_logits, v_rec_gnorm_w, v_rec_w_out):
    b_loc, t_len, _ = x.shape
    n = b_loc * t_len
    ga_in = _all_gather_by_chip(attn_w_in[0].astype(BF16))
    wa_in = _qzkv(ga_in.transpose(1, 0, 2).reshape(D_MODEL, ATTN_IN))

    loss_tile, dx, landed, small = _step(
        x.reshape(n, D_MODEL), positions.reshape(n, 1).astype(F32), loss_target.reshape(n, D_MODEL),
        pre_norm_w, post_norm_w, wa_in, _qzkv(attn_b_in), attn_sinks, attn_w_out[0].astype(BF16), attn_b_out,
        rec_w_in[0].astype(BF16), rec_lb_logits, rec_gnorm_w, rec_w_out[0].astype(BF16), b_loc, t_len)

    lift = lambda outs: tuple(a[None] for a in outs)
    (l_r_in, o_r_in), (l_r_out, o_r_out), (l_a_out, o_a_out), (l_a_in, o_a_in) = landed
    r_a_in = lift(_sum_adamw(l_a_in, o_a_in, attn_w_in[0], m_attn_w_in[0], v_attn_w_in[0], "adamw_attn_w_in"))
    r_r_in = lift(_sum_adamw(l_r_in, o_r_in, rec_w_in[0], m_rec_w_in[0], v_rec_w_in[0], "adamw_rec_w_in"))
    r_r_out = lift(_sum_adamw(l_r_out, o_r_out, rec_w_out[0], m_rec_w_out[0], v_rec_w_out[0], "adamw_rec_w_out"))
    r_a_out = lift(_sum_adamw(l_a_out, o_a_out, attn_w_out[0], m_attn_w_out[0], v_attn_w_out[0], "adamw_attn_w_out"))

    gsmall = dict(pre_norm_w=small["pre"], post_norm_w=small["post"], attn_b_in=_qkvz(small["ba_in"]), attn_sinks=small["sinks"],
                  attn_b_out=small["ba_out"], rec_lb_logits=jnp.concatenate([small["lb"], jnp.zeros_like(small["lb"])], axis=0),
                  rec_gnorm_w=small["gnorm"], loss=loss_tile[0:1, 0:1])
    nil = jnp.zeros((1, 1), F32)
    wsmall = dict(pre_norm_w=pre_norm_w, post_norm_w=post_norm_w, attn_b_in=attn_b_in, attn_sinks=attn_sinks,
                  attn_b_out=attn_b_out, rec_lb_logits=rec_lb_logits, rec_gnorm_w=rec_gnorm_w, loss=nil)
    msmall = dict(pre_norm_w=m_pre_norm_w, post_norm_w=m_post_norm_w, attn_b_in=m_attn_b_in, attn_sinks=m_attn_sinks,
                  attn_b_out=m_attn_b_out, rec_lb_logits=m_rec_lb_logits, rec_gnorm_w=m_rec_gnorm_w, loss=nil)
    vsmall = dict(pre_norm_w=v_pre_norm_w, post_norm_w=v_post_norm_w, attn_b_in=v_attn_b_in, attn_sinks=v_attn_sinks,
                  attn_b_out=v_attn_b_out, rec_lb_logits=v_rec_lb_logits, rec_gnorm_w=v_rec_gnorm_w, loss=nil)
    shapes = {k: a.shape for k, a in wsmall.items()}
    packed = _small_allreduce_adamw(_pack_small(gsmall), _pack_small(wsmall), _pack_small(msmall), _pack_small(vsmall))
    sg, sd, sm, sv = [_unpack_small(a, shapes) for a in packed]

    big = {"attn_w_in": r_a_in, "attn_w_out": r_a_out, "rec_w_in": r_r_in, "rec_w_out": r_r_out}
    order = ["pre_norm_w", "post_norm_w", "attn_w_in", "attn_b_in", "attn_sinks", "attn_w_out", "attn_b_out", "rec_w_in",
             "rec_lb_logits", "rec_gnorm_w", "rec_w_out"]
    outs = [sg["loss"][0, 0], dx.reshape(b_loc, t_len, D_MODEL)]
    for idx, small_set in enumerate((sg, sd, sm, sv)):
        outs += [big[nm][idx] if nm in big else small_set[nm] for nm in order]
    return tuple(outs)
```

```python
import numpy as np
import jax
import jax.numpy as jnp
from jax import lax
from jax.experimental import pallas as pl
from jax.experimental.pallas import tpu as pltpu

F32, BF16 = jnp.float32, jnp.bfloat16
MESH = pl.DeviceIdType.MESH
N_DEV = 8

D_MODEL = 1024
N_HEADS, HEAD_DIM, N_KV, GROUP = 16, 64, 2, 8
ATTN_WIDTH, KV_WIDTH = 1024, 128
ATTN_IN = 2 * ATTN_WIDTH + 2 * KV_WIDTH
BLK = 128
ROPE_THETA, ROPE_HALF = 500000.0, 8
REC_HEADS, REC_K = 8, 128
REC_IN = 4 * 1024
CH = 32
NORM_EPS = 1e-6
F32_TINY = 1.1754944e-38
ADAM_LR, ADAM_B1, ADAM_B2, ADAM_EPS, ADAM_WD, ADAM_STEP = 0.001, 0.9, 0.999, 1e-08, 0.01, 10

LANES, SUBLANES = 128, 8
TM = 512
NT_DIMS = (((1,), (1,)), ((), ()))
TN_DIMS = (((0,), (0,)), ((), ()))
MB = 2 ** 20


def _params(sem=None, vmem_mb=48, **kw):
    return pltpu.CompilerParams(dimension_semantics=sem, vmem_limit_bytes=vmem_mb * MB, **kw)


def _col_chunk(m):
    return 768 if m % 1024 else 1024


def _sigmoid(x):
    return 1.0 / (1.0 + jnp.exp(-x))


def _split3(x):
    hi = x.astype(BF16)
    r1 = x - hi.astype(F32)
    mid = r1.astype(BF16)
    lo = (r1 - mid.astype(F32)).astype(BF16)
    return hi, mid, lo


def _dot3(l_bf, x):
    hi, mid, lo = _split3(x)
    return (jnp.dot(l_bf, hi, preferred_element_type=F32) + jnp.dot(l_bf, mid, preferred_element_type=F32)
            + jnp.dot(l_bf, lo, preferred_element_type=F32))


def _rope_tables(pos_col):
    n = pos_col.shape[0]
    lane = np.arange(LANES) % HEAD_DIM
    inv = np.float32(ROPE_THETA) ** (-(np.arange(ROPE_HALF, dtype=np.float32) * np.float32(2.0) / np.float32(2 * ROPE_HALF)))
    freq = np.where(lane < 2 * ROPE_HALF, inv[lane % ROPE_HALF], 0.0).astype(np.float32)[None, :]
    sign = np.where(lane < ROPE_HALF, -1.0, np.where(lane < 2 * ROPE_HALF, 1.0, 0.0)).astype(np.float32)[None, :]

    def body(p_ref, f_ref, s_ref, c_out, s_out):
        ang = p_ref[...] * f_ref[...]
        c_out[...] = jnp.cos(ang)
        s_out[...] = jnp.sin(ang) * s_ref[...]

    row = pl.BlockSpec((TM, 1), lambda i: (i, 0))
    vec = pl.BlockSpec((1, LANES), lambda i: (0, 0))
    out = pl.BlockSpec((TM, LANES), lambda i: (i, 0))
    return pl.pallas_call(
        body, name="rope_tables", grid=(n // TM,), in_specs=[row, vec, vec], out_specs=[out, out],
        out_shape=[jax.ShapeDtypeStruct((n, LANES), F32)] * 2, compiler_params=_params(("arbitrary",)),
    )(pos_col, jnp.asarray(freq), jnp.asarray(sign))


def _rope_apply(xv, c, s, lm):
    partner = jnp.where(lm < ROPE_HALF, pltpu.roll(xv, LANES - ROPE_HALF, 1), pltpu.roll(xv, ROPE_HALF, 1))
    return xv * c + partner * s


def _rope_bwd(dy, c, s, lm):
    t = dy * s
    partner = jnp.where(lm < ROPE_HALF, pltpu.roll(t, LANES - ROPE_HALF, 1),
                        jnp.where(lm < 2 * ROPE_HALF, pltpu.roll(t, ROPE_HALF, 1), 0.0))
    return dy * c + partner


def _lower_bound(lb_logits):
    def body(l_ref, o_ref):
        l0, l1 = l_ref[0:1, :], l_ref[1:2, :]
        m = jnp.maximum(l0, l1)
        e0, e1 = jnp.exp(l0 - m), jnp.exp(l1 - m)
        o_ref[...] = e1 / (e0 + e1)

    return pl.pallas_call(body, name="lower_bound", out_shape=jax.ShapeDtypeStruct((1, lb_logits.shape[1]), F32))(lb_logits)


def _norm_matmul(x, pw, w, bias, name, shards=()):
    n, m = x.shape[0], w.shape[1]
    cn = _col_chunk(m)
    has_bias = bias is not None
    nsh, steps = len(shards), n // TM

    def body(*refs):
        refs = list(refs)
        x_ref, pw_ref, w_ref = refs[:3]
        b_ref = refs[3] if has_bias else None
        refs = refs[4 if has_bias else 3:]
        sh_in, (p_ref, h_ref), sh_out, sems = refs[:nsh], refs[nsh:nsh + 2], refs[nsh + 2:2 * nsh + 2], refs[2 * nsh + 2:]
        if nsh:
            @pl.when(pl.program_id(0) == 0)
            def _():
                _gather_start(sh_in, sh_out, sems)

        xv = x_ref[...]
        r = lax.rsqrt(jnp.mean(xv * xv, axis=-1, keepdims=True) + NORM_EPS)
        h = ((xv * r) * pw_ref[...]).astype(BF16)
        h_ref[...] = h
        for j in range(0, m, cn):
            acc = jnp.dot(h, w_ref[:, j:j + cn], preferred_element_type=F32)
            if has_bias:
                acc = acc + b_ref[:, j:j + cn]
            p_ref[:, j:j + cn] = acc

        if nsh:
            @pl.when(pl.program_id(0) == steps - 1)
            def _():
                _gather_wait(sh_in, sh_out, sems)

    rows = pl.BlockSpec((TM, D_MODEL), lambda i: (i, 0))
    const = lambda shape: pl.BlockSpec(shape, lambda i: (0, 0))
    hbm = pl.BlockSpec(memory_space=pl.ANY)
    in_specs = [rows, const((1, D_MODEL)), const((D_MODEL, m))] + ([const((1, m))] if has_bias else []) + [hbm] * nsh
    args = (x, pw, w) + ((bias,) if has_bias else ()) + tuple(shards)
    return pl.pallas_call(
        body, name=name, grid=(steps,), in_specs=in_specs,
        out_specs=[pl.BlockSpec((TM, m), lambda i: (i, 0)), rows] + [hbm] * nsh,
        out_shape=[jax.ShapeDtypeStruct((n, m), F32), jax.ShapeDtypeStruct((n, D_MODEL), BF16)] + _gather_shapes(shards),
        scratch_shapes=_gather_sems(nsh) if nsh else [],
        compiler_params=_params(("arbitrary",), 56),
    )(*args)


def _outproj_postnorm(g, w, bias, xres, qw, name):
    n = g.shape[0]

    def body(g_ref, w_ref, b_ref, x_ref, qw_ref, y_ref, o_ref):
        y = jnp.dot(g_ref[...], w_ref[...], preferred_element_type=F32) + b_ref[...]
        y_ref[...] = y
        r = lax.rsqrt(jnp.mean(y * y, axis=-1, keepdims=True) + NORM_EPS)
        o_ref[...] = x_ref[...] + (y * r) * qw_ref[...]

    rows = pl.BlockSpec((TM, D_MODEL), lambda i: (i, 0))
    const = lambda shape: pl.BlockSpec(shape, lambda i: (0, 0))
    return pl.pallas_call(
        body, name=name, grid=(n // TM,),
        in_specs=[rows, const((D_MODEL, D_MODEL)), const((1, D_MODEL)), rows, const((1, D_MODEL))],
        out_specs=[rows, rows], out_shape=[jax.ShapeDtypeStruct((n, D_MODEL), F32)] * 2,
        compiler_params=_params(("arbitrary",), 48),
    )(g, w, bias, xres, qw)


def _outproj_loss_bwd(g, w, xres, qw, tgt, name):
    n = g.shape[0]
    steps = n // TM

    def body(g_ref, w_ref, x_ref, qw_ref, t_ref, dx_ref, dg_ref, dy_ref, dqw_ref, loss_ref, acc_ref):
        i = pl.program_id(0)

        @pl.when(i == 0)
        def _():
            acc_ref[...] = jnp.zeros_like(acc_ref)
            dqw_ref[...] = jnp.zeros_like(dqw_ref)

        y = jnp.dot(g_ref[...], w_ref[...], preferred_element_type=F32)
        r = lax.rsqrt(jnp.mean(y * y, axis=-1, keepdims=True) + NORM_EPS)
        u = y * r
        e = (x_ref[...] + u * qw_ref[...]) - t_ref[...]
        dxn = e * (1.0 / D_MODEL)
        dx_ref[...] = dxn
        acc_ref[...] += jnp.sum(e * e, axis=0, keepdims=True)
        du = dxn * qw_ref[...]
        dy = (r * (du - u * jnp.mean(du * u, axis=-1, keepdims=True))).astype(BF16)
        dqw_ref[...] += jnp.sum(dxn * u, axis=0, keepdims=True)
        dy_ref[...] = dy
        dg_ref[...] = lax.dot_general(dy, w_ref[...], NT_DIMS, preferred_element_type=F32)

        @pl.when(i == steps - 1)
        def _():
            loss_ref[...] = jnp.full(loss_ref.shape, jnp.sum(acc_ref[...]) * (0.5 / D_MODEL), F32)

    rows = pl.BlockSpec((TM, D_MODEL), lambda i: (i, 0))
    const = lambda shape: pl.BlockSpec(shape, lambda i: (0, 0))
    return pl.pallas_call(
        body, name=name, grid=(steps,),
        in_specs=[rows, const((D_MODEL, D_MODEL)), rows, const((1, D_MODEL)), rows],
        out_specs=[rows, rows, rows, const((1, D_MODEL)), const((SUBLANES, LANES))],
        out_shape=[jax.ShapeDtypeStruct((n, D_MODEL), F32), jax.ShapeDtypeStruct((n, D_MODEL), F32),
                   jax.ShapeDtypeStruct((n, D_MODEL), BF16), jax.ShapeDtypeStruct((1, D_MODEL), F32),
                   jax.ShapeDtypeStruct((SUBLANES, LANES), F32)],
        scratch_shapes=[pltpu.VMEM((1, D_MODEL), F32)], compiler_params=_params(("arbitrary",), 48),
    )(g, w, xres, qw, tgt)


_QCOL, _ZCOL, _KCOL, _VCOL = 0, 1024, 2048, 2176


def _head_stack(chunks, heads, lt64):
    return jnp.concatenate([jnp.where(lt64 if n % 2 == 0 else ~lt64, chunks[n // 2], 0.0) for n in heads], axis=0)


def _dup_half(x, h, lt64):
    r = pltpu.roll(x, HEAD_DIM, 1)
    return jnp.where(lt64, x, r) if h == 0 else jnp.where(lt64, r, x)


def _pair_chunk(xt, c2):
    a, b = 2 * c2, 2 * c2 + 1
    return jnp.concatenate([xt[:HEAD_DIM, a * BLK:(a + 1) * BLK], xt[HEAD_DIM:, b * BLK:(b + 1) * BLK]], axis=0).T


def _attn_mask_t(i):
    key = lax.broadcasted_iota(jnp.int32, (2 * BLK, BLK), 0)
    qry = lax.broadcasted_iota(jnp.int32, (2 * BLK, BLK), 1)
    valid = (key > qry) & (key <= qry + BLK) & ((key >= BLK) | (i > 0))
    return jnp.tile(jnp.where(valid, 0.0, -1e30), (1, GROUP))


def _attn_probs_t(s, heads, sink_ref, mask):
    s = s + mask
    head = lax.broadcasted_iota(jnp.int32, (1, len(heads) * BLK), 1) >> 7
    sk = jnp.zeros((1, len(heads) * BLK), F32)
    for j, n in enumerate(heads):
        sk = jnp.where(head == j, sink_ref[0, n], sk)
    m = jnp.maximum(jnp.max(s, axis=0, keepdims=True), sk)
    p = jnp.exp(s - m)
    esk = jnp.exp(sk - m)
    inv = 1.0 / (jnp.sum(p, axis=0, keepdims=True) + esk)
    return p * inv, esk * inv


def _attn_fwd(p, ct, st, sinks, b_loc, nb, shards):
    n = p.shape[0]
    nsh = len(shards)

    def body(sink_ref, q_ref, z_ref, kc_ref, kp_ref, vc_ref, vp_ref, cc_ref, sc_ref, cp_ref, sp_ref, *rest):
        sh_in, (o_ref, g_ref), sh_out, sems = rest[:nsh], rest[nsh:nsh + 2], rest[nsh + 2:2 * nsh + 2], rest[2 * nsh + 2:]
        b, i = pl.program_id(0), pl.program_id(1)

        @pl.when((b == 0) & (i == 0))
        def _():
            _gather_start(sh_in, sh_out, sems)

        lane = lax.broadcasted_iota(jnp.int32, (BLK, LANES), 1)
        lm = lane & (HEAD_DIM - 1)
        cc, sc = cc_ref[...], sc_ref[...]
        kcat = jnp.concatenate([_rope_apply(kp_ref[...], cp_ref[...], sp_ref[...], lm),
                                _rope_apply(kc_ref[...], cc, sc, lm)], axis=0)
        vcat = jnp.concatenate([vp_ref[...], vc_ref[...]], axis=0)
        qr = [_rope_apply(q_ref[:, c * LANES:(c + 1) * LANES], cc, sc, lm) * (HEAD_DIM ** -0.5) for c in range(8)]
        valid = _attn_mask_t(i)
        lt64, lt64k = lane < HEAD_DIM, lax.broadcasted_iota(jnp.int32, (2 * BLK, LANES), 1) < HEAD_DIM
        def kv_head(h):
            heads = list(range(h * GROUP, (h + 1) * GROUP))
            kext, vext = _dup_half(kcat, h, lt64k).astype(BF16), _dup_half(vcat, h, lt64k).astype(BF16)
            qst = _head_stack(qr, heads, lt64).astype(BF16)
            s = lax.dot_general(kext, qst, NT_DIMS, preferred_element_type=F32)
            yield
            pn, _ = _attn_probs_t(s, heads, sink_ref, valid)
            ot = lax.dot_general(vext, pn.astype(BF16), TN_DIMS, preferred_element_type=F32)
            yield
            for c2 in range(GROUP // 2):
                oc = _pair_chunk(ot, c2)
                cols = slice((4 * h + c2) * LANES, (4 * h + c2 + 1) * LANES)
                zc = z_ref[:, cols]
                o_ref[:, cols] = oc
                g_ref[:, cols] = (oc * (zc * _sigmoid(zc))).astype(BF16)

        _in_stages([kv_head(h) for h in range(N_KV)])

        @pl.when((b == b_loc - 1) & (i == nb - 1))
        def _():
            _gather_wait(sh_in, sh_out, sems)

    cur = lambda b, i: b * nb + i
    prev = lambda b, i: b * nb + jnp.maximum(i - 1, 0)
    wide = lambda cb: pl.BlockSpec((BLK, ATTN_WIDTH), lambda b, i: (cur(b, i), cb))
    kv = lambda rowf, cb: pl.BlockSpec((BLK, LANES), lambda b, i: (rowf(b, i), cb))
    hbm = pl.BlockSpec(memory_space=pl.ANY)
    in_specs = [pl.BlockSpec(memory_space=pltpu.SMEM), wide(0), wide(1),
                kv(cur, _KCOL // LANES), kv(prev, _KCOL // LANES), kv(cur, _VCOL // LANES), kv(prev, _VCOL // LANES),
                kv(cur, 0), kv(cur, 0), kv(prev, 0), kv(prev, 0)] + [hbm] * nsh
    return pl.pallas_call(
        body, name="attn_fwd", grid=(b_loc, nb), in_specs=in_specs, out_specs=[wide(0), wide(0)] + [hbm] * nsh,
        out_shape=[jax.ShapeDtypeStruct((n, ATTN_WIDTH), F32), jax.ShapeDtypeStruct((n, ATTN_WIDTH), BF16)] + _gather_shapes(shards),
        scratch_shapes=_gather_sems(nsh), compiler_params=_params(("arbitrary", "arbitrary"), 48),
    )(sinks, p, p, p, p, p, p, ct, st, ct, st, *shards)


def _attn_bwd(p, ct, st, sinks, o, dg, b_loc, nb, parts):
    n = p.shape[0]
    nparts = len(parts)

    def body(sink_ref, q_ref, z_ref, kc_ref, kp_ref, vc_ref, vp_ref, cc_ref, sc_ref, cp_ref, sp_ref, o_ref, dg_ref, *rest):
        part_refs, (dp_ref, ds_ref), land_refs = rest[:nparts], rest[nparts:nparts + 2], rest[nparts + 2:2 * nparts + 2]
        dq_s, dz_s, dk_s, dv_s = rest[2 * nparts + 2:2 * nparts + 6]
        sems = rest[2 * nparts + 6:]
        b, i = pl.program_id(0), pl.program_id(1)

        @pl.when((b == 0) & (i == 0))
        def _():
            _scatter_start(part_refs, land_refs, sems)

        @pl.when((b == b_loc - 1) & (i == nb))
        def _():
            _scatter_wait(part_refs, land_refs, sems)

        lane = lax.broadcasted_iota(jnp.int32, (BLK, LANES), 1)
        lm = lane & (HEAD_DIM - 1)

        @pl.when((b == 0) & (i == 0))
        def _():
            ds_ref[...] = jnp.zeros_like(ds_ref)

        @pl.when(i < nb)
        def _compute():
            cc, sc = cc_ref[...], sc_ref[...]
            kcat = jnp.concatenate([_rope_apply(kp_ref[...], cp_ref[...], sp_ref[...], lm),
                                    _rope_apply(kc_ref[...], cc, sc, lm)], axis=0)
            vcat = jnp.concatenate([vp_ref[...], vc_ref[...]], axis=0)
            qr = [_rope_apply(q_ref[:, c * LANES:(c + 1) * LANES], cc, sc, lm) * (HEAD_DIM ** -0.5) for c in range(8)]
            valid = _attn_mask_t(i)
            lt64, lt64k = lane < HEAD_DIM, lax.broadcasted_iota(jnp.int32, (2 * BLK, LANES), 1) < HEAD_DIM
            do_chunks, doo_chunks, dz_chunks = [], [], []
            for c in range(8):
                cols = slice(c * LANES, (c + 1) * LANES)
                zc, oc, dgc = z_ref[:, cols], o_ref[:, cols], dg_ref[:, cols]
                sg = _sigmoid(zc)
                do_chunks.append(dgc * (zc * sg))
                dz_chunks.append(dgc * oc * (sg * (1.0 + zc * (1.0 - sg))))
                doo_chunks.append(do_chunks[c] * oc)
            dq_chunks = [None] * 8
            dk_h, dv_h, ds_parts = [None] * N_KV, [None] * N_KV, [None] * N_KV
            tile_lane = lax.broadcasted_iota(jnp.int32, (SUBLANES, LANES), 1)
            tile_row = lax.broadcasted_iota(jnp.int32, (SUBLANES, LANES), 0)
            ones8 = jnp.ones((SUBLANES, LANES), BF16)

            def kv_head(h):
                heads = list(range(h * GROUP, (h + 1) * GROUP))
                kext = _dup_half(kcat, h, lt64k)
                kext_bf, kext_t = kext.astype(BF16), kext.T.astype(BF16)
                vext = _dup_half(vcat, h, lt64k).astype(BF16)
                qst = _head_stack(qr, heads, lt64).astype(BF16)
                pn, psink = _attn_probs_t(lax.dot_general(kext_bf, qst, NT_DIMS, preferred_element_type=F32), heads, sink_ref, valid)
                do_bf = _head_stack(do_chunks, heads, lt64).astype(BF16)
                delta = sum(lax.dot_general(ones8, part, NT_DIMS, preferred_element_type=F32)
                            for part in _split3(_head_stack(doo_chunks, heads, lt64)))[0:1, :]
                dpt = lax.dot_general(vext, do_bf, NT_DIMS, preferred_element_type=F32)
                dst = (pn * (dpt - delta)).astype(BF16)
                sink_term = psink * delta
                ds_acc = jnp.zeros((SUBLANES, LANES), F32)
                for j, n in enumerate(heads):
                    val = -jnp.sum(sink_term[:, j * BLK:(j + 1) * BLK])
                    ds_acc = ds_acc + jnp.where((tile_lane == n) & (tile_row == 0), val, 0.0)
                ds_parts[h] = ds_acc
                dqt = jnp.dot(kext_t, dst, preferred_element_type=F32) * (HEAD_DIM ** -0.5)
                dk_ext = jnp.dot(dst, qst, preferred_element_type=F32)
                dv_ext = jnp.dot(pn.astype(BF16), do_bf, preferred_element_type=F32)
                dk_h[h] = dk_ext + pltpu.roll(dk_ext, HEAD_DIM, 1)
                dv_h[h] = dv_ext + pltpu.roll(dv_ext, HEAD_DIM, 1)
                for c2 in range(GROUP // 2):
                    dq_chunks[4 * h + c2] = _rope_bwd(_pair_chunk(dqt, c2), cc, sc, lm)

            for h in range(N_KV):
                kv_head(h)
            ds_ref[...] += ds_parts[0] + ds_parts[1]
            dk_full = jnp.where(lt64k, dk_h[0], dk_h[1])
            dv_full = jnp.where(lt64k, dv_h[0], dv_h[1])

            @pl.when(i >= 1)
            def _emit():
                dp_ref[:, _QCOL:_QCOL + ATTN_WIDTH] = dq_s[...]
                dp_ref[:, _ZCOL:_ZCOL + ATTN_WIDTH] = dz_s[...]
                dp_ref[:, _KCOL:_KCOL + KV_WIDTH] = _rope_bwd(dk_s[...] + dk_full[:BLK], cp_ref[...], sp_ref[...], lm)
                dp_ref[:, _VCOL:_VCOL + KV_WIDTH] = dv_s[...] + dv_full[:BLK]

            for c in range(8):
                dq_s[:, c * LANES:(c + 1) * LANES] = dq_chunks[c]
                dz_s[:, c * LANES:(c + 1) * LANES] = dz_chunks[c]
            dk_s[...] = dk_full[BLK:]
            dv_s[...] = dv_full[BLK:]

        @pl.when(i == nb)
        def _final():
            dp_ref[:, _QCOL:_QCOL + ATTN_WIDTH] = dq_s[...]
            dp_ref[:, _ZCOL:_ZCOL + ATTN_WIDTH] = dz_s[...]
            dp_ref[:, _KCOL:_KCOL + KV_WIDTH] = _rope_bwd(dk_s[...], cc_ref[...], sc_ref[...], lm)
            dp_ref[:, _VCOL:_VCOL + KV_WIDTH] = dv_s[...]

    cur = lambda b, i: b * nb + jnp.minimum(i, nb - 1)
    prev = lambda b, i: b * nb + jnp.maximum(jnp.minimum(i, nb - 1) - 1, 0)
    emit = lambda b, i: b * nb + jnp.maximum(i - 1, 0)
    hbm = pl.BlockSpec(memory_space=pl.ANY)
    wide = lambda cb: pl.BlockSpec((BLK, ATTN_WIDTH), lambda b, i: (cur(b, i), cb))
    kv = lambda rowf, cb: pl.BlockSpec((BLK, LANES), lambda b, i: (rowf(b, i), cb))
    in_specs = [pl.BlockSpec(memory_space=pltpu.SMEM), wide(0), wide(1),
                kv(cur, _KCOL // LANES), kv(prev, _KCOL // LANES), kv(cur, _VCOL // LANES), kv(prev, _VCOL // LANES),
                kv(cur, 0), kv(cur, 0), kv(prev, 0), kv(prev, 0), wide(0), wide(0)] + [hbm] * nparts
    out_specs = [pl.BlockSpec((BLK, ATTN_IN), lambda b, i: (emit(b, i), 0)),
                 pl.BlockSpec((SUBLANES, LANES), lambda b, i: (0, 0))] + [hbm] * nparts
    return pl.pallas_call(
        body, name="attn_bwd", grid=(b_loc, nb + 1), in_specs=in_specs, out_specs=out_specs,
        out_shape=[jax.ShapeDtypeStruct((n, ATTN_IN), F32), jax.ShapeDtypeStruct((SUBLANES, LANES), F32)] + _scatter_lands(parts),
        scratch_shapes=[pltpu.VMEM((BLK, ATTN_WIDTH), F32), pltpu.VMEM((BLK, ATTN_WIDTH), F32),
                        pltpu.VMEM((BLK, KV_WIDTH), F32), pltpu.VMEM((BLK, KV_WIDTH), F32)] + _scatter_sems(nparts),
        compiler_params=_params(("arbitrary", "arbitrary"), 48),
    )(sinks, p, p, p, p, p, p, ct, st, ct, st, o, dg, *parts)


_CUM_ROWS = 256
HALF = CH // 2
_ROW0 = [SUBLANES * (s // SUBLANES) for s in range(CH)]
_ROW1 = [HALF * (s // HALF + 1) for s in range(CH)]
_ROWS_OF = [_ROW1[s] - _ROW0[s] for s in range(CH)]
_OFF_OF = [sum(_ROWS_OF[:s]) for s in range(CH)]


def _tri(lower):
    r = lax.broadcasted_iota(jnp.int32, (_CUM_ROWS, _CUM_ROWS), 0)
    c = lax.broadcasted_iota(jnp.int32, (_CUM_ROWS, _CUM_ROWS), 1)
    same = (r ^ c) < CH
    return (same & ((c <= r) if lower else (c >= r))).astype(BF16)


def _gates(qp, fp, lb):
    e = jnp.exp(-jnp.abs(fp))
    r = 1.0 / (1.0 + e)
    sig_neg = jnp.where(fp >= 0, e, 1.0) * r
    sig = jnp.where(fp >= 0, 1.0, e) * r
    g = jnp.log(jnp.maximum(lb + (1.0 - lb) * sig, F32_TINY))
    return qp * _sigmoid(qp), g, (1.0 - lb) * sig_neg, sig_neg


def _pair_rows(bc, s):
    return jnp.exp(jnp.minimum(bc[_ROW0[s]:_ROW1[s], :] - bc[s:s + 1, :], 0.0))


def _cross_half(q, k, bc):
    r = bc[HALF - 1:HALF, :]
    e1, e0 = jnp.exp(bc[HALF:, :] - r), jnp.exp(r - bc[:HALF, :])
    return q[HALF:, :] * e1, e1, k[:HALF, :] * e0, e0


HP = 8
REC_TB = 256
_HW = HP * REC_K


def _hgrn_specs(tb, nt, reverse):
    tmap = (lambda t: nt - 1 - t) if reverse else (lambda t: t)
    groups = REC_HEADS // HP
    blk = lambda cb: pl.BlockSpec((tb, _HW), lambda h, b, t: (b * nt + tmap(t), cb * groups + h))
    head = pl.BlockSpec((tb, _HW), lambda h, b, t: (b * nt + tmap(t), h))
    lbs = pl.BlockSpec((1, _HW), lambda h, b, t: (0, h))
    gws = pl.BlockSpec((1, REC_K), lambda h, b, t: (0, 0))
    hist = pl.BlockSpec((HP, 1, tb // CH, REC_K, REC_K), lambda h, b, t: (h, b, tmap(t), 0, 0))
    return blk, head, lbs, gws, hist


def _chunk_rows(c, first=0, size=CH):
    start = c * CH + first
    return pl.ds(start if isinstance(start, int) else pl.multiple_of(start, CH if first % CH == 0 else SUBLANES), size)


def _in_stages(heads):
    live = list(heads)
    while live:
        live = [g for g in live if next(g, live) is not live]


def _cumsum_chunks(tri, x, out_ref, tb):
    for r in range(0, tb, _CUM_ROWS):
        out_ref[r:r + _CUM_ROWS, :] = _dot3(tri, x[r:r + _CUM_ROWS, :])


def _hgrn_fwd(p, lb, gw, b_loc, t_len):
    n = p.shape[0]
    tb = min(REC_TB, t_len)
    nt, nck = t_len // tb, tb // CH

    def body(qp_ref, fp_ref, i_ref, z_ref, lb_ref, gw_ref, oraw_ref, g_ref, sh_ref, q_s, k_s, b_s, o_s, st_ref,
             car_o, car_a, car_s, car_st):
        @pl.when(pl.program_id(2) == 0)
        def _():
            st_ref[...] = jnp.zeros_like(st_ref)

        qv, g, kk, _ = _gates(qp_ref[...], fp_ref[...], lb_ref[...])
        q_s[...] = qv
        k_s[...] = kk
        _cumsum_chunks(_tri(True), g, b_s, tb)
        ones = jnp.ones((REC_K, REC_K), BF16)
        sub = lax.broadcasted_iota(jnp.int32, (SUBLANES, REC_K), 0)

        rows_of = _chunk_rows

        def issue(c, hp):
            rs, cs = rows_of(c), slice(hp * REC_K, (hp + 1) * REC_K)
            q, k, bc, v = q_s[rs, cs], k_s[rs, cs], b_s[rs, cs], i_ref[rs, cs]
            st = st_ref[hp]
            sh_ref[hp, 0, c] = st
            o = lax.dot_general((q * jnp.exp(bc)).astype(BF16), st.astype(BF16), NT_DIMS, preferred_element_type=F32)
            w = jnp.concatenate([q[_ROW0[s]:_ROW1[s], :] * _pair_rows(bc, s) * k[s:s + 1, :] for s in range(CH)], axis=0)
            a = jnp.dot(w.astype(BF16), ones, preferred_element_type=F32)
            qe1, _, ke0, _ = _cross_half(q, k, bc)
            s10 = lax.dot_general(qe1.astype(BF16), ke0.astype(BF16), NT_DIMS, preferred_element_type=F32)
            kd = k * jnp.exp(bc[CH - 1:CH, :] - bc)
            st_new = lax.dot_general(v.astype(BF16), kd.astype(BF16), TN_DIMS, preferred_element_type=F32)
            return o, a, s10, st_new

        def advance_state(c, hp, st_new):
            bl = b_s[_chunk_rows(c, CH - SUBLANES, SUBLANES), hp * REC_K:(hp + 1) * REC_K][SUBLANES - 1:, :]
            st_ref[hp] = st_ref[hp] * jnp.exp(bl) + st_new

        def cross(c, hp, s10):
            v0 = i_ref[_chunk_rows(c, 0, HALF), hp * REC_K:(hp + 1) * REC_K]
            return jnp.dot(s10.astype(BF16), v0.astype(BF16), preferred_element_type=F32)

        def finish(c, hp, o, a, o_cross):
            rs, cs = rows_of(c), slice(hp * REC_K, (hp + 1) * REC_K)
            v = i_ref[rs, cs]
            acc = [jnp.zeros((SUBLANES, REC_K), F32) for _ in range(CH // SUBLANES)]
            for s in range(CH):
                j = s // SUBLANES
                vs = v[s:s + 1, :]
                for jj in range(j, _ROW1[s] // SUBLANES):
                    blk = a[_OFF_OF[s] + (jj - j) * SUBLANES:_OFF_OF[s] + (jj - j + 1) * SUBLANES, :]
                    if jj == j:
                        blk = jnp.where(sub >= s - j * SUBLANES, blk, 0.0)
                    acc[jj] = acc[jj] + blk * vs
            o_s[rs, cs] = o + jnp.concatenate(acc, axis=0) + jnp.concatenate([jnp.zeros((HALF, REC_K), F32), o_cross], axis=0)

        def park(slot, results):
            for hp, (o, a, s10, st_new) in enumerate(results):
                car_o[slot, hp], car_a[slot, hp], car_s[slot, hp], car_st[slot, hp] = o, a, s10, st_new

        def retire(c, slot):
            for hp in range(HP):
                advance_state(c, hp, car_st[slot, hp])
            yield
            crosses = [cross(c, hp, car_s[slot, hp]) for hp in range(HP)]
            for hp in range(HP):
                finish(c, hp, car_o[slot, hp], car_a[slot, hp], crosses[hp])

        def step(c, slot):
            closing = retire(c - 1, slot)
            next(closing)
            park(1 - slot, [issue(c, hp) for hp in range(HP)])
            next(closing, None)

        def trip(j, carry):
            step(2 * j + 1, 0)
            step(2 * j + 2, 1)
            return carry

        assert nck % 2 == 0
        park(0, [issue(0, hp) for hp in range(HP)])
        lax.fori_loop(0, nck // 2 - 1, trip, 0)
        step(nck - 1, 0)
        for _ in retire(nck - 1, 1):
            pass
        oraw_ref[...] = o_s[...]
        for hp in range(HP):
            cs = slice(hp * REC_K, (hp + 1) * REC_K)
            o, zc = o_s[:, cs], z_ref[:, cs]
            on = (o * lax.rsqrt(jnp.mean(o * o, axis=-1, keepdims=True) + NORM_EPS)) * gw_ref[...]
            g_ref[:, cs] = (on * (zc * _sigmoid(zc))).astype(BF16)

    blk, head, lbs, gws, hist = _hgrn_specs(tb, nt, False)
    return pl.pallas_call(
        body, name="hgrn_fwd", grid=(REC_HEADS // HP, b_loc, nt),
        in_specs=[blk(0), blk(1), blk(2), blk(3), lbs, gws], out_specs=[head, head, hist],
        out_shape=[jax.ShapeDtypeStruct((n, 1024), F32), jax.ShapeDtypeStruct((n, 1024), BF16),
                   jax.ShapeDtypeStruct((REC_HEADS, b_loc, t_len // CH, REC_K, REC_K), F32)],
        scratch_shapes=[pltpu.VMEM((tb, _HW), F32)] * 4 + [pltpu.VMEM((HP, REC_K, REC_K), F32)] + [
            pltpu.VMEM((2, HP, CH, REC_K), F32), pltpu.VMEM((2, HP, sum(_ROWS_OF), REC_K), F32),
            pltpu.VMEM((2, HP, HALF, HALF), F32), pltpu.VMEM((2, HP, REC_K, REC_K), F32)],
        compiler_params=_params(("arbitrary", "arbitrary", "arbitrary"), 48),
    )(p, p, p, p, lb, gw)


def _hgrn_bwd(p, lb, gw, oraw, sh, dg, b_loc, t_len):
    n = p.shape[0]
    tb = min(REC_TB, t_len)
    nt, nck = t_len // tb, tb // CH
    assert HP == REC_HEADS

    def body(qp_ref, fp_ref, i_ref, z_ref, lb_ref, gw_ref, oraw_ref, dg_ref, sh_ref,
             dp_ref, dlb_ref, dgw_ref,
             q_s, k_s, b_s, do_s, dqv_s, dk_s, db_s, dst_ref, car_r, car_dst, car_dec, car_a, car_da, car_x):
        dq_ref, df_ref, di_ref, dz_ref = (dp_ref.at[:, part * 1024:(part + 1) * 1024] for part in range(4))
        b, t = pl.program_id(1), pl.program_id(2)

        @pl.when(t == 0)
        def _():
            dst_ref[...] = jnp.zeros_like(dst_ref)

        @pl.when((b == 0) & (t == 0))
        def _():
            dlb_ref[...] = jnp.zeros_like(dlb_ref)
            dgw_ref[...] = jnp.zeros_like(dgw_ref)

        lbv, qp, fp = lb_ref[...], qp_ref[...], fp_ref[...]
        qv, g, kk, sig_neg = _gates(qp, fp, lbv)
        q_s[...] = qv
        k_s[...] = kk
        _cumsum_chunks(_tri(True), g, b_s, tb)
        gwv = gw_ref[...]
        for hp in range(HP):
            cs = slice(hp * REC_K, (hp + 1) * REC_K)
            o, zc, dgv = oraw_ref[:, cs], z_ref[:, cs], dg_ref[:, cs]
            rn = lax.rsqrt(jnp.mean(o * o, axis=-1, keepdims=True) + NORM_EPS)
            on = o * rn
            sgz = _sigmoid(zc)
            dz_ref[:, cs] = (dgv * (on * gwv) * (sgz * (1.0 + zc * (1.0 - sgz)))).astype(BF16)
            dpre = dgv * (zc * sgz)
            dgw_ref[hp] += jnp.sum(dpre * on, axis=0, keepdims=True)
            don = dpre * gwv
            do_s[:, cs] = rn * (don - on * jnp.mean(don * on, axis=-1, keepdims=True))

        ones = jnp.ones((REC_K, REC_K), BF16)
        sub = lax.broadcasted_iota(jnp.int32, (SUBLANES, REC_K), 0)
        rowid = lax.broadcasted_iota(jnp.int32, (CH, REC_K), 0)
        ngrp = CH // SUBLANES
        piece_row = lax.broadcasted_iota(jnp.int32, (1, sum(_ROWS_OF)), 1)
        key_of = jnp.zeros((1, sum(_ROWS_OF)), jnp.int32)
        for s in range(1, CH):
            key_of = jnp.where(piece_row >= _OFF_OF[s], s, key_of)
        pick = (key_of == lax.broadcasted_iota(jnp.int32, (CH, sum(_ROWS_OF)), 0)).astype(BF16)

        def operands(c, hp):
            rs, cs = _chunk_rows(c), slice(hp * REC_K, (hp + 1) * REC_K)
            return rs, cs, q_s[rs, cs], k_s[rs, cs], b_s[rs, cs], i_ref[rs, cs], do_s[rs, cs]

        def issue(c, hp, slot):
            _, _, q, k, bc, v, do = operands(c, hp)
            st, dst = sh_ref[hp, 0, c], dst_ref[hp]
            qe, kd = q * jnp.exp(bc), k * jnp.exp(bc[CH - 1:CH, :] - bc)
            do_bf, dst_bf = do.astype(BF16), dst.astype(BF16)
            car_r[slot, hp, 0:CH] = jnp.dot(do_bf, st.astype(BF16), preferred_element_type=F32)
            car_r[slot, hp, CH:2 * CH] = jnp.dot(v.astype(BF16), dst_bf, preferred_element_type=F32)
            car_r[slot, hp, 2 * CH:3 * CH] = lax.dot_general(kd.astype(BF16), dst_bf, NT_DIMS, preferred_element_type=F32)
            car_dst[slot, hp] = lax.dot_general(do_bf, qe.astype(BF16), TN_DIMS, preferred_element_type=F32)
            dec = jnp.concatenate([_pair_rows(bc, s) for s in range(CH)], axis=0)
            qk = jnp.concatenate([q[_ROW0[s]:_ROW1[s], :] * k[s:s + 1, :] for s in range(CH)], axis=0)
            x = jnp.concatenate([do[_ROW0[s]:_ROW1[s], :] * v[s:s + 1, :] for s in range(CH)], axis=0)
            car_dec[slot, hp] = dec
            car_a[slot, hp] = jnp.dot((qk * dec).astype(BF16), ones, preferred_element_type=F32)
            car_da[slot, hp] = jnp.dot(x.astype(BF16), ones, preferred_element_type=F32)
            qe1, _, ke0, _ = _cross_half(q, k, bc)
            qe1_bf, ke0_bf = qe1.astype(BF16), ke0.astype(BF16)
            do1_bf, v0_bf = do[HALF:, :].astype(BF16), v[:HALF, :].astype(BF16)
            car_x[slot, hp, 0:HALF] = lax.dot_general(ke0_bf, qe1_bf, NT_DIMS, preferred_element_type=F32)
            car_x[slot, hp, HALF:2 * HALF] = lax.dot_general(do1_bf, v0_bf, NT_DIMS, preferred_element_type=F32)
            car_x[slot, hp, 2 * HALF:3 * HALF] = lax.dot_general(v0_bf, do1_bf, NT_DIMS, preferred_element_type=F32)

        def advance_state(c, hp, slot):
            ebl = jnp.exp(b_s[_chunk_rows(c, CH - SUBLANES, SUBLANES), hp * REC_K:(hp + 1) * REC_K][SUBLANES - 1:, :])
            st, dst = sh_ref[hp, 0, c], dst_ref[hp]
            dst_ref[hp] = dst * ebl + car_dst[slot, hp]
            return ebl * jnp.sum(st * dst, axis=0, keepdims=True)

        def cross(c, hp, slot):
            _, _, q, k, bc, v, do = operands(c, hp)
            qe1, _, ke0, _ = _cross_half(q, k, bc)
            xs = car_x[slot, hp]
            dqe1 = jnp.dot(xs[HALF:2 * HALF].astype(BF16), ke0.astype(BF16), preferred_element_type=F32)
            dke0 = jnp.dot(xs[2 * HALF:].astype(BF16), qe1.astype(BF16), preferred_element_type=F32)
            dv1 = jnp.dot(xs[:HALF].astype(BF16), do[HALF:, :].astype(BF16), preferred_element_type=F32)
            return dqe1, dke0, dv1

        def retire(c, slot):
            dbl_state = [advance_state(c, hp, slot) for hp in range(HP)]
            yield
            crossed = [cross(c, hp, slot) for hp in range(HP)]
            for hp in range(HP):
                finish(c, hp, slot, dbl_state[hp], *crossed[hp])

        def step(c, slot):
            closing = retire(c + 1, slot)
            next(closing)
            for hp in range(HP):
                issue(c, hp, 1 - slot)
            next(closing, None)

        def trip(j, carry):
            step(nck - 2 - 2 * j, 0)
            step(nck - 3 - 2 * j, 1)
            return carry

        def finish(c, hp, slot, dbl_state, dqe1, dke0, dv1):
            rs, cs, q, k, bc, v, do = operands(c, hp)
            eb, ekd = jnp.exp(bc), jnp.exp(bc[CH - 1:CH, :] - bc)
            qe, kd = q * eb, k * ekd
            qe1, e1, ke0, e0 = _cross_half(q, k, bc)
            dqe, dkd, dv = car_r[slot, hp, 0:CH], car_r[slot, hp, CH:2 * CH], car_r[slot, hp, 2 * CH:3 * CH]
            a, da, decs = car_a[slot, hp], car_da[slot, hp], car_dec[slot, hp]
            dec = [decs[_OFF_OF[s]:_OFF_OF[s] + _ROWS_OF[s], :] for s in range(CH)]
            dbl = jnp.sum(dkd * kd, axis=0, keepdims=True) + dbl_state
            dq_acc = [jnp.zeros((SUBLANES, REC_K), F32) for _ in range(ngrp)]
            uk, uv = [], []
            for s in range(CH):
                j = s // SUBLANES
                r0 = j * SUBLANES
                ks = k[s:s + 1, :]
                for jj in range(j, _ROW1[s] // SUBLANES):
                    lo, hi = _OFF_OF[s] + (jj - j) * SUBLANES, _OFF_OF[s] + (jj - j + 1) * SUBLANES
                    a_blk, da_blk = a[lo:hi, :], da[lo:hi, :]
                    if jj == j:
                        keep = sub >= s - r0
                        a_blk, da_blk = jnp.where(keep, a_blk, 0.0), jnp.where(keep, da_blk, 0.0)
                    rows = slice(jj * SUBLANES, (jj + 1) * SUBLANES)
                    tt = da_blk * dec[s][(jj - j) * SUBLANES:(jj - j + 1) * SUBLANES, :]
                    dq_acc[jj] = dq_acc[jj] + tt * ks
                    uk.append(tt * q[rows, :])
                    uv.append(a_blk * do[rows, :])
            dk_in = jnp.dot(pick, jnp.concatenate(uk, axis=0).astype(BF16), preferred_element_type=F32)
            dv_in = jnp.dot(pick, jnp.concatenate(uv, axis=0).astype(BF16), preferred_element_type=F32)
            zero_half = jnp.zeros((HALF, REC_K), F32)
            dq_x = jnp.concatenate([zero_half, dqe1 * e1], axis=0)
            dk_x = jnp.concatenate([dke0 * e0, zero_half], axis=0)
            dv_x = jnp.concatenate([dv1, zero_half], axis=0)
            db_x = jnp.concatenate([-(dke0 * ke0), dqe1 * qe1], axis=0)
            dq_in = jnp.concatenate(dq_acc, axis=0)
            dqv_s[rs, cs] = dqe * eb + dq_in + dq_x
            dk_s[rs, cs] = dkd * ekd + dk_in + dk_x
            di_ref[rs, cs] = (dv + dv_in + dv_x).astype(BF16)
            db = dqe * qe - dkd * kd + q * dq_in - k * dk_in + db_x
            db_s[rs, cs] = db + jnp.where(rowid == CH - 1, dbl, 0.0)

        assert nck % 2 == 0
        for hp in range(HP):
            issue(nck - 1, hp, 0)
        lax.fori_loop(0, nck // 2 - 1, trip, 0)
        step(0, 0)
        for _ in retire(0, 1):
            pass
        up = _tri(False)
        sgq = _sigmoid(qp)
        dq_ref[...] = (dqv_s[...] * (sgq * (1.0 + qp * (1.0 - sgq)))).astype(BF16)
        dlb_acc = jnp.zeros((1, _HW), F32)
        for r in range(0, tb, _CUM_ROWS):
            rows = slice(r, r + _CUM_ROWS)
            dgl = _dot3(up, db_s[rows, :])
            dfg = dgl * jnp.exp(-g[rows, :]) - dk_s[rows, :]
            sn = sig_neg[rows, :]
            df_ref[rows, :] = (dfg * (1.0 - lbv) * (1.0 - sn) * sn).astype(BF16)
            dlb_acc = dlb_acc + jnp.sum(dfg * sn, axis=0, keepdims=True)
        dlb_ref[...] += dlb_acc

    blk, head, lbs, gws, hist = _hgrn_specs(tb, nt, True)
    out_specs = [pl.BlockSpec((tb, REC_IN), lambda h, b, t: (b * nt + nt - 1 - t, 0)), lbs,
                 pl.BlockSpec((HP, 1, REC_K), lambda h, b, t: (h, 0, 0))]
    out_shape = [jax.ShapeDtypeStruct((n, REC_IN), BF16),
                 jax.ShapeDtypeStruct((1, 1024), F32), jax.ShapeDtypeStruct((REC_HEADS, 1, REC_K), F32)]
    return pl.pallas_call(
        body, name="hgrn_bwd", grid=(REC_HEADS // HP, b_loc, nt),
        in_specs=[blk(0), blk(1), blk(2), blk(3), lbs, gws, head, head, hist],
        out_specs=out_specs, out_shape=out_shape,
        scratch_shapes=[pltpu.VMEM((tb, _HW), F32)] * 7 + [pltpu.VMEM((HP, REC_K, REC_K), F32)] + [
            pltpu.VMEM((2, HP, 3 * CH, REC_K), F32), pltpu.VMEM((2, HP, REC_K, REC_K), F32)] + [
            pltpu.VMEM((2, HP, sum(_ROWS_OF), REC_K), F32)] * 3 + [pltpu.VMEM((2, HP, 3 * HALF, HALF), F32)],
        compiler_params=_params(("arbitrary", "arbitrary", "arbitrary"), 56),
    )(p, p, p, p, lb, gw, oraw, dg, sh)


def _postnorm_bwd_nt(dxo, y, qw, w, name):
    n = dxo.shape[0]

    def body(dx_ref, y_ref, qw_ref, w_ref, dg_ref, dy_ref, dqw_ref, db_ref):
        @pl.when(pl.program_id(0) == 0)
        def _():
            dqw_ref[...] = jnp.zeros_like(dqw_ref)
            db_ref[...] = jnp.zeros_like(db_ref)

        yv, dxv = y_ref[...], dx_ref[...]
        r = lax.rsqrt(jnp.mean(yv * yv, axis=-1, keepdims=True) + NORM_EPS)
        u = yv * r
        du = dxv * qw_ref[...]
        dy = r * (du - u * jnp.mean(du * u, axis=-1, keepdims=True))
        dqw_ref[...] += jnp.sum(dxv * u, axis=0, keepdims=True)
        db_ref[...] += jnp.sum(dy, axis=0, keepdims=True)
        dyb = dy.astype(BF16)
        dy_ref[...] = dyb
        dg_ref[...] = lax.dot_general(dyb, w_ref[...], NT_DIMS, preferred_element_type=F32)

    rows = pl.BlockSpec((TM, D_MODEL), lambda i: (i, 0))
    const = lambda shape: pl.BlockSpec(shape, lambda i: (0, 0))
    return pl.pallas_call(
        body, name=name, grid=(n // TM,), in_specs=[rows, rows, const((1, D_MODEL)), const((D_MODEL, D_MODEL))],
        out_specs=[rows, rows, const((1, D_MODEL)), const((1, D_MODEL))],
        out_shape=[jax.ShapeDtypeStruct((n, D_MODEL), F32), jax.ShapeDtypeStruct((n, D_MODEL), BF16),
                   jax.ShapeDtypeStruct((1, D_MODEL), F32), jax.ShapeDtypeStruct((1, D_MODEL), F32)],
        compiler_params=_params(("arbitrary",), 48),
    )(dxo, y, qw, w)


def _nt_prenorm_bwd(dps, w, x, pw, dxo, has_bias, name, parts=()):
    n = x.shape[0]
    widths = [d.shape[1] for d in dps]
    m = sum(widths)
    npieces, nparts, steps = len(dps), len(parts), n // TM

    def body(*refs):
        dp_refs = refs[:npieces]
        w_ref, x_ref, pw_ref, dxo_ref = refs[npieces:npieces + 4]
        part_refs = refs[npieces + 4:npieces + 4 + nparts]
        dx_ref, dpw_ref, db_ref = refs[npieces + 4 + nparts:npieces + 7 + nparts]
        land_refs = refs[npieces + 7 + nparts:npieces + 7 + 2 * nparts]
        sems = refs[npieces + 7 + 2 * nparts:]

        @pl.when(pl.program_id(0) == 0)
        def _():
            dpw_ref[...] = jnp.zeros_like(dpw_ref)
            db_ref[...] = jnp.zeros_like(db_ref)
            if nparts:
                _scatter_start(part_refs, land_refs, sems)

        dh = jnp.zeros((TM, D_MODEL), F32)
        off = 0
        for dp_ref, wd in zip(dp_refs, widths):
            cn = _col_chunk(wd)
            for j in range(0, wd, cn):
                dpc = dp_ref[:, j:j + cn]
                if has_bias:
                    db_ref[:, off + j:off + j + cn] += jnp.sum(dpc, axis=0, keepdims=True)
                dh = dh + lax.dot_general(dpc.astype(BF16), w_ref[:, off + j:off + j + cn], NT_DIMS, preferred_element_type=F32)
            off += wd
        xv = x_ref[...]
        r = lax.rsqrt(jnp.mean(xv * xv, axis=-1, keepdims=True) + NORM_EPS)
        xn = xv * r
        dpw_ref[...] += jnp.sum(dh * xn, axis=0, keepdims=True)
        dxn = dh * pw_ref[...]
        dx_ref[...] = dxo_ref[...] + r * (dxn - xn * jnp.mean(dxn * xn, axis=-1, keepdims=True))

        if nparts:
            @pl.when(pl.program_id(0) == steps - 1)
            def _():
                _scatter_wait(part_refs, land_refs, sems)

    rows = pl.BlockSpec((TM, D_MODEL), lambda i: (i, 0))
    const = lambda shape: pl.BlockSpec(shape, lambda i: (0, 0))
    hbm = pl.BlockSpec(memory_space=pl.ANY)
    in_specs = ([pl.BlockSpec((TM, wd), lambda i: (i, 0)) for wd in widths] + [const((D_MODEL, m)), rows, const((1, D_MODEL)), rows]
                + [hbm] * nparts)
    return pl.pallas_call(
        body, name=name, grid=(steps,), in_specs=in_specs,
        out_specs=[rows, const((1, D_MODEL)), const((1, m))] + [hbm] * nparts,
        out_shape=[jax.ShapeDtypeStruct((n, D_MODEL), F32), jax.ShapeDtypeStruct((1, D_MODEL), F32),
                   jax.ShapeDtypeStruct((1, m), F32)] + _scatter_lands(parts),
        scratch_shapes=_scatter_sems(nparts) if nparts else [],
        compiler_params=_params(("arbitrary",), 56),
    )(*dps, w, x, pw, dxo, *parts)


def _matmul_tn(a, b, name):
    n, k = a.shape
    m = b.shape[1]
    tk, tm, tn = k, _col_chunk(m), 1024 if n % 1024 == 0 else n

    def body(a_ref, b_ref, o_ref):
        @pl.when(pl.program_id(2) == 0)
        def _():
            o_ref[...] = jnp.zeros_like(o_ref)

        o_ref[...] += lax.dot_general(a_ref[...], b_ref[...].astype(BF16), TN_DIMS, preferred_element_type=F32)

    return pl.pallas_call(
        body, name=name, grid=(k // tk, m // tm, n // tn),
        in_specs=[pl.BlockSpec((tn, tk), lambda i, j, l: (l, i)), pl.BlockSpec((tn, tm), lambda i, j, l: (l, j))],
        out_specs=pl.BlockSpec((tk, tm), lambda i, j, l: (i, j)),
        out_shape=jax.ShapeDtypeStruct((k, m), F32),
        compiler_params=_params(("arbitrary", "arbitrary", "arbitrary"), 48),
    )(a, b)


def _matmul_tn_by_owner(a, b, name):
    n, k = a.shape
    c = b.shape[1] // N_DEV
    tn = 1024 if n % 1024 == 0 else n
    steps = n // tn
    per = 2

    def body(a_ref, b_ref, o_ref, w_ref):
        @pl.when(pl.program_id(1) == 0)
        def _():
            o_ref[...] = jnp.zeros_like(o_ref)

        r = lax.dot_general(a_ref[...], b_ref[...], TN_DIMS, preferred_element_type=F32)
        for j in range(per):
            o_ref[j] += r[:, j * c:(j + 1) * c]

        @pl.when(pl.program_id(1) == steps - 1)
        def _():
            w_ref[...] = o_ref[...].astype(BF16)

    out = pl.BlockSpec((per, k, c), lambda j, l: (j, 0, 0))
    return pl.pallas_call(
        body, name=name, grid=(N_DEV // per, steps),
        in_specs=[pl.BlockSpec((tn, k), lambda j, l: (l, 0)), pl.BlockSpec((tn, per * c), lambda j, l: (l, j))],
        out_specs=[out, out],
        out_shape=[jax.ShapeDtypeStruct((N_DEV, k, c), F32), jax.ShapeDtypeStruct((N_DEV, k, c), BF16)],
        compiler_params=_params(("arbitrary", "arbitrary"), 48),
    )(a, b)


def _by_owner_cols(dw):
    k, m = dw.shape
    return dw.reshape(k, N_DEV, m // N_DEV).transpose(1, 0, 2)


def _own_and_bf16(part):
    return lax.dynamic_index_in_dim(part, _my_id(), 0, keepdims=False), part.astype(BF16)


def _step(x, pos_col, tgt, pre_w, post_w, wa_in, ba_in, sinks, wa_out_shard, ba_out, wr_in_shard, lb_logits, gnorm_w, wr_out_shard, b_loc, t_len):
    nb = t_len // BLK
    ct, st = _rope_tables(pos_col)
    lb = _lower_bound(lb_logits)
    p0, h0, ga_out = _norm_matmul(x, pre_w[0:1], wa_in, ba_in, "attn_in_proj", [wa_out_shard])
    wa_out = ga_out.reshape(ATTN_WIDTH, D_MODEL)
    o0, g0, gr_in, gr_out = _attn_fwd(p0, ct, st, sinks, b_loc, nb, [wr_in_shard, wr_out_shard])
    wr_in = gr_in.transpose(1, 0, 2).reshape(D_MODEL, REC_IN)
    wr_out = gr_out.reshape(1024, D_MODEL)
    y0, x1 = _outproj_postnorm(g0, wa_out, ba_out, x, post_w[0:1], "attn_out_proj")
    p1, h1 = _norm_matmul(x1, pre_w[1:2], wr_in, None, "rec_in_proj")
    o1, g1, sh = _hgrn_fwd(p1, lb, gnorm_w, b_loc, t_len)
    dx2, dg1, dy1, dpost1, loss_tile = _outproj_loss_bwd(g1, wr_out, x1, post_w[1:2], tgt, "rec_out_proj_loss_bwd")
    d_wr_out = _matmul_tn(g1, dy1, "rec_w_out_grad")
    dp1, dlb, dgw = _hgrn_bwd(p1, lb, gnorm_w, o1, sh, dg1, b_loc, t_len)
    dx1, dpre1, _ = _nt_prenorm_bwd([dp1], wr_in, x1, pre_w[1:2], dx2, False, "rec_in_bwd")
    part_r_in, wire_r_in = _matmul_tn_by_owner(h1, dp1, "rec_w_in_grad")
    own_r_in = lax.dynamic_index_in_dim(part_r_in, _my_id(), 0, keepdims=False)
    dg0, dy0, dpost0, dba_out = _postnorm_bwd_nt(dx1, y0, post_w[0:1], wa_out, "attn_out_bwd")
    d_wa_out = _matmul_tn(g0, dy0, "attn_w_out_grad")
    owns, wires = zip(*[_own_and_bf16(part) for part in (
        d_wr_out.reshape(N_DEV, 1024 // N_DEV, D_MODEL), d_wa_out.reshape(N_DEV, ATTN_WIDTH // N_DEV, D_MODEL))])
    owns, wires = (own_r_in,) + owns, (wire_r_in,) + wires
    dp0, dsink_tile, *lands = _attn_bwd(p0, ct, st, sinks, o0, dg0, b_loc, nb, list(wires))
    d_wa_in = _matmul_tn(h0, dp0, "attn_w_in_grad")
    own_a_in, wire_a_in = _own_and_bf16(_by_owner_cols(_qkvz(d_wa_in)))
    dx0, dpre0, dba_in, land_a_in = _nt_prenorm_bwd([dp0], wa_in, x, pre_w[0:1], dx1, True, "attn_in_bwd", [wire_a_in])
    small = dict(pre=jnp.concatenate([dpre0, dpre1], axis=0), post=jnp.concatenate([dpost0, dpost1], axis=0),
                 ba_in=dba_in, sinks=dsink_tile[0:1, 0:N_HEADS], ba_out=dba_out, lb=dlb, gnorm=jnp.sum(dgw, axis=0))
    return loss_tile, dx0, list(zip(lands, owns)) + [(land_a_in, own_a_in)], small


def _my_id():
    return lax.axis_index("x") * 4 + lax.axis_index("y") * 2 + lax.axis_index("c")


def _peer(k):
    x, y, c = lax.axis_index("x"), lax.axis_index("y"), lax.axis_index("c")
    return (x ^ ((k >> 2) & 1), y ^ ((k >> 1) & 1), c ^ (k & 1))


def _peer_id(k):
    return _my_id() ^ k


def _all_gather_by_chip(shard):
    def body(x_ref, out_ref, send_sems, recv_sems, local_sem):
        x, y, c = lax.axis_index("x"), lax.axis_index("y"), lax.axis_index("c")
        me, sibling = (x, y, c), (x, y, 1 - c)
        chips = [(1 - x, y), (x, 1 - y), (1 - x, 1 - y)]

        def rows(px, py, pc):
            return out_ref.at[4 * px + 2 * py + pc]

        def copy(k, block, to, src=None):
            return pltpu.make_async_remote_copy(src_ref=rows(*block) if src is None else src, dst_ref=rows(*block),
                                                send_sem=send_sems.at[k], recv_sem=recv_sems.at[k], device_id=to, device_id_type=MESH)

        mine = pltpu.make_async_copy(x_ref, rows(*me), local_sem)
        mine.start()
        first = [copy(0, me, sibling, src=x_ref)] + [copy(1 + j, me, (*chip, c), src=x_ref) for j, chip in enumerate(chips)]
        for cp in first:
            cp.start()
        passed = [copy(4 + j, (*chip, c), sibling) for j, chip in enumerate(chips)]
        for j, chip in enumerate(chips):
            copy(1 + j, (*chip, c), me).wait_recv()
            passed[j].start()
        copy(0, sibling, me).wait_recv()
        for j, chip in enumerate(chips):
            copy(4 + j, (*chip, 1 - c), me).wait_recv()
        for cp in first + passed:
            cp.wait_send()
        mine.wait()

    hbm = pl.BlockSpec(memory_space=pl.ANY)
    return pl.pallas_call(
        body, name="comm_all_gather_by_chip", in_specs=[hbm], out_specs=hbm,
        out_shape=jax.ShapeDtypeStruct((N_DEV,) + shard.shape, shard.dtype),
        scratch_shapes=[pltpu.SemaphoreType.DMA((N_DEV - 1,)), pltpu.SemaphoreType.DMA((N_DEV - 1,)), pltpu.SemaphoreType.DMA],
    )(shard)


def _gather_shapes(shards):
    return [jax.ShapeDtypeStruct((N_DEV,) + s.shape, s.dtype) for s in shards]


def _gather_sems(nsh):
    return [pltpu.SemaphoreType.DMA((nsh, N_DEV - 1)), pltpu.SemaphoreType.DMA((nsh, N_DEV - 1)), pltpu.SemaphoreType.DMA((nsh,))]


def _gather_copies(ins, outs, sems, received):
    send_sems, recv_sems, local_sems = sems
    me = _my_id()
    local = [pltpu.make_async_copy(ins[a], outs[a].at[me], local_sems.at[a]) for a in range(len(ins))]
    remote = [pltpu.make_async_remote_copy(
        src_ref=ins[a], dst_ref=outs[a].at[_peer_id(k) if received else me], send_sem=send_sems.at[a, k - 1],
        recv_sem=recv_sems.at[a, k - 1], device_id=_peer(k), device_id_type=MESH)
        for a in range(len(ins)) for k in range(1, N_DEV)]
    return local, remote


def _gather_start(ins, outs, sems):
    local, sends = _gather_copies(ins, outs, sems, False)
    for cp in local + sends:
        cp.start()


def _gather_wait(ins, outs, sems):
    local, recvs = _gather_copies(ins, outs, sems, True)
    for cp in recvs:
        cp.wait_recv()
    for cp in recvs:
        cp.wait_send()
    for cp in local:
        cp.wait()


def _scatter_lands(parts):
    return [jax.ShapeDtypeStruct((N_DEV - 1,) + p.shape[1:], p.dtype) for p in parts]


def _scatter_sems(nparts):
    return [pltpu.SemaphoreType.DMA((nparts, N_DEV - 1)), pltpu.SemaphoreType.DMA((nparts, N_DEV - 1))]


def _scatter_copies(parts, lands, sems):
    send_sems, recv_sems = sems
    return [pltpu.make_async_remote_copy(
        src_ref=parts[a].at[_peer_id(k)], dst_ref=lands[a].at[k - 1], send_sem=send_sems.at[a, k - 1],
        recv_sem=recv_sems.at[a, k - 1], device_id=_peer(k), device_id_type=MESH)
        for a in range(len(parts)) for k in range(1, N_DEV)]


def _scatter_start(parts, lands, sems):
    for cp in _scatter_copies(parts, lands, sems):
        cp.start()


def _scatter_wait(parts, lands, sems):
    copies = _scatter_copies(parts, lands, sems)
    for cp in copies:
        cp.wait_recv()
    for cp in copies:
        cp.wait_send()


def _adamw(w, g, m, v):
    m2 = ADAM_B1 * m + (1.0 - ADAM_B1) * g
    v2 = ADAM_B2 * v + (1.0 - ADAM_B2) * (g * g)
    m_hat = m2 / (1.0 - ADAM_B1 ** ADAM_STEP)
    v_hat = v2 / (1.0 - ADAM_B2 ** ADAM_STEP)
    delta = -ADAM_LR * (m_hat / (jnp.sqrt(v_hat) + ADAM_EPS) + ADAM_WD * w)
    return delta, m2, v2


def _sum_adamw(land, own, w, m, v, name):
    r, c = own.shape
    rb = min(r, 256)

    def body(land_ref, own_ref, w_ref, m_ref, v_ref, g_ref, d_ref, m2_ref, v2_ref):
        me = _my_id()
        g = jnp.zeros((rb, c), F32)
        for dev in range(N_DEV):
            k = dev ^ me
            g = g + jnp.where(k == 0, own_ref[...], land_ref[jnp.maximum(k - 1, 0)].astype(F32))
        delta, m2, v2 = _adamw(w_ref[...], g, m_ref[...], v_ref[...])
        g_ref[...] = g
        d_ref[...] = delta
        m2_ref[...] = m2
        v2_ref[...] = v2

    rows = pl.BlockSpec((rb, c), lambda i: (i, 0))
    return pl.pallas_call(
        body, name=name, grid=(r // rb,), in_specs=[pl.BlockSpec((N_DEV - 1, rb, c), lambda i: (0, i, 0))] + [rows] * 4,
        out_specs=[rows] * 4, out_shape=[jax.ShapeDtypeStruct((r, c), F32)] * 4,
        compiler_params=_params(("arbitrary",), 32),
    )(land, own, w, m, v)


_SMALL = [("pre_norm_w", 2048), ("post_norm_w", 2048), ("attn_b_in", 2304), ("attn_sinks", 16), ("attn_b_out", 1024),
          ("rec_lb_logits", 2048), ("rec_gnorm_w", 128), ("loss", 1)]
_TILE = SUBLANES * LANES


def _small_rows(size):
    return -(-size // _TILE) * SUBLANES


_SMALL_OFF = {}
_r = 0
for _name, _size in _SMALL:
    _SMALL_OFF[_name] = _r
    _r += _small_rows(_size)
_SMALL_ROWS = _r


def _pack_small(pieces):
    out = []
    for name, size in _SMALL:
        flat = pieces[name].reshape(-1).astype(F32)
        out.append(jnp.pad(flat, (0, _small_rows(size) * LANES - size)).reshape(-1, LANES))
    return jnp.concatenate(out, axis=0)


def _unpack_small(packed, shapes):
    return {name: packed[_SMALL_OFF[name]:_SMALL_OFF[name] + _small_rows(size)].reshape(-1)[:size].reshape(shapes[name])
            for name, size in _SMALL}


def _small_allreduce_adamw(gpart, w, m, v):
    lb0 = _SMALL_OFF["rec_lb_logits"]

    def body(gp_ref, w_ref, m_ref, v_ref, g_ref, d_ref, m2_ref, v2_ref, land_ref, send_sems, recv_sems):
        me = _my_id()
        sent = []
        for k in range(1, N_DEV):
            cp = pltpu.make_async_remote_copy(src_ref=gp_ref, dst_ref=land_ref.at[k - 1], send_sem=send_sems.at[k - 1],
                                              recv_sem=recv_sems.at[k - 1], device_id=_peer(k), device_id_type=MESH)
            cp.start()
            sent.append(cp)
        for cp in sent:
            cp.wait_recv()
        for cp in sent:
            cp.wait_send()
        g = jnp.zeros((_SMALL_ROWS, LANES), F32)
        for dev in range(N_DEV):
            k = dev ^ me
            g = g + jnp.where(k == 0, gp_ref[...], land_ref[jnp.maximum(k - 1, 0)])
        g_ref[...] = g
        l0, l1 = w_ref[lb0:lb0 + SUBLANES, :], w_ref[lb0 + SUBLANES:lb0 + 2 * SUBLANES, :]
        mx = jnp.maximum(l0, l1)
        e0, e1 = jnp.exp(l0 - mx), jnp.exp(l1 - mx)
        p1 = e1 / (e0 + e1)
        dl1 = (1.0 - p1) * p1 * g[lb0:lb0 + SUBLANES, :]
        g_ref[lb0:lb0 + SUBLANES, :] = -dl1
        g_ref[lb0 + SUBLANES:lb0 + 2 * SUBLANES, :] = dl1
        delta, m2, v2 = _adamw(w_ref[...], g_ref[...], m_ref[...], v_ref[...])
        d_ref[...] = delta
        m2_ref[...] = m2
        v2_ref[...] = v2

    vmem = pl.BlockSpec(memory_space=pltpu.VMEM)
    return pl.pallas_call(
        body, name="comm_small_allreduce_adamw", in_specs=[vmem] * 4, out_specs=[vmem] * 4,
        out_shape=[jax.ShapeDtypeStruct((_SMALL_ROWS, LANES), F32)] * 4,
        scratch_shapes=[pltpu.VMEM((N_DEV - 1, _SMALL_ROWS, LANES), F32), pltpu.SemaphoreType.DMA((N_DEV - 1,)),
                        pltpu.SemaphoreType.DMA((N_DEV - 1,))],
    )(gpart, w, m, v)


def _qzkv(a):
    return jnp.concatenate([a[..., :1024], a[..., 1280:], a[..., 1024:1280]], axis=-1)


def _qkvz(a):
    return jnp.concatenate([a[..., :1024], a[..., 2048:], a[..., 1024:2048]], axis=-1)


def kernel(x, positions, pre_norm_w, post_norm_w, attn_w_in, attn_b_in, attn_sinks, attn_w_out, attn_b_out, rec_w_in, rec_lb_logits, rec_gnorm_w, rec_w_out, loss_target, m_pre_norm_w, m_post_norm_w, m_attn_w_in, m_attn_b_in, m_attn_sinks, m_attn_w_out, m_attn_b_out, m_rec_w_in, m_rec_lb_logits, m_rec_gnorm_w, m_rec_w_out, v_pre_norm_w, v_post_norm_w, v_attn_w_in, v_attn_b_in, v_attn_sinks, v_attn_w_out, v_attn_b_out, v_rec_w_in, v_rec_lb_logits, v_rec_gnorm_w, v_rec_w_out):
    b_loc, t_len, _ = x.shape
    n = b_loc * t_len
    ga_in = _all_gather_by_chip(attn_w_in[0].astype(BF16))
    wa_in = _qzkv(ga_in.transpose(1, 0, 2).reshape(D_MODEL, ATTN_IN))

    loss_tile, dx, landed, small = _step(
        x.reshape(n, D_MODEL), positions.reshape(n, 1).astype(F32), loss_target.reshape(n, D_MODEL),
        pre_norm_w, post_norm_w, wa_in, _qzkv(attn_b_in), attn_sinks, attn_w_out[0].astype(BF16), attn_b_out,
        rec_w_in[0].astype(BF16), rec_lb_logits, rec_gnorm_w, rec_w_out[0].astype(BF16), b_loc, t_len)

    lift = lambda outs: tuple(a[None] for a in outs)
    (l_r_in, o_r_in), (l_r_out, o_r_out), (l_a_out, o_a_out), (l_a_in, o_a_in) = landed
    r_a_in = lift(_sum_adamw(l_a_in, o_a_in, attn_w_in[0], m_attn_w_in[0], v_attn_w_in[0], "adamw_attn_w_in"))
    r_r_in = lift(_sum_adamw(l_r_in, o_r_in, rec_w_in[0], m_rec_w_in[0], v_rec_w_in[0], "adamw_rec_w_in"))
    r_r_out = lift(_sum_adamw(l_r_out, o_r_out, rec_w_out[0], m_rec_w_out[0], v_rec_w_out[0], "adamw_rec_w_out"))
    r_a_out = lift(_sum_adamw(l_a_out, o_a_out, attn_w_out[0], m_attn_w_out[0], v_attn_w_out[0], "adamw_attn_w_out"))

    gsmall = dict(pre_norm_w=small["pre"], post_norm_w=small["post"], attn_b_in=_qkvz(small["ba_in"]), attn_sinks=small["sinks"],
                  attn_b_out=small["ba_out"], rec_lb_logits=jnp.concatenate([small["lb"], jnp.zeros_like(small["lb"])], axis=0),
                  rec_gnorm_w=small["gnorm"], loss=loss_tile[0:1, 0:1])
    nil = jnp.zeros((1, 1), F32)
    wsmall = dict(pre_norm_w=pre_norm_w, post_norm_w=post_norm_w, attn_b_in=attn_b_in, attn_sinks=attn_sinks,
                  attn_b_out=attn_b_out, rec_lb_logits=rec_lb_logits, rec_gnorm_w=rec_gnorm_w, loss=nil)
    msmall = dict(pre_norm_w=m_pre_norm_w, post_norm_w=m_post_norm_w, attn_b_in=m_attn_b_in, attn_sinks=m_attn_sinks,
                  attn_b_out=m_attn_b_out, rec_lb_logits=m_rec_lb_logits, rec_gnorm_w=m_rec_gnorm_w, loss=nil)
    vsmall = dict(pre_norm_w=v_pre_norm_w, post_norm_w=v_post_norm_w, attn_b_in=v_attn_b_in, attn_sinks=v_attn_sinks,
                  attn_b_out=v_attn_b_out, rec_lb_logits=v_rec_lb_logits, rec_gnorm_w=v_rec_gnorm_w, loss=nil)
    shapes = {k: a.shape for k, a in wsmall.items()}
    packed = _small_allreduce_adamw(_pack_small(gsmall), _pack_small(wsmall), _pack_small(msmall), _pack_small(vsmall))
    sg, sd, sm, sv = [_unpack_small(a, shapes) for a in packed]

    big = {"attn_w_in": r_a_in, "attn_w_out": r_a_out, "rec_w_in": r_r_in, "rec_w_out": r_r_out}
    order = ["pre_norm_w", "post_norm_w", "attn_w_in", "attn_b_in", "attn_sinks", "attn_w_out", "attn_b_out", "rec_w_in",
             "rec_lb_logits", "rec_gnorm_w", "rec_w_out"]
    outs = [sg["loss"][0, 0], dx.reshape(b_loc, t_len, D_MODEL)]
    for idx, small_set in enumerate((sg, sd, sm, sv)):
        outs += [big[nm][idx] if nm in big else small_set[nm] for nm in order]
    return tuple(outs)
```

```python
import numpy as np
import jax
import jax.numpy as jnp
from jax import lax
from jax.experimental import pallas as pl
from jax.experimental.pallas import tpu as pltpu

F32, BF16 = jnp.float32, jnp.bfloat16
MESH = pl.DeviceIdType.MESH
N_DEV = 8

D_MODEL = 1024
N_HEADS, HEAD_DIM, N_KV, GROUP = 16, 64, 2, 8
ATTN_WIDTH, KV_WIDTH = 1024, 128
ATTN_IN = 2 * ATTN_WIDTH + 2 * KV_WIDTH
BLK = 128
ROPE_THETA, ROPE_HALF = 500000.0, 8
REC_HEADS, REC_K = 8, 128
REC_WIDTH = REC_HEADS * REC_K
REC_IN = 4 * REC_WIDTH
TN_ROWS = 1024
CH = 32
NORM_EPS = 1e-6
F32_TINY = 1.1754944e-38
ADAM_LR, ADAM_B1, ADAM_B2, ADAM_EPS, ADAM_WD, ADAM_STEP = 0.001, 0.9, 0.999, 1e-08, 0.01, 10

LANES, SUBLANES = 128, 8
TM = 512
NT_DIMS = (((1,), (1,)), ((), ()))
TN_DIMS = (((0,), (0,)), ((), ()))
MB = 2 ** 20


def _params(sem=None, vmem_mb=48, **kw):
    return pltpu.CompilerParams(dimension_semantics=sem, vmem_limit_bytes=vmem_mb * MB, **kw)


def _col_chunk(m):
    return 768 if m % 1024 else 1024


def _sigmoid(x):
    return 1.0 / (1.0 + jnp.exp(-x))


def _split3(x):
    hi = x.astype(BF16)
    r1 = x - hi.astype(F32)
    mid = r1.astype(BF16)
    lo = (r1 - mid.astype(F32)).astype(BF16)
    return hi, mid, lo


def _dot3(l_bf, x):
    hi, mid, lo = _split3(x)
    return (jnp.dot(l_bf, hi, preferred_element_type=F32) + jnp.dot(l_bf, mid, preferred_element_type=F32)
            + jnp.dot(l_bf, lo, preferred_element_type=F32))


def _rope_tables(pos_col):
    n = pos_col.shape[0]
    lane = np.arange(LANES) % HEAD_DIM
    inv = np.float32(ROPE_THETA) ** (-(np.arange(ROPE_HALF, dtype=np.float32) * np.float32(2.0) / np.float32(2 * ROPE_HALF)))
    freq = np.where(lane < 2 * ROPE_HALF, inv[lane % ROPE_HALF], 0.0).astype(np.float32)[None, :]
    sign = np.where(lane < ROPE_HALF, -1.0, np.where(lane < 2 * ROPE_HALF, 1.0, 0.0)).astype(np.float32)[None, :]

    def body(p_ref, f_ref, s_ref, c_out, s_out):
        ang = p_ref[...] * f_ref[...]
        c_out[...] = jnp.cos(ang)
        s_out[...] = jnp.sin(ang) * s_ref[...]

    row = pl.BlockSpec((TM, 1), lambda i: (i, 0))
    vec = pl.BlockSpec((1, LANES), lambda i: (0, 0))
    out = pl.BlockSpec((TM, LANES), lambda i: (i, 0))
    return pl.pallas_call(
        body, name="rope_tables", grid=(n // TM,), in_specs=[row, vec, vec], out_specs=[out, out],
        out_shape=[jax.ShapeDtypeStruct((n, LANES), F32)] * 2, compiler_params=_params(("arbitrary",)),
    )(pos_col, jnp.asarray(freq), jnp.asarray(sign))


def _rope_apply(xv, c, s, lm):
    partner = jnp.where(lm < ROPE_HALF, pltpu.roll(xv, LANES - ROPE_HALF, 1), pltpu.roll(xv, ROPE_HALF, 1))
    return xv * c + partner * s


def _rope_bwd(dy, c, s, lm):
    t = dy * s
    partner = jnp.where(lm < ROPE_HALF, pltpu.roll(t, LANES - ROPE_HALF, 1),
                        jnp.where(lm < 2 * ROPE_HALF, pltpu.roll(t, ROPE_HALF, 1), 0.0))
    return dy * c + partner


def _lower_bound(lb_logits):
    def body(l_ref, o_ref):
        l0, l1 = l_ref[0:1, :], l_ref[1:2, :]
        m = jnp.maximum(l0, l1)
        e0, e1 = jnp.exp(l0 - m), jnp.exp(l1 - m)
        o_ref[...] = e1 / (e0 + e1)

    return pl.pallas_call(body, name="lower_bound", out_shape=jax.ShapeDtypeStruct((1, lb_logits.shape[1]), F32))(lb_logits)


def _norm_matmul(x, pw, w, bias, name, shards=()):
    n, m = x.shape[0], w.shape[1]
    cn = _col_chunk(m)
    has_bias = bias is not None
    nsh, steps = len(shards), n // TM

    def body(*refs):
        refs = list(refs)
        x_ref, pw_ref, w_ref = refs[:3]
        b_ref = refs[3] if has_bias else None
        refs = refs[4 if has_bias else 3:]
        sh_in, (p_ref, h_ref), sh_out, sems = refs[:nsh], refs[nsh:nsh + 2], refs[nsh + 2:2 * nsh + 2], refs[2 * nsh + 2:]
        if nsh:
            @pl.when(pl.program_id(0) == 0)
            def _():
                _gather_start(sh_in, sh_out, sems)

        xv = x_ref[...]
        r = lax.rsqrt(jnp.mean(xv * xv, axis=-1, keepdims=True) + NORM_EPS)
        h = ((xv * r) * pw_ref[...]).astype(BF16)
        h_ref[...] = h
        for j in range(0, m, cn):
            acc = jnp.dot(h, w_ref[:, j:j + cn], preferred_element_type=F32)
            if has_bias:
                acc = acc + b_ref[:, j:j + cn]
            p_ref[:, j:j + cn] = acc

        if nsh:
            @pl.when(pl.program_id(0) == steps - 1)
            def _():
                _gather_wait(sh_in, sh_out, sems)

    rows = pl.BlockSpec((TM, D_MODEL), lambda i: (i, 0))
    const = lambda shape: pl.BlockSpec(shape, lambda i: (0, 0))
    hbm = pl.BlockSpec(memory_space=pl.ANY)
    in_specs = [rows, const((1, D_MODEL)), const((D_MODEL, m))] + ([const((1, m))] if has_bias else []) + [hbm] * nsh
    args = (x, pw, w) + ((bias,) if has_bias else ()) + tuple(shards)
    return pl.pallas_call(
        body, name=name, grid=(steps,), in_specs=in_specs,
        out_specs=[pl.BlockSpec((TM, m), lambda i: (i, 0)), rows] + [hbm] * nsh,
        out_shape=[jax.ShapeDtypeStruct((n, m), F32), jax.ShapeDtypeStruct((n, D_MODEL), BF16)] + _gather_shapes(shards),
        scratch_shapes=_gather_sems(nsh) if nsh else [],
        compiler_params=_params(("arbitrary",), 56),
    )(*args)


def _outproj_postnorm(g, w, bias, xres, qw, name):
    n = g.shape[0]

    def body(g_ref, w_ref, b_ref, x_ref, qw_ref, y_ref, o_ref):
        y = jnp.dot(g_ref[...], w_ref[...], preferred_element_type=F32) + b_ref[...]
        y_ref[...] = y
        r = lax.rsqrt(jnp.mean(y * y, axis=-1, keepdims=True) + NORM_EPS)
        o_ref[...] = x_ref[...] + (y * r) * qw_ref[...]

    rows = pl.BlockSpec((TM, D_MODEL), lambda i: (i, 0))
    const = lambda shape: pl.BlockSpec(shape, lambda i: (0, 0))
    return pl.pallas_call(
        body, name=name, grid=(n // TM,),
        in_specs=[rows, const((D_MODEL, D_MODEL)), const((1, D_MODEL)), rows, const((1, D_MODEL))],
        out_specs=[rows, rows], out_shape=[jax.ShapeDtypeStruct((n, D_MODEL), F32)] * 2,
        compiler_params=_params(("arbitrary",), 48),
    )(g, w, bias, xres, qw)


def _outproj_loss_bwd(g, w, xres, qw, tgt, name):
    n = g.shape[0]
    steps = n // TM

    def body(g_ref, w_ref, x_ref, qw_ref, t_ref, dx_ref, dg_ref, dy_ref, dqw_ref, loss_ref, acc_ref):
        i = pl.program_id(0)

        @pl.when(i == 0)
        def _():
            acc_ref[...] = jnp.zeros_like(acc_ref)
            dqw_ref[...] = jnp.zeros_like(dqw_ref)

        y = jnp.dot(g_ref[...], w_ref[...], preferred_element_type=F32)
        r = lax.rsqrt(jnp.mean(y * y, axis=-1, keepdims=True) + NORM_EPS)
        u = y * r
        e = (x_ref[...] + u * qw_ref[...]) - t_ref[...]
        dxn = e * (1.0 / D_MODEL)
        dx_ref[...] = dxn
        acc_ref[...] += jnp.sum(e * e, axis=0, keepdims=True)
        du = dxn * qw_ref[...]
        dy = (r * (du - u * jnp.mean(du * u, axis=-1, keepdims=True))).astype(BF16)
        dqw_ref[...] += jnp.sum(dxn * u, axis=0, keepdims=True)
        dy_ref[...] = dy
        dg_ref[...] = lax.dot_general(dy, w_ref[...], NT_DIMS, preferred_element_type=F32)

        @pl.when(i == steps - 1)
        def _():
            loss_ref[...] = jnp.full(loss_ref.shape, jnp.sum(acc_ref[...]) * (0.5 / D_MODEL), F32)

    rows = pl.BlockSpec((TM, D_MODEL), lambda i: (i, 0))
    const = lambda shape: pl.BlockSpec(shape, lambda i: (0, 0))
    return pl.pallas_call(
        body, name=name, grid=(steps,),
        in_specs=[rows, const((D_MODEL, D_MODEL)), rows, const((1, D_MODEL)), rows],
        out_specs=[rows, rows, rows, const((1, D_MODEL)), const((SUBLANES, LANES))],
        out_shape=[jax.ShapeDtypeStruct((n, D_MODEL), F32), jax.ShapeDtypeStruct((n, D_MODEL), F32),
                   jax.ShapeDtypeStruct((n, D_MODEL), BF16), jax.ShapeDtypeStruct((1, D_MODEL), F32),
                   jax.ShapeDtypeStruct((SUBLANES, LANES), F32)],
        scratch_shapes=[pltpu.VMEM((1, D_MODEL), F32)], compiler_params=_params(("arbitrary",), 48),
    )(g, w, xres, qw, tgt)


_QCOL, _ZCOL, _KCOL, _VCOL = 0, 1024, 2048, 2176


def _head_stack(chunks, heads, lt64):
    return jnp.concatenate([jnp.where(lt64 if n % 2 == 0 else ~lt64, chunks[n // 2], 0.0) for n in heads], axis=0)


def _dup_half(x, h, lt64):
    r = pltpu.roll(x, HEAD_DIM, 1)
    return jnp.where(lt64, x, r) if h == 0 else jnp.where(lt64, r, x)


def _pair_chunk(xt, c2):
    a, b = 2 * c2, 2 * c2 + 1
    return jnp.concatenate([xt[:HEAD_DIM, a * BLK:(a + 1) * BLK], xt[HEAD_DIM:, b * BLK:(b + 1) * BLK]], axis=0).T


def _attn_mask_t(i):
    key = lax.broadcasted_iota(jnp.int32, (2 * BLK, BLK), 0)
    qry = lax.broadcasted_iota(jnp.int32, (2 * BLK, BLK), 1)
    valid = (key > qry) & (key <= qry + BLK) & ((key >= BLK) | (i > 0))
    return jnp.tile(jnp.where(valid, 0.0, -1e30), (1, GROUP))


def _attn_probs_t(s, heads, sink_ref, mask):
    s = s + mask
    head = lax.broadcasted_iota(jnp.int32, (1, len(heads) * BLK), 1) >> 7
    sk = jnp.zeros((1, len(heads) * BLK), F32)
    for j, n in enumerate(heads):
        sk = jnp.where(head == j, sink_ref[0, n], sk)
    m = jnp.maximum(jnp.max(s, axis=0, keepdims=True), sk)
    p = jnp.exp(s - m)
    esk = jnp.exp(sk - m)
    inv = 1.0 / (jnp.sum(p, axis=0, keepdims=True) + esk)
    return p * inv, esk * inv


def _attn_fwd(p, ct, st, sinks, b_loc, nb, shards):
    n = p.shape[0]
    nsh = len(shards)

    def body(sink_ref, q_ref, z_ref, kc_ref, kp_ref, vc_ref, vp_ref, cc_ref, sc_ref, cp_ref, sp_ref, *rest):
        sh_in, (o_ref, g_ref), sh_out, sems = rest[:nsh], rest[nsh:nsh + 2], rest[nsh + 2:2 * nsh + 2], rest[2 * nsh + 2:]
        b, i = pl.program_id(0), pl.program_id(1)

        @pl.when((b == 0) & (i == 0))
        def _():
            _gather_start(sh_in, sh_out, sems)

        lane = lax.broadcasted_iota(jnp.int32, (BLK, LANES), 1)
        lm = lane & (HEAD_DIM - 1)
        cc, sc = cc_ref[...], sc_ref[...]
        kcat = jnp.concatenate([_rope_apply(kp_ref[...], cp_ref[...], sp_ref[...], lm),
                                _rope_apply(kc_ref[...], cc, sc, lm)], axis=0)
        vcat = jnp.concatenate([vp_ref[...], vc_ref[...]], axis=0)
        qr = [_rope_apply(q_ref[:, c * LANES:(c + 1) * LANES], cc, sc, lm) * (HEAD_DIM ** -0.5) for c in range(8)]
        valid = _attn_mask_t(i)
        lt64, lt64k = lane < HEAD_DIM, lax.broadcasted_iota(jnp.int32, (2 * BLK, LANES), 1) < HEAD_DIM
        def kv_head(h):
            heads = list(range(h * GROUP, (h + 1) * GROUP))
            kext, vext = _dup_half(kcat, h, lt64k).astype(BF16), _dup_half(vcat, h, lt64k).astype(BF16)
            qst = _head_stack(qr, heads, lt64).astype(BF16)
            s = lax.dot_general(kext, qst, NT_DIMS, preferred_element_type=F32)
            yield
            pn, _ = _attn_probs_t(s, heads, sink_ref, valid)
            ot = lax.dot_general(vext, pn.astype(BF16), TN_DIMS, preferred_element_type=F32)
            yield
            for c2 in range(GROUP // 2):
                oc = _pair_chunk(ot, c2)
                cols = slice((4 * h + c2) * LANES, (4 * h + c2 + 1) * LANES)
                zc = z_ref[:, cols]
                o_ref[:, cols] = oc
                g_ref[:, cols] = (oc * (zc * _sigmoid(zc))).astype(BF16)

        _in_stages([kv_head(h) for h in range(N_KV)])

        @pl.when((b == b_loc - 1) & (i == nb - 1))
        def _():
            _gather_wait(sh_in, sh_out, sems)

    cur = lambda b, i: b * nb + i
    prev = lambda b, i: b * nb + jnp.maximum(i - 1, 0)
    wide = lambda cb: pl.BlockSpec((BLK, ATTN_WIDTH), lambda b, i: (cur(b, i), cb))
    kv = lambda rowf, cb: pl.BlockSpec((BLK, LANES), lambda b, i: (rowf(b, i), cb))
    hbm = pl.BlockSpec(memory_space=pl.ANY)
    in_specs = [pl.BlockSpec(memory_space=pltpu.SMEM), wide(0), wide(1),
                kv(cur, _KCOL // LANES), kv(prev, _KCOL // LANES), kv(cur, _VCOL // LANES), kv(prev, _VCOL // LANES),
                kv(cur, 0), kv(cur, 0), kv(prev, 0), kv(prev, 0)] + [hbm] * nsh
    return pl.pallas_call(
        body, name="attn_fwd", grid=(b_loc, nb), in_specs=in_specs, out_specs=[wide(0), wide(0)] + [hbm] * nsh,
        out_shape=[jax.ShapeDtypeStruct((n, ATTN_WIDTH), F32), jax.ShapeDtypeStruct((n, ATTN_WIDTH), BF16)] + _gather_shapes(shards),
        scratch_shapes=_gather_sems(nsh), compiler_params=_params(("arbitrary", "arbitrary"), 48),
    )(sinks, p, p, p, p, p, p, ct, st, ct, st, *shards)


def _attn_bwd(p, ct, st, sinks, o, dg, b_loc, nb, parts):
    n = p.shape[0]
    nparts = len(parts)

    def body(sink_ref, q_ref, z_ref, kc_ref, kp_ref, vc_ref, vp_ref, cc_ref, sc_ref, cp_ref, sp_ref, o_ref, dg_ref, *rest):
        part_refs, (dp_ref, ds_ref), land_refs = rest[:nparts], rest[nparts:nparts + 2], rest[nparts + 2:2 * nparts + 2]
        dq_s, dz_s, dk_s, dv_s = rest[2 * nparts + 2:2 * nparts + 6]
        sems = rest[2 * nparts + 6:]
        b, i = pl.program_id(0), pl.program_id(1)

        @pl.when((b == 0) & (i == 0))
        def _():
            _scatter_start(part_refs, land_refs, sems)

        @pl.when((b == b_loc - 1) & (i == nb))
        def _():
            _scatter_wait(part_refs, land_refs, sems)

        lane = lax.broadcasted_iota(jnp.int32, (BLK, LANES), 1)
        lm = lane & (HEAD_DIM - 1)

        @pl.when((b == 0) & (i == 0))
        def _():
            ds_ref[...] = jnp.zeros_like(ds_ref)

        @pl.when(i < nb)
        def _compute():
            cc, sc = cc_ref[...], sc_ref[...]
            kcat = jnp.concatenate([_rope_apply(kp_ref[...], cp_ref[...], sp_ref[...], lm),
                                    _rope_apply(kc_ref[...], cc, sc, lm)], axis=0)
            vcat = jnp.concatenate([vp_ref[...], vc_ref[...]], axis=0)
            qr = [_rope_apply(q_ref[:, c * LANES:(c + 1) * LANES], cc, sc, lm) * (HEAD_DIM ** -0.5) for c in range(8)]
            valid = _attn_mask_t(i)
            lt64, lt64k = lane < HEAD_DIM, lax.broadcasted_iota(jnp.int32, (2 * BLK, LANES), 1) < HEAD_DIM
            do_chunks, doo_chunks, dz_chunks = [], [], []
            for c in range(8):
                cols = slice(c * LANES, (c + 1) * LANES)
                zc, oc, dgc = z_ref[:, cols], o_ref[:, cols], dg_ref[:, cols]
                sg = _sigmoid(zc)
                do_chunks.append(dgc * (zc * sg))
                dz_chunks.append(dgc * oc * (sg * (1.0 + zc * (1.0 - sg))))
                doo_chunks.append(do_chunks[c] * oc)
            dq_chunks = [None] * 8
            dk_h, dv_h, ds_parts = [None] * N_KV, [None] * N_KV, [None] * N_KV
            tile_lane = lax.broadcasted_iota(jnp.int32, (SUBLANES, LANES), 1)
            tile_row = lax.broadcasted_iota(jnp.int32, (SUBLANES, LANES), 0)
            ones8 = jnp.ones((SUBLANES, LANES), BF16)

            def kv_head(h):
                heads = list(range(h * GROUP, (h + 1) * GROUP))
                kext = _dup_half(kcat, h, lt64k)
                kext_bf, kext_t = kext.astype(BF16), kext.T.astype(BF16)
                vext = _dup_half(vcat, h, lt64k).astype(BF16)
                qst = _head_stack(qr, heads, lt64).astype(BF16)
                pn, psink = _attn_probs_t(lax.dot_general(kext_bf, qst, NT_DIMS, preferred_element_type=F32), heads, sink_ref, valid)
                do_bf = _head_stack(do_chunks, heads, lt64).astype(BF16)
                delta = sum(lax.dot_general(ones8, part, NT_DIMS, preferred_element_type=F32)
                            for part in _split3(_head_stack(doo_chunks, heads, lt64)))[0:1, :]
                dpt = lax.dot_general(vext, do_bf, NT_DIMS, preferred_element_type=F32)
                dst = (pn * (dpt - delta)).astype(BF16)
                sink_term = psink * delta
                ds_acc = jnp.zeros((SUBLANES, LANES), F32)
                for j, n in enumerate(heads):
                    val = -jnp.sum(sink_term[:, j * BLK:(j + 1) * BLK])
                    ds_acc = ds_acc + jnp.where((tile_lane == n) & (tile_row == 0), val, 0.0)
                ds_parts[h] = ds_acc
                dqt = jnp.dot(kext_t, dst, preferred_element_type=F32) * (HEAD_DIM ** -0.5)
                dk_ext = jnp.dot(dst, qst, preferred_element_type=F32)
                dv_ext = jnp.dot(pn.astype(BF16), do_bf, preferred_element_type=F32)
                dk_h[h] = dk_ext + pltpu.roll(dk_ext, HEAD_DIM, 1)
                dv_h[h] = dv_ext + pltpu.roll(dv_ext, HEAD_DIM, 1)
                for c2 in range(GROUP // 2):
                    dq_chunks[4 * h + c2] = _rope_bwd(_pair_chunk(dqt, c2), cc, sc, lm)

            for h in range(N_KV):
                kv_head(h)
            ds_ref[...] += ds_parts[0] + ds_parts[1]
            dk_full = jnp.where(lt64k, dk_h[0], dk_h[1])
            dv_full = jnp.where(lt64k, dv_h[0], dv_h[1])

            @pl.when(i >= 1)
            def _emit():
                dp_ref[:, _QCOL:_QCOL + ATTN_WIDTH] = dq_s[...]
                dp_ref[:, _ZCOL:_ZCOL + ATTN_WIDTH] = dz_s[...]
                dp_ref[:, _KCOL:_KCOL + KV_WIDTH] = _rope_bwd(dk_s[...] + dk_full[:BLK], cp_ref[...], sp_ref[...], lm)
                dp_ref[:, _VCOL:_VCOL + KV_WIDTH] = dv_s[...] + dv_full[:BLK]

            for c in range(8):
                dq_s[:, c * LANES:(c + 1) * LANES] = dq_chunks[c]
                dz_s[:, c * LANES:(c + 1) * LANES] = dz_chunks[c]
            dk_s[...] = dk_full[BLK:]
            dv_s[...] = dv_full[BLK:]

        @pl.when(i == nb)
        def _final():
            dp_ref[:, _QCOL:_QCOL + ATTN_WIDTH] = dq_s[...]
            dp_ref[:, _ZCOL:_ZCOL + ATTN_WIDTH] = dz_s[...]
            dp_ref[:, _KCOL:_KCOL + KV_WIDTH] = _rope_bwd(dk_s[...], cc_ref[...], sc_ref[...], lm)
            dp_ref[:, _VCOL:_VCOL + KV_WIDTH] = dv_s[...]

    cur = lambda b, i: b * nb + jnp.minimum(i, nb - 1)
    prev = lambda b, i: b * nb + jnp.maximum(jnp.minimum(i, nb - 1) - 1, 0)
    emit = lambda b, i: b * nb + jnp.maximum(i - 1, 0)
    hbm = pl.BlockSpec(memory_space=pl.ANY)
    wide = lambda cb: pl.BlockSpec((BLK, ATTN_WIDTH), lambda b, i: (cur(b, i), cb))
    kv = lambda rowf, cb: pl.BlockSpec((BLK, LANES), lambda b, i: (rowf(b, i), cb))
    in_specs = [pl.BlockSpec(memory_space=pltpu.SMEM), wide(0), wide(1),
                kv(cur, _KCOL // LANES), kv(prev, _KCOL // LANES), kv(cur, _VCOL // LANES), kv(prev, _VCOL // LANES),
                kv(cur, 0), kv(cur, 0), kv(prev, 0), kv(prev, 0), wide(0), wide(0)] + [hbm] * nparts
    out_specs = [pl.BlockSpec((BLK, ATTN_IN), lambda b, i: (emit(b, i), 0)),
                 pl.BlockSpec((SUBLANES, LANES), lambda b, i: (0, 0))] + [hbm] * nparts
    return pl.pallas_call(
        body, name="attn_bwd", grid=(b_loc, nb + 1), in_specs=in_specs, out_specs=out_specs,
        out_shape=[jax.ShapeDtypeStruct((n, ATTN_IN), F32), jax.ShapeDtypeStruct((SUBLANES, LANES), F32)] + _scatter_lands(parts),
        scratch_shapes=[pltpu.VMEM((BLK, ATTN_WIDTH), F32), pltpu.VMEM((BLK, ATTN_WIDTH), F32),
                        pltpu.VMEM((BLK, KV_WIDTH), F32), pltpu.VMEM((BLK, KV_WIDTH), F32)] + _scatter_sems(nparts),
        compiler_params=_params(("arbitrary", "arbitrary"), 48),
    )(sinks, p, p, p, p, p, p, ct, st, ct, st, o, dg, *parts)


_CUM_ROWS = 256
HALF = CH // 2
_ROW0 = [SUBLANES * (s // SUBLANES) for s in range(CH)]
_ROW1 = [HALF * (s // HALF + 1) for s in range(CH)]
_ROWS_OF = [_ROW1[s] - _ROW0[s] for s in range(CH)]
_OFF_OF = [sum(_ROWS_OF[:s]) for s in range(CH)]


def _tri(lower):
    r = lax.broadcasted_iota(jnp.int32, (_CUM_ROWS, _CUM_ROWS), 0)
    c = lax.broadcasted_iota(jnp.int32, (_CUM_ROWS, _CUM_ROWS), 1)
    same = (r ^ c) < CH
    return (same & ((c <= r) if lower else (c >= r))).astype(BF16)


def _gates(qp, fp, lb):
    e = jnp.exp(-jnp.abs(fp))
    r = 1.0 / (1.0 + e)
    sig_neg = jnp.where(fp >= 0, e, 1.0) * r
    sig = jnp.where(fp >= 0, 1.0, e) * r
    g = jnp.log(jnp.maximum(lb + (1.0 - lb) * sig, F32_TINY))
    return qp * _sigmoid(qp), g, (1.0 - lb) * sig_neg, sig_neg


def _pair_rows(bc, s):
    return jnp.exp(jnp.minimum(bc[_ROW0[s]:_ROW1[s], :] - bc[s:s + 1, :], 0.0))


def _cross_half(q, k, bc):
    r = bc[HALF - 1:HALF, :]
    e1, e0 = jnp.exp(bc[HALF:, :] - r), jnp.exp(r - bc[:HALF, :])
    return q[HALF:, :] * e1, e1, k[:HALF, :] * e0, e0


HP = 8
REC_TB = 256
_HW = HP * REC_K


def _hgrn_specs(tb, nt, reverse):
    tmap = (lambda t: nt - 1 - t) if reverse else (lambda t: t)
    groups = REC_HEADS // HP
    blk = lambda cb: pl.BlockSpec((tb, _HW), lambda h, b, t: (b * nt + tmap(t), cb * groups + h))
    head = pl.BlockSpec((tb, _HW), lambda h, b, t: (b * nt + tmap(t), h))
    lbs = pl.BlockSpec((1, _HW), lambda h, b, t: (0, h))
    gws = pl.BlockSpec((1, REC_K), lambda h, b, t: (0, 0))
    hist = pl.BlockSpec((HP, 1, tb // CH, REC_K, REC_K), lambda h, b, t: (h, b, tmap(t), 0, 0))
    return blk, head, lbs, gws, hist


def _chunk_rows(c, first=0, size=CH):
    start = c * CH + first
    return pl.ds(start if isinstance(start, int) else pl.multiple_of(start, CH if first % CH == 0 else SUBLANES), size)


def _in_stages(heads):
    live = list(heads)
    while live:
        live = [g for g in live if next(g, live) is not live]


def _cumsum_chunks(tri, x, out_ref, tb):
    for r in range(0, tb, _CUM_ROWS):
        out_ref[r:r + _CUM_ROWS, :] = _dot3(tri, x[r:r + _CUM_ROWS, :])


def _hgrn_fwd(p, lb, gw, b_loc, t_len):
    n = p.shape[0]
    tb = min(REC_TB, t_len)
    nt, nck = t_len // tb, tb // CH

    def body(qp_ref, fp_ref, i_ref, z_ref, lb_ref, gw_ref, oraw_ref, g_ref, sh_ref, q_s, k_s, b_s, o_s, st_ref,
             car_o, car_a, car_s, car_st):
        @pl.when(pl.program_id(2) == 0)
        def _():
            st_ref[...] = jnp.zeros_like(st_ref)

        qv, g, kk, _ = _gates(qp_ref[...], fp_ref[...], lb_ref[...])
        q_s[...] = qv
        k_s[...] = kk
        _cumsum_chunks(_tri(True), g, b_s, tb)
        ones = jnp.ones((REC_K, REC_K), BF16)
        sub = lax.broadcasted_iota(jnp.int32, (SUBLANES, REC_K), 0)

        rows_of = _chunk_rows

        def issue(c, hp):
            rs, cs = rows_of(c), slice(hp * REC_K, (hp + 1) * REC_K)
            q, k, bc, v = q_s[rs, cs], k_s[rs, cs], b_s[rs, cs], i_ref[rs, cs]
            st = st_ref[hp]
            sh_ref[hp, 0, c] = st
            o = lax.dot_general((q * jnp.exp(bc)).astype(BF16), st.astype(BF16), NT_DIMS, preferred_element_type=F32)
            w = jnp.concatenate([q[_ROW0[s]:_ROW1[s], :] * _pair_rows(bc, s) * k[s:s + 1, :] for s in range(CH)], axis=0)
            a = jnp.dot(w.astype(BF16), ones, preferred_element_type=F32)
            qe1, _, ke0, _ = _cross_half(q, k, bc)
            s10 = lax.dot_general(qe1.astype(BF16), ke0.astype(BF16), NT_DIMS, preferred_element_type=F32)
            kd = k * jnp.exp(bc[CH - 1:CH, :] - bc)
            st_new = lax.dot_general(v.astype(BF16), kd.astype(BF16), TN_DIMS, preferred_element_type=F32)
            return o, a, s10, st_new

        def advance_state(c, hp, st_new):
            bl = b_s[_chunk_rows(c, CH - SUBLANES, SUBLANES), hp * REC_K:(hp + 1) * REC_K][SUBLANES - 1:, :]
            st_ref[hp] = st_ref[hp] * jnp.exp(bl) + st_new

        def cross(c, hp, s10):
            v0 = i_ref[_chunk_rows(c, 0, HALF), hp * REC_K:(hp + 1) * REC_K]
            return jnp.dot(s10.astype(BF16), v0.astype(BF16), preferred_element_type=F32)

        def finish(c, hp, o, a, o_cross):
            rs, cs = rows_of(c), slice(hp * REC_K, (hp + 1) * REC_K)
            v = i_ref[rs, cs]
            acc = [jnp.zeros((SUBLANES, REC_K), F32) for _ in range(CH // SUBLANES)]
            for s in range(CH):
                j = s // SUBLANES
                vs = v[s:s + 1, :]
                for jj in range(j, _ROW1[s] // SUBLANES):
                    blk = a[_OFF_OF[s] + (jj - j) * SUBLANES:_OFF_OF[s] + (jj - j + 1) * SUBLANES, :]
                    if jj == j:
                        blk = jnp.where(sub >= s - j * SUBLANES, blk, 0.0)
                    acc[jj] = acc[jj] + blk * vs
            o_s[rs, cs] = o + jnp.concatenate(acc, axis=0) + jnp.concatenate([jnp.zeros((HALF, REC_K), F32), o_cross], axis=0)

        def park(slot, results):
            for hp, (o, a, s10, st_new) in enumerate(results):
                car_o[slot, hp], car_a[slot, hp], car_s[slot, hp], car_st[slot, hp] = o, a, s10, st_new

        def retire(c, slot):
            for hp in range(HP):
                advance_state(c, hp, car_st[slot, hp])
            yield
            crosses = [cross(c, hp, car_s[slot, hp]) for hp in range(HP)]
            for hp in range(HP):
                finish(c, hp, car_o[slot, hp], car_a[slot, hp], crosses[hp])

        def step(c, slot):
            closing = retire(c - 1, slot)
            next(closing)
            park(1 - slot, [issue(c, hp) for hp in range(HP)])
            next(closing, None)

        def trip(j, carry):
            step(2 * j + 1, 0)
            step(2 * j + 2, 1)
            return carry

        assert nck % 2 == 0
        park(0, [issue(0, hp) for hp in range(HP)])
        lax.fori_loop(0, nck // 2 - 1, trip, 0)
        step(nck - 1, 0)
        for _ in retire(nck - 1, 1):
            pass
        oraw_ref[...] = o_s[...]
        for hp in range(HP):
            cs = slice(hp * REC_K, (hp + 1) * REC_K)
            o, zc = o_s[:, cs], z_ref[:, cs]
            on = (o * lax.rsqrt(jnp.mean(o * o, axis=-1, keepdims=True) + NORM_EPS)) * gw_ref[...]
            g_ref[:, cs] = (on * (zc * _sigmoid(zc))).astype(BF16)

    blk, head, lbs, gws, hist = _hgrn_specs(tb, nt, False)
    return pl.pallas_call(
        body, name="hgrn_fwd", grid=(REC_HEADS // HP, b_loc, nt),
        in_specs=[blk(0), blk(1), blk(2), blk(3), lbs, gws], out_specs=[head, head, hist],
        out_shape=[jax.ShapeDtypeStruct((n, REC_WIDTH), F32), jax.ShapeDtypeStruct((n, REC_WIDTH), BF16),
                   jax.ShapeDtypeStruct((REC_HEADS, b_loc, t_len // CH, REC_K, REC_K), F32)],
        scratch_shapes=[pltpu.VMEM((tb, _HW), F32)] * 4 + [pltpu.VMEM((HP, REC_K, REC_K), F32)] + [
            pltpu.VMEM((2, HP, CH, REC_K), F32), pltpu.VMEM((2, HP, sum(_ROWS_OF), REC_K), F32),
            pltpu.VMEM((2, HP, HALF, HALF), F32), pltpu.VMEM((2, HP, REC_K, REC_K), F32)],
        compiler_params=_params(("arbitrary", "arbitrary", "arbitrary"), 48),
    )(p, p, p, p, lb, gw)


def _hgrn_bwd(p, lb, gw, oraw, sh, dg, b_loc, t_len):
    n = p.shape[0]
    tb = min(REC_TB, t_len)
    nt, nck = t_len // tb, tb // CH
    assert HP == REC_HEADS

    def body(qp_ref, fp_ref, i_ref, z_ref, lb_ref, gw_ref, oraw_ref, dg_ref, sh_ref,
             dp_ref, dlb_ref, dgw_ref,
             q_s, k_s, b_s, do_s, dqv_s, dk_s, db_s, dst_ref, car_r, car_dst, car_dec, car_a, car_da, car_x):
        dq_ref, df_ref, di_ref, dz_ref = (dp_ref.at[:, part * REC_WIDTH:(part + 1) * REC_WIDTH] for part in range(4))
        b, t = pl.program_id(1), pl.program_id(2)

        @pl.when(t == 0)
        def _():
            dst_ref[...] = jnp.zeros_like(dst_ref)

        @pl.when((b == 0) & (t == 0))
        def _():
            dlb_ref[...] = jnp.zeros_like(dlb_ref)
            dgw_ref[...] = jnp.zeros_like(dgw_ref)

        lbv, qp, fp = lb_ref[...], qp_ref[...], fp_ref[...]
        qv, g, kk, sig_neg = _gates(qp, fp, lbv)
        q_s[...] = qv
        k_s[...] = kk
        _cumsum_chunks(_tri(True), g, b_s, tb)
        gwv = gw_ref[...]
        for hp in range(HP):
            cs = slice(hp * REC_K, (hp + 1) * REC_K)
            o, zc, dgv = oraw_ref[:, cs], z_ref[:, cs], dg_ref[:, cs]
            rn = lax.rsqrt(jnp.mean(o * o, axis=-1, keepdims=True) + NORM_EPS)
            on = o * rn
            sgz = _sigmoid(zc)
            dz_ref[:, cs] = (dgv * (on * gwv) * (sgz * (1.0 + zc * (1.0 - sgz)))).astype(BF16)
            dpre = dgv * (zc * sgz)
            dgw_ref[hp] += jnp.sum(dpre * on, axis=0, keepdims=True)
            don = dpre * gwv
            do_s[:, cs] = rn * (don - on * jnp.mean(don * on, axis=-1, keepdims=True))

        ones = jnp.ones((REC_K, REC_K), BF16)
        sub = lax.broadcasted_iota(jnp.int32, (SUBLANES, REC_K), 0)
        rowid = lax.broadcasted_iota(jnp.int32, (CH, REC_K), 0)
        ngrp = CH // SUBLANES
        piece_row = lax.broadcasted_iota(jnp.int32, (1, sum(_ROWS_OF)), 1)
        key_of = jnp.zeros((1, sum(_ROWS_OF)), jnp.int32)
        for s in range(1, CH):
            key_of = jnp.where(piece_row >= _OFF_OF[s], s, key_of)
        pick = (key_of == lax.broadcasted_iota(jnp.int32, (CH, sum(_ROWS_OF)), 0)).astype(BF16)

        def operands(c, hp):
            rs, cs = _chunk_rows(c), slice(hp * REC_K, (hp + 1) * REC_K)
            return rs, cs, q_s[rs, cs], k_s[rs, cs], b_s[rs, cs], i_ref[rs, cs], do_s[rs, cs]

        def issue(c, hp, slot):
            _, _, q, k, bc, v, do = operands(c, hp)
            st, dst = sh_ref[hp, 0, c], dst_ref[hp]
            qe, kd = q * jnp.exp(bc), k * jnp.exp(bc[CH - 1:CH, :] - bc)
            do_bf, dst_bf = do.astype(BF16), dst.astype(BF16)
            car_r[slot, hp, 0:CH] = jnp.dot(do_bf, st.astype(BF16), preferred_element_type=F32)
            car_r[slot, hp, CH:2 * CH] = jnp.dot(v.astype(BF16), dst_bf, preferred_element_type=F32)
            car_r[slot, hp, 2 * CH:3 * CH] = lax.dot_general(kd.astype(BF16), dst_bf, NT_DIMS, preferred_element_type=F32)
            car_dst[slot, hp] = lax.dot_general(do_bf, qe.astype(BF16), TN_DIMS, preferred_element_type=F32)
            dec = jnp.concatenate([_pair_rows(bc, s) for s in range(CH)], axis=0)
            qk = jnp.concatenate([q[_ROW0[s]:_ROW1[s], :] * k[s:s + 1, :] for s in range(CH)], axis=0)
            x = jnp.concatenate([do[_ROW0[s]:_ROW1[s], :] * v[s:s + 1, :] for s in range(CH)], axis=0)
            car_dec[slot, hp] = dec
            car_a[slot, hp] = jnp.dot((qk * dec).astype(BF16), ones, preferred_element_type=F32)
            car_da[slot, hp] = jnp.dot(x.astype(BF16), ones, preferred_element_type=F32)
            qe1, _, ke0, _ = _cross_half(q, k, bc)
            qe1_bf, ke0_bf = qe1.astype(BF16), ke0.astype(BF16)
            do1_bf, v0_bf = do[HALF:, :].astype(BF16), v[:HALF, :].astype(BF16)
            car_x[slot, hp, 0:HALF] = lax.dot_general(ke0_bf, qe1_bf, NT_DIMS, preferred_element_type=F32)
            car_x[slot, hp, HALF:2 * HALF] = lax.dot_general(do1_bf, v0_bf, NT_DIMS, preferred_element_type=F32)
            car_x[slot, hp, 2 * HALF:3 * HALF] = lax.dot_general(v0_bf, do1_bf, NT_DIMS, preferred_element_type=F32)

        def advance_state(c, hp, slot):
            ebl = jnp.exp(b_s[_chunk_rows(c, CH - SUBLANES, SUBLANES), hp * REC_K:(hp + 1) * REC_K][SUBLANES - 1:, :])
            st, dst = sh_ref[hp, 0, c], dst_ref[hp]
            dst_ref[hp] = dst * ebl + car_dst[slot, hp]
            return ebl * jnp.sum(st * dst, axis=0, keepdims=True)

        def cross(c, hp, slot):
            _, _, q, k, bc, v, do = operands(c, hp)
            qe1, _, ke0, _ = _cross_half(q, k, bc)
            xs = car_x[slot, hp]
            dqe1 = jnp.dot(xs[HALF:2 * HALF].astype(BF16), ke0.astype(BF16), preferred_element_type=F32)
            dke0 = jnp.dot(xs[2 * HALF:].astype(BF16), qe1.astype(BF16), preferred_element_type=F32)
            dv1 = jnp.dot(xs[:HALF].astype(BF16), do[HALF:, :].astype(BF16), preferred_element_type=F32)
            return dqe1, dke0, dv1

        def retire(c, slot):
            dbl_state = [advance_state(c, hp, slot) for hp in range(HP)]
            yield
            crossed = [cross(c, hp, slot) for hp in range(HP)]
            for hp in range(HP):
                finish(c, hp, slot, dbl_state[hp], *crossed[hp])

        def step(c, slot):
            closing = retire(c + 1, slot)
            next(closing)
            for hp in range(HP):
                issue(c, hp, 1 - slot)
            next(closing, None)

        def trip(j, carry):
            step(nck - 2 - 2 * j, 0)
            step(nck - 3 - 2 * j, 1)
            return carry

        def finish(c, hp, slot, dbl_state, dqe1, dke0, dv1):
            rs, cs, q, k, bc, v, do = operands(c, hp)
            eb, ekd = jnp.exp(bc), jnp.exp(bc[CH - 1:CH, :] - bc)
            qe, kd = q * eb, k * ekd
            qe1, e1, ke0, e0 = _cross_half(q, k, bc)
            dqe, dkd, dv = car_r[slot, hp, 0:CH], car_r[slot, hp, CH:2 * CH], car_r[slot, hp, 2 * CH:3 * CH]
            a, da, decs = car_a[slot, hp], car_da[slot, hp], car_dec[slot, hp]
            dec = [decs[_OFF_OF[s]:_OFF_OF[s] + _ROWS_OF[s], :] for s in range(CH)]
            dbl = jnp.sum(dkd * kd, axis=0, keepdims=True) + dbl_state
            dq_acc = [jnp.zeros((SUBLANES, REC_K), F32) for _ in range(ngrp)]
            uk, uv = [], []
            for s in range(CH):
                j = s // SUBLANES
                r0 = j * SUBLANES
                ks = k[s:s + 1, :]
                for jj in range(j, _ROW1[s] // SUBLANES):
                    lo, hi = _OFF_OF[s] + (jj - j) * SUBLANES, _OFF_OF[s] + (jj - j + 1) * SUBLANES
                    a_blk, da_blk = a[lo:hi, :], da[lo:hi, :]
                    if jj == j:
                        keep = sub >= s - r0
                        a_blk, da_blk = jnp.where(keep, a_blk, 0.0), jnp.where(keep, da_blk, 0.0)
                    rows = slice(jj * SUBLANES, (jj + 1) * SUBLANES)
                    tt = da_blk * dec[s][(jj - j) * SUBLANES:(jj - j + 1) * SUBLANES, :]
                    dq_acc[jj] = dq_acc[jj] + tt * ks
                    uk.append(tt * q[rows, :])
                    uv.append(a_blk * do[rows, :])
            dk_in = jnp.dot(pick, jnp.concatenate(uk, axis=0).astype(BF16), preferred_element_type=F32)
            dv_in = jnp.dot(pick, jnp.concatenate(uv, axis=0).astype(BF16), preferred_element_type=F32)
            zero_half = jnp.zeros((HALF, REC_K), F32)
            dq_x = jnp.concatenate([zero_half, dqe1 * e1], axis=0)
            dk_x = jnp.concatenate([dke0 * e0, zero_half], axis=0)
            dv_x = jnp.concatenate([dv1, zero_half], axis=0)
            db_x = jnp.concatenate([-(dke0 * ke0), dqe1 * qe1], axis=0)
            dq_in = jnp.concatenate(dq_acc, axis=0)
            dqv_s[rs, cs] = dqe * eb + dq_in + dq_x
            dk_s[rs, cs] = dkd * ekd + dk_in + dk_x
            di_ref[rs, cs] = (dv + dv_in + dv_x).astype(BF16)
            db = dqe * qe - dkd * kd + q * dq_in - k * dk_in + db_x
            db_s[rs, cs] = db + jnp.where(rowid == CH - 1, dbl, 0.0)

        assert nck % 2 == 0
        for hp in range(HP):
            issue(nck - 1, hp, 0)
        lax.fori_loop(0, nck // 2 - 1, trip, 0)
        step(0, 0)
        for _ in retire(0, 1):
            pass
        up = _tri(False)
        sgq = _sigmoid(qp)
        dq_ref[...] = (dqv_s[...] * (sgq * (1.0 + qp * (1.0 - sgq)))).astype(BF16)
        dlb_acc = jnp.zeros((1, _HW), F32)
        for r in range(0, tb, _CUM_ROWS):
            rows = slice(r, r + _CUM_ROWS)
            dgl = _dot3(up, db_s[rows, :])
            dfg = dgl * jnp.exp(-g[rows, :]) - dk_s[rows, :]
            sn = sig_neg[rows, :]
            df_ref[rows, :] = (dfg * (1.0 - lbv) * (1.0 - sn) * sn).astype(BF16)
            dlb_acc = dlb_acc + jnp.sum(dfg * sn, axis=0, keepdims=True)
        dlb_ref[...] += dlb_acc

    blk, head, lbs, gws, hist = _hgrn_specs(tb, nt, True)
    out_specs = [pl.BlockSpec((tb, REC_IN), lambda h, b, t: (b * nt + nt - 1 - t, 0)), lbs,
                 pl.BlockSpec((HP, 1, REC_K), lambda h, b, t: (h, 0, 0))]
    out_shape = [jax.ShapeDtypeStruct((n, REC_IN), BF16),
                 jax.ShapeDtypeStruct((1, REC_WIDTH), F32), jax.ShapeDtypeStruct((REC_HEADS, 1, REC_K), F32)]
    return pl.pallas_call(
        body, name="hgrn_bwd", grid=(REC_HEADS // HP, b_loc, nt),
        in_specs=[blk(0), blk(1), blk(2), blk(3), lbs, gws, head, head, hist],
        out_specs=out_specs, out_shape=out_shape,
        scratch_shapes=[pltpu.VMEM((tb, _HW), F32)] * 7 + [pltpu.VMEM((HP, REC_K, REC_K), F32)] + [
            pltpu.VMEM((2, HP, 3 * CH, REC_K), F32), pltpu.VMEM((2, HP, REC_K, REC_K), F32)] + [
            pltpu.VMEM((2, HP, sum(_ROWS_OF), REC_K), F32)] * 3 + [pltpu.VMEM((2, HP, 3 * HALF, HALF), F32)],
        compiler_params=_params(("arbitrary", "arbitrary", "arbitrary"), 56),
    )(p, p, p, p, lb, gw, oraw, dg, sh)


def _postnorm_bwd_nt(dxo, y, qw, w, name):
    n = dxo.shape[0]

    def body(dx_ref, y_ref, qw_ref, w_ref, dg_ref, dy_ref, dqw_ref, db_ref):
        @pl.when(pl.program_id(0) == 0)
        def _():
            dqw_ref[...] = jnp.zeros_like(dqw_ref)
            db_ref[...] = jnp.zeros_like(db_ref)

        yv, dxv = y_ref[...], dx_ref[...]
        r = lax.rsqrt(jnp.mean(yv * yv, axis=-1, keepdims=True) + NORM_EPS)
        u = yv * r
        du = dxv * qw_ref[...]
        dy = r * (du - u * jnp.mean(du * u, axis=-1, keepdims=True))
        dqw_ref[...] += jnp.sum(dxv * u, axis=0, keepdims=True)
        db_ref[...] += jnp.sum(dy, axis=0, keepdims=True)
        dyb = dy.astype(BF16)
        dy_ref[...] = dyb
        dg_ref[...] = lax.dot_general(dyb, w_ref[...], NT_DIMS, preferred_element_type=F32)

    rows = pl.BlockSpec((TM, D_MODEL), lambda i: (i, 0))
    const = lambda shape: pl.BlockSpec(shape, lambda i: (0, 0))
    return pl.pallas_call(
        body, name=name, grid=(n // TM,), in_specs=[rows, rows, const((1, D_MODEL)), const((D_MODEL, D_MODEL))],
        out_specs=[rows, rows, const((1, D_MODEL)), const((1, D_MODEL))],
        out_shape=[jax.ShapeDtypeStruct((n, D_MODEL), F32), jax.ShapeDtypeStruct((n, D_MODEL), BF16),
                   jax.ShapeDtypeStruct((1, D_MODEL), F32), jax.ShapeDtypeStruct((1, D_MODEL), F32)],
        compiler_params=_params(("arbitrary",), 48),
    )(dxo, y, qw, w)


def _nt_prenorm_bwd(dps, w, x, pw, dxo, has_bias, name, parts=()):
    n = x.shape[0]
    widths = [d.shape[1] for d in dps]
    m = sum(widths)
    npieces, nparts, steps = len(dps), len(parts), n // TM

    def body(*refs):
        dp_refs = refs[:npieces]
        w_ref, x_ref, pw_ref, dxo_ref = refs[npieces:npieces + 4]
        part_refs = refs[npieces + 4:npieces + 4 + nparts]
        dx_ref, dpw_ref, db_ref = refs[npieces + 4 + nparts:npieces + 7 + nparts]
        land_refs = refs[npieces + 7 + nparts:npieces + 7 + 2 * nparts]
        sems = refs[npieces + 7 + 2 * nparts:]

        @pl.when(pl.program_id(0) == 0)
        def _():
            dpw_ref[...] = jnp.zeros_like(dpw_ref)
            db_ref[...] = jnp.zeros_like(db_ref)
            if nparts:
                _scatter_start(part_refs, land_refs, sems)

        dh = jnp.zeros((TM, D_MODEL), F32)
        off = 0
        for dp_ref, wd in zip(dp_refs, widths):
            cn = _col_chunk(wd)
            for j in range(0, wd, cn):
                dpc = dp_ref[:, j:j + cn]
                if has_bias:
                    db_ref[:, off + j:off + j + cn] += jnp.sum(dpc, axis=0, keepdims=True)
                dh = dh + lax.dot_general(dpc.astype(BF16), w_ref[:, off + j:off + j + cn], NT_DIMS, preferred_element_type=F32)
            off += wd
        xv = x_ref[...]
        r = lax.rsqrt(jnp.mean(xv * xv, axis=-1, keepdims=True) + NORM_EPS)
        xn = xv * r
        dpw_ref[...] += jnp.sum(dh * xn, axis=0, keepdims=True)
        dxn = dh * pw_ref[...]
        dx_ref[...] = dxo_ref[...] + r * (dxn - xn * jnp.mean(dxn * xn, axis=-1, keepdims=True))

        if nparts:
            @pl.when(pl.program_id(0) == steps - 1)
            def _():
                _scatter_wait(part_refs, land_refs, sems)

    rows = pl.BlockSpec((TM, D_MODEL), lambda i: (i, 0))
    const = lambda shape: pl.BlockSpec(shape, lambda i: (0, 0))
    hbm = pl.BlockSpec(memory_space=pl.ANY)
    in_specs = ([pl.BlockSpec((TM, wd), lambda i: (i, 0)) for wd in widths] + [const((D_MODEL, m)), rows, const((1, D_MODEL)), rows]
                + [hbm] * nparts)
    return pl.pallas_call(
        body, name=name, grid=(steps,), in_specs=in_specs,
        out_specs=[rows, const((1, D_MODEL)), const((1, m))] + [hbm] * nparts,
        out_shape=[jax.ShapeDtypeStruct((n, D_MODEL), F32), jax.ShapeDtypeStruct((1, D_MODEL), F32),
                   jax.ShapeDtypeStruct((1, m), F32)] + _scatter_lands(parts),
        scratch_shapes=_scatter_sems(nparts) if nparts else [],
        compiler_params=_params(("arbitrary",), 56),
    )(*dps, w, x, pw, dxo, *parts)


def _matmul_tn(a, b, name):
    n, k = a.shape
    m = b.shape[1]
    tk, tm, tn = k, _col_chunk(m), TN_ROWS if n % TN_ROWS == 0 else n

    def body(a_ref, b_ref, o_ref):
        @pl.when(pl.program_id(2) == 0)
        def _():
            o_ref[...] = jnp.zeros_like(o_ref)

        o_ref[...] += lax.dot_general(a_ref[...], b_ref[...].astype(BF16), TN_DIMS, preferred_element_type=F32)

    return pl.pallas_call(
        body, name=name, grid=(k // tk, m // tm, n // tn),
        in_specs=[pl.BlockSpec((tn, tk), lambda i, j, l: (l, i)), pl.BlockSpec((tn, tm), lambda i, j, l: (l, j))],
        out_specs=pl.BlockSpec((tk, tm), lambda i, j, l: (i, j)),
        out_shape=jax.ShapeDtypeStruct((k, m), F32),
        compiler_params=_params(("arbitrary", "arbitrary", "arbitrary"), 48),
    )(a, b)


def _matmul_tn_by_owner(a, b, name):
    n, k = a.shape
    c = b.shape[1] // N_DEV
    tn = TN_ROWS if n % TN_ROWS == 0 else n
    steps = n // tn
    per = 2

    def body(a_ref, b_ref, o_ref, w_ref):
        @pl.when(pl.program_id(1) == 0)
        def _():
            o_ref[...] = jnp.zeros_like(o_ref)

        r = lax.dot_general(a_ref[...], b_ref[...], TN_DIMS, preferred_element_type=F32)
        for j in range(per):
            o_ref[j] += r[:, j * c:(j + 1) * c]

        @pl.when(pl.program_id(1) == steps - 1)
        def _():
            w_ref[...] = o_ref[...].astype(BF16)

    out = pl.BlockSpec((per, k, c), lambda j, l: (j, 0, 0))
    return pl.pallas_call(
        body, name=name, grid=(N_DEV // per, steps),
        in_specs=[pl.BlockSpec((tn, k), lambda j, l: (l, 0)), pl.BlockSpec((tn, per * c), lambda j, l: (l, j))],
        out_specs=[out, out],
        out_shape=[jax.ShapeDtypeStruct((N_DEV, k, c), F32), jax.ShapeDtypeStruct((N_DEV, k, c), BF16)],
        compiler_params=_params(("arbitrary", "arbitrary"), 48),
    )(a, b)


def _by_owner_cols(dw):
    k, m = dw.shape
    return dw.reshape(k, N_DEV, m // N_DEV).transpose(1, 0, 2)


def _own_and_bf16(part):
    return lax.dynamic_index_in_dim(part, _my_id(), 0, keepdims=False), part.astype(BF16)


def _step(x, pos_col, tgt, pre_w, post_w, wa_in, ba_in, sinks, wa_out_shard, ba_out, wr_in_shard, lb_logits, gnorm_w, wr_out_shard, b_loc, t_len):
    nb = t_len // BLK
    ct, st = _rope_tables(pos_col)
    lb = _lower_bound(lb_logits)
    p0, h0, ga_out = _norm_matmul(x, pre_w[0:1], wa_in, ba_in, "attn_in_proj", [wa_out_shard])
    wa_out = ga_out.reshape(ATTN_WIDTH, D_MODEL)
    o0, g0, gr_in, gr_out = _attn_fwd(p0, ct, st, sinks, b_loc, nb, [wr_in_shard, wr_out_shard])
    wr_in = gr_in.transpose(1, 0, 2).reshape(D_MODEL, REC_IN)
    wr_out = gr_out.reshape(REC_WIDTH, D_MODEL)
    y0, x1 = _outproj_postnorm(g0, wa_out, ba_out, x, post_w[0:1], "attn_out_proj")
    p1, h1 = _norm_matmul(x1, pre_w[1:2], wr_in, None, "rec_in_proj")
    o1, g1, sh = _hgrn_fwd(p1, lb, gnorm_w, b_loc, t_len)
    dx2, dg1, dy1, dpost1, loss_tile = _outproj_loss_bwd(g1, wr_out, x1, post_w[1:2], tgt, "rec_out_proj_loss_bwd")
    d_wr_out = _matmul_tn(g1, dy1, "rec_w_out_grad")
    dp1, dlb, dgw = _hgrn_bwd(p1, lb, gnorm_w, o1, sh, dg1, b_loc, t_len)
    dx1, dpre1, _ = _nt_prenorm_bwd([dp1], wr_in, x1, pre_w[1:2], dx2, False, "rec_in_bwd")
    part_r_in, wire_r_in = _matmul_tn_by_owner(h1, dp1, "rec_w_in_grad")
    own_r_in = lax.dynamic_index_in_dim(part_r_in, _my_id(), 0, keepdims=False)
    dg0, dy0, dpost0, dba_out = _postnorm_bwd_nt(dx1, y0, post_w[0:1], wa_out, "attn_out_bwd")
    d_wa_out = _matmul_tn(g0, dy0, "attn_w_out_grad")
    owns, wires = zip(*[_own_and_bf16(part) for part in (
        d_wr_out.reshape(N_DEV, REC_WIDTH // N_DEV, D_MODEL), d_wa_out.reshape(N_DEV, ATTN_WIDTH // N_DEV, D_MODEL))])
    owns, wires = (own_r_in,) + owns, (wire_r_in,) + wires
    dp0, dsink_tile, *lands = _attn_bwd(p0, ct, st, sinks, o0, dg0, b_loc, nb, list(wires))
    d_wa_in = _matmul_tn(h0, dp0, "attn_w_in_grad")
    own_a_in, wire_a_in = _own_and_bf16(_by_owner_cols(_qkvz(d_wa_in)))
    dx0, dpre0, dba_in, land_a_in = _nt_prenorm_bwd([dp0], wa_in, x, pre_w[0:1], dx1, True, "attn_in_bwd", [wire_a_in])
    small = dict(pre=jnp.concatenate([dpre0, dpre1], axis=0), post=jnp.concatenate([dpost0, dpost1], axis=0),
                 ba_in=dba_in, sinks=dsink_tile[0:1, 0:N_HEADS], ba_out=dba_out, lb=dlb, gnorm=jnp.sum(dgw, axis=0))
    return loss_tile, dx0, list(zip(lands, owns)) + [(land_a_in, own_a_in)], small


def _my_id():
    return lax.axis_index("x") * 4 + lax.axis_index("y") * 2 + lax.axis_index("c")


def _peer(k):
    x, y, c = lax.axis_index("x"), lax.axis_index("y"), lax.axis_index("c")
    return (x ^ ((k >> 2) & 1), y ^ ((k >> 1) & 1), c ^ (k & 1))


def _peer_id(k):
    return _my_id() ^ k


def _all_gather_by_chip(shard):
    def body(x_ref, out_ref, send_sems, recv_sems, local_sem):
        x, y, c = lax.axis_index("x"), lax.axis_index("y"), lax.axis_index("c")
        me, sibling = (x, y, c), (x, y, 1 - c)
        chips = [(1 - x, y), (x, 1 - y), (1 - x, 1 - y)]

        def rows(px, py, pc):
            return out_ref.at[4 * px + 2 * py + pc]

        def copy(k, block, to, src=None):
            return pltpu.make_async_remote_copy(src_ref=rows(*block) if src is None else src, dst_ref=rows(*block),
                                                send_sem=send_sems.at[k], recv_sem=recv_sems.at[k], device_id=to, device_id_type=MESH)

        mine = pltpu.make_async_copy(x_ref, rows(*me), local_sem)
        mine.start()
        first = [copy(0, me, sibling, src=x_ref)] + [copy(1 + j, me, (*chip, c), src=x_ref) for j, chip in enumerate(chips)]
        for cp in first:
            cp.start()
        passed = [copy(4 + j, (*chip, c), sibling) for j, chip in enumerate(chips)]
        for j, chip in enumerate(chips):
            copy(1 + j, (*chip, c), me).wait_recv()
            passed[j].start()
        copy(0, sibling, me).wait_recv()
        for j, chip in enumerate(chips):
            copy(4 + j, (*chip, 1 - c), me).wait_recv()
        for cp in first + passed:
            cp.wait_send()
        mine.wait()

    hbm = pl.BlockSpec(memory_space=pl.ANY)
    return pl.pallas_call(
        body, name="comm_all_gather_by_chip", in_specs=[hbm], out_specs=hbm,
        out_shape=jax.ShapeDtypeStruct((N_DEV,) + shard.shape, shard.dtype),
        scratch_shapes=[pltpu.SemaphoreType.DMA((N_DEV - 1,)), pltpu.SemaphoreType.DMA((N_DEV - 1,)), pltpu.SemaphoreType.DMA],
    )(shard)


def _gather_shapes(shards):
    return [jax.ShapeDtypeStruct((N_DEV,) + s.shape, s.dtype) for s in shards]


def _gather_sems(nsh):
    return [pltpu.SemaphoreType.DMA((nsh, N_DEV - 1)), pltpu.SemaphoreType.DMA((nsh, N_DEV - 1)), pltpu.SemaphoreType.DMA((nsh,))]


def _gather_copies(ins, outs, sems, received):
    send_sems, recv_sems, local_sems = sems
    me = _my_id()
    local = [pltpu.make_async_copy(ins[a], outs[a].at[me], local_sems.at[a]) for a in range(len(ins))]
    remote = [pltpu.make_async_remote_copy(
        src_ref=ins[a], dst_ref=outs[a].at[_peer_id(k) if received else me], send_sem=send_sems.at[a, k - 1],
        recv_sem=recv_sems.at[a, k - 1], device_id=_peer(k), device_id_type=MESH)
        for a in range(len(ins)) for k in range(1, N_DEV)]
    return local, remote


def _gather_start(ins, outs, sems):
    local, sends = _gather_copies(ins, outs, sems, False)
    for cp in local + sends:
        cp.start()


def _gather_wait(ins, outs, sems):
    local, recvs = _gather_copies(ins, outs, sems, True)
    for cp in recvs:
        cp.wait_recv()
    for cp in recvs:
        cp.wait_send()
    for cp in local:
        cp.wait()


def _scatter_lands(parts):
    return [jax.ShapeDtypeStruct((N_DEV - 1,) + p.shape[1:], p.dtype) for p in parts]


def _scatter_sems(nparts):
    return [pltpu.SemaphoreType.DMA((nparts, N_DEV - 1)), pltpu.SemaphoreType.DMA((nparts, N_DEV - 1))]


def _scatter_copies(parts, lands, sems):
    send_sems, recv_sems = sems
    return [pltpu.make_async_remote_copy(
        src_ref=parts[a].at[_peer_id(k)], dst_ref=lands[a].at[k - 1], send_sem=send_sems.at[a, k - 1],
        recv_sem=recv_sems.at[a, k - 1], device_id=_peer(k), device_id_type=MESH)
        for a in range(len(parts)) for k in range(1, N_DEV)]


def _scatter_start(parts, lands, sems):
    for cp in _scatter_copies(parts, lands, sems):
        cp.start()


def _scatter_wait(parts, lands, sems):
    copies = _scatter_copies(parts, lands, sems)
    for cp in copies:
        cp.wait_recv()
    for cp in copies:
        cp.wait_send()


def _adamw(w, g, m, v):
    m2 = ADAM_B1 * m + (1.0 - ADAM_B1) * g
    v2 = ADAM_B2 * v + (1.0 - ADAM_B2) * (g * g)
    m_hat = m2 / (1.0 - ADAM_B1 ** ADAM_STEP)
    v_hat = v2 / (1.0 - ADAM_B2 ** ADAM_STEP)
    delta = -ADAM_LR * (m_hat / (jnp.sqrt(v_hat) + ADAM_EPS) + ADAM_WD * w)
    return delta, m2, v2


def _sum_adamw(land, own, w, m, v, name):
    r, c = own.shape
    rb = min(r, 256)

    def body(land_ref, own_ref, w_ref, m_ref, v_ref, g_ref, d_ref, m2_ref, v2_ref):
        me = _my_id()
        g = jnp.zeros((rb, c), F32)
        for dev in range(N_DEV):
            k = dev ^ me
            g = g + jnp.where(k == 0, own_ref[...], land_ref[jnp.maximum(k - 1, 0)].astype(F32))
        delta, m2, v2 = _adamw(w_ref[...], g, m_ref[...], v_ref[...])
        g_ref[...] = g
        d_ref[...] = delta
        m2_ref[...] = m2
        v2_ref[...] = v2

    rows = pl.BlockSpec((rb, c), lambda i: (i, 0))
    return pl.pallas_call(
        body, name=name, grid=(r // rb,), in_specs=[pl.BlockSpec((N_DEV - 1, rb, c), lambda i: (0, i, 0))] + [rows] * 4,
        out_specs=[rows] * 4, out_shape=[jax.ShapeDtypeStruct((r, c), F32)] * 4,
        compiler_params=_params(("arbitrary",), 32),
    )(land, own, w, m, v)


_SMALL = [("pre_norm_w", 2, D_MODEL), ("post_norm_w", 2, D_MODEL), ("attn_b_out", 1, D_MODEL), ("rec_lb_logits", 2, D_MODEL),
          ("attn_b_in", 3, ATTN_IN), ("attn_sinks", 1, N_HEADS), ("rec_gnorm_w", 1, REC_K), ("loss", 1, 1)]
_SMALL_OFF = {name: sum(r for _, r, _ in _SMALL[:i]) for i, (name, _, _) in enumerate(_SMALL)}
_SMALL_ROWS = 2 * SUBLANES
assert sum(r for _, r, _ in _SMALL) <= _SMALL_ROWS


def _pack_small(pieces):
    out = []
    for name, rows, width in _SMALL:
        a = pieces[name].astype(F32)
        if width != D_MODEL:
            a = jnp.pad(a.reshape(1, width), ((0, 0), (0, rows * D_MODEL - width)))
        out.append(a.reshape(rows, D_MODEL))
    out.append(jnp.zeros((_SMALL_ROWS - sum(r for _, r, _ in _SMALL), D_MODEL), F32))
    return jnp.concatenate(out, axis=0)


def _unpack_small(packed, shapes):
    res = {}
    for name, rows, width in _SMALL:
        a = packed[_SMALL_OFF[name]:_SMALL_OFF[name] + rows]
        if width != D_MODEL:
            a = a.reshape(1, rows * D_MODEL)[:, :width]
        res[name] = a.reshape(shapes[name])
    return res


def _small_allreduce_adamw(gpart, w, m, v):
    lb0 = _SMALL_OFF["rec_lb_logits"]

    def body(gp_ref, w_ref, m_ref, v_ref, g_ref, d_ref, m2_ref, v2_ref, land_ref, send_sems, recv_sems):
        me = _my_id()
        sent = []
        for k in range(1, N_DEV):
            cp = pltpu.make_async_remote_copy(src_ref=gp_ref, dst_ref=land_ref.at[k - 1], send_sem=send_sems.at[k - 1],
                                              recv_sem=recv_sems.at[k - 1], device_id=_peer(k), device_id_type=MESH)
            cp.start()
            sent.append(cp)
        for cp in sent:
            cp.wait_recv()
        for cp in sent:
            cp.wait_send()
        g = jnp.zeros((_SMALL_ROWS, D_MODEL), F32)
        for dev in range(N_DEV):
            k = dev ^ me
            g = g + jnp.where(k == 0, gp_ref[...], land_ref[jnp.maximum(k - 1, 0)])
        g_ref[...] = g
        l0, l1 = w_ref[lb0:lb0 + 1, :], w_ref[lb0 + 1:lb0 + 2, :]
        mx = jnp.maximum(l0, l1)
        e0, e1 = jnp.exp(l0 - mx), jnp.exp(l1 - mx)
        p1 = e1 / (e0 + e1)
        dl1 = (1.0 - p1) * p1 * g[lb0:lb0 + 1, :]
        g_ref[lb0:lb0 + 1, :] = -dl1
        g_ref[lb0 + 1:lb0 + 2, :] = dl1
        delta, m2, v2 = _adamw(w_ref[...], g_ref[...], m_ref[...], v_ref[...])
        d_ref[...] = delta
        m2_ref[...] = m2
        v2_ref[...] = v2

    vmem = pl.BlockSpec(memory_space=pltpu.VMEM)
    return pl.pallas_call(
        body, name="comm_small_allreduce_adamw", in_specs=[vmem] * 4, out_specs=[vmem] * 4,
        out_shape=[jax.ShapeDtypeStruct((_SMALL_ROWS, D_MODEL), F32)] * 4,
        scratch_shapes=[pltpu.VMEM((N_DEV - 1, _SMALL_ROWS, D_MODEL), F32), pltpu.SemaphoreType.DMA((N_DEV - 1,)),
                        pltpu.SemaphoreType.DMA((N_DEV - 1,))],
    )(gpart, w, m, v)


def _qzkv(a):
    kv_end = ATTN_WIDTH + 2 * KV_WIDTH
    return jnp.concatenate([a[..., :ATTN_WIDTH], a[..., kv_end:], a[..., ATTN_WIDTH:kv_end]], axis=-1)


def _qkvz(a):
    return jnp.concatenate([a[..., :ATTN_WIDTH], a[..., 2 * ATTN_WIDTH:], a[..., ATTN_WIDTH:2 * ATTN_WIDTH]], axis=-1)


def kernel(x, positions, pre_norm_w, post_norm_w, attn_w_in, attn_b_in, attn_sinks, attn_w_out, attn_b_out, rec_w_in, rec_lb_logits, rec_gnorm_w, rec_w_out, loss_target, m_pre_norm_w, m_post_norm_w, m_attn_w_in, m_attn_b_in, m_attn_sinks, m_attn_w_out, m_attn_b_out, m_rec_w_in, m_rec_lb_logits, m_rec_gnorm_w, m_rec_w_out, v_pre_norm_w, v_post_norm_w, v_attn_w_in, v_attn_b_in, v_attn_sinks, v_attn_w_out, v_attn_b_out, v_rec_w_in, v_rec_lb_logits, v_rec_gnorm_w, v_rec_w_out):
    b_loc, t_len, _ = x.shape
    n = b_loc * t_len
    ga_in = _all_gather_by_chip(attn_w_in[0].astype(BF16))
    wa_in = _qzkv(ga_in.transpose(1, 0, 2).reshape(D_MODEL, ATTN_IN))

    loss_tile, dx, landed, small = _step(
        x.reshape(n, D_MODEL), positions.reshape(n, 1).astype(F32), loss_target.reshape(n, D_MODEL),
        pre_norm_w, post_norm_w, wa_in, _qzkv(attn_b_in), attn_sinks, attn_w_out[0].astype(BF16), attn_b_out,
        rec_w_in[0].astype(BF16), rec_lb_logits, rec_gnorm_w, rec_w_out[0].astype(BF16), b_loc, t_len)

    lift = lambda outs: tuple(a[None] for a in outs)
    (l_r_in, o_r_in), (l_r_out, o_r_out), (l_a_out, o_a_out), (l_a_in, o_a_in) = landed
    r_a_in = lift(_sum_adamw(l_a_in, o_a_in, attn_w_in[0], m_attn_w_in[0], v_attn_w_in[0], "adamw_attn_w_in"))
    r_r_in = lift(_sum_adamw(l_r_in, o_r_in, rec_w_in[0], m_rec_w_in[0], v_rec_w_in[0], "adamw_rec_w_in"))
    r_r_out = lift(_sum_adamw(l_r_out, o_r_out, rec_w_out[0], m_rec_w_out[0], v_rec_w_out[0], "adamw_rec_w_out"))
    r_a_out = lift(_sum_adamw(l_a_out, o_a_out, attn_w_out[0], m_attn_w_out[0], v_attn_w_out[0], "adamw_attn_w_out"))

    gsmall = dict(pre_norm_w=small["pre"], post_norm_w=small["post"], attn_b_in=_qkvz(small["ba_in"]), attn_sinks=small["sinks"],
                  attn_b_out=small["ba_out"], rec_lb_logits=jnp.concatenate([small["lb"], jnp.zeros_like(small["lb"])], axis=0),
                  rec_gnorm_w=small["gnorm"], loss=loss_tile[0:1, 0:1])
    nil = jnp.zeros((1, 1), F32)
    wsmall = dict(pre_norm_w=pre_norm_w, post_norm_w=post_norm_w, attn_b_in=attn_b_in, attn_sinks=attn_sinks,
                  attn_b_out=attn_b_out, rec_lb_logits=rec_lb_logits, rec_gnorm_w=rec_gnorm_w, loss=nil)
    msmall = dict(pre_norm_w=m_pre_norm_w, post_norm_w=m_post_norm_w, attn_b_in=m_attn_b_in, attn_sinks=m_attn_sinks,
                  attn_b_out=m_attn_b_out, rec_lb_logits=m_rec_lb_logits, rec_gnorm_w=m_rec_gnorm_w, loss=nil)
    vsmall = dict(pre_norm_w=v_pre_norm_w, post_norm_w=v_post_norm_w, attn_b_in=v_attn_b_in, attn_sinks=v_attn_sinks,
                  attn_b_out=v_attn_b_out, rec_lb_logits=v_rec_lb_logits, rec_gnorm_w=v_rec_gnorm_w, loss=nil)
    shapes = {k: a.shape for k, a in wsmall.items()}
    packed = _small_allreduce_adamw(_pack_small(gsmall), _pack_small(wsmall), _pack_small(msmall), _pack_small(vsmall))
    sg, sd, sm, sv = [_unpack_small(a, shapes) for a in packed]

    big = {"attn_w_in": r_a_in, "attn_w_out": r_a_out, "rec_w_in": r_r_in, "rec_w_out": r_r_out}
    order = ["pre_norm_w", "post_norm_w", "attn_w_in", "attn_b_in", "attn_sinks", "attn_w_out", "attn_b_out", "rec_w_in",
             "rec_lb_logits", "rec_gnorm_w", "rec_w_out"]
    outs = [sg["loss"][0, 0], dx.reshape(b_loc, t_len, D_MODEL)]
    for idx, small_set in enumerate((sg, sd, sm, sv)):
        outs += [big[nm][idx] if nm in big else small_set[nm] for nm in order]
    return tuple(outs)
```

```python
import numpy as np
import jax
import jax.numpy as jnp
from jax import lax
from jax.experimental import pallas as pl
from jax.experimental.pallas import tpu as pltpu

F32, BF16 = jnp.float32, jnp.bfloat16
MESH = pl.DeviceIdType.MESH
N_DEV = 8

D_MODEL = 1024
N_HEADS, HEAD_DIM, N_KV, GROUP = 16, 64, 2, 8
ATTN_WIDTH, KV_WIDTH = 1024, 128
ATTN_IN = 2 * ATTN_WIDTH + 2 * KV_WIDTH
BLK = 128
ROPE_THETA, ROPE_HALF = 500000.0, 8
REC_HEADS, REC_K = 8, 128
REC_WIDTH = REC_HEADS * REC_K
REC_IN = 4 * REC_WIDTH
TN_ROWS = 1024
CH = 32
NORM_EPS = 1e-6
F32_TINY = 1.1754944e-38
ADAM_LR, ADAM_B1, ADAM_B2, ADAM_EPS, ADAM_WD, ADAM_STEP = 0.001, 0.9, 0.999, 1e-08, 0.01, 10

LANES, SUBLANES = 128, 8
TM = 512
NT_DIMS = (((1,), (1,)), ((), ()))
TN_DIMS = (((0,), (0,)), ((), ()))
MB = 2 ** 20


def _params(sem=None, vmem_mb=48, **kw):
    return pltpu.CompilerParams(dimension_semantics=sem, vmem_limit_bytes=vmem_mb * MB, **kw)


def _col_chunk(m):
    return 768 if m % 1024 else 1024


def _sigmoid(x):
    return 1.0 / (1.0 + jnp.exp(-x))


def _split3(x):
    hi = x.astype(BF16)
    r1 = x - hi.astype(F32)
    mid = r1.astype(BF16)
    lo = (r1 - mid.astype(F32)).astype(BF16)
    return hi, mid, lo


def _dot3(l_bf, x):
    hi, mid, lo = _split3(x)
    return (jnp.dot(l_bf, hi, preferred_element_type=F32) + jnp.dot(l_bf, mid, preferred_element_type=F32)
            + jnp.dot(l_bf, lo, preferred_element_type=F32))


def _rope_lanes():
    lane = np.arange(LANES) % HEAD_DIM
    inv = np.float32(ROPE_THETA) ** (-(np.arange(ROPE_HALF, dtype=np.float32) * np.float32(2.0) / np.float32(2 * ROPE_HALF)))
    freq = np.where(lane < 2 * ROPE_HALF, inv[lane % ROPE_HALF], 0.0).astype(np.float32)[None, :]
    sign = np.where(lane < ROPE_HALF, -1.0, np.where(lane < 2 * ROPE_HALF, 1.0, 0.0)).astype(np.float32)[None, :]
    return jnp.asarray(freq), jnp.asarray(sign)


def _rope_tables_into(p_ref, f_ref, s_ref, c_out, s_out):
    def rows(i, carry):
        rs = pl.ds(pl.multiple_of(i * TM, TM), TM)
        ang = p_ref[rs, :] * f_ref[...]
        c_out[rs, :] = jnp.cos(ang)
        s_out[rs, :] = jnp.sin(ang) * s_ref[...]
        return carry

    lax.fori_loop(0, p_ref.shape[0] // TM, rows, 0)


def _rope_apply(xv, c, s, lm):
    partner = jnp.where(lm < ROPE_HALF, pltpu.roll(xv, LANES - ROPE_HALF, 1), pltpu.roll(xv, ROPE_HALF, 1))
    return xv * c + partner * s


def _rope_bwd(dy, c, s, lm):
    t = dy * s
    partner = jnp.where(lm < ROPE_HALF, pltpu.roll(t, LANES - ROPE_HALF, 1),
                        jnp.where(lm < 2 * ROPE_HALF, pltpu.roll(t, ROPE_HALF, 1), 0.0))
    return dy * c + partner


def _lower_bound(lb_logits):
    def body(l_ref, o_ref):
        l0, l1 = l_ref[0:1, :], l_ref[1:2, :]
        m = jnp.maximum(l0, l1)
        e0, e1 = jnp.exp(l0 - m), jnp.exp(l1 - m)
        o_ref[...] = e1 / (e0 + e1)

    return pl.pallas_call(body, name="lower_bound", out_shape=jax.ShapeDtypeStruct((1, lb_logits.shape[1]), F32))(lb_logits)


def _norm_matmul(x, pw, w, bias, name, shards=()):
    n, m = x.shape[0], w.shape[1]
    cn = _col_chunk(m)
    has_bias = bias is not None
    nsh, steps = len(shards), n // TM

    def body(*refs):
        refs = list(refs)
        x_ref, pw_ref, w_ref = refs[:3]
        b_ref = refs[3] if has_bias else None
        refs = refs[4 if has_bias else 3:]
        sh_in, (p_ref, h_ref), sh_out, sems = refs[:nsh], refs[nsh:nsh + 2], refs[nsh + 2:2 * nsh + 2], refs[2 * nsh + 2:]
        if nsh:
            @pl.when(pl.program_id(0) == 0)
            def _():
                _gather_start(sh_in, sh_out, sems)

        xv = x_ref[...]
        r = lax.rsqrt(jnp.mean(xv * xv, axis=-1, keepdims=True) + NORM_EPS)
        h = ((xv * r) * pw_ref[...]).astype(BF16)
        h_ref[...] = h
        for j in range(0, m, cn):
            acc = jnp.dot(h, w_ref[:, j:j + cn], preferred_element_type=F32)
            if has_bias:
                acc = acc + b_ref[:, j:j + cn]
            p_ref[:, j:j + cn] = acc

        if nsh:
            @pl.when(pl.program_id(0) == steps - 1)
            def _():
                _gather_wait(sh_in, sh_out, sems)

    rows = pl.BlockSpec((TM, D_MODEL), lambda i: (i, 0))
    const = lambda shape: pl.BlockSpec(shape, lambda i: (0, 0))
    hbm = pl.BlockSpec(memory_space=pl.ANY)
    in_specs = [rows, const((1, D_MODEL)), const((D_MODEL, m))] + ([const((1, m))] if has_bias else []) + [hbm] * nsh
    args = (x, pw, w) + ((bias,) if has_bias else ()) + tuple(shards)
    return pl.pallas_call(
        body, name=name, grid=(steps,), in_specs=in_specs,
        out_specs=[pl.BlockSpec((TM, m), lambda i: (i, 0)), rows] + [hbm] * nsh,
        out_shape=[jax.ShapeDtypeStruct((n, m), F32), jax.ShapeDtypeStruct((n, D_MODEL), BF16)] + _gather_shapes(shards),
        scratch_shapes=_gather_sems(nsh) if nsh else [],
        compiler_params=_params(("arbitrary",), 56),
    )(*args)


def _outproj_postnorm(g, w, bias, xres, qw, name):
    n = g.shape[0]

    def body(g_ref, w_ref, b_ref, x_ref, qw_ref, y_ref, o_ref):
        y = jnp.dot(g_ref[...], w_ref[...], preferred_element_type=F32) + b_ref[...]
        y_ref[...] = y
        r = lax.rsqrt(jnp.mean(y * y, axis=-1, keepdims=True) + NORM_EPS)
        o_ref[...] = x_ref[...] + (y * r) * qw_ref[...]

    rows = pl.BlockSpec((TM, D_MODEL), lambda i: (i, 0))
    const = lambda shape: pl.BlockSpec(shape, lambda i: (0, 0))
    return pl.pallas_call(
        body, name=name, grid=(n // TM,),
        in_specs=[rows, const((D_MODEL, D_MODEL)), const((1, D_MODEL)), rows, const((1, D_MODEL))],
        out_specs=[rows, rows], out_shape=[jax.ShapeDtypeStruct((n, D_MODEL), F32)] * 2,
        compiler_params=_params(("arbitrary",), 48),
    )(g, w, bias, xres, qw)


def _outproj_loss_bwd(g, w, xres, qw, tgt, name):
    n = g.shape[0]
    steps = n // TM

    def body(g_ref, w_ref, x_ref, qw_ref, t_ref, dx_ref, dg_ref, dy_ref, dqw_ref, loss_ref, acc_ref):
        i = pl.program_id(0)

        @pl.when(i == 0)
        def _():
            acc_ref[...] = jnp.zeros_like(acc_ref)
            dqw_ref[...] = jnp.zeros_like(dqw_ref)

        y = jnp.dot(g_ref[...], w_ref[...], preferred_element_type=F32)
        r = lax.rsqrt(jnp.mean(y * y, axis=-1, keepdims=True) + NORM_EPS)
        u = y * r
        e = (x_ref[...] + u * qw_ref[...]) - t_ref[...]
        dxn = e * (1.0 / D_MODEL)
        dx_ref[...] = dxn
        acc_ref[...] += jnp.sum(e * e, axis=0, keepdims=True)
        du = dxn * qw_ref[...]
        dy = (r * (du - u * jnp.mean(du * u, axis=-1, keepdims=True))).astype(BF16)
        dqw_ref[...] += jnp.sum(dxn * u, axis=0, keepdims=True)
        dy_ref[...] = dy
        dg_ref[...] = lax.dot_general(dy, w_ref[...], NT_DIMS, preferred_element_type=F32)

        @pl.when(i == steps - 1)
        def _():
            loss_ref[...] = jnp.full(loss_ref.shape, jnp.sum(acc_ref[...]) * (0.5 / D_MODEL), F32)

    rows = pl.BlockSpec((TM, D_MODEL), lambda i: (i, 0))
    const = lambda shape: pl.BlockSpec(shape, lambda i: (0, 0))
    return pl.pallas_call(
        body, name=name, grid=(steps,),
        in_specs=[rows, const((D_MODEL, D_MODEL)), rows, const((1, D_MODEL)), rows],
        out_specs=[rows, rows, rows, const((1, D_MODEL)), const((SUBLANES, LANES))],
        out_shape=[jax.ShapeDtypeStruct((n, D_MODEL), F32), jax.ShapeDtypeStruct((n, D_MODEL), F32),
                   jax.ShapeDtypeStruct((n, D_MODEL), BF16), jax.ShapeDtypeStruct((1, D_MODEL), F32),
                   jax.ShapeDtypeStruct((SUBLANES, LANES), F32)],
        scratch_shapes=[pltpu.VMEM((1, D_MODEL), F32)], compiler_params=_params(("arbitrary",), 48),
    )(g, w, xres, qw, tgt)


_QCOL, _ZCOL, _KCOL, _VCOL = 0, 1024, 2048, 2176


def _head_stack(chunks, heads, lt64):
    return jnp.concatenate([jnp.where(lt64 if n % 2 == 0 else ~lt64, chunks[n // 2], 0.0) for n in heads], axis=0)


def _dup_half(x, h, lt64):
    r = pltpu.roll(x, HEAD_DIM, 1)
    return jnp.where(lt64, x, r) if h == 0 else jnp.where(lt64, r, x)


def _pair_chunk(xt, c2):
    a, b = 2 * c2, 2 * c2 + 1
    return jnp.concatenate([xt[:HEAD_DIM, a * BLK:(a + 1) * BLK], xt[HEAD_DIM:, b * BLK:(b + 1) * BLK]], axis=0).T


def _attn_mask_t(i):
    key = lax.broadcasted_iota(jnp.int32, (2 * BLK, BLK), 0)
    qry = lax.broadcasted_iota(jnp.int32, (2 * BLK, BLK), 1)
    valid = (key > qry) & (key <= qry + BLK) & ((key >= BLK) | (i > 0))
    return jnp.tile(jnp.where(valid, 0.0, -1e30), (1, GROUP))


def _attn_probs_t(s, heads, sink_ref, mask):
    s = s + mask
    head = lax.broadcasted_iota(jnp.int32, (1, len(heads) * BLK), 1) >> 7
    sk = jnp.zeros((1, len(heads) * BLK), F32)
    for j, n in enumerate(heads):
        sk = jnp.where(head == j, sink_ref[0, n], sk)
    m = jnp.maximum(jnp.max(s, axis=0, keepdims=True), sk)
    p = jnp.exp(s - m)
    esk = jnp.exp(sk - m)
    inv = 1.0 / (jnp.sum(p, axis=0, keepdims=True) + esk)
    return p * inv, esk * inv


def _attn_fwd(p, ct, st, sinks, b_loc, nb, shards):
    n = p.shape[0]
    nsh = len(shards)

    def body(sink_ref, q_ref, z_ref, kc_ref, kp_ref, vc_ref, vp_ref, cc_ref, sc_ref, cp_ref, sp_ref, *rest):
        sh_in, (o_ref, g_ref), sh_out, sems = rest[:nsh], rest[nsh:nsh + 2], rest[nsh + 2:2 * nsh + 2], rest[2 * nsh + 2:]
        b, i = pl.program_id(0), pl.program_id(1)

        @pl.when((b == 0) & (i == 0))
        def _():
            _gather_start(sh_in, sh_out, sems)

        lane = lax.broadcasted_iota(jnp.int32, (BLK, LANES), 1)
        lm = lane & (HEAD_DIM - 1)
        cc, sc = cc_ref[...], sc_ref[...]
        kcat = jnp.concatenate([_rope_apply(kp_ref[...], cp_ref[...], sp_ref[...], lm),
                                _rope_apply(kc_ref[...], cc, sc, lm)], axis=0)
        vcat = jnp.concatenate([vp_ref[...], vc_ref[...]], axis=0)
        qr = [_rope_apply(q_ref[:, c * LANES:(c + 1) * LANES], cc, sc, lm) * (HEAD_DIM ** -0.5) for c in range(8)]
        valid = _attn_mask_t(i)
        lt64, lt64k = lane < HEAD_DIM, lax.broadcasted_iota(jnp.int32, (2 * BLK, LANES), 1) < HEAD_DIM
        def kv_head(h):
            heads = list(range(h * GROUP, (h + 1) * GROUP))
            kext, vext = _dup_half(kcat, h, lt64k).astype(BF16), _dup_half(vcat, h, lt64k).astype(BF16)
            qst = _head_stack(qr, heads, lt64).astype(BF16)
            s = lax.dot_general(kext, qst, NT_DIMS, preferred_element_type=F32)
            yield
            pn, _ = _attn_probs_t(s, heads, sink_ref, valid)
            ot = lax.dot_general(vext, pn.astype(BF16), TN_DIMS, preferred_element_type=F32)
            yield
            for c2 in range(GROUP // 2):
                oc = _pair_chunk(ot, c2)
                cols = slice((4 * h + c2) * LANES, (4 * h + c2 + 1) * LANES)
                zc = z_ref[:, cols]
                o_ref[:, cols] = oc
                g_ref[:, cols] = (oc * (zc * _sigmoid(zc))).astype(BF16)

        _in_stages([kv_head(h) for h in range(N_KV)])

        @pl.when((b == b_loc - 1) & (i == nb - 1))
        def _():
            _gather_wait(sh_in, sh_out, sems)

    cur = lambda b, i: b * nb + i
    prev = lambda b, i: b * nb + jnp.maximum(i - 1, 0)
    wide = lambda cb: pl.BlockSpec((BLK, ATTN_WIDTH), lambda b, i: (cur(b, i), cb))
    kv = lambda rowf, cb: pl.BlockSpec((BLK, LANES), lambda b, i: (rowf(b, i), cb))
    hbm = pl.BlockSpec(memory_space=pl.ANY)
    in_specs = [pl.BlockSpec(memory_space=pltpu.SMEM), wide(0), wide(1),
                kv(cur, _KCOL // LANES), kv(prev, _KCOL // LANES), kv(cur, _VCOL // LANES), kv(prev, _VCOL // LANES),
                kv(cur, 0), kv(cur, 0), kv(prev, 0), kv(prev, 0)] + [hbm] * nsh
    return pl.pallas_call(
        body, name="attn_fwd", grid=(b_loc, nb), in_specs=in_specs, out_specs=[wide(0), wide(0)] + [hbm] * nsh,
        out_shape=[jax.ShapeDtypeStruct((n, ATTN_WIDTH), F32), jax.ShapeDtypeStruct((n, ATTN_WIDTH), BF16)] + _gather_shapes(shards),
        scratch_shapes=_gather_sems(nsh), compiler_params=_params(("arbitrary", "arbitrary"), 48),
    )(sinks, p, p, p, p, p, p, ct, st, ct, st, *shards)


def _attn_bwd(p, ct, st, sinks, o, dg, b_loc, nb, parts):
    n = p.shape[0]
    nparts = len(parts)

    def body(sink_ref, q_ref, z_ref, kc_ref, kp_ref, vc_ref, vp_ref, cc_ref, sc_ref, cp_ref, sp_ref, o_ref, dg_ref, *rest):
        part_refs, (dp_ref, ds_ref), land_refs = rest[:nparts], rest[nparts:nparts + 2], rest[nparts + 2:2 * nparts + 2]
        dq_s, dz_s, dk_s, dv_s = rest[2 * nparts + 2:2 * nparts + 6]
        sems = rest[2 * nparts + 6:]
        b, i = pl.program_id(0), pl.program_id(1)

        @pl.when((b == 0) & (i == 0))
        def _():
            _scatter_start(part_refs, land_refs, sems)

        @pl.when((b == b_loc - 1) & (i == nb))
        def _():
            _scatter_wait(part_refs, land_refs, sems)

        lane = lax.broadcasted_iota(jnp.int32, (BLK, LANES), 1)
        lm = lane & (HEAD_DIM - 1)

        @pl.when((b == 0) & (i == 0))
        def _():
            ds_ref[...] = jnp.zeros_like(ds_ref)

        @pl.when(i < nb)
        def _compute():
            cc, sc = cc_ref[...], sc_ref[...]
            kcat = jnp.concatenate([_rope_apply(kp_ref[...], cp_ref[...], sp_ref[...], lm),
                                    _rope_apply(kc_ref[...], cc, sc, lm)], axis=0)
            vcat = jnp.concatenate([vp_ref[...], vc_ref[...]], axis=0)
            qr = [_rope_apply(q_ref[:, c * LANES:(c + 1) * LANES], cc, sc, lm) * (HEAD_DIM ** -0.5) for c in range(8)]
            valid = _attn_mask_t(i)
            lt64, lt64k = lane < HEAD_DIM, lax.broadcasted_iota(jnp.int32, (2 * BLK, LANES), 1) < HEAD_DIM
            do_chunks, doo_chunks, dz_chunks = [], [], []
            for c in range(8):
                cols = slice(c * LANES, (c + 1) * LANES)
                zc, oc, dgc = z_ref[:, cols], o_ref[:, cols], dg_ref[:, cols]
                sg = _sigmoid(zc)
                do_chunks.append(dgc * (zc * sg))
                dz_chunks.append(dgc * oc * (sg * (1.0 + zc * (1.0 - sg))))
                doo_chunks.append(do_chunks[c] * oc)
            dq_chunks = [None] * 8
            dk_h, dv_h, ds_parts = [None] * N_KV, [None] * N_KV, [None] * N_KV
            tile_lane = lax.broadcasted_iota(jnp.int32, (SUBLANES, LANES), 1)
            tile_row = lax.broadcasted_iota(jnp.int32, (SUBLANES, LANES), 0)
            ones8 = jnp.ones((SUBLANES, LANES), BF16)

            def kv_head(h):
                heads = list(range(h * GROUP, (h + 1) * GROUP))
                kext = _dup_half(kcat, h, lt64k)
                kext_bf, kext_t = kext.astype(BF16), kext.T.astype(BF16)
                vext = _dup_half(vcat, h, lt64k).astype(BF16)
                qst = _head_stack(qr, heads, lt64).astype(BF16)
                pn, psink = _attn_probs_t(lax.dot_general(kext_bf, qst, NT_DIMS, preferred_element_type=F32), heads, sink_ref, valid)
                do_bf = _head_stack(do_chunks, heads, lt64).astype(BF16)
                delta = sum(lax.dot_general(ones8, part, NT_DIMS, preferred_element_type=F32)
                            for part in _split3(_head_stack(doo_chunks, heads, lt64)))[0:1, :]
                dpt = lax.dot_general(vext, do_bf, NT_DIMS, preferred_element_type=F32)
                dst = (pn * (dpt - delta)).astype(BF16)
                sink_term = psink * delta
                ds_acc = jnp.zeros((SUBLANES, LANES), F32)
                for j, n in enumerate(heads):
                    val = -jnp.sum(sink_term[:, j * BLK:(j + 1) * BLK])
                    ds_acc = ds_acc + jnp.where((tile_lane == n) & (tile_row == 0), val, 0.0)
                ds_parts[h] = ds_acc
                dqt = jnp.dot(kext_t, dst, preferred_element_type=F32) * (HEAD_DIM ** -0.5)
                dk_ext = jnp.dot(dst, qst, preferred_element_type=F32)
                dv_ext = jnp.dot(pn.astype(BF16), do_bf, preferred_element_type=F32)
                dk_h[h] = dk_ext + pltpu.roll(dk_ext, HEAD_DIM, 1)
                dv_h[h] = dv_ext + pltpu.roll(dv_ext, HEAD_DIM, 1)
                for c2 in range(GROUP // 2):
                    dq_chunks[4 * h + c2] = _rope_bwd(_pair_chunk(dqt, c2), cc, sc, lm)

            for h in range(N_KV):
                kv_head(h)
            ds_ref[...] += ds_parts[0] + ds_parts[1]
            dk_full = jnp.where(lt64k, dk_h[0], dk_h[1])
            dv_full = jnp.where(lt64k, dv_h[0], dv_h[1])

            @pl.when(i >= 1)
            def _emit():
                dp_ref[:, _QCOL:_QCOL + ATTN_WIDTH] = dq_s[...]
                dp_ref[:, _ZCOL:_ZCOL + ATTN_WIDTH] = dz_s[...]
                dp_ref[:, _KCOL:_KCOL + KV_WIDTH] = _rope_bwd(dk_s[...] + dk_full[:BLK], cp_ref[...], sp_ref[...], lm)
                dp_ref[:, _VCOL:_VCOL + KV_WIDTH] = dv_s[...] + dv_full[:BLK]

            for c in range(8):
                dq_s[:, c * LANES:(c + 1) * LANES] = dq_chunks[c]
                dz_s[:, c * LANES:(c + 1) * LANES] = dz_chunks[c]
            dk_s[...] = dk_full[BLK:]
            dv_s[...] = dv_full[BLK:]

        @pl.when(i == nb)
        def _final():
            dp_ref[:, _QCOL:_QCOL + ATTN_WIDTH] = dq_s[...]
            dp_ref[:, _ZCOL:_ZCOL + ATTN_WIDTH] = dz_s[...]
            dp_ref[:, _KCOL:_KCOL + KV_WIDTH] = _rope_bwd(dk_s[...], cc_ref[...], sc_ref[...], lm)
            dp_ref[:, _VCOL:_VCOL + KV_WIDTH] = dv_s[...]

    cur = lambda b, i: b * nb + jnp.minimum(i, nb - 1)
    prev = lambda b, i: b * nb + jnp.maximum(jnp.minimum(i, nb - 1) - 1, 0)
    emit = lambda b, i: b * nb + jnp.maximum(i - 1, 0)
    hbm = pl.BlockSpec(memory_space=pl.ANY)
    wide = lambda cb: pl.BlockSpec((BLK, ATTN_WIDTH), lambda b, i: (cur(b, i), cb))
    kv = lambda rowf, cb: pl.BlockSpec((BLK, LANES), lambda b, i: (rowf(b, i), cb))
    in_specs = [pl.BlockSpec(memory_space=pltpu.SMEM), wide(0), wide(1),
                kv(cur, _KCOL // LANES), kv(prev, _KCOL // LANES), kv(cur, _VCOL // LANES), kv(prev, _VCOL // LANES),
                kv(cur, 0), kv(cur, 0), kv(prev, 0), kv(prev, 0), wide(0), wide(0)] + [hbm] * nparts
    out_specs = [pl.BlockSpec((BLK, ATTN_IN), lambda b, i: (emit(b, i), 0)),
                 pl.BlockSpec((SUBLANES, LANES), lambda b, i: (0, 0))] + [hbm] * nparts
    return pl.pallas_call(
        body, name="attn_bwd", grid=(b_loc, nb + 1), in_specs=in_specs, out_specs=out_specs,
        out_shape=[jax.ShapeDtypeStruct((n, ATTN_IN), F32), jax.ShapeDtypeStruct((SUBLANES, LANES), F32)] + _scatter_lands(parts),
        scratch_shapes=[pltpu.VMEM((BLK, ATTN_WIDTH), F32), pltpu.VMEM((BLK, ATTN_WIDTH), F32),
                        pltpu.VMEM((BLK, KV_WIDTH), F32), pltpu.VMEM((BLK, KV_WIDTH), F32)] + _scatter_sems(nparts),
        compiler_params=_params(("arbitrary", "arbitrary"), 48),
    )(sinks, p, p, p, p, p, p, ct, st, ct, st, o, dg, *parts)


_CUM_ROWS = 256
HALF = CH // 2
_ROW0 = [SUBLANES * (s // SUBLANES) for s in range(CH)]
_ROW1 = [HALF * (s // HALF + 1) for s in range(CH)]
_ROWS_OF = [_ROW1[s] - _ROW0[s] for s in range(CH)]
_OFF_OF = [sum(_ROWS_OF[:s]) for s in range(CH)]


def _tri(lower):
    r = lax.broadcasted_iota(jnp.int32, (_CUM_ROWS, _CUM_ROWS), 0)
    c = lax.broadcasted_iota(jnp.int32, (_CUM_ROWS, _CUM_ROWS), 1)
    same = (r ^ c) < CH
    return (same & ((c <= r) if lower else (c >= r))).astype(BF16)


def _gates(qp, fp, lb):
    e = jnp.exp(-jnp.abs(fp))
    r = 1.0 / (1.0 + e)
    sig_neg = jnp.where(fp >= 0, e, 1.0) * r
    sig = jnp.where(fp >= 0, 1.0, e) * r
    g = jnp.log(jnp.maximum(lb + (1.0 - lb) * sig, F32_TINY))
    return qp * _sigmoid(qp), g, (1.0 - lb) * sig_neg, sig_neg


def _pair_rows(bc, s):
    return jnp.exp(jnp.minimum(bc[_ROW0[s]:_ROW1[s], :] - bc[s:s + 1, :], 0.0))


def _cross_half(q, k, bc):
    r = bc[HALF - 1:HALF, :]
    e1, e0 = jnp.exp(bc[HALF:, :] - r), jnp.exp(r - bc[:HALF, :])
    return q[HALF:, :] * e1, e1, k[:HALF, :] * e0, e0


HP = 8
REC_TB = 256
_HW = HP * REC_K


def _hgrn_specs(tb, nt, reverse):
    tmap = (lambda t: nt - 1 - t) if reverse else (lambda t: t)
    groups = REC_HEADS // HP
    blk = lambda cb: pl.BlockSpec((tb, _HW), lambda h, b, t: (b * nt + tmap(t), cb * groups + h))
    head = pl.BlockSpec((tb, _HW), lambda h, b, t: (b * nt + tmap(t), h))
    lbs = pl.BlockSpec((1, _HW), lambda h, b, t: (0, h))
    gws = pl.BlockSpec((1, REC_K), lambda h, b, t: (0, 0))
    hist = pl.BlockSpec((HP, 1, tb // CH, REC_K, REC_K), lambda h, b, t: (h, b, tmap(t), 0, 0))
    return blk, head, lbs, gws, hist


def _chunk_rows(c, first=0, size=CH):
    start = c * CH + first
    return pl.ds(start if isinstance(start, int) else pl.multiple_of(start, CH if first % CH == 0 else SUBLANES), size)


def _in_stages(heads):
    live = list(heads)
    while live:
        live = [g for g in live if next(g, live) is not live]


def _cumsum_chunks(tri, x, out_ref, tb):
    for r in range(0, tb, _CUM_ROWS):
        out_ref[r:r + _CUM_ROWS, :] = _dot3(tri, x[r:r + _CUM_ROWS, :])


def _hgrn_fwd(p, lb, gw, b_loc, t_len):
    n = p.shape[0]
    tb = min(REC_TB, t_len)
    nt, nck = t_len // tb, tb // CH

    def body(qp_ref, fp_ref, i_ref, z_ref, lb_ref, gw_ref, oraw_ref, g_ref, sh_ref, q_s, k_s, b_s, o_s, st_ref,
             car_o, car_a, car_s, car_st):
        @pl.when(pl.program_id(2) == 0)
        def _():
            st_ref[...] = jnp.zeros_like(st_ref)

        qv, g, kk, _ = _gates(qp_ref[...], fp_ref[...], lb_ref[...])
        q_s[...] = qv
        k_s[...] = kk
        _cumsum_chunks(_tri(True), g, b_s, tb)
        ones = jnp.ones((REC_K, REC_K), BF16)
        sub = lax.broadcasted_iota(jnp.int32, (SUBLANES, REC_K), 0)

        rows_of = _chunk_rows

        def issue(c, hp):
            rs, cs = rows_of(c), slice(hp * REC_K, (hp + 1) * REC_K)
            q, k, bc, v = q_s[rs, cs], k_s[rs, cs], b_s[rs, cs], i_ref[rs, cs]
            st = st_ref[hp]
            sh_ref[hp, 0, c] = st
            o = lax.dot_general((q * jnp.exp(bc)).astype(BF16), st.astype(BF16), NT_DIMS, preferred_element_type=F32)
            w = jnp.concatenate([q[_ROW0[s]:_ROW1[s], :] * _pair_rows(bc, s) * k[s:s + 1, :] for s in range(CH)], axis=0)
            a = jnp.dot(w.astype(BF16), ones, preferred_element_type=F32)
            qe1, _, ke0, _ = _cross_half(q, k, bc)
            s10 = lax.dot_general(qe1.astype(BF16), ke0.astype(BF16), NT_DIMS, preferred_element_type=F32)
            kd = k * jnp.exp(bc[CH - 1:CH, :] - bc)
            st_new = lax.dot_general(v.astype(BF16), kd.astype(BF16), TN_DIMS, preferred_element_type=F32)
            return o, a, s10, st_new

        def advance_state(c, hp, st_new):
            bl = b_s[_chunk_rows(c, CH - SUBLANES, SUBLANES), hp * REC_K:(hp + 1) * REC_K][SUBLANES - 1:, :]
            st_ref[hp] = st_ref[hp] * jnp.exp(bl) + st_new

        def cross(c, hp, s10):
            v0 = i_ref[_chunk_rows(c, 0, HALF), hp * REC_K:(hp + 1) * REC_K]
            return jnp.dot(s10.astype(BF16), v0.astype(BF16), preferred_element_type=F32)

        def finish(c, hp, o, a, o_cross):
            rs, cs = rows_of(c), slice(hp * REC_K, (hp + 1) * REC_K)
            v = i_ref[rs, cs]
            acc = [jnp.zeros((SUBLANES, REC_K), F32) for _ in range(CH // SUBLANES)]
            for s in range(CH):
                j = s // SUBLANES
                vs = v[s:s + 1, :]
                for jj in range(j, _ROW1[s] // SUBLANES):
                    blk = a[_OFF_OF[s] + (jj - j) * SUBLANES:_OFF_OF[s] + (jj - j + 1) * SUBLANES, :]
                    if jj == j:
                        blk = jnp.where(sub >= s - j * SUBLANES, blk, 0.0)
                    acc[jj] = acc[jj] + blk * vs
            o_s[rs, cs] = o + jnp.concatenate(acc, axis=0) + jnp.concatenate([jnp.zeros((HALF, REC_K), F32), o_cross], axis=0)

        def park(slot, results):
            for hp, (o, a, s10, st_new) in enumerate(results):
                car_o[slot, hp], car_a[slot, hp], car_s[slot, hp], car_st[slot, hp] = o, a, s10, st_new

        def retire(c, slot):
            for hp in range(HP):
                advance_state(c, hp, car_st[slot, hp])
            yield
            crosses = [cross(c, hp, car_s[slot, hp]) for hp in range(HP)]
            for hp in range(HP):
                finish(c, hp, car_o[slot, hp], car_a[slot, hp], crosses[hp])

        def step(c, slot):
            closing = retire(c - 1, slot)
            next(closing)
            park(1 - slot, [issue(c, hp) for hp in range(HP)])
            next(closing, None)

        def trip(j, carry):
            step(2 * j + 1, 0)
            step(2 * j + 2, 1)
            return carry

        assert nck % 2 == 0
        park(0, [issue(0, hp) for hp in range(HP)])
        lax.fori_loop(0, nck // 2 - 1, trip, 0)
        step(nck - 1, 0)
        for _ in retire(nck - 1, 1):
            pass
        oraw_ref[...] = o_s[...]
        for hp in range(HP):
            cs = slice(hp * REC_K, (hp + 1) * REC_K)
            o, zc = o_s[:, cs], z_ref[:, cs]
            on = (o * lax.rsqrt(jnp.mean(o * o, axis=-1, keepdims=True) + NORM_EPS)) * gw_ref[...]
            g_ref[:, cs] = (on * (zc * _sigmoid(zc))).astype(BF16)

    blk, head, lbs, gws, hist = _hgrn_specs(tb, nt, False)
    return pl.pallas_call(
        body, name="hgrn_fwd", grid=(REC_HEADS // HP, b_loc, nt),
        in_specs=[blk(0), blk(1), blk(2), blk(3), lbs, gws], out_specs=[head, head, hist],
        out_shape=[jax.ShapeDtypeStruct((n, REC_WIDTH), F32), jax.ShapeDtypeStruct((n, REC_WIDTH), BF16),
                   jax.ShapeDtypeStruct((REC_HEADS, b_loc, t_len // CH, REC_K, REC_K), F32)],
        scratch_shapes=[pltpu.VMEM((tb, _HW), F32)] * 4 + [pltpu.VMEM((HP, REC_K, REC_K), F32)] + [
            pltpu.VMEM((2, HP, CH, REC_K), F32), pltpu.VMEM((2, HP, sum(_ROWS_OF), REC_K), F32),
            pltpu.VMEM((2, HP, HALF, HALF), F32), pltpu.VMEM((2, HP, REC_K, REC_K), F32)],
        compiler_params=_params(("arbitrary", "arbitrary", "arbitrary"), 48),
    )(p, p, p, p, lb, gw)


def _hgrn_bwd(p, lb, gw, oraw, sh, dg, b_loc, t_len):
    n = p.shape[0]
    tb = min(REC_TB, t_len)
    nt, nck = t_len // tb, tb // CH
    assert HP == REC_HEADS

    def body(qp_ref, fp_ref, i_ref, z_ref, lb_ref, gw_ref, oraw_ref, dg_ref, sh_ref,
             dp_ref, dlb_ref, dgw_ref,
             q_s, k_s, b_s, do_s, dqv_s, dk_s, db_s, dst_ref, car_r, car_dst, car_dec, car_a, car_da, car_x):
        dq_ref, df_ref, di_ref, dz_ref = (dp_ref.at[:, part * REC_WIDTH:(part + 1) * REC_WIDTH] for part in range(4))
        b, t = pl.program_id(1), pl.program_id(2)

        @pl.when(t == 0)
        def _():
            dst_ref[...] = jnp.zeros_like(dst_ref)

        @pl.when((b == 0) & (t == 0))
        def _():
            dlb_ref[...] = jnp.zeros_like(dlb_ref)
            dgw_ref[...] = jnp.zeros_like(dgw_ref)

        lbv, qp, fp = lb_ref[...], qp_ref[...], fp_ref[...]
        qv, g, kk, sig_neg = _gates(qp, fp, lbv)
        q_s[...] = qv
        k_s[...] = kk
        _cumsum_chunks(_tri(True), g, b_s, tb)
        gwv = gw_ref[...]
        for hp in range(HP):
            cs = slice(hp * REC_K, (hp + 1) * REC_K)
            o, zc, dgv = oraw_ref[:, cs], z_ref[:, cs], dg_ref[:, cs]
            rn = lax.rsqrt(jnp.mean(o * o, axis=-1, keepdims=True) + NORM_EPS)
            on = o * rn
            sgz = _sigmoid(zc)
            dz_ref[:, cs] = (dgv * (on * gwv) * (sgz * (1.0 + zc * (1.0 - sgz)))).astype(BF16)
            dpre = dgv * (zc * sgz)
            dgw_ref[hp] += jnp.sum(dpre * on, axis=0, keepdims=True)
            don = dpre * gwv
            do_s[:, cs] = rn * (don - on * jnp.mean(don * on, axis=-1, keepdims=True))

        ones = jnp.ones((REC_K, REC_K), BF16)
        sub = lax.broadcasted_iota(jnp.int32, (SUBLANES, REC_K), 0)
        rowid = lax.broadcasted_iota(jnp.int32, (CH, REC_K), 0)
        ngrp = CH // SUBLANES
        piece_row = lax.broadcasted_iota(jnp.int32, (1, sum(_ROWS_OF)), 1)
        key_of = jnp.zeros((1, sum(_ROWS_OF)), jnp.int32)
        for s in range(1, CH):
            key_of = jnp.where(piece_row >= _OFF_OF[s], s, key_of)
        pick = (key_of == lax.broadcasted_iota(jnp.int32, (CH, sum(_ROWS_OF)), 0)).astype(BF16)

        def operands(c, hp):
            rs, cs = _chunk_rows(c), slice(hp * REC_K, (hp + 1) * REC_K)
            return rs, cs, q_s[rs, cs], k_s[rs, cs], b_s[rs, cs], i_ref[rs, cs], do_s[rs, cs]

        def issue(c, hp, slot):
            _, _, q, k, bc, v, do = operands(c, hp)
            st, dst = sh_ref[hp, 0, c], dst_ref[hp]
            qe, kd = q * jnp.exp(bc), k * jnp.exp(bc[CH - 1:CH, :] - bc)
            do_bf, dst_bf = do.astype(BF16), dst.astype(BF16)
            car_r[slot, hp, 0:CH] = jnp.dot(do_bf, st.astype(BF16), preferred_element_type=F32)
            car_r[slot, hp, CH:2 * CH] = jnp.dot(v.astype(BF16), dst_bf, preferred_element_type=F32)
            car_r[slot, hp, 2 * CH:3 * CH] = lax.dot_general(kd.astype(BF16), dst_bf, NT_DIMS, preferred_element_type=F32)
            car_dst[slot, hp] = lax.dot_general(do_bf, qe.astype(BF16), TN_DIMS, preferred_element_type=F32)
            dec = jnp.concatenate([_pair_rows(bc, s) for s in range(CH)], axis=0)
            qk = jnp.concatenate([q[_ROW0[s]:_ROW1[s], :] * k[s:s + 1, :] for s in range(CH)], axis=0)
            x = jnp.concatenate([do[_ROW0[s]:_ROW1[s], :] * v[s:s + 1, :] for s in range(CH)], axis=0)
            car_dec[slot, hp] = dec
            car_a[slot, hp] = jnp.dot((qk * dec).astype(BF16), ones, preferred_element_type=F32)
            car_da[slot, hp] = jnp.dot(x.astype(BF16), ones, preferred_element_type=F32)
            qe1, _, ke0, _ = _cross_half(q, k, bc)
            qe1_bf, ke0_bf = qe1.astype(BF16), ke0.astype(BF16)
            do1_bf, v0_bf = do[HALF:, :].astype(BF16), v[:HALF, :].astype(BF16)
            car_x[slot, hp, 0:HALF] = lax.dot_general(ke0_bf, qe1_bf, NT_DIMS, preferred_element_type=F32)
            car_x[slot, hp, HALF:2 * HALF] = lax.dot_general(do1_bf, v0_bf, NT_DIMS, preferred_element_type=F32)
            car_x[slot, hp, 2 * HALF:3 * HALF] = lax.dot_general(v0_bf, do1_bf, NT_DIMS, preferred_element_type=F32)

        def advance_state(c, hp, slot):
            ebl = jnp.exp(b_s[_chunk_rows(c, CH - SUBLANES, SUBLANES), hp * REC_K:(hp + 1) * REC_K][SUBLANES - 1:, :])
            st, dst = sh_ref[hp, 0, c], dst_ref[hp]
            dst_ref[hp] = dst * ebl + car_dst[slot, hp]
            return ebl * jnp.sum(st * dst, axis=0, keepdims=True)

        def cross(c, hp, slot):
            _, _, q, k, bc, v, do = operands(c, hp)
            qe1, _, ke0, _ = _cross_half(q, k, bc)
            xs = car_x[slot, hp]
            dqe1 = jnp.dot(xs[HALF:2 * HALF].astype(BF16), ke0.astype(BF16), preferred_element_type=F32)
            dke0 = jnp.dot(xs[2 * HALF:].astype(BF16), qe1.astype(BF16), preferred_element_type=F32)
            dv1 = jnp.dot(xs[:HALF].astype(BF16), do[HALF:, :].astype(BF16), preferred_element_type=F32)
            return dqe1, dke0, dv1

        def retire(c, slot):
            dbl_state = [advance_state(c, hp, slot) for hp in range(HP)]
            yield
            crossed = [cross(c, hp, slot) for hp in range(HP)]
            for hp in range(HP):
                finish(c, hp, slot, dbl_state[hp], *crossed[hp])

        def step(c, slot):
            closing = retire(c + 1, slot)
            next(closing)
            for hp in range(HP):
                issue(c, hp, 1 - slot)
            next(closing, None)

        def trip(j, carry):
            step(nck - 2 - 2 * j, 0)
            step(nck - 3 - 2 * j, 1)
            return carry

        def finish(c, hp, slot, dbl_state, dqe1, dke0, dv1):
            rs, cs, q, k, bc, v, do = operands(c, hp)
            eb, ekd = jnp.exp(bc), jnp.exp(bc[CH - 1:CH, :] - bc)
            qe, kd = q * eb, k * ekd
            qe1, e1, ke0, e0 = _cross_half(q, k, bc)
            dqe, dkd, dv = car_r[slot, hp, 0:CH], car_r[slot, hp, CH:2 * CH], car_r[slot, hp, 2 * CH:3 * CH]
            a, da, decs = car_a[slot, hp], car_da[slot, hp], car_dec[slot, hp]
            dec = [decs[_OFF_OF[s]:_OFF_OF[s] + _ROWS_OF[s], :] for s in range(CH)]
            dbl = jnp.sum(dkd * kd, axis=0, keepdims=True) + dbl_state
            dq_acc = [jnp.zeros((SUBLANES, REC_K), F32) for _ in range(ngrp)]
            uk, uv = [], []
            for s in range(CH):
                j = s // SUBLANES
                r0 = j * SUBLANES
                ks = k[s:s + 1, :]
                for jj in range(j, _ROW1[s] // SUBLANES):
                    lo, hi = _OFF_OF[s] + (jj - j) * SUBLANES, _OFF_OF[s] + (jj - j + 1) * SUBLANES
                    a_blk, da_blk = a[lo:hi, :], da[lo:hi, :]
                    if jj == j:
                        keep = sub >= s - r0
                        a_blk, da_blk = jnp.where(keep, a_blk, 0.0), jnp.where(keep, da_blk, 0.0)
                    rows = slice(jj * SUBLANES, (jj + 1) * SUBLANES)
                    tt = da_blk * dec[s][(jj - j) * SUBLANES:(jj - j + 1) * SUBLANES, :]
                    dq_acc[jj] = dq_acc[jj] + tt * ks
                    uk.append(tt * q[rows, :])
                    uv.append(a_blk * do[rows, :])
            dk_in = jnp.dot(pick, jnp.concatenate(uk, axis=0).astype(BF16), preferred_element_type=F32)
            dv_in = jnp.dot(pick, jnp.concatenate(uv, axis=0).astype(BF16), preferred_element_type=F32)
            zero_half = jnp.zeros((HALF, REC_K), F32)
            dq_x = jnp.concatenate([zero_half, dqe1 * e1], axis=0)
            dk_x = jnp.concatenate([dke0 * e0, zero_half], axis=0)
            dv_x = jnp.concatenate([dv1, zero_half], axis=0)
            db_x = jnp.concatenate([-(dke0 * ke0), dqe1 * qe1], axis=0)
            dq_in = jnp.concatenate(dq_acc, axis=0)
            dqv_s[rs, cs] = dqe * eb + dq_in + dq_x
            dk_s[rs, cs] = dkd * ekd + dk_in + dk_x
            di_ref[rs, cs] = (dv + dv_in + dv_x).astype(BF16)
            db = dqe * qe - dkd * kd + q * dq_in - k * dk_in + db_x
            db_s[rs, cs] = db + jnp.where(rowid == CH - 1, dbl, 0.0)

        assert nck % 2 == 0
        for hp in range(HP):
            issue(nck - 1, hp, 0)
        lax.fori_loop(0, nck // 2 - 1, trip, 0)
        step(0, 0)
        for _ in retire(0, 1):
            pass
        up = _tri(False)
        sgq = _sigmoid(qp)
        dq_ref[...] = (dqv_s[...] * (sgq * (1.0 + qp * (1.0 - sgq)))).astype(BF16)
        dlb_acc = jnp.zeros((1, _HW), F32)
        for r in range(0, tb, _CUM_ROWS):
            rows = slice(r, r + _CUM_ROWS)
            dgl = _dot3(up, db_s[rows, :])
            dfg = dgl * jnp.exp(-g[rows, :]) - dk_s[rows, :]
            sn = sig_neg[rows, :]
            df_ref[rows, :] = (dfg * (1.0 - lbv) * (1.0 - sn) * sn).astype(BF16)
            dlb_acc = dlb_acc + jnp.sum(dfg * sn, axis=0, keepdims=True)
        dlb_ref[...] += dlb_acc

    blk, head, lbs, gws, hist = _hgrn_specs(tb, nt, True)
    out_specs = [pl.BlockSpec((tb, REC_IN), lambda h, b, t: (b * nt + nt - 1 - t, 0)), lbs,
                 pl.BlockSpec((HP, 1, REC_K), lambda h, b, t: (h, 0, 0))]
    out_shape = [jax.ShapeDtypeStruct((n, REC_IN), BF16),
                 jax.ShapeDtypeStruct((1, REC_WIDTH), F32), jax.ShapeDtypeStruct((REC_HEADS, 1, REC_K), F32)]
    return pl.pallas_call(
        body, name="hgrn_bwd", grid=(REC_HEADS // HP, b_loc, nt),
        in_specs=[blk(0), blk(1), blk(2), blk(3), lbs, gws, head, head, hist],
        out_specs=out_specs, out_shape=out_shape,
        scratch_shapes=[pltpu.VMEM((tb, _HW), F32)] * 7 + [pltpu.VMEM((HP, REC_K, REC_K), F32)] + [
            pltpu.VMEM((2, HP, 3 * CH, REC_K), F32), pltpu.VMEM((2, HP, REC_K, REC_K), F32)] + [
            pltpu.VMEM((2, HP, sum(_ROWS_OF), REC_K), F32)] * 3 + [pltpu.VMEM((2, HP, 3 * HALF, HALF), F32)],
        compiler_params=_params(("arbitrary", "arbitrary", "arbitrary"), 56),
    )(p, p, p, p, lb, gw, oraw, dg, sh)


def _postnorm_bwd_nt(dxo, y, qw, w, name):
    n = dxo.shape[0]

    def body(dx_ref, y_ref, qw_ref, w_ref, dg_ref, dy_ref, dqw_ref, db_ref):
        @pl.when(pl.program_id(0) == 0)
        def _():
            dqw_ref[...] = jnp.zeros_like(dqw_ref)
            db_ref[...] = jnp.zeros_like(db_ref)

        yv, dxv = y_ref[...], dx_ref[...]
        r = lax.rsqrt(jnp.mean(yv * yv, axis=-1, keepdims=True) + NORM_EPS)
        u = yv * r
        du = dxv * qw_ref[...]
        dy = r * (du - u * jnp.mean(du * u, axis=-1, keepdims=True))
        dqw_ref[...] += jnp.sum(dxv * u, axis=0, keepdims=True)
        db_ref[...] += jnp.sum(dy, axis=0, keepdims=True)
        dyb = dy.astype(BF16)
        dy_ref[...] = dyb
        dg_ref[...] = lax.dot_general(dyb, w_ref[...], NT_DIMS, preferred_element_type=F32)

    rows = pl.BlockSpec((TM, D_MODEL), lambda i: (i, 0))
    const = lambda shape: pl.BlockSpec(shape, lambda i: (0, 0))
    return pl.pallas_call(
        body, name=name, grid=(n // TM,), in_specs=[rows, rows, const((1, D_MODEL)), const((D_MODEL, D_MODEL))],
        out_specs=[rows, rows, const((1, D_MODEL)), const((1, D_MODEL))],
        out_shape=[jax.ShapeDtypeStruct((n, D_MODEL), F32), jax.ShapeDtypeStruct((n, D_MODEL), BF16),
                   jax.ShapeDtypeStruct((1, D_MODEL), F32), jax.ShapeDtypeStruct((1, D_MODEL), F32)],
        compiler_params=_params(("arbitrary",), 48),
    )(dxo, y, qw, w)


def _nt_prenorm_bwd(dps, w, x, pw, dxo, has_bias, name, parts=()):
    n = x.shape[0]
    widths = [d.shape[1] for d in dps]
    m = sum(widths)
    npieces, nparts, steps = len(dps), len(parts), n // TM

    def body(*refs):
        dp_refs = refs[:npieces]
        w_ref, x_ref, pw_ref, dxo_ref = refs[npieces:npieces + 4]
        part_refs = refs[npieces + 4:npieces + 4 + nparts]
        dx_ref, dpw_ref, db_ref = refs[npieces + 4 + nparts:npieces + 7 + nparts]
        land_refs = refs[npieces + 7 + nparts:npieces + 7 + 2 * nparts]
        sems = refs[npieces + 7 + 2 * nparts:]

        @pl.when(pl.program_id(0) == 0)
        def _():
            dpw_ref[...] = jnp.zeros_like(dpw_ref)
            db_ref[...] = jnp.zeros_like(db_ref)
            if nparts:
                _scatter_start(part_refs, land_refs, sems)

        dh = jnp.zeros((TM, D_MODEL), F32)
        off = 0
        for dp_ref, wd in zip(dp_refs, widths):
            cn = _col_chunk(wd)
            for j in range(0, wd, cn):
                dpc = dp_ref[:, j:j + cn]
                if has_bias:
                    db_ref[:, off + j:off + j + cn] += jnp.sum(dpc, axis=0, keepdims=True)
                dh = dh + lax.dot_general(dpc.astype(BF16), w_ref[:, off + j:off + j + cn], NT_DIMS, preferred_element_type=F32)
            off += wd
        xv = x_ref[...]
        r = lax.rsqrt(jnp.mean(xv * xv, axis=-1, keepdims=True) + NORM_EPS)
        xn = xv * r
        dpw_ref[...] += jnp.sum(dh * xn, axis=0, keepdims=True)
        dxn = dh * pw_ref[...]
        dx_ref[...] = dxo_ref[...] + r * (dxn - xn * jnp.mean(dxn * xn, axis=-1, keepdims=True))

        if nparts:
            @pl.when(pl.program_id(0) == steps - 1)
            def _():
                _scatter_wait(part_refs, land_refs, sems)

    rows = pl.BlockSpec((TM, D_MODEL), lambda i: (i, 0))
    const = lambda shape: pl.BlockSpec(shape, lambda i: (0, 0))
    hbm = pl.BlockSpec(memory_space=pl.ANY)
    in_specs = ([pl.BlockSpec((TM, wd), lambda i: (i, 0)) for wd in widths] + [const((D_MODEL, m)), rows, const((1, D_MODEL)), rows]
                + [hbm] * nparts)
    return pl.pallas_call(
        body, name=name, grid=(steps,), in_specs=in_specs,
        out_specs=[rows, const((1, D_MODEL)), const((1, m))] + [hbm] * nparts,
        out_shape=[jax.ShapeDtypeStruct((n, D_MODEL), F32), jax.ShapeDtypeStruct((1, D_MODEL), F32),
                   jax.ShapeDtypeStruct((1, m), F32)] + _scatter_lands(parts),
        scratch_shapes=_scatter_sems(nparts) if nparts else [],
        compiler_params=_params(("arbitrary",), 56),
    )(*dps, w, x, pw, dxo, *parts)


def _matmul_tn(a, b, name):
    n, k = a.shape
    m = b.shape[1]
    tk, tm, tn = k, _col_chunk(m), TN_ROWS if n % TN_ROWS == 0 else n

    def body(a_ref, b_ref, o_ref):
        @pl.when(pl.program_id(2) == 0)
        def _():
            o_ref[...] = jnp.zeros_like(o_ref)

        o_ref[...] += lax.dot_general(a_ref[...], b_ref[...].astype(BF16), TN_DIMS, preferred_element_type=F32)

    return pl.pallas_call(
        body, name=name, grid=(k // tk, m // tm, n // tn),
        in_specs=[pl.BlockSpec((tn, tk), lambda i, j, l: (l, i)), pl.BlockSpec((tn, tm), lambda i, j, l: (l, j))],
        out_specs=pl.BlockSpec((tk, tm), lambda i, j, l: (i, j)),
        out_shape=jax.ShapeDtypeStruct((k, m), F32),
        compiler_params=_params(("arbitrary", "arbitrary", "arbitrary"), 48),
    )(a, b)


def _matmul_tn_by_owner(a, b, name):
    n, k = a.shape
    c = b.shape[1] // N_DEV
    tn = TN_ROWS if n % TN_ROWS == 0 else n
    steps = n // tn
    per = 2

    def body(a_ref, b_ref, o_ref, w_ref):
        @pl.when(pl.program_id(1) == 0)
        def _():
            o_ref[...] = jnp.zeros_like(o_ref)

        r = lax.dot_general(a_ref[...], b_ref[...], TN_DIMS, preferred_element_type=F32)
        for j in range(per):
            o_ref[j] += r[:, j * c:(j + 1) * c]

        @pl.when(pl.program_id(1) == steps - 1)
        def _():
            w_ref[...] = o_ref[...].astype(BF16)

    out = pl.BlockSpec((per, k, c), lambda j, l: (j, 0, 0))
    return pl.pallas_call(
        body, name=name, grid=(N_DEV // per, steps),
        in_specs=[pl.BlockSpec((tn, k), lambda j, l: (l, 0)), pl.BlockSpec((tn, per * c), lambda j, l: (l, j))],
        out_specs=[out, out],
        out_shape=[jax.ShapeDtypeStruct((N_DEV, k, c), F32), jax.ShapeDtypeStruct((N_DEV, k, c), BF16)],
        compiler_params=_params(("arbitrary", "arbitrary"), 48),
    )(a, b)


def _by_owner_cols(dw):
    k, m = dw.shape
    return dw.reshape(k, N_DEV, m // N_DEV).transpose(1, 0, 2)


def _own_and_bf16(part):
    return lax.dynamic_index_in_dim(part, _my_id(), 0, keepdims=False), part.astype(BF16)


def _step(x, ct, st, tgt, pre_w, post_w, wa_in, ba_in, sinks, wa_out_shard, ba_out, wr_in_shard, lb_logits, gnorm_w, wr_out_shard, b_loc, t_len):
    nb = t_len // BLK
    lb = _lower_bound(lb_logits)
    p0, h0, ga_out = _norm_matmul(x, pre_w[0:1], wa_in, ba_in, "attn_in_proj", [wa_out_shard])
    wa_out = ga_out.reshape(ATTN_WIDTH, D_MODEL)
    o0, g0, gr_in, gr_out = _attn_fwd(p0, ct, st, sinks, b_loc, nb, [wr_in_shard, wr_out_shard])
    wr_in = gr_in.transpose(1, 0, 2).reshape(D_MODEL, REC_IN)
    wr_out = gr_out.reshape(REC_WIDTH, D_MODEL)
    y0, x1 = _outproj_postnorm(g0, wa_out, ba_out, x, post_w[0:1], "attn_out_proj")
    p1, h1 = _norm_matmul(x1, pre_w[1:2], wr_in, None, "rec_in_proj")
    o1, g1, sh = _hgrn_fwd(p1, lb, gnorm_w, b_loc, t_len)
    dx2, dg1, dy1, dpost1, loss_tile = _outproj_loss_bwd(g1, wr_out, x1, post_w[1:2], tgt, "rec_out_proj_loss_bwd")
    d_wr_out = _matmul_tn(g1, dy1, "rec_w_out_grad")
    dp1, dlb, dgw = _hgrn_bwd(p1, lb, gnorm_w, o1, sh, dg1, b_loc, t_len)
    dx1, dpre1, _ = _nt_prenorm_bwd([dp1], wr_in, x1, pre_w[1:2], dx2, False, "rec_in_bwd")
    part_r_in, wire_r_in = _matmul_tn_by_owner(h1, dp1, "rec_w_in_grad")
    own_r_in = lax.dynamic_index_in_dim(part_r_in, _my_id(), 0, keepdims=False)
    dg0, dy0, dpost0, dba_out = _postnorm_bwd_nt(dx1, y0, post_w[0:1], wa_out, "attn_out_bwd")
    d_wa_out = _matmul_tn(g0, dy0, "attn_w_out_grad")
    owns, wires = zip(*[_own_and_bf16(part) for part in (
        d_wr_out.reshape(N_DEV, REC_WIDTH // N_DEV, D_MODEL), d_wa_out.reshape(N_DEV, ATTN_WIDTH // N_DEV, D_MODEL))])
    owns, wires = (own_r_in,) + owns, (wire_r_in,) + wires
    dp0, dsink_tile, *lands = _attn_bwd(p0, ct, st, sinks, o0, dg0, b_loc, nb, list(wires))
    d_wa_in = _matmul_tn(h0, dp0, "attn_w_in_grad")
    own_a_in, wire_a_in = _own_and_bf16(_by_owner_cols(_qkvz(d_wa_in)))
    dx0, dpre0, dba_in, land_a_in = _nt_prenorm_bwd([dp0], wa_in, x, pre_w[0:1], dx1, True, "attn_in_bwd", [wire_a_in])
    small = dict(pre=jnp.concatenate([dpre0, dpre1], axis=0), post=jnp.concatenate([dpost0, dpost1], axis=0),
                 ba_in=dba_in, sinks=dsink_tile[0:1, 0:N_HEADS], ba_out=dba_out, lb=dlb, gnorm=jnp.sum(dgw, axis=0))
    return loss_tile, dx0, list(zip(lands, owns)) + [(land_a_in, own_a_in)], small


def _my_id():
    return lax.axis_index("x") * 4 + lax.axis_index("y") * 2 + lax.axis_index("c")


def _peer(k):
    x, y, c = lax.axis_index("x"), lax.axis_index("y"), lax.axis_index("c")
    return (x ^ ((k >> 2) & 1), y ^ ((k >> 1) & 1), c ^ (k & 1))


def _peer_id(k):
    return _my_id() ^ k


def _all_gather_by_chip(shard, pos_col):
    n = pos_col.shape[0]

    def body(x_ref, p_ref, f_ref, s_ref, out_ref, ct_ref, st_ref, send_sems, recv_sems, local_sem):
        x, y, c = lax.axis_index("x"), lax.axis_index("y"), lax.axis_index("c")
        me, sibling = (x, y, c), (x, y, 1 - c)
        chips = [(1 - x, y), (x, 1 - y), (1 - x, 1 - y)]

        def rows(px, py, pc):
            return out_ref.at[4 * px + 2 * py + pc]

        def copy(k, block, to, src=None):
            return pltpu.make_async_remote_copy(src_ref=rows(*block) if src is None else src, dst_ref=rows(*block),
                                                send_sem=send_sems.at[k], recv_sem=recv_sems.at[k], device_id=to, device_id_type=MESH)

        mine = pltpu.make_async_copy(x_ref, rows(*me), local_sem)
        mine.start()
        first = [copy(0, me, sibling, src=x_ref)] + [copy(1 + j, me, (*chip, c), src=x_ref) for j, chip in enumerate(chips)]
        for cp in first:
            cp.start()
        _rope_tables_into(p_ref, f_ref, s_ref, ct_ref, st_ref)
        passed = [copy(4 + j, (*chip, c), sibling) for j, chip in enumerate(chips)]
        for j, chip in enumerate(chips):
            copy(1 + j, (*chip, c), me).wait_recv()
            passed[j].start()
        copy(0, sibling, me).wait_recv()
        for j, chip in enumerate(chips):
            copy(4 + j, (*chip, 1 - c), me).wait_recv()
        for cp in first + passed:
            cp.wait_send()
        mine.wait()

    hbm, vmem = pl.BlockSpec(memory_space=pl.ANY), pl.BlockSpec(memory_space=pltpu.VMEM)
    return pl.pallas_call(
        body, name="comm_all_gather_by_chip", in_specs=[hbm, vmem, vmem, vmem], out_specs=[hbm, vmem, vmem],
        out_shape=[jax.ShapeDtypeStruct((N_DEV,) + shard.shape, shard.dtype)] + [jax.ShapeDtypeStruct((n, LANES), F32)] * 2,
        scratch_shapes=[pltpu.SemaphoreType.DMA((N_DEV - 1,)), pltpu.SemaphoreType.DMA((N_DEV - 1,)), pltpu.SemaphoreType.DMA],
        compiler_params=_params(None, 32),
    )(shard, pos_col, *_rope_lanes())


def _gather_shapes(shards):
    return [jax.ShapeDtypeStruct((N_DEV,) + s.shape, s.dtype) for s in shards]


def _gather_sems(nsh):
    return [pltpu.SemaphoreType.DMA((nsh, N_DEV - 1)), pltpu.SemaphoreType.DMA((nsh, N_DEV - 1)), pltpu.SemaphoreType.DMA((nsh,))]


def _gather_copies(ins, outs, sems, received):
    send_sems, recv_sems, local_sems = sems
    me = _my_id()
    local = [pltpu.make_async_copy(ins[a], outs[a].at[me], local_sems.at[a]) for a in range(len(ins))]
    remote = [pltpu.make_async_remote_copy(
        src_ref=ins[a], dst_ref=outs[a].at[_peer_id(k) if received else me], send_sem=send_sems.at[a, k - 1],
        recv_sem=recv_sems.at[a, k - 1], device_id=_peer(k), device_id_type=MESH)
        for a in range(len(ins)) for k in range(1, N_DEV)]
    return local, remote


def _gather_start(ins, outs, sems):
    local, sends = _gather_copies(ins, outs, sems, False)
    for cp in local + sends:
        cp.start()


def _gather_wait(ins, outs, sems):
    local, recvs = _gather_copies(ins, outs, sems, True)
    for cp in recvs:
        cp.wait_recv()
    for cp in recvs:
        cp.wait_send()
    for cp in local:
        cp.wait()


def _scatter_lands(parts):
    return [jax.ShapeDtypeStruct((N_DEV - 1,) + p.shape[1:], p.dtype) for p in parts]


def _scatter_sems(nparts):
    return [pltpu.SemaphoreType.DMA((nparts, N_DEV - 1)), pltpu.SemaphoreType.DMA((nparts, N_DEV - 1))]


def _scatter_copies(parts, lands, sems):
    send_sems, recv_sems = sems
    return [pltpu.make_async_remote_copy(
        src_ref=parts[a].at[_peer_id(k)], dst_ref=lands[a].at[k - 1], send_sem=send_sems.at[a, k - 1],
        recv_sem=recv_sems.at[a, k - 1], device_id=_peer(k), device_id_type=MESH)
        for a in range(len(parts)) for k in range(1, N_DEV)]


def _scatter_start(parts, lands, sems):
    for cp in _scatter_copies(parts, lands, sems):
        cp.start()


def _scatter_wait(parts, lands, sems):
    copies = _scatter_copies(parts, lands, sems)
    for cp in copies:
        cp.wait_recv()
    for cp in copies:
        cp.wait_send()


def _adamw(w, g, m, v):
    m2 = ADAM_B1 * m + (1.0 - ADAM_B1) * g
    v2 = ADAM_B2 * v + (1.0 - ADAM_B2) * (g * g)
    m_hat = m2 / (1.0 - ADAM_B1 ** ADAM_STEP)
    v_hat = v2 / (1.0 - ADAM_B2 ** ADAM_STEP)
    delta = -ADAM_LR * (m_hat / (jnp.sqrt(v_hat) + ADAM_EPS) + ADAM_WD * w)
    return delta, m2, v2


def _sum_adamw(land, own, w, m, v, name):
    r, c = own.shape
    rb = min(r, 256)

    def body(land_ref, own_ref, w_ref, m_ref, v_ref, g_ref, d_ref, m2_ref, v2_ref):
        me = _my_id()
        g = jnp.zeros((rb, c), F32)
        for dev in range(N_DEV):
            k = dev ^ me
            g = g + jnp.where(k == 0, own_ref[...], land_ref[jnp.maximum(k - 1, 0)].astype(F32))
        delta, m2, v2 = _adamw(w_ref[...], g, m_ref[...], v_ref[...])
        g_ref[...] = g
        d_ref[...] = delta
        m2_ref[...] = m2
        v2_ref[...] = v2

    rows = pl.BlockSpec((rb, c), lambda i: (i, 0))
    return pl.pallas_call(
        body, name=name, grid=(r // rb,), in_specs=[pl.BlockSpec((N_DEV - 1, rb, c), lambda i: (0, i, 0))] + [rows] * 4,
        out_specs=[rows] * 4, out_shape=[jax.ShapeDtypeStruct((r, c), F32)] * 4,
        compiler_params=_params(("arbitrary",), 32),
    )(land, own, w, m, v)


_SMALL = [("pre_norm_w", 2048), ("post_norm_w", 2048), ("attn_b_in", 2304), ("attn_sinks", 16), ("attn_b_out", 1024),
          ("rec_lb_logits", 2048), ("rec_gnorm_w", 128), ("loss", 1)]
_TILE = SUBLANES * LANES


def _small_rows(size):
    return -(-size // _TILE) * SUBLANES


_SMALL_OFF = {}
_r = 0
for _name, _size in _SMALL:
    _SMALL_OFF[_name] = _r
    _r += _small_rows(_size)
_SMALL_ROWS = _r


def _pack_small(pieces):
    out = []
    for name, size in _SMALL:
        flat = pieces[name].reshape(-1).astype(F32)
        out.append(jnp.pad(flat, (0, _small_rows(size) * LANES - size)).reshape(-1, LANES))
    return jnp.concatenate(out, axis=0)


def _unpack_small(packed, shapes):
    return {name: packed[_SMALL_OFF[name]:_SMALL_OFF[name] + _small_rows(size)].reshape(-1)[:size].reshape(shapes[name])
            for name, size in _SMALL}


def _small_allreduce_adamw(gpart, w, m, v):
    lb0 = _SMALL_OFF["rec_lb_logits"]

    def body(gp_ref, w_ref, m_ref, v_ref, g_ref, d_ref, m2_ref, v2_ref, land_ref, send_sems, recv_sems):
        me = _my_id()
        sent = []
        for k in range(1, N_DEV):
            cp = pltpu.make_async_remote_copy(src_ref=gp_ref, dst_ref=land_ref.at[k - 1], send_sem=send_sems.at[k - 1],
                                              recv_sem=recv_sems.at[k - 1], device_id=_peer(k), device_id_type=MESH)
            cp.start()
            sent.append(cp)
        for cp in sent:
            cp.wait_recv()
        for cp in sent:
            cp.wait_send()
        g = jnp.zeros((_SMALL_ROWS, LANES), F32)
        for dev in range(N_DEV):
            k = dev ^ me
            g = g + jnp.where(k == 0, gp_ref[...], land_ref[jnp.maximum(k - 1, 0)])
        g_ref[...] = g
        l0, l1 = w_ref[lb0:lb0 + SUBLANES, :], w_ref[lb0 + SUBLANES:lb0 + 2 * SUBLANES, :]
        mx = jnp.maximum(l0, l1)
        e0, e1 = jnp.exp(l0 - mx), jnp.exp(l1 - mx)
        p1 = e1 / (e0 + e1)
        dl1 = (1.0 - p1) * p1 * g[lb0:lb0 + SUBLANES, :]
        g_ref[lb0:lb0 + SUBLANES, :] = -dl1
        g_ref[lb0 + SUBLANES:lb0 + 2 * SUBLANES, :] = dl1
        delta, m2, v2 = _adamw(w_ref[...], g_ref[...], m_ref[...], v_ref[...])
        d_ref[...] = delta
        m2_ref[...] = m2
        v2_ref[...] = v2

    vmem = pl.BlockSpec(memory_space=pltpu.VMEM)
    return pl.pallas_call(
        body, name="comm_small_allreduce_adamw", in_specs=[vmem] * 4, out_specs=[vmem] * 4,
        out_shape=[jax.ShapeDtypeStruct((_SMALL_ROWS, LANES), F32)] * 4,
        scratch_shapes=[pltpu.VMEM((N_DEV - 1, _SMALL_ROWS, LANES), F32), pltpu.SemaphoreType.DMA((N_DEV - 1,)),
                        pltpu.SemaphoreType.DMA((N_DEV - 1,))],
    )(gpart, w, m, v)


def _qzkv(a):
    kv_end = ATTN_WIDTH + 2 * KV_WIDTH
    return jnp.concatenate([a[..., :ATTN_WIDTH], a[..., kv_end:], a[..., ATTN_WIDTH:kv_end]], axis=-1)


def _qkvz(a):
    return jnp.concatenate([a[..., :ATTN_WIDTH], a[..., 2 * ATTN_WIDTH:], a[..., ATTN_WIDTH:2 * ATTN_WIDTH]], axis=-1)


def kernel(x, positions, pre_norm_w, post_norm_w, attn_w_in, attn_b_in, attn_sinks, attn_w_out, attn_b_out, rec_w_in, rec_lb_logits, rec_gnorm_w, rec_w_out, loss_target, m_pre_norm_w, m_post_norm_w, m_attn_w_in, m_attn_b_in, m_attn_sinks, m_attn_w_out, m_attn_b_out, m_rec_w_in, m_rec_lb_logits, m_rec_gnorm_w, m_rec_w_out, v_pre_norm_w, v_post_norm_w, v_attn_w_in, v_attn_b_in, v_attn_sinks, v_attn_w_out, v_attn_b_out, v_rec_w_in, v_rec_lb_logits, v_rec_gnorm_w, v_rec_w_out):
    b_loc, t_len, _ = x.shape
    n = b_loc * t_len
    ga_in, ct, st = _all_gather_by_chip(attn_w_in[0].astype(BF16), positions.reshape(n, 1).astype(F32))
    wa_in = _qzkv(ga_in.transpose(1, 0, 2).reshape(D_MODEL, ATTN_IN))

    loss_tile, dx, landed, small = _step(
        x.reshape(n, D_MODEL), ct, st, loss_target.reshape(n, D_MODEL),
        pre_norm_w, post_norm_w, wa_in, _qzkv(attn_b_in), attn_sinks, attn_w_out[0].astype(BF16), attn_b_out,
        rec_w_in[0].astype(BF16), rec_lb_logits, rec_gnorm_w, rec_w_out[0].astype(BF16), b_loc, t_len)

    lift = lambda outs: tuple(a[None] for a in outs)
    (l_r_in, o_r_in), (l_r_out, o_r_out), (l_a_out, o_a_out), (l_a_in, o_a_in) = landed
    r_a_in = lift(_sum_adamw(l_a_in, o_a_in, attn_w_in[0], m_attn_w_in[0], v_attn_w_in[0], "adamw_attn_w_in"))
    r_r_in = lift(_sum_adamw(l_r_in, o_r_in, rec_w_in[0], m_rec_w_in[0], v_rec_w_in[0], "adamw_rec_w_in"))
    r_r_out = lift(_sum_adamw(l_r_out, o_r_out, rec_w_out[0], m_rec_w_out[0], v_rec_w_out[0], "adamw_rec_w_out"))
    r_a_out = lift(_sum_adamw(l_a_out, o_a_out, attn_w_out[0], m_attn_w_out[0], v_attn_w_out[0], "adamw_attn_w_out"))

    gsmall = dict(pre_norm_w=small["pre"], post_norm_w=small["post"], attn_b_in=_qkvz(small["ba_in"]), attn_sinks=small["sinks"],
                  attn_b_out=small["ba_out"], rec_lb_logits=jnp.concatenate([small["lb"], jnp.zeros_like(small["lb"])], axis=0),
                  rec_gnorm_w=small["gnorm"], loss=loss_tile[0:1, 0:1])
    nil = jnp.zeros((1, 1), F32)
    wsmall = dict(pre_norm_w=pre_norm_w, post_norm_w=post_norm_w, attn_b_in=attn_b_in, attn_sinks=attn_sinks,
                  attn_b_out=attn_b_out, rec_lb_logits=rec_lb_logits, rec_gnorm_w=rec_gnorm_w, loss=nil)
    msmall = dict(pre_norm_w=m_pre_norm_w, post_norm_w=m_post_norm_w, attn_b_in=m_attn_b_in, attn_sinks=m_attn_sinks,
                  attn_b_out=m_attn_b_out, rec_lb_logits=m_rec_lb_logits, rec_gnorm_w=m_rec_gnorm_w, loss=nil)
    vsmall = dict(pre_norm_w=v_pre_norm_w, post_norm_w=v_post_norm_w, attn_b_in=v_attn_b_in, attn_sinks=v_attn_sinks,
                  attn_b_out=v_attn_b_out, rec_lb_logits=v_rec_lb_logits, rec_gnorm_w=v_rec_gnorm_w, loss=nil)
    shapes = {k: a.shape for k, a in wsmall.items()}
    packed = _small_allreduce_adamw(_pack_small(gsmall), _pack_small(wsmall), _pack_small(msmall), _pack_small(vsmall))
    sg, sd, sm, sv = [_unpack_small(a, shapes) for a in packed]

    big = {"attn_w_in": r_a_in, "attn_w_out": r_a_out, "rec_w_in": r_r_in, "rec_w_out": r_r_out}
    order = ["pre_norm_w", "post_norm_w", "attn_w_in", "attn_b_in", "attn_sinks", "attn_w_out", "attn_b_out", "rec_w_in",
             "rec_lb_logits", "rec_gnorm_w", "rec_w_out"]
    outs = [sg["loss"][0, 0], dx.reshape(b_loc, t_len, D_MODEL)]
    for idx, small_set in enumerate((sg, sd, sm, sv)):
        outs += [big[nm][idx] if nm in big else small_set[nm] for nm in order]
    return tuple(outs)
```

```python
import numpy as np
import jax
import jax.numpy as jnp
from jax import lax
from jax.experimental import pallas as pl
from jax.experimental.pallas import tpu as pltpu

F32, BF16 = jnp.float32, jnp.bfloat16
MESH = pl.DeviceIdType.MESH
N_DEV = 8

D_MODEL = 1024
N_HEADS, HEAD_DIM, N_KV, GROUP = 16, 64, 2, 8
ATTN_WIDTH, KV_WIDTH = 1024, 128
ATTN_IN = 2 * ATTN_WIDTH + 2 * KV_WIDTH
BLK = 128
ROPE_THETA, ROPE_HALF = 500000.0, 8
REC_HEADS, REC_K = 8, 128
REC_WIDTH = REC_HEADS * REC_K
REC_IN = 4 * REC_WIDTH
TN_ROWS = 1024
CH = 32
NORM_EPS = 1e-6
F32_TINY = 1.1754944e-38
ADAM_LR, ADAM_B1, ADAM_B2, ADAM_EPS, ADAM_WD, ADAM_STEP = 0.001, 0.9, 0.999, 1e-08, 0.01, 10

LANES, SUBLANES = 128, 8
TM = 512
NT_DIMS = (((1,), (1,)), ((), ()))
TN_DIMS = (((0,), (0,)), ((), ()))
MB = 2 ** 20


def _params(sem=None, vmem_mb=48, **kw):
    return pltpu.CompilerParams(dimension_semantics=sem, vmem_limit_bytes=vmem_mb * MB, **kw)


def _col_chunk(m):
    return 768 if m % 1024 else 1024


def _sigmoid(x):
    return 1.0 / (1.0 + jnp.exp(-x))


def _split3(x):
    hi = x.astype(BF16)
    r1 = x - hi.astype(F32)
    mid = r1.astype(BF16)
    lo = (r1 - mid.astype(F32)).astype(BF16)
    return hi, mid, lo


def _dot3(l_bf, x):
    hi, mid, lo = _split3(x)
    return (jnp.dot(l_bf, hi, preferred_element_type=F32) + jnp.dot(l_bf, mid, preferred_element_type=F32)
            + jnp.dot(l_bf, lo, preferred_element_type=F32))


def _rope_lanes():
    lane = np.arange(LANES) % HEAD_DIM
    inv = np.float32(ROPE_THETA) ** (-(np.arange(ROPE_HALF, dtype=np.float32) * np.float32(2.0) / np.float32(2 * ROPE_HALF)))
    freq = np.where(lane < 2 * ROPE_HALF, inv[lane % ROPE_HALF], 0.0).astype(np.float32)[None, :]
    sign = np.where(lane < ROPE_HALF, -1.0, np.where(lane < 2 * ROPE_HALF, 1.0, 0.0)).astype(np.float32)[None, :]
    return jnp.asarray(freq), jnp.asarray(sign)


def _rope_tables_into(p_ref, f_ref, s_ref, c_out, s_out):
    def rows(i, carry):
        rs = pl.ds(pl.multiple_of(i * TM, TM), TM)
        ang = p_ref[rs, :] * f_ref[...]
        c_out[rs, :] = jnp.cos(ang)
        s_out[rs, :] = jnp.sin(ang) * s_ref[...]
        return carry

    lax.fori_loop(0, p_ref.shape[0] // TM, rows, 0)


def _rope_apply(xv, c, s, lm):
    partner = jnp.where(lm < ROPE_HALF, pltpu.roll(xv, LANES - ROPE_HALF, 1), pltpu.roll(xv, ROPE_HALF, 1))
    return xv * c + partner * s


def _rope_bwd(dy, c, s, lm):
    t = dy * s
    partner = jnp.where(lm < ROPE_HALF, pltpu.roll(t, LANES - ROPE_HALF, 1),
                        jnp.where(lm < 2 * ROPE_HALF, pltpu.roll(t, ROPE_HALF, 1), 0.0))
    return dy * c + partner


def _lower_bound(lb_logits):
    def body(l_ref, o_ref):
        l0, l1 = l_ref[0:1, :], l_ref[1:2, :]
        m = jnp.maximum(l0, l1)
        e0, e1 = jnp.exp(l0 - m), jnp.exp(l1 - m)
        o_ref[...] = e1 / (e0 + e1)

    return pl.pallas_call(body, name="lower_bound", out_shape=jax.ShapeDtypeStruct((1, lb_logits.shape[1]), F32))(lb_logits)


def _norm_matmul(x, pw, w, bias, name, shards=()):
    n, m = x.shape[0], w.shape[1]
    cn = _col_chunk(m)
    has_bias = bias is not None
    nsh, steps = len(shards), n // TM

    def body(*refs):
        refs = list(refs)
        x_ref, pw_ref, w_ref = refs[:3]
        b_ref = refs[3] if has_bias else None
        refs = refs[4 if has_bias else 3:]
        sh_in, (p_ref, h_ref), sh_out, sems = refs[:nsh], refs[nsh:nsh + 2], refs[nsh + 2:2 * nsh + 2], refs[2 * nsh + 2:]
        if nsh:
            @pl.when(pl.program_id(0) == 0)
            def _():
                _gather_start(sh_in, sh_out, sems)

        xv = x_ref[...]
        r = lax.rsqrt(jnp.mean(xv * xv, axis=-1, keepdims=True) + NORM_EPS)
        h = ((xv * r) * pw_ref[...]).astype(BF16)
        h_ref[...] = h
        for j in range(0, m, cn):
            acc = jnp.dot(h, w_ref[:, j:j + cn], preferred_element_type=F32)
            if has_bias:
                acc = acc + b_ref[:, j:j + cn]
            p_ref[:, j:j + cn] = acc

        if nsh:
            @pl.when(pl.program_id(0) == steps - 1)
            def _():
                _gather_wait(sh_in, sh_out, sems)

    rows = pl.BlockSpec((TM, D_MODEL), lambda i: (i, 0))
    const = lambda shape: pl.BlockSpec(shape, lambda i: (0, 0))
    hbm = pl.BlockSpec(memory_space=pl.ANY)
    in_specs = [rows, const((1, D_MODEL)), const((D_MODEL, m))] + ([const((1, m))] if has_bias else []) + [hbm] * nsh
    args = (x, pw, w) + ((bias,) if has_bias else ()) + tuple(shards)
    return pl.pallas_call(
        body, name=name, grid=(steps,), in_specs=in_specs,
        out_specs=[pl.BlockSpec((TM, m), lambda i: (i, 0)), rows] + [hbm] * nsh,
        out_shape=[jax.ShapeDtypeStruct((n, m), F32), jax.ShapeDtypeStruct((n, D_MODEL), BF16)] + _gather_shapes(shards),
        scratch_shapes=_gather_sems(nsh) if nsh else [],
        compiler_params=_params(("arbitrary",), 56),
    )(*args)


def _outproj_postnorm(g, w, bias, xres, qw, name):
    n = g.shape[0]

    def body(g_ref, w_ref, b_ref, x_ref, qw_ref, y_ref, o_ref):
        y = jnp.dot(g_ref[...], w_ref[...], preferred_element_type=F32) + b_ref[...]
        y_ref[...] = y
        r = lax.rsqrt(jnp.mean(y * y, axis=-1, keepdims=True) + NORM_EPS)
        o_ref[...] = x_ref[...] + (y * r) * qw_ref[...]

    rows = pl.BlockSpec((TM, D_MODEL), lambda i: (i, 0))
    const = lambda shape: pl.BlockSpec(shape, lambda i: (0, 0))
    return pl.pallas_call(
        body, name=name, grid=(n // TM,),
        in_specs=[rows, const((D_MODEL, D_MODEL)), const((1, D_MODEL)), rows, const((1, D_MODEL))],
        out_specs=[rows, rows], out_shape=[jax.ShapeDtypeStruct((n, D_MODEL), F32)] * 2,
        compiler_params=_params(("arbitrary",), 48),
    )(g, w, bias, xres, qw)


def _outproj_loss_bwd(g, w, xres, qw, tgt, name):
    n = g.shape[0]
    steps = n // TM

    def body(g_ref, w_ref, x_ref, qw_ref, t_ref, dx_ref, dg_ref, dy_ref, dqw_ref, loss_ref, acc_ref):
        i = pl.program_id(0)

        @pl.when(i == 0)
        def _():
            acc_ref[...] = jnp.zeros_like(acc_ref)
            dqw_ref[...] = jnp.zeros_like(dqw_ref)

        y = jnp.dot(g_ref[...], w_ref[...], preferred_element_type=F32)
        r = lax.rsqrt(jnp.mean(y * y, axis=-1, keepdims=True) + NORM_EPS)
        u = y * r
        e = (x_ref[...] + u * qw_ref[...]) - t_ref[...]
        dxn = e * (1.0 / D_MODEL)
        dx_ref[...] = dxn
        acc_ref[...] += jnp.sum(e * e, axis=0, keepdims=True)
        du = dxn * qw_ref[...]
        dy = (r * (du - u * jnp.mean(du * u, axis=-1, keepdims=True))).astype(BF16)
        dqw_ref[...] += jnp.sum(dxn * u, axis=0, keepdims=True)
        dy_ref[...] = dy
        dg_ref[...] = lax.dot_general(dy, w_ref[...], NT_DIMS, preferred_element_type=F32)

        @pl.when(i == steps - 1)
        def _():
            loss_ref[...] = jnp.full(loss_ref.shape, jnp.sum(acc_ref[...]) * (0.5 / D_MODEL), F32)

    rows = pl.BlockSpec((TM, D_MODEL), lambda i: (i, 0))
    const = lambda shape: pl.BlockSpec(shape, lambda i: (0, 0))
    return pl.pallas_call(
        body, name=name, grid=(steps,),
        in_specs=[rows, const((D_MODEL, D_MODEL)), rows, const((1, D_MODEL)), rows],
        out_specs=[rows, rows, rows, const((1, D_MODEL)), const((SUBLANES, LANES))],
        out_shape=[jax.ShapeDtypeStruct((n, D_MODEL), F32), jax.ShapeDtypeStruct((n, D_MODEL), F32),
                   jax.ShapeDtypeStruct((n, D_MODEL), BF16), jax.ShapeDtypeStruct((1, D_MODEL), F32),
                   jax.ShapeDtypeStruct((SUBLANES, LANES), F32)],
        scratch_shapes=[pltpu.VMEM((1, D_MODEL), F32)], compiler_params=_params(("arbitrary",), 48),
    )(g, w, xres, qw, tgt)


_QCOL, _ZCOL, _KCOL, _VCOL = 0, 1024, 2048, 2176


def _head_stack(chunks, heads, lt64):
    return jnp.concatenate([jnp.where(lt64 if n % 2 == 0 else ~lt64, chunks[n // 2], 0.0) for n in heads], axis=0)


def _dup_half(x, h, lt64):
    r = pltpu.roll(x, HEAD_DIM, 1)
    return jnp.where(lt64, x, r) if h == 0 else jnp.where(lt64, r, x)


def _pair_chunk(xt, c2):
    a, b = 2 * c2, 2 * c2 + 1
    return jnp.concatenate([xt[:HEAD_DIM, a * BLK:(a + 1) * BLK], xt[HEAD_DIM:, b * BLK:(b + 1) * BLK]], axis=0).T


def _attn_mask_t(i):
    key = lax.broadcasted_iota(jnp.int32, (2 * BLK, BLK), 0)
    qry = lax.broadcasted_iota(jnp.int32, (2 * BLK, BLK), 1)
    valid = (key > qry) & (key <= qry + BLK) & ((key >= BLK) | (i > 0))
    return jnp.tile(jnp.where(valid, 0.0, -1e30), (1, GROUP))


def _attn_probs_t(s, heads, sink_ref, mask):
    s = s + mask
    head = lax.broadcasted_iota(jnp.int32, (1, len(heads) * BLK), 1) >> 7
    sk = jnp.zeros((1, len(heads) * BLK), F32)
    for j, n in enumerate(heads):
        sk = jnp.where(head == j, sink_ref[0, n], sk)
    m = jnp.maximum(jnp.max(s, axis=0, keepdims=True), sk)
    p = jnp.exp(s - m)
    esk = jnp.exp(sk - m)
    inv = 1.0 / (jnp.sum(p, axis=0, keepdims=True) + esk)
    return p * inv, esk * inv


def _attn_fwd(p, ct, st, sinks, b_loc, nb, shards):
    n = p.shape[0]
    nsh = len(shards)

    def body(sink_ref, q_ref, z_ref, kc_ref, kp_ref, vc_ref, vp_ref, cc_ref, sc_ref, cp_ref, sp_ref, *rest):
        sh_in, (o_ref, g_ref), sh_out, sems = rest[:nsh], rest[nsh:nsh + 2], rest[nsh + 2:2 * nsh + 2], rest[2 * nsh + 2:]
        b, i = pl.program_id(0), pl.program_id(1)

        @pl.when((b == 0) & (i == 0))
        def _():
            _gather_start(sh_in, sh_out, sems)

        lane = lax.broadcasted_iota(jnp.int32, (BLK, LANES), 1)
        lm = lane & (HEAD_DIM - 1)
        cc, sc = cc_ref[...], sc_ref[...]
        kcat = jnp.concatenate([_rope_apply(kp_ref[...], cp_ref[...], sp_ref[...], lm),
                                _rope_apply(kc_ref[...], cc, sc, lm)], axis=0)
        vcat = jnp.concatenate([vp_ref[...], vc_ref[...]], axis=0)
        qr = [_rope_apply(q_ref[:, c * LANES:(c + 1) * LANES], cc, sc, lm) * (HEAD_DIM ** -0.5) for c in range(8)]
        valid = _attn_mask_t(i)
        lt64, lt64k = lane < HEAD_DIM, lax.broadcasted_iota(jnp.int32, (2 * BLK, LANES), 1) < HEAD_DIM
        def kv_head(h):
            heads = list(range(h * GROUP, (h + 1) * GROUP))
            kext, vext = _dup_half(kcat, h, lt64k).astype(BF16), _dup_half(vcat, h, lt64k).astype(BF16)
            qst = _head_stack(qr, heads, lt64).astype(BF16)
            s = lax.dot_general(kext, qst, NT_DIMS, preferred_element_type=F32)
            yield
            pn, _ = _attn_probs_t(s, heads, sink_ref, valid)
            ot = lax.dot_general(vext, pn.astype(BF16), TN_DIMS, preferred_element_type=F32)
            yield
            for c2 in range(GROUP // 2):
                oc = _pair_chunk(ot, c2)
                cols = slice((4 * h + c2) * LANES, (4 * h + c2 + 1) * LANES)
                zc = z_ref[:, cols]
                o_ref[:, cols] = oc
                g_ref[:, cols] = (oc * (zc * _sigmoid(zc))).astype(BF16)

        _in_stages([kv_head(h) for h in range(N_KV)])

        @pl.when((b == b_loc - 1) & (i == nb - 1))
        def _():
            _gather_wait(sh_in, sh_out, sems)

    cur = lambda b, i: b * nb + i
    prev = lambda b, i: b * nb + jnp.maximum(i - 1, 0)
    wide = lambda cb: pl.BlockSpec((BLK, ATTN_WIDTH), lambda b, i: (cur(b, i), cb))
    kv = lambda rowf, cb: pl.BlockSpec((BLK, LANES), lambda b, i: (rowf(b, i), cb))
    hbm = pl.BlockSpec(memory_space=pl.ANY)
    in_specs = [pl.BlockSpec(memory_space=pltpu.SMEM), wide(0), wide(1),
                kv(cur, _KCOL // LANES), kv(prev, _KCOL // LANES), kv(cur, _VCOL // LANES), kv(prev, _VCOL // LANES),
                kv(cur, 0), kv(cur, 0), kv(prev, 0), kv(prev, 0)] + [hbm] * nsh
    return pl.pallas_call(
        body, name="attn_fwd", grid=(b_loc, nb), in_specs=in_specs, out_specs=[wide(0), wide(0)] + [hbm] * nsh,
        out_shape=[jax.ShapeDtypeStruct((n, ATTN_WIDTH), F32), jax.ShapeDtypeStruct((n, ATTN_WIDTH), BF16)] + _gather_shapes(shards),
        scratch_shapes=_gather_sems(nsh), compiler_params=_params(("arbitrary", "arbitrary"), 48),
    )(sinks, p, p, p, p, p, p, ct, st, ct, st, *shards)


def _attn_bwd(p, ct, st, sinks, o, dg, b_loc, nb, parts):
    n = p.shape[0]
    nparts = len(parts)

    def body(sink_ref, q_ref, z_ref, kc_ref, kp_ref, vc_ref, vp_ref, cc_ref, sc_ref, cp_ref, sp_ref, o_ref, dg_ref, *rest):
        part_refs, (dp_ref, ds_ref), land_refs = rest[:nparts], rest[nparts:nparts + 2], rest[nparts + 2:2 * nparts + 2]
        dq_s, dz_s, dk_s, dv_s = rest[2 * nparts + 2:2 * nparts + 6]
        sems = rest[2 * nparts + 6:]
        b, i = pl.program_id(0), pl.program_id(1)

        @pl.when((b == 0) & (i == 0))
        def _():
            _scatter_start(part_refs, land_refs, sems)

        @pl.when((b == b_loc - 1) & (i == nb))
        def _():
            _scatter_wait(part_refs, land_refs, sems)

        lane = lax.broadcasted_iota(jnp.int32, (BLK, LANES), 1)
        lm = lane & (HEAD_DIM - 1)

        @pl.when((b == 0) & (i == 0))
        def _():
            ds_ref[...] = jnp.zeros_like(ds_ref)

        @pl.when(i < nb)
        def _compute():
            cc, sc = cc_ref[...], sc_ref[...]
            kcat = jnp.concatenate([_rope_apply(kp_ref[...], cp_ref[...], sp_ref[...], lm),
                                    _rope_apply(kc_ref[...], cc, sc, lm)], axis=0)
            vcat = jnp.concatenate([vp_ref[...], vc_ref[...]], axis=0)
            qr = [_rope_apply(q_ref[:, c * LANES:(c + 1) * LANES], cc, sc, lm) * (HEAD_DIM ** -0.5) for c in range(8)]
            valid = _attn_mask_t(i)
            lt64, lt64k = lane < HEAD_DIM, lax.broadcasted_iota(jnp.int32, (2 * BLK, LANES), 1) < HEAD_DIM
            do_chunks, doo_chunks, dz_chunks = [], [], []
            for c in range(8):
                cols = slice(c * LANES, (c + 1) * LANES)
                zc, oc, dgc = z_ref[:, cols], o_ref[:, cols], dg_ref[:, cols]
                sg = _sigmoid(zc)
                do_chunks.append(dgc * (zc * sg))
                dz_chunks.append(dgc * oc * (sg * (1.0 + zc * (1.0 - sg))))
                doo_chunks.append(do_chunks[c] * oc)
            dq_chunks = [None] * 8
            dk_h, dv_h, ds_parts = [None] * N_KV, [None] * N_KV, [None] * N_KV
            tile_lane = lax.broadcasted_iota(jnp.int32, (SUBLANES, LANES), 1)
            tile_row = lax.broadcasted_iota(jnp.int32, (SUBLANES, LANES), 0)
            ones8 = jnp.ones((SUBLANES, LANES), BF16)

            def kv_head(h):
                heads = list(range(h * GROUP, (h + 1) * GROUP))
                kext = _dup_half(kcat, h, lt64k)
                kext_bf, kext_t = kext.astype(BF16), kext.T.astype(BF16)
                vext = _dup_half(vcat, h, lt64k).astype(BF16)
                qst = _head_stack(qr, heads, lt64).astype(BF16)
                pn, psink = _attn_probs_t(lax.dot_general(kext_bf, qst, NT_DIMS, preferred_element_type=F32), heads, sink_ref, valid)
                do_bf = _head_stack(do_chunks, heads, lt64).astype(BF16)
                delta = sum(lax.dot_general(ones8, part, NT_DIMS, preferred_element_type=F32)
                            for part in _split3(_head_stack(doo_chunks, heads, lt64)))[0:1, :]
                dpt = lax.dot_general(vext, do_bf, NT_DIMS, preferred_element_type=F32)
                dst = (pn * (dpt - delta)).astype(BF16)
                sink_term = psink * delta
                ds_acc = jnp.zeros((SUBLANES, LANES), F32)
                for j, n in enumerate(heads):
                    val = -jnp.sum(sink_term[:, j * BLK:(j + 1) * BLK])
                    ds_acc = ds_acc + jnp.where((tile_lane == n) & (tile_row == 0), val, 0.0)
                ds_parts[h] = ds_acc
                dqt = jnp.dot(kext_t, dst, preferred_element_type=F32) * (HEAD_DIM ** -0.5)
                dk_ext = jnp.dot(dst, qst, preferred_element_type=F32)
                dv_ext = jnp.dot(pn.astype(BF16), do_bf, preferred_element_type=F32)
                dk_h[h] = dk_ext + pltpu.roll(dk_ext, HEAD_DIM, 1)
                dv_h[h] = dv_ext + pltpu.roll(dv_ext, HEAD_DIM, 1)
                for c2 in range(GROUP // 2):
                    dq_chunks[4 * h + c2] = _rope_bwd(_pair_chunk(dqt, c2), cc, sc, lm)

            for h in range(N_KV):
                kv_head(h)
            ds_ref[...] += ds_parts[0] + ds_parts[1]
            dk_full = jnp.where(lt64k, dk_h[0], dk_h[1])
            dv_full = jnp.where(lt64k, dv_h[0], dv_h[1])

            @pl.when(i >= 1)
            def _emit():
                dp_ref[:, _QCOL:_QCOL + ATTN_WIDTH] = dq_s[...]
                dp_ref[:, _ZCOL:_ZCOL + ATTN_WIDTH] = dz_s[...]
                dp_ref[:, _KCOL:_KCOL + KV_WIDTH] = _rope_bwd(dk_s[...] + dk_full[:BLK], cp_ref[...], sp_ref[...], lm)
                dp_ref[:, _VCOL:_VCOL + KV_WIDTH] = dv_s[...] + dv_full[:BLK]

            for c in range(8):
                dq_s[:, c * LANES:(c + 1) * LANES] = dq_chunks[c]
                dz_s[:, c * LANES:(c + 1) * LANES] = dz_chunks[c]
            dk_s[...] = dk_full[BLK:]
            dv_s[...] = dv_full[BLK:]

        @pl.when(i == nb)
        def _final():
            dp_ref[:, _QCOL:_QCOL + ATTN_WIDTH] = dq_s[...]
            dp_ref[:, _ZCOL:_ZCOL + ATTN_WIDTH] = dz_s[...]
            dp_ref[:, _KCOL:_KCOL + KV_WIDTH] = _rope_bwd(dk_s[...], cc_ref[...], sc_ref[...], lm)
            dp_ref[:, _VCOL:_VCOL + KV_WIDTH] = dv_s[...]

    cur = lambda b, i: b * nb + jnp.minimum(i, nb - 1)
    prev = lambda b, i: b * nb + jnp.maximum(jnp.minimum(i, nb - 1) - 1, 0)
    emit = lambda b, i: b * nb + jnp.maximum(i - 1, 0)
    hbm = pl.BlockSpec(memory_space=pl.ANY)
    wide = lambda cb: pl.BlockSpec((BLK, ATTN_WIDTH), lambda b, i: (cur(b, i), cb))
    kv = lambda rowf, cb: pl.BlockSpec((BLK, LANES), lambda b, i: (rowf(b, i), cb))
    in_specs = [pl.BlockSpec(memory_space=pltpu.SMEM), wide(0), wide(1),
                kv(cur, _KCOL // LANES), kv(prev, _KCOL // LANES), kv(cur, _VCOL // LANES), kv(prev, _VCOL // LANES),
                kv(cur, 0), kv(cur, 0), kv(prev, 0), kv(prev, 0), wide(0), wide(0)] + [hbm] * nparts
    out_specs = [pl.BlockSpec((BLK, ATTN_IN), lambda b, i: (emit(b, i), 0)),
                 pl.BlockSpec((SUBLANES, LANES), lambda b, i: (0, 0))] + [hbm] * nparts
    return pl.pallas_call(
        body, name="attn_bwd", grid=(b_loc, nb + 1), in_specs=in_specs, out_specs=out_specs,
        out_shape=[jax.ShapeDtypeStruct((n, ATTN_IN), F32), jax.ShapeDtypeStruct((SUBLANES, LANES), F32)] + _scatter_lands(parts),
        scratch_shapes=[pltpu.VMEM((BLK, ATTN_WIDTH), F32), pltpu.VMEM((BLK, ATTN_WIDTH), F32),
                        pltpu.VMEM((BLK, KV_WIDTH), F32), pltpu.VMEM((BLK, KV_WIDTH), F32)] + _scatter_sems(nparts),
        compiler_params=_params(("arbitrary", "arbitrary"), 48),
    )(sinks, p, p, p, p, p, p, ct, st, ct, st, o, dg, *parts)


_CUM_ROWS = 256
HALF = CH // 2
_ROW0 = [SUBLANES * (s // SUBLANES) for s in range(CH)]
_ROW1 = [HALF * (s // HALF + 1) for s in range(CH)]
_ROWS_OF = [_ROW1[s] - _ROW0[s] for s in range(CH)]
_OFF_OF = [sum(_ROWS_OF[:s]) for s in range(CH)]


def _tri(lower):
    r = lax.broadcasted_iota(jnp.int32, (_CUM_ROWS, _CUM_ROWS), 0)
    c = lax.broadcasted_iota(jnp.int32, (_CUM_ROWS, _CUM_ROWS), 1)
    same = (r ^ c) < CH
    return (same & ((c <= r) if lower else (c >= r))).astype(BF16)


def _gates(qp, fp, lb):
    e = jnp.exp(-jnp.abs(fp))
    r = 1.0 / (1.0 + e)
    sig_neg = jnp.where(fp >= 0, e, 1.0) * r
    sig = jnp.where(fp >= 0, 1.0, e) * r
    g = jnp.log(jnp.maximum(lb + (1.0 - lb) * sig, F32_TINY))
    return qp * _sigmoid(qp), g, (1.0 - lb) * sig_neg, sig_neg


def _pair_rows(bc, s):
    diff = bc[_ROW0[s]:_ROW1[s], :] - bc[s:s + 1, :]
    head = jnp.minimum(diff[:SUBLANES, :], 0.0)
    return jnp.exp(head if diff.shape[0] == SUBLANES else jnp.concatenate([head, diff[SUBLANES:, :]], axis=0))


def _cross_half(q, k, bc):
    r = bc[HALF - 1:HALF, :]
    e1, e0 = jnp.exp(bc[HALF:, :] - r), jnp.exp(r - bc[:HALF, :])
    return q[HALF:, :] * e1, e1, k[:HALF, :] * e0, e0


HP = 8
REC_TB = 256
_HW = HP * REC_K


def _hgrn_specs(tb, nt, reverse):
    tmap = (lambda t: nt - 1 - t) if reverse else (lambda t: t)
    groups = REC_HEADS // HP
    blk = lambda cb: pl.BlockSpec((tb, _HW), lambda h, b, t: (b * nt + tmap(t), cb * groups + h))
    head = pl.BlockSpec((tb, _HW), lambda h, b, t: (b * nt + tmap(t), h))
    lbs = pl.BlockSpec((1, _HW), lambda h, b, t: (0, h))
    gws = pl.BlockSpec((1, REC_K), lambda h, b, t: (0, 0))
    hist = pl.BlockSpec((HP, 1, tb // CH, REC_K, REC_K), lambda h, b, t: (h, b, tmap(t), 0, 0))
    return blk, head, lbs, gws, hist


def _chunk_rows(c, first=0, size=CH):
    start = c * CH + first
    return pl.ds(start if isinstance(start, int) else pl.multiple_of(start, CH if first % CH == 0 else SUBLANES), size)


def _in_stages(heads):
    live = list(heads)
    while live:
        live = [g for g in live if next(g, live) is not live]


def _cumsum_chunks(tri, x, out_ref, tb):
    for r in range(0, tb, _CUM_ROWS):
        out_ref[r:r + _CUM_ROWS, :] = _dot3(tri, x[r:r + _CUM_ROWS, :])


def _hgrn_fwd(p, lb, gw, b_loc, t_len):
    n = p.shape[0]
    tb = min(REC_TB, t_len)
    nt, nck = t_len // tb, tb // CH

    def body(qp_ref, fp_ref, i_ref, z_ref, lb_ref, gw_ref, oraw_ref, g_ref, sh_ref, q_s, k_s, b_s, o_s, st_ref,
             car_o, car_a, car_s, car_st):
        @pl.when(pl.program_id(2) == 0)
        def _():
            st_ref[...] = jnp.zeros_like(st_ref)

        qv, g, kk, _ = _gates(qp_ref[...], fp_ref[...], lb_ref[...])
        q_s[...] = qv
        k_s[...] = kk
        _cumsum_chunks(_tri(True), g, b_s, tb)
        ones = jnp.ones((REC_K, REC_K), BF16)
        sub = lax.broadcasted_iota(jnp.int32, (SUBLANES, REC_K), 0)

        rows_of = _chunk_rows

        def issue(c, hp):
            rs, cs = rows_of(c), slice(hp * REC_K, (hp + 1) * REC_K)
            q, k, bc, v = q_s[rs, cs], k_s[rs, cs], b_s[rs, cs], i_ref[rs, cs]
            st = st_ref[hp]
            sh_ref[hp, 0, c] = st
            o = lax.dot_general((q * jnp.exp(bc)).astype(BF16), st.astype(BF16), NT_DIMS, preferred_element_type=F32)
            w = jnp.concatenate([q[_ROW0[s]:_ROW1[s], :] * _pair_rows(bc, s) * k[s:s + 1, :] for s in range(CH)], axis=0)
            a = jnp.dot(w.astype(BF16), ones, preferred_element_type=F32)
            qe1, _, ke0, _ = _cross_half(q, k, bc)
            s10 = lax.dot_general(qe1.astype(BF16), ke0.astype(BF16), NT_DIMS, preferred_element_type=F32)
            kd = k * jnp.exp(bc[CH - 1:CH, :] - bc)
            st_new = lax.dot_general(v.astype(BF16), kd.astype(BF16), TN_DIMS, preferred_element_type=F32)
            return o, a, s10, st_new

        def advance_state(c, hp, st_new):
            bl = b_s[_chunk_rows(c, CH - SUBLANES, SUBLANES), hp * REC_K:(hp + 1) * REC_K][SUBLANES - 1:, :]
            st_ref[hp] = st_ref[hp] * jnp.exp(bl) + st_new

        def cross(c, hp, s10):
            v0 = i_ref[_chunk_rows(c, 0, HALF), hp * REC_K:(hp + 1) * REC_K]
            return jnp.dot(s10.astype(BF16), v0.astype(BF16), preferred_element_type=F32)

        def finish(c, hp, o, a, o_cross):
            rs, cs = rows_of(c), slice(hp * REC_K, (hp + 1) * REC_K)
            v = i_ref[rs, cs]
            acc = [jnp.zeros((SUBLANES, REC_K), F32) for _ in range(CH // SUBLANES)]
            for s in range(CH):
                j = s // SUBLANES
                vs = v[s:s + 1, :]
                for jj in range(j, _ROW1[s] // SUBLANES):
                    blk = a[_OFF_OF[s] + (jj - j) * SUBLANES:_OFF_OF[s] + (jj - j + 1) * SUBLANES, :]
                    if jj == j:
                        blk = jnp.where(sub >= s - j * SUBLANES, blk, 0.0)
                    acc[jj] = acc[jj] + blk * vs
            o_s[rs, cs] = o + jnp.concatenate(acc, axis=0) + jnp.concatenate([jnp.zeros((HALF, REC_K), F32), o_cross], axis=0)

        def park(slot, results):
            for hp, (o, a, s10, st_new) in enumerate(results):
                car_o[slot, hp], car_a[slot, hp], car_s[slot, hp], car_st[slot, hp] = o, a, s10, st_new

        def retire(c, slot):
            for hp in range(HP):
                advance_state(c, hp, car_st[slot, hp])
            yield
            crosses = [cross(c, hp, car_s[slot, hp]) for hp in range(HP)]
            for hp in range(HP):
                finish(c, hp, car_o[slot, hp], car_a[slot, hp], crosses[hp])

        def step(c, slot):
            closing = retire(c - 1, slot)
            next(closing)
            park(1 - slot, [issue(c, hp) for hp in range(HP)])
            next(closing, None)

        def trip(j, carry):
            step(2 * j + 1, 0)
            step(2 * j + 2, 1)
            return carry

        assert nck % 2 == 0
        park(0, [issue(0, hp) for hp in range(HP)])
        lax.fori_loop(0, nck // 2 - 1, trip, 0)
        step(nck - 1, 0)
        for _ in retire(nck - 1, 1):
            pass
        oraw_ref[...] = o_s[...]
        for hp in range(HP):
            cs = slice(hp * REC_K, (hp + 1) * REC_K)
            o, zc = o_s[:, cs], z_ref[:, cs]
            on = (o * lax.rsqrt(jnp.mean(o * o, axis=-1, keepdims=True) + NORM_EPS)) * gw_ref[...]
            g_ref[:, cs] = (on * (zc * _sigmoid(zc))).astype(BF16)

    blk, head, lbs, gws, hist = _hgrn_specs(tb, nt, False)
    return pl.pallas_call(
        body, name="hgrn_fwd", grid=(REC_HEADS // HP, b_loc, nt),
        in_specs=[blk(0), blk(1), blk(2), blk(3), lbs, gws], out_specs=[head, head, hist],
        out_shape=[jax.ShapeDtypeStruct((n, REC_WIDTH), F32), jax.ShapeDtypeStruct((n, REC_WIDTH), BF16),
                   jax.ShapeDtypeStruct((REC_HEADS, b_loc, t_len // CH, REC_K, REC_K), F32)],
        scratch_shapes=[pltpu.VMEM((tb, _HW), F32)] * 4 + [pltpu.VMEM((HP, REC_K, REC_K), F32)] + [
            pltpu.VMEM((2, HP, CH, REC_K), F32), pltpu.VMEM((2, HP, sum(_ROWS_OF), REC_K), F32),
            pltpu.VMEM((2, HP, HALF, HALF), F32), pltpu.VMEM((2, HP, REC_K, REC_K), F32)],
        compiler_params=_params(("arbitrary", "arbitrary", "arbitrary"), 48),
    )(p, p, p, p, lb, gw)


def _hgrn_bwd(p, lb, gw, oraw, sh, dg, b_loc, t_len):
    n = p.shape[0]
    tb = min(REC_TB, t_len)
    nt, nck = t_len // tb, tb // CH
    assert HP == REC_HEADS

    def body(qp_ref, fp_ref, i_ref, z_ref, lb_ref, gw_ref, oraw_ref, dg_ref, sh_ref,
             dp_ref, dlb_ref, dgw_ref,
             q_s, k_s, b_s, do_s, dqv_s, dk_s, db_s, dst_ref, car_r, car_dst, car_dec, car_a, car_da, car_x):
        dq_ref, df_ref, di_ref, dz_ref = (dp_ref.at[:, part * REC_WIDTH:(part + 1) * REC_WIDTH] for part in range(4))
        b, t = pl.program_id(1), pl.program_id(2)

        @pl.when(t == 0)
        def _():
            dst_ref[...] = jnp.zeros_like(dst_ref)

        @pl.when((b == 0) & (t == 0))
        def _():
            dlb_ref[...] = jnp.zeros_like(dlb_ref)
            dgw_ref[...] = jnp.zeros_like(dgw_ref)

        lbv, qp, fp = lb_ref[...], qp_ref[...], fp_ref[...]
        qv, g, kk, sig_neg = _gates(qp, fp, lbv)
        q_s[...] = qv
        k_s[...] = kk
        _cumsum_chunks(_tri(True), g, b_s, tb)
        gwv = gw_ref[...]
        for hp in range(HP):
            cs = slice(hp * REC_K, (hp + 1) * REC_K)
            o, zc, dgv = oraw_ref[:, cs], z_ref[:, cs], dg_ref[:, cs]
            rn = lax.rsqrt(jnp.mean(o * o, axis=-1, keepdims=True) + NORM_EPS)
            on = o * rn
            sgz = _sigmoid(zc)
            dz_ref[:, cs] = (dgv * (on * gwv) * (sgz * (1.0 + zc * (1.0 - sgz)))).astype(BF16)
            dpre = dgv * (zc * sgz)
            dgw_ref[hp] += jnp.sum(dpre * on, axis=0, keepdims=True)
            don = dpre * gwv
            do_s[:, cs] = rn * (don - on * jnp.mean(don * on, axis=-1, keepdims=True))

        ones = jnp.ones((REC_K, REC_K), BF16)
        sub = lax.broadcasted_iota(jnp.int32, (SUBLANES, REC_K), 0)
        rowid = lax.broadcasted_iota(jnp.int32, (CH, REC_K), 0)
        ngrp = CH // SUBLANES
        piece_row = lax.broadcasted_iota(jnp.int32, (1, sum(_ROWS_OF)), 1)
        key_of = jnp.zeros((1, sum(_ROWS_OF)), jnp.int32)
        for s in range(1, CH):
            key_of = jnp.where(piece_row >= _OFF_OF[s], s, key_of)
        pick = (key_of == lax.broadcasted_iota(jnp.int32, (CH, sum(_ROWS_OF)), 0)).astype(BF16)

        def operands(c, hp):
            rs, cs = _chunk_rows(c), slice(hp * REC_K, (hp + 1) * REC_K)
            return rs, cs, q_s[rs, cs], k_s[rs, cs], b_s[rs, cs], i_ref[rs, cs], do_s[rs, cs]

        def issue(c, hp, slot):
            _, _, q, k, bc, v, do = operands(c, hp)
            st, dst = sh_ref[hp, 0, c], dst_ref[hp]
            qe, kd = q * jnp.exp(bc), k * jnp.exp(bc[CH - 1:CH, :] - bc)
            do_bf, dst_bf = do.astype(BF16), dst.astype(BF16)
            car_r[slot, hp, 0:CH] = jnp.dot(do_bf, st.astype(BF16), preferred_element_type=F32)
            car_r[slot, hp, CH:2 * CH] = jnp.dot(v.astype(BF16), dst_bf, preferred_element_type=F32)
            car_r[slot, hp, 2 * CH:3 * CH] = lax.dot_general(kd.astype(BF16), dst_bf, NT_DIMS, preferred_element_type=F32)
            car_dst[slot, hp] = lax.dot_general(do_bf, qe.astype(BF16), TN_DIMS, preferred_element_type=F32)
            dec = jnp.concatenate([_pair_rows(bc, s) for s in range(CH)], axis=0)
            qk = jnp.concatenate([q[_ROW0[s]:_ROW1[s], :] * k[s:s + 1, :] for s in range(CH)], axis=0)
            x = jnp.concatenate([do[_ROW0[s]:_ROW1[s], :] * v[s:s + 1, :] for s in range(CH)], axis=0)
            car_dec[slot, hp] = dec
            car_a[slot, hp] = jnp.dot((qk * dec).astype(BF16), ones, preferred_element_type=F32)
            car_da[slot, hp] = jnp.dot(x.astype(BF16), ones, preferred_element_type=F32)
            qe1, _, ke0, _ = _cross_half(q, k, bc)
            qe1_bf, ke0_bf = qe1.astype(BF16), ke0.astype(BF16)
            do1_bf, v0_bf = do[HALF:, :].astype(BF16), v[:HALF, :].astype(BF16)
            car_x[slot, hp, 0:HALF] = lax.dot_general(ke0_bf, qe1_bf, NT_DIMS, preferred_element_type=F32)
            car_x[slot, hp, HALF:2 * HALF] = lax.dot_general(do1_bf, v0_bf, NT_DIMS, preferred_element_type=F32)
            car_x[slot, hp, 2 * HALF:3 * HALF] = lax.dot_general(v0_bf, do1_bf, NT_DIMS, preferred_element_type=F32)

        def advance_state(c, hp, slot):
            ebl = jnp.exp(b_s[_chunk_rows(c, CH - SUBLANES, SUBLANES), hp * REC_K:(hp + 1) * REC_K][SUBLANES - 1:, :])
            st, dst = sh_ref[hp, 0, c], dst_ref[hp]
            dst_ref[hp] = dst * ebl + car_dst[slot, hp]
            return ebl * jnp.sum(st * dst, axis=0, keepdims=True)

        def cross(c, hp, slot):
            _, _, q, k, bc, v, do = operands(c, hp)
            qe1, _, ke0, _ = _cross_half(q, k, bc)
            xs = car_x[slot, hp]
            dqe1 = jnp.dot(xs[HALF:2 * HALF].astype(BF16), ke0.astype(BF16), preferred_element_type=F32)
            dke0 = jnp.dot(xs[2 * HALF:].astype(BF16), qe1.astype(BF16), preferred_element_type=F32)
            dv1 = jnp.dot(xs[:HALF].astype(BF16), do[HALF:, :].astype(BF16), preferred_element_type=F32)
            return dqe1, dke0, dv1

        def retire(c, slot):
            dbl_state = [advance_state(c, hp, slot) for hp in range(HP)]
            yield
            crossed = [cross(c, hp, slot) for hp in range(HP)]
            for hp in range(HP):
                finish(c, hp, slot, dbl_state[hp], *crossed[hp])

        def step(c, slot):
            closing = retire(c + 1, slot)
            next(closing)
            for hp in range(HP):
                issue(c, hp, 1 - slot)
            next(closing, None)

        def trip(j, carry):
            step(nck - 2 - 2 * j, 0)
            step(nck - 3 - 2 * j, 1)
            return carry

        def finish(c, hp, slot, dbl_state, dqe1, dke0, dv1):
            rs, cs, q, k, bc, v, do = operands(c, hp)
            eb, ekd = jnp.exp(bc), jnp.exp(bc[CH - 1:CH, :] - bc)
            qe, kd = q * eb, k * ekd
            qe1, e1, ke0, e0 = _cross_half(q, k, bc)
            dqe, dkd, dv = car_r[slot, hp, 0:CH], car_r[slot, hp, CH:2 * CH], car_r[slot, hp, 2 * CH:3 * CH]
            a, da, decs = car_a[slot, hp], car_da[slot, hp], car_dec[slot, hp]
            dec = [decs[_OFF_OF[s]:_OFF_OF[s] + _ROWS_OF[s], :] for s in range(CH)]
            dbl = jnp.sum(dkd * kd, axis=0, keepdims=True) + dbl_state
            dq_acc = [jnp.zeros((SUBLANES, REC_K), F32) for _ in range(ngrp)]
            uk, uv = [], []
            for s in range(CH):
                j = s // SUBLANES
                r0 = j * SUBLANES
                ks = k[s:s + 1, :]
                for jj in range(j, _ROW1[s] // SUBLANES):
                    lo, hi = _OFF_OF[s] + (jj - j) * SUBLANES, _OFF_OF[s] + (jj - j + 1) * SUBLANES
                    a_blk, da_blk = a[lo:hi, :], da[lo:hi, :]
                    if jj == j:
                        keep = sub >= s - r0
                        a_blk, da_blk = jnp.where(keep, a_blk, 0.0), jnp.where(keep, da_blk, 0.0)
                    rows = slice(jj * SUBLANES, (jj + 1) * SUBLANES)
                    tt = da_blk * dec[s][(jj - j) * SUBLANES:(jj - j + 1) * SUBLANES, :]
                    dq_acc[jj] = dq_acc[jj] + tt * ks
                    uk.append(tt * q[rows, :])
                    uv.append(a_blk * do[rows, :])
            dk_in = jnp.dot(pick, jnp.concatenate(uk, axis=0).astype(BF16), preferred_element_type=F32)
            dv_in = jnp.dot(pick, jnp.concatenate(uv, axis=0).astype(BF16), preferred_element_type=F32)
            zero_half = jnp.zeros((HALF, REC_K), F32)
            dq_x = jnp.concatenate([zero_half, dqe1 * e1], axis=0)
            dk_x = jnp.concatenate([dke0 * e0, zero_half], axis=0)
            dv_x = jnp.concatenate([dv1, zero_half], axis=0)
            db_x = jnp.concatenate([-(dke0 * ke0), dqe1 * qe1], axis=0)
            dq_in = jnp.concatenate(dq_acc, axis=0)
            dqv_s[rs, cs] = dqe * eb + dq_in + dq_x
            dk_s[rs, cs] = dkd * ekd + dk_in + dk_x
            di_ref[rs, cs] = (dv + dv_in + dv_x).astype(BF16)
            db = dqe * qe - dkd * kd + q * dq_in - k * dk_in + db_x
            db_s[rs, cs] = db + jnp.where(rowid == CH - 1, dbl, 0.0)

        assert nck % 2 == 0
        for hp in range(HP):
            issue(nck - 1, hp, 0)
        lax.fori_loop(0, nck // 2 - 1, trip, 0)
        step(0, 0)
        for _ in retire(0, 1):
            pass
        up = _tri(False)
        sgq = _sigmoid(qp)
        dq_ref[...] = (dqv_s[...] * (sgq * (1.0 + qp * (1.0 - sgq)))).astype(BF16)
        dlb_acc = jnp.zeros((1, _HW), F32)
        for r in range(0, tb, _CUM_ROWS):
            rows = slice(r, r + _CUM_ROWS)
            dgl = _dot3(up, db_s[rows, :])
            dfg = dgl * jnp.exp(-g[rows, :]) - dk_s[rows, :]
            sn = sig_neg[rows, :]
            df_ref[rows, :] = (dfg * (1.0 - lbv) * (1.0 - sn) * sn).astype(BF16)
            dlb_acc = dlb_acc + jnp.sum(dfg * sn, axis=0, keepdims=True)
        dlb_ref[...] += dlb_acc

    blk, head, lbs, gws, hist = _hgrn_specs(tb, nt, True)
    out_specs = [pl.BlockSpec((tb, REC_IN), lambda h, b, t: (b * nt + nt - 1 - t, 0)), lbs,
                 pl.BlockSpec((HP, 1, REC_K), lambda h, b, t: (h, 0, 0))]
    out_shape = [jax.ShapeDtypeStruct((n, REC_IN), BF16),
                 jax.ShapeDtypeStruct((1, REC_WIDTH), F32), jax.ShapeDtypeStruct((REC_HEADS, 1, REC_K), F32)]
    return pl.pallas_call(
        body, name="hgrn_bwd", grid=(REC_HEADS // HP, b_loc, nt),
        in_specs=[blk(0), blk(1), blk(2), blk(3), lbs, gws, head, head, hist],
        out_specs=out_specs, out_shape=out_shape,
        scratch_shapes=[pltpu.VMEM((tb, _HW), F32)] * 7 + [pltpu.VMEM((HP, REC_K, REC_K), F32)] + [
            pltpu.VMEM((2, HP, 3 * CH, REC_K), F32), pltpu.VMEM((2, HP, REC_K, REC_K), F32)] + [
            pltpu.VMEM((2, HP, sum(_ROWS_OF), REC_K), F32)] * 3 + [pltpu.VMEM((2, HP, 3 * HALF, HALF), F32)],
        compiler_params=_params(("arbitrary", "arbitrary", "arbitrary"), 56),
    )(p, p, p, p, lb, gw, oraw, dg, sh)


def _postnorm_bwd_nt(dxo, y, qw, w, name):
    n = dxo.shape[0]

    def body(dx_ref, y_ref, qw_ref, w_ref, dg_ref, dy_ref, dqw_ref, db_ref):
        @pl.when(pl.program_id(0) == 0)
        def _():
            dqw_ref[...] = jnp.zeros_like(dqw_ref)
            db_ref[...] = jnp.zeros_like(db_ref)

        yv, dxv = y_ref[...], dx_ref[...]
        r = lax.rsqrt(jnp.mean(yv * yv, axis=-1, keepdims=True) + NORM_EPS)
        u = yv * r
        du = dxv * qw_ref[...]
        dy = r * (du - u * jnp.mean(du * u, axis=-1, keepdims=True))
        dqw_ref[...] += jnp.sum(dxv * u, axis=0, keepdims=True)
        db_ref[...] += jnp.sum(dy, axis=0, keepdims=True)
        dyb = dy.astype(BF16)
        dy_ref[...] = dyb
        dg_ref[...] = lax.dot_general(dyb, w_ref[...], NT_DIMS, preferred_element_type=F32)

    rows = pl.BlockSpec((TM, D_MODEL), lambda i: (i, 0))
    const = lambda shape: pl.BlockSpec(shape, lambda i: (0, 0))
    return pl.pallas_call(
        body, name=name, grid=(n // TM,), in_specs=[rows, rows, const((1, D_MODEL)), const((D_MODEL, D_MODEL))],
        out_specs=[rows, rows, const((1, D_MODEL)), const((1, D_MODEL))],
        out_shape=[jax.ShapeDtypeStruct((n, D_MODEL), F32), jax.ShapeDtypeStruct((n, D_MODEL), BF16),
                   jax.ShapeDtypeStruct((1, D_MODEL), F32), jax.ShapeDtypeStruct((1, D_MODEL), F32)],
        compiler_params=_params(("arbitrary",), 48),
    )(dxo, y, qw, w)


def _nt_prenorm_bwd(dps, w, x, pw, dxo, has_bias, name, parts=()):
    n = x.shape[0]
    widths = [d.shape[1] for d in dps]
    m = sum(widths)
    npieces, nparts, steps = len(dps), len(parts), n // TM

    def body(*refs):
        dp_refs = refs[:npieces]
        w_ref, x_ref, pw_ref, dxo_ref = refs[npieces:npieces + 4]
        part_refs = refs[npieces + 4:npieces + 4 + nparts]
        dx_ref, dpw_ref, db_ref = refs[npieces + 4 + nparts:npieces + 7 + nparts]
        land_refs = refs[npieces + 7 + nparts:npieces + 7 + 2 * nparts]
        sems = refs[npieces + 7 + 2 * nparts:]

        @pl.when(pl.program_id(0) == 0)
        def _():
            dpw_ref[...] = jnp.zeros_like(dpw_ref)
            db_ref[...] = jnp.zeros_like(db_ref)
            if nparts:
                _scatter_start(part_refs, land_refs, sems)

        dh = jnp.zeros((TM, D_MODEL), F32)
        off = 0
        for dp_ref, wd in zip(dp_refs, widths):
            cn = _col_chunk(wd)
            for j in range(0, wd, cn):
                dpc = dp_ref[:, j:j + cn]
                if has_bias:
                    db_ref[:, off + j:off + j + cn] += jnp.sum(dpc, axis=0, keepdims=True)
                dh = dh + lax.dot_general(dpc.astype(BF16), w_ref[:, off + j:off + j + cn], NT_DIMS, preferred_element_type=F32)
            off += wd
        xv = x_ref[...]
        r = lax.rsqrt(jnp.mean(xv * xv, axis=-1, keepdims=True) + NORM_EPS)
        xn = xv * r
        dpw_ref[...] += jnp.sum(dh * xn, axis=0, keepdims=True)
        dxn = dh * pw_ref[...]
        dx_ref[...] = dxo_ref[...] + r * (dxn - xn * jnp.mean(dxn * xn, axis=-1, keepdims=True))

        if nparts:
            @pl.when(pl.program_id(0) == steps - 1)
            def _():
                _scatter_wait(part_refs, land_refs, sems)

    rows = pl.BlockSpec((TM, D_MODEL), lambda i: (i, 0))
    const = lambda shape: pl.BlockSpec(shape, lambda i: (0, 0))
    hbm = pl.BlockSpec(memory_space=pl.ANY)
    in_specs = ([pl.BlockSpec((TM, wd), lambda i: (i, 0)) for wd in widths] + [const((D_MODEL, m)), rows, const((1, D_MODEL)), rows]
                + [hbm] * nparts)
    return pl.pallas_call(
        body, name=name, grid=(steps,), in_specs=in_specs,
        out_specs=[rows, const((1, D_MODEL)), const((1, m))] + [hbm] * nparts,
        out_shape=[jax.ShapeDtypeStruct((n, D_MODEL), F32), jax.ShapeDtypeStruct((1, D_MODEL), F32),
                   jax.ShapeDtypeStruct((1, m), F32)] + _scatter_lands(parts),
        scratch_shapes=_scatter_sems(nparts) if nparts else [],
        compiler_params=_params(("arbitrary",), 56),
    )(*dps, w, x, pw, dxo, *parts)


def _matmul_tn(a, b, name):
    n, k = a.shape
    m = b.shape[1]
    tk, tm, tn = k, _col_chunk(m), TN_ROWS if n % TN_ROWS == 0 else n

    def body(a_ref, b_ref, o_ref):
        @pl.when(pl.program_id(2) == 0)
        def _():
            o_ref[...] = jnp.zeros_like(o_ref)

        o_ref[...] += lax.dot_general(a_ref[...], b_ref[...].astype(BF16), TN_DIMS, preferred_element_type=F32)

    return pl.pallas_call(
        body, name=name, grid=(k // tk, m // tm, n // tn),
        in_specs=[pl.BlockSpec((tn, tk), lambda i, j, l: (l, i)), pl.BlockSpec((tn, tm), lambda i, j, l: (l, j))],
        out_specs=pl.BlockSpec((tk, tm), lambda i, j, l: (i, j)),
        out_shape=jax.ShapeDtypeStruct((k, m), F32),
        compiler_params=_params(("arbitrary", "arbitrary", "arbitrary"), 48),
    )(a, b)


def _matmul_tn_by_owner(a, b, name):
    n, k = a.shape
    c = b.shape[1] // N_DEV
    tn = TN_ROWS if n % TN_ROWS == 0 else n
    steps = n // tn
    per = 2

    def body(a_ref, b_ref, o_ref, w_ref):
        @pl.when(pl.program_id(1) == 0)
        def _():
            o_ref[...] = jnp.zeros_like(o_ref)

        r = lax.dot_general(a_ref[...], b_ref[...], TN_DIMS, preferred_element_type=F32)
        for j in range(per):
            o_ref[j] += r[:, j * c:(j + 1) * c]

        @pl.when(pl.program_id(1) == steps - 1)
        def _():
            w_ref[...] = o_ref[...].astype(BF16)

    out = pl.BlockSpec((per, k, c), lambda j, l: (j, 0, 0))
    return pl.pallas_call(
        body, name=name, grid=(N_DEV // per, steps),
        in_specs=[pl.BlockSpec((tn, k), lambda j, l: (l, 0)), pl.BlockSpec((tn, per * c), lambda j, l: (l, j))],
        out_specs=[out, out],
        out_shape=[jax.ShapeDtypeStruct((N_DEV, k, c), F32), jax.ShapeDtypeStruct((N_DEV, k, c), BF16)],
        compiler_params=_params(("arbitrary", "arbitrary"), 48),
    )(a, b)


def _by_owner_cols(dw):
    k, m = dw.shape
    return dw.reshape(k, N_DEV, m // N_DEV).transpose(1, 0, 2)


def _own_and_bf16(part):
    return lax.dynamic_index_in_dim(part, _my_id(), 0, keepdims=False), part.astype(BF16)


def _step(x, ct, st, tgt, pre_w, post_w, wa_in, ba_in, sinks, wa_out_shard, ba_out, wr_in_shard, lb_logits, gnorm_w, wr_out_shard, b_loc, t_len):
    nb = t_len // BLK
    lb = _lower_bound(lb_logits)
    p0, h0, ga_out = _norm_matmul(x, pre_w[0:1], wa_in, ba_in, "attn_in_proj", [wa_out_shard])
    wa_out = ga_out.reshape(ATTN_WIDTH, D_MODEL)
    o0, g0, gr_in = _attn_fwd(p0, ct, st, sinks, b_loc, nb, [wr_in_shard])
    wr_in = gr_in.transpose(1, 0, 2).reshape(D_MODEL, REC_IN)
    y0, x1 = _outproj_postnorm(g0, wa_out, ba_out, x, post_w[0:1], "attn_out_proj")
    p1, h1, gr_out = _norm_matmul(x1, pre_w[1:2], wr_in, None, "rec_in_proj", [wr_out_shard])
    wr_out = gr_out.reshape(REC_WIDTH, D_MODEL)
    o1, g1, sh = _hgrn_fwd(p1, lb, gnorm_w, b_loc, t_len)
    dx2, dg1, dy1, dpost1, loss_tile = _outproj_loss_bwd(g1, wr_out, x1, post_w[1:2], tgt, "rec_out_proj_loss_bwd")
    d_wr_out = _matmul_tn(g1, dy1, "rec_w_out_grad")
    dp1, dlb, dgw = _hgrn_bwd(p1, lb, gnorm_w, o1, sh, dg1, b_loc, t_len)
    dx1, dpre1, _ = _nt_prenorm_bwd([dp1], wr_in, x1, pre_w[1:2], dx2, False, "rec_in_bwd")
    part_r_in, wire_r_in = _matmul_tn_by_owner(h1, dp1, "rec_w_in_grad")
    own_r_in = lax.dynamic_index_in_dim(part_r_in, _my_id(), 0, keepdims=False)
    dg0, dy0, dpost0, dba_out = _postnorm_bwd_nt(dx1, y0, post_w[0:1], wa_out, "attn_out_bwd")
    d_wa_out = _matmul_tn(g0, dy0, "attn_w_out_grad")
    owns, wires = zip(*[_own_and_bf16(part) for part in (
        d_wr_out.reshape(N_DEV, REC_WIDTH // N_DEV, D_MODEL), d_wa_out.reshape(N_DEV, ATTN_WIDTH // N_DEV, D_MODEL))])
    owns, wires = (own_r_in,) + owns, (wire_r_in,) + wires
    dp0, dsink_tile, *lands = _attn_bwd(p0, ct, st, sinks, o0, dg0, b_loc, nb, list(wires))
    d_wa_in = _matmul_tn(h0, dp0, "attn_w_in_grad")
    own_a_in, wire_a_in = _own_and_bf16(_by_owner_cols(_qkvz(d_wa_in)))
    dx0, dpre0, dba_in, land_a_in = _nt_prenorm_bwd([dp0], wa_in, x, pre_w[0:1], dx1, True, "attn_in_bwd", [wire_a_in])
    small = dict(pre=jnp.concatenate([dpre0, dpre1], axis=0), post=jnp.concatenate([dpost0, dpost1], axis=0),
                 ba_in=dba_in, sinks=dsink_tile[0:1, 0:N_HEADS], ba_out=dba_out, lb=dlb, gnorm=jnp.sum(dgw, axis=0))
    return loss_tile, dx0, list(zip(lands, owns)) + [(land_a_in, own_a_in)], small


def _my_id():
    return lax.axis_index("x") * 4 + lax.axis_index("y") * 2 + lax.axis_index("c")


def _peer(k):
    x, y, c = lax.axis_index("x"), lax.axis_index("y"), lax.axis_index("c")
    return (x ^ ((k >> 2) & 1), y ^ ((k >> 1) & 1), c ^ (k & 1))


def _peer_id(k):
    return _my_id() ^ k


def _all_gather_by_chip(shard, pos_col):
    n = pos_col.shape[0]

    def body(x_ref, p_ref, f_ref, s_ref, out_ref, ct_ref, st_ref, send_sems, recv_sems, local_sem):
        x, y, c = lax.axis_index("x"), lax.axis_index("y"), lax.axis_index("c")
        me, sibling = (x, y, c), (x, y, 1 - c)
        chips = [(1 - x, y), (x, 1 - y), (1 - x, 1 - y)]

        def rows(px, py, pc):
            return out_ref.at[4 * px + 2 * py + pc]

        def copy(k, block, to, src=None):
            return pltpu.make_async_remote_copy(src_ref=rows(*block) if src is None else src, dst_ref=rows(*block),
                                                send_sem=send_sems.at[k], recv_sem=recv_sems.at[k], device_id=to, device_id_type=MESH)

        mine = pltpu.make_async_copy(x_ref, rows(*me), local_sem)
        mine.start()
        first = [copy(0, me, sibling, src=x_ref)] + [copy(1 + j, me, (*chip, c), src=x_ref) for j, chip in enumerate(chips)]
        for cp in first:
            cp.start()
        _rope_tables_into(p_ref, f_ref, s_ref, ct_ref, st_ref)
        passed = [copy(4 + j, (*chip, c), sibling) for j, chip in enumerate(chips)]
        for j, chip in enumerate(chips):
            copy(1 + j, (*chip, c), me).wait_recv()
            passed[j].start()
        copy(0, sibling, me).wait_recv()
        for j, chip in enumerate(chips):
            copy(4 + j, (*chip, 1 - c), me).wait_recv()
        for cp in first + passed:
            cp.wait_send()
        mine.wait()

    hbm, vmem = pl.BlockSpec(memory_space=pl.ANY), pl.BlockSpec(memory_space=pltpu.VMEM)
    return pl.pallas_call(
        body, name="comm_all_gather_by_chip", in_specs=[hbm, vmem, vmem, vmem], out_specs=[hbm, vmem, vmem],
        out_shape=[jax.ShapeDtypeStruct((N_DEV,) + shard.shape, shard.dtype)] + [jax.ShapeDtypeStruct((n, LANES), F32)] * 2,
        scratch_shapes=[pltpu.SemaphoreType.DMA((N_DEV - 1,)), pltpu.SemaphoreType.DMA((N_DEV - 1,)), pltpu.SemaphoreType.DMA],
        compiler_params=_params(None, 32),
    )(shard, pos_col, *_rope_lanes())


def _gather_shapes(shards):
    return [jax.ShapeDtypeStruct((N_DEV,) + s.shape, s.dtype) for s in shards]


def _gather_sems(nsh):
    return [pltpu.SemaphoreType.DMA((nsh, N_DEV - 1)), pltpu.SemaphoreType.DMA((nsh, N_DEV - 1)), pltpu.SemaphoreType.DMA((nsh,))]


def _gather_copies(ins, outs, sems, received):
    send_sems, recv_sems, local_sems = sems
    me = _my_id()
    local = [pltpu.make_async_copy(ins[a], outs[a].at[me], local_sems.at[a]) for a in range(len(ins))]
    remote = [pltpu.make_async_remote_copy(
        src_ref=ins[a], dst_ref=outs[a].at[_peer_id(k) if received else me], send_sem=send_sems.at[a, k - 1],
        recv_sem=recv_sems.at[a, k - 1], device_id=_peer(k), device_id_type=MESH)
        for a in range(len(ins)) for k in range(1, N_DEV)]
    return local, remote


def _gather_start(ins, outs, sems):
    local, sends = _gather_copies(ins, outs, sems, False)
    for cp in local + sends:
        cp.start()


def _gather_wait(ins, outs, sems):
    local, recvs = _gather_copies(ins, outs, sems, True)
    for cp in recvs:
        cp.wait_recv()
    for cp in recvs:
        cp.wait_send()
    for cp in local:
        cp.wait()


def _scatter_lands(parts):
    return [jax.ShapeDtypeStruct((N_DEV - 1,) + p.shape[1:], p.dtype) for p in parts]


def _scatter_sems(nparts):
    return [pltpu.SemaphoreType.DMA((nparts, N_DEV - 1)), pltpu.SemaphoreType.DMA((nparts, N_DEV - 1))]


def _scatter_copies(parts, lands, sems):
    send_sems, recv_sems = sems
    return [pltpu.make_async_remote_copy(
        src_ref=parts[a].at[_peer_id(k)], dst_ref=lands[a].at[k - 1], send_sem=send_sems.at[a, k - 1],
        recv_sem=recv_sems.at[a, k - 1], device_id=_peer(k), device_id_type=MESH)
        for a in range(len(parts)) for k in range(1, N_DEV)]


def _scatter_start(parts, lands, sems):
    for cp in _scatter_copies(parts, lands, sems):
        cp.start()


def _scatter_wait(parts, lands, sems):
    copies = _scatter_copies(parts, lands, sems)
    for cp in copies:
        cp.wait_recv()
    for cp in copies:
        cp.wait_send()


def _adamw(w, g, m, v):
    m2 = ADAM_B1 * m + (1.0 - ADAM_B1) * g
    v2 = ADAM_B2 * v + (1.0 - ADAM_B2) * (g * g)
    m_hat = m2 / (1.0 - ADAM_B1 ** ADAM_STEP)
    v_hat = v2 / (1.0 - ADAM_B2 ** ADAM_STEP)
    delta = -ADAM_LR * (m_hat / (jnp.sqrt(v_hat) + ADAM_EPS) + ADAM_WD * w)
    return delta, m2, v2


def _sum_adamw(land, own, w, m, v, name):
    r, c = own.shape
    rb = min(r, 256)

    def body(land_ref, own_ref, w_ref, m_ref, v_ref, g_ref, d_ref, m2_ref, v2_ref):
        me = _my_id()
        g = jnp.zeros((rb, c), F32)
        for dev in range(N_DEV):
            k = dev ^ me
            g = g + jnp.where(k == 0, own_ref[...], land_ref[jnp.maximum(k - 1, 0)].astype(F32))
        delta, m2, v2 = _adamw(w_ref[...], g, m_ref[...], v_ref[...])
        g_ref[...] = g
        d_ref[...] = delta
        m2_ref[...] = m2
        v2_ref[...] = v2

    rows = pl.BlockSpec((rb, c), lambda i: (i, 0))
    return pl.pallas_call(
        body, name=name, grid=(r // rb,), in_specs=[pl.BlockSpec((N_DEV - 1, rb, c), lambda i: (0, i, 0))] + [rows] * 4,
        out_specs=[rows] * 4, out_shape=[jax.ShapeDtypeStruct((r, c), F32)] * 4,
        compiler_params=_params(("arbitrary",), 32),
    )(land, own, w, m, v)


_SMALL = [("pre_norm_w", 2048), ("post_norm_w", 2048), ("attn_b_in", 2304), ("attn_sinks", 16), ("attn_b_out", 1024),
          ("rec_lb_logits", 2048), ("rec_gnorm_w", 128), ("loss", 1)]
_TILE = SUBLANES * LANES


def _small_rows(size):
    return -(-size // _TILE) * SUBLANES


_SMALL_OFF = {}
_r = 0
for _name, _size in _SMALL:
    _SMALL_OFF[_name] = _r
    _r += _small_rows(_size)
_SMALL_ROWS = _r


def _pack_small(pieces):
    out = []
    for name, size in _SMALL:
        flat = pieces[name].reshape(-1).astype(F32)
        out.append(jnp.pad(flat, (0, _small_rows(size) * LANES - size)).reshape(-1, LANES))
    return jnp.concatenate(out, axis=0)


def _unpack_small(packed, shapes):
    return {name: packed[_SMALL_OFF[name]:_SMALL_OFF[name] + _small_rows(size)].reshape(-1)[:size].reshape(shapes[name])
            for name, size in _SMALL}


def _small_allreduce_adamw(gpart, w, m, v):
    lb0 = _SMALL_OFF["rec_lb_logits"]

    def body(gp_ref, w_ref, m_ref, v_ref, g_ref, d_ref, m2_ref, v2_ref, land_ref, send_sems, recv_sems):
        me = _my_id()
        sent = []
        for k in range(1, N_DEV):
            cp = pltpu.make_async_remote_copy(src_ref=gp_ref, dst_ref=land_ref.at[k - 1], send_sem=send_sems.at[k - 1],
                                              recv_sem=recv_sems.at[k - 1], device_id=_peer(k), device_id_type=MESH)
            cp.start()
            sent.append(cp)
        for cp in sent:
            cp.wait_recv()
        for cp in sent:
            cp.wait_send()
        g = jnp.zeros((_SMALL_ROWS, LANES), F32)
        for dev in range(N_DEV):
            k = dev ^ me
            g = g + jnp.where(k == 0, gp_ref[...], land_ref[jnp.maximum(k - 1, 0)])
        g_ref[...] = g
        l0, l1 = w_ref[lb0:lb0 + SUBLANES, :], w_ref[lb0 + SUBLANES:lb0 + 2 * SUBLANES, :]
        mx = jnp.maximum(l0, l1)
        e0, e1 = jnp.exp(l0 - mx), jnp.exp(l1 - mx)
        p1 = e1 / (e0 + e1)
        dl1 = (1.0 - p1) * p1 * g[lb0:lb0 + SUBLANES, :]
        g_ref[lb0:lb0 + SUBLANES, :] = -dl1
        g_ref[lb0 + SUBLANES:lb0 + 2 * SUBLANES, :] = dl1
        delta, m2, v2 = _adamw(w_ref[...], g_ref[...], m_ref[...], v_ref[...])
        d_ref[...] = delta
        m2_ref[...] = m2
        v2_ref[...] = v2

    vmem = pl.BlockSpec(memory_space=pltpu.VMEM)
    return pl.pallas_call(
        body, name="comm_small_allreduce_adamw", in_specs=[vmem] * 4, out_specs=[vmem] * 4,
        out_shape=[jax.ShapeDtypeStruct((_SMALL_ROWS, LANES), F32)] * 4,
        scratch_shapes=[pltpu.VMEM((N_DEV - 1, _SMALL_ROWS, LANES), F32), pltpu.SemaphoreType.DMA((N_DEV - 1,)),
                        pltpu.SemaphoreType.DMA((N_DEV - 1,))],
    )(gpart, w, m, v)


def _qzkv(a):
    kv_end = ATTN_WIDTH + 2 * KV_WIDTH
    return jnp.concatenate([a[..., :ATTN_WIDTH], a[..., kv_end:], a[..., ATTN_WIDTH:kv_end]], axis=-1)


def _qkvz(a):
    return jnp.concatenate([a[..., :ATTN_WIDTH], a[..., 2 * ATTN_WIDTH:], a[..., ATTN_WIDTH:2 * ATTN_WIDTH]], axis=-1)


def kernel(x, positions, pre_norm_w, post_norm_w, attn_w_in, attn_b_in, attn_sinks, attn_w_out, attn_b_out, rec_w_in, rec_lb_logits, rec_gnorm_w, rec_w_out, loss_target, m_pre_norm_w, m_post_norm_w, m_attn_w_in, m_attn_b_in, m_attn_sinks, m_attn_w_out, m_attn_b_out, m_rec_w_in, m_rec_lb_logits, m_rec_gnorm_w, m_rec_w_out, v_pre_norm_w, v_post_norm_w, v_attn_w_in, v_attn_b_in, v_attn_sinks, v_attn_w_out, v_attn_b_out, v_rec_w_in, v_rec_lb_logits, v_rec_gnorm_w, v_rec_w_out):
    b_loc, t_len, _ = x.shape
    n = b_loc * t_len
    ga_in, ct, st = _all_gather_by_chip(attn_w_in[0].astype(BF16), positions.reshape(n, 1).astype(F32))
    wa_in = _qzkv(ga_in.transpose(1, 0, 2).reshape(D_MODEL, ATTN_IN))

    loss_tile, dx, landed, small = _step(
        x.reshape(n, D_MODEL), ct, st, loss_target.reshape(n, D_MODEL),
        pre_norm_w, post_norm_w, wa_in, _qzkv(attn_b_in), attn_sinks, attn_w_out[0].astype(BF16), attn_b_out,
        rec_w_in[0].astype(BF16), rec_lb_logits, rec_gnorm_w, rec_w_out[0].astype(BF16), b_loc, t_len)

    lift = lambda outs: tuple(a[None] for a in outs)
    (l_r_in, o_r_in), (l_r_out, o_r_out), (l_a_out, o_a_out), (l_a_in, o_a_in) = landed
    r_a_in = lift(_sum_adamw(l_a_in, o_a_in, attn_w_in[0], m_attn_w_in[0], v_attn_w_in[0], "adamw_attn_w_in"))
    r_r_in = lift(_sum_adamw(l_r_in, o_r_in, rec_w_in[0], m_rec_w_in[0], v_rec_w_in[0], "adamw_rec_w_in"))
    r_r_out = lift(_sum_adamw(l_r_out, o_r_out, rec_w_out[0], m_rec_w_out[0], v_rec_w_out[0], "adamw_rec_w_out"))
    r_a_out = lift(_sum_adamw(l_a_out, o_a_out, attn_w_out[0], m_attn_w_out[0], v_attn_w_out[0], "adamw_attn_w_out"))

    gsmall = dict(pre_norm_w=small["pre"], post_norm_w=small["post"], attn_b_in=_qkvz(small["ba_in"]), attn_sinks=small["sinks"],
                  attn_b_out=small["ba_out"], rec_lb_logits=jnp.concatenate([small["lb"], jnp.zeros_like(small["lb"])], axis=0),
                  rec_gnorm_w=small["gnorm"], loss=loss_tile[0:1, 0:1])
    nil = jnp.zeros((1, 1), F32)
    wsmall = dict(pre_norm_w=pre_norm_w, post_norm_w=post_norm_w, attn_b_in=attn_b_in, attn_sinks=attn_sinks,
                  attn_b_out=attn_b_out, rec_lb_logits=rec_lb_logits, rec_gnorm_w=rec_gnorm_w, loss=nil)
    msmall = dict(pre_norm_w=m_pre_norm_w, post_norm_w=m_post_norm_w, attn_b_in=m_attn_b_in, attn_sinks=m_attn_sinks,
                  attn_b_out=m_attn_b_out, rec_lb_logits=m_rec_lb_logits, rec_gnorm_w=m_rec_gnorm_w, loss=nil)
    vsmall = dict(pre_norm_w=v_pre_norm_w, post_norm_w=v_post_norm_w, attn_b_in=v_attn_b_in, attn_sinks=v_attn_sinks,
                  attn_b_out=v_attn_b_out, rec_lb_logits=v_rec_lb_logits, rec_gnorm_w=v_rec_gnorm_w, loss=nil)
    shapes = {k: a.shape for k, a in wsmall.items()}
    packed = _small_allreduce_adamw(_pack_small(gsmall), _pack_small(wsmall), _pack_small(msmall), _pack_small(vsmall))
    sg, sd, sm, sv = [_unpack_small(a, shapes) for a in packed]

    big = {"attn_w_in": r_a_in, "attn_w_out": r_a_out, "rec_w_in": r_r_in, "rec_w_out": r_r_out}
    order = ["pre_norm_w", "post_norm_w", "attn_w_in", "attn_b_in", "attn_sinks", "attn_w_out", "attn_b_out", "rec_w_in",
             "rec_lb_logits", "rec_gnorm_w", "rec_w_out"]
    outs = [sg["loss"][0, 0], dx.reshape(b_loc, t_len, D_MODEL)]
    for idx, small_set in enumerate((sg, sd, sm, sv)):
        outs += [big[nm][idx] if nm in big else small_set[nm] for nm in order]
    return tuple(outs)
```

```python
import numpy as np
import jax
import jax.numpy as jnp
from jax import lax
from jax.experimental import pallas as pl
from jax.experimental.pallas import tpu as pltpu

F32, BF16 = jnp.float32, jnp.bfloat16
MESH = pl.DeviceIdType.MESH
N_DEV = 8

D_MODEL = 1024
N_HEADS, HEAD_DIM, N_KV, GROUP = 16, 64, 2, 8
ATTN_WIDTH, KV_WIDTH = 1024, 128
ATTN_IN = 2 * ATTN_WIDTH + 2 * KV_WIDTH
BLK = 128
ROPE_THETA, ROPE_HALF = 500000.0, 8
REC_HEADS, REC_K = 8, 128
REC_WIDTH = REC_HEADS * REC_K
REC_IN = 4 * REC_WIDTH
TN_ROWS = 1024
CH = 32
NORM_EPS = 1e-6
F32_TINY = 1.1754944e-38
ADAM_LR, ADAM_B1, ADAM_B2, ADAM_EPS, ADAM_WD, ADAM_STEP = 0.001, 0.9, 0.999, 1e-08, 0.01, 10

LANES, SUBLANES = 128, 8
TM = 512
NT_DIMS = (((1,), (1,)), ((), ()))
TN_DIMS = (((0,), (0,)), ((), ()))
MB = 2 ** 20


def _params(sem=None, vmem_mb=48, **kw):
    return pltpu.CompilerParams(dimension_semantics=sem, vmem_limit_bytes=vmem_mb * MB, **kw)


def _col_chunk(m):
    return 768 if m % 1024 else 1024


def _sigmoid(x):
    return 1.0 / (1.0 + jnp.exp(-x))


def _split3(x):
    hi = x.astype(BF16)
    r1 = x - hi.astype(F32)
    mid = r1.astype(BF16)
    lo = (r1 - mid.astype(F32)).astype(BF16)
    return hi, mid, lo


def _dot3(l_bf, x):
    hi, mid, lo = _split3(x)
    return (jnp.dot(l_bf, hi, preferred_element_type=F32) + jnp.dot(l_bf, mid, preferred_element_type=F32)
            + jnp.dot(l_bf, lo, preferred_element_type=F32))


def _rope_lanes():
    lane = np.arange(LANES) % HEAD_DIM
    inv = np.float32(ROPE_THETA) ** (-(np.arange(ROPE_HALF, dtype=np.float32) * np.float32(2.0) / np.float32(2 * ROPE_HALF)))
    freq = np.where(lane < 2 * ROPE_HALF, inv[lane % ROPE_HALF], 0.0).astype(np.float32)[None, :]
    sign = np.where(lane < ROPE_HALF, -1.0, np.where(lane < 2 * ROPE_HALF, 1.0, 0.0)).astype(np.float32)[None, :]
    return jnp.asarray(freq), jnp.asarray(sign)


def _rope_tables_into(p_ref, f_ref, s_ref, c_out, s_out):
    def rows(i, carry):
        rs = pl.ds(pl.multiple_of(i * TM, TM), TM)
        ang = p_ref[rs, :] * f_ref[...]
        c_out[rs, :] = jnp.cos(ang)
        s_out[rs, :] = jnp.sin(ang) * s_ref[...]
        return carry

    lax.fori_loop(0, p_ref.shape[0] // TM, rows, 0)


def _rope_apply(xv, c, s, lm):
    partner = jnp.where(lm < ROPE_HALF, pltpu.roll(xv, LANES - ROPE_HALF, 1), pltpu.roll(xv, ROPE_HALF, 1))
    return xv * c + partner * s


def _rope_bwd(dy, c, s, lm):
    t = dy * s
    partner = jnp.where(lm < ROPE_HALF, pltpu.roll(t, LANES - ROPE_HALF, 1),
                        jnp.where(lm < 2 * ROPE_HALF, pltpu.roll(t, ROPE_HALF, 1), 0.0))
    return dy * c + partner


def _lower_bound(lb_logits):
    def body(l_ref, o_ref):
        l0, l1 = l_ref[0:1, :], l_ref[1:2, :]
        m = jnp.maximum(l0, l1)
        e0, e1 = jnp.exp(l0 - m), jnp.exp(l1 - m)
        o_ref[...] = e1 / (e0 + e1)

    return pl.pallas_call(body, name="lower_bound", out_shape=jax.ShapeDtypeStruct((1, lb_logits.shape[1]), F32))(lb_logits)


def _norm_matmul(x, pw, w, bias, name, shards=()):
    n, m = x.shape[0], w.shape[1]
    cn = _col_chunk(m)
    has_bias = bias is not None
    nsh, steps = len(shards), n // TM

    def body(*refs):
        refs = list(refs)
        x_ref, pw_ref, w_ref = refs[:3]
        b_ref = refs[3] if has_bias else None
        refs = refs[4 if has_bias else 3:]
        sh_in, (p_ref, h_ref), sh_out, sems = refs[:nsh], refs[nsh:nsh + 2], refs[nsh + 2:2 * nsh + 2], refs[2 * nsh + 2:]
        if nsh:
            @pl.when(pl.program_id(0) == 0)
            def _():
                _gather_start(sh_in, sh_out, sems)

        xv = x_ref[...]
        r = lax.rsqrt(jnp.mean(xv * xv, axis=-1, keepdims=True) + NORM_EPS)
        h = ((xv * r) * pw_ref[...]).astype(BF16)
        h_ref[...] = h
        for j in range(0, m, cn):
            acc = jnp.dot(h, w_ref[:, j:j + cn], preferred_element_type=F32)
            if has_bias:
                acc = acc + b_ref[:, j:j + cn]
            p_ref[:, j:j + cn] = acc

        if nsh:
            @pl.when(pl.program_id(0) == steps - 1)
            def _():
                _gather_wait(sh_in, sh_out, sems)

    rows = pl.BlockSpec((TM, D_MODEL), lambda i: (i, 0))
    const = lambda shape: pl.BlockSpec(shape, lambda i: (0, 0))
    hbm = pl.BlockSpec(memory_space=pl.ANY)
    in_specs = [rows, const((1, D_MODEL)), const((D_MODEL, m))] + ([const((1, m))] if has_bias else []) + [hbm] * nsh
    args = (x, pw, w) + ((bias,) if has_bias else ()) + tuple(shards)
    return pl.pallas_call(
        body, name=name, grid=(steps,), in_specs=in_specs,
        out_specs=[pl.BlockSpec((TM, m), lambda i: (i, 0)), rows] + [hbm] * nsh,
        out_shape=[jax.ShapeDtypeStruct((n, m), F32), jax.ShapeDtypeStruct((n, D_MODEL), BF16)] + _gather_shapes(shards),
        scratch_shapes=_gather_sems(nsh) if nsh else [],
        compiler_params=_params(("arbitrary",), 56),
    )(*args)


def _outproj_postnorm(g, w, bias, xres, qw, name):
    n = g.shape[0]

    def body(g_ref, w_ref, b_ref, x_ref, qw_ref, y_ref, o_ref):
        y = jnp.dot(g_ref[...], w_ref[...], preferred_element_type=F32) + b_ref[...]
        y_ref[...] = y
        r = lax.rsqrt(jnp.mean(y * y, axis=-1, keepdims=True) + NORM_EPS)
        o_ref[...] = x_ref[...] + (y * r) * qw_ref[...]

    rows = pl.BlockSpec((TM, D_MODEL), lambda i: (i, 0))
    const = lambda shape: pl.BlockSpec(shape, lambda i: (0, 0))
    return pl.pallas_call(
        body, name=name, grid=(n // TM,),
        in_specs=[rows, const((D_MODEL, D_MODEL)), const((1, D_MODEL)), rows, const((1, D_MODEL))],
        out_specs=[rows, rows], out_shape=[jax.ShapeDtypeStruct((n, D_MODEL), F32)] * 2,
        compiler_params=_params(("arbitrary",), 48),
    )(g, w, bias, xres, qw)


def _outproj_loss_bwd(g, w, xres, qw, tgt, name):
    n = g.shape[0]
    steps = n // TM

    def body(g_ref, w_ref, x_ref, qw_ref, t_ref, dx_ref, dg_ref, dy_ref, dqw_ref, loss_ref, acc_ref):
        i = pl.program_id(0)

        @pl.when(i == 0)
        def _():
            acc_ref[...] = jnp.zeros_like(acc_ref)
            dqw_ref[...] = jnp.zeros_like(dqw_ref)

        y = jnp.dot(g_ref[...], w_ref[...], preferred_element_type=F32)
        r = lax.rsqrt(jnp.mean(y * y, axis=-1, keepdims=True) + NORM_EPS)
        u = y * r
        e = (x_ref[...] + u * qw_ref[...]) - t_ref[...]
        dxn = e * (1.0 / D_MODEL)
        dx_ref[...] = dxn
        acc_ref[...] += jnp.sum(e * e, axis=0, keepdims=True)
        du = dxn * qw_ref[...]
        dy = (r * (du - u * jnp.mean(du * u, axis=-1, keepdims=True))).astype(BF16)
        dqw_ref[...] += jnp.sum(dxn * u, axis=0, keepdims=True)
        dy_ref[...] = dy
        dg_ref[...] = lax.dot_general(dy, w_ref[...], NT_DIMS, preferred_element_type=F32)

        @pl.when(i == steps - 1)
        def _():
            loss_ref[...] = jnp.full(loss_ref.shape, jnp.sum(acc_ref[...]) * (0.5 / D_MODEL), F32)

    rows = pl.BlockSpec((TM, D_MODEL), lambda i: (i, 0))
    const = lambda shape: pl.BlockSpec(shape, lambda i: (0, 0))
    return pl.pallas_call(
        body, name=name, grid=(steps,),
        in_specs=[rows, const((D_MODEL, D_MODEL)), rows, const((1, D_MODEL)), rows],
        out_specs=[rows, rows, rows, const((1, D_MODEL)), const((SUBLANES, LANES))],
        out_shape=[jax.ShapeDtypeStruct((n, D_MODEL), F32), jax.ShapeDtypeStruct((n, D_MODEL), F32),
                   jax.ShapeDtypeStruct((n, D_MODEL), BF16), jax.ShapeDtypeStruct((1, D_MODEL), F32),
                   jax.ShapeDtypeStruct((SUBLANES, LANES), F32)],
        scratch_shapes=[pltpu.VMEM((1, D_MODEL), F32)], compiler_params=_params(("arbitrary",), 48),
    )(g, w, xres, qw, tgt)


_QCOL, _ZCOL, _KCOL, _VCOL = 0, 1024, 2048, 2176


def _head_stack(chunks, heads, lt64):
    return jnp.concatenate([jnp.where(lt64 if n % 2 == 0 else ~lt64, chunks[n // 2], 0.0) for n in heads], axis=0)


def _dup_half(x, h, lt64):
    r = pltpu.roll(x, HEAD_DIM, 1)
    return jnp.where(lt64, x, r) if h == 0 else jnp.where(lt64, r, x)


def _pair_chunk(xt, c2):
    a, b = 2 * c2, 2 * c2 + 1
    return jnp.concatenate([xt[:HEAD_DIM, a * BLK:(a + 1) * BLK], xt[HEAD_DIM:, b * BLK:(b + 1) * BLK]], axis=0).T


def _attn_mask_t(i):
    key = lax.broadcasted_iota(jnp.int32, (2 * BLK, BLK), 0)
    qry = lax.broadcasted_iota(jnp.int32, (2 * BLK, BLK), 1)
    valid = (key > qry) & (key <= qry + BLK) & ((key >= BLK) | (i > 0))
    return jnp.tile(jnp.where(valid, 0.0, -1e30), (1, GROUP))


def _attn_probs_t(s, heads, sink_ref, mask):
    s = s + mask
    head = lax.broadcasted_iota(jnp.int32, (1, len(heads) * BLK), 1) >> 7
    sk = jnp.zeros((1, len(heads) * BLK), F32)
    for j, n in enumerate(heads):
        sk = jnp.where(head == j, sink_ref[0, n], sk)
    m = jnp.maximum(jnp.max(s, axis=0, keepdims=True), sk)
    p = jnp.exp(s - m)
    esk = jnp.exp(sk - m)
    return p, 1.0 / (jnp.sum(p, axis=0, keepdims=True) + esk), esk


def _attn_fwd(p, ct, st, sinks, b_loc, nb, shards):
    n = p.shape[0]
    nsh = len(shards)

    def body(sink_ref, q_ref, z_ref, kc_ref, kp_ref, vc_ref, vp_ref, cc_ref, sc_ref, cp_ref, sp_ref, *rest):
        sh_in, (o_ref, g_ref), sh_out, sems = rest[:nsh], rest[nsh:nsh + 2], rest[nsh + 2:2 * nsh + 2], rest[2 * nsh + 2:]
        b, i = pl.program_id(0), pl.program_id(1)

        @pl.when((b == 0) & (i == 0))
        def _():
            _gather_start(sh_in, sh_out, sems)

        lane = lax.broadcasted_iota(jnp.int32, (BLK, LANES), 1)
        lm = lane & (HEAD_DIM - 1)
        cc, sc = cc_ref[...], sc_ref[...]
        kcat = jnp.concatenate([_rope_apply(kp_ref[...], cp_ref[...], sp_ref[...], lm),
                                _rope_apply(kc_ref[...], cc, sc, lm)], axis=0)
        vcat = jnp.concatenate([vp_ref[...], vc_ref[...]], axis=0)
        qr = [_rope_apply(q_ref[:, c * LANES:(c + 1) * LANES], cc, sc, lm) * (HEAD_DIM ** -0.5) for c in range(8)]
        valid = _attn_mask_t(i)
        lt64, lt64k = lane < HEAD_DIM, lax.broadcasted_iota(jnp.int32, (2 * BLK, LANES), 1) < HEAD_DIM
        for h in range(N_KV):
            heads = list(range(h * GROUP, (h + 1) * GROUP))
            kext, vext = _dup_half(kcat, h, lt64k).astype(BF16), _dup_half(vcat, h, lt64k).astype(BF16)
            qst = _head_stack(qr, heads, lt64).astype(BF16)
            p, inv, _ = _attn_probs_t(lax.dot_general(kext, qst, NT_DIMS, preferred_element_type=F32), heads, sink_ref, valid)
            ot = lax.dot_general(vext, p.astype(BF16), TN_DIMS, preferred_element_type=F32) * inv
            for c2 in range(GROUP // 2):
                oc = _pair_chunk(ot, c2)
                cols = slice((4 * h + c2) * LANES, (4 * h + c2 + 1) * LANES)
                zc = z_ref[:, cols]
                o_ref[:, cols] = oc
                g_ref[:, cols] = (oc * (zc * _sigmoid(zc))).astype(BF16)

        @pl.when((b == b_loc - 1) & (i == nb - 1))
        def _():
            _gather_wait(sh_in, sh_out, sems)

    cur = lambda b, i: b * nb + i
    prev = lambda b, i: b * nb + jnp.maximum(i - 1, 0)
    wide = lambda cb: pl.BlockSpec((BLK, ATTN_WIDTH), lambda b, i: (cur(b, i), cb))
    kv = lambda rowf, cb: pl.BlockSpec((BLK, LANES), lambda b, i: (rowf(b, i), cb))
    hbm = pl.BlockSpec(memory_space=pl.ANY)
    in_specs = [pl.BlockSpec(memory_space=pltpu.SMEM), wide(0), wide(1),
                kv(cur, _KCOL // LANES), kv(prev, _KCOL // LANES), kv(cur, _VCOL // LANES), kv(prev, _VCOL // LANES),
                kv(cur, 0), kv(cur, 0), kv(prev, 0), kv(prev, 0)] + [hbm] * nsh
    return pl.pallas_call(
        body, name="attn_fwd", grid=(b_loc, nb), in_specs=in_specs, out_specs=[wide(0), wide(0)] + [hbm] * nsh,
        out_shape=[jax.ShapeDtypeStruct((n, ATTN_WIDTH), F32), jax.ShapeDtypeStruct((n, ATTN_WIDTH), BF16)] + _gather_shapes(shards),
        scratch_shapes=_gather_sems(nsh), compiler_params=_params(("arbitrary", "arbitrary"), 48),
    )(sinks, p, p, p, p, p, p, ct, st, ct, st, *shards)


def _attn_bwd(p, ct, st, sinks, o, dg, b_loc, nb, parts):
    n = p.shape[0]
    nparts = len(parts)

    def body(sink_ref, q_ref, z_ref, kc_ref, kp_ref, vc_ref, vp_ref, cc_ref, sc_ref, cp_ref, sp_ref, o_ref, dg_ref, *rest):
        part_refs, (dp_ref, ds_ref), land_refs = rest[:nparts], rest[nparts:nparts + 2], rest[nparts + 2:2 * nparts + 2]
        dq_s, dz_s, dk_s, dv_s = rest[2 * nparts + 2:2 * nparts + 6]
        sems = rest[2 * nparts + 6:]
        b, i = pl.program_id(0), pl.program_id(1)

        @pl.when((b == 0) & (i == 0))
        def _():
            _scatter_start(part_refs, land_refs, sems)

        @pl.when((b == b_loc - 1) & (i == nb))
        def _():
            _scatter_wait(part_refs, land_refs, sems)

        lane = lax.broadcasted_iota(jnp.int32, (BLK, LANES), 1)
        lm = lane & (HEAD_DIM - 1)

        @pl.when((b == 0) & (i == 0))
        def _():
            ds_ref[...] = jnp.zeros_like(ds_ref)

        @pl.when(i < nb)
        def _compute():
            cc, sc = cc_ref[...], sc_ref[...]
            kcat = jnp.concatenate([_rope_apply(kp_ref[...], cp_ref[...], sp_ref[...], lm),
                                    _rope_apply(kc_ref[...], cc, sc, lm)], axis=0)
            vcat = jnp.concatenate([vp_ref[...], vc_ref[...]], axis=0)
            qr = [_rope_apply(q_ref[:, c * LANES:(c + 1) * LANES], cc, sc, lm) * (HEAD_DIM ** -0.5) for c in range(8)]
            valid = _attn_mask_t(i)
            lt64, lt64k = lane < HEAD_DIM, lax.broadcasted_iota(jnp.int32, (2 * BLK, LANES), 1) < HEAD_DIM
            do_chunks, doo_chunks, dz_chunks = [], [], []
            for c in range(8):
                cols = slice(c * LANES, (c + 1) * LANES)
                zc, oc, dgc = z_ref[:, cols], o_ref[:, cols], dg_ref[:, cols]
                sg = _sigmoid(zc)
                do_chunks.append(dgc * (zc * sg))
                dz_chunks.append(dgc * oc * (sg * (1.0 + zc * (1.0 - sg))))
                doo_chunks.append(do_chunks[c] * oc)
            dq_chunks = [None] * 8
            dk_h, dv_h, ds_parts = [None] * N_KV, [None] * N_KV, [None] * N_KV
            tile_lane = lax.broadcasted_iota(jnp.int32, (SUBLANES, LANES), 1)
            tile_row = lax.broadcasted_iota(jnp.int32, (SUBLANES, LANES), 0)
            ones8 = jnp.ones((SUBLANES, LANES), BF16)

            def kv_head(h):
                heads = list(range(h * GROUP, (h + 1) * GROUP))
                kext = _dup_half(kcat, h, lt64k)
                kext_bf, kext_t = kext.astype(BF16), kext.T.astype(BF16)
                vext = _dup_half(vcat, h, lt64k).astype(BF16)
                qst = _head_stack(qr, heads, lt64).astype(BF16)
                p_un, inv, esk = _attn_probs_t(lax.dot_general(kext_bf, qst, NT_DIMS, preferred_element_type=F32), heads, sink_ref, valid)
                pn, psink = p_un * inv, esk * inv
                do_bf = _head_stack(do_chunks, heads, lt64).astype(BF16)
                delta = sum(lax.dot_general(ones8, part, NT_DIMS, preferred_element_type=F32)
                            for part in _split3(_head_stack(doo_chunks, heads, lt64)))[0:1, :]
                dpt = lax.dot_general(vext, do_bf, NT_DIMS, preferred_element_type=F32)
                dst = (pn * (dpt - delta)).astype(BF16)
                sink_term = psink * delta
                ds_acc = jnp.zeros((SUBLANES, LANES), F32)
                for j, n in enumerate(heads):
                    val = -jnp.sum(sink_term[:, j * BLK:(j + 1) * BLK])
                    ds_acc = ds_acc + jnp.where((tile_lane == n) & (tile_row == 0), val, 0.0)
                ds_parts[h] = ds_acc
                dqt = jnp.dot(kext_t, dst, preferred_element_type=F32) * (HEAD_DIM ** -0.5)
                dk_ext = jnp.dot(dst, qst, preferred_element_type=F32)
                dv_ext = jnp.dot(pn.astype(BF16), do_bf, preferred_element_type=F32)
                dk_h[h] = dk_ext + pltpu.roll(dk_ext, HEAD_DIM, 1)
                dv_h[h] = dv_ext + pltpu.roll(dv_ext, HEAD_DIM, 1)
                for c2 in range(GROUP // 2):
                    dq_chunks[4 * h + c2] = _rope_bwd(_pair_chunk(dqt, c2), cc, sc, lm)

            for h in range(N_KV):
                kv_head(h)
            ds_ref[...] += ds_parts[0] + ds_parts[1]
            dk_full = jnp.where(lt64k, dk_h[0], dk_h[1])
            dv_full = jnp.where(lt64k, dv_h[0], dv_h[1])

            @pl.when(i >= 1)
            def _emit():
                dp_ref[:, _QCOL:_QCOL + ATTN_WIDTH] = dq_s[...]
                dp_ref[:, _ZCOL:_ZCOL + ATTN_WIDTH] = dz_s[...]
                dp_ref[:, _KCOL:_KCOL + KV_WIDTH] = _rope_bwd(dk_s[...] + dk_full[:BLK], cp_ref[...], sp_ref[...], lm)
                dp_ref[:, _VCOL:_VCOL + KV_WIDTH] = dv_s[...] + dv_full[:BLK]

            for c in range(8):
                dq_s[:, c * LANES:(c + 1) * LANES] = dq_chunks[c]
                dz_s[:, c * LANES:(c + 1) * LANES] = dz_chunks[c]
            dk_s[...] = dk_full[BLK:]
            dv_s[...] = dv_full[BLK:]

        @pl.when(i == nb)
        def _final():
            dp_ref[:, _QCOL:_QCOL + ATTN_WIDTH] = dq_s[...]
            dp_ref[:, _ZCOL:_ZCOL + ATTN_WIDTH] = dz_s[...]
            dp_ref[:, _KCOL:_KCOL + KV_WIDTH] = _rope_bwd(dk_s[...], cc_ref[...], sc_ref[...], lm)
            dp_ref[:, _VCOL:_VCOL + KV_WIDTH] = dv_s[...]

    cur = lambda b, i: b * nb + jnp.minimum(i, nb - 1)
    prev = lambda b, i: b * nb + jnp.maximum(jnp.minimum(i, nb - 1) - 1, 0)
    emit = lambda b, i: b * nb + jnp.maximum(i - 1, 0)
    hbm = pl.BlockSpec(memory_space=pl.ANY)
    wide = lambda cb: pl.BlockSpec((BLK, ATTN_WIDTH), lambda b, i: (cur(b, i), cb))
    kv = lambda rowf, cb: pl.BlockSpec((BLK, LANES), lambda b, i: (rowf(b, i), cb))
    in_specs = [pl.BlockSpec(memory_space=pltpu.SMEM), wide(0), wide(1),
                kv(cur, _KCOL // LANES), kv(prev, _KCOL // LANES), kv(cur, _VCOL // LANES), kv(prev, _VCOL // LANES),
                kv(cur, 0), kv(cur, 0), kv(prev, 0), kv(prev, 0), wide(0), wide(0)] + [hbm] * nparts
    out_specs = [pl.BlockSpec((BLK, ATTN_IN), lambda b, i: (emit(b, i), 0)),
                 pl.BlockSpec((SUBLANES, LANES), lambda b, i: (0, 0))] + [hbm] * nparts
    return pl.pallas_call(
        body, name="attn_bwd", grid=(b_loc, nb + 1), in_specs=in_specs, out_specs=out_specs,
        out_shape=[jax.ShapeDtypeStruct((n, ATTN_IN), F32), jax.ShapeDtypeStruct((SUBLANES, LANES), F32)] + _scatter_lands(parts),
        scratch_shapes=[pltpu.VMEM((BLK, ATTN_WIDTH), F32), pltpu.VMEM((BLK, ATTN_WIDTH), F32),
                        pltpu.VMEM((BLK, KV_WIDTH), F32), pltpu.VMEM((BLK, KV_WIDTH), F32)] + _scatter_sems(nparts),
        compiler_params=_params(("arbitrary", "arbitrary"), 48),
    )(sinks, p, p, p, p, p, p, ct, st, ct, st, o, dg, *parts)


_CUM_ROWS = 256
HALF = CH // 2
_ROW0 = [SUBLANES * (s // SUBLANES) for s in range(CH)]
_ROW1 = [HALF * (s // HALF + 1) for s in range(CH)]
_ROWS_OF = [_ROW1[s] - _ROW0[s] for s in range(CH)]
_OFF_OF = [sum(_ROWS_OF[:s]) for s in range(CH)]


def _tri(lower):
    r = lax.broadcasted_iota(jnp.int32, (_CUM_ROWS, _CUM_ROWS), 0)
    c = lax.broadcasted_iota(jnp.int32, (_CUM_ROWS, _CUM_ROWS), 1)
    same = (r ^ c) < CH
    return (same & ((c <= r) if lower else (c >= r))).astype(BF16)


def _gates(qp, fp, lb):
    e = jnp.exp(-jnp.abs(fp))
    r = 1.0 / (1.0 + e)
    sig_neg = jnp.where(fp >= 0, e, 1.0) * r
    sig = jnp.where(fp >= 0, 1.0, e) * r
    g = jnp.log(jnp.maximum(lb + (1.0 - lb) * sig, F32_TINY))
    return qp * _sigmoid(qp), g, (1.0 - lb) * sig_neg, sig_neg


def _pair_rows(bc, s):
    diff = bc[_ROW0[s]:_ROW1[s], :] - bc[s:s + 1, :]
    head = jnp.minimum(diff[:SUBLANES, :], 0.0)
    return jnp.exp(head if diff.shape[0] == SUBLANES else jnp.concatenate([head, diff[SUBLANES:, :]], axis=0))


def _cross_half(q, k, bc):
    r = bc[HALF - 1:HALF, :]
    e1, e0 = jnp.exp(bc[HALF:, :] - r), jnp.exp(r - bc[:HALF, :])
    return q[HALF:, :] * e1, e1, k[:HALF, :] * e0, e0


HP = 8
REC_TB = 256
_HW = HP * REC_K


def _hgrn_specs(tb, nt, reverse):
    tmap = (lambda t: nt - 1 - t) if reverse else (lambda t: t)
    groups = REC_HEADS // HP
    blk = lambda cb: pl.BlockSpec((tb, _HW), lambda h, b, t: (b * nt + tmap(t), cb * groups + h))
    head = pl.BlockSpec((tb, _HW), lambda h, b, t: (b * nt + tmap(t), h))
    lbs = pl.BlockSpec((1, _HW), lambda h, b, t: (0, h))
    gws = pl.BlockSpec((1, REC_K), lambda h, b, t: (0, 0))
    hist = pl.BlockSpec((HP, 1, tb // CH, REC_K, REC_K), lambda h, b, t: (h, b, tmap(t), 0, 0))
    return blk, head, lbs, gws, hist


def _chunk_rows(c, first=0, size=CH):
    start = c * CH + first
    return pl.ds(start if isinstance(start, int) else pl.multiple_of(start, CH if first % CH == 0 else SUBLANES), size)


def _cumsum_chunks(tri, x, out_ref, tb):
    for r in range(0, tb, _CUM_ROWS):
        out_ref[r:r + _CUM_ROWS, :] = _dot3(tri, x[r:r + _CUM_ROWS, :])


def _hgrn_fwd(p, lb, gw, b_loc, t_len):
    n = p.shape[0]
    tb = min(REC_TB, t_len)
    nt, nck = t_len // tb, tb // CH

    def body(qp_ref, fp_ref, i_ref, z_ref, lb_ref, gw_ref, oraw_ref, g_ref, sh_ref, q_s, k_s, b_s, o_s, st_ref,
             car_o, car_a, car_s, car_st):
        @pl.when(pl.program_id(2) == 0)
        def _():
            st_ref[...] = jnp.zeros_like(st_ref)

        qv, g, kk, _ = _gates(qp_ref[...], fp_ref[...], lb_ref[...])
        q_s[...] = qv
        k_s[...] = kk
        _cumsum_chunks(_tri(True), g, b_s, tb)
        ones = jnp.ones((REC_K, REC_K), BF16)
        sub = lax.broadcasted_iota(jnp.int32, (SUBLANES, REC_K), 0)

        rows_of = _chunk_rows

        def issue(c, hp):
            rs, cs = rows_of(c), slice(hp * REC_K, (hp + 1) * REC_K)
            q, k, bc, v = q_s[rs, cs], k_s[rs, cs], b_s[rs, cs], i_ref[rs, cs]
            st = st_ref[hp]
            sh_ref[hp, 0, c] = st
            o = lax.dot_general((q * jnp.exp(bc)).astype(BF16), st.astype(BF16), NT_DIMS, preferred_element_type=F32)
            w = jnp.concatenate([q[_ROW0[s]:_ROW1[s], :] * _pair_rows(bc, s) * k[s:s + 1, :] for s in range(CH)], axis=0)
            a = jnp.dot(w.astype(BF16), ones, preferred_element_type=F32)
            qe1, _, ke0, _ = _cross_half(q, k, bc)
            s10 = lax.dot_general(qe1.astype(BF16), ke0.astype(BF16), NT_DIMS, preferred_element_type=F32)
            kd = k * jnp.exp(bc[CH - 1:CH, :] - bc)
            st_new = lax.dot_general(v.astype(BF16), kd.astype(BF16), TN_DIMS, preferred_element_type=F32)
            return o, a, s10, st_new

        def advance_state(c, hp, st_new):
            bl = b_s[_chunk_rows(c, CH - SUBLANES, SUBLANES), hp * REC_K:(hp + 1) * REC_K][SUBLANES - 1:, :]
            st_ref[hp] = st_ref[hp] * jnp.exp(bl) + st_new

        def cross(c, hp, s10):
            v0 = i_ref[_chunk_rows(c, 0, HALF), hp * REC_K:(hp + 1) * REC_K]
            return jnp.dot(s10.astype(BF16), v0.astype(BF16), preferred_element_type=F32)

        def finish(c, hp, o, a, o_cross):
            rs, cs = rows_of(c), slice(hp * REC_K, (hp + 1) * REC_K)
            v = i_ref[rs, cs]
            acc = [jnp.zeros((SUBLANES, REC_K), F32) for _ in range(CH // SUBLANES)]
            for s in range(CH):
                j = s // SUBLANES
                vs = v[s:s + 1, :]
                for jj in range(j, _ROW1[s] // SUBLANES):
                    blk = a[_OFF_OF[s] + (jj - j) * SUBLANES:_OFF_OF[s] + (jj - j + 1) * SUBLANES, :]
                    if jj == j:
                        blk = jnp.where(sub >= s - j * SUBLANES, blk, 0.0)
                    acc[jj] = acc[jj] + blk * vs
            o_s[rs, cs] = o + jnp.concatenate(acc, axis=0) + jnp.concatenate([jnp.zeros((HALF, REC_K), F32), o_cross], axis=0)

        def park(slot, results):
            for hp, (o, a, s10, st_new) in enumerate(results):
                car_o[slot, hp], car_a[slot, hp], car_s[slot, hp], car_st[slot, hp] = o, a, s10, st_new

        def retire(c, slot):
            for hp in range(HP):
                advance_state(c, hp, car_st[slot, hp])
            yield
            crosses = [cross(c, hp, car_s[slot, hp]) for hp in range(HP)]
            for hp in range(HP):
                finish(c, hp, car_o[slot, hp], car_a[slot, hp], crosses[hp])

        def step(c, slot):
            closing = retire(c - 1, slot)
            next(closing)
            park(1 - slot, [issue(c, hp) for hp in range(HP)])
            next(closing, None)

        def trip(j, carry):
            step(2 * j + 1, 0)
            step(2 * j + 2, 1)
            return carry

        assert nck % 2 == 0
        park(0, [issue(0, hp) for hp in range(HP)])
        lax.fori_loop(0, nck // 2 - 1, trip, 0)
        step(nck - 1, 0)
        for _ in retire(nck - 1, 1):
            pass
        oraw_ref[...] = o_s[...]
        for hp in range(HP):
            cs = slice(hp * REC_K, (hp + 1) * REC_K)
            o, zc = o_s[:, cs], z_ref[:, cs]
            on = (o * lax.rsqrt(jnp.mean(o * o, axis=-1, keepdims=True) + NORM_EPS)) * gw_ref[...]
            g_ref[:, cs] = (on * (zc * _sigmoid(zc))).astype(BF16)

    blk, head, lbs, gws, hist = _hgrn_specs(tb, nt, False)
    return pl.pallas_call(
        body, name="hgrn_fwd", grid=(REC_HEADS // HP, b_loc, nt),
        in_specs=[blk(0), blk(1), blk(2), blk(3), lbs, gws], out_specs=[head, head, hist],
        out_shape=[jax.ShapeDtypeStruct((n, REC_WIDTH), F32), jax.ShapeDtypeStruct((n, REC_WIDTH), BF16),
                   jax.ShapeDtypeStruct((REC_HEADS, b_loc, t_len // CH, REC_K, REC_K), F32)],
        scratch_shapes=[pltpu.VMEM((tb, _HW), F32)] * 4 + [pltpu.VMEM((HP, REC_K, REC_K), F32)] + [
            pltpu.VMEM((2, HP, CH, REC_K), F32), pltpu.VMEM((2, HP, sum(_ROWS_OF), REC_K), F32),
            pltpu.VMEM((2, HP, HALF, HALF), F32), pltpu.VMEM((2, HP, REC_K, REC_K), F32)],
        compiler_params=_params(("arbitrary", "arbitrary", "arbitrary"), 48),
    )(p, p, p, p, lb, gw)


def _hgrn_bwd(p, lb, gw, oraw, sh, dg, b_loc, t_len):
    n = p.shape[0]
    tb = min(REC_TB, t_len)
    nt, nck = t_len // tb, tb // CH
    assert HP == REC_HEADS

    def body(qp_ref, fp_ref, i_ref, z_ref, lb_ref, gw_ref, oraw_ref, dg_ref, sh_ref,
             dp_ref, dlb_ref, dgw_ref,
             q_s, k_s, b_s, do_s, dqv_s, dk_s, db_s, dst_ref, car_r, car_dst, car_dec, car_a, car_da, car_x):
        dq_ref, df_ref, di_ref, dz_ref = (dp_ref.at[:, part * REC_WIDTH:(part + 1) * REC_WIDTH] for part in range(4))
        b, t = pl.program_id(1), pl.program_id(2)

        @pl.when(t == 0)
        def _():
            dst_ref[...] = jnp.zeros_like(dst_ref)

        @pl.when((b == 0) & (t == 0))
        def _():
            dlb_ref[...] = jnp.zeros_like(dlb_ref)
            dgw_ref[...] = jnp.zeros_like(dgw_ref)

        lbv, qp, fp = lb_ref[...], qp_ref[...], fp_ref[...]
        qv, g, kk, sig_neg = _gates(qp, fp, lbv)
        q_s[...] = qv
        k_s[...] = kk
        _cumsum_chunks(_tri(True), g, b_s, tb)
        gwv = gw_ref[...]
        for hp in range(HP):
            cs = slice(hp * REC_K, (hp + 1) * REC_K)
            o, zc, dgv = oraw_ref[:, cs], z_ref[:, cs], dg_ref[:, cs]
            rn = lax.rsqrt(jnp.mean(o * o, axis=-1, keepdims=True) + NORM_EPS)
            on = o * rn
            sgz = _sigmoid(zc)
            dz_ref[:, cs] = (dgv * (on * gwv) * (sgz * (1.0 + zc * (1.0 - sgz)))).astype(BF16)
            dpre = dgv * (zc * sgz)
            dgw_ref[hp] += jnp.sum(dpre * on, axis=0, keepdims=True)
            don = dpre * gwv
            do_s[:, cs] = rn * (don - on * jnp.mean(don * on, axis=-1, keepdims=True))

        ones = jnp.ones((REC_K, REC_K), BF16)
        sub = lax.broadcasted_iota(jnp.int32, (SUBLANES, REC_K), 0)
        rowid = lax.broadcasted_iota(jnp.int32, (CH, REC_K), 0)
        ngrp = CH // SUBLANES
        piece_row = lax.broadcasted_iota(jnp.int32, (1, sum(_ROWS_OF)), 1)
        key_of = jnp.zeros((1, sum(_ROWS_OF)), jnp.int32)
        for s in range(1, CH):
            key_of = jnp.where(piece_row >= _OFF_OF[s], s, key_of)
        pick = (key_of == lax.broadcasted_iota(jnp.int32, (CH, sum(_ROWS_OF)), 0)).astype(BF16)

        def operands(c, hp):
            rs, cs = _chunk_rows(c), slice(hp * REC_K, (hp + 1) * REC_K)
            return rs, cs, q_s[rs, cs], k_s[rs, cs], b_s[rs, cs], i_ref[rs, cs], do_s[rs, cs]

        def issue(c, hp, slot):
            _, _, q, k, bc, v, do = operands(c, hp)
            st, dst = sh_ref[hp, 0, c], dst_ref[hp]
            qe, kd = q * jnp.exp(bc), k * jnp.exp(bc[CH - 1:CH, :] - bc)
            do_bf, dst_bf = do.astype(BF16), dst.astype(BF16)
            car_r[slot, hp, 0:CH] = jnp.dot(do_bf, st.astype(BF16), preferred_element_type=F32)
            car_r[slot, hp, CH:2 * CH] = jnp.dot(v.astype(BF16), dst_bf, preferred_element_type=F32)
            car_r[slot, hp, 2 * CH:3 * CH] = lax.dot_general(kd.astype(BF16), dst_bf, NT_DIMS, preferred_element_type=F32)
            car_dst[slot, hp] = lax.dot_general(do_bf, qe.astype(BF16), TN_DIMS, preferred_element_type=F32)
            dec = jnp.concatenate([_pair_rows(bc, s) for s in range(CH)], axis=0)
            qk = jnp.concatenate([q[_ROW0[s]:_ROW1[s], :] * k[s:s + 1, :] for s in range(CH)], axis=0)
            x = jnp.concatenate([do[_ROW0[s]:_ROW1[s], :] * v[s:s + 1, :] for s in range(CH)], axis=0)
            car_dec[slot, hp] = dec
            car_a[slot, hp] = jnp.dot((qk * dec).astype(BF16), ones, preferred_element_type=F32)
            car_da[slot, hp] = jnp.dot(x.astype(BF16), ones, preferred_element_type=F32)
            qe1, _, ke0, _ = _cross_half(q, k, bc)
            qe1_bf, ke0_bf = qe1.astype(BF16), ke0.astype(BF16)
            do1_bf, v0_bf = do[HALF:, :].astype(BF16), v[:HALF, :].astype(BF16)
            car_x[slot, hp, 0:HALF] = lax.dot_general(ke0_bf, qe1_bf, NT_DIMS, preferred_element_type=F32)
            car_x[slot, hp, HALF:2 * HALF] = lax.dot_general(do1_bf, v0_bf, NT_DIMS, preferred_element_type=F32)
            car_x[slot, hp, 2 * HALF:3 * HALF] = lax.dot_general(v0_bf, do1_bf, NT_DIMS, preferred_element_type=F32)

        def advance_state(c, hp, slot):
            ebl = jnp.exp(b_s[_chunk_rows(c, CH - SUBLANES, SUBLANES), hp * REC_K:(hp + 1) * REC_K][SUBLANES - 1:, :])
            st, dst = sh_ref[hp, 0, c], dst_ref[hp]
            dst_ref[hp] = dst * ebl + car_dst[slot, hp]
            return ebl * jnp.sum(st * dst, axis=0, keepdims=True)

        def cross(c, hp, slot):
            _, _, q, k, bc, v, do = operands(c, hp)
            qe1, _, ke0, _ = _cross_half(q, k, bc)
            xs = car_x[slot, hp]
            dqe1 = jnp.dot(xs[HALF:2 * HALF].astype(BF16), ke0.astype(BF16), preferred_element_type=F32)
            dke0 = jnp.dot(xs[2 * HALF:].astype(BF16), qe1.astype(BF16), preferred_element_type=F32)
            dv1 = jnp.dot(xs[:HALF].astype(BF16), do[HALF:, :].astype(BF16), preferred_element_type=F32)
            return dqe1, dke0, dv1

        def retire(c, slot):
            dbl_state = [advance_state(c, hp, slot) for hp in range(HP)]
            yield
            crossed = [cross(c, hp, slot) for hp in range(HP)]
            for hp in range(HP):
                finish(c, hp, slot, dbl_state[hp], *crossed[hp])

        def step(c, slot):
            closing = retire(c + 1, slot)
            next(closing)
            for hp in range(HP):
                issue(c, hp, 1 - slot)
            next(closing, None)

        def trip(j, carry):
            step(nck - 2 - 2 * j, 0)
            step(nck - 3 - 2 * j, 1)
            return carry

        def finish(c, hp, slot, dbl_state, dqe1, dke0, dv1):
            rs, cs, q, k, bc, v, do = operands(c, hp)
            eb, ekd = jnp.exp(bc), jnp.exp(bc[CH - 1:CH, :] - bc)
            qe, kd = q * eb, k * ekd
            qe1, e1, ke0, e0 = _cross_half(q, k, bc)
            dqe, dkd, dv = car_r[slot, hp, 0:CH], car_r[slot, hp, CH:2 * CH], car_r[slot, hp, 2 * CH:3 * CH]
            a, da, decs = car_a[slot, hp], car_da[slot, hp], car_dec[slot, hp]
            dec = [decs[_OFF_OF[s]:_OFF_OF[s] + _ROWS_OF[s], :] for s in range(CH)]
            dbl = jnp.sum(dkd * kd, axis=0, keepdims=True) + dbl_state
            dq_acc = [jnp.zeros((SUBLANES, REC_K), F32) for _ in range(ngrp)]
            uk, uv = [], []
            for s in range(CH):
                j = s // SUBLANES
                r0 = j * SUBLANES
                ks = k[s:s + 1, :]
                for jj in range(j, _ROW1[s] // SUBLANES):
                    lo, hi = _OFF_OF[s] + (jj - j) * SUBLANES, _OFF_OF[s] + (jj - j + 1) * SUBLANES
                    a_blk, da_blk = a[lo:hi, :], da[lo:hi, :]
                    if jj == j:
                        keep = sub >= s - r0
                        a_blk, da_blk = jnp.where(keep, a_blk, 0.0), jnp.where(keep, da_blk, 0.0)
                    rows = slice(jj * SUBLANES, (jj + 1) * SUBLANES)
                    tt = da_blk * dec[s][(jj - j) * SUBLANES:(jj - j + 1) * SUBLANES, :]
                    dq_acc[jj] = dq_acc[jj] + tt * ks
                    uk.append(tt * q[rows, :])
                    uv.append(a_blk * do[rows, :])
            dk_in = jnp.dot(pick, jnp.concatenate(uk, axis=0).astype(BF16), preferred_element_type=F32)
            dv_in = jnp.dot(pick, jnp.concatenate(uv, axis=0).astype(BF16), preferred_element_type=F32)
            zero_half = jnp.zeros((HALF, REC_K), F32)
            dq_x = jnp.concatenate([zero_half, dqe1 * e1], axis=0)
            dk_x = jnp.concatenate([dke0 * e0, zero_half], axis=0)
            dv_x = jnp.concatenate([dv1, zero_half], axis=0)
            db_x = jnp.concatenate([-(dke0 * ke0), dqe1 * qe1], axis=0)
            dq_in = jnp.concatenate(dq_acc, axis=0)
            dqv_s[rs, cs] = dqe * eb + dq_in + dq_x
            dk_s[rs, cs] = dkd * ekd + dk_in + dk_x
            di_ref[rs, cs] = (dv + dv_in + dv_x).astype(BF16)
            db = dqe * qe - dkd * kd + q * dq_in - k * dk_in + db_x
            db_s[rs, cs] = db + jnp.where(rowid == CH - 1, dbl, 0.0)

        assert nck % 2 == 0
        for hp in range(HP):
            issue(nck - 1, hp, 0)
        lax.fori_loop(0, nck // 2 - 1, trip, 0)
        step(0, 0)
        for _ in retire(0, 1):
            pass
        up = _tri(False)
        sgq = _sigmoid(qp)
        dq_ref[...] = (dqv_s[...] * (sgq * (1.0 + qp * (1.0 - sgq)))).astype(BF16)
        dlb_acc = jnp.zeros((1, _HW), F32)
        for r in range(0, tb, _CUM_ROWS):
            rows = slice(r, r + _CUM_ROWS)
            dgl = _dot3(up, db_s[rows, :])
            dfg = dgl * jnp.exp(-g[rows, :]) - dk_s[rows, :]
            sn = sig_neg[rows, :]
            df_ref[rows, :] = (dfg * (1.0 - lbv) * (1.0 - sn) * sn).astype(BF16)
            dlb_acc = dlb_acc + jnp.sum(dfg * sn, axis=0, keepdims=True)
        dlb_ref[...] += dlb_acc

    blk, head, lbs, gws, hist = _hgrn_specs(tb, nt, True)
    out_specs = [pl.BlockSpec((tb, REC_IN), lambda h, b, t: (b * nt + nt - 1 - t, 0)), lbs,
                 pl.BlockSpec((HP, 1, REC_K), lambda h, b, t: (h, 0, 0))]
    out_shape = [jax.ShapeDtypeStruct((n, REC_IN), BF16),
                 jax.ShapeDtypeStruct((1, REC_WIDTH), F32), jax.ShapeDtypeStruct((REC_HEADS, 1, REC_K), F32)]
    return pl.pallas_call(
        body, name="hgrn_bwd", grid=(REC_HEADS // HP, b_loc, nt),
        in_specs=[blk(0), blk(1), blk(2), blk(3), lbs, gws, head, head, hist],
        out_specs=out_specs, out_shape=out_shape,
        scratch_shapes=[pltpu.VMEM((tb, _HW), F32)] * 7 + [pltpu.VMEM((HP, REC_K, REC_K), F32)] + [
            pltpu.VMEM((2, HP, 3 * CH, REC_K), F32), pltpu.VMEM((2, HP, REC_K, REC_K), F32)] + [
            pltpu.VMEM((2, HP, sum(_ROWS_OF), REC_K), F32)] * 3 + [pltpu.VMEM((2, HP, 3 * HALF, HALF), F32)],
        compiler_params=_params(("arbitrary", "arbitrary", "arbitrary"), 56),
    )(p, p, p, p, lb, gw, oraw, dg, sh)


def _postnorm_bwd_nt(dxo, y, qw, w, name):
    n = dxo.shape[0]

    def body(dx_ref, y_ref, qw_ref, w_ref, dg_ref, dy_ref, dqw_ref, db_ref):
        @pl.when(pl.program_id(0) == 0)
        def _():
            dqw_ref[...] = jnp.zeros_like(dqw_ref)
            db_ref[...] = jnp.zeros_like(db_ref)

        yv, dxv = y_ref[...], dx_ref[...]
        r = lax.rsqrt(jnp.mean(yv * yv, axis=-1, keepdims=True) + NORM_EPS)
        u = yv * r
        du = dxv * qw_ref[...]
        dy = r * (du - u * jnp.mean(du * u, axis=-1, keepdims=True))
        dqw_ref[...] += jnp.sum(dxv * u, axis=0, keepdims=True)
        db_ref[...] += jnp.sum(dy, axis=0, keepdims=True)
        dyb = dy.astype(BF16)
        dy_ref[...] = dyb
        dg_ref[...] = lax.dot_general(dyb, w_ref[...], NT_DIMS, preferred_element_type=F32)

    rows = pl.BlockSpec((TM, D_MODEL), lambda i: (i, 0))
    const = lambda shape: pl.BlockSpec(shape, lambda i: (0, 0))
    return pl.pallas_call(
        body, name=name, grid=(n // TM,), in_specs=[rows, rows, const((1, D_MODEL)), const((D_MODEL, D_MODEL))],
        out_specs=[rows, rows, const((1, D_MODEL)), const((1, D_MODEL))],
        out_shape=[jax.ShapeDtypeStruct((n, D_MODEL), F32), jax.ShapeDtypeStruct((n, D_MODEL), BF16),
                   jax.ShapeDtypeStruct((1, D_MODEL), F32), jax.ShapeDtypeStruct((1, D_MODEL), F32)],
        compiler_params=_params(("arbitrary",), 48),
    )(dxo, y, qw, w)


def _nt_prenorm_bwd(dps, w, x, pw, dxo, has_bias, name, parts=()):
    n = x.shape[0]
    widths = [d.shape[1] for d in dps]
    m = sum(widths)
    npieces, nparts, steps = len(dps), len(parts), n // TM

    def body(*refs):
        dp_refs = refs[:npieces]
        w_ref, x_ref, pw_ref, dxo_ref = refs[npieces:npieces + 4]
        part_refs = refs[npieces + 4:npieces + 4 + nparts]
        dx_ref, dpw_ref, db_ref = refs[npieces + 4 + nparts:npieces + 7 + nparts]
        land_refs = refs[npieces + 7 + nparts:npieces + 7 + 2 * nparts]
        sems = refs[npieces + 7 + 2 * nparts:]

        @pl.when(pl.program_id(0) == 0)
        def _():
            dpw_ref[...] = jnp.zeros_like(dpw_ref)
            db_ref[...] = jnp.zeros_like(db_ref)
            if nparts:
                _scatter_start(part_refs, land_refs, sems)

        dh = jnp.zeros((TM, D_MODEL), F32)
        off = 0
        for dp_ref, wd in zip(dp_refs, widths):
            cn = _col_chunk(wd)
            for j in range(0, wd, cn):
                dpc = dp_ref[:, j:j + cn]
                if has_bias:
                    db_ref[:, off + j:off + j + cn] += jnp.sum(dpc, axis=0, keepdims=True)
                dh = dh + lax.dot_general(dpc.astype(BF16), w_ref[:, off + j:off + j + cn], NT_DIMS, preferred_element_type=F32)
            off += wd
        xv = x_ref[...]
        r = lax.rsqrt(jnp.mean(xv * xv, axis=-1, keepdims=True) + NORM_EPS)
        xn = xv * r
        dpw_ref[...] += jnp.sum(dh * xn, axis=0, keepdims=True)
        dxn = dh * pw_ref[...]
        dx_ref[...] = dxo_ref[...] + r * (dxn - xn * jnp.mean(dxn * xn, axis=-1, keepdims=True))

        if nparts:
            @pl.when(pl.program_id(0) == steps - 1)
            def _():
                _scatter_wait(part_refs, land_refs, sems)

    rows = pl.BlockSpec((TM, D_MODEL), lambda i: (i, 0))
    const = lambda shape: pl.BlockSpec(shape, lambda i: (0, 0))
    hbm = pl.BlockSpec(memory_space=pl.ANY)
    in_specs = ([pl.BlockSpec((TM, wd), lambda i: (i, 0)) for wd in widths] + [const((D_MODEL, m)), rows, const((1, D_MODEL)), rows]
                + [hbm] * nparts)
    return pl.pallas_call(
        body, name=name, grid=(steps,), in_specs=in_specs,
        out_specs=[rows, const((1, D_MODEL)), const((1, m))] + [hbm] * nparts,
        out_shape=[jax.ShapeDtypeStruct((n, D_MODEL), F32), jax.ShapeDtypeStruct((1, D_MODEL), F32),
                   jax.ShapeDtypeStruct((1, m), F32)] + _scatter_lands(parts),
        scratch_shapes=_scatter_sems(nparts) if nparts else [],
        compiler_params=_params(("arbitrary",), 56),
    )(*dps, w, x, pw, dxo, *parts)


def _matmul_tn(a, b, name):
    n, k = a.shape
    m = b.shape[1]
    tk, tm, tn = k, _col_chunk(m), TN_ROWS if n % TN_ROWS == 0 else n

    def body(a_ref, b_ref, o_ref):
        @pl.when(pl.program_id(2) == 0)
        def _():
            o_ref[...] = jnp.zeros_like(o_ref)

        o_ref[...] += lax.dot_general(a_ref[...], b_ref[...].astype(BF16), TN_DIMS, preferred_element_type=F32)

    return pl.pallas_call(
        body, name=name, grid=(k // tk, m // tm, n // tn),
        in_specs=[pl.BlockSpec((tn, tk), lambda i, j, l: (l, i)), pl.BlockSpec((tn, tm), lambda i, j, l: (l, j))],
        out_specs=pl.BlockSpec((tk, tm), lambda i, j, l: (i, j)),
        out_shape=jax.ShapeDtypeStruct((k, m), F32),
        compiler_params=_params(("arbitrary", "arbitrary", "arbitrary"), 48),
    )(a, b)


def _matmul_tn_by_owner(a, b, name):
    n, k = a.shape
    c = b.shape[1] // N_DEV
    tn = TN_ROWS if n % TN_ROWS == 0 else n
    steps = n // tn
    per = 2

    def body(a_ref, b_ref, o_ref, w_ref):
        @pl.when(pl.program_id(1) == 0)
        def _():
            o_ref[...] = jnp.zeros_like(o_ref)

        r = lax.dot_general(a_ref[...], b_ref[...], TN_DIMS, preferred_element_type=F32)
        for j in range(per):
            o_ref[j] += r[:, j * c:(j + 1) * c]

        @pl.when(pl.program_id(1) == steps - 1)
        def _():
            w_ref[...] = o_ref[...].astype(BF16)

    out = pl.BlockSpec((per, k, c), lambda j, l: (j, 0, 0))
    return pl.pallas_call(
        body, name=name, grid=(N_DEV // per, steps),
        in_specs=[pl.BlockSpec((tn, k), lambda j, l: (l, 0)), pl.BlockSpec((tn, per * c), lambda j, l: (l, j))],
        out_specs=[out, out],
        out_shape=[jax.ShapeDtypeStruct((N_DEV, k, c), F32), jax.ShapeDtypeStruct((N_DEV, k, c), BF16)],
        compiler_params=_params(("arbitrary", "arbitrary"), 48),
    )(a, b)


def _by_owner_cols(dw):
    k, m = dw.shape
    return dw.reshape(k, N_DEV, m // N_DEV).transpose(1, 0, 2)


def _own_and_bf16(part):
    return lax.dynamic_index_in_dim(part, _my_id(), 0, keepdims=False), part.astype(BF16)


def _step(x, ct, st, tgt, pre_w, post_w, wa_in, ba_in, sinks, wa_out_shard, ba_out, wr_in_shard, lb_logits, gnorm_w, wr_out_shard, b_loc, t_len):
    nb = t_len // BLK
    lb = _lower_bound(lb_logits)
    p0, h0, ga_out = _norm_matmul(x, pre_w[0:1], wa_in, ba_in, "attn_in_proj", [wa_out_shard])
    wa_out = ga_out.reshape(ATTN_WIDTH, D_MODEL)
    o0, g0, gr_in = _attn_fwd(p0, ct, st, sinks, b_loc, nb, [wr_in_shard])
    wr_in = gr_in.transpose(1, 0, 2).reshape(D_MODEL, REC_IN)
    y0, x1 = _outproj_postnorm(g0, wa_out, ba_out, x, post_w[0:1], "attn_out_proj")
    p1, h1, gr_out = _norm_matmul(x1, pre_w[1:2], wr_in, None, "rec_in_proj", [wr_out_shard])
    wr_out = gr_out.reshape(REC_WIDTH, D_MODEL)
    o1, g1, sh = _hgrn_fwd(p1, lb, gnorm_w, b_loc, t_len)
    dx2, dg1, dy1, dpost1, loss_tile = _outproj_loss_bwd(g1, wr_out, x1, post_w[1:2], tgt, "rec_out_proj_loss_bwd")
    d_wr_out = _matmul_tn(g1, dy1, "rec_w_out_grad")
    dp1, dlb, dgw = _hgrn_bwd(p1, lb, gnorm_w, o1, sh, dg1, b_loc, t_len)
    dx1, dpre1, _ = _nt_prenorm_bwd([dp1], wr_in, x1, pre_w[1:2], dx2, False, "rec_in_bwd")
    part_r_in, wire_r_in = _matmul_tn_by_owner(h1, dp1, "rec_w_in_grad")
    own_r_in = lax.dynamic_index_in_dim(part_r_in, _my_id(), 0, keepdims=False)
    dg0, dy0, dpost0, dba_out = _postnorm_bwd_nt(dx1, y0, post_w[0:1], wa_out, "attn_out_bwd")
    d_wa_out = _matmul_tn(g0, dy0, "attn_w_out_grad")
    owns, wires = zip(*[_own_and_bf16(part) for part in (
        d_wr_out.reshape(N_DEV, REC_WIDTH // N_DEV, D_MODEL), d_wa_out.reshape(N_DEV, ATTN_WIDTH // N_DEV, D_MODEL))])
    owns, wires = (own_r_in,) + owns, (wire_r_in,) + wires
    dp0, dsink_tile, *lands = _attn_bwd(p0, ct, st, sinks, o0, dg0, b_loc, nb, list(wires))
    d_wa_in = _matmul_tn(h0, dp0, "attn_w_in_grad")
    own_a_in, wire_a_in = _own_and_bf16(_by_owner_cols(_qkvz(d_wa_in)))
    dx0, dpre0, dba_in, land_a_in = _nt_prenorm_bwd([dp0], wa_in, x, pre_w[0:1], dx1, True, "attn_in_bwd", [wire_a_in])
    small = dict(pre=jnp.concatenate([dpre0, dpre1], axis=0), post=jnp.concatenate([dpost0, dpost1], axis=0),
                 ba_in=dba_in, sinks=dsink_tile[0:1, 0:N_HEADS], ba_out=dba_out, lb=dlb, gnorm=jnp.sum(dgw, axis=0))
    return loss_tile, dx0, list(zip(lands, owns)) + [(land_a_in, own_a_in)], small


def _my_id():
    return lax.axis_index("x") * 4 + lax.axis_index("y") * 2 + lax.axis_index("c")


def _peer(k):
    x, y, c = lax.axis_index("x"), lax.axis_index("y"), lax.axis_index("c")
    return (x ^ ((k >> 2) & 1), y ^ ((k >> 1) & 1), c ^ (k & 1))


def _peer_id(k):
    return _my_id() ^ k


def _all_gather_by_chip(shard, pos_col):
    n = pos_col.shape[0]

    def body(x_ref, p_ref, f_ref, s_ref, out_ref, ct_ref, st_ref, send_sems, recv_sems, local_sem):
        x, y, c = lax.axis_index("x"), lax.axis_index("y"), lax.axis_index("c")
        me, sibling = (x, y, c), (x, y, 1 - c)
        chips = [(1 - x, y), (x, 1 - y), (1 - x, 1 - y)]

        def rows(px, py, pc):
            return out_ref.at[4 * px + 2 * py + pc]

        def copy(k, block, to, src=None):
            return pltpu.make_async_remote_copy(src_ref=rows(*block) if src is None else src, dst_ref=rows(*block),
                                                send_sem=send_sems.at[k], recv_sem=recv_sems.at[k], device_id=to, device_id_type=MESH)

        mine = pltpu.make_async_copy(x_ref, rows(*me), local_sem)
        mine.start()
        first = [copy(0, me, sibling, src=x_ref)] + [copy(1 + j, me, (*chip, c), src=x_ref) for j, chip in enumerate(chips)]
        for cp in first:
            cp.start()
        _rope_tables_into(p_ref, f_ref, s_ref, ct_ref, st_ref)
        passed = [copy(4 + j, (*chip, c), sibling) for j, chip in enumerate(chips)]
        for j, chip in enumerate(chips):
            copy(1 + j, (*chip, c), me).wait_recv()
            passed[j].start()
        copy(0, sibling, me).wait_recv()
        for j, chip in enumerate(chips):
            copy(4 + j, (*chip, 1 - c), me).wait_recv()
        for cp in first + passed:
            cp.wait_send()
        mine.wait()

    hbm, vmem = pl.BlockSpec(memory_space=pl.ANY), pl.BlockSpec(memory_space=pltpu.VMEM)
    return pl.pallas_call(
        body, name="comm_all_gather_by_chip", in_specs=[hbm, vmem, vmem, vmem], out_specs=[hbm, vmem, vmem],
        out_shape=[jax.ShapeDtypeStruct((N_DEV,) + shard.shape, shard.dtype)] + [jax.ShapeDtypeStruct((n, LANES), F32)] * 2,
        scratch_shapes=[pltpu.SemaphoreType.DMA((N_DEV - 1,)), pltpu.SemaphoreType.DMA((N_DEV - 1,)), pltpu.SemaphoreType.DMA],
        compiler_params=_params(None, 32),
    )(shard, pos_col, *_rope_lanes())


def _gather_shapes(shards):
    return [jax.ShapeDtypeStruct((N_DEV,) + s.shape, s.dtype) for s in shards]


def _gather_sems(nsh):
    return [pltpu.SemaphoreType.DMA((nsh, N_DEV - 1)), pltpu.SemaphoreType.DMA((nsh, N_DEV - 1)), pltpu.SemaphoreType.DMA((nsh,))]


def _gather_copies(ins, outs, sems, received):
    send_sems, recv_sems, local_sems = sems
    me = _my_id()
    local = [pltpu.make_async_copy(ins[a], outs[a].at[me], local_sems.at[a]) for a in range(len(ins))]
    remote = [pltpu.make_async_remote_copy(
        src_ref=ins[a], dst_ref=outs[a].at[_peer_id(k) if received else me], send_sem=send_sems.at[a, k - 1],
        recv_sem=recv_sems.at[a, k - 1], device_id=_peer(k), device_id_type=MESH)
        for a in range(len(ins)) for k in range(1, N_DEV)]
    return local, remote


def _gather_start(ins, outs, sems):
    local, sends = _gather_copies(ins, outs, sems, False)
    for cp in local + sends:
        cp.start()


def _gather_wait(ins, outs, sems):
    local, recvs = _gather_copies(ins, outs, sems, True)
    for cp in recvs:
        cp.wait_recv()
    for cp in recvs:
        cp.wait_send()
    for cp in local:
        cp.wait()


def _scatter_lands(parts):
    return [jax.ShapeDtypeStruct((N_DEV - 1,) + p.shape[1:], p.dtype) for p in parts]


def _scatter_sems(nparts):
    return [pltpu.SemaphoreType.DMA((nparts, N_DEV - 1)), pltpu.SemaphoreType.DMA((nparts, N_DEV - 1))]


def _scatter_copies(parts, lands, sems):
    send_sems, recv_sems = sems
    return [pltpu.make_async_remote_copy(
        src_ref=parts[a].at[_peer_id(k)], dst_ref=lands[a].at[k - 1], send_sem=send_sems.at[a, k - 1],
        recv_sem=recv_sems.at[a, k - 1], device_id=_peer(k), device_id_type=MESH)
        for a in range(len(parts)) for k in range(1, N_DEV)]


def _scatter_start(parts, lands, sems):
    for cp in _scatter_copies(parts, lands, sems):
        cp.start()


def _scatter_wait(parts, lands, sems):
    copies = _scatter_copies(parts, lands, sems)
    for cp in copies:
        cp.wait_recv()
    for cp in copies:
        cp.wait_send()


def _adamw(w, g, m, v):
    m2 = ADAM_B1 * m + (1.0 - ADAM_B1) * g
    v2 = ADAM_B2 * v + (1.0 - ADAM_B2) * (g * g)
    m_hat = m2 / (1.0 - ADAM_B1 ** ADAM_STEP)
    v_hat = v2 / (1.0 - ADAM_B2 ** ADAM_STEP)
    delta = -ADAM_LR * (m_hat / (jnp.sqrt(v_hat) + ADAM_EPS) + ADAM_WD * w)
    return delta, m2, v2


def _sum_adamw(land, own, w, m, v, name):
    r, c = own.shape
    rb = min(r, 256)

    def body(land_ref, own_ref, w_ref, m_ref, v_ref, g_ref, d_ref, m2_ref, v2_ref):
        me = _my_id()
        g = jnp.zeros((rb, c), F32)
        for dev in range(N_DEV):
            k = dev ^ me
            g = g + jnp.where(k == 0, own_ref[...], land_ref[jnp.maximum(k - 1, 0)].astype(F32))
        delta, m2, v2 = _adamw(w_ref[...], g, m_ref[...], v_ref[...])
        g_ref[...] = g
        d_ref[...] = delta
        m2_ref[...] = m2
        v2_ref[...] = v2

    rows = pl.BlockSpec((rb, c), lambda i: (i, 0))
    return pl.pallas_call(
        body, name=name, grid=(r // rb,), in_specs=[pl.BlockSpec((N_DEV - 1, rb, c), lambda i: (0, i, 0))] + [rows] * 4,
        out_specs=[rows] * 4, out_shape=[jax.ShapeDtypeStruct((r, c), F32)] * 4,
        compiler_params=_params(("arbitrary",), 32),
    )(land, own, w, m, v)


_SMALL = [("pre_norm_w", 2048), ("post_norm_w", 2048), ("attn_b_in", 2304), ("attn_sinks", 16), ("attn_b_out", 1024),
          ("rec_lb_logits", 2048), ("rec_gnorm_w", 128), ("loss", 1)]
_TILE = SUBLANES * LANES


def _small_rows(size):
    return -(-size // _TILE) * SUBLANES


_SMALL_OFF = {}
_r = 0
for _name, _size in _SMALL:
    _SMALL_OFF[_name] = _r
    _r += _small_rows(_size)
_SMALL_ROWS = _r


def _pack_small(pieces):
    out = []
    for name, size in _SMALL:
        flat = pieces[name].reshape(-1).astype(F32)
        out.append(jnp.pad(flat, (0, _small_rows(size) * LANES - size)).reshape(-1, LANES))
    return jnp.concatenate(out, axis=0)


def _unpack_small(packed, shapes):
    return {name: packed[_SMALL_OFF[name]:_SMALL_OFF[name] + _small_rows(size)].reshape(-1)[:size].reshape(shapes[name])
            for name, size in _SMALL}


def _small_allreduce_adamw(gpart, w, m, v):
    lb0 = _SMALL_OFF["rec_lb_logits"]

    def body(gp_ref, w_ref, m_ref, v_ref, g_ref, d_ref, m2_ref, v2_ref, land_ref, send_sems, recv_sems):
        me = _my_id()
        sent = []
        for k in range(1, N_DEV):
            cp = pltpu.make_async_remote_copy(src_ref=gp_ref, dst_ref=land_ref.at[k - 1], send_sem=send_sems.at[k - 1],
                                              recv_sem=recv_sems.at[k - 1], device_id=_peer(k), device_id_type=MESH)
            cp.start()
            sent.append(cp)
        for cp in sent:
            cp.wait_recv()
        for cp in sent:
            cp.wait_send()
        g = jnp.zeros((_SMALL_ROWS, LANES), F32)
        for dev in range(N_DEV):
            k = dev ^ me
            g = g + jnp.where(k == 0, gp_ref[...], land_ref[jnp.maximum(k - 1, 0)])
        g_ref[...] = g
        l0, l1 = w_ref[lb0:lb0 + SUBLANES, :], w_ref[lb0 + SUBLANES:lb0 + 2 * SUBLANES, :]
        mx = jnp.maximum(l0, l1)
        e0, e1 = jnp.exp(l0 - mx), jnp.exp(l1 - mx)
        p1 = e1 / (e0 + e1)
        dl1 = (1.0 - p1) * p1 * g[lb0:lb0 + SUBLANES, :]
        g_ref[lb0:lb0 + SUBLANES, :] = -dl1
        g_ref[lb0 + SUBLANES:lb0 + 2 * SUBLANES, :] = dl1
        delta, m2, v2 = _adamw(w_ref[...], g_ref[...], m_ref[...], v_ref[...])
        d_ref[...] = delta
        m2_ref[...] = m2
        v2_ref[...] = v2

    vmem = pl.BlockSpec(memory_space=pltpu.VMEM)
    return pl.pallas_call(
        body, name="comm_small_allreduce_adamw", in_specs=[vmem] * 4, out_specs=[vmem] * 4,
        out_shape=[jax.ShapeDtypeStruct((_SMALL_ROWS, LANES), F32)] * 4,
        scratch_shapes=[pltpu.VMEM((N_DEV - 1, _SMALL_ROWS, LANES), F32), pltpu.SemaphoreType.DMA((N_DEV - 1,)),
                        pltpu.SemaphoreType.DMA((N_DEV - 1,))],
    )(gpart, w, m, v)


def _qzkv(a):
    kv_end = ATTN_WIDTH + 2 * KV_WIDTH
    return jnp.concatenate([a[..., :ATTN_WIDTH], a[..., kv_end:], a[..., ATTN_WIDTH:kv_end]], axis=-1)


def _qkvz(a):
    return jnp.concatenate([a[..., :ATTN_WIDTH], a[..., 2 * ATTN_WIDTH:], a[..., ATTN_WIDTH:2 * ATTN_WIDTH]], axis=-1)


def kernel(x, positions, pre_norm_w, post_norm_w, attn_w_in, attn_b_in, attn_sinks, attn_w_out, attn_b_out, rec_w_in, rec_lb_logits, rec_gnorm_w, rec_w_out, loss_target, m_pre_norm_w, m_post_norm_w, m_attn_w_in, m_attn_b_in, m_attn_sinks, m_attn_w_out, m_attn_b_out, m_rec_w_in, m_rec_lb_logits, m_rec_gnorm_w, m_rec_w_out, v_pre_norm_w, v_post_norm_w, v_attn_w_in, v_attn_b_in, v_attn_sinks, v_attn_w_out, v_attn_b_out, v_rec_w_in, v_rec_lb_logits, v_rec_gnorm_w, v_rec_w_out):
    b_loc, t_len, _ = x.shape
    n = b_loc * t_len
    ga_in, ct, st = _all_gather_by_chip(attn_w_in[0].astype(BF16), positions.reshape(n, 1).astype(F32))
    wa_in = _qzkv(ga_in.transpose(1, 0, 2).reshape(D_MODEL, ATTN_IN))

    loss_tile, dx, landed, small = _step(
        x.reshape(n, D_MODEL), ct, st, loss_target.reshape(n, D_MODEL),
        pre_norm_w, post_norm_w, wa_in, _qzkv(attn_b_in), attn_sinks, attn_w_out[0].astype(BF16), attn_b_out,
        rec_w_in[0].astype(BF16), rec_lb_logits, rec_gnorm_w, rec_w_out[0].astype(BF16), b_loc, t_len)

    lift = lambda outs: tuple(a[None] for a in outs)
    (l_r_in, o_r_in), (l_r_out, o_r_out), (l_a_out, o_a_out), (l_a_in, o_a_in) = landed
    r_a_in = lift(_sum_adamw(l_a_in, o_a_in, attn_w_in[0], m_attn_w_in[0], v_attn_w_in[0], "adamw_attn_w_in"))
    r_r_in = lift(_sum_adamw(l_r_in, o_r_in, rec_w_in[0], m_rec_w_in[0], v_rec_w_in[0], "adamw_rec_w_in"))
    r_r_out = lift(_sum_adamw(l_r_out, o_r_out, rec_w_out[0], m_rec_w_out[0], v_rec_w_out[0], "adamw_rec_w_out"))
    r_a_out = lift(_sum_adamw(l_a_out, o_a_out, attn_w_out[0], m_attn_w_out[0], v_attn_w_out[0], "adamw_attn_w_out"))

    gsmall = dict(pre_norm_w=small["pre"], post_norm_w=small["post"], attn_b_in=_qkvz(small["ba_in"]), attn_sinks=small["sinks"],
                  attn_b_out=small["ba_out"], rec_lb_logits=jnp.concatenate([small["lb"], jnp.zeros_like(small["lb"])], axis=0),
                  rec_gnorm_w=small["gnorm"], loss=loss_tile[0:1, 0:1])
    nil = jnp.zeros((1, 1), F32)
    wsmall = dict(pre_norm_w=pre_norm_w, post_norm_w=post_norm_w, attn_b_in=attn_b_in, attn_sinks=attn_sinks,
                  attn_b_out=attn_b_out, rec_lb_logits=rec_lb_logits, rec_gnorm_w=rec_gnorm_w, loss=nil)
    msmall = dict(pre_norm_w=m_pre_norm_w, post_norm_w=m_post_norm_w, attn_b_in=m_attn_b_in, attn_sinks=m_attn_sinks,
                  attn_b_out=m_attn_b_out, rec_lb_logits=m_rec_lb_logits, rec_gnorm_w=m_rec_gnorm_w, loss=nil)
    vsmall = dict(pre_norm_w=v_pre_norm_w, post_norm_w=v_post_norm_w, attn_b_in=v_attn_b_in, attn_sinks=v_attn_sinks,
                  attn_b_out=v_attn_b_out, rec_lb_logits=v_rec_lb_logits, rec_gnorm_w=v_rec_gnorm_w, loss=nil)
    shapes = {k: a.shape for k, a in wsmall.items()}
    packed = _small_allreduce_adamw(_pack_small(gsmall), _pack_small(wsmall), _pack_small(msmall), _pack_small(vsmall))
    sg, sd, sm, sv = [_unpack_small(a, shapes) for a in packed]

    big = {"attn_w_in": r_a_in, "attn_w_out": r_a_out, "rec_w_in": r_r_in, "rec_w_out": r_r_out}
    order = ["pre_norm_w", "post_norm_w", "attn_w_in", "attn_b_in", "attn_sinks", "attn_w_out", "attn_b_out", "rec_w_in",
             "rec_lb_logits", "rec_gnorm_w", "rec_w_out"]
    outs = [sg["loss"][0, 0], dx.reshape(b_loc, t_len, D_MODEL)]
    for idx, small_set in enumerate((sg, sd, sm, sv)):
        outs += [big[nm][idx] if nm in big else small_set[nm] for nm in order]
    return tuple(outs)
```

```python
import numpy as np
import jax
import jax.numpy as jnp
from jax import lax
from jax.experimental import pallas as pl
from jax.experimental.pallas import tpu as pltpu

F32, BF16 = jnp.float32, jnp.bfloat16
MESH = pl.DeviceIdType.MESH
N_DEV = 8

D_MODEL = 1024
N_HEADS, HEAD_DIM, N_KV, GROUP = 16, 64, 2, 8
ATTN_WIDTH, KV_WIDTH = 1024, 128
ATTN_IN = 2 * ATTN_WIDTH + 2 * KV_WIDTH
BLK = 128
ROPE_THETA, ROPE_HALF = 500000.0, 8
REC_HEADS, REC_K = 8, 128
REC_WIDTH = REC_HEADS * REC_K
REC_IN = 4 * REC_WIDTH
TN_ROWS = 1024
CH = 32
NORM_EPS = 1e-6
F32_TINY = 1.1754944e-38
ADAM_LR, ADAM_B1, ADAM_B2, ADAM_EPS, ADAM_WD, ADAM_STEP = 0.001, 0.9, 0.999, 1e-08, 0.01, 10

LANES, SUBLANES = 128, 8
TM = 512
NT_DIMS = (((1,), (1,)), ((), ()))
TN_DIMS = (((0,), (0,)), ((), ()))
MB = 2 ** 20


def _params(sem=None, vmem_mb=48, **kw):
    return pltpu.CompilerParams(dimension_semantics=sem, vmem_limit_bytes=vmem_mb * MB, **kw)


def _col_chunk(m):
    return 768 if m % 1024 else 1024


def _sigmoid(x):
    return 1.0 / (1.0 + jnp.exp(-x))


def _split3(x):
    hi = x.astype(BF16)
    r1 = x - hi.astype(F32)
    mid = r1.astype(BF16)
    lo = (r1 - mid.astype(F32)).astype(BF16)
    return hi, mid, lo


def _dot3(l_bf, x):
    hi, mid, lo = _split3(x)
    return (jnp.dot(l_bf, hi, preferred_element_type=F32) + jnp.dot(l_bf, mid, preferred_element_type=F32)
            + jnp.dot(l_bf, lo, preferred_element_type=F32))


def _rope_lanes():
    lane = np.arange(LANES) % HEAD_DIM
    inv = np.float32(ROPE_THETA) ** (-(np.arange(ROPE_HALF, dtype=np.float32) * np.float32(2.0) / np.float32(2 * ROPE_HALF)))
    freq = np.where(lane < 2 * ROPE_HALF, inv[lane % ROPE_HALF], 0.0).astype(np.float32)[None, :]
    sign = np.where(lane < ROPE_HALF, -1.0, np.where(lane < 2 * ROPE_HALF, 1.0, 0.0)).astype(np.float32)[None, :]
    return jnp.asarray(freq), jnp.asarray(sign)


def _rope_tables_into(p_ref, f_ref, s_ref, c_out, s_out):
    def rows(i, carry):
        rs = pl.ds(pl.multiple_of(i * TM, TM), TM)
        ang = p_ref[rs, :] * f_ref[...]
        c_out[rs, :] = jnp.cos(ang)
        s_out[rs, :] = jnp.sin(ang) * s_ref[...]
        return carry

    lax.fori_loop(0, p_ref.shape[0] // TM, rows, 0)


def _rope_apply(xv, c, s, lm):
    partner = jnp.where(lm < ROPE_HALF, pltpu.roll(xv, LANES - ROPE_HALF, 1), pltpu.roll(xv, ROPE_HALF, 1))
    return xv * c + partner * s


def _rope_bwd(dy, c, s, lm):
    t = dy * s
    partner = jnp.where(lm < ROPE_HALF, pltpu.roll(t, LANES - ROPE_HALF, 1),
                        jnp.where(lm < 2 * ROPE_HALF, pltpu.roll(t, ROPE_HALF, 1), 0.0))
    return dy * c + partner


def _lower_bound(lb_logits):
    def body(l_ref, o_ref):
        l0, l1 = l_ref[0:1, :], l_ref[1:2, :]
        m = jnp.maximum(l0, l1)
        e0, e1 = jnp.exp(l0 - m), jnp.exp(l1 - m)
        o_ref[...] = e1 / (e0 + e1)

    return pl.pallas_call(body, name="lower_bound", out_shape=jax.ShapeDtypeStruct((1, lb_logits.shape[1]), F32))(lb_logits)


def _norm_matmul(x, pw, w, bias, name, shards=()):
    n, m = x.shape[0], w.shape[1]
    cn = _col_chunk(m)
    has_bias = bias is not None
    nsh, steps = len(shards), n // TM

    def body(*refs):
        refs = list(refs)
        x_ref, pw_ref, w_ref = refs[:3]
        b_ref = refs[3] if has_bias else None
        refs = refs[4 if has_bias else 3:]
        sh_in, (p_ref, h_ref), sh_out, sems = refs[:nsh], refs[nsh:nsh + 2], refs[nsh + 2:2 * nsh + 2], refs[2 * nsh + 2:]
        if nsh:
            @pl.when(pl.program_id(0) == 0)
            def _():
                _gather_start(sh_in, sh_out, sems)

        xv = x_ref[...]
        r = lax.rsqrt(jnp.mean(xv * xv, axis=-1, keepdims=True) + NORM_EPS)
        h = ((xv * r) * pw_ref[...]).astype(BF16)
        h_ref[...] = h
        for j in range(0, m, cn):
            acc = jnp.dot(h, w_ref[:, j:j + cn], preferred_element_type=F32)
            if has_bias:
                acc = acc + b_ref[:, j:j + cn]
            p_ref[:, j:j + cn] = acc

        if nsh:
            @pl.when(pl.program_id(0) == steps - 1)
            def _():
                _gather_wait(sh_in, sh_out, sems)

    rows = pl.BlockSpec((TM, D_MODEL), lambda i: (i, 0))
    const = lambda shape: pl.BlockSpec(shape, lambda i: (0, 0))
    hbm = pl.BlockSpec(memory_space=pl.ANY)
    in_specs = [rows, const((1, D_MODEL)), const((D_MODEL, m))] + ([const((1, m))] if has_bias else []) + [hbm] * nsh
    args = (x, pw, w) + ((bias,) if has_bias else ()) + tuple(shards)
    return pl.pallas_call(
        body, name=name, grid=(steps,), in_specs=in_specs,
        out_specs=[pl.BlockSpec((TM, m), lambda i: (i, 0)), rows] + [hbm] * nsh,
        out_shape=[jax.ShapeDtypeStruct((n, m), F32), jax.ShapeDtypeStruct((n, D_MODEL), BF16)] + _gather_shapes(shards),
        scratch_shapes=_gather_sems(nsh) if nsh else [],
        compiler_params=_params(("arbitrary",), 56),
    )(*args)


def _outproj_postnorm(g, w, bias, xres, qw, name, shards=()):
    n = g.shape[0]
    nsh, steps = len(shards), n // TM

    def body(g_ref, w_ref, b_ref, x_ref, qw_ref, *rest):
        sh_in, (y_ref, o_ref), sh_out, sems = rest[:nsh], rest[nsh:nsh + 2], rest[nsh + 2:2 * nsh + 2], rest[2 * nsh + 2:]
        if nsh:
            @pl.when(pl.program_id(0) == 0)
            def _():
                _gather_start(sh_in, sh_out, sems)

        y = jnp.dot(g_ref[...], w_ref[...], preferred_element_type=F32) + b_ref[...]
        y_ref[...] = y
        r = lax.rsqrt(jnp.mean(y * y, axis=-1, keepdims=True) + NORM_EPS)
        o_ref[...] = x_ref[...] + (y * r) * qw_ref[...]

        if nsh:
            @pl.when(pl.program_id(0) == steps - 1)
            def _():
                _gather_wait(sh_in, sh_out, sems)

    rows = pl.BlockSpec((TM, D_MODEL), lambda i: (i, 0))
    const = lambda shape: pl.BlockSpec(shape, lambda i: (0, 0))
    hbm = pl.BlockSpec(memory_space=pl.ANY)
    return pl.pallas_call(
        body, name=name, grid=(steps,),
        in_specs=[rows, const((D_MODEL, D_MODEL)), const((1, D_MODEL)), rows, const((1, D_MODEL))] + [hbm] * nsh,
        out_specs=[rows, rows] + [hbm] * nsh,
        out_shape=[jax.ShapeDtypeStruct((n, D_MODEL), F32)] * 2 + _gather_shapes(shards),
        scratch_shapes=_gather_sems(nsh) if nsh else [], compiler_params=_params(("arbitrary",), 48),
    )(g, w, bias, xres, qw, *shards)


def _outproj_loss_bwd(g, w, xres, qw, tgt, name):
    n = g.shape[0]
    steps = n // TM

    def body(g_ref, w_ref, x_ref, qw_ref, t_ref, dx_ref, dg_ref, dy_ref, dqw_ref, loss_ref, acc_ref):
        i = pl.program_id(0)

        @pl.when(i == 0)
        def _():
            acc_ref[...] = jnp.zeros_like(acc_ref)
            dqw_ref[...] = jnp.zeros_like(dqw_ref)

        y = jnp.dot(g_ref[...], w_ref[...], preferred_element_type=F32)
        r = lax.rsqrt(jnp.mean(y * y, axis=-1, keepdims=True) + NORM_EPS)
        u = y * r
        e = (x_ref[...] + u * qw_ref[...]) - t_ref[...]
        dxn = e * (1.0 / D_MODEL)
        dx_ref[...] = dxn
        acc_ref[...] += jnp.sum(e * e, axis=0, keepdims=True)
        du = dxn * qw_ref[...]
        dy = (r * (du - u * jnp.mean(du * u, axis=-1, keepdims=True))).astype(BF16)
        dqw_ref[...] += jnp.sum(dxn * u, axis=0, keepdims=True)
        dy_ref[...] = dy
        dg_ref[...] = lax.dot_general(dy, w_ref[...], NT_DIMS, preferred_element_type=F32)

        @pl.when(i == steps - 1)
        def _():
            loss_ref[...] = jnp.full(loss_ref.shape, jnp.sum(acc_ref[...]) * (0.5 / D_MODEL), F32)

    rows = pl.BlockSpec((TM, D_MODEL), lambda i: (i, 0))
    const = lambda shape: pl.BlockSpec(shape, lambda i: (0, 0))
    return pl.pallas_call(
        body, name=name, grid=(steps,),
        in_specs=[rows, const((D_MODEL, D_MODEL)), rows, const((1, D_MODEL)), rows],
        out_specs=[rows, rows, rows, const((1, D_MODEL)), const((SUBLANES, LANES))],
        out_shape=[jax.ShapeDtypeStruct((n, D_MODEL), F32), jax.ShapeDtypeStruct((n, D_MODEL), F32),
                   jax.ShapeDtypeStruct((n, D_MODEL), BF16), jax.ShapeDtypeStruct((1, D_MODEL), F32),
                   jax.ShapeDtypeStruct((SUBLANES, LANES), F32)],
        scratch_shapes=[pltpu.VMEM((1, D_MODEL), F32)], compiler_params=_params(("arbitrary",), 48),
    )(g, w, xres, qw, tgt)


_QCOL, _ZCOL, _KCOL, _VCOL = 0, 1024, 2048, 2176


def _head_stack(chunks, heads, lt64):
    return jnp.concatenate([jnp.where(lt64 if n % 2 == 0 else ~lt64, chunks[n // 2], 0.0) for n in heads], axis=0)


def _dup_half(x, h, lt64):
    r = pltpu.roll(x, HEAD_DIM, 1)
    return jnp.where(lt64, x, r) if h == 0 else jnp.where(lt64, r, x)


def _pair_chunk(xt, c2):
    a, b = 2 * c2, 2 * c2 + 1
    return jnp.concatenate([xt[:HEAD_DIM, a * BLK:(a + 1) * BLK], xt[HEAD_DIM:, b * BLK:(b + 1) * BLK]], axis=0).T


def _attn_mask_t(i):
    key = lax.broadcasted_iota(jnp.int32, (2 * BLK, BLK), 0)
    qry = lax.broadcasted_iota(jnp.int32, (2 * BLK, BLK), 1)
    valid = (key > qry) & (key <= qry + BLK) & ((key >= BLK) | (i > 0))
    return jnp.tile(jnp.where(valid, 0.0, -1e30), (1, GROUP))


def _attn_probs_t(s, heads, sink_ref, mask):
    s = s + mask
    head = lax.broadcasted_iota(jnp.int32, (1, len(heads) * BLK), 1) >> 7
    sk = jnp.zeros((1, len(heads) * BLK), F32)
    for j, n in enumerate(heads):
        sk = jnp.where(head == j, sink_ref[0, n], sk)
    m = jnp.maximum(jnp.max(s, axis=0, keepdims=True), sk)
    p = jnp.exp(s - m)
    esk = jnp.exp(sk - m)
    return p, 1.0 / (jnp.sum(p, axis=0, keepdims=True) + esk), esk


def _attn_fwd(p, ct, st, sinks, b_loc, nb, shards):
    n = p.shape[0]
    nsh = len(shards)

    def body(sink_ref, q_ref, z_ref, kc_ref, kp_ref, vc_ref, vp_ref, cc_ref, sc_ref, cp_ref, sp_ref, *rest):
        sh_in, (o_ref, g_ref), sh_out, sems = rest[:nsh], rest[nsh:nsh + 2], rest[nsh + 2:2 * nsh + 2], rest[2 * nsh + 2:]
        b, i = pl.program_id(0), pl.program_id(1)

        @pl.when((b == 0) & (i == 0))
        def _():
            _gather_start(sh_in, sh_out, sems)

        lane = lax.broadcasted_iota(jnp.int32, (BLK, LANES), 1)
        lm = lane & (HEAD_DIM - 1)
        cc, sc = cc_ref[...], sc_ref[...]
        kcat = jnp.concatenate([_rope_apply(kp_ref[...], cp_ref[...], sp_ref[...], lm),
                                _rope_apply(kc_ref[...], cc, sc, lm)], axis=0)
        vcat = jnp.concatenate([vp_ref[...], vc_ref[...]], axis=0)
        qr = [_rope_apply(q_ref[:, c * LANES:(c + 1) * LANES], cc, sc, lm) * (HEAD_DIM ** -0.5) for c in range(8)]
        valid = _attn_mask_t(i)
        lt64, lt64k = lane < HEAD_DIM, lax.broadcasted_iota(jnp.int32, (2 * BLK, LANES), 1) < HEAD_DIM
        for h in range(N_KV):
            heads = list(range(h * GROUP, (h + 1) * GROUP))
            kext, vext = _dup_half(kcat, h, lt64k).astype(BF16), _dup_half(vcat, h, lt64k).astype(BF16)
            qst = _head_stack(qr, heads, lt64).astype(BF16)
            p, inv, _ = _attn_probs_t(lax.dot_general(kext, qst, NT_DIMS, preferred_element_type=F32), heads, sink_ref, valid)
            ot = lax.dot_general(vext, p.astype(BF16), TN_DIMS, preferred_element_type=F32) * inv
            for c2 in range(GROUP // 2):
                oc = _pair_chunk(ot, c2)
                cols = slice((4 * h + c2) * LANES, (4 * h + c2 + 1) * LANES)
                zc = z_ref[:, cols]
                o_ref[:, cols] = oc
                g_ref[:, cols] = (oc * (zc * _sigmoid(zc))).astype(BF16)

        @pl.when((b == b_loc - 1) & (i == nb - 1))
        def _():
            _gather_wait(sh_in, sh_out, sems)

    cur = lambda b, i: b * nb + i
    prev = lambda b, i: b * nb + jnp.maximum(i - 1, 0)
    wide = lambda cb: pl.BlockSpec((BLK, ATTN_WIDTH), lambda b, i: (cur(b, i), cb))
    kv = lambda rowf, cb: pl.BlockSpec((BLK, LANES), lambda b, i: (rowf(b, i), cb))
    hbm = pl.BlockSpec(memory_space=pl.ANY)
    in_specs = [pl.BlockSpec(memory_space=pltpu.SMEM), wide(0), wide(1),
                kv(cur, _KCOL // LANES), kv(prev, _KCOL // LANES), kv(cur, _VCOL // LANES), kv(prev, _VCOL // LANES),
                kv(cur, 0), kv(cur, 0), kv(prev, 0), kv(prev, 0)] + [hbm] * nsh
    return pl.pallas_call(
        body, name="attn_fwd", grid=(b_loc, nb), in_specs=in_specs, out_specs=[wide(0), wide(0)] + [hbm] * nsh,
        out_shape=[jax.ShapeDtypeStruct((n, ATTN_WIDTH), F32), jax.ShapeDtypeStruct((n, ATTN_WIDTH), BF16)] + _gather_shapes(shards),
        scratch_shapes=_gather_sems(nsh), compiler_params=_params(("arbitrary", "arbitrary"), 48),
    )(sinks, p, p, p, p, p, p, ct, st, ct, st, *shards)


def _attn_bwd(p, ct, st, sinks, o, dg, b_loc, nb, parts):
    n = p.shape[0]
    nparts = len(parts)

    def body(sink_ref, q_ref, z_ref, kc_ref, kp_ref, vc_ref, vp_ref, cc_ref, sc_ref, cp_ref, sp_ref, o_ref, dg_ref, *rest):
        part_refs, (dp_ref, ds_ref), land_refs = rest[:nparts], rest[nparts:nparts + 2], rest[nparts + 2:2 * nparts + 2]
        dq_s, dz_s, dk_s, dv_s = rest[2 * nparts + 2:2 * nparts + 6]
        sems = rest[2 * nparts + 6:]
        b, i = pl.program_id(0), pl.program_id(1)

        @pl.when((b == 0) & (i == 0))
        def _():
            _scatter_start(part_refs, land_refs, sems)

        @pl.when((b == b_loc - 1) & (i == nb))
        def _():
            _scatter_wait(part_refs, land_refs, sems)

        lane = lax.broadcasted_iota(jnp.int32, (BLK, LANES), 1)
        lm = lane & (HEAD_DIM - 1)

        @pl.when((b == 0) & (i == 0))
        def _():
            ds_ref[...] = jnp.zeros_like(ds_ref)

        @pl.when(i < nb)
        def _compute():
            cc, sc = cc_ref[...], sc_ref[...]
            kcat = jnp.concatenate([_rope_apply(kp_ref[...], cp_ref[...], sp_ref[...], lm),
                                    _rope_apply(kc_ref[...], cc, sc, lm)], axis=0)
            vcat = jnp.concatenate([vp_ref[...], vc_ref[...]], axis=0)
            qr = [_rope_apply(q_ref[:, c * LANES:(c + 1) * LANES], cc, sc, lm) * (HEAD_DIM ** -0.5) for c in range(8)]
            valid = _attn_mask_t(i)
            lt64, lt64k = lane < HEAD_DIM, lax.broadcasted_iota(jnp.int32, (2 * BLK, LANES), 1) < HEAD_DIM
            do_chunks, doo_chunks, dz_chunks = [], [], []
            for c in range(8):
                cols = slice(c * LANES, (c + 1) * LANES)
                zc, oc, dgc = z_ref[:, cols], o_ref[:, cols], dg_ref[:, cols]
                sg = _sigmoid(zc)
                do_chunks.append(dgc * (zc * sg))
                dz_chunks.append(dgc * oc * (sg * (1.0 + zc * (1.0 - sg))))
                doo_chunks.append(do_chunks[c] * oc)
            dq_chunks = [None] * 8
            dk_h, dv_h, ds_parts = [None] * N_KV, [None] * N_KV, [None] * N_KV
            tile_lane = lax.broadcasted_iota(jnp.int32, (SUBLANES, LANES), 1)
            tile_row = lax.broadcasted_iota(jnp.int32, (SUBLANES, LANES), 0)
            ones8 = jnp.ones((SUBLANES, LANES), BF16)

            def kv_head(h):
                heads = list(range(h * GROUP, (h + 1) * GROUP))
                kext = _dup_half(kcat, h, lt64k)
                kext_bf, kext_t = kext.astype(BF16), kext.T.astype(BF16)
                vext = _dup_half(vcat, h, lt64k).astype(BF16)
                qst = _head_stack(qr, heads, lt64).astype(BF16)
                p_un, inv, esk = _attn_probs_t(lax.dot_general(kext_bf, qst, NT_DIMS, preferred_element_type=F32), heads, sink_ref, valid)
                pn, psink = p_un * inv, esk * inv
                do_bf = _head_stack(do_chunks, heads, lt64).astype(BF16)
                delta = sum(lax.dot_general(ones8, part, NT_DIMS, preferred_element_type=F32)
                            for part in _split3(_head_stack(doo_chunks, heads, lt64)))[0:1, :]
                dpt = lax.dot_general(vext, do_bf, NT_DIMS, preferred_element_type=F32)
                dst = (pn * (dpt - delta)).astype(BF16)
                sink_term = psink * delta
                ds_acc = jnp.zeros((SUBLANES, LANES), F32)
                for j, n in enumerate(heads):
                    val = -jnp.sum(sink_term[:, j * BLK:(j + 1) * BLK])
                    ds_acc = ds_acc + jnp.where((tile_lane == n) & (tile_row == 0), val, 0.0)
                ds_parts[h] = ds_acc
                dqt = jnp.dot(kext_t, dst, preferred_element_type=F32) * (HEAD_DIM ** -0.5)
                dk_ext = jnp.dot(dst, qst, preferred_element_type=F32)
                dv_ext = jnp.dot(pn.astype(BF16), do_bf, preferred_element_type=F32)
                dk_h[h] = dk_ext + pltpu.roll(dk_ext, HEAD_DIM, 1)
                dv_h[h] = dv_ext + pltpu.roll(dv_ext, HEAD_DIM, 1)
                for c2 in range(GROUP // 2):
                    dq_chunks[4 * h + c2] = _rope_bwd(_pair_chunk(dqt, c2), cc, sc, lm)

            for h in range(N_KV):
                kv_head(h)
            ds_ref[...] += ds_parts[0] + ds_parts[1]
            dk_full = jnp.where(lt64k, dk_h[0], dk_h[1])
            dv_full = jnp.where(lt64k, dv_h[0], dv_h[1])

            @pl.when(i >= 1)
            def _emit():
                dp_ref[:, _QCOL:_QCOL + ATTN_WIDTH] = dq_s[...]
                dp_ref[:, _ZCOL:_ZCOL + ATTN_WIDTH] = dz_s[...]
                dp_ref[:, _KCOL:_KCOL + KV_WIDTH] = _rope_bwd(dk_s[...] + dk_full[:BLK], cp_ref[...], sp_ref[...], lm)
                dp_ref[:, _VCOL:_VCOL + KV_WIDTH] = dv_s[...] + dv_full[:BLK]

            for c in range(8):
                dq_s[:, c * LANES:(c + 1) * LANES] = dq_chunks[c]
                dz_s[:, c * LANES:(c + 1) * LANES] = dz_chunks[c]
            dk_s[...] = dk_full[BLK:]
            dv_s[...] = dv_full[BLK:]

        @pl.when(i == nb)
        def _final():
            dp_ref[:, _QCOL:_QCOL + ATTN_WIDTH] = dq_s[...]
            dp_ref[:, _ZCOL:_ZCOL + ATTN_WIDTH] = dz_s[...]
            dp_ref[:, _KCOL:_KCOL + KV_WIDTH] = _rope_bwd(dk_s[...], cc_ref[...], sc_ref[...], lm)
            dp_ref[:, _VCOL:_VCOL + KV_WIDTH] = dv_s[...]

    cur = lambda b, i: b * nb + jnp.minimum(i, nb - 1)
    prev = lambda b, i: b * nb + jnp.maximum(jnp.minimum(i, nb - 1) - 1, 0)
    emit = lambda b, i: b * nb + jnp.maximum(i - 1, 0)
    hbm = pl.BlockSpec(memory_space=pl.ANY)
    wide = lambda cb: pl.BlockSpec((BLK, ATTN_WIDTH), lambda b, i: (cur(b, i), cb))
    kv = lambda rowf, cb: pl.BlockSpec((BLK, LANES), lambda b, i: (rowf(b, i), cb))
    in_specs = [pl.BlockSpec(memory_space=pltpu.SMEM), wide(0), wide(1),
                kv(cur, _KCOL // LANES), kv(prev, _KCOL // LANES), kv(cur, _VCOL // LANES), kv(prev, _VCOL // LANES),
                kv(cur, 0), kv(cur, 0), kv(prev, 0), kv(prev, 0), wide(0), wide(0)] + [hbm] * nparts
    out_specs = [pl.BlockSpec((BLK, ATTN_IN), lambda b, i: (emit(b, i), 0)),
                 pl.BlockSpec((SUBLANES, LANES), lambda b, i: (0, 0))] + [hbm] * nparts
    return pl.pallas_call(
        body, name="attn_bwd", grid=(b_loc, nb + 1), in_specs=in_specs, out_specs=out_specs,
        out_shape=[jax.ShapeDtypeStruct((n, ATTN_IN), F32), jax.ShapeDtypeStruct((SUBLANES, LANES), F32)] + _scatter_lands(parts),
        scratch_shapes=[pltpu.VMEM((BLK, ATTN_WIDTH), F32), pltpu.VMEM((BLK, ATTN_WIDTH), F32),
                        pltpu.VMEM((BLK, KV_WIDTH), F32), pltpu.VMEM((BLK, KV_WIDTH), F32)] + _scatter_sems(nparts),
        compiler_params=_params(("arbitrary", "arbitrary"), 48),
    )(sinks, p, p, p, p, p, p, ct, st, ct, st, o, dg, *parts)


_CUM_ROWS = 256
HALF = CH // 2
_ROW0 = [SUBLANES * (s // SUBLANES) for s in range(CH)]
_ROW1 = [HALF * (s // HALF + 1) for s in range(CH)]
_ROWS_OF = [_ROW1[s] - _ROW0[s] for s in range(CH)]
_OFF_OF = [sum(_ROWS_OF[:s]) for s in range(CH)]


def _tri(lower):
    r = lax.broadcasted_iota(jnp.int32, (_CUM_ROWS, _CUM_ROWS), 0)
    c = lax.broadcasted_iota(jnp.int32, (_CUM_ROWS, _CUM_ROWS), 1)
    same = (r ^ c) < CH
    return (same & ((c <= r) if lower else (c >= r))).astype(BF16)


def _gates(qp, fp, lb):
    e = jnp.exp(-jnp.abs(fp))
    r = 1.0 / (1.0 + e)
    sig_neg = jnp.where(fp >= 0, e, 1.0) * r
    sig = jnp.where(fp >= 0, 1.0, e) * r
    g = jnp.log(jnp.maximum(lb + (1.0 - lb) * sig, F32_TINY))
    return qp * _sigmoid(qp), g, (1.0 - lb) * sig_neg, sig_neg


def _pair_rows(bc, s):
    diff = bc[_ROW0[s]:_ROW1[s], :] - bc[s:s + 1, :]
    head = jnp.minimum(diff[:SUBLANES, :], 0.0)
    return jnp.exp(head if diff.shape[0] == SUBLANES else jnp.concatenate([head, diff[SUBLANES:, :]], axis=0))


def _cross_half(q, k, bc):
    r = bc[HALF - 1:HALF, :]
    e1, e0 = jnp.exp(bc[HALF:, :] - r), jnp.exp(r - bc[:HALF, :])
    return q[HALF:, :] * e1, e1, k[:HALF, :] * e0, e0


HP = 8
REC_TB = 256
_HW = HP * REC_K


def _hgrn_specs(tb, nt, reverse):
    tmap = (lambda t: nt - 1 - t) if reverse else (lambda t: t)
    groups = REC_HEADS // HP
    blk = lambda cb: pl.BlockSpec((tb, _HW), lambda h, b, t: (b * nt + tmap(t), cb * groups + h))
    head = pl.BlockSpec((tb, _HW), lambda h, b, t: (b * nt + tmap(t), h))
    lbs = pl.BlockSpec((1, _HW), lambda h, b, t: (0, h))
    gws = pl.BlockSpec((1, REC_K), lambda h, b, t: (0, 0))
    hist = pl.BlockSpec((HP, 1, tb // CH, REC_K, REC_K), lambda h, b, t: (h, b, tmap(t), 0, 0))
    return blk, head, lbs, gws, hist


def _chunk_rows(c, first=0, size=CH):
    start = c * CH + first
    return pl.ds(start if isinstance(start, int) else pl.multiple_of(start, CH if first % CH == 0 else SUBLANES), size)


def _cumsum_chunks(tri, x, out_ref, tb):
    for r in range(0, tb, _CUM_ROWS):
        out_ref[r:r + _CUM_ROWS, :] = _dot3(tri, x[r:r + _CUM_ROWS, :])


def _hgrn_fwd(p, lb, gw, b_loc, t_len):
    n = p.shape[0]
    tb = min(REC_TB, t_len)
    nt, nck = t_len // tb, tb // CH

    def body(qp_ref, fp_ref, i_ref, z_ref, lb_ref, gw_ref, oraw_ref, g_ref, sh_ref, q_s, k_s, b_s, o_s, st_ref,
             car_o, car_a, car_s, car_st):
        @pl.when(pl.program_id(2) == 0)
        def _():
            st_ref[...] = jnp.zeros_like(st_ref)

        qv, g, kk, _ = _gates(qp_ref[...], fp_ref[...], lb_ref[...])
        q_s[...] = qv
        k_s[...] = kk
        _cumsum_chunks(_tri(True), g, b_s, tb)
        ones = jnp.ones((REC_K, REC_K), BF16)
        sub = lax.broadcasted_iota(jnp.int32, (SUBLANES, REC_K), 0)

        rows_of = _chunk_rows

        def issue(c, hp):
            rs, cs = rows_of(c), slice(hp * REC_K, (hp + 1) * REC_K)
            q, k, bc, v = q_s[rs, cs], k_s[rs, cs], b_s[rs, cs], i_ref[rs, cs]
            st = st_ref[hp]
            sh_ref[hp, 0, c] = st
            o = lax.dot_general((q * jnp.exp(bc)).astype(BF16), st.astype(BF16), NT_DIMS, preferred_element_type=F32)
            w = jnp.concatenate([q[_ROW0[s]:_ROW1[s], :] * _pair_rows(bc, s) * k[s:s + 1, :] for s in range(CH)], axis=0)
            a = jnp.dot(w.astype(BF16), ones, preferred_element_type=F32)
            qe1, _, ke0, _ = _cross_half(q, k, bc)
            s10 = lax.dot_general(qe1.astype(BF16), ke0.astype(BF16), NT_DIMS, preferred_element_type=F32)
            kd = k * jnp.exp(bc[CH - 1:CH, :] - bc)
            st_new = lax.dot_general(v.astype(BF16), kd.astype(BF16), TN_DIMS, preferred_element_type=F32)
            return o, a, s10, st_new

        def advance_state(c, hp, st_new):
            bl = b_s[_chunk_rows(c, CH - SUBLANES, SUBLANES), hp * REC_K:(hp + 1) * REC_K][SUBLANES - 1:, :]
            st_ref[hp] = st_ref[hp] * jnp.exp(bl) + st_new

        def cross(c, hp, s10):
            v0 = i_ref[_chunk_rows(c, 0, HALF), hp * REC_K:(hp + 1) * REC_K]
            return jnp.dot(s10.astype(BF16), v0.astype(BF16), preferred_element_type=F32)

        def finish(c, hp, o, a, o_cross):
            rs, cs = rows_of(c), slice(hp * REC_K, (hp + 1) * REC_K)
            v = i_ref[rs, cs]
            acc = [jnp.zeros((SUBLANES, REC_K), F32) for _ in range(CH // SUBLANES)]
            for s in range(CH):
                j = s // SUBLANES
                vs = v[s:s + 1, :]
                for jj in range(j, _ROW1[s] // SUBLANES):
                    blk = a[_OFF_OF[s] + (jj - j) * SUBLANES:_OFF_OF[s] + (jj - j + 1) * SUBLANES, :]
                    if jj == j:
                        blk = jnp.where(sub >= s - j * SUBLANES, blk, 0.0)
                    acc[jj] = acc[jj] + blk * vs
            o_s[rs, cs] = o + jnp.concatenate(acc, axis=0) + jnp.concatenate([jnp.zeros((HALF, REC_K), F32), o_cross], axis=0)

        def park(slot, results):
            for hp, (o, a, s10, st_new) in enumerate(results):
                car_o[slot, hp], car_a[slot, hp], car_s[slot, hp], car_st[slot, hp] = o, a, s10, st_new

        def retire(c, slot):
            for hp in range(HP):
                advance_state(c, hp, car_st[slot, hp])
            yield
            crosses = [cross(c, hp, car_s[slot, hp]) for hp in range(HP)]
            for hp in range(HP):
                finish(c, hp, car_o[slot, hp], car_a[slot, hp], crosses[hp])

        def step(c, slot):
            closing = retire(c - 1, slot)
            next(closing)
            park(1 - slot, [issue(c, hp) for hp in range(HP)])
            next(closing, None)

        def trip(j, carry):
            step(2 * j + 1, 0)
            step(2 * j + 2, 1)
            return carry

        assert nck % 2 == 0
        park(0, [issue(0, hp) for hp in range(HP)])
        lax.fori_loop(0, nck // 2 - 1, trip, 0)
        step(nck - 1, 0)
        for _ in retire(nck - 1, 1):
            pass
        oraw_ref[...] = o_s[...]
        for hp in range(HP):
            cs = slice(hp * REC_K, (hp + 1) * REC_K)
            o, zc = o_s[:, cs], z_ref[:, cs]
            on = (o * lax.rsqrt(jnp.mean(o * o, axis=-1, keepdims=True) + NORM_EPS)) * gw_ref[...]
            g_ref[:, cs] = (on * (zc * _sigmoid(zc))).astype(BF16)

    blk, head, lbs, gws, hist = _hgrn_specs(tb, nt, False)
    return pl.pallas_call(
        body, name="hgrn_fwd", grid=(REC_HEADS // HP, b_loc, nt),
        in_specs=[blk(0), blk(1), blk(2), blk(3), lbs, gws], out_specs=[head, head, hist],
        out_shape=[jax.ShapeDtypeStruct((n, REC_WIDTH), F32), jax.ShapeDtypeStruct((n, REC_WIDTH), BF16),
                   jax.ShapeDtypeStruct((REC_HEADS, b_loc, t_len // CH, REC_K, REC_K), F32)],
        scratch_shapes=[pltpu.VMEM((tb, _HW), F32)] * 4 + [pltpu.VMEM((HP, REC_K, REC_K), F32)] + [
            pltpu.VMEM((2, HP, CH, REC_K), F32), pltpu.VMEM((2, HP, sum(_ROWS_OF), REC_K), F32),
            pltpu.VMEM((2, HP, HALF, HALF), F32), pltpu.VMEM((2, HP, REC_K, REC_K), F32)],
        compiler_params=_params(("arbitrary", "arbitrary", "arbitrary"), 48),
    )(p, p, p, p, lb, gw)


def _hgrn_bwd(p, lb, gw, oraw, sh, dg, b_loc, t_len):
    n = p.shape[0]
    tb = min(REC_TB, t_len)
    nt, nck = t_len // tb, tb // CH
    assert HP == REC_HEADS

    def body(qp_ref, fp_ref, i_ref, z_ref, lb_ref, gw_ref, oraw_ref, dg_ref, sh_ref,
             dp_ref, dlb_ref, dgw_ref,
             q_s, k_s, b_s, do_s, dqv_s, dk_s, db_s, dst_ref, car_r, car_dst, car_dec, car_a, car_da, car_x):
        dq_ref, df_ref, di_ref, dz_ref = (dp_ref.at[:, part * REC_WIDTH:(part + 1) * REC_WIDTH] for part in range(4))
        b, t = pl.program_id(1), pl.program_id(2)

        @pl.when(t == 0)
        def _():
            dst_ref[...] = jnp.zeros_like(dst_ref)

        @pl.when((b == 0) & (t == 0))
        def _():
            dlb_ref[...] = jnp.zeros_like(dlb_ref)
            dgw_ref[...] = jnp.zeros_like(dgw_ref)

        lbv, qp, fp = lb_ref[...], qp_ref[...], fp_ref[...]
        qv, g, kk, sig_neg = _gates(qp, fp, lbv)
        q_s[...] = qv
        k_s[...] = kk
        _cumsum_chunks(_tri(True), g, b_s, tb)
        gwv = gw_ref[...]
        for hp in range(HP):
            cs = slice(hp * REC_K, (hp + 1) * REC_K)
            o, zc, dgv = oraw_ref[:, cs], z_ref[:, cs], dg_ref[:, cs]
            rn = lax.rsqrt(jnp.mean(o * o, axis=-1, keepdims=True) + NORM_EPS)
            on = o * rn
            sgz = _sigmoid(zc)
            dz_ref[:, cs] = (dgv * (on * gwv) * (sgz * (1.0 + zc * (1.0 - sgz)))).astype(BF16)
            dpre = dgv * (zc * sgz)
            dgw_ref[hp] += jnp.sum(dpre * on, axis=0, keepdims=True)
            don = dpre * gwv
            do_s[:, cs] = rn * (don - on * jnp.mean(don * on, axis=-1, keepdims=True))

        ones = jnp.ones((REC_K, REC_K), BF16)
        sub = lax.broadcasted_iota(jnp.int32, (SUBLANES, REC_K), 0)
        rowid = lax.broadcasted_iota(jnp.int32, (CH, REC_K), 0)
        ngrp = CH // SUBLANES
        piece_row = lax.broadcasted_iota(jnp.int32, (1, sum(_ROWS_OF)), 1)
        key_of = jnp.zeros((1, sum(_ROWS_OF)), jnp.int32)
        for s in range(1, CH):
            key_of = jnp.where(piece_row >= _OFF_OF[s], s, key_of)
        pick = (key_of == lax.broadcasted_iota(jnp.int32, (CH, sum(_ROWS_OF)), 0)).astype(BF16)

        def operands(c, hp):
            rs, cs = _chunk_rows(c), slice(hp * REC_K, (hp + 1) * REC_K)
            return rs, cs, q_s[rs, cs], k_s[rs, cs], b_s[rs, cs], i_ref[rs, cs], do_s[rs, cs]

        def issue(c, hp, slot):
            _, _, q, k, bc, v, do = operands(c, hp)
            st, dst = sh_ref[hp, 0, c], dst_ref[hp]
            qe, kd = q * jnp.exp(bc), k * jnp.exp(bc[CH - 1:CH, :] - bc)
            do_bf, dst_bf = do.astype(BF16), dst.astype(BF16)
            car_r[slot, hp, 0:CH] = jnp.dot(do_bf, st.astype(BF16), preferred_element_type=F32)
            car_r[slot, hp, CH:2 * CH] = jnp.dot(v.astype(BF16), dst_bf, preferred_element_type=F32)
            car_r[slot, hp, 2 * CH:3 * CH] = lax.dot_general(kd.astype(BF16), dst_bf, NT_DIMS, preferred_element_type=F32)
            car_dst[slot, hp] = lax.dot_general(do_bf, qe.astype(BF16), TN_DIMS, preferred_element_type=F32)
            dec = jnp.concatenate([_pair_rows(bc, s) for s in range(CH)], axis=0)
            qk = jnp.concatenate([q[_ROW0[s]:_ROW1[s], :] * k[s:s + 1, :] for s in range(CH)], axis=0)
            x = jnp.concatenate([do[_ROW0[s]:_ROW1[s], :] * v[s:s + 1, :] for s in range(CH)], axis=0)
            car_dec[slot, hp] = dec
            car_a[slot, hp] = jnp.dot((qk * dec).astype(BF16), ones, preferred_element_type=F32)
            car_da[slot, hp] = jnp.dot(x.astype(BF16), ones, preferred_element_type=F32)
            qe1, _, ke0, _ = _cross_half(q, k, bc)
            qe1_bf, ke0_bf = qe1.astype(BF16), ke0.astype(BF16)
            do1_bf, v0_bf = do[HALF:, :].astype(BF16), v[:HALF, :].astype(BF16)
            car_x[slot, hp, 0:HALF] = lax.dot_general(ke0_bf, qe1_bf, NT_DIMS, preferred_element_type=F32)
            car_x[slot, hp, HALF:2 * HALF] = lax.dot_general(do1_bf, v0_bf, NT_DIMS, preferred_element_type=F32)
            car_x[slot, hp, 2 * HALF:3 * HALF] = lax.dot_general(v0_bf, do1_bf, NT_DIMS, preferred_element_type=F32)

        def advance_state(c, hp, slot):
            ebl = jnp.exp(b_s[_chunk_rows(c, CH - SUBLANES, SUBLANES), hp * REC_K:(hp + 1) * REC_K][SUBLANES - 1:, :])
            st, dst = sh_ref[hp, 0, c], dst_ref[hp]
            dst_ref[hp] = dst * ebl + car_dst[slot, hp]
            return ebl * jnp.sum(st * dst, axis=0, keepdims=True)

        def cross(c, hp, slot):
            _, _, q, k, bc, v, do = operands(c, hp)
            qe1, _, ke0, _ = _cross_half(q, k, bc)
            xs = car_x[slot, hp]
            dqe1 = jnp.dot(xs[HALF:2 * HALF].astype(BF16), ke0.astype(BF16), preferred_element_type=F32)
            dke0 = jnp.dot(xs[2 * HALF:].astype(BF16), qe1.astype(BF16), preferred_element_type=F32)
            dv1 = jnp.dot(xs[:HALF].astype(BF16), do[HALF:, :].astype(BF16), preferred_element_type=F32)
            return dqe1, dke0, dv1

        def retire(c, slot):
            dbl_state = [advance_state(c, hp, slot) for hp in range(HP)]
            yield
            crossed = [cross(c, hp, slot) for hp in range(HP)]
            for hp in range(HP):
                finish(c, hp, slot, dbl_state[hp], *crossed[hp])

        def step(c, slot):
            closing = retire(c + 1, slot)
            next(closing)
            for hp in range(HP):
                issue(c, hp, 1 - slot)
            next(closing, None)

        def trip(j, carry):
            step(nck - 2 - 2 * j, 0)
            step(nck - 3 - 2 * j, 1)
            return carry

        def finish(c, hp, slot, dbl_state, dqe1, dke0, dv1):
            rs, cs, q, k, bc, v, do = operands(c, hp)
            eb, ekd = jnp.exp(bc), jnp.exp(bc[CH - 1:CH, :] - bc)
            qe, kd = q * eb, k * ekd
            qe1, e1, ke0, e0 = _cross_half(q, k, bc)
            dqe, dkd, dv = car_r[slot, hp, 0:CH], car_r[slot, hp, CH:2 * CH], car_r[slot, hp, 2 * CH:3 * CH]
            a, da, decs = car_a[slot, hp], car_da[slot, hp], car_dec[slot, hp]
            dec = [decs[_OFF_OF[s]:_OFF_OF[s] + _ROWS_OF[s], :] for s in range(CH)]
            dbl = jnp.sum(dkd * kd, axis=0, keepdims=True) + dbl_state
            dq_acc = [jnp.zeros((SUBLANES, REC_K), F32) for _ in range(ngrp)]
            uk, uv = [], []
            for s in range(CH):
                j = s // SUBLANES
                r0 = j * SUBLANES
                ks = k[s:s + 1, :]
                for jj in range(j, _ROW1[s] // SUBLANES):
                    lo, hi = _OFF_OF[s] + (jj - j) * SUBLANES, _OFF_OF[s] + (jj - j + 1) * SUBLANES
                    a_blk, da_blk = a[lo:hi, :], da[lo:hi, :]
                    if jj == j:
                        keep = sub >= s - r0
                        a_blk, da_blk = jnp.where(keep, a_blk, 0.0), jnp.where(keep, da_blk, 0.0)
                    rows = slice(jj * SUBLANES, (jj + 1) * SUBLANES)
                    tt = da_blk * dec[s][(jj - j) * SUBLANES:(jj - j + 1) * SUBLANES, :]
                    dq_acc[jj] = dq_acc[jj] + tt * ks
                    uk.append(tt * q[rows, :])
                    uv.append(a_blk * do[rows, :])
            dk_in = jnp.dot(pick, jnp.concatenate(uk, axis=0).astype(BF16), preferred_element_type=F32)
            dv_in = jnp.dot(pick, jnp.concatenate(uv, axis=0).astype(BF16), preferred_element_type=F32)
            zero_half = jnp.zeros((HALF, REC_K), F32)
            dq_x = jnp.concatenate([zero_half, dqe1 * e1], axis=0)
            dk_x = jnp.concatenate([dke0 * e0, zero_half], axis=0)
            dv_x = jnp.concatenate([dv1, zero_half], axis=0)
            db_x = jnp.concatenate([-(dke0 * ke0), dqe1 * qe1], axis=0)
            dq_in = jnp.concatenate(dq_acc, axis=0)
            dqv_s[rs, cs] = dqe * eb + dq_in + dq_x
            dk_s[rs, cs] = dkd * ekd + dk_in + dk_x
            di_ref[rs, cs] = (dv + dv_in + dv_x).astype(BF16)
            db = dqe * qe - dkd * kd + q * dq_in - k * dk_in + db_x
            db_s[rs, cs] = db + jnp.where(rowid == CH - 1, dbl, 0.0)

        assert nck % 2 == 0
        for hp in range(HP):
            issue(nck - 1, hp, 0)
        lax.fori_loop(0, nck // 2 - 1, trip, 0)
        step(0, 0)
        for _ in retire(0, 1):
            pass
        up = _tri(False)
        sgq = _sigmoid(qp)
        dq_ref[...] = (dqv_s[...] * (sgq * (1.0 + qp * (1.0 - sgq)))).astype(BF16)
        dlb_acc = jnp.zeros((1, _HW), F32)
        for r in range(0, tb, _CUM_ROWS):
            rows = slice(r, r + _CUM_ROWS)
            dgl = _dot3(up, db_s[rows, :])
            dfg = dgl * jnp.exp(-g[rows, :]) - dk_s[rows, :]
            sn = sig_neg[rows, :]
            df_ref[rows, :] = (dfg * (1.0 - lbv) * (1.0 - sn) * sn).astype(BF16)
            dlb_acc = dlb_acc + jnp.sum(dfg * sn, axis=0, keepdims=True)
        dlb_ref[...] += dlb_acc

    blk, head, lbs, gws, hist = _hgrn_specs(tb, nt, True)
    out_specs = [pl.BlockSpec((tb, REC_IN), lambda h, b, t: (b * nt + nt - 1 - t, 0)), lbs,
                 pl.BlockSpec((HP, 1, REC_K), lambda h, b, t: (h, 0, 0))]
    out_shape = [jax.ShapeDtypeStruct((n, REC_IN), BF16),
                 jax.ShapeDtypeStruct((1, REC_WIDTH), F32), jax.ShapeDtypeStruct((REC_HEADS, 1, REC_K), F32)]
    return pl.pallas_call(
        body, name="hgrn_bwd", grid=(REC_HEADS // HP, b_loc, nt),
        in_specs=[blk(0), blk(1), blk(2), blk(3), lbs, gws, head, head, hist],
        out_specs=out_specs, out_shape=out_shape,
        scratch_shapes=[pltpu.VMEM((tb, _HW), F32)] * 7 + [pltpu.VMEM((HP, REC_K, REC_K), F32)] + [
            pltpu.VMEM((2, HP, 3 * CH, REC_K), F32), pltpu.VMEM((2, HP, REC_K, REC_K), F32)] + [
            pltpu.VMEM((2, HP, sum(_ROWS_OF), REC_K), F32)] * 3 + [pltpu.VMEM((2, HP, 3 * HALF, HALF), F32)],
        compiler_params=_params(("arbitrary", "arbitrary", "arbitrary"), 56),
    )(p, p, p, p, lb, gw, oraw, dg, sh)


def _postnorm_bwd_nt(dxo, y, qw, w, name):
    n = dxo.shape[0]

    def body(dx_ref, y_ref, qw_ref, w_ref, dg_ref, dy_ref, dqw_ref, db_ref):
        @pl.when(pl.program_id(0) == 0)
        def _():
            dqw_ref[...] = jnp.zeros_like(dqw_ref)
            db_ref[...] = jnp.zeros_like(db_ref)

        yv, dxv = y_ref[...], dx_ref[...]
        r = lax.rsqrt(jnp.mean(yv * yv, axis=-1, keepdims=True) + NORM_EPS)
        u = yv * r
        du = dxv * qw_ref[...]
        dy = r * (du - u * jnp.mean(du * u, axis=-1, keepdims=True))
        dqw_ref[...] += jnp.sum(dxv * u, axis=0, keepdims=True)
        db_ref[...] += jnp.sum(dy, axis=0, keepdims=True)
        dyb = dy.astype(BF16)
        dy_ref[...] = dyb
        dg_ref[...] = lax.dot_general(dyb, w_ref[...], NT_DIMS, preferred_element_type=F32)

    rows = pl.BlockSpec((TM, D_MODEL), lambda i: (i, 0))
    const = lambda shape: pl.BlockSpec(shape, lambda i: (0, 0))
    return pl.pallas_call(
        body, name=name, grid=(n // TM,), in_specs=[rows, rows, const((1, D_MODEL)), const((D_MODEL, D_MODEL))],
        out_specs=[rows, rows, const((1, D_MODEL)), const((1, D_MODEL))],
        out_shape=[jax.ShapeDtypeStruct((n, D_MODEL), F32), jax.ShapeDtypeStruct((n, D_MODEL), BF16),
                   jax.ShapeDtypeStruct((1, D_MODEL), F32), jax.ShapeDtypeStruct((1, D_MODEL), F32)],
        compiler_params=_params(("arbitrary",), 48),
    )(dxo, y, qw, w)


def _nt_prenorm_bwd(dps, w, x, pw, dxo, has_bias, name, parts=()):
    n = x.shape[0]
    widths = [d.shape[1] for d in dps]
    m = sum(widths)
    npieces, nparts, steps = len(dps), len(parts), n // TM

    def body(*refs):
        dp_refs = refs[:npieces]
        w_ref, x_ref, pw_ref, dxo_ref = refs[npieces:npieces + 4]
        part_refs = refs[npieces + 4:npieces + 4 + nparts]
        dx_ref, dpw_ref, db_ref = refs[npieces + 4 + nparts:npieces + 7 + nparts]
        land_refs = refs[npieces + 7 + nparts:npieces + 7 + 2 * nparts]
        sems = refs[npieces + 7 + 2 * nparts:]

        @pl.when(pl.program_id(0) == 0)
        def _():
            dpw_ref[...] = jnp.zeros_like(dpw_ref)
            db_ref[...] = jnp.zeros_like(db_ref)
            if nparts:
                _scatter_start(part_refs, land_refs, sems)

        dh = jnp.zeros((TM, D_MODEL), F32)
        off = 0
        for dp_ref, wd in zip(dp_refs, widths):
            cn = _col_chunk(wd)
            for j in range(0, wd, cn):
                dpc = dp_ref[:, j:j + cn]
                if has_bias:
                    db_ref[:, off + j:off + j + cn] += jnp.sum(dpc, axis=0, keepdims=True)
                dh = dh + lax.dot_general(dpc.astype(BF16), w_ref[:, off + j:off + j + cn], NT_DIMS, preferred_element_type=F32)
            off += wd
        xv = x_ref[...]
        r = lax.rsqrt(jnp.mean(xv * xv, axis=-1, keepdims=True) + NORM_EPS)
        xn = xv * r
        dpw_ref[...] += jnp.sum(dh * xn, axis=0, keepdims=True)
        dxn = dh * pw_ref[...]
        dx_ref[...] = dxo_ref[...] + r * (dxn - xn * jnp.mean(dxn * xn, axis=-1, keepdims=True))

        if nparts:
            @pl.when(pl.program_id(0) == steps - 1)
            def _():
                _scatter_wait(part_refs, land_refs, sems)

    rows = pl.BlockSpec((TM, D_MODEL), lambda i: (i, 0))
    const = lambda shape: pl.BlockSpec(shape, lambda i: (0, 0))
    hbm = pl.BlockSpec(memory_space=pl.ANY)
    in_specs = ([pl.BlockSpec((TM, wd), lambda i: (i, 0)) for wd in widths] + [const((D_MODEL, m)), rows, const((1, D_MODEL)), rows]
                + [hbm] * nparts)
    return pl.pallas_call(
        body, name=name, grid=(steps,), in_specs=in_specs,
        out_specs=[rows, const((1, D_MODEL)), const((1, m))] + [hbm] * nparts,
        out_shape=[jax.ShapeDtypeStruct((n, D_MODEL), F32), jax.ShapeDtypeStruct((1, D_MODEL), F32),
                   jax.ShapeDtypeStruct((1, m), F32)] + _scatter_lands(parts),
        scratch_shapes=_scatter_sems(nparts) if nparts else [],
        compiler_params=_params(("arbitrary",), 56),
    )(*dps, w, x, pw, dxo, *parts)


def _matmul_tn(a, b, name):
    n, k = a.shape
    m = b.shape[1]
    tk, tm, tn = k, _col_chunk(m), TN_ROWS if n % TN_ROWS == 0 else n

    def body(a_ref, b_ref, o_ref):
        @pl.when(pl.program_id(2) == 0)
        def _():
            o_ref[...] = jnp.zeros_like(o_ref)

        o_ref[...] += lax.dot_general(a_ref[...], b_ref[...].astype(BF16), TN_DIMS, preferred_element_type=F32)

    return pl.pallas_call(
        body, name=name, grid=(k // tk, m // tm, n // tn),
        in_specs=[pl.BlockSpec((tn, tk), lambda i, j, l: (l, i)), pl.BlockSpec((tn, tm), lambda i, j, l: (l, j))],
        out_specs=pl.BlockSpec((tk, tm), lambda i, j, l: (i, j)),
        out_shape=jax.ShapeDtypeStruct((k, m), F32),
        compiler_params=_params(("arbitrary", "arbitrary", "arbitrary"), 48),
    )(a, b)


def _matmul_tn_by_owner(a, b, name):
    n, k = a.shape
    c = b.shape[1] // N_DEV
    tn = TN_ROWS if n % TN_ROWS == 0 else n
    steps = n // tn
    per = 2

    def body(a_ref, b_ref, o_ref, w_ref):
        @pl.when(pl.program_id(1) == 0)
        def _():
            o_ref[...] = jnp.zeros_like(o_ref)

        r = lax.dot_general(a_ref[...], b_ref[...], TN_DIMS, preferred_element_type=F32)
        for j in range(per):
            o_ref[j] += r[:, j * c:(j + 1) * c]

        @pl.when(pl.program_id(1) == steps - 1)
        def _():
            w_ref[...] = o_ref[...].astype(BF16)

    out = pl.BlockSpec((per, k, c), lambda j, l: (j, 0, 0))
    return pl.pallas_call(
        body, name=name, grid=(N_DEV // per, steps),
        in_specs=[pl.BlockSpec((tn, k), lambda j, l: (l, 0)), pl.BlockSpec((tn, per * c), lambda j, l: (l, j))],
        out_specs=[out, out],
        out_shape=[jax.ShapeDtypeStruct((N_DEV, k, c), F32), jax.ShapeDtypeStruct((N_DEV, k, c), BF16)],
        compiler_params=_params(("arbitrary", "arbitrary"), 48),
    )(a, b)


def _by_owner_cols(dw):
    k, m = dw.shape
    return dw.reshape(k, N_DEV, m // N_DEV).transpose(1, 0, 2)


def _own_and_bf16(part):
    return lax.dynamic_index_in_dim(part, _my_id(), 0, keepdims=False), part.astype(BF16)


def _step(x, ct, st, tgt, pre_w, post_w, wa_in, ba_in, sinks, wa_out_shard, ba_out, wr_in_shard, lb_logits, gnorm_w, wr_out_shard, b_loc, t_len):
    nb = t_len // BLK
    lb = _lower_bound(lb_logits)
    p0, h0, ga_out = _norm_matmul(x, pre_w[0:1], wa_in, ba_in, "attn_in_proj", [wa_out_shard])
    wa_out = ga_out.reshape(ATTN_WIDTH, D_MODEL)
    half = D_MODEL // 2
    o0, g0, gr_in_top = _attn_fwd(p0, ct, st, sinks, b_loc, nb, [wr_in_shard[:half]])
    y0, x1, gr_in_bot = _outproj_postnorm(g0, wa_out, ba_out, x, post_w[0:1], "attn_out_proj", [wr_in_shard[half:]])
    wr_in = jnp.concatenate([gr_in_top.transpose(1, 0, 2).reshape(half, REC_IN),
                             gr_in_bot.transpose(1, 0, 2).reshape(half, REC_IN)], axis=0)
    p1, h1, gr_out = _norm_matmul(x1, pre_w[1:2], wr_in, None, "rec_in_proj", [wr_out_shard])
    wr_out = gr_out.reshape(REC_WIDTH, D_MODEL)
    o1, g1, sh = _hgrn_fwd(p1, lb, gnorm_w, b_loc, t_len)
    dx2, dg1, dy1, dpost1, loss_tile = _outproj_loss_bwd(g1, wr_out, x1, post_w[1:2], tgt, "rec_out_proj_loss_bwd")
    d_wr_out = _matmul_tn(g1, dy1, "rec_w_out_grad")
    dp1, dlb, dgw = _hgrn_bwd(p1, lb, gnorm_w, o1, sh, dg1, b_loc, t_len)
    dx1, dpre1, _ = _nt_prenorm_bwd([dp1], wr_in, x1, pre_w[1:2], dx2, False, "rec_in_bwd")
    part_r_in, wire_r_in = _matmul_tn_by_owner(h1, dp1, "rec_w_in_grad")
    own_r_in = lax.dynamic_index_in_dim(part_r_in, _my_id(), 0, keepdims=False)
    dg0, dy0, dpost0, dba_out = _postnorm_bwd_nt(dx1, y0, post_w[0:1], wa_out, "attn_out_bwd")
    d_wa_out = _matmul_tn(g0, dy0, "attn_w_out_grad")
    owns, wires = zip(*[_own_and_bf16(part) for part in (
        d_wr_out.reshape(N_DEV, REC_WIDTH // N_DEV, D_MODEL), d_wa_out.reshape(N_DEV, ATTN_WIDTH // N_DEV, D_MODEL))])
    owns, wires = (own_r_in,) + owns, (wire_r_in,) + wires
    dp0, dsink_tile, *lands = _attn_bwd(p0, ct, st, sinks, o0, dg0, b_loc, nb, list(wires))
    d_wa_in = _matmul_tn(h0, dp0, "attn_w_in_grad")
    own_a_in, wire_a_in = _own_and_bf16(_by_owner_cols(_qkvz(d_wa_in)))
    dx0, dpre0, dba_in, land_a_in = _nt_prenorm_bwd([dp0], wa_in, x, pre_w[0:1], dx1, True, "attn_in_bwd", [wire_a_in])
    small = dict(pre=jnp.concatenate([dpre0, dpre1], axis=0), post=jnp.concatenate([dpost0, dpost1], axis=0),
                 ba_in=dba_in, sinks=dsink_tile[0:1, 0:N_HEADS], ba_out=dba_out, lb=dlb, gnorm=jnp.sum(dgw, axis=0))
    return loss_tile, dx0, list(zip(lands, owns)) + [(land_a_in, own_a_in)], small


def _my_id():
    return lax.axis_index("x") * 4 + lax.axis_index("y") * 2 + lax.axis_index("c")


def _peer(k):
    x, y, c = lax.axis_index("x"), lax.axis_index("y"), lax.axis_index("c")
    return (x ^ ((k >> 2) & 1), y ^ ((k >> 1) & 1), c ^ (k & 1))


def _peer_id(k):
    return _my_id() ^ k


def _all_gather_by_chip(shard, pos_col):
    n = pos_col.shape[0]

    def body(x_ref, p_ref, f_ref, s_ref, out_ref, ct_ref, st_ref, send_sems, recv_sems, local_sem):
        x, y, c = lax.axis_index("x"), lax.axis_index("y"), lax.axis_index("c")
        me, sibling = (x, y, c), (x, y, 1 - c)
        chips = [(1 - x, y), (x, 1 - y), (1 - x, 1 - y)]

        def rows(px, py, pc):
            return out_ref.at[4 * px + 2 * py + pc]

        def copy(k, block, to, src=None):
            return pltpu.make_async_remote_copy(src_ref=rows(*block) if src is None else src, dst_ref=rows(*block),
                                                send_sem=send_sems.at[k], recv_sem=recv_sems.at[k], device_id=to, device_id_type=MESH)

        mine = pltpu.make_async_copy(x_ref, rows(*me), local_sem)
        mine.start()
        first = [copy(0, me, sibling, src=x_ref)] + [copy(1 + j, me, (*chip, c), src=x_ref) for j, chip in enumerate(chips)]
        for cp in first:
            cp.start()
        _rope_tables_into(p_ref, f_ref, s_ref, ct_ref, st_ref)
        passed = [copy(4 + j, (*chip, c), sibling) for j, chip in enumerate(chips)]
        for j, chip in enumerate(chips):
            copy(1 + j, (*chip, c), me).wait_recv()
            passed[j].start()
        copy(0, sibling, me).wait_recv()
        for j, chip in enumerate(chips):
            copy(4 + j, (*chip, 1 - c), me).wait_recv()
        for cp in first + passed:
            cp.wait_send()
        mine.wait()

    hbm, vmem = pl.BlockSpec(memory_space=pl.ANY), pl.BlockSpec(memory_space=pltpu.VMEM)
    return pl.pallas_call(
        body, name="comm_all_gather_by_chip", in_specs=[hbm, vmem, vmem, vmem], out_specs=[hbm, vmem, vmem],
        out_shape=[jax.ShapeDtypeStruct((N_DEV,) + shard.shape, shard.dtype)] + [jax.ShapeDtypeStruct((n, LANES), F32)] * 2,
        scratch_shapes=[pltpu.SemaphoreType.DMA((N_DEV - 1,)), pltpu.SemaphoreType.DMA((N_DEV - 1,)), pltpu.SemaphoreType.DMA],
        compiler_params=_params(None, 32),
    )(shard, pos_col, *_rope_lanes())


def _gather_shapes(shards):
    return [jax.ShapeDtypeStruct((N_DEV,) + s.shape, s.dtype) for s in shards]


def _gather_sems(nsh):
    return [pltpu.SemaphoreType.DMA((nsh, N_DEV - 1)), pltpu.SemaphoreType.DMA((nsh, N_DEV - 1)), pltpu.SemaphoreType.DMA((nsh,))]


def _gather_copies(ins, outs, sems, received):
    send_sems, recv_sems, local_sems = sems
    me = _my_id()
    local = [pltpu.make_async_copy(ins[a], outs[a].at[me], local_sems.at[a]) for a in range(len(ins))]
    remote = [pltpu.make_async_remote_copy(
        src_ref=ins[a], dst_ref=outs[a].at[_peer_id(k) if received else me], send_sem=send_sems.at[a, k - 1],
        recv_sem=recv_sems.at[a, k - 1], device_id=_peer(k), device_id_type=MESH)
        for a in range(len(ins)) for k in range(1, N_DEV)]
    return local, remote


def _gather_start(ins, outs, sems):
    local, sends = _gather_copies(ins, outs, sems, False)
    for cp in local + sends:
        cp.start()


def _gather_wait(ins, outs, sems):
    local, recvs = _gather_copies(ins, outs, sems, True)
    for cp in recvs:
        cp.wait_recv()
    for cp in recvs:
        cp.wait_send()
    for cp in local:
        cp.wait()


def _scatter_lands(parts):
    return [jax.ShapeDtypeStruct((N_DEV - 1,) + p.shape[1:], p.dtype) for p in parts]


def _scatter_sems(nparts):
    return [pltpu.SemaphoreType.DMA((nparts, N_DEV - 1)), pltpu.SemaphoreType.DMA((nparts, N_DEV - 1))]


def _scatter_copies(parts, lands, sems):
    send_sems, recv_sems = sems
    return [pltpu.make_async_remote_copy(
        src_ref=parts[a].at[_peer_id(k)], dst_ref=lands[a].at[k - 1], send_sem=send_sems.at[a, k - 1],
        recv_sem=recv_sems.at[a, k - 1], device_id=_peer(k), device_id_type=MESH)
        for a in range(len(parts)) for k in range(1, N_DEV)]


def _scatter_start(parts, lands, sems):
    for cp in _scatter_copies(parts, lands, sems):
        cp.start()


def _scatter_wait(parts, lands, sems):
    copies = _scatter_copies(parts, lands, sems)
    for cp in copies:
        cp.wait_recv()
    for cp in copies:
        cp.wait_send()


def _adamw(w, g, m, v):
    m2 = ADAM_B1 * m + (1.0 - ADAM_B1) * g
    v2 = ADAM_B2 * v + (1.0 - ADAM_B2) * (g * g)
    m_hat = m2 / (1.0 - ADAM_B1 ** ADAM_STEP)
    v_hat = v2 / (1.0 - ADAM_B2 ** ADAM_STEP)
    delta = -ADAM_LR * (m_hat / (jnp.sqrt(v_hat) + ADAM_EPS) + ADAM_WD * w)
    return delta, m2, v2


def _sum_adamw(land, own, w, m, v, name):
    r, c = own.shape
    rb = min(r, 256)

    def body(land_ref, own_ref, w_ref, m_ref, v_ref, g_ref, d_ref, m2_ref, v2_ref):
        me = _my_id()
        g = jnp.zeros((rb, c), F32)
        for dev in range(N_DEV):
            k = dev ^ me
            g = g + jnp.where(k == 0, own_ref[...], land_ref[jnp.maximum(k - 1, 0)].astype(F32))
        delta, m2, v2 = _adamw(w_ref[...], g, m_ref[...], v_ref[...])
        g_ref[...] = g
        d_ref[...] = delta
        m2_ref[...] = m2
        v2_ref[...] = v2

    rows = pl.BlockSpec((rb, c), lambda i: (i, 0))
    return pl.pallas_call(
        body, name=name, grid=(r // rb,), in_specs=[pl.BlockSpec((N_DEV - 1, rb, c), lambda i: (0, i, 0))] + [rows] * 4,
        out_specs=[rows] * 4, out_shape=[jax.ShapeDtypeStruct((r, c), F32)] * 4,
        compiler_params=_params(("arbitrary",), 32),
    )(land, own, w, m, v)


_SMALL = [("pre_norm_w", 2048), ("post_norm_w", 2048), ("attn_b_in", 2304), ("attn_sinks", 16), ("attn_b_out", 1024),
          ("rec_lb_logits", 2048), ("rec_gnorm_w", 128), ("loss", 1)]
_TILE = SUBLANES * LANES


def _small_rows(size):
    return -(-size // _TILE) * SUBLANES


_SMALL_OFF = {}
_r = 0
for _name, _size in _SMALL:
    _SMALL_OFF[_name] = _r
    _r += _small_rows(_size)
_SMALL_ROWS = _r


def _pack_small(pieces):
    out = []
    for name, size in _SMALL:
        flat = pieces[name].reshape(-1).astype(F32)
        out.append(jnp.pad(flat, (0, _small_rows(size) * LANES - size)).reshape(-1, LANES))
    return jnp.concatenate(out, axis=0)


def _unpack_small(packed, shapes):
    return {name: packed[_SMALL_OFF[name]:_SMALL_OFF[name] + _small_rows(size)].reshape(-1)[:size].reshape(shapes[name])
            for name, size in _SMALL}


def _small_allreduce_adamw(gpart, w, m, v):
    lb0 = _SMALL_OFF["rec_lb_logits"]

    def body(gp_ref, w_ref, m_ref, v_ref, g_ref, d_ref, m2_ref, v2_ref, land_ref, send_sems, recv_sems):
        me = _my_id()
        sent = []
        for k in range(1, N_DEV):
            cp = pltpu.make_async_remote_copy(src_ref=gp_ref, dst_ref=land_ref.at[k - 1], send_sem=send_sems.at[k - 1],
                                              recv_sem=recv_sems.at[k - 1], device_id=_peer(k), device_id_type=MESH)
            cp.start()
            sent.append(cp)
        for cp in sent:
            cp.wait_recv()
        for cp in sent:
            cp.wait_send()
        g = jnp.zeros((_SMALL_ROWS, LANES), F32)
        for dev in range(N_DEV):
            k = dev ^ me
            g = g + jnp.where(k == 0, gp_ref[...], land_ref[jnp.maximum(k - 1, 0)])
        g_ref[...] = g
        l0, l1 = w_ref[lb0:lb0 + SUBLANES, :], w_ref[lb0 + SUBLANES:lb0 + 2 * SUBLANES, :]
        mx = jnp.maximum(l0, l1)
        e0, e1 = jnp.exp(l0 - mx), jnp.exp(l1 - mx)
        p1 = e1 / (e0 + e1)
        dl1 = (1.0 - p1) * p1 * g[lb0:lb0 + SUBLANES, :]
        g_ref[lb0:lb0 + SUBLANES, :] = -dl1
        g_ref[lb0 + SUBLANES:lb0 + 2 * SUBLANES, :] = dl1
        delta, m2, v2 = _adamw(w_ref[...], g_ref[...], m_ref[...], v_ref[...])
        d_ref[...] = delta
        m2_ref[...] = m2
        v2_ref[...] = v2

    vmem = pl.BlockSpec(memory_space=pltpu.VMEM)
    return pl.pallas_call(
        body, name="comm_small_allreduce_adamw", in_specs=[vmem] * 4, out_specs=[vmem] * 4,
        out_shape=[jax.ShapeDtypeStruct((_SMALL_ROWS, LANES), F32)] * 4,
        scratch_shapes=[pltpu.VMEM((N_DEV - 1, _SMALL_ROWS, LANES), F32), pltpu.SemaphoreType.DMA((N_DEV - 1,)),
                        pltpu.SemaphoreType.DMA((N_DEV - 1,))],
    )(gpart, w, m, v)


def _qzkv(a):
    kv_end = ATTN_WIDTH + 2 * KV_WIDTH
    return jnp.concatenate([a[..., :ATTN_WIDTH], a[..., kv_end:], a[..., ATTN_WIDTH:kv_end]], axis=-1)


def _qkvz(a):
    return jnp.concatenate([a[..., :ATTN_WIDTH], a[..., 2 * ATTN_WIDTH:], a[..., ATTN_WIDTH:2 * ATTN_WIDTH]], axis=-1)


def kernel(x, positions, pre_norm_w, post_norm_w, attn_w_in, attn_b_in, attn_sinks, attn_w_out, attn_b_out, rec_w_in, rec_lb_logits, rec_gnorm_w, rec_w_out, loss_target, m_pre_norm_w, m_post_norm_w, m_attn_w_in, m_attn_b_in, m_attn_sinks, m_attn_w_out, m_attn_b_out, m_rec_w_in, m_rec_lb_logits, m_rec_gnorm_w, m_rec_w_out, v_pre_norm_w, v_post_norm_w, v_attn_w_in, v_attn_b_in, v_attn_sinks, v_attn_w_out, v_attn_b_out, v_rec_w_in, v_rec_lb_logits, v_rec_gnorm_w, v_rec_w_out):
    b_loc, t_len, _ = x.shape
    n = b_loc * t_len
    ga_in, ct, st = _all_gather_by_chip(attn_w_in[0].astype(BF16), positions.reshape(n, 1).astype(F32))
    wa_in = _qzkv(ga_in.transpose(1, 0, 2).reshape(D_MODEL, ATTN_IN))

    loss_tile, dx, landed, small = _step(
        x.reshape(n, D_MODEL), ct, st, loss_target.reshape(n, D_MODEL),
        pre_norm_w, post_norm_w, wa_in, _qzkv(attn_b_in), attn_sinks, attn_w_out[0].astype(BF16), attn_b_out,
        rec_w_in[0].astype(BF16), rec_lb_logits, rec_gnorm_w, rec_w_out[0].astype(BF16), b_loc, t_len)

    lift = lambda outs: tuple(a[None] for a in outs)
    (l_r_in, o_r_in), (l_r_out, o_r_out), (l_a_out, o_a_out), (l_a_in, o_a_in) = landed
    r_a_in = lift(_sum_adamw(l_a_in, o_a_in, attn_w_in[0], m_attn_w_in[0], v_attn_w_in[0], "adamw_attn_w_in"))
    r_r_in = lift(_sum_adamw(l_r_in, o_r_in, rec_w_in[0], m_rec_w_in[0], v_rec_w_in[0], "adamw_rec_w_in"))
    r_r_out = lift(_sum_adamw(l_r_out, o_r_out, rec_w_out[0], m_rec_w_out[0], v_rec_w_out[0], "adamw_rec_w_out"))
    r_a_out = lift(_sum_adamw(l_a_out, o_a_out, attn_w_out[0], m_attn_w_out[0], v_attn_w_out[0], "adamw_attn_w_out"))

    gsmall = dict(pre_norm_w=small["pre"], post_norm_w=small["post"], attn_b_in=_qkvz(small["ba_in"]), attn_sinks=small["sinks"],
                  attn_b_out=small["ba_out"], rec_lb_logits=jnp.concatenate([small["lb"], jnp.zeros_like(small["lb"])], axis=0),
                  rec_gnorm_w=small["gnorm"], loss=loss_tile[0:1, 0:1])
    nil = jnp.zeros((1, 1), F32)
    wsmall = dict(pre_norm_w=pre_norm_w, post_norm_w=post_norm_w, attn_b_in=attn_b_in, attn_sinks=attn_sinks,
                  attn_b_out=attn_b_out, rec_lb_logits=rec_lb_logits, rec_gnorm_w=rec_gnorm_w, loss=nil)
    msmall = dict(pre_norm_w=m_pre_norm_w, post_norm_w=m_post_norm_w, attn_b_in=m_attn_b_in, attn_sinks=m_attn_sinks,
                  attn_b_out=m_attn_b_out, rec_lb_logits=m_rec_lb_logits, rec_gnorm_w=m_rec_gnorm_w, loss=nil)
    vsmall = dict(pre_norm_w=v_pre_norm_w, post_norm_w=v_post_norm_w, attn_b_in=v_attn_b_in, attn_sinks=v_attn_sinks,
                  attn_b_out=v_attn_b_out, rec_lb_logits=v_rec_lb_logits, rec_gnorm_w=v_rec_gnorm_w, loss=nil)
    shapes = {k: a.shape for k, a in wsmall.items()}
    packed = _small_allreduce_adamw(_pack_small(gsmall), _pack_small(wsmall), _pack_small(msmall), _pack_small(vsmall))
    sg, sd, sm, sv = [_unpack_small(a, shapes) for a in packed]

    big = {"attn_w_in": r_a_in, "attn_w_out": r_a_out, "rec_w_in": r_r_in, "rec_w_out": r_r_out}
    order = ["pre_norm_w", "post_norm_w", "attn_w_in", "attn_b_in", "attn_sinks", "attn_w_out", "attn_b_out", "rec_w_in",
             "rec_lb_logits", "rec_gnorm_w", "rec_w_out"]
    outs = [sg["loss"][0, 0], dx.reshape(b_loc, t_len, D_MODEL)]
    for idx, small_set in enumerate((sg, sd, sm, sv)):
        outs += [big[nm][idx] if nm in big else small_set[nm] for nm in order]
    return tuple(outs)
```

```python
import numpy as np
import jax
import jax.numpy as jnp
from jax import lax
from jax.experimental import pallas as pl
from jax.experimental.pallas import tpu as pltpu

F32, BF16 = jnp.float32, jnp.bfloat16
MESH = pl.DeviceIdType.MESH
N_DEV = 8

D_MODEL = 1024
N_HEADS, HEAD_DIM, N_KV, GROUP = 16, 64, 2, 8
ATTN_WIDTH, KV_WIDTH = 1024, 128
ATTN_IN = 2 * ATTN_WIDTH + 2 * KV_WIDTH
BLK = 128
ROPE_THETA, ROPE_HALF = 500000.0, 8
REC_HEADS, REC_K = 8, 128
REC_WIDTH = REC_HEADS * REC_K
REC_IN = 4 * REC_WIDTH
TN_ROWS = 2048
CH = 32
NORM_EPS = 1e-6
F32_TINY = 1.1754944e-38
ADAM_LR, ADAM_B1, ADAM_B2, ADAM_EPS, ADAM_WD, ADAM_STEP = 0.001, 0.9, 0.999, 1e-08, 0.01, 10

LANES, SUBLANES = 128, 8
TM = 512
NT_DIMS = (((1,), (1,)), ((), ()))
TN_DIMS = (((0,), (0,)), ((), ()))
MB = 2 ** 20


def _params(sem=None, vmem_mb=48, **kw):
    return pltpu.CompilerParams(dimension_semantics=sem, vmem_limit_bytes=vmem_mb * MB, **kw)


def _col_chunk(m):
    return 768 if m % 1024 else 1024


def _sigmoid(x):
    return 1.0 / (1.0 + jnp.exp(-x))


def _split3(x):
    hi = x.astype(BF16)
    r1 = x - hi.astype(F32)
    mid = r1.astype(BF16)
    lo = (r1 - mid.astype(F32)).astype(BF16)
    return hi, mid, lo


def _dot3(l_bf, x):
    hi, mid, lo = _split3(x)
    return (jnp.dot(l_bf, hi, preferred_element_type=F32) + jnp.dot(l_bf, mid, preferred_element_type=F32)
            + jnp.dot(l_bf, lo, preferred_element_type=F32))


def _rope_lanes():
    lane = np.arange(LANES) % HEAD_DIM
    inv = np.float32(ROPE_THETA) ** (-(np.arange(ROPE_HALF, dtype=np.float32) * np.float32(2.0) / np.float32(2 * ROPE_HALF)))
    freq = np.where(lane < 2 * ROPE_HALF, inv[lane % ROPE_HALF], 0.0).astype(np.float32)[None, :]
    sign = np.where(lane < ROPE_HALF, -1.0, np.where(lane < 2 * ROPE_HALF, 1.0, 0.0)).astype(np.float32)[None, :]
    return jnp.asarray(freq), jnp.asarray(sign)


def _rope_tables_into(p_ref, f_ref, s_ref, c_out, s_out):
    def rows(i, carry):
        rs = pl.ds(pl.multiple_of(i * TM, TM), TM)
        ang = p_ref[rs, :] * f_ref[...]
        c_out[rs, :] = jnp.cos(ang)
        s_out[rs, :] = jnp.sin(ang) * s_ref[...]
        return carry

    lax.fori_loop(0, p_ref.shape[0] // TM, rows, 0)


def _rope_apply(xv, c, s, lm):
    partner = jnp.where(lm < ROPE_HALF, pltpu.roll(xv, LANES - ROPE_HALF, 1), pltpu.roll(xv, ROPE_HALF, 1))
    return xv * c + partner * s


def _rope_bwd(dy, c, s, lm):
    t = dy * s
    partner = jnp.where(lm < ROPE_HALF, pltpu.roll(t, LANES - ROPE_HALF, 1),
                        jnp.where(lm < 2 * ROPE_HALF, pltpu.roll(t, ROPE_HALF, 1), 0.0))
    return dy * c + partner


def _lower_bound(lb_logits):
    def body(l_ref, o_ref):
        l0, l1 = l_ref[0:1, :], l_ref[1:2, :]
        m = jnp.maximum(l0, l1)
        e0, e1 = jnp.exp(l0 - m), jnp.exp(l1 - m)
        o_ref[...] = e1 / (e0 + e1)

    return pl.pallas_call(body, name="lower_bound", out_shape=jax.ShapeDtypeStruct((1, lb_logits.shape[1]), F32))(lb_logits)


def _norm_matmul(x, pw, w, bias, name, shards=()):
    n, m = x.shape[0], w.shape[1]
    cn = _col_chunk(m)
    has_bias = bias is not None
    nsh, steps = len(shards), n // TM

    def body(*refs):
        refs = list(refs)
        x_ref, pw_ref, w_ref = refs[:3]
        b_ref = refs[3] if has_bias else None
        refs = refs[4 if has_bias else 3:]
        sh_in, (p_ref, h_ref), sh_out, sems = refs[:nsh], refs[nsh:nsh + 2], refs[nsh + 2:2 * nsh + 2], refs[2 * nsh + 2:]
        if nsh:
            @pl.when(pl.program_id(0) == 0)
            def _():
                _gather_start(sh_in, sh_out, sems)

        xv = x_ref[...]
        r = lax.rsqrt(jnp.mean(xv * xv, axis=-1, keepdims=True) + NORM_EPS)
        h = ((xv * r) * pw_ref[...]).astype(BF16)
        h_ref[...] = h
        for j in range(0, m, cn):
            acc = jnp.dot(h, w_ref[:, j:j + cn], preferred_element_type=F32)
            if has_bias:
                acc = acc + b_ref[:, j:j + cn]
            p_ref[:, j:j + cn] = acc

        if nsh:
            @pl.when(pl.program_id(0) == steps - 1)
            def _():
                _gather_wait(sh_in, sh_out, sems)

    rows = pl.BlockSpec((TM, D_MODEL), lambda i: (i, 0))
    const = lambda shape: pl.BlockSpec(shape, lambda i: (0, 0))
    hbm = pl.BlockSpec(memory_space=pl.ANY)
    in_specs = [rows, const((1, D_MODEL)), const((D_MODEL, m))] + ([const((1, m))] if has_bias else []) + [hbm] * nsh
    args = (x, pw, w) + ((bias,) if has_bias else ()) + tuple(shards)
    return pl.pallas_call(
        body, name=name, grid=(steps,), in_specs=in_specs,
        out_specs=[pl.BlockSpec((TM, m), lambda i: (i, 0)), rows] + [hbm] * nsh,
        out_shape=[jax.ShapeDtypeStruct((n, m), F32), jax.ShapeDtypeStruct((n, D_MODEL), BF16)] + _gather_shapes(shards),
        scratch_shapes=_gather_sems(nsh) if nsh else [],
        compiler_params=_params(("arbitrary",), 56),
    )(*args)


def _outproj_postnorm(g, w, bias, xres, qw, name):
    n = g.shape[0]

    def body(g_ref, w_ref, b_ref, x_ref, qw_ref, y_ref, o_ref):
        y = jnp.dot(g_ref[...], w_ref[...], preferred_element_type=F32) + b_ref[...]
        y_ref[...] = y
        r = lax.rsqrt(jnp.mean(y * y, axis=-1, keepdims=True) + NORM_EPS)
        o_ref[...] = x_ref[...] + (y * r) * qw_ref[...]

    rows = pl.BlockSpec((TM, D_MODEL), lambda i: (i, 0))
    const = lambda shape: pl.BlockSpec(shape, lambda i: (0, 0))
    return pl.pallas_call(
        body, name=name, grid=(n // TM,),
        in_specs=[rows, const((D_MODEL, D_MODEL)), const((1, D_MODEL)), rows, const((1, D_MODEL))],
        out_specs=[rows, rows], out_shape=[jax.ShapeDtypeStruct((n, D_MODEL), F32)] * 2,
        compiler_params=_params(("arbitrary",), 48),
    )(g, w, bias, xres, qw)


def _outproj_loss_bwd(g, w, xres, qw, tgt, name):
    n = g.shape[0]
    steps = n // TM

    def body(g_ref, w_ref, x_ref, qw_ref, t_ref, dx_ref, dg_ref, dy_ref, dqw_ref, loss_ref, acc_ref):
        i = pl.program_id(0)

        @pl.when(i == 0)
        def _():
            acc_ref[...] = jnp.zeros_like(acc_ref)
            dqw_ref[...] = jnp.zeros_like(dqw_ref)

        y = jnp.dot(g_ref[...], w_ref[...], preferred_element_type=F32)
        r = lax.rsqrt(jnp.mean(y * y, axis=-1, keepdims=True) + NORM_EPS)
        u = y * r
        e = (x_ref[...] + u * qw_ref[...]) - t_ref[...]
        dxn = e * (1.0 / D_MODEL)
        dx_ref[...] = dxn
        acc_ref[...] += jnp.sum(e * e, axis=0, keepdims=True)
        du = dxn * qw_ref[...]
        dy = (r * (du - u * jnp.mean(du * u, axis=-1, keepdims=True))).astype(BF16)
        dqw_ref[...] += jnp.sum(dxn * u, axis=0, keepdims=True)
        dy_ref[...] = dy
        dg_ref[...] = lax.dot_general(dy, w_ref[...], NT_DIMS, preferred_element_type=F32)

        @pl.when(i == steps - 1)
        def _():
            loss_ref[...] = jnp.full(loss_ref.shape, jnp.sum(acc_ref[...]) * (0.5 / D_MODEL), F32)

    rows = pl.BlockSpec((TM, D_MODEL), lambda i: (i, 0))
    const = lambda shape: pl.BlockSpec(shape, lambda i: (0, 0))
    return pl.pallas_call(
        body, name=name, grid=(steps,),
        in_specs=[rows, const((D_MODEL, D_MODEL)), rows, const((1, D_MODEL)), rows],
        out_specs=[rows, rows, rows, const((1, D_MODEL)), const((SUBLANES, LANES))],
        out_shape=[jax.ShapeDtypeStruct((n, D_MODEL), F32), jax.ShapeDtypeStruct((n, D_MODEL), F32),
                   jax.ShapeDtypeStruct((n, D_MODEL), BF16), jax.ShapeDtypeStruct((1, D_MODEL), F32),
                   jax.ShapeDtypeStruct((SUBLANES, LANES), F32)],
        scratch_shapes=[pltpu.VMEM((1, D_MODEL), F32)], compiler_params=_params(("arbitrary",), 48),
    )(g, w, xres, qw, tgt)


_QCOL, _ZCOL, _KCOL, _VCOL = 0, 1024, 2048, 2176


def _head_stack(chunks, heads, lt64):
    return jnp.concatenate([jnp.where(lt64 if n % 2 == 0 else ~lt64, chunks[n // 2], 0.0) for n in heads], axis=0)


def _dup_half(x, h, lt64):
    r = pltpu.roll(x, HEAD_DIM, 1)
    return jnp.where(lt64, x, r) if h == 0 else jnp.where(lt64, r, x)


def _pair_chunk(xt, c2):
    a, b = 2 * c2, 2 * c2 + 1
    return jnp.concatenate([xt[:HEAD_DIM, a * BLK:(a + 1) * BLK], xt[HEAD_DIM:, b * BLK:(b + 1) * BLK]], axis=0).T


def _attn_mask_t(i):
    key = lax.broadcasted_iota(jnp.int32, (2 * BLK, BLK), 0)
    qry = lax.broadcasted_iota(jnp.int32, (2 * BLK, BLK), 1)
    valid = (key > qry) & (key <= qry + BLK) & ((key >= BLK) | (i > 0))
    return jnp.tile(jnp.where(valid, 0.0, -1e30), (1, GROUP))


def _attn_probs_t(s, heads, sink_ref, mask):
    s = s + mask
    head = lax.broadcasted_iota(jnp.int32, (1, len(heads) * BLK), 1) >> 7
    sk = jnp.zeros((1, len(heads) * BLK), F32)
    for j, n in enumerate(heads):
        sk = jnp.where(head == j, sink_ref[0, n], sk)
    m = jnp.maximum(jnp.max(s, axis=0, keepdims=True), sk)
    p = jnp.exp(s - m)
    esk = jnp.exp(sk - m)
    inv = 1.0 / (jnp.sum(p, axis=0, keepdims=True) + esk)
    return p * inv, esk * inv


def _attn_fwd(p, ct, st, sinks, b_loc, nb, shards):
    n = p.shape[0]
    nsh = len(shards)

    def body(sink_ref, q_ref, z_ref, kc_ref, kp_ref, vc_ref, vp_ref, cc_ref, sc_ref, cp_ref, sp_ref, *rest):
        sh_in, (o_ref, g_ref), sh_out, sems = rest[:nsh], rest[nsh:nsh + 2], rest[nsh + 2:2 * nsh + 2], rest[2 * nsh + 2:]
        b, i = pl.program_id(0), pl.program_id(1)

        @pl.when((b == 0) & (i == 0))
        def _():
            _gather_start(sh_in, sh_out, sems)

        lane = lax.broadcasted_iota(jnp.int32, (BLK, LANES), 1)
        lm = lane & (HEAD_DIM - 1)
        cc, sc = cc_ref[...], sc_ref[...]
        kcat = jnp.concatenate([_rope_apply(kp_ref[...], cp_ref[...], sp_ref[...], lm),
                                _rope_apply(kc_ref[...], cc, sc, lm)], axis=0)
        vcat = jnp.concatenate([vp_ref[...], vc_ref[...]], axis=0)
        qr = [_rope_apply(q_ref[:, c * LANES:(c + 1) * LANES], cc, sc, lm) * (HEAD_DIM ** -0.5) for c in range(8)]
        valid = _attn_mask_t(i)
        lt64, lt64k = lane < HEAD_DIM, lax.broadcasted_iota(jnp.int32, (2 * BLK, LANES), 1) < HEAD_DIM
        def kv_head(h):
            heads = list(range(h * GROUP, (h + 1) * GROUP))
            kext, vext = _dup_half(kcat, h, lt64k).astype(BF16), _dup_half(vcat, h, lt64k).astype(BF16)
            qst = _head_stack(qr, heads, lt64).astype(BF16)
            s = lax.dot_general(kext, qst, NT_DIMS, preferred_element_type=F32)
            yield
            pn, _ = _attn_probs_t(s, heads, sink_ref, valid)
            ot = lax.dot_general(vext, pn.astype(BF16), TN_DIMS, preferred_element_type=F32)
            yield
            for c2 in range(GROUP // 2):
                oc = _pair_chunk(ot, c2)
                cols = slice((4 * h + c2) * LANES, (4 * h + c2 + 1) * LANES)
                zc = z_ref[:, cols]
                o_ref[:, cols] = oc
                g_ref[:, cols] = (oc * (zc * _sigmoid(zc))).astype(BF16)

        _in_stages([kv_head(h) for h in range(N_KV)])

        @pl.when((b == b_loc - 1) & (i == nb - 1))
        def _():
            _gather_wait(sh_in, sh_out, sems)

    cur = lambda b, i: b * nb + i
    prev = lambda b, i: b * nb + jnp.maximum(i - 1, 0)
    wide = lambda cb: pl.BlockSpec((BLK, ATTN_WIDTH), lambda b, i: (cur(b, i), cb))
    kv = lambda rowf, cb: pl.BlockSpec((BLK, LANES), lambda b, i: (rowf(b, i), cb))
    hbm = pl.BlockSpec(memory_space=pl.ANY)
    in_specs = [pl.BlockSpec(memory_space=pltpu.SMEM), wide(0), wide(1),
                kv(cur, _KCOL // LANES), kv(prev, _KCOL // LANES), kv(cur, _VCOL // LANES), kv(prev, _VCOL // LANES),
                kv(cur, 0), kv(cur, 0), kv(prev, 0), kv(prev, 0)] + [hbm] * nsh
    return pl.pallas_call(
        body, name="attn_fwd", grid=(b_loc, nb), in_specs=in_specs, out_specs=[wide(0), wide(0)] + [hbm] * nsh,
        out_shape=[jax.ShapeDtypeStruct((n, ATTN_WIDTH), F32), jax.ShapeDtypeStruct((n, ATTN_WIDTH), BF16)] + _gather_shapes(shards),
        scratch_shapes=_gather_sems(nsh), compiler_params=_params(("arbitrary", "arbitrary"), 48),
    )(sinks, p, p, p, p, p, p, ct, st, ct, st, *shards)


def _attn_bwd(p, ct, st, sinks, o, dg, b_loc, nb, parts):
    n = p.shape[0]
    nparts = len(parts)

    def body(sink_ref, q_ref, z_ref, kc_ref, kp_ref, vc_ref, vp_ref, cc_ref, sc_ref, cp_ref, sp_ref, o_ref, dg_ref, *rest):
        part_refs, (dp_ref, ds_ref), land_refs = rest[:nparts], rest[nparts:nparts + 2], rest[nparts + 2:2 * nparts + 2]
        dq_s, dz_s, dk_s, dv_s = rest[2 * nparts + 2:2 * nparts + 6]
        sems = rest[2 * nparts + 6:]
        b, i = pl.program_id(0), pl.program_id(1)

        @pl.when((b == 0) & (i == 0))
        def _():
            _scatter_start(part_refs, land_refs, sems)

        @pl.when((b == b_loc - 1) & (i == nb))
        def _():
            _scatter_wait(part_refs, land_refs, sems)

        lane = lax.broadcasted_iota(jnp.int32, (BLK, LANES), 1)
        lm = lane & (HEAD_DIM - 1)

        @pl.when((b == 0) & (i == 0))
        def _():
            ds_ref[...] = jnp.zeros_like(ds_ref)

        @pl.when(i < nb)
        def _compute():
            cc, sc = cc_ref[...], sc_ref[...]
            kcat = jnp.concatenate([_rope_apply(kp_ref[...], cp_ref[...], sp_ref[...], lm),
                                    _rope_apply(kc_ref[...], cc, sc, lm)], axis=0)
            vcat = jnp.concatenate([vp_ref[...], vc_ref[...]], axis=0)
            qr = [_rope_apply(q_ref[:, c * LANES:(c + 1) * LANES], cc, sc, lm) * (HEAD_DIM ** -0.5) for c in range(8)]
            valid = _attn_mask_t(i)
            lt64, lt64k = lane < HEAD_DIM, lax.broadcasted_iota(jnp.int32, (2 * BLK, LANES), 1) < HEAD_DIM
            do_chunks, doo_chunks, dz_chunks = [], [], []
            for c in range(8):
                cols = slice(c * LANES, (c + 1) * LANES)
                zc, oc, dgc = z_ref[:, cols], o_ref[:, cols], dg_ref[:, cols]
                sg = _sigmoid(zc)
                do_chunks.append(dgc * (zc * sg))
                dz_chunks.append(dgc * oc * (sg * (1.0 + zc * (1.0 - sg))))
                doo_chunks.append(do_chunks[c] * oc)
            dq_chunks = [None] * 8
            dk_h, dv_h, ds_parts = [None] * N_KV, [None] * N_KV, [None] * N_KV
            tile_lane = lax.broadcasted_iota(jnp.int32, (SUBLANES, LANES), 1)
            tile_row = lax.broadcasted_iota(jnp.int32, (SUBLANES, LANES), 0)
            ones8 = jnp.ones((SUBLANES, LANES), BF16)

            def kv_head(h):
                heads = list(range(h * GROUP, (h + 1) * GROUP))
                kext = _dup_half(kcat, h, lt64k)
                kext_bf, kext_t = kext.astype(BF16), kext.T.astype(BF16)
                vext = _dup_half(vcat, h, lt64k).astype(BF16)
                qst = _head_stack(qr, heads, lt64).astype(BF16)
                pn, psink = _attn_probs_t(lax.dot_general(kext_bf, qst, NT_DIMS, preferred_element_type=F32), heads, sink_ref, valid)
                do_bf = _head_stack(do_chunks, heads, lt64).astype(BF16)
                delta = sum(lax.dot_general(ones8, part, NT_DIMS, preferred_element_type=F32)
                            for part in _split3(_head_stack(doo_chunks, heads, lt64)))[0:1, :]
                dpt = lax.dot_general(vext, do_bf, NT_DIMS, preferred_element_type=F32)
                dst = (pn * (dpt - delta)).astype(BF16)
                sink_term = psink * delta
                ds_acc = jnp.zeros((SUBLANES, LANES), F32)
                for j, n in enumerate(heads):
                    val = -jnp.sum(sink_term[:, j * BLK:(j + 1) * BLK])
                    ds_acc = ds_acc + jnp.where((tile_lane == n) & (tile_row == 0), val, 0.0)
                ds_parts[h] = ds_acc
                dqt = jnp.dot(kext_t, dst, preferred_element_type=F32) * (HEAD_DIM ** -0.5)
                dk_ext = jnp.dot(dst, qst, preferred_element_type=F32)
                dv_ext = jnp.dot(pn.astype(BF16), do_bf, preferred_element_type=F32)
                dk_h[h] = dk_ext + pltpu.roll(dk_ext, HEAD_DIM, 1)
                dv_h[h] = dv_ext + pltpu.roll(dv_ext, HEAD_DIM, 1)
                for c2 in range(GROUP // 2):
                    dq_chunks[4 * h + c2] = _rope_bwd(_pair_chunk(dqt, c2), cc, sc, lm)

            for h in range(N_KV):
                kv_head(h)
            ds_ref[...] += ds_parts[0] + ds_parts[1]
            dk_full = jnp.where(lt64k, dk_h[0], dk_h[1])
            dv_full = jnp.where(lt64k, dv_h[0], dv_h[1])

            @pl.when(i >= 1)
            def _emit():
                dp_ref[:, _QCOL:_QCOL + ATTN_WIDTH] = dq_s[...]
                dp_ref[:, _ZCOL:_ZCOL + ATTN_WIDTH] = dz_s[...]
                dp_ref[:, _KCOL:_KCOL + KV_WIDTH] = _rope_bwd(dk_s[...] + dk_full[:BLK], cp_ref[...], sp_ref[...], lm)
                dp_ref[:, _VCOL:_VCOL + KV_WIDTH] = dv_s[...] + dv_full[:BLK]

            for c in range(8):
                dq_s[:, c * LANES:(c + 1) * LANES] = dq_chunks[c]
                dz_s[:, c * LANES:(c + 1) * LANES] = dz_chunks[c]
            dk_s[...] = dk_full[BLK:]
            dv_s[...] = dv_full[BLK:]

        @pl.when(i == nb)
        def _final():
            dp_ref[:, _QCOL:_QCOL + ATTN_WIDTH] = dq_s[...]
            dp_ref[:, _ZCOL:_ZCOL + ATTN_WIDTH] = dz_s[...]
            dp_ref[:, _KCOL:_KCOL + KV_WIDTH] = _rope_bwd(dk_s[...], cc_ref[...], sc_ref[...], lm)
            dp_ref[:, _VCOL:_VCOL + KV_WIDTH] = dv_s[...]

    cur = lambda b, i: b * nb + jnp.minimum(i, nb - 1)
    prev = lambda b, i: b * nb + jnp.maximum(jnp.minimum(i, nb - 1) - 1, 0)
    emit = lambda b, i: b * nb + jnp.maximum(i - 1, 0)
    hbm = pl.BlockSpec(memory_space=pl.ANY)
    wide = lambda cb: pl.BlockSpec((BLK, ATTN_WIDTH), lambda b, i: (cur(b, i), cb))
    kv = lambda rowf, cb: pl.BlockSpec((BLK, LANES), lambda b, i: (rowf(b, i), cb))
    in_specs = [pl.BlockSpec(memory_space=pltpu.SMEM), wide(0), wide(1),
                kv(cur, _KCOL // LANES), kv(prev, _KCOL // LANES), kv(cur, _VCOL // LANES), kv(prev, _VCOL // LANES),
                kv(cur, 0), kv(cur, 0), kv(prev, 0), kv(prev, 0), wide(0), wide(0)] + [hbm] * nparts
    out_specs = [pl.BlockSpec((BLK, ATTN_IN), lambda b, i: (emit(b, i), 0)),
                 pl.BlockSpec((SUBLANES, LANES), lambda b, i: (0, 0))] + [hbm] * nparts
    return pl.pallas_call(
        body, name="attn_bwd", grid=(b_loc, nb + 1), in_specs=in_specs, out_specs=out_specs,
        out_shape=[jax.ShapeDtypeStruct((n, ATTN_IN), F32), jax.ShapeDtypeStruct((SUBLANES, LANES), F32)] + _scatter_lands(parts),
        scratch_shapes=[pltpu.VMEM((BLK, ATTN_WIDTH), F32), pltpu.VMEM((BLK, ATTN_WIDTH), F32),
                        pltpu.VMEM((BLK, KV_WIDTH), F32), pltpu.VMEM((BLK, KV_WIDTH), F32)] + _scatter_sems(nparts),
        compiler_params=_params(("arbitrary", "arbitrary"), 48),
    )(sinks, p, p, p, p, p, p, ct, st, ct, st, o, dg, *parts)


_CUM_ROWS = 256
HALF = CH // 2
_ROW0 = [SUBLANES * (s // SUBLANES) for s in range(CH)]
_ROW1 = [HALF * (s // HALF + 1) for s in range(CH)]
_ROWS_OF = [_ROW1[s] - _ROW0[s] for s in range(CH)]
_OFF_OF = [sum(_ROWS_OF[:s]) for s in range(CH)]


def _tri(lower):
    r = lax.broadcasted_iota(jnp.int32, (_CUM_ROWS, _CUM_ROWS), 0)
    c = lax.broadcasted_iota(jnp.int32, (_CUM_ROWS, _CUM_ROWS), 1)
    same = (r ^ c) < CH
    return (same & ((c <= r) if lower else (c >= r))).astype(BF16)


def _gates(qp, fp, lb):
    e = jnp.exp(-jnp.abs(fp))
    r = 1.0 / (1.0 + e)
    sig_neg = jnp.where(fp >= 0, e, 1.0) * r
    sig = jnp.where(fp >= 0, 1.0, e) * r
    g = jnp.log(jnp.maximum(lb + (1.0 - lb) * sig, F32_TINY))
    return qp * _sigmoid(qp), g, (1.0 - lb) * sig_neg, sig_neg


def _pair_rows(bc, s):
    diff = bc[_ROW0[s]:_ROW1[s], :] - bc[s:s + 1, :]
    head = jnp.minimum(diff[:SUBLANES, :], 0.0)
    return jnp.exp(head if diff.shape[0] == SUBLANES else jnp.concatenate([head, diff[SUBLANES:, :]], axis=0))


def _cross_half(q, k, bc):
    r = bc[HALF - 1:HALF, :]
    e1, e0 = jnp.exp(bc[HALF:, :] - r), jnp.exp(r - bc[:HALF, :])
    return q[HALF:, :] * e1, e1, k[:HALF, :] * e0, e0


HP = 8
REC_TB = 256
_HW = HP * REC_K


def _hgrn_specs(tb, nt, reverse):
    tmap = (lambda t: nt - 1 - t) if reverse else (lambda t: t)
    groups = REC_HEADS // HP
    blk = lambda cb: pl.BlockSpec((tb, _HW), lambda h, b, t: (b * nt + tmap(t), cb * groups + h))
    head = pl.BlockSpec((tb, _HW), lambda h, b, t: (b * nt + tmap(t), h))
    lbs = pl.BlockSpec((1, _HW), lambda h, b, t: (0, h))
    gws = pl.BlockSpec((1, REC_K), lambda h, b, t: (0, 0))
    hist = pl.BlockSpec((HP, 1, tb // CH, REC_K, REC_K), lambda h, b, t: (h, b, tmap(t), 0, 0))
    return blk, head, lbs, gws, hist


def _chunk_rows(c, first=0, size=CH):
    start = c * CH + first
    return pl.ds(start if isinstance(start, int) else pl.multiple_of(start, CH if first % CH == 0 else SUBLANES), size)


def _in_stages(heads):
    live = list(heads)
    while live:
        live = [g for g in live if next(g, live) is not live]


def _cumsum_chunks(tri, x, out_ref, tb):
    for r in range(0, tb, _CUM_ROWS):
        out_ref[r:r + _CUM_ROWS, :] = _dot3(tri, x[r:r + _CUM_ROWS, :])


def _hgrn_fwd(p, lb, gw, b_loc, t_len):
    n = p.shape[0]
    tb = min(REC_TB, t_len)
    nt, nck = t_len // tb, tb // CH

    def body(qp_ref, fp_ref, i_ref, z_ref, lb_ref, gw_ref, oraw_ref, g_ref, sh_ref, q_s, k_s, b_s, o_s, st_ref,
             car_o, car_a, car_s, car_st):
        @pl.when(pl.program_id(2) == 0)
        def _():
            st_ref[...] = jnp.zeros_like(st_ref)

        qv, g, kk, _ = _gates(qp_ref[...], fp_ref[...], lb_ref[...])
        q_s[...] = qv
        k_s[...] = kk
        _cumsum_chunks(_tri(True), g, b_s, tb)
        ones = jnp.ones((REC_K, REC_K), BF16)
        sub = lax.broadcasted_iota(jnp.int32, (SUBLANES, REC_K), 0)

        rows_of = _chunk_rows

        def issue(c, hp):
            rs, cs = rows_of(c), slice(hp * REC_K, (hp + 1) * REC_K)
            q, k, bc, v = q_s[rs, cs], k_s[rs, cs], b_s[rs, cs], i_ref[rs, cs]
            st = st_ref[hp]
            sh_ref[hp, 0, c] = st
            o = lax.dot_general((q * jnp.exp(bc)).astype(BF16), st.astype(BF16), NT_DIMS, preferred_element_type=F32)
            w = jnp.concatenate([q[_ROW0[s]:_ROW1[s], :] * _pair_rows(bc, s) * k[s:s + 1, :] for s in range(CH)], axis=0)
            a = jnp.dot(w.astype(BF16), ones, preferred_element_type=F32)
            qe1, _, ke0, _ = _cross_half(q, k, bc)
            s10 = lax.dot_general(qe1.astype(BF16), ke0.astype(BF16), NT_DIMS, preferred_element_type=F32)
            kd = k * jnp.exp(bc[CH - 1:CH, :] - bc)
            st_new = lax.dot_general(v.astype(BF16), kd.astype(BF16), TN_DIMS, preferred_element_type=F32)
            return o, a, s10, st_new

        def advance_state(c, hp, st_new):
            bl = b_s[_chunk_rows(c, CH - SUBLANES, SUBLANES), hp * REC_K:(hp + 1) * REC_K][SUBLANES - 1:, :]
            st_ref[hp] = st_ref[hp] * jnp.exp(bl) + st_new

        def cross(c, hp, s10):
            v0 = i_ref[_chunk_rows(c, 0, HALF), hp * REC_K:(hp + 1) * REC_K]
            return jnp.dot(s10.astype(BF16), v0.astype(BF16), preferred_element_type=F32)

        def finish(c, hp, o, a, o_cross):
            rs, cs = rows_of(c), slice(hp * REC_K, (hp + 1) * REC_K)
            v = i_ref[rs, cs]
            acc = [jnp.zeros((SUBLANES, REC_K), F32) for _ in range(CH // SUBLANES)]
            for s in range(CH):
                j = s // SUBLANES
                vs = v[s:s + 1, :]
                for jj in range(j, _ROW1[s] // SUBLANES):
                    blk = a[_OFF_OF[s] + (jj - j) * SUBLANES:_OFF_OF[s] + (jj - j + 1) * SUBLANES, :]
                    if jj == j:
                        blk = jnp.where(sub >= s - j * SUBLANES, blk, 0.0)
                    acc[jj] = acc[jj] + blk * vs
            o_s[rs, cs] = o + jnp.concatenate(acc, axis=0) + jnp.concatenate([jnp.zeros((HALF, REC_K), F32), o_cross], axis=0)

        def park(slot, results):
            for hp, (o, a, s10, st_new) in enumerate(results):
                car_o[slot, hp], car_a[slot, hp], car_s[slot, hp], car_st[slot, hp] = o, a, s10, st_new

        def retire(c, slot):
            for hp in range(HP):
                advance_state(c, hp, car_st[slot, hp])
            yield
            crosses = [cross(c, hp, car_s[slot, hp]) for hp in range(HP)]
            for hp in range(HP):
                finish(c, hp, car_o[slot, hp], car_a[slot, hp], crosses[hp])

        def step(c, slot):
            closing = retire(c - 1, slot)
            next(closing)
            park(1 - slot, [issue(c, hp) for hp in range(HP)])
            next(closing, None)

        def trip(j, carry):
            step(2 * j + 1, 0)
            step(2 * j + 2, 1)
            return carry

        assert nck % 2 == 0
        park(0, [issue(0, hp) for hp in range(HP)])
        lax.fori_loop(0, nck // 2 - 1, trip, 0)
        step(nck - 1, 0)
        for _ in retire(nck - 1, 1):
            pass
        oraw_ref[...] = o_s[...]
        for hp in range(HP):
            cs = slice(hp * REC_K, (hp + 1) * REC_K)
            o, zc = o_s[:, cs], z_ref[:, cs]
            on = (o * lax.rsqrt(jnp.mean(o * o, axis=-1, keepdims=True) + NORM_EPS)) * gw_ref[...]
            g_ref[:, cs] = (on * (zc * _sigmoid(zc))).astype(BF16)

    blk, head, lbs, gws, hist = _hgrn_specs(tb, nt, False)
    return pl.pallas_call(
        body, name="hgrn_fwd", grid=(REC_HEADS // HP, b_loc, nt),
        in_specs=[blk(0), blk(1), blk(2), blk(3), lbs, gws], out_specs=[head, head, hist],
        out_shape=[jax.ShapeDtypeStruct((n, REC_WIDTH), F32), jax.ShapeDtypeStruct((n, REC_WIDTH), BF16),
                   jax.ShapeDtypeStruct((REC_HEADS, b_loc, t_len // CH, REC_K, REC_K), F32)],
        scratch_shapes=[pltpu.VMEM((tb, _HW), F32)] * 4 + [pltpu.VMEM((HP, REC_K, REC_K), F32)] + [
            pltpu.VMEM((2, HP, CH, REC_K), F32), pltpu.VMEM((2, HP, sum(_ROWS_OF), REC_K), F32),
            pltpu.VMEM((2, HP, HALF, HALF), F32), pltpu.VMEM((2, HP, REC_K, REC_K), F32)],
        compiler_params=_params(("arbitrary", "arbitrary", "arbitrary"), 48),
    )(p, p, p, p, lb, gw)


def _hgrn_bwd(p, lb, gw, oraw, sh, dg, b_loc, t_len):
    n = p.shape[0]
    tb = min(REC_TB, t_len)
    nt, nck = t_len // tb, tb // CH
    assert HP == REC_HEADS

    def body(qp_ref, fp_ref, i_ref, z_ref, lb_ref, gw_ref, oraw_ref, dg_ref, sh_ref,
             dp_ref, dlb_ref, dgw_ref,
             q_s, k_s, b_s, do_s, dqv_s, dk_s, db_s, dst_ref, car_r, car_dst, car_dec, car_a, car_da, car_x):
        dq_ref, df_ref, di_ref, dz_ref = (dp_ref.at[:, part * REC_WIDTH:(part + 1) * REC_WIDTH] for part in range(4))
        b, t = pl.program_id(1), pl.program_id(2)

        @pl.when(t == 0)
        def _():
            dst_ref[...] = jnp.zeros_like(dst_ref)

        @pl.when((b == 0) & (t == 0))
        def _():
            dlb_ref[...] = jnp.zeros_like(dlb_ref)
            dgw_ref[...] = jnp.zeros_like(dgw_ref)

        lbv, qp, fp = lb_ref[...], qp_ref[...], fp_ref[...]
        qv, g, kk, sig_neg = _gates(qp, fp, lbv)
        q_s[...] = qv
        k_s[...] = kk
        _cumsum_chunks(_tri(True), g, b_s, tb)
        gwv = gw_ref[...]
        for hp in range(HP):
            cs = slice(hp * REC_K, (hp + 1) * REC_K)
            o, zc, dgv = oraw_ref[:, cs], z_ref[:, cs], dg_ref[:, cs]
            rn = lax.rsqrt(jnp.mean(o * o, axis=-1, keepdims=True) + NORM_EPS)
            on = o * rn
            sgz = _sigmoid(zc)
            dz_ref[:, cs] = (dgv * (on * gwv) * (sgz * (1.0 + zc * (1.0 - sgz)))).astype(BF16)
            dpre = dgv * (zc * sgz)
            dgw_ref[hp] += jnp.sum(dpre * on, axis=0, keepdims=True)
            don = dpre * gwv
            do_s[:, cs] = rn * (don - on * jnp.mean(don * on, axis=-1, keepdims=True))

        ones = jnp.ones((REC_K, REC_K), BF16)
        sub = lax.broadcasted_iota(jnp.int32, (SUBLANES, REC_K), 0)
        rowid = lax.broadcasted_iota(jnp.int32, (CH, REC_K), 0)
        ngrp = CH // SUBLANES
        piece_row = lax.broadcasted_iota(jnp.int32, (1, sum(_ROWS_OF)), 1)
        key_of = jnp.zeros((1, sum(_ROWS_OF)), jnp.int32)
        for s in range(1, CH):
            key_of = jnp.where(piece_row >= _OFF_OF[s], s, key_of)
        pick = (key_of == lax.broadcasted_iota(jnp.int32, (CH, sum(_ROWS_OF)), 0)).astype(BF16)

        def operands(c, hp):
            rs, cs = _chunk_rows(c), slice(hp * REC_K, (hp + 1) * REC_K)
            return rs, cs, q_s[rs, cs], k_s[rs, cs], b_s[rs, cs], i_ref[rs, cs], do_s[rs, cs]

        def issue(c, hp, slot):
            _, _, q, k, bc, v, do = operands(c, hp)
            st, dst = sh_ref[hp, 0, c], dst_ref[hp]
            qe, kd = q * jnp.exp(bc), k * jnp.exp(bc[CH - 1:CH, :] - bc)
            do_bf, dst_bf = do.astype(BF16), dst.astype(BF16)
            car_r[slot, hp, 0:CH] = jnp.dot(do_bf, st.astype(BF16), preferred_element_type=F32)
            car_r[slot, hp, CH:2 * CH] = jnp.dot(v.astype(BF16), dst_bf, preferred_element_type=F32)
            car_r[slot, hp, 2 * CH:3 * CH] = lax.dot_general(kd.astype(BF16), dst_bf, NT_DIMS, preferred_element_type=F32)
            car_dst[slot, hp] = lax.dot_general(do_bf, qe.astype(BF16), TN_DIMS, preferred_element_type=F32)
            dec = jnp.concatenate([_pair_rows(bc, s) for s in range(CH)], axis=0)
            qk = jnp.concatenate([q[_ROW0[s]:_ROW1[s], :] * k[s:s + 1, :] for s in range(CH)], axis=0)
            x = jnp.concatenate([do[_ROW0[s]:_ROW1[s], :] * v[s:s + 1, :] for s in range(CH)], axis=0)
            car_dec[slot, hp] = dec
            car_a[slot, hp] = jnp.dot((qk * dec).astype(BF16), ones, preferred_element_type=F32)
            car_da[slot, hp] = jnp.dot(x.astype(BF16), ones, preferred_element_type=F32)
            qe1, _, ke0, _ = _cross_half(q, k, bc)
            qe1_bf, ke0_bf = qe1.astype(BF16), ke0.astype(BF16)
            do1_bf, v0_bf = do[HALF:, :].astype(BF16), v[:HALF, :].astype(BF16)
            car_x[slot, hp, 0:HALF] = lax.dot_general(ke0_bf, qe1_bf, NT_DIMS, preferred_element_type=F32)
            car_x[slot, hp, HALF:2 * HALF] = lax.dot_general(do1_bf, v0_bf, NT_DIMS, preferred_element_type=F32)
            car_x[slot, hp, 2 * HALF:3 * HALF] = lax.dot_general(v0_bf, do1_bf, NT_DIMS, preferred_element_type=F32)

        def advance_state(c, hp, slot):
            ebl = jnp.exp(b_s[_chunk_rows(c, CH - SUBLANES, SUBLANES), hp * REC_K:(hp + 1) * REC_K][SUBLANES - 1:, :])
            st, dst = sh_ref[hp, 0, c], dst_ref[hp]
            dst_ref[hp] = dst * ebl + car_dst[slot, hp]
            return ebl * jnp.sum(st * dst, axis=0, keepdims=True)

        def cross(c, hp, slot):
            _, _, q, k, bc, v, do = operands(c, hp)
            qe1, _, ke0, _ = _cross_half(q, k, bc)
            xs = car_x[slot, hp]
            dqe1 = jnp.dot(xs[HALF:2 * HALF].astype(BF16), ke0.astype(BF16), preferred_element_type=F32)
            dke0 = jnp.dot(xs[2 * HALF:].astype(BF16), qe1.astype(BF16), preferred_element_type=F32)
            dv1 = jnp.dot(xs[:HALF].astype(BF16), do[HALF:, :].astype(BF16), preferred_element_type=F32)
            return dqe1, dke0, dv1

        def retire(c, slot):
            dbl_state = [advance_state(c, hp, slot) for hp in range(HP)]
            yield
            crossed = [cross(c, hp, slot) for hp in range(HP)]
            for hp in range(HP):
                finish(c, hp, slot, dbl_state[hp], *crossed[hp])

        def step(c, slot):
            closing = retire(c + 1, slot)
            next(closing)
            for hp in range(HP):
                issue(c, hp, 1 - slot)
            next(closing, None)

        def trip(j, carry):
            step(nck - 2 - 2 * j, 0)
            step(nck - 3 - 2 * j, 1)
            return carry

        def finish(c, hp, slot, dbl_state, dqe1, dke0, dv1):
            rs, cs, q, k, bc, v, do = operands(c, hp)
            eb, ekd = jnp.exp(bc), jnp.exp(bc[CH - 1:CH, :] - bc)
            qe, kd = q * eb, k * ekd
            qe1, e1, ke0, e0 = _cross_half(q, k, bc)
            dqe, dkd, dv = car_r[slot, hp, 0:CH], car_r[slot, hp, CH:2 * CH], car_r[slot, hp, 2 * CH:3 * CH]
            a, da, decs = car_a[slot, hp], car_da[slot, hp], car_dec[slot, hp]
            dec = [decs[_OFF_OF[s]:_OFF_OF[s] + _ROWS_OF[s], :] for s in range(CH)]
            dbl = jnp.sum(dkd * kd, axis=0, keepdims=True) + dbl_state
            dq_acc = [jnp.zeros((SUBLANES, REC_K), F32) for _ in range(ngrp)]
            uk, uv = [], []
            for s in range(CH):
                j = s // SUBLANES
                r0 = j * SUBLANES
                ks = k[s:s + 1, :]
                for jj in range(j, _ROW1[s] // SUBLANES):
                    lo, hi = _OFF_OF[s] + (jj - j) * SUBLANES, _OFF_OF[s] + (jj - j + 1) * SUBLANES
                    a_blk, da_blk = a[lo:hi, :], da[lo:hi, :]
                    if jj == j:
                        keep = sub >= s - r0
                        a_blk, da_blk = jnp.where(keep, a_blk, 0.0), jnp.where(keep, da_blk, 0.0)
                    rows = slice(jj * SUBLANES, (jj + 1) * SUBLANES)
                    tt = da_blk * dec[s][(jj - j) * SUBLANES:(jj - j + 1) * SUBLANES, :]
                    dq_acc[jj] = dq_acc[jj] + tt * ks
                    uk.append(tt * q[rows, :])
                    uv.append(a_blk * do[rows, :])
            dk_in = jnp.dot(pick, jnp.concatenate(uk, axis=0).astype(BF16), preferred_element_type=F32)
            dv_in = jnp.dot(pick, jnp.concatenate(uv, axis=0).astype(BF16), preferred_element_type=F32)
            zero_half = jnp.zeros((HALF, REC_K), F32)
            dq_x = jnp.concatenate([zero_half, dqe1 * e1], axis=0)
            dk_x = jnp.concatenate([dke0 * e0, zero_half], axis=0)
            dv_x = jnp.concatenate([dv1, zero_half], axis=0)
            db_x = jnp.concatenate([-(dke0 * ke0), dqe1 * qe1], axis=0)
            dq_in = jnp.concatenate(dq_acc, axis=0)
            dqv_s[rs, cs] = dqe * eb + dq_in + dq_x
            dk_s[rs, cs] = dkd * ekd + dk_in + dk_x
            di_ref[rs, cs] = (dv + dv_in + dv_x).astype(BF16)
            db = dqe * qe - dkd * kd + q * dq_in - k * dk_in + db_x
            db_s[rs, cs] = db + jnp.where(rowid == CH - 1, dbl, 0.0)

        assert nck % 2 == 0
        for hp in range(HP):
            issue(nck - 1, hp, 0)
        lax.fori_loop(0, nck // 2 - 1, trip, 0)
        step(0, 0)
        for _ in retire(0, 1):
            pass
        up = _tri(False)
        sgq = _sigmoid(qp)
        dq_ref[...] = (dqv_s[...] * (sgq * (1.0 + qp * (1.0 - sgq)))).astype(BF16)
        dlb_acc = jnp.zeros((1, _HW), F32)
        for r in range(0, tb, _CUM_ROWS):
            rows = slice(r, r + _CUM_ROWS)
            dgl = _dot3(up, db_s[rows, :])
            dfg = dgl * jnp.exp(-g[rows, :]) - dk_s[rows, :]
            sn = sig_neg[rows, :]
            df_ref[rows, :] = (dfg * (1.0 - lbv) * (1.0 - sn) * sn).astype(BF16)
            dlb_acc = dlb_acc + jnp.sum(dfg * sn, axis=0, keepdims=True)
        dlb_ref[...] += dlb_acc

    blk, head, lbs, gws, hist = _hgrn_specs(tb, nt, True)
    out_specs = [pl.BlockSpec((tb, REC_IN), lambda h, b, t: (b * nt + nt - 1 - t, 0)), lbs,
                 pl.BlockSpec((HP, 1, REC_K), lambda h, b, t: (h, 0, 0))]
    out_shape = [jax.ShapeDtypeStruct((n, REC_IN), BF16),
                 jax.ShapeDtypeStruct((1, REC_WIDTH), F32), jax.ShapeDtypeStruct((REC_HEADS, 1, REC_K), F32)]
    return pl.pallas_call(
        body, name="hgrn_bwd", grid=(REC_HEADS // HP, b_loc, nt),
        in_specs=[blk(0), blk(1), blk(2), blk(3), lbs, gws, head, head, hist],
        out_specs=out_specs, out_shape=out_shape,
        scratch_shapes=[pltpu.VMEM((tb, _HW), F32)] * 7 + [pltpu.VMEM((HP, REC_K, REC_K), F32)] + [
            pltpu.VMEM((2, HP, 3 * CH, REC_K), F32), pltpu.VMEM((2, HP, REC_K, REC_K), F32)] + [
            pltpu.VMEM((2, HP, sum(_ROWS_OF), REC_K), F32)] * 3 + [pltpu.VMEM((2, HP, 3 * HALF, HALF), F32)],
        compiler_params=_params(("arbitrary", "arbitrary", "arbitrary"), 56),
    )(p, p, p, p, lb, gw, oraw, dg, sh)


def _postnorm_bwd_nt(dxo, y, qw, w, name):
    n = dxo.shape[0]

    def body(dx_ref, y_ref, qw_ref, w_ref, dg_ref, dy_ref, dqw_ref, db_ref):
        @pl.when(pl.program_id(0) == 0)
        def _():
            dqw_ref[...] = jnp.zeros_like(dqw_ref)
            db_ref[...] = jnp.zeros_like(db_ref)

        yv, dxv = y_ref[...], dx_ref[...]
        r = lax.rsqrt(jnp.mean(yv * yv, axis=-1, keepdims=True) + NORM_EPS)
        u = yv * r
        du = dxv * qw_ref[...]
        dy = r * (du - u * jnp.mean(du * u, axis=-1, keepdims=True))
        dqw_ref[...] += jnp.sum(dxv * u, axis=0, keepdims=True)
        db_ref[...] += jnp.sum(dy, axis=0, keepdims=True)
        dyb = dy.astype(BF16)
        dy_ref[...] = dyb
        dg_ref[...] = lax.dot_general(dyb, w_ref[...], NT_DIMS, preferred_element_type=F32)

    rows = pl.BlockSpec((TM, D_MODEL), lambda i: (i, 0))
    const = lambda shape: pl.BlockSpec(shape, lambda i: (0, 0))
    return pl.pallas_call(
        body, name=name, grid=(n // TM,), in_specs=[rows, rows, const((1, D_MODEL)), const((D_MODEL, D_MODEL))],
        out_specs=[rows, rows, const((1, D_MODEL)), const((1, D_MODEL))],
        out_shape=[jax.ShapeDtypeStruct((n, D_MODEL), F32), jax.ShapeDtypeStruct((n, D_MODEL), BF16),
                   jax.ShapeDtypeStruct((1, D_MODEL), F32), jax.ShapeDtypeStruct((1, D_MODEL), F32)],
        compiler_params=_params(("arbitrary",), 48),
    )(dxo, y, qw, w)


def _nt_prenorm_bwd(dps, w, x, pw, dxo, has_bias, name, parts=()):
    n = x.shape[0]
    widths = [d.shape[1] for d in dps]
    m = sum(widths)
    npieces, nparts, steps = len(dps), len(parts), n // TM

    def body(*refs):
        dp_refs = refs[:npieces]
        w_ref, x_ref, pw_ref, dxo_ref = refs[npieces:npieces + 4]
        part_refs = refs[npieces + 4:npieces + 4 + nparts]
        dx_ref, dpw_ref, db_ref = refs[npieces + 4 + nparts:npieces + 7 + nparts]
        land_refs = refs[npieces + 7 + nparts:npieces + 7 + 2 * nparts]
        sems = refs[npieces + 7 + 2 * nparts:]

        @pl.when(pl.program_id(0) == 0)
        def _():
            dpw_ref[...] = jnp.zeros_like(dpw_ref)
            db_ref[...] = jnp.zeros_like(db_ref)
            if nparts:
                _scatter_start(part_refs, land_refs, sems)

        dh = jnp.zeros((TM, D_MODEL), F32)
        off = 0
        for dp_ref, wd in zip(dp_refs, widths):
            cn = _col_chunk(wd)
            for j in range(0, wd, cn):
                dpc = dp_ref[:, j:j + cn]
                if has_bias:
                    db_ref[:, off + j:off + j + cn] += jnp.sum(dpc, axis=0, keepdims=True)
                dh = dh + lax.dot_general(dpc.astype(BF16), w_ref[:, off + j:off + j + cn], NT_DIMS, preferred_element_type=F32)
            off += wd
        xv = x_ref[...]
        r = lax.rsqrt(jnp.mean(xv * xv, axis=-1, keepdims=True) + NORM_EPS)
        xn = xv * r
        dpw_ref[...] += jnp.sum(dh * xn, axis=0, keepdims=True)
        dxn = dh * pw_ref[...]
        dx_ref[...] = dxo_ref[...] + r * (dxn - xn * jnp.mean(dxn * xn, axis=-1, keepdims=True))

        if nparts:
            @pl.when(pl.program_id(0) == steps - 1)
            def _():
                _scatter_wait(part_refs, land_refs, sems)

    rows = pl.BlockSpec((TM, D_MODEL), lambda i: (i, 0))
    const = lambda shape: pl.BlockSpec(shape, lambda i: (0, 0))
    hbm = pl.BlockSpec(memory_space=pl.ANY)
    in_specs = ([pl.BlockSpec((TM, wd), lambda i: (i, 0)) for wd in widths] + [const((D_MODEL, m)), rows, const((1, D_MODEL)), rows]
                + [hbm] * nparts)
    return pl.pallas_call(
        body, name=name, grid=(steps,), in_specs=in_specs,
        out_specs=[rows, const((1, D_MODEL)), const((1, m))] + [hbm] * nparts,
        out_shape=[jax.ShapeDtypeStruct((n, D_MODEL), F32), jax.ShapeDtypeStruct((1, D_MODEL), F32),
                   jax.ShapeDtypeStruct((1, m), F32)] + _scatter_lands(parts),
        scratch_shapes=_scatter_sems(nparts) if nparts else [],
        compiler_params=_params(("arbitrary",), 56),
    )(*dps, w, x, pw, dxo, *parts)


def _matmul_tn(a, b, name):
    n, k = a.shape
    m = b.shape[1]
    tk, tm, tn = k, _col_chunk(m), TN_ROWS if n % TN_ROWS == 0 else n

    def body(a_ref, b_ref, o_ref):
        @pl.when(pl.program_id(2) == 0)
        def _():
            o_ref[...] = jnp.zeros_like(o_ref)

        o_ref[...] += lax.dot_general(a_ref[...], b_ref[...].astype(BF16), TN_DIMS, preferred_element_type=F32)

    return pl.pallas_call(
        body, name=name, grid=(k // tk, m // tm, n // tn),
        in_specs=[pl.BlockSpec((tn, tk), lambda i, j, l: (l, i)), pl.BlockSpec((tn, tm), lambda i, j, l: (l, j))],
        out_specs=pl.BlockSpec((tk, tm), lambda i, j, l: (i, j)),
        out_shape=jax.ShapeDtypeStruct((k, m), F32),
        compiler_params=_params(("arbitrary", "arbitrary", "arbitrary"), 48),
    )(a, b)


def _matmul_tn_by_owner(a, b, name):
    n, k = a.shape
    c = b.shape[1] // N_DEV
    tn = TN_ROWS if n % TN_ROWS == 0 else n
    steps = n // tn
    per = 2

    def body(a_ref, b_ref, o_ref, w_ref):
        @pl.when(pl.program_id(1) == 0)
        def _():
            o_ref[...] = jnp.zeros_like(o_ref)

        r = lax.dot_general(a_ref[...], b_ref[...], TN_DIMS, preferred_element_type=F32)
        for j in range(per):
            o_ref[j] += r[:, j * c:(j + 1) * c]

        @pl.when(pl.program_id(1) == steps - 1)
        def _():
            w_ref[...] = o_ref[...].astype(BF16)

    out = pl.BlockSpec((per, k, c), lambda j, l: (j, 0, 0))
    return pl.pallas_call(
        body, name=name, grid=(N_DEV // per, steps),
        in_specs=[pl.BlockSpec((tn, k), lambda j, l: (l, 0)), pl.BlockSpec((tn, per * c), lambda j, l: (l, j))],
        out_specs=[out, out],
        out_shape=[jax.ShapeDtypeStruct((N_DEV, k, c), F32), jax.ShapeDtypeStruct((N_DEV, k, c), BF16)],
        compiler_params=_params(("arbitrary", "arbitrary"), 48),
    )(a, b)


def _by_owner_cols(dw):
    k, m = dw.shape
    return dw.reshape(k, N_DEV, m // N_DEV).transpose(1, 0, 2)


def _own_and_bf16(part):
    return lax.dynamic_index_in_dim(part, _my_id(), 0, keepdims=False), part.astype(BF16)


def _step(x, ct, st, tgt, pre_w, post_w, wa_in, ba_in, sinks, wa_out_shard, ba_out, wr_in_shard, lb_logits, gnorm_w, wr_out_shard, b_loc, t_len):
    nb = t_len // BLK
    lb = _lower_bound(lb_logits)
    p0, h0, ga_out = _norm_matmul(x, pre_w[0:1], wa_in, ba_in, "attn_in_proj", [wa_out_shard])
    wa_out = ga_out.reshape(ATTN_WIDTH, D_MODEL)
    o0, g0, gr_in = _attn_fwd(p0, ct, st, sinks, b_loc, nb, [wr_in_shard])
    wr_in = gr_in.transpose(1, 0, 2).reshape(D_MODEL, REC_IN)
    y0, x1 = _outproj_postnorm(g0, wa_out, ba_out, x, post_w[0:1], "attn_out_proj")
    p1, h1, gr_out = _norm_matmul(x1, pre_w[1:2], wr_in, None, "rec_in_proj", [wr_out_shard])
    wr_out = gr_out.reshape(REC_WIDTH, D_MODEL)
    o1, g1, sh = _hgrn_fwd(p1, lb, gnorm_w, b_loc, t_len)
    dx2, dg1, dy1, dpost1, loss_tile = _outproj_loss_bwd(g1, wr_out, x1, post_w[1:2], tgt, "rec_out_proj_loss_bwd")
    d_wr_out = _matmul_tn(g1, dy1, "rec_w_out_grad")
    dp1, dlb, dgw = _hgrn_bwd(p1, lb, gnorm_w, o1, sh, dg1, b_loc, t_len)
    dx1, dpre1, _ = _nt_prenorm_bwd([dp1], wr_in, x1, pre_w[1:2], dx2, False, "rec_in_bwd")
    part_r_in, wire_r_in = _matmul_tn_by_owner(h1, dp1, "rec_w_in_grad")
    own_r_in = lax.dynamic_index_in_dim(part_r_in, _my_id(), 0, keepdims=False)
    dg0, dy0, dpost0, dba_out = _postnorm_bwd_nt(dx1, y0, post_w[0:1], wa_out, "attn_out_bwd")
    d_wa_out = _matmul_tn(g0, dy0, "attn_w_out_grad")
    owns, wires = zip(*[_own_and_bf16(part) for part in (
        d_wr_out.reshape(N_DEV, REC_WIDTH // N_DEV, D_MODEL), d_wa_out.reshape(N_DEV, ATTN_WIDTH // N_DEV, D_MODEL))])
    owns, wires = (own_r_in,) + owns, (wire_r_in,) + wires
    dp0, dsink_tile, *lands = _attn_bwd(p0, ct, st, sinks, o0, dg0, b_loc, nb, list(wires))
    d_wa_in = _matmul_tn(h0, dp0, "attn_w_in_grad")
    own_a_in, wire_a_in = _own_and_bf16(_by_owner_cols(_qkvz(d_wa_in)))
    dx0, dpre0, dba_in, land_a_in = _nt_prenorm_bwd([dp0], wa_in, x, pre_w[0:1], dx1, True, "attn_in_bwd", [wire_a_in])
    small = dict(pre=jnp.concatenate([dpre0, dpre1], axis=0), post=jnp.concatenate([dpost0, dpost1], axis=0),
                 ba_in=dba_in, sinks=dsink_tile[0:1, 0:N_HEADS], ba_out=dba_out, lb=dlb, gnorm=jnp.sum(dgw, axis=0))
    return loss_tile, dx0, list(zip(lands, owns)) + [(land_a_in, own_a_in)], small


def _my_id():
    return lax.axis_index("x") * 4 + lax.axis_index("y") * 2 + lax.axis_index("c")


def _peer(k):
    x, y, c = lax.axis_index("x"), lax.axis_index("y"), lax.axis_index("c")
    return (x ^ ((k >> 2) & 1), y ^ ((k >> 1) & 1), c ^ (k & 1))


def _peer_id(k):
    return _my_id() ^ k


def _all_gather_by_chip(shard, pos_col):
    n = pos_col.shape[0]

    def body(x_ref, p_ref, f_ref, s_ref, out_ref, ct_ref, st_ref, send_sems, recv_sems, local_sem):
        x, y, c = lax.axis_index("x"), lax.axis_index("y"), lax.axis_index("c")
        me, sibling = (x, y, c), (x, y, 1 - c)
        chips = [(1 - x, y), (x, 1 - y), (1 - x, 1 - y)]

        def rows(px, py, pc):
            return out_ref.at[4 * px + 2 * py + pc]

        def copy(k, block, to, src=None):
            return pltpu.make_async_remote_copy(src_ref=rows(*block) if src is None else src, dst_ref=rows(*block),
                                                send_sem=send_sems.at[k], recv_sem=recv_sems.at[k], device_id=to, device_id_type=MESH)

        mine = pltpu.make_async_copy(x_ref, rows(*me), local_sem)
        mine.start()
        first = [copy(0, me, sibling, src=x_ref)] + [copy(1 + j, me, (*chip, c), src=x_ref) for j, chip in enumerate(chips)]
        for cp in first:
            cp.start()
        _rope_tables_into(p_ref, f_ref, s_ref, ct_ref, st_ref)
        passed = [copy(4 + j, (*chip, c), sibling) for j, chip in enumerate(chips)]
        for j, chip in enumerate(chips):
            copy(1 + j, (*chip, c), me).wait_recv()
            passed[j].start()
        copy(0, sibling, me).wait_recv()
        for j, chip in enumerate(chips):
            copy(4 + j, (*chip, 1 - c), me).wait_recv()
        for cp in first + passed:
            cp.wait_send()
        mine.wait()

    hbm, vmem = pl.BlockSpec(memory_space=pl.ANY), pl.BlockSpec(memory_space=pltpu.VMEM)
    return pl.pallas_call(
        body, name="comm_all_gather_by_chip", in_specs=[hbm, vmem, vmem, vmem], out_specs=[hbm, vmem, vmem],
        out_shape=[jax.ShapeDtypeStruct((N_DEV,) + shard.shape, shard.dtype)] + [jax.ShapeDtypeStruct((n, LANES), F32)] * 2,
        scratch_shapes=[pltpu.SemaphoreType.DMA((N_DEV - 1,)), pltpu.SemaphoreType.DMA((N_DEV - 1,)), pltpu.SemaphoreType.DMA],
        compiler_params=_params(None, 32),
    )(shard, pos_col, *_rope_lanes())


def _gather_shapes(shards):
    return [jax.ShapeDtypeStruct((N_DEV,) + s.shape, s.dtype) for s in shards]


def _gather_sems(nsh):
    return [pltpu.SemaphoreType.DMA((nsh, N_DEV - 1)), pltpu.SemaphoreType.DMA((nsh, N_DEV - 1)), pltpu.SemaphoreType.DMA((nsh,))]


def _gather_copies(ins, outs, sems, received):
    send_sems, recv_sems, local_sems = sems
    me = _my_id()
    local = [pltpu.make_async_copy(ins[a], outs[a].at[me], local_sems.at[a]) for a in range(len(ins))]
    remote = [pltpu.make_async_remote_copy(
        src_ref=ins[a], dst_ref=outs[a].at[_peer_id(k) if received else me], send_sem=send_sems.at[a, k - 1],
        recv_sem=recv_sems.at[a, k - 1], device_id=_peer(k), device_id_type=MESH)
        for a in range(len(ins)) for k in range(1, N_DEV)]
    return local, remote


def _gather_start(ins, outs, sems):
    local, sends = _gather_copies(ins, outs, sems, False)
    for cp in local + sends:
        cp.start()


def _gather_wait(ins, outs, sems):
    local, recvs = _gather_copies(ins, outs, sems, True)
    for cp in recvs:
        cp.wait_recv()
    for cp in recvs:
        cp.wait_send()
    for cp in local:
        cp.wait()


def _scatter_lands(parts):
    return [jax.ShapeDtypeStruct((N_DEV - 1,) + p.shape[1:], p.dtype) for p in parts]


def _scatter_sems(nparts):
    return [pltpu.SemaphoreType.DMA((nparts, N_DEV - 1)), pltpu.SemaphoreType.DMA((nparts, N_DEV - 1))]


def _scatter_copies(parts, lands, sems):
    send_sems, recv_sems = sems
    return [pltpu.make_async_remote_copy(
        src_ref=parts[a].at[_peer_id(k)], dst_ref=lands[a].at[k - 1], send_sem=send_sems.at[a, k - 1],
        recv_sem=recv_sems.at[a, k - 1], device_id=_peer(k), device_id_type=MESH)
        for a in range(len(parts)) for k in range(1, N_DEV)]


def _scatter_start(parts, lands, sems):
    for cp in _scatter_copies(parts, lands, sems):
        cp.start()


def _scatter_wait(parts, lands, sems):
    copies = _scatter_copies(parts, lands, sems)
    for cp in copies:
        cp.wait_recv()
    for cp in copies:
        cp.wait_send()


def _adamw(w, g, m, v):
    m2 = ADAM_B1 * m + (1.0 - ADAM_B1) * g
    v2 = ADAM_B2 * v + (1.0 - ADAM_B2) * (g * g)
    m_hat = m2 / (1.0 - ADAM_B1 ** ADAM_STEP)
    v_hat = v2 / (1.0 - ADAM_B2 ** ADAM_STEP)
    delta = -ADAM_LR * (m_hat / (jnp.sqrt(v_hat) + ADAM_EPS) + ADAM_WD * w)
    return delta, m2, v2


def _sum_adamw(land, own, w, m, v, name):
    r, c = own.shape
    rb = min(r, 256)

    def body(land_ref, own_ref, w_ref, m_ref, v_ref, g_ref, d_ref, m2_ref, v2_ref):
        me = _my_id()
        g = jnp.zeros((rb, c), F32)
        for dev in range(N_DEV):
            k = dev ^ me
            g = g + jnp.where(k == 0, own_ref[...], land_ref[jnp.maximum(k - 1, 0)].astype(F32))
        delta, m2, v2 = _adamw(w_ref[...], g, m_ref[...], v_ref[...])
        g_ref[...] = g
        d_ref[...] = delta
        m2_ref[...] = m2
        v2_ref[...] = v2

    rows = pl.BlockSpec((rb, c), lambda i: (i, 0))
    return pl.pallas_call(
        body, name=name, grid=(r // rb,), in_specs=[pl.BlockSpec((N_DEV - 1, rb, c), lambda i: (0, i, 0))] + [rows] * 4,
        out_specs=[rows] * 4, out_shape=[jax.ShapeDtypeStruct((r, c), F32)] * 4,
        compiler_params=_params(("arbitrary",), 32),
    )(land, own, w, m, v)


_SMALL = [("pre_norm_w", 2048), ("post_norm_w", 2048), ("attn_b_in", 2304), ("attn_sinks", 16), ("attn_b_out", 1024),
          ("rec_lb_logits", 2048), ("rec_gnorm_w", 128), ("loss", 1)]
_TILE = SUBLANES * LANES


def _small_rows(size):
    return -(-size // _TILE) * SUBLANES


_SMALL_OFF = {}
_r = 0
for _name, _size in _SMALL:
    _SMALL_OFF[_name] = _r
    _r += _small_rows(_size)
_SMALL_ROWS = _r


def _pack_small(pieces):
    out = []
    for name, size in _SMALL:
        flat = pieces[name].reshape(-1).astype(F32)
        out.append(jnp.pad(flat, (0, _small_rows(size) * LANES - size)).reshape(-1, LANES))
    return jnp.concatenate(out, axis=0)


def _unpack_small(packed, shapes):
    return {name: packed[_SMALL_OFF[name]:_SMALL_OFF[name] + _small_rows(size)].reshape(-1)[:size].reshape(shapes[name])
            for name, size in _SMALL}


def _small_allreduce_adamw(gpart, w, m, v):
    lb0 = _SMALL_OFF["rec_lb_logits"]

    def body(gp_ref, w_ref, m_ref, v_ref, g_ref, d_ref, m2_ref, v2_ref, land_ref, send_sems, recv_sems):
        me = _my_id()
        sent = []
        for k in range(1, N_DEV):
            cp = pltpu.make_async_remote_copy(src_ref=gp_ref, dst_ref=land_ref.at[k - 1], send_sem=send_sems.at[k - 1],
                                              recv_sem=recv_sems.at[k - 1], device_id=_peer(k), device_id_type=MESH)
            cp.start()
            sent.append(cp)
        for cp in sent:
            cp.wait_recv()
        for cp in sent:
            cp.wait_send()
        g = jnp.zeros((_SMALL_ROWS, LANES), F32)
        for dev in range(N_DEV):
            k = dev ^ me
            g = g + jnp.where(k == 0, gp_ref[...], land_ref[jnp.maximum(k - 1, 0)])
        g_ref[...] = g
        l0, l1 = w_ref[lb0:lb0 + SUBLANES, :], w_ref[lb0 + SUBLANES:lb0 + 2 * SUBLANES, :]
        mx = jnp.maximum(l0, l1)
        e0, e1 = jnp.exp(l0 - mx), jnp.exp(l1 - mx)
        p1 = e1 / (e0 + e1)
        dl1 = (1.0 - p1) * p1 * g[lb0:lb0 + SUBLANES, :]
        g_ref[lb0:lb0 + SUBLANES, :] = -dl1
        g_ref[lb0 + SUBLANES:lb0 + 2 * SUBLANES, :] = dl1
        delta, m2, v2 = _adamw(w_ref[...], g_ref[...], m_ref[...], v_ref[...])
        d_ref[...] = delta
        m2_ref[...] = m2
        v2_ref[...] = v2

    vmem = pl.BlockSpec(memory_space=pltpu.VMEM)
    return pl.pallas_call(
        body, name="comm_small_allreduce_adamw", in_specs=[vmem] * 4, out_specs=[vmem] * 4,
        out_shape=[jax.ShapeDtypeStruct((_SMALL_ROWS, LANES), F32)] * 4,
        scratch_shapes=[pltpu.VMEM((N_DEV - 1, _SMALL_ROWS, LANES), F32), pltpu.SemaphoreType.DMA((N_DEV - 1,)),
                        pltpu.SemaphoreType.DMA((N_DEV - 1,))],
    )(gpart, w, m, v)


def _qzkv(a):
    kv_end = ATTN_WIDTH + 2 * KV_WIDTH
    return jnp.concatenate([a[..., :ATTN_WIDTH], a[..., kv_end:], a[..., ATTN_WIDTH:kv_end]], axis=-1)


def _qkvz(a):
    return jnp.concatenate([a[..., :ATTN_WIDTH], a[..., 2 * ATTN_WIDTH:], a[..., ATTN_WIDTH:2 * ATTN_WIDTH]], axis=-1)


def kernel(x, positions, pre_norm_w, post_norm_w, attn_w_in, attn_b_in, attn_sinks, attn_w_out, attn_b_out, rec_w_in, rec_lb_logits, rec_gnorm_w, rec_w_out, loss_target, m_pre_norm_w, m_post_norm_w, m_attn_w_in, m_attn_b_in, m_attn_sinks, m_attn_w_out, m_attn_b_out, m_rec_w_in, m_rec_lb_logits, m_rec_gnorm_w, m_rec_w_out, v_pre_norm_w, v_post_norm_w, v_attn_w_in, v_attn_b_in, v_attn_sinks, v_attn_w_out, v_attn_b_out, v_rec_w_in, v_rec_lb_logits, v_rec_gnorm_w, v_rec_w_out):
    b_loc, t_len, _ = x.shape
    n = b_loc * t_len
    ga_in, ct, st = _all_gather_by_chip(attn_w_in[0].astype(BF16), positions.reshape(n, 1).astype(F32))
    wa_in = _qzkv(ga_in.transpose(1, 0, 2).reshape(D_MODEL, ATTN_IN))

    loss_tile, dx, landed, small = _step(
        x.reshape(n, D_MODEL), ct, st, loss_target.reshape(n, D_MODEL),
        pre_norm_w, post_norm_w, wa_in, _qzkv(attn_b_in), attn_sinks, attn_w_out[0].astype(BF16), attn_b_out,
        rec_w_in[0].astype(BF16), rec_lb_logits, rec_gnorm_w, rec_w_out[0].astype(BF16), b_loc, t_len)

    lift = lambda outs: tuple(a[None] for a in outs)
    (l_r_in, o_r_in), (l_r_out, o_r_out), (l_a_out, o_a_out), (l_a_in, o_a_in) = landed
    r_a_in = lift(_sum_adamw(l_a_in, o_a_in, attn_w_in[0], m_attn_w_in[0], v_attn_w_in[0], "adamw_attn_w_in"))
    r_r_in = lift(_sum_adamw(l_r_in, o_r_in, rec_w_in[0], m_rec_w_in[0], v_rec_w_in[0], "adamw_rec_w_in"))
    r_r_out = lift(_sum_adamw(l_r_out, o_r_out, rec_w_out[0], m_rec_w_out[0], v_rec_w_out[0], "adamw_rec_w_out"))
    r_a_out = lift(_sum_adamw(l_a_out, o_a_out, attn_w_out[0], m_attn_w_out[0], v_attn_w_out[0], "adamw_attn_w_out"))

    gsmall = dict(pre_norm_w=small["pre"], post_norm_w=small["post"], attn_b_in=_qkvz(small["ba_in"]), attn_sinks=small["sinks"],
                  attn_b_out=small["ba_out"], rec_lb_logits=jnp.concatenate([small["lb"], jnp.zeros_like(small["lb"])], axis=0),
                  rec_gnorm_w=small["gnorm"], loss=loss_tile[0:1, 0:1])
    nil = jnp.zeros((1, 1), F32)
    wsmall = dict(pre_norm_w=pre_norm_w, post_norm_w=post_norm_w, attn_b_in=attn_b_in, attn_sinks=attn_sinks,
                  attn_b_out=attn_b_out, rec_lb_logits=rec_lb_logits, rec_gnorm_w=rec_gnorm_w, loss=nil)
    msmall = dict(pre_norm_w=m_pre_norm_w, post_norm_w=m_post_norm_w, attn_b_in=m_attn_b_in, attn_sinks=m_attn_sinks,
                  attn_b_out=m_attn_b_out, rec_lb_logits=m_rec_lb_logits, rec_gnorm_w=m_rec_gnorm_w, loss=nil)
    vsmall = dict(pre_norm_w=v_pre_norm_w, post_norm_w=v_post_norm_w, attn_b_in=v_attn_b_in, attn_sinks=v_attn_sinks,
                  attn_b_out=v_attn_b_out, rec_lb_logits=v_rec_lb_logits, rec_gnorm_w=v_rec_gnorm_w, loss=nil)
    shapes = {k: a.shape for k, a in wsmall.items()}
    packed = _small_allreduce_adamw(_pack_small(gsmall), _pack_small(wsmall), _pack_small(msmall), _pack_small(vsmall))
    sg, sd, sm, sv = [_unpack_small(a, shapes) for a in packed]

    big = {"attn_w_in": r_a_in, "attn_w_out": r_a_out, "rec_w_in": r_r_in, "rec_w_out": r_r_out}
    order = ["pre_norm_w", "post_norm_w", "attn_w_in", "attn_b_in", "attn_sinks", "attn_w_out", "attn_b_out", "rec_w_in",
             "rec_lb_logits", "rec_gnorm_w", "rec_w_out"]
    outs = [sg["loss"][0, 0], dx.reshape(b_loc, t_len, D_MODEL)]
    for idx, small_set in enumerate((sg, sd, sm, sv)):
        outs += [big[nm][idx] if nm in big else small_set[nm] for nm in order]
    return tuple(outs)
```

```python
import numpy as np
import jax
import jax.numpy as jnp
from jax import lax
from jax.experimental import pallas as pl
from jax.experimental.pallas import tpu as pltpu

F32, BF16 = jnp.float32, jnp.bfloat16
MESH = pl.DeviceIdType.MESH
N_DEV = 8

D_MODEL = 1024
N_HEADS, HEAD_DIM, N_KV, GROUP = 16, 64, 2, 8
ATTN_WIDTH, KV_WIDTH = 1024, 128
ATTN_IN = 2 * ATTN_WIDTH + 2 * KV_WIDTH
BLK = 128
ROPE_THETA, ROPE_HALF = 500000.0, 8
REC_HEADS, REC_K = 8, 128
REC_WIDTH = REC_HEADS * REC_K
REC_IN = 4 * REC_WIDTH
TN_ROWS = 2048
CH = 32
NORM_EPS = 1e-6
F32_TINY = 1.1754944e-38
ADAM_LR, ADAM_B1, ADAM_B2, ADAM_EPS, ADAM_WD, ADAM_STEP = 0.001, 0.9, 0.999, 1e-08, 0.01, 10

LANES, SUBLANES = 128, 8
TM = 512
NT_DIMS = (((1,), (1,)), ((), ()))
TN_DIMS = (((0,), (0,)), ((), ()))
MB = 2 ** 20


def _params(sem=None, vmem_mb=48, **kw):
    return pltpu.CompilerParams(dimension_semantics=sem, vmem_limit_bytes=vmem_mb * MB, **kw)


def _wide_rows(n):
    return 2 * TM if n % (2 * TM) == 0 else TM


def _col_chunk(m):
    return 768 if m % 1024 else 1024


def _sigmoid(x):
    return 1.0 / (1.0 + jnp.exp(-x))


def _split3(x):
    hi = x.astype(BF16)
    r1 = x - hi.astype(F32)
    mid = r1.astype(BF16)
    lo = (r1 - mid.astype(F32)).astype(BF16)
    return hi, mid, lo


def _dot3(l_bf, x):
    hi, mid, lo = _split3(x)
    return (jnp.dot(l_bf, hi, preferred_element_type=F32) + jnp.dot(l_bf, mid, preferred_element_type=F32)
            + jnp.dot(l_bf, lo, preferred_element_type=F32))


def _rope_lanes():
    lane = np.arange(LANES) % HEAD_DIM
    inv = np.float32(ROPE_THETA) ** (-(np.arange(ROPE_HALF, dtype=np.float32) * np.float32(2.0) / np.float32(2 * ROPE_HALF)))
    freq = np.where(lane < 2 * ROPE_HALF, inv[lane % ROPE_HALF], 0.0).astype(np.float32)[None, :]
    sign = np.where(lane < ROPE_HALF, -1.0, np.where(lane < 2 * ROPE_HALF, 1.0, 0.0)).astype(np.float32)[None, :]
    return jnp.asarray(freq), jnp.asarray(sign)


def _rope_tables_into(p_ref, f_ref, s_ref, c_out, s_out):
    def rows(i, carry):
        rs = pl.ds(pl.multiple_of(i * TM, TM), TM)
        ang = p_ref[rs, :] * f_ref[...]
        c_out[rs, :] = jnp.cos(ang)
        s_out[rs, :] = jnp.sin(ang) * s_ref[...]
        return carry

    lax.fori_loop(0, p_ref.shape[0] // TM, rows, 0)


def _rope_apply(xv, c, s, lm):
    partner = jnp.where(lm < ROPE_HALF, pltpu.roll(xv, LANES - ROPE_HALF, 1), pltpu.roll(xv, ROPE_HALF, 1))
    return xv * c + partner * s


def _rope_bwd(dy, c, s, lm):
    t = dy * s
    partner = jnp.where(lm < ROPE_HALF, pltpu.roll(t, LANES - ROPE_HALF, 1),
                        jnp.where(lm < 2 * ROPE_HALF, pltpu.roll(t, ROPE_HALF, 1), 0.0))
    return dy * c + partner


def _lower_bound(lb_logits):
    def body(l_ref, o_ref):
        l0, l1 = l_ref[0:1, :], l_ref[1:2, :]
        m = jnp.maximum(l0, l1)
        e0, e1 = jnp.exp(l0 - m), jnp.exp(l1 - m)
        o_ref[...] = e1 / (e0 + e1)

    return pl.pallas_call(body, name="lower_bound", out_shape=jax.ShapeDtypeStruct((1, lb_logits.shape[1]), F32))(lb_logits)


def _norm_matmul(x, pw, w, bias, name, shards=()):
    n, m = x.shape[0], w.shape[1]
    cn = _col_chunk(m)
    has_bias = bias is not None
    nsh, steps = len(shards), n // TM

    def body(*refs):
        refs = list(refs)
        x_ref, pw_ref, w_ref = refs[:3]
        b_ref = refs[3] if has_bias else None
        refs = refs[4 if has_bias else 3:]
        sh_in, (p_ref, h_ref), sh_out, sems = refs[:nsh], refs[nsh:nsh + 2], refs[nsh + 2:2 * nsh + 2], refs[2 * nsh + 2:]
        if nsh:
            @pl.when(pl.program_id(0) == 0)
            def _():
                _gather_start(sh_in, sh_out, sems)

        xv = x_ref[...]
        r = lax.rsqrt(jnp.mean(xv * xv, axis=-1, keepdims=True) + NORM_EPS)
        h = ((xv * r) * pw_ref[...]).astype(BF16)
        h_ref[...] = h
        for j in range(0, m, cn):
            acc = jnp.dot(h, w_ref[:, j:j + cn], preferred_element_type=F32)
            if has_bias:
                acc = acc + b_ref[:, j:j + cn]
            p_ref[:, j:j + cn] = acc

        if nsh:
            @pl.when(pl.program_id(0) == steps - 1)
            def _():
                _gather_wait(sh_in, sh_out, sems)

    rows = pl.BlockSpec((TM, D_MODEL), lambda i: (i, 0))
    const = lambda shape: pl.BlockSpec(shape, lambda i: (0, 0))
    hbm = pl.BlockSpec(memory_space=pl.ANY)
    in_specs = [rows, const((1, D_MODEL)), const((D_MODEL, m))] + ([const((1, m))] if has_bias else []) + [hbm] * nsh
    args = (x, pw, w) + ((bias,) if has_bias else ()) + tuple(shards)
    return pl.pallas_call(
        body, name=name, grid=(steps,), in_specs=in_specs,
        out_specs=[pl.BlockSpec((TM, m), lambda i: (i, 0)), rows] + [hbm] * nsh,
        out_shape=[jax.ShapeDtypeStruct((n, m), F32), jax.ShapeDtypeStruct((n, D_MODEL), BF16)] + _gather_shapes(shards),
        scratch_shapes=_gather_sems(nsh) if nsh else [],
        compiler_params=_params(("arbitrary",), 56),
    )(*args)


def _outproj_postnorm(g, w, bias, xres, qw, name):
    n = g.shape[0]

    def body(g_ref, w_ref, b_ref, x_ref, qw_ref, y_ref, o_ref):
        y = jnp.dot(g_ref[...], w_ref[...], preferred_element_type=F32) + b_ref[...]
        y_ref[...] = y
        r = lax.rsqrt(jnp.mean(y * y, axis=-1, keepdims=True) + NORM_EPS)
        o_ref[...] = x_ref[...] + (y * r) * qw_ref[...]

    tm = _wide_rows(n)
    rows = pl.BlockSpec((tm, D_MODEL), lambda i: (i, 0))
    const = lambda shape: pl.BlockSpec(shape, lambda i: (0, 0))
    return pl.pallas_call(
        body, name=name, grid=(n // tm,),
        in_specs=[rows, const((D_MODEL, D_MODEL)), const((1, D_MODEL)), rows, const((1, D_MODEL))],
        out_specs=[rows, rows], out_shape=[jax.ShapeDtypeStruct((n, D_MODEL), F32)] * 2,
        compiler_params=_params(("arbitrary",), 48),
    )(g, w, bias, xres, qw)


def _outproj_loss_bwd(g, w, xres, qw, tgt, name):
    n = g.shape[0]
    tm = _wide_rows(n)
    steps = n // tm

    def body(g_ref, w_ref, x_ref, qw_ref, t_ref, dx_ref, dg_ref, dy_ref, dqw_ref, loss_ref, acc_ref):
        i = pl.program_id(0)

        @pl.when(i == 0)
        def _():
            acc_ref[...] = jnp.zeros_like(acc_ref)
            dqw_ref[...] = jnp.zeros_like(dqw_ref)

        y = jnp.dot(g_ref[...], w_ref[...], preferred_element_type=F32)
        r = lax.rsqrt(jnp.mean(y * y, axis=-1, keepdims=True) + NORM_EPS)
        u = y * r
        e = (x_ref[...] + u * qw_ref[...]) - t_ref[...]
        dxn = e * (1.0 / D_MODEL)
        dx_ref[...] = dxn
        acc_ref[...] += jnp.sum(e * e, axis=0, keepdims=True)
        du = dxn * qw_ref[...]
        dy = (r * (du - u * jnp.mean(du * u, axis=-1, keepdims=True))).astype(BF16)
        dqw_ref[...] += jnp.sum(dxn * u, axis=0, keepdims=True)
        dy_ref[...] = dy
        dg_ref[...] = lax.dot_general(dy, w_ref[...], NT_DIMS, preferred_element_type=F32)

        @pl.when(i == steps - 1)
        def _():
            loss_ref[...] = jnp.full(loss_ref.shape, jnp.sum(acc_ref[...]) * (0.5 / D_MODEL), F32)

    rows = pl.BlockSpec((tm, D_MODEL), lambda i: (i, 0))
    const = lambda shape: pl.BlockSpec(shape, lambda i: (0, 0))
    return pl.pallas_call(
        body, name=name, grid=(steps,),
        in_specs=[rows, const((D_MODEL, D_MODEL)), rows, const((1, D_MODEL)), rows],
        out_specs=[rows, rows, rows, const((1, D_MODEL)), const((SUBLANES, LANES))],
        out_shape=[jax.ShapeDtypeStruct((n, D_MODEL), F32), jax.ShapeDtypeStruct((n, D_MODEL), F32),
                   jax.ShapeDtypeStruct((n, D_MODEL), BF16), jax.ShapeDtypeStruct((1, D_MODEL), F32),
                   jax.ShapeDtypeStruct((SUBLANES, LANES), F32)],
        scratch_shapes=[pltpu.VMEM((1, D_MODEL), F32)], compiler_params=_params(("arbitrary",), 56),
    )(g, w, xres, qw, tgt)


_QCOL, _ZCOL, _KCOL, _VCOL = 0, 1024, 2048, 2176


def _head_stack(chunks, heads, lt64):
    return jnp.concatenate([jnp.where(lt64 if n % 2 == 0 else ~lt64, chunks[n // 2], 0.0) for n in heads], axis=0)


def _dup_half(x, h, lt64):
    r = pltpu.roll(x, HEAD_DIM, 1)
    return jnp.where(lt64, x, r) if h == 0 else jnp.where(lt64, r, x)


def _pair_chunk(xt, c2):
    a, b = 2 * c2, 2 * c2 + 1
    return jnp.concatenate([xt[:HEAD_DIM, a * BLK:(a + 1) * BLK], xt[HEAD_DIM:, b * BLK:(b + 1) * BLK]], axis=0).T


def _attn_mask_t(i):
    key = lax.broadcasted_iota(jnp.int32, (2 * BLK, BLK), 0)
    qry = lax.broadcasted_iota(jnp.int32, (2 * BLK, BLK), 1)
    valid = (key > qry) & (key <= qry + BLK) & ((key >= BLK) | (i > 0))
    return jnp.tile(jnp.where(valid, 0.0, -1e30), (1, GROUP))


def _attn_probs_t(s, heads, sink_ref, mask):
    s = s + mask
    head = lax.broadcasted_iota(jnp.int32, (1, len(heads) * BLK), 1) >> 7
    sk = jnp.zeros((1, len(heads) * BLK), F32)
    for j, n in enumerate(heads):
        sk = jnp.where(head == j, sink_ref[0, n], sk)
    m = jnp.maximum(jnp.max(s, axis=0, keepdims=True), sk)
    p = jnp.exp(s - m)
    esk = jnp.exp(sk - m)
    inv = 1.0 / (jnp.sum(p, axis=0, keepdims=True) + esk)
    return p * inv, esk * inv


def _attn_fwd(p, ct, st, sinks, b_loc, nb, shards):
    n = p.shape[0]
    nsh = len(shards)

    def body(sink_ref, q_ref, z_ref, kc_ref, kp_ref, vc_ref, vp_ref, cc_ref, sc_ref, cp_ref, sp_ref, *rest):
        sh_in, (o_ref, g_ref), sh_out, sems = rest[:nsh], rest[nsh:nsh + 2], rest[nsh + 2:2 * nsh + 2], rest[2 * nsh + 2:]
        b, i = pl.program_id(0), pl.program_id(1)

        @pl.when((b == 0) & (i == 0))
        def _():
            _gather_start(sh_in, sh_out, sems)

        lane = lax.broadcasted_iota(jnp.int32, (BLK, LANES), 1)
        lm = lane & (HEAD_DIM - 1)
        cc, sc = cc_ref[...], sc_ref[...]
        kcat = jnp.concatenate([_rope_apply(kp_ref[...], cp_ref[...], sp_ref[...], lm),
                                _rope_apply(kc_ref[...], cc, sc, lm)], axis=0)
        vcat = jnp.concatenate([vp_ref[...], vc_ref[...]], axis=0)
        qr = [_rope_apply(q_ref[:, c * LANES:(c + 1) * LANES], cc, sc, lm) * (HEAD_DIM ** -0.5) for c in range(8)]
        valid = _attn_mask_t(i)
        lt64, lt64k = lane < HEAD_DIM, lax.broadcasted_iota(jnp.int32, (2 * BLK, LANES), 1) < HEAD_DIM
        def kv_head(h):
            heads = list(range(h * GROUP, (h + 1) * GROUP))
            kext, vext = _dup_half(kcat, h, lt64k).astype(BF16), _dup_half(vcat, h, lt64k).astype(BF16)
            qst = _head_stack(qr, heads, lt64).astype(BF16)
            s = lax.dot_general(kext, qst, NT_DIMS, preferred_element_type=F32)
            yield
            pn, _ = _attn_probs_t(s, heads, sink_ref, valid)
            ot = lax.dot_general(vext, pn.astype(BF16), TN_DIMS, preferred_element_type=F32)
            yield
            for c2 in range(GROUP // 2):
                oc = _pair_chunk(ot, c2)
                cols = slice((4 * h + c2) * LANES, (4 * h + c2 + 1) * LANES)
                zc = z_ref[:, cols]
                o_ref[:, cols] = oc
                g_ref[:, cols] = (oc * (zc * _sigmoid(zc))).astype(BF16)

        _in_stages([kv_head(h) for h in range(N_KV)])

        @pl.when((b == b_loc - 1) & (i == nb - 1))
        def _():
            _gather_wait(sh_in, sh_out, sems)

    cur = lambda b, i: b * nb + i
    prev = lambda b, i: b * nb + jnp.maximum(i - 1, 0)
    wide = lambda cb: pl.BlockSpec((BLK, ATTN_WIDTH), lambda b, i: (cur(b, i), cb))
    kv = lambda rowf, cb: pl.BlockSpec((BLK, LANES), lambda b, i: (rowf(b, i), cb))
    hbm = pl.BlockSpec(memory_space=pl.ANY)
    in_specs = [pl.BlockSpec(memory_space=pltpu.SMEM), wide(0), wide(1),
                kv(cur, _KCOL // LANES), kv(prev, _KCOL // LANES), kv(cur, _VCOL // LANES), kv(prev, _VCOL // LANES),
                kv(cur, 0), kv(cur, 0), kv(prev, 0), kv(prev, 0)] + [hbm] * nsh
    return pl.pallas_call(
        body, name="attn_fwd", grid=(b_loc, nb), in_specs=in_specs, out_specs=[wide(0), wide(0)] + [hbm] * nsh,
        out_shape=[jax.ShapeDtypeStruct((n, ATTN_WIDTH), F32), jax.ShapeDtypeStruct((n, ATTN_WIDTH), BF16)] + _gather_shapes(shards),
        scratch_shapes=_gather_sems(nsh), compiler_params=_params(("arbitrary", "arbitrary"), 48),
    )(sinks, p, p, p, p, p, p, ct, st, ct, st, *shards)


def _attn_bwd(p, ct, st, sinks, o, dg, b_loc, nb, parts):
    n = p.shape[0]
    nparts = len(parts)

    def body(sink_ref, q_ref, z_ref, kc_ref, kp_ref, vc_ref, vp_ref, cc_ref, sc_ref, cp_ref, sp_ref, o_ref, dg_ref, *rest):
        part_refs, (dp_ref, ds_ref), land_refs = rest[:nparts], rest[nparts:nparts + 2], rest[nparts + 2:2 * nparts + 2]
        dq_s, dz_s, dk_s, dv_s = rest[2 * nparts + 2:2 * nparts + 6]
        sems = rest[2 * nparts + 6:]
        b, i = pl.program_id(0), pl.program_id(1)

        @pl.when((b == 0) & (i == 0))
        def _():
            _scatter_start(part_refs, land_refs, sems)

        @pl.when((b == b_loc - 1) & (i == nb))
        def _():
            _scatter_wait(part_refs, land_refs, sems)

        lane = lax.broadcasted_iota(jnp.int32, (BLK, LANES), 1)
        lm = lane & (HEAD_DIM - 1)

        @pl.when((b == 0) & (i == 0))
        def _():
            ds_ref[...] = jnp.zeros_like(ds_ref)

        @pl.when(i < nb)
        def _compute():
            cc, sc = cc_ref[...], sc_ref[...]
            kcat = jnp.concatenate([_rope_apply(kp_ref[...], cp_ref[...], sp_ref[...], lm),
                                    _rope_apply(kc_ref[...], cc, sc, lm)], axis=0)
            vcat = jnp.concatenate([vp_ref[...], vc_ref[...]], axis=0)
            qr = [_rope_apply(q_ref[:, c * LANES:(c + 1) * LANES], cc, sc, lm) * (HEAD_DIM ** -0.5) for c in range(8)]
            valid = _attn_mask_t(i)
            lt64, lt64k = lane < HEAD_DIM, lax.broadcasted_iota(jnp.int32, (2 * BLK, LANES), 1) < HEAD_DIM
            do_chunks, doo_chunks, dz_chunks = [], [], []
            for c in range(8):
                cols = slice(c * LANES, (c + 1) * LANES)
                zc, oc, dgc = z_ref[:, cols], o_ref[:, cols], dg_ref[:, cols]
                sg = _sigmoid(zc)
                do_chunks.append(dgc * (zc * sg))
                dz_chunks.append(dgc * oc * (sg * (1.0 + zc * (1.0 - sg))))
                doo_chunks.append(do_chunks[c] * oc)
            dq_chunks = [None] * 8
            dk_h, dv_h, ds_parts = [None] * N_KV, [None] * N_KV, [None] * N_KV
            tile_lane = lax.broadcasted_iota(jnp.int32, (SUBLANES, LANES), 1)
            tile_row = lax.broadcasted_iota(jnp.int32, (SUBLANES, LANES), 0)
            ones8 = jnp.ones((SUBLANES, LANES), BF16)

            def kv_head(h):
                heads = list(range(h * GROUP, (h + 1) * GROUP))
                kext = _dup_half(kcat, h, lt64k)
                kext_bf, kext_t = kext.astype(BF16), kext.T.astype(BF16)
                vext = _dup_half(vcat, h, lt64k).astype(BF16)
                qst = _head_stack(qr, heads, lt64).astype(BF16)
                pn, psink = _attn_probs_t(lax.dot_general(kext_bf, qst, NT_DIMS, preferred_element_type=F32), heads, sink_ref, valid)
                do_bf = _head_stack(do_chunks, heads, lt64).astype(BF16)
                delta = sum(lax.dot_general(ones8, part, NT_DIMS, preferred_element_type=F32)
                            for part in _split3(_head_stack(doo_chunks, heads, lt64)))[0:1, :]
                dpt = lax.dot_general(vext, do_bf, NT_DIMS, preferred_element_type=F32)
                dst = (pn * (dpt - delta)).astype(BF16)
                sink_term = psink * delta
                ds_acc = jnp.zeros((SUBLANES, LANES), F32)
                for j, n in enumerate(heads):
                    val = -jnp.sum(sink_term[:, j * BLK:(j + 1) * BLK])
                    ds_acc = ds_acc + jnp.where((tile_lane == n) & (tile_row == 0), val, 0.0)
                ds_parts[h] = ds_acc
                dqt = jnp.dot(kext_t, dst, preferred_element_type=F32) * (HEAD_DIM ** -0.5)
                dk_ext = jnp.dot(dst, qst, preferred_element_type=F32)
                dv_ext = jnp.dot(pn.astype(BF16), do_bf, preferred_element_type=F32)
                dk_h[h] = dk_ext + pltpu.roll(dk_ext, HEAD_DIM, 1)
                dv_h[h] = dv_ext + pltpu.roll(dv_ext, HEAD_DIM, 1)
                for c2 in range(GROUP // 2):
                    dq_chunks[4 * h + c2] = _rope_bwd(_pair_chunk(dqt, c2), cc, sc, lm)

            for h in range(N_KV):
                kv_head(h)
            ds_ref[...] += ds_parts[0] + ds_parts[1]
            dk_full = jnp.where(lt64k, dk_h[0], dk_h[1])
            dv_full = jnp.where(lt64k, dv_h[0], dv_h[1])

            @pl.when(i >= 1)
            def _emit():
                dp_ref[:, _QCOL:_QCOL + ATTN_WIDTH] = dq_s[...]
                dp_ref[:, _ZCOL:_ZCOL + ATTN_WIDTH] = dz_s[...]
                dp_ref[:, _KCOL:_KCOL + KV_WIDTH] = _rope_bwd(dk_s[...] + dk_full[:BLK], cp_ref[...], sp_ref[...], lm)
                dp_ref[:, _VCOL:_VCOL + KV_WIDTH] = dv_s[...] + dv_full[:BLK]

            for c in range(8):
                dq_s[:, c * LANES:(c + 1) * LANES] = dq_chunks[c]
                dz_s[:, c * LANES:(c + 1) * LANES] = dz_chunks[c]
            dk_s[...] = dk_full[BLK:]
            dv_s[...] = dv_full[BLK:]

        @pl.when(i == nb)
        def _final():
            dp_ref[:, _QCOL:_QCOL + ATTN_WIDTH] = dq_s[...]
            dp_ref[:, _ZCOL:_ZCOL + ATTN_WIDTH] = dz_s[...]
            dp_ref[:, _KCOL:_KCOL + KV_WIDTH] = _rope_bwd(dk_s[...], cc_ref[...], sc_ref[...], lm)
            dp_ref[:, _VCOL:_VCOL + KV_WIDTH] = dv_s[...]

    cur = lambda b, i: b * nb + jnp.minimum(i, nb - 1)
    prev = lambda b, i: b * nb + jnp.maximum(jnp.minimum(i, nb - 1) - 1, 0)
    emit = lambda b, i: b * nb + jnp.maximum(i - 1, 0)
    hbm = pl.BlockSpec(memory_space=pl.ANY)
    wide = lambda cb: pl.BlockSpec((BLK, ATTN_WIDTH), lambda b, i: (cur(b, i), cb))
    kv = lambda rowf, cb: pl.BlockSpec((BLK, LANES), lambda b, i: (rowf(b, i), cb))
    in_specs = [pl.BlockSpec(memory_space=pltpu.SMEM), wide(0), wide(1),
                kv(cur, _KCOL // LANES), kv(prev, _KCOL // LANES), kv(cur, _VCOL // LANES), kv(prev, _VCOL // LANES),
                kv(cur, 0), kv(cur, 0), kv(prev, 0), kv(prev, 0), wide(0), wide(0)] + [hbm] * nparts
    out_specs = [pl.BlockSpec((BLK, ATTN_IN), lambda b, i: (emit(b, i), 0)),
                 pl.BlockSpec((SUBLANES, LANES), lambda b, i: (0, 0))] + [hbm] * nparts
    return pl.pallas_call(
        body, name="attn_bwd", grid=(b_loc, nb + 1), in_specs=in_specs, out_specs=out_specs,
        out_shape=[jax.ShapeDtypeStruct((n, ATTN_IN), F32), jax.ShapeDtypeStruct((SUBLANES, LANES), F32)] + _scatter_lands(parts),
        scratch_shapes=[pltpu.VMEM((BLK, ATTN_WIDTH), F32), pltpu.VMEM((BLK, ATTN_WIDTH), F32),
                        pltpu.VMEM((BLK, KV_WIDTH), F32), pltpu.VMEM((BLK, KV_WIDTH), F32)] + _scatter_sems(nparts),
        compiler_params=_params(("arbitrary", "arbitrary"), 48),
    )(sinks, p, p, p, p, p, p, ct, st, ct, st, o, dg, *parts)


_CUM_ROWS = 256
HALF = CH // 2
_ROW0 = [SUBLANES * (s // SUBLANES) for s in range(CH)]
_ROW1 = [HALF * (s // HALF + 1) for s in range(CH)]
_ROWS_OF = [_ROW1[s] - _ROW0[s] for s in range(CH)]
_OFF_OF = [sum(_ROWS_OF[:s]) for s in range(CH)]


def _tri(lower):
    r = lax.broadcasted_iota(jnp.int32, (_CUM_ROWS, _CUM_ROWS), 0)
    c = lax.broadcasted_iota(jnp.int32, (_CUM_ROWS, _CUM_ROWS), 1)
    same = (r ^ c) < CH
    return (same & ((c <= r) if lower else (c >= r))).astype(BF16)


def _gates(qp, fp, lb):
    e = jnp.exp(-jnp.abs(fp))
    r = 1.0 / (1.0 + e)
    sig_neg = jnp.where(fp >= 0, e, 1.0) * r
    sig = jnp.where(fp >= 0, 1.0, e) * r
    g = jnp.log(jnp.maximum(lb + (1.0 - lb) * sig, F32_TINY))
    return qp * _sigmoid(qp), g, (1.0 - lb) * sig_neg, sig_neg


def _pair_rows(bc, s):
    diff = bc[_ROW0[s]:_ROW1[s], :] - bc[s:s + 1, :]
    head = jnp.minimum(diff[:SUBLANES, :], 0.0)
    return jnp.exp(head if diff.shape[0] == SUBLANES else jnp.concatenate([head, diff[SUBLANES:, :]], axis=0))


def _cross_half(q, k, bc):
    r = bc[HALF - 1:HALF, :]
    e1, e0 = jnp.exp(bc[HALF:, :] - r), jnp.exp(r - bc[:HALF, :])
    return q[HALF:, :] * e1, e1, k[:HALF, :] * e0, e0


HP = 8
REC_TB = 256
_HW = HP * REC_K


def _hgrn_specs(tb, nt, reverse):
    tmap = (lambda t: nt - 1 - t) if reverse else (lambda t: t)
    groups = REC_HEADS // HP
    blk = lambda cb: pl.BlockSpec((tb, _HW), lambda h, b, t: (b * nt + tmap(t), cb * groups + h))
    head = pl.BlockSpec((tb, _HW), lambda h, b, t: (b * nt + tmap(t), h))
    lbs = pl.BlockSpec((1, _HW), lambda h, b, t: (0, h))
    gws = pl.BlockSpec((1, REC_K), lambda h, b, t: (0, 0))
    hist = pl.BlockSpec((HP, 1, tb // CH, REC_K, REC_K), lambda h, b, t: (h, b, tmap(t), 0, 0))
    return blk, head, lbs, gws, hist


def _chunk_rows(c, first=0, size=CH):
    start = c * CH + first
    return pl.ds(start if isinstance(start, int) else pl.multiple_of(start, CH if first % CH == 0 else SUBLANES), size)


def _in_stages(heads):
    live = list(heads)
    while live:
        live = [g for g in live if next(g, live) is not live]


def _cumsum_chunks(tri, x, out_ref, tb):
    for r in range(0, tb, _CUM_ROWS):
        out_ref[r:r + _CUM_ROWS, :] = _dot3(tri, x[r:r + _CUM_ROWS, :])


def _hgrn_fwd(p, lb, gw, b_loc, t_len):
    n = p.shape[0]
    tb = min(REC_TB, t_len)
    nt, nck = t_len // tb, tb // CH

    def body(qp_ref, fp_ref, i_ref, z_ref, lb_ref, gw_ref, oraw_ref, g_ref, sh_ref, q_s, k_s, b_s, o_s, st_ref,
             car_o, car_a, car_s, car_st):
        @pl.when(pl.program_id(2) == 0)
        def _():
            st_ref[...] = jnp.zeros_like(st_ref)

        qv, g, kk, _ = _gates(qp_ref[...], fp_ref[...], lb_ref[...])
        q_s[...] = qv
        k_s[...] = kk
        _cumsum_chunks(_tri(True), g, b_s, tb)
        ones = jnp.ones((REC_K, REC_K), BF16)
        sub = lax.broadcasted_iota(jnp.int32, (SUBLANES, REC_K), 0)

        rows_of = _chunk_rows

        def issue(c, hp):
            rs, cs = rows_of(c), slice(hp * REC_K, (hp + 1) * REC_K)
            q, k, bc, v = q_s[rs, cs], k_s[rs, cs], b_s[rs, cs], i_ref[rs, cs]
            st = st_ref[hp]
            sh_ref[hp, 0, c] = st
            o = lax.dot_general((q * jnp.exp(bc)).astype(BF16), st.astype(BF16), NT_DIMS, preferred_element_type=F32)
            w = jnp.concatenate([q[_ROW0[s]:_ROW1[s], :] * _pair_rows(bc, s) * k[s:s + 1, :] for s in range(CH)], axis=0)
            a = jnp.dot(w.astype(BF16), ones, preferred_element_type=F32)
            qe1, _, ke0, _ = _cross_half(q, k, bc)
            s10 = lax.dot_general(qe1.astype(BF16), ke0.astype(BF16), NT_DIMS, preferred_element_type=F32)
            kd = k * jnp.exp(bc[CH - 1:CH, :] - bc)
            st_new = lax.dot_general(v.astype(BF16), kd.astype(BF16), TN_DIMS, preferred_element_type=F32)
            return o, a, s10, st_new

        def advance_state(c, hp, st_new):
            bl = b_s[_chunk_rows(c, CH - SUBLANES, SUBLANES), hp * REC_K:(hp + 1) * REC_K][SUBLANES - 1:, :]
            st_ref[hp] = st_ref[hp] * jnp.exp(bl) + st_new

        def cross(c, hp, s10):
            v0 = i_ref[_chunk_rows(c, 0, HALF), hp * REC_K:(hp + 1) * REC_K]
            return jnp.dot(s10.astype(BF16), v0.astype(BF16), preferred_element_type=F32)

        def finish(c, hp, o, a, o_cross):
            rs, cs = rows_of(c), slice(hp * REC_K, (hp + 1) * REC_K)
            v = i_ref[rs, cs]
            acc = [jnp.zeros((SUBLANES, REC_K), F32) for _ in range(CH // SUBLANES)]
            for s in range(CH):
                j = s // SUBLANES
                vs = v[s:s + 1, :]
                for jj in range(j, _ROW1[s] // SUBLANES):
                    blk = a[_OFF_OF[s] + (jj - j) * SUBLANES:_OFF_OF[s] + (jj - j + 1) * SUBLANES, :]
                    if jj == j:
                        blk = jnp.where(sub >= s - j * SUBLANES, blk, 0.0)
                    acc[jj] = acc[jj] + blk * vs
            o_s[rs, cs] = o + jnp.concatenate(acc, axis=0) + jnp.concatenate([jnp.zeros((HALF, REC_K), F32), o_cross], axis=0)

        def park(slot, results):
            for hp, (o, a, s10, st_new) in enumerate(results):
                car_o[slot, hp], car_a[slot, hp], car_s[slot, hp], car_st[slot, hp] = o, a, s10, st_new

        def retire(c, slot):
            for hp in range(HP):
                advance_state(c, hp, car_st[slot, hp])
            yield
            crosses = [cross(c, hp, car_s[slot, hp]) for hp in range(HP)]
            for hp in range(HP):
                finish(c, hp, car_o[slot, hp], car_a[slot, hp], crosses[hp])

        def step(c, slot):
            closing = retire(c - 1, slot)
            next(closing)
            park(1 - slot, [issue(c, hp) for hp in range(HP)])
            next(closing, None)

        def trip(j, carry):
            step(2 * j + 1, 0)
            step(2 * j + 2, 1)
            return carry

        assert nck % 2 == 0
        park(0, [issue(0, hp) for hp in range(HP)])
        lax.fori_loop(0, nck // 2 - 1, trip, 0)
        step(nck - 1, 0)
        for _ in retire(nck - 1, 1):
            pass
        oraw_ref[...] = o_s[...]
        for hp in range(HP):
            cs = slice(hp * REC_K, (hp + 1) * REC_K)
            o, zc = o_s[:, cs], z_ref[:, cs]
            on = (o * lax.rsqrt(jnp.mean(o * o, axis=-1, keepdims=True) + NORM_EPS)) * gw_ref[...]
            g_ref[:, cs] = (on * (zc * _sigmoid(zc))).astype(BF16)

    blk, head, lbs, gws, hist = _hgrn_specs(tb, nt, False)
    return pl.pallas_call(
        body, name="hgrn_fwd", grid=(REC_HEADS // HP, b_loc, nt),
        in_specs=[blk(0), blk(1), blk(2), blk(3), lbs, gws], out_specs=[head, head, hist],
        out_shape=[jax.ShapeDtypeStruct((n, REC_WIDTH), F32), jax.ShapeDtypeStruct((n, REC_WIDTH), BF16),
                   jax.ShapeDtypeStruct((REC_HEADS, b_loc, t_len // CH, REC_K, REC_K), F32)],
        scratch_shapes=[pltpu.VMEM((tb, _HW), F32)] * 4 + [pltpu.VMEM((HP, REC_K, REC_K), F32)] + [
            pltpu.VMEM((2, HP, CH, REC_K), F32), pltpu.VMEM((2, HP, sum(_ROWS_OF), REC_K), F32),
            pltpu.VMEM((2, HP, HALF, HALF), F32), pltpu.VMEM((2, HP, REC_K, REC_K), F32)],
        compiler_params=_params(("arbitrary", "arbitrary", "arbitrary"), 48),
    )(p, p, p, p, lb, gw)


def _hgrn_bwd(p, lb, gw, oraw, sh, dg, b_loc, t_len):
    n = p.shape[0]
    tb = min(REC_TB, t_len)
    nt, nck = t_len // tb, tb // CH
    assert HP == REC_HEADS

    def body(qp_ref, fp_ref, i_ref, z_ref, lb_ref, gw_ref, oraw_ref, dg_ref, sh_ref,
             dp_ref, dlb_ref, dgw_ref,
             q_s, k_s, b_s, do_s, dqv_s, dk_s, db_s, dst_ref, car_r, car_dst, car_dec, car_a, car_da, car_x):
        dq_ref, df_ref, di_ref, dz_ref = (dp_ref.at[:, part * REC_WIDTH:(part + 1) * REC_WIDTH] for part in range(4))
        b, t = pl.program_id(1), pl.program_id(2)

        @pl.when(t == 0)
        def _():
            dst_ref[...] = jnp.zeros_like(dst_ref)

        @pl.when((b == 0) & (t == 0))
        def _():
            dlb_ref[...] = jnp.zeros_like(dlb_ref)
            dgw_ref[...] = jnp.zeros_like(dgw_ref)

        lbv, qp, fp = lb_ref[...], qp_ref[...], fp_ref[...]
        qv, g, kk, sig_neg = _gates(qp, fp, lbv)
        q_s[...] = qv
        k_s[...] = kk
        _cumsum_chunks(_tri(True), g, b_s, tb)
        gwv = gw_ref[...]
        for hp in range(HP):
            cs = slice(hp * REC_K, (hp + 1) * REC_K)
            o, zc, dgv = oraw_ref[:, cs], z_ref[:, cs], dg_ref[:, cs]
            rn = lax.rsqrt(jnp.mean(o * o, axis=-1, keepdims=True) + NORM_EPS)
            on = o * rn
            sgz = _sigmoid(zc)
            dz_ref[:, cs] = (dgv * (on * gwv) * (sgz * (1.0 + zc * (1.0 - sgz)))).astype(BF16)
            dpre = dgv * (zc * sgz)
            dgw_ref[hp] += jnp.sum(dpre * on, axis=0, keepdims=True)
            don = dpre * gwv
            do_s[:, cs] = rn * (don - on * jnp.mean(don * on, axis=-1, keepdims=True))

        ones = jnp.ones((REC_K, REC_K), BF16)
        sub = lax.broadcasted_iota(jnp.int32, (SUBLANES, REC_K), 0)
        rowid = lax.broadcasted_iota(jnp.int32, (CH, REC_K), 0)
        ngrp = CH // SUBLANES
        piece_row = lax.broadcasted_iota(jnp.int32, (1, sum(_ROWS_OF)), 1)
        key_of = jnp.zeros((1, sum(_ROWS_OF)), jnp.int32)
        for s in range(1, CH):
            key_of = jnp.where(piece_row >= _OFF_OF[s], s, key_of)
        pick = (key_of == lax.broadcasted_iota(jnp.int32, (CH, sum(_ROWS_OF)), 0)).astype(BF16)

        def operands(c, hp):
            rs, cs = _chunk_rows(c), slice(hp * REC_K, (hp + 1) * REC_K)
            return rs, cs, q_s[rs, cs], k_s[rs, cs], b_s[rs, cs], i_ref[rs, cs], do_s[rs, cs]

        def issue(c, hp, slot):
            _, _, q, k, bc, v, do = operands(c, hp)
            st, dst = sh_ref[hp, 0, c], dst_ref[hp]
            qe, kd = q * jnp.exp(bc), k * jnp.exp(bc[CH - 1:CH, :] - bc)
            do_bf, dst_bf = do.astype(BF16), dst.astype(BF16)
            car_r[slot, hp, 0:CH] = jnp.dot(do_bf, st.astype(BF16), preferred_element_type=F32)
            car_r[slot, hp, CH:2 * CH] = jnp.dot(v.astype(BF16), dst_bf, preferred_element_type=F32)
            car_r[slot, hp, 2 * CH:3 * CH] = lax.dot_general(kd.astype(BF16), dst_bf, NT_DIMS, preferred_element_type=F32)
            car_dst[slot, hp] = lax.dot_general(do_bf, qe.astype(BF16), TN_DIMS, preferred_element_type=F32)
            dec = jnp.concatenate([_pair_rows(bc, s) for s in range(CH)], axis=0)
            qk = jnp.concatenate([q[_ROW0[s]:_ROW1[s], :] * k[s:s + 1, :] for s in range(CH)], axis=0)
            x = jnp.concatenate([do[_ROW0[s]:_ROW1[s], :] * v[s:s + 1, :] for s in range(CH)], axis=0)
            car_dec[slot, hp] = dec
            car_a[slot, hp] = jnp.dot((qk * dec).astype(BF16), ones, preferred_element_type=F32)
            car_da[slot, hp] = jnp.dot(x.astype(BF16), ones, preferred_element_type=F32)
            qe1, _, ke0, _ = _cross_half(q, k, bc)
            qe1_bf, ke0_bf = qe1.astype(BF16), ke0.astype(BF16)
            do1_bf, v0_bf = do[HALF:, :].astype(BF16), v[:HALF, :].astype(BF16)
            car_x[slot, hp, 0:HALF] = lax.dot_general(ke0_bf, qe1_bf, NT_DIMS, preferred_element_type=F32)
            car_x[slot, hp, HALF:2 * HALF] = lax.dot_general(do1_bf, v0_bf, NT_DIMS, preferred_element_type=F32)
            car_x[slot, hp, 2 * HALF:3 * HALF] = lax.dot_general(v0_bf, do1_bf, NT_DIMS, preferred_element_type=F32)

        def advance_state(c, hp, slot):
            ebl = jnp.exp(b_s[_chunk_rows(c, CH - SUBLANES, SUBLANES), hp * REC_K:(hp + 1) * REC_K][SUBLANES - 1:, :])
            st, dst = sh_ref[hp, 0, c], dst_ref[hp]
            dst_ref[hp] = dst * ebl + car_dst[slot, hp]
            return ebl * jnp.sum(st * dst, axis=0, keepdims=True)

        def cross(c, hp, slot):
            _, _, q, k, bc, v, do = operands(c, hp)
            qe1, _, ke0, _ = _cross_half(q, k, bc)
            xs = car_x[slot, hp]
            dqe1 = jnp.dot(xs[HALF:2 * HALF].astype(BF16), ke0.astype(BF16), preferred_element_type=F32)
            dke0 = jnp.dot(xs[2 * HALF:].astype(BF16), qe1.astype(BF16), preferred_element_type=F32)
            dv1 = jnp.dot(xs[:HALF].astype(BF16), do[HALF:, :].astype(BF16), preferred_element_type=F32)
            return dqe1, dke0, dv1

        def retire(c, slot):
            dbl_state = [advance_state(c, hp, slot) for hp in range(HP)]
            yield
            crossed = [cross(c, hp, slot) for hp in range(HP)]
            for hp in range(HP):
                finish(c, hp, slot, dbl_state[hp], *crossed[hp])

        def step(c, slot):
            closing = retire(c + 1, slot)
            next(closing)
            for hp in range(HP):
                issue(c, hp, 1 - slot)
            next(closing, None)

        def trip(j, carry):
            step(nck - 2 - 2 * j, 0)
            step(nck - 3 - 2 * j, 1)
            return carry

        def finish(c, hp, slot, dbl_state, dqe1, dke0, dv1):
            rs, cs, q, k, bc, v, do = operands(c, hp)
            eb, ekd = jnp.exp(bc), jnp.exp(bc[CH - 1:CH, :] - bc)
            qe, kd = q * eb, k * ekd
            qe1, e1, ke0, e0 = _cross_half(q, k, bc)
            dqe, dkd, dv = car_r[slot, hp, 0:CH], car_r[slot, hp, CH:2 * CH], car_r[slot, hp, 2 * CH:3 * CH]
            a, da, decs = car_a[slot, hp], car_da[slot, hp], car_dec[slot, hp]
            dec = [decs[_OFF_OF[s]:_OFF_OF[s] + _ROWS_OF[s], :] for s in range(CH)]
            dbl = jnp.sum(dkd * kd, axis=0, keepdims=True) + dbl_state
            dq_acc = [jnp.zeros((SUBLANES, REC_K), F32) for _ in range(ngrp)]
            uk, uv = [], []
            for s in range(CH):
                j = s // SUBLANES
                r0 = j * SUBLANES
                ks = k[s:s + 1, :]
                for jj in range(j, _ROW1[s] // SUBLANES):
                    lo, hi = _OFF_OF[s] + (jj - j) * SUBLANES, _OFF_OF[s] + (jj - j + 1) * SUBLANES
                    a_blk, da_blk = a[lo:hi, :], da[lo:hi, :]
                    if jj == j:
                        keep = sub >= s - r0
                        a_blk, da_blk = jnp.where(keep, a_blk, 0.0), jnp.where(keep, da_blk, 0.0)
                    rows = slice(jj * SUBLANES, (jj + 1) * SUBLANES)
                    tt = da_blk * dec[s][(jj - j) * SUBLANES:(jj - j + 1) * SUBLANES, :]
                    dq_acc[jj] = dq_acc[jj] + tt * ks
                    uk.append(tt * q[rows, :])
                    uv.append(a_blk * do[rows, :])
            dk_in = jnp.dot(pick, jnp.concatenate(uk, axis=0).astype(BF16), preferred_element_type=F32)
            dv_in = jnp.dot(pick, jnp.concatenate(uv, axis=0).astype(BF16), preferred_element_type=F32)
            zero_half = jnp.zeros((HALF, REC_K), F32)
            dq_x = jnp.concatenate([zero_half, dqe1 * e1], axis=0)
            dk_x = jnp.concatenate([dke0 * e0, zero_half], axis=0)
            dv_x = jnp.concatenate([dv1, zero_half], axis=0)
            db_x = jnp.concatenate([-(dke0 * ke0), dqe1 * qe1], axis=0)
            dq_in = jnp.concatenate(dq_acc, axis=0)
            dqv_s[rs, cs] = dqe * eb + dq_in + dq_x
            dk_s[rs, cs] = dkd * ekd + dk_in + dk_x
            di_ref[rs, cs] = (dv + dv_in + dv_x).astype(BF16)
            db = dqe * qe - dkd * kd + q * dq_in - k * dk_in + db_x
            db_s[rs, cs] = db + jnp.where(rowid == CH - 1, dbl, 0.0)

        assert nck % 2 == 0
        for hp in range(HP):
            issue(nck - 1, hp, 0)
        lax.fori_loop(0, nck // 2 - 1, trip, 0)
        step(0, 0)
        for _ in retire(0, 1):
            pass
        up = _tri(False)
        sgq = _sigmoid(qp)
        dq_ref[...] = (dqv_s[...] * (sgq * (1.0 + qp * (1.0 - sgq)))).astype(BF16)
        dlb_acc = jnp.zeros((1, _HW), F32)
        for r in range(0, tb, _CUM_ROWS):
            rows = slice(r, r + _CUM_ROWS)
            dgl = _dot3(up, db_s[rows, :])
            dfg = dgl * jnp.exp(-g[rows, :]) - dk_s[rows, :]
            sn = sig_neg[rows, :]
            df_ref[rows, :] = (dfg * (1.0 - lbv) * (1.0 - sn) * sn).astype(BF16)
            dlb_acc = dlb_acc + jnp.sum(dfg * sn, axis=0, keepdims=True)
        dlb_ref[...] += dlb_acc

    blk, head, lbs, gws, hist = _hgrn_specs(tb, nt, True)
    out_specs = [pl.BlockSpec((tb, REC_IN), lambda h, b, t: (b * nt + nt - 1 - t, 0)), lbs,
                 pl.BlockSpec((HP, 1, REC_K), lambda h, b, t: (h, 0, 0))]
    out_shape = [jax.ShapeDtypeStruct((n, REC_IN), BF16),
                 jax.ShapeDtypeStruct((1, REC_WIDTH), F32), jax.ShapeDtypeStruct((REC_HEADS, 1, REC_K), F32)]
    return pl.pallas_call(
        body, name="hgrn_bwd", grid=(REC_HEADS // HP, b_loc, nt),
        in_specs=[blk(0), blk(1), blk(2), blk(3), lbs, gws, head, head, hist],
        out_specs=out_specs, out_shape=out_shape,
        scratch_shapes=[pltpu.VMEM((tb, _HW), F32)] * 7 + [pltpu.VMEM((HP, REC_K, REC_K), F32)] + [
            pltpu.VMEM((2, HP, 3 * CH, REC_K), F32), pltpu.VMEM((2, HP, REC_K, REC_K), F32)] + [
            pltpu.VMEM((2, HP, sum(_ROWS_OF), REC_K), F32)] * 3 + [pltpu.VMEM((2, HP, 3 * HALF, HALF), F32)],
        compiler_params=_params(("arbitrary", "arbitrary", "arbitrary"), 56),
    )(p, p, p, p, lb, gw, oraw, dg, sh)


def _postnorm_bwd_nt(dxo, y, qw, w, name):
    n = dxo.shape[0]

    def body(dx_ref, y_ref, qw_ref, w_ref, dg_ref, dy_ref, dqw_ref, db_ref):
        @pl.when(pl.program_id(0) == 0)
        def _():
            dqw_ref[...] = jnp.zeros_like(dqw_ref)
            db_ref[...] = jnp.zeros_like(db_ref)

        yv, dxv = y_ref[...], dx_ref[...]
        r = lax.rsqrt(jnp.mean(yv * yv, axis=-1, keepdims=True) + NORM_EPS)
        u = yv * r
        du = dxv * qw_ref[...]
        dy = r * (du - u * jnp.mean(du * u, axis=-1, keepdims=True))
        dqw_ref[...] += jnp.sum(dxv * u, axis=0, keepdims=True)
        db_ref[...] += jnp.sum(dy, axis=0, keepdims=True)
        dyb = dy.astype(BF16)
        dy_ref[...] = dyb
        dg_ref[...] = lax.dot_general(dyb, w_ref[...], NT_DIMS, preferred_element_type=F32)

    tm = _wide_rows(n)
    rows = pl.BlockSpec((tm, D_MODEL), lambda i: (i, 0))
    const = lambda shape: pl.BlockSpec(shape, lambda i: (0, 0))
    return pl.pallas_call(
        body, name=name, grid=(n // tm,), in_specs=[rows, rows, const((1, D_MODEL)), const((D_MODEL, D_MODEL))],
        out_specs=[rows, rows, const((1, D_MODEL)), const((1, D_MODEL))],
        out_shape=[jax.ShapeDtypeStruct((n, D_MODEL), F32), jax.ShapeDtypeStruct((n, D_MODEL), BF16),
                   jax.ShapeDtypeStruct((1, D_MODEL), F32), jax.ShapeDtypeStruct((1, D_MODEL), F32)],
        compiler_params=_params(("arbitrary",), 48),
    )(dxo, y, qw, w)


def _nt_prenorm_bwd(dps, w, x, pw, dxo, has_bias, name, parts=()):
    n = x.shape[0]
    widths = [d.shape[1] for d in dps]
    m = sum(widths)
    npieces, nparts, steps = len(dps), len(parts), n // TM

    def body(*refs):
        dp_refs = refs[:npieces]
        w_ref, x_ref, pw_ref, dxo_ref = refs[npieces:npieces + 4]
        part_refs = refs[npieces + 4:npieces + 4 + nparts]
        dx_ref, dpw_ref, db_ref = refs[npieces + 4 + nparts:npieces + 7 + nparts]
        land_refs = refs[npieces + 7 + nparts:npieces + 7 + 2 * nparts]
        sems = refs[npieces + 7 + 2 * nparts:]

        @pl.when(pl.program_id(0) == 0)
        def _():
            dpw_ref[...] = jnp.zeros_like(dpw_ref)
            db_ref[...] = jnp.zeros_like(db_ref)
            if nparts:
                _scatter_start(part_refs, land_refs, sems)

        dh = jnp.zeros((TM, D_MODEL), F32)
        off = 0
        for dp_ref, wd in zip(dp_refs, widths):
            cn = _col_chunk(wd)
            for j in range(0, wd, cn):
                dpc = dp_ref[:, j:j + cn]
                if has_bias:
                    db_ref[:, off + j:off + j + cn] += jnp.sum(dpc, axis=0, keepdims=True)
                dh = dh + lax.dot_general(dpc.astype(BF16), w_ref[:, off + j:off + j + cn], NT_DIMS, preferred_element_type=F32)
            off += wd
        xv = x_ref[...]
        r = lax.rsqrt(jnp.mean(xv * xv, axis=-1, keepdims=True) + NORM_EPS)
        xn = xv * r
        dpw_ref[...] += jnp.sum(dh * xn, axis=0, keepdims=True)
        dxn = dh * pw_ref[...]
        dx_ref[...] = dxo_ref[...] + r * (dxn - xn * jnp.mean(dxn * xn, axis=-1, keepdims=True))

        if nparts:
            @pl.when(pl.program_id(0) == steps - 1)
            def _():
                _scatter_wait(part_refs, land_refs, sems)

    rows = pl.BlockSpec((TM, D_MODEL), lambda i: (i, 0))
    const = lambda shape: pl.BlockSpec(shape, lambda i: (0, 0))
    hbm = pl.BlockSpec(memory_space=pl.ANY)
    in_specs = ([pl.BlockSpec((TM, wd), lambda i: (i, 0)) for wd in widths] + [const((D_MODEL, m)), rows, const((1, D_MODEL)), rows]
                + [hbm] * nparts)
    return pl.pallas_call(
        body, name=name, grid=(steps,), in_specs=in_specs,
        out_specs=[rows, const((1, D_MODEL)), const((1, m))] + [hbm] * nparts,
        out_shape=[jax.ShapeDtypeStruct((n, D_MODEL), F32), jax.ShapeDtypeStruct((1, D_MODEL), F32),
                   jax.ShapeDtypeStruct((1, m), F32)] + _scatter_lands(parts),
        scratch_shapes=_scatter_sems(nparts) if nparts else [],
        compiler_params=_params(("arbitrary",), 56),
    )(*dps, w, x, pw, dxo, *parts)


def _matmul_tn(a, b, name):
    n, k = a.shape
    m = b.shape[1]
    tk, tm, tn = k, _col_chunk(m), TN_ROWS if n % TN_ROWS == 0 else n

    def body(a_ref, b_ref, o_ref):
        @pl.when(pl.program_id(2) == 0)
        def _():
            o_ref[...] = jnp.zeros_like(o_ref)

        o_ref[...] += lax.dot_general(a_ref[...], b_ref[...].astype(BF16), TN_DIMS, preferred_element_type=F32)

    return pl.pallas_call(
        body, name=name, grid=(k // tk, m // tm, n // tn),
        in_specs=[pl.BlockSpec((tn, tk), lambda i, j, l: (l, i)), pl.BlockSpec((tn, tm), lambda i, j, l: (l, j))],
        out_specs=pl.BlockSpec((tk, tm), lambda i, j, l: (i, j)),
        out_shape=jax.ShapeDtypeStruct((k, m), F32),
        compiler_params=_params(("arbitrary", "arbitrary", "arbitrary"), 48),
    )(a, b)


def _matmul_tn_by_owner(a, b, name):
    n, k = a.shape
    c = b.shape[1] // N_DEV
    tn = TN_ROWS if n % TN_ROWS == 0 else n
    steps = n // tn
    per = 2

    def body(a_ref, b_ref, o_ref, w_ref):
        @pl.when(pl.program_id(1) == 0)
        def _():
            o_ref[...] = jnp.zeros_like(o_ref)

        r = lax.dot_general(a_ref[...], b_ref[...], TN_DIMS, preferred_element_type=F32)
        for j in range(per):
            o_ref[j] += r[:, j * c:(j + 1) * c]

        @pl.when(pl.program_id(1) == steps - 1)
        def _():
            w_ref[...] = o_ref[...].astype(BF16)

    out = pl.BlockSpec((per, k, c), lambda j, l: (j, 0, 0))
    return pl.pallas_call(
        body, name=name, grid=(N_DEV // per, steps),
        in_specs=[pl.BlockSpec((tn, k), lambda j, l: (l, 0)), pl.BlockSpec((tn, per * c), lambda j, l: (l, j))],
        out_specs=[out, out],
        out_shape=[jax.ShapeDtypeStruct((N_DEV, k, c), F32), jax.ShapeDtypeStruct((N_DEV, k, c), BF16)],
        compiler_params=_params(("arbitrary", "arbitrary"), 48),
    )(a, b)


def _by_owner_cols(dw):
    k, m = dw.shape
    return dw.reshape(k, N_DEV, m // N_DEV).transpose(1, 0, 2)


def _own_and_bf16(part):
    return lax.dynamic_index_in_dim(part, _my_id(), 0, keepdims=False), part.astype(BF16)


def _step(x, ct, st, tgt, pre_w, post_w, wa_in, ba_in, sinks, wa_out_shard, ba_out, wr_in_shard, lb_logits, gnorm_w, wr_out_shard, b_loc, t_len):
    nb = t_len // BLK
    lb = _lower_bound(lb_logits)
    p0, h0, ga_out = _norm_matmul(x, pre_w[0:1], wa_in, ba_in, "attn_in_proj", [wa_out_shard])
    wa_out = ga_out.reshape(ATTN_WIDTH, D_MODEL)
    o0, g0, gr_in = _attn_fwd(p0, ct, st, sinks, b_loc, nb, [wr_in_shard])
    wr_in = gr_in.transpose(1, 0, 2).reshape(D_MODEL, REC_IN)
    y0, x1 = _outproj_postnorm(g0, wa_out, ba_out, x, post_w[0:1], "attn_out_proj")
    p1, h1, gr_out = _norm_matmul(x1, pre_w[1:2], wr_in, None, "rec_in_proj", [wr_out_shard])
    wr_out = gr_out.reshape(REC_WIDTH, D_MODEL)
    o1, g1, sh = _hgrn_fwd(p1, lb, gnorm_w, b_loc, t_len)
    dx2, dg1, dy1, dpost1, loss_tile = _outproj_loss_bwd(g1, wr_out, x1, post_w[1:2], tgt, "rec_out_proj_loss_bwd")
    d_wr_out = _matmul_tn(g1, dy1, "rec_w_out_grad")
    dp1, dlb, dgw = _hgrn_bwd(p1, lb, gnorm_w, o1, sh, dg1, b_loc, t_len)
    dx1, dpre1, _ = _nt_prenorm_bwd([dp1], wr_in, x1, pre_w[1:2], dx2, False, "rec_in_bwd")
    part_r_in, wire_r_in = _matmul_tn_by_owner(h1, dp1, "rec_w_in_grad")
    own_r_in = lax.dynamic_index_in_dim(part_r_in, _my_id(), 0, keepdims=False)
    dg0, dy0, dpost0, dba_out = _postnorm_bwd_nt(dx1, y0, post_w[0:1], wa_out, "attn_out_bwd")
    d_wa_out = _matmul_tn(g0, dy0, "attn_w_out_grad")
    owns, wires = zip(*[_own_and_bf16(part) for part in (
        d_wr_out.reshape(N_DEV, REC_WIDTH // N_DEV, D_MODEL), d_wa_out.reshape(N_DEV, ATTN_WIDTH // N_DEV, D_MODEL))])
    owns, wires = (own_r_in,) + owns, (wire_r_in,) + wires
    dp0, dsink_tile, *lands = _attn_bwd(p0, ct, st, sinks, o0, dg0, b_loc, nb, list(wires))
    d_wa_in = _matmul_tn(h0, dp0, "attn_w_in_grad")
    own_a_in, wire_a_in = _own_and_bf16(_by_owner_cols(_qkvz(d_wa_in)))
    dx0, dpre0, dba_in, land_a_in = _nt_prenorm_bwd([dp0], wa_in, x, pre_w[0:1], dx1, True, "attn_in_bwd", [wire_a_in])
    small = dict(pre=jnp.concatenate([dpre0, dpre1], axis=0), post=jnp.concatenate([dpost0, dpost1], axis=0),
                 ba_in=dba_in, sinks=dsink_tile[0:1, 0:N_HEADS], ba_out=dba_out, lb=dlb, gnorm=jnp.sum(dgw, axis=0))
    return loss_tile, dx0, list(zip(lands, owns)) + [(land_a_in, own_a_in)], small


def _my_id():
    return lax.axis_index("x") * 4 + lax.axis_index("y") * 2 + lax.axis_index("c")


def _peer(k):
    x, y, c = lax.axis_index("x"), lax.axis_index("y"), lax.axis_index("c")
    return (x ^ ((k >> 2) & 1), y ^ ((k >> 1) & 1), c ^ (k & 1))


def _peer_id(k):
    return _my_id() ^ k


def _all_gather_by_chip(shard, pos_col):
    n = pos_col.shape[0]

    def body(x_ref, p_ref, f_ref, s_ref, out_ref, ct_ref, st_ref, send_sems, recv_sems, local_sem):
        x, y, c = lax.axis_index("x"), lax.axis_index("y"), lax.axis_index("c")
        me, sibling = (x, y, c), (x, y, 1 - c)
        chips = [(1 - x, y), (x, 1 - y), (1 - x, 1 - y)]

        def rows(px, py, pc):
            return out_ref.at[4 * px + 2 * py + pc]

        def copy(k, block, to, src=None):
            return pltpu.make_async_remote_copy(src_ref=rows(*block) if src is None else src, dst_ref=rows(*block),
                                                send_sem=send_sems.at[k], recv_sem=recv_sems.at[k], device_id=to, device_id_type=MESH)

        mine = pltpu.make_async_copy(x_ref, rows(*me), local_sem)
        mine.start()
        first = [copy(0, me, sibling, src=x_ref)] + [copy(1 + j, me, (*chip, c), src=x_ref) for j, chip in enumerate(chips)]
        for cp in first:
            cp.start()
        _rope_tables_into(p_ref, f_ref, s_ref, ct_ref, st_ref)
        passed = [copy(4 + j, (*chip, c), sibling) for j, chip in enumerate(chips)]
        for j, chip in enumerate(chips):
            copy(1 + j, (*chip, c), me).wait_recv()
            passed[j].start()
        copy(0, sibling, me).wait_recv()
        for j, chip in enumerate(chips):
            copy(4 + j, (*chip, 1 - c), me).wait_recv()
        for cp in first + passed:
            cp.wait_send()
        mine.wait()

    hbm, vmem = pl.BlockSpec(memory_space=pl.ANY), pl.BlockSpec(memory_space=pltpu.VMEM)
    return pl.pallas_call(
        body, name="comm_all_gather_by_chip", in_specs=[hbm, vmem, vmem, vmem], out_specs=[hbm, vmem, vmem],
        out_shape=[jax.ShapeDtypeStruct((N_DEV,) + shard.shape, shard.dtype)] + [jax.ShapeDtypeStruct((n, LANES), F32)] * 2,
        scratch_shapes=[pltpu.SemaphoreType.DMA((N_DEV - 1,)), pltpu.SemaphoreType.DMA((N_DEV - 1,)), pltpu.SemaphoreType.DMA],
        compiler_params=_params(None, 32),
    )(shard, pos_col, *_rope_lanes())


def _gather_shapes(shards):
    return [jax.ShapeDtypeStruct((N_DEV,) + s.shape, s.dtype) for s in shards]


def _gather_sems(nsh):
    return [pltpu.SemaphoreType.DMA((nsh, N_DEV - 1)), pltpu.SemaphoreType.DMA((nsh, N_DEV - 1)), pltpu.SemaphoreType.DMA((nsh,))]


def _gather_copies(ins, outs, sems, received):
    send_sems, recv_sems, local_sems = sems
    me = _my_id()
    local = [pltpu.make_async_copy(ins[a], outs[a].at[me], local_sems.at[a]) for a in range(len(ins))]
    remote = [pltpu.make_async_remote_copy(
        src_ref=ins[a], dst_ref=outs[a].at[_peer_id(k) if received else me], send_sem=send_sems.at[a, k - 1],
        recv_sem=recv_sems.at[a, k - 1], device_id=_peer(k), device_id_type=MESH)
        for a in range(len(ins)) for k in range(1, N_DEV)]
    return local, remote


def _gather_start(ins, outs, sems):
    local, sends = _gather_copies(ins, outs, sems, False)
    for cp in local + sends:
        cp.start()


def _gather_wait(ins, outs, sems):
    local, recvs = _gather_copies(ins, outs, sems, True)
    for cp in recvs:
        cp.wait_recv()
    for cp in recvs:
        cp.wait_send()
    for cp in local:
        cp.wait()


def _scatter_lands(parts):
    return [jax.ShapeDtypeStruct((N_DEV - 1,) + p.shape[1:], p.dtype) for p in parts]


def _scatter_sems(nparts):
    return [pltpu.SemaphoreType.DMA((nparts, N_DEV - 1)), pltpu.SemaphoreType.DMA((nparts, N_DEV - 1))]


def _scatter_copies(parts, lands, sems):
    send_sems, recv_sems = sems
    return [pltpu.make_async_remote_copy(
        src_ref=parts[a].at[_peer_id(k)], dst_ref=lands[a].at[k - 1], send_sem=send_sems.at[a, k - 1],
        recv_sem=recv_sems.at[a, k - 1], device_id=_peer(k), device_id_type=MESH)
        for a in range(len(parts)) for k in range(1, N_DEV)]


def _scatter_start(parts, lands, sems):
    for cp in _scatter_copies(parts, lands, sems):
        cp.start()


def _scatter_wait(parts, lands, sems):
    copies = _scatter_copies(parts, lands, sems)
    for cp in copies:
        cp.wait_recv()
    for cp in copies:
        cp.wait_send()


def _adamw(w, g, m, v):
    m2 = ADAM_B1 * m + (1.0 - ADAM_B1) * g
    v2 = ADAM_B2 * v + (1.0 - ADAM_B2) * (g * g)
    m_hat = m2 / (1.0 - ADAM_B1 ** ADAM_STEP)
    v_hat = v2 / (1.0 - ADAM_B2 ** ADAM_STEP)
    delta = -ADAM_LR * (m_hat / (jnp.sqrt(v_hat) + ADAM_EPS) + ADAM_WD * w)
    return delta, m2, v2


def _sum_adamw(land, own, w, m, v, name):
    r, c = own.shape
    rb = min(r, 256)

    def body(land_ref, own_ref, w_ref, m_ref, v_ref, g_ref, d_ref, m2_ref, v2_ref):
        me = _my_id()
        g = jnp.zeros((rb, c), F32)
        for dev in range(N_DEV):
            k = dev ^ me
            g = g + jnp.where(k == 0, own_ref[...], land_ref[jnp.maximum(k - 1, 0)].astype(F32))
        delta, m2, v2 = _adamw(w_ref[...], g, m_ref[...], v_ref[...])
        g_ref[...] = g
        d_ref[...] = delta
        m2_ref[...] = m2
        v2_ref[...] = v2

    rows = pl.BlockSpec((rb, c), lambda i: (i, 0))
    return pl.pallas_call(
        body, name=name, grid=(r // rb,), in_specs=[pl.BlockSpec((N_DEV - 1, rb, c), lambda i: (0, i, 0))] + [rows] * 4,
        out_specs=[rows] * 4, out_shape=[jax.ShapeDtypeStruct((r, c), F32)] * 4,
        compiler_params=_params(("arbitrary",), 32),
    )(land, own, w, m, v)


_SMALL = [("pre_norm_w", 2048), ("post_norm_w", 2048), ("attn_b_in", 2304), ("attn_sinks", 16), ("attn_b_out", 1024),
          ("rec_lb_logits", 2048), ("rec_gnorm_w", 128), ("loss", 1)]
_TILE = SUBLANES * LANES


def _small_rows(size):
    return -(-size // _TILE) * SUBLANES


_SMALL_OFF = {}
_r = 0
for _name, _size in _SMALL:
    _SMALL_OFF[_name] = _r
    _r += _small_rows(_size)
_SMALL_ROWS = _r


def _pack_small(pieces):
    out = []
    for name, size in _SMALL:
        flat = pieces[name].reshape(-1).astype(F32)
        out.append(jnp.pad(flat, (0, _small_rows(size) * LANES - size)).reshape(-1, LANES))
    return jnp.concatenate(out, axis=0)


def _unpack_small(packed, shapes):
    return {name: packed[_SMALL_OFF[name]:_SMALL_OFF[name] + _small_rows(size)].reshape(-1)[:size].reshape(shapes[name])
            for name, size in _SMALL}


def _small_allreduce_adamw(gpart, w, m, v):
    lb0 = _SMALL_OFF["rec_lb_logits"]

    def body(gp_ref, w_ref, m_ref, v_ref, g_ref, d_ref, m2_ref, v2_ref, land_ref, send_sems, recv_sems):
        me = _my_id()
        sent = []
        for k in range(1, N_DEV):
            cp = pltpu.make_async_remote_copy(src_ref=gp_ref, dst_ref=land_ref.at[k - 1], send_sem=send_sems.at[k - 1],
                                              recv_sem=recv_sems.at[k - 1], device_id=_peer(k), device_id_type=MESH)
            cp.start()
            sent.append(cp)
        for cp in sent:
            cp.wait_recv()
        for cp in sent:
            cp.wait_send()
        g = jnp.zeros((_SMALL_ROWS, LANES), F32)
        for dev in range(N_DEV):
            k = dev ^ me
            g = g + jnp.where(k == 0, gp_ref[...], land_ref[jnp.maximum(k - 1, 0)])
        g_ref[...] = g
        l0, l1 = w_ref[lb0:lb0 + SUBLANES, :], w_ref[lb0 + SUBLANES:lb0 + 2 * SUBLANES, :]
        mx = jnp.maximum(l0, l1)
        e0, e1 = jnp.exp(l0 - mx), jnp.exp(l1 - mx)
        p1 = e1 / (e0 + e1)
        dl1 = (1.0 - p1) * p1 * g[lb0:lb0 + SUBLANES, :]
        g_ref[lb0:lb0 + SUBLANES, :] = -dl1
        g_ref[lb0 + SUBLANES:lb0 + 2 * SUBLANES, :] = dl1
        delta, m2, v2 = _adamw(w_ref[...], g_ref[...], m_ref[...], v_ref[...])
        d_ref[...] = delta
        m2_ref[...] = m2
        v2_ref[...] = v2

    vmem = pl.BlockSpec(memory_space=pltpu.VMEM)
    return pl.pallas_call(
        body, name="comm_small_allreduce_adamw", in_specs=[vmem] * 4, out_specs=[vmem] * 4,
        out_shape=[jax.ShapeDtypeStruct((_SMALL_ROWS, LANES), F32)] * 4,
        scratch_shapes=[pltpu.VMEM((N_DEV - 1, _SMALL_ROWS, LANES), F32), pltpu.SemaphoreType.DMA((N_DEV - 1,)),
                        pltpu.SemaphoreType.DMA((N_DEV - 1,))],
    )(gpart, w, m, v)


def _qzkv(a):
    kv_end = ATTN_WIDTH + 2 * KV_WIDTH
    return jnp.concatenate([a[..., :ATTN_WIDTH], a[..., kv_end:], a[..., ATTN_WIDTH:kv_end]], axis=-1)


def _qkvz(a):
    return jnp.concatenate([a[..., :ATTN_WIDTH], a[..., 2 * ATTN_WIDTH:], a[..., ATTN_WIDTH:2 * ATTN_WIDTH]], axis=-1)


def kernel(x, positions, pre_norm_w, post_norm_w, attn_w_in, attn_b_in, attn_sinks, attn_w_out, attn_b_out, rec_w_in, rec_lb_logits, rec_gnorm_w, rec_w_out, loss_target, m_pre_norm_w, m_post_norm_w, m_attn_w_in, m_attn_b_in, m_attn_sinks, m_attn_w_out, m_attn_b_out, m_rec_w_in, m_rec_lb_logits, m_rec_gnorm_w, m_rec_w_out, v_pre_norm_w, v_post_norm_w, v_attn_w_in, v_attn_b_in, v_attn_sinks, v_attn_w_out, v_attn_b_out, v_rec_w_in, v_rec_lb_logits, v_rec_gnorm_w, v_rec_w_out):
    b_loc, t_len, _ = x.shape
    n = b_loc * t_len
    ga_in, ct, st = _all_gather_by_chip(attn_w_in[0].astype(BF16), positions.reshape(n, 1).astype(F32))
    wa_in = _qzkv(ga_in.transpose(1, 0, 2).reshape(D_MODEL, ATTN_IN))

    loss_tile, dx, landed, small = _step(
        x.reshape(n, D_MODEL), ct, st, loss_target.reshape(n, D_MODEL),
        pre_norm_w, post_norm_w, wa_in, _qzkv(attn_b_in), attn_sinks, attn_w_out[0].astype(BF16), attn_b_out,
        rec_w_in[0].astype(BF16), rec_lb_logits, rec_gnorm_w, rec_w_out[0].astype(BF16), b_loc, t_len)

    lift = lambda outs: tuple(a[None] for a in outs)
    (l_r_in, o_r_in), (l_r_out, o_r_out), (l_a_out, o_a_out), (l_a_in, o_a_in) = landed
    r_a_in = lift(_sum_adamw(l_a_in, o_a_in, attn_w_in[0], m_attn_w_in[0], v_attn_w_in[0], "adamw_attn_w_in"))
    r_r_in = lift(_sum_adamw(l_r_in, o_r_in, rec_w_in[0], m_rec_w_in[0], v_rec_w_in[0], "adamw_rec_w_in"))
    r_r_out = lift(_sum_adamw(l_r_out, o_r_out, rec_w_out[0], m_rec_w_out[0], v_rec_w_out[0], "adamw_rec_w_out"))
    r_a_out = lift(_sum_adamw(l_a_out, o_a_out, attn_w_out[0], m_attn_w_out[0], v_attn_w_out[0], "adamw_attn_w_out"))

    gsmall = dict(pre_norm_w=small["pre"], post_norm_w=small["post"], attn_b_in=_qkvz(small["ba_in"]), attn_sinks=small["sinks"],
                  attn_b_out=small["ba_out"], rec_lb_logits=jnp.concatenate([small["lb"], jnp.zeros_like(small["lb"])], axis=0),
                  rec_gnorm_w=small["gnorm"], loss=loss_tile[0:1, 0:1])
    nil = jnp.zeros((1, 1), F32)
    wsmall = dict(pre_norm_w=pre_norm_w, post_norm_w=post_norm_w, attn_b_in=attn_b_in, attn_sinks=attn_sinks,
                  attn_b_out=attn_b_out, rec_lb_logits=rec_lb_logits, rec_gnorm_w=rec_gnorm_w, loss=nil)
    msmall = dict(pre_norm_w=m_pre_norm_w, post_norm_w=m_post_norm_w, attn_b_in=m_attn_b_in, attn_sinks=m_attn_sinks,
                  attn_b_out=m_attn_b_out, rec_lb_logits=m_rec_lb_logits, rec_gnorm_w=m_rec_gnorm_w, loss=nil)
    vsmall = dict(pre_norm_w=v_pre_norm_w, post_norm_w=v_post_norm_w, attn_b_in=v_attn_b_in, attn_sinks=v_attn_sinks,
                  attn_b_out=v_attn_b_out, rec_lb_logits=v_rec_lb_logits, rec_gnorm_w=v_rec_gnorm_w, loss=nil)
    shapes = {k: a.shape for k, a in wsmall.items()}
    packed = _small_allreduce_adamw(_pack_small(gsmall), _pack_small(wsmall), _pack_small(msmall), _pack_small(vsmall))
    sg, sd, sm, sv = [_unpack_small(a, shapes) for a in packed]

    big = {"attn_w_in": r_a_in, "attn_w_out": r_a_out, "rec_w_in": r_r_in, "rec_w_out": r_r_out}
    order = ["pre_norm_w", "post_norm_w", "attn_w_in", "attn_b_in", "attn_sinks", "attn_w_out", "attn_b_out", "rec_w_in",
             "rec_lb_logits", "rec_gnorm_w", "rec_w_out"]
    outs = [sg["loss"][0, 0], dx.reshape(b_loc, t_len, D_MODEL)]
    for idx, small_set in enumerate((sg, sd, sm, sv)):
        outs += [big[nm][idx] if nm in big else small_set[nm] for nm in order]
    return tuple(outs)
```

```python
import numpy as np
import jax
import jax.numpy as jnp
from jax import lax
from jax.experimental import pallas as pl
from jax.experimental.pallas import tpu as pltpu

F32, BF16 = jnp.float32, jnp.bfloat16
MESH = pl.DeviceIdType.MESH
N_DEV = 8

D_MODEL = 1024
N_HEADS, HEAD_DIM, N_KV, GROUP = 16, 64, 2, 8
ATTN_WIDTH, KV_WIDTH = 1024, 128
ATTN_IN = 2 * ATTN_WIDTH + 2 * KV_WIDTH
BLK = 128
ROPE_THETA, ROPE_HALF = 500000.0, 8
REC_HEADS, REC_K = 8, 128
REC_WIDTH = REC_HEADS * REC_K
REC_IN = 4 * REC_WIDTH
TN_ROWS = 2048
CH = 32
NORM_EPS = 1e-6
F32_TINY = 1.1754944e-38
ADAM_LR, ADAM_B1, ADAM_B2, ADAM_EPS, ADAM_WD, ADAM_STEP = 0.001, 0.9, 0.999, 1e-08, 0.01, 10

LANES, SUBLANES = 128, 8
TM = 512
NT_DIMS = (((1,), (1,)), ((), ()))
TN_DIMS = (((0,), (0,)), ((), ()))
MB = 2 ** 20


def _params(sem=None, vmem_mb=48, **kw):
    return pltpu.CompilerParams(dimension_semantics=sem, vmem_limit_bytes=vmem_mb * MB, **kw)


def _wide_rows(n):
    return 2 * TM if n % (2 * TM) == 0 else TM


def _col_chunk(m):
    return 768 if m % 1024 else 1024


def _sigmoid(x):
    return 1.0 / (1.0 + jnp.exp(-x))


def _split3(x):
    hi = x.astype(BF16)
    r1 = x - hi.astype(F32)
    mid = r1.astype(BF16)
    lo = (r1 - mid.astype(F32)).astype(BF16)
    return hi, mid, lo


def _dot3(l_bf, x):
    hi, mid, lo = _split3(x)
    return (jnp.dot(l_bf, hi, preferred_element_type=F32) + jnp.dot(l_bf, mid, preferred_element_type=F32)
            + jnp.dot(l_bf, lo, preferred_element_type=F32))


def _rope_lanes():
    lane = np.arange(LANES) % HEAD_DIM
    inv = np.float32(ROPE_THETA) ** (-(np.arange(ROPE_HALF, dtype=np.float32) * np.float32(2.0) / np.float32(2 * ROPE_HALF)))
    freq = np.where(lane < 2 * ROPE_HALF, inv[lane % ROPE_HALF], 0.0).astype(np.float32)[None, :]
    sign = np.where(lane < ROPE_HALF, -1.0, np.where(lane < 2 * ROPE_HALF, 1.0, 0.0)).astype(np.float32)[None, :]
    return jnp.asarray(freq), jnp.asarray(sign)


def _rope_tables_into(p_ref, f_ref, s_ref, c_out, s_out):
    def rows(i, carry):
        rs = pl.ds(pl.multiple_of(i * TM, TM), TM)
        ang = p_ref[rs, :] * f_ref[...]
        c_out[rs, :] = jnp.cos(ang)
        s_out[rs, :] = jnp.sin(ang) * s_ref[...]
        return carry

    lax.fori_loop(0, p_ref.shape[0] // TM, rows, 0)


def _rope_apply(xv, c, s, lm):
    partner = jnp.where(lm < ROPE_HALF, pltpu.roll(xv, LANES - ROPE_HALF, 1), pltpu.roll(xv, ROPE_HALF, 1))
    return xv * c + partner * s


def _rope_bwd(dy, c, s, lm):
    t = dy * s
    partner = jnp.where(lm < ROPE_HALF, pltpu.roll(t, LANES - ROPE_HALF, 1),
                        jnp.where(lm < 2 * ROPE_HALF, pltpu.roll(t, ROPE_HALF, 1), 0.0))
    return dy * c + partner


def _lower_bound(lb_logits):
    def body(l_ref, o_ref):
        l0, l1 = l_ref[0:1, :], l_ref[1:2, :]
        m = jnp.maximum(l0, l1)
        e0, e1 = jnp.exp(l0 - m), jnp.exp(l1 - m)
        o_ref[...] = e1 / (e0 + e1)

    return pl.pallas_call(body, name="lower_bound", out_shape=jax.ShapeDtypeStruct((1, lb_logits.shape[1]), F32))(lb_logits)


def _norm_matmul(x, pw, w, bias, name, shards=()):
    n, m = x.shape[0], w.shape[1]
    cn = _col_chunk(m)
    has_bias = bias is not None
    tm = _wide_rows(n) if m <= ATTN_IN else TM
    nsh, steps = len(shards), n // tm

    def body(*refs):
        refs = list(refs)
        x_ref, pw_ref, w_ref = refs[:3]
        b_ref = refs[3] if has_bias else None
        refs = refs[4 if has_bias else 3:]
        sh_in, (p_ref, h_ref), sh_out, sems = refs[:nsh], refs[nsh:nsh + 2], refs[nsh + 2:2 * nsh + 2], refs[2 * nsh + 2:]
        if nsh:
            @pl.when(pl.program_id(0) == 0)
            def _():
                _gather_start(sh_in, sh_out, sems)

        xv = x_ref[...]
        r = lax.rsqrt(jnp.mean(xv * xv, axis=-1, keepdims=True) + NORM_EPS)
        h = ((xv * r) * pw_ref[...]).astype(BF16)
        h_ref[...] = h
        for j in range(0, m, cn):
            acc = jnp.dot(h, w_ref[:, j:j + cn], preferred_element_type=F32)
            if has_bias:
                acc = acc + b_ref[:, j:j + cn]
            p_ref[:, j:j + cn] = acc

        if nsh:
            @pl.when(pl.program_id(0) == steps - 1)
            def _():
                _gather_wait(sh_in, sh_out, sems)

    rows = pl.BlockSpec((tm, D_MODEL), lambda i: (i, 0))
    const = lambda shape: pl.BlockSpec(shape, lambda i: (0, 0))
    hbm = pl.BlockSpec(memory_space=pl.ANY)
    in_specs = [rows, const((1, D_MODEL)), const((D_MODEL, m))] + ([const((1, m))] if has_bias else []) + [hbm] * nsh
    args = (x, pw, w) + ((bias,) if has_bias else ()) + tuple(shards)
    return pl.pallas_call(
        body, name=name, grid=(steps,), in_specs=in_specs,
        out_specs=[pl.BlockSpec((tm, m), lambda i: (i, 0)), rows] + [hbm] * nsh,
        out_shape=[jax.ShapeDtypeStruct((n, m), F32), jax.ShapeDtypeStruct((n, D_MODEL), BF16)] + _gather_shapes(shards),
        scratch_shapes=_gather_sems(nsh) if nsh else [],
        compiler_params=_params(("arbitrary",), 56),
    )(*args)


def _outproj_postnorm(g, w, bias, xres, qw, name):
    n = g.shape[0]

    def body(g_ref, w_ref, b_ref, x_ref, qw_ref, y_ref, o_ref):
        y = jnp.dot(g_ref[...], w_ref[...], preferred_element_type=F32) + b_ref[...]
        y_ref[...] = y
        r = lax.rsqrt(jnp.mean(y * y, axis=-1, keepdims=True) + NORM_EPS)
        o_ref[...] = x_ref[...] + (y * r) * qw_ref[...]

    tm = _wide_rows(n)
    rows = pl.BlockSpec((tm, D_MODEL), lambda i: (i, 0))
    const = lambda shape: pl.BlockSpec(shape, lambda i: (0, 0))
    return pl.pallas_call(
        body, name=name, grid=(n // tm,),
        in_specs=[rows, const((D_MODEL, D_MODEL)), const((1, D_MODEL)), rows, const((1, D_MODEL))],
        out_specs=[rows, rows], out_shape=[jax.ShapeDtypeStruct((n, D_MODEL), F32)] * 2,
        compiler_params=_params(("arbitrary",), 48),
    )(g, w, bias, xres, qw)


def _outproj_loss_bwd(g, w, xres, qw, tgt, name):
    n = g.shape[0]
    tm = _wide_rows(n)
    steps = n // tm

    def body(g_ref, w_ref, x_ref, qw_ref, t_ref, dx_ref, dg_ref, dy_ref, dqw_ref, loss_ref, acc_ref):
        i = pl.program_id(0)

        @pl.when(i == 0)
        def _():
            acc_ref[...] = jnp.zeros_like(acc_ref)
            dqw_ref[...] = jnp.zeros_like(dqw_ref)

        y = jnp.dot(g_ref[...], w_ref[...], preferred_element_type=F32)
        r = lax.rsqrt(jnp.mean(y * y, axis=-1, keepdims=True) + NORM_EPS)
        u = y * r
        e = (x_ref[...] + u * qw_ref[...]) - t_ref[...]
        dxn = e * (1.0 / D_MODEL)
        dx_ref[...] = dxn
        acc_ref[...] += jnp.sum(e * e, axis=0, keepdims=True)
        du = dxn * qw_ref[...]
        dy = (r * (du - u * jnp.mean(du * u, axis=-1, keepdims=True))).astype(BF16)
        dqw_ref[...] += jnp.sum(dxn * u, axis=0, keepdims=True)
        dy_ref[...] = dy
        dg_ref[...] = lax.dot_general(dy, w_ref[...], NT_DIMS, preferred_element_type=F32)

        @pl.when(i == steps - 1)
        def _():
            loss_ref[...] = jnp.full(loss_ref.shape, jnp.sum(acc_ref[...]) * (0.5 / D_MODEL), F32)

    rows = pl.BlockSpec((tm, D_MODEL), lambda i: (i, 0))
    const = lambda shape: pl.BlockSpec(shape, lambda i: (0, 0))
    return pl.pallas_call(
        body, name=name, grid=(steps,),
        in_specs=[rows, const((D_MODEL, D_MODEL)), rows, const((1, D_MODEL)), rows],
        out_specs=[rows, rows, rows, const((1, D_MODEL)), const((SUBLANES, LANES))],
        out_shape=[jax.ShapeDtypeStruct((n, D_MODEL), F32), jax.ShapeDtypeStruct((n, D_MODEL), F32),
                   jax.ShapeDtypeStruct((n, D_MODEL), BF16), jax.ShapeDtypeStruct((1, D_MODEL), F32),
                   jax.ShapeDtypeStruct((SUBLANES, LANES), F32)],
        scratch_shapes=[pltpu.VMEM((1, D_MODEL), F32)], compiler_params=_params(("arbitrary",), 56),
    )(g, w, xres, qw, tgt)


_QCOL, _ZCOL, _KCOL, _VCOL = 0, 1024, 2048, 2176


def _head_stack(chunks, heads, lt64):
    return jnp.concatenate([jnp.where(lt64 if n % 2 == 0 else ~lt64, chunks[n // 2], 0.0) for n in heads], axis=0)


def _dup_half(x, h, lt64):
    r = pltpu.roll(x, HEAD_DIM, 1)
    return jnp.where(lt64, x, r) if h == 0 else jnp.where(lt64, r, x)


def _pair_chunk(xt, c2):
    a, b = 2 * c2, 2 * c2 + 1
    return jnp.concatenate([xt[:HEAD_DIM, a * BLK:(a + 1) * BLK], xt[HEAD_DIM:, b * BLK:(b + 1) * BLK]], axis=0).T


def _attn_mask_t(i):
    key = lax.broadcasted_iota(jnp.int32, (2 * BLK, BLK), 0)
    qry = lax.broadcasted_iota(jnp.int32, (2 * BLK, BLK), 1)
    valid = (key > qry) & (key <= qry + BLK) & ((key >= BLK) | (i > 0))
    return jnp.tile(jnp.where(valid, 0.0, -1e30), (1, GROUP))


def _attn_probs_t(s, heads, sink_ref, mask):
    s = s + mask
    head = lax.broadcasted_iota(jnp.int32, (1, len(heads) * BLK), 1) >> 7
    sk = jnp.zeros((1, len(heads) * BLK), F32)
    for j, n in enumerate(heads):
        sk = jnp.where(head == j, sink_ref[0, n], sk)
    m = jnp.maximum(jnp.max(s, axis=0, keepdims=True), sk)
    p = jnp.exp(s - m)
    esk = jnp.exp(sk - m)
    inv = 1.0 / (jnp.sum(p, axis=0, keepdims=True) + esk)
    return p * inv, esk * inv


def _attn_fwd(p, ct, st, sinks, b_loc, nb, shards):
    n = p.shape[0]
    nsh = len(shards)

    def body(sink_ref, q_ref, z_ref, kc_ref, kp_ref, vc_ref, vp_ref, cc_ref, sc_ref, cp_ref, sp_ref, *rest):
        sh_in, (o_ref, g_ref), sh_out, sems = rest[:nsh], rest[nsh:nsh + 2], rest[nsh + 2:2 * nsh + 2], rest[2 * nsh + 2:]
        b, i = pl.program_id(0), pl.program_id(1)

        @pl.when((b == 0) & (i == 0))
        def _():
            _gather_start(sh_in, sh_out, sems)

        lane = lax.broadcasted_iota(jnp.int32, (BLK, LANES), 1)
        lm = lane & (HEAD_DIM - 1)
        cc, sc = cc_ref[...], sc_ref[...]
        kcat = jnp.concatenate([_rope_apply(kp_ref[...], cp_ref[...], sp_ref[...], lm),
                                _rope_apply(kc_ref[...], cc, sc, lm)], axis=0)
        vcat = jnp.concatenate([vp_ref[...], vc_ref[...]], axis=0)
        qr = [_rope_apply(q_ref[:, c * LANES:(c + 1) * LANES], cc, sc, lm) * (HEAD_DIM ** -0.5) for c in range(8)]
        valid = _attn_mask_t(i)
        lt64, lt64k = lane < HEAD_DIM, lax.broadcasted_iota(jnp.int32, (2 * BLK, LANES), 1) < HEAD_DIM
        def kv_head(h):
            heads = list(range(h * GROUP, (h + 1) * GROUP))
            kext, vext = _dup_half(kcat, h, lt64k).astype(BF16), _dup_half(vcat, h, lt64k).astype(BF16)
            qst = _head_stack(qr, heads, lt64).astype(BF16)
            s = lax.dot_general(kext, qst, NT_DIMS, preferred_element_type=F32)
            yield
            pn, _ = _attn_probs_t(s, heads, sink_ref, valid)
            ot = lax.dot_general(vext, pn.astype(BF16), TN_DIMS, preferred_element_type=F32)
            yield
            for c2 in range(GROUP // 2):
                oc = _pair_chunk(ot, c2)
                cols = slice((4 * h + c2) * LANES, (4 * h + c2 + 1) * LANES)
                zc = z_ref[:, cols]
                o_ref[:, cols] = oc
                g_ref[:, cols] = (oc * (zc * _sigmoid(zc))).astype(BF16)

        _in_stages([kv_head(h) for h in range(N_KV)])

        @pl.when((b == b_loc - 1) & (i == nb - 1))
        def _():
            _gather_wait(sh_in, sh_out, sems)

    cur = lambda b, i: b * nb + i
    prev = lambda b, i: b * nb + jnp.maximum(i - 1, 0)
    wide = lambda cb: pl.BlockSpec((BLK, ATTN_WIDTH), lambda b, i: (cur(b, i), cb))
    kv = lambda rowf, cb: pl.BlockSpec((BLK, LANES), lambda b, i: (rowf(b, i), cb))
    hbm = pl.BlockSpec(memory_space=pl.ANY)
    in_specs = [pl.BlockSpec(memory_space=pltpu.SMEM), wide(0), wide(1),
                kv(cur, _KCOL // LANES), kv(prev, _KCOL // LANES), kv(cur, _VCOL // LANES), kv(prev, _VCOL // LANES),
                kv(cur, 0), kv(cur, 0), kv(prev, 0), kv(prev, 0)] + [hbm] * nsh
    return pl.pallas_call(
        body, name="attn_fwd", grid=(b_loc, nb), in_specs=in_specs, out_specs=[wide(0), wide(0)] + [hbm] * nsh,
        out_shape=[jax.ShapeDtypeStruct((n, ATTN_WIDTH), F32), jax.ShapeDtypeStruct((n, ATTN_WIDTH), BF16)] + _gather_shapes(shards),
        scratch_shapes=_gather_sems(nsh), compiler_params=_params(("arbitrary", "arbitrary"), 48),
    )(sinks, p, p, p, p, p, p, ct, st, ct, st, *shards)


def _attn_bwd(p, ct, st, sinks, o, dg, b_loc, nb, parts):
    n = p.shape[0]
    nparts = len(parts)

    def body(sink_ref, q_ref, z_ref, kc_ref, kp_ref, vc_ref, vp_ref, cc_ref, sc_ref, cp_ref, sp_ref, o_ref, dg_ref, *rest):
        part_refs, (dp_ref, ds_ref), land_refs = rest[:nparts], rest[nparts:nparts + 2], rest[nparts + 2:2 * nparts + 2]
        dq_s, dz_s, dk_s, dv_s = rest[2 * nparts + 2:2 * nparts + 6]
        sems = rest[2 * nparts + 6:]
        b, i = pl.program_id(0), pl.program_id(1)

        @pl.when((b == 0) & (i == 0))
        def _():
            _scatter_start(part_refs, land_refs, sems)

        @pl.when((b == b_loc - 1) & (i == nb))
        def _():
            _scatter_wait(part_refs, land_refs, sems)

        lane = lax.broadcasted_iota(jnp.int32, (BLK, LANES), 1)
        lm = lane & (HEAD_DIM - 1)

        @pl.when((b == 0) & (i == 0))
        def _():
            ds_ref[...] = jnp.zeros_like(ds_ref)

        @pl.when(i < nb)
        def _compute():
            cc, sc = cc_ref[...], sc_ref[...]
            kcat = jnp.concatenate([_rope_apply(kp_ref[...], cp_ref[...], sp_ref[...], lm),
                                    _rope_apply(kc_ref[...], cc, sc, lm)], axis=0)
            vcat = jnp.concatenate([vp_ref[...], vc_ref[...]], axis=0)
            qr = [_rope_apply(q_ref[:, c * LANES:(c + 1) * LANES], cc, sc, lm) * (HEAD_DIM ** -0.5) for c in range(8)]
            valid = _attn_mask_t(i)
            lt64, lt64k = lane < HEAD_DIM, lax.broadcasted_iota(jnp.int32, (2 * BLK, LANES), 1) < HEAD_DIM
            do_chunks, doo_chunks, dz_chunks = [], [], []
            for c in range(8):
                cols = slice(c * LANES, (c + 1) * LANES)
                zc, oc, dgc = z_ref[:, cols], o_ref[:, cols], dg_ref[:, cols]
                sg = _sigmoid(zc)
                do_chunks.append(dgc * (zc * sg))
                dz_chunks.append(dgc * oc * (sg * (1.0 + zc * (1.0 - sg))))
                doo_chunks.append(do_chunks[c] * oc)
            dq_chunks = [None] * 8
            dk_h, dv_h, ds_parts = [None] * N_KV, [None] * N_KV, [None] * N_KV
            tile_lane = lax.broadcasted_iota(jnp.int32, (SUBLANES, LANES), 1)
            tile_row = lax.broadcasted_iota(jnp.int32, (SUBLANES, LANES), 0)
            ones8 = jnp.ones((SUBLANES, LANES), BF16)

            def kv_head(h):
                heads = list(range(h * GROUP, (h + 1) * GROUP))
                kext = _dup_half(kcat, h, lt64k)
                kext_bf, kext_t = kext.astype(BF16), kext.T.astype(BF16)
                vext = _dup_half(vcat, h, lt64k).astype(BF16)
                qst = _head_stack(qr, heads, lt64).astype(BF16)
                pn, psink = _attn_probs_t(lax.dot_general(kext_bf, qst, NT_DIMS, preferred_element_type=F32), heads, sink_ref, valid)
                do_bf = _head_stack(do_chunks, heads, lt64).astype(BF16)
                delta = sum(lax.dot_general(ones8, part, NT_DIMS, preferred_element_type=F32)
                            for part in _split3(_head_stack(doo_chunks, heads, lt64)))[0:1, :]
                dpt = lax.dot_general(vext, do_bf, NT_DIMS, preferred_element_type=F32)
                dst = (pn * (dpt - delta)).astype(BF16)
                sink_term = psink * delta
                ds_acc = jnp.zeros((SUBLANES, LANES), F32)
                for j, n in enumerate(heads):
                    val = -jnp.sum(sink_term[:, j * BLK:(j + 1) * BLK])
                    ds_acc = ds_acc + jnp.where((tile_lane == n) & (tile_row == 0), val, 0.0)
                ds_parts[h] = ds_acc
                dqt = jnp.dot(kext_t, dst, preferred_element_type=F32) * (HEAD_DIM ** -0.5)
                dk_ext = jnp.dot(dst, qst, preferred_element_type=F32)
                dv_ext = jnp.dot(pn.astype(BF16), do_bf, preferred_element_type=F32)
                dk_h[h] = dk_ext + pltpu.roll(dk_ext, HEAD_DIM, 1)
                dv_h[h] = dv_ext + pltpu.roll(dv_ext, HEAD_DIM, 1)
                for c2 in range(GROUP // 2):
                    dq_chunks[4 * h + c2] = _rope_bwd(_pair_chunk(dqt, c2), cc, sc, lm)

            for h in range(N_KV):
                kv_head(h)
            ds_ref[...] += ds_parts[0] + ds_parts[1]
            dk_full = jnp.where(lt64k, dk_h[0], dk_h[1])
            dv_full = jnp.where(lt64k, dv_h[0], dv_h[1])

            @pl.when(i >= 1)
            def _emit():
                dp_ref[:, _QCOL:_QCOL + ATTN_WIDTH] = dq_s[...]
                dp_ref[:, _ZCOL:_ZCOL + ATTN_WIDTH] = dz_s[...]
                dp_ref[:, _KCOL:_KCOL + KV_WIDTH] = _rope_bwd(dk_s[...] + dk_full[:BLK], cp_ref[...], sp_ref[...], lm)
                dp_ref[:, _VCOL:_VCOL + KV_WIDTH] = dv_s[...] + dv_full[:BLK]

            for c in range(8):
                dq_s[:, c * LANES:(c + 1) * LANES] = dq_chunks[c]
                dz_s[:, c * LANES:(c + 1) * LANES] = dz_chunks[c]
            dk_s[...] = dk_full[BLK:]
            dv_s[...] = dv_full[BLK:]

        @pl.when(i == nb)
        def _final():
            dp_ref[:, _QCOL:_QCOL + ATTN_WIDTH] = dq_s[...]
            dp_ref[:, _ZCOL:_ZCOL + ATTN_WIDTH] = dz_s[...]
            dp_ref[:, _KCOL:_KCOL + KV_WIDTH] = _rope_bwd(dk_s[...], cc_ref[...], sc_ref[...], lm)
            dp_ref[:, _VCOL:_VCOL + KV_WIDTH] = dv_s[...]

    cur = lambda b, i: b * nb + jnp.minimum(i, nb - 1)
    prev = lambda b, i: b * nb + jnp.maximum(jnp.minimum(i, nb - 1) - 1, 0)
    emit = lambda b, i: b * nb + jnp.maximum(i - 1, 0)
    hbm = pl.BlockSpec(memory_space=pl.ANY)
    wide = lambda cb: pl.BlockSpec((BLK, ATTN_WIDTH), lambda b, i: (cur(b, i), cb))
    kv = lambda rowf, cb: pl.BlockSpec((BLK, LANES), lambda b, i: (rowf(b, i), cb))
    in_specs = [pl.BlockSpec(memory_space=pltpu.SMEM), wide(0), wide(1),
                kv(cur, _KCOL // LANES), kv(prev, _KCOL // LANES), kv(cur, _VCOL // LANES), kv(prev, _VCOL // LANES),
                kv(cur, 0), kv(cur, 0), kv(prev, 0), kv(prev, 0), wide(0), wide(0)] + [hbm] * nparts
    out_specs = [pl.BlockSpec((BLK, ATTN_IN), lambda b, i: (emit(b, i), 0)),
                 pl.BlockSpec((SUBLANES, LANES), lambda b, i: (0, 0))] + [hbm] * nparts
    return pl.pallas_call(
        body, name="attn_bwd", grid=(b_loc, nb + 1), in_specs=in_specs, out_specs=out_specs,
        out_shape=[jax.ShapeDtypeStruct((n, ATTN_IN), F32), jax.ShapeDtypeStruct((SUBLANES, LANES), F32)] + _scatter_lands(parts),
        scratch_shapes=[pltpu.VMEM((BLK, ATTN_WIDTH), F32), pltpu.VMEM((BLK, ATTN_WIDTH), F32),
                        pltpu.VMEM((BLK, KV_WIDTH), F32), pltpu.VMEM((BLK, KV_WIDTH), F32)] + _scatter_sems(nparts),
        compiler_params=_params(("arbitrary", "arbitrary"), 48),
    )(sinks, p, p, p, p, p, p, ct, st, ct, st, o, dg, *parts)


_CUM_ROWS = 256
HALF = CH // 2
_ROW0 = [SUBLANES * (s // SUBLANES) for s in range(CH)]
_ROW1 = [HALF * (s // HALF + 1) for s in range(CH)]
_ROWS_OF = [_ROW1[s] - _ROW0[s] for s in range(CH)]
_OFF_OF = [sum(_ROWS_OF[:s]) for s in range(CH)]


def _tri(lower):
    r = lax.broadcasted_iota(jnp.int32, (_CUM_ROWS, _CUM_ROWS), 0)
    c = lax.broadcasted_iota(jnp.int32, (_CUM_ROWS, _CUM_ROWS), 1)
    same = (r ^ c) < CH
    return (same & ((c <= r) if lower else (c >= r))).astype(BF16)


def _gates(qp, fp, lb):
    e = jnp.exp(-jnp.abs(fp))
    r = 1.0 / (1.0 + e)
    sig_neg = jnp.where(fp >= 0, e, 1.0) * r
    sig = jnp.where(fp >= 0, 1.0, e) * r
    g = jnp.log(jnp.maximum(lb + (1.0 - lb) * sig, F32_TINY))
    return qp * _sigmoid(qp), g, (1.0 - lb) * sig_neg, sig_neg


def _pair_rows(bc, s):
    diff = bc[_ROW0[s]:_ROW1[s], :] - bc[s:s + 1, :]
    head = jnp.minimum(diff[:SUBLANES, :], 0.0)
    return jnp.exp(head if diff.shape[0] == SUBLANES else jnp.concatenate([head, diff[SUBLANES:, :]], axis=0))


def _cross_half(q, k, bc):
    r = bc[HALF - 1:HALF, :]
    e1, e0 = jnp.exp(bc[HALF:, :] - r), jnp.exp(r - bc[:HALF, :])
    return q[HALF:, :] * e1, e1, k[:HALF, :] * e0, e0


HP = 8
REC_TB = 256
_HW = HP * REC_K


def _hgrn_specs(tb, nt, reverse):
    tmap = (lambda t: nt - 1 - t) if reverse else (lambda t: t)
    groups = REC_HEADS // HP
    blk = lambda cb: pl.BlockSpec((tb, _HW), lambda h, b, t: (b * nt + tmap(t), cb * groups + h))
    head = pl.BlockSpec((tb, _HW), lambda h, b, t: (b * nt + tmap(t), h))
    lbs = pl.BlockSpec((1, _HW), lambda h, b, t: (0, h))
    gws = pl.BlockSpec((1, REC_K), lambda h, b, t: (0, 0))
    hist = pl.BlockSpec((HP, 1, tb // CH, REC_K, REC_K), lambda h, b, t: (h, b, tmap(t), 0, 0))
    return blk, head, lbs, gws, hist


def _chunk_rows(c, first=0, size=CH):
    start = c * CH + first
    return pl.ds(start if isinstance(start, int) else pl.multiple_of(start, CH if first % CH == 0 else SUBLANES), size)


def _in_stages(heads):
    live = list(heads)
    while live:
        live = [g for g in live if next(g, live) is not live]


def _cumsum_chunks(tri, x, out_ref, tb):
    for r in range(0, tb, _CUM_ROWS):
        out_ref[r:r + _CUM_ROWS, :] = _dot3(tri, x[r:r + _CUM_ROWS, :])


def _hgrn_fwd(p, lb, gw, b_loc, t_len):
    n = p.shape[0]
    tb = min(REC_TB, t_len)
    nt, nck = t_len // tb, tb // CH

    def body(qp_ref, fp_ref, i_ref, z_ref, lb_ref, gw_ref, oraw_ref, g_ref, sh_ref, q_s, k_s, b_s, o_s, st_ref,
             car_o, car_a, car_s, car_st):
        @pl.when(pl.program_id(2) == 0)
        def _():
            st_ref[...] = jnp.zeros_like(st_ref)

        qv, g, kk, _ = _gates(qp_ref[...], fp_ref[...], lb_ref[...])
        q_s[...] = qv
        k_s[...] = kk
        _cumsum_chunks(_tri(True), g, b_s, tb)
        ones = jnp.ones((REC_K, REC_K), BF16)
        sub = lax.broadcasted_iota(jnp.int32, (SUBLANES, REC_K), 0)

        rows_of = _chunk_rows

        def issue(c, hp):
            rs, cs = rows_of(c), slice(hp * REC_K, (hp + 1) * REC_K)
            q, k, bc, v = q_s[rs, cs], k_s[rs, cs], b_s[rs, cs], i_ref[rs, cs]
            st = st_ref[hp]
            sh_ref[hp, 0, c] = st
            o = lax.dot_general((q * jnp.exp(bc)).astype(BF16), st.astype(BF16), NT_DIMS, preferred_element_type=F32)
            w = jnp.concatenate([q[_ROW0[s]:_ROW1[s], :] * _pair_rows(bc, s) * k[s:s + 1, :] for s in range(CH)], axis=0)
            a = jnp.dot(w.astype(BF16), ones, preferred_element_type=F32)
            qe1, _, ke0, _ = _cross_half(q, k, bc)
            s10 = lax.dot_general(qe1.astype(BF16), ke0.astype(BF16), NT_DIMS, preferred_element_type=F32)
            kd = k * jnp.exp(bc[CH - 1:CH, :] - bc)
            st_new = lax.dot_general(v.astype(BF16), kd.astype(BF16), TN_DIMS, preferred_element_type=F32)
            return o, a, s10, st_new

        def advance_state(c, hp, st_new):
            bl = b_s[_chunk_rows(c, CH - SUBLANES, SUBLANES), hp * REC_K:(hp + 1) * REC_K][SUBLANES - 1:, :]
            st_ref[hp] = st_ref[hp] * jnp.exp(bl) + st_new

        def cross(c, hp, s10):
            v0 = i_ref[_chunk_rows(c, 0, HALF), hp * REC_K:(hp + 1) * REC_K]
            return jnp.dot(s10.astype(BF16), v0.astype(BF16), preferred_element_type=F32)

        def finish(c, hp, o, a, o_cross):
            rs, cs = rows_of(c), slice(hp * REC_K, (hp + 1) * REC_K)
            v = i_ref[rs, cs]
            acc = [jnp.zeros((SUBLANES, REC_K), F32) for _ in range(CH // SUBLANES)]
            for s in range(CH):
                j = s // SUBLANES
                vs = v[s:s + 1, :]
                for jj in range(j, _ROW1[s] // SUBLANES):
                    blk = a[_OFF_OF[s] + (jj - j) * SUBLANES:_OFF_OF[s] + (jj - j + 1) * SUBLANES, :]
                    if jj == j:
                        blk = jnp.where(sub >= s - j * SUBLANES, blk, 0.0)
                    acc[jj] = acc[jj] + blk * vs
            o_s[rs, cs] = o + jnp.concatenate(acc, axis=0) + jnp.concatenate([jnp.zeros((HALF, REC_K), F32), o_cross], axis=0)

        def park(slot, results):
            for hp, (o, a, s10, st_new) in enumerate(results):
                car_o[slot, hp], car_a[slot, hp], car_s[slot, hp], car_st[slot, hp] = o, a, s10, st_new

        def retire(c, slot):
            for hp in range(HP):
                advance_state(c, hp, car_st[slot, hp])
            yield
            crosses = [cross(c, hp, car_s[slot, hp]) for hp in range(HP)]
            for hp in range(HP):
                finish(c, hp, car_o[slot, hp], car_a[slot, hp], crosses[hp])

        def step(c, slot):
            closing = retire(c - 1, slot)
            next(closing)
            park(1 - slot, [issue(c, hp) for hp in range(HP)])
            next(closing, None)

        def trip(j, carry):
            step(2 * j + 1, 0)
            step(2 * j + 2, 1)
            return carry

        assert nck % 2 == 0
        park(0, [issue(0, hp) for hp in range(HP)])
        lax.fori_loop(0, nck // 2 - 1, trip, 0)
        step(nck - 1, 0)
        for _ in retire(nck - 1, 1):
            pass
        oraw_ref[...] = o_s[...]
        for hp in range(HP):
            cs = slice(hp * REC_K, (hp + 1) * REC_K)
            o, zc = o_s[:, cs], z_ref[:, cs]
            on = (o * lax.rsqrt(jnp.mean(o * o, axis=-1, keepdims=True) + NORM_EPS)) * gw_ref[...]
            g_ref[:, cs] = (on * (zc * _sigmoid(zc))).astype(BF16)

    blk, head, lbs, gws, hist = _hgrn_specs(tb, nt, False)
    return pl.pallas_call(
        body, name="hgrn_fwd", grid=(REC_HEADS // HP, b_loc, nt),
        in_specs=[blk(0), blk(1), blk(2), blk(3), lbs, gws], out_specs=[head, head, hist],
        out_shape=[jax.ShapeDtypeStruct((n, REC_WIDTH), F32), jax.ShapeDtypeStruct((n, REC_WIDTH), BF16),
                   jax.ShapeDtypeStruct((REC_HEADS, b_loc, t_len // CH, REC_K, REC_K), F32)],
        scratch_shapes=[pltpu.VMEM((tb, _HW), F32)] * 4 + [pltpu.VMEM((HP, REC_K, REC_K), F32)] + [
            pltpu.VMEM((2, HP, CH, REC_K), F32), pltpu.VMEM((2, HP, sum(_ROWS_OF), REC_K), F32),
            pltpu.VMEM((2, HP, HALF, HALF), F32), pltpu.VMEM((2, HP, REC_K, REC_K), F32)],
        compiler_params=_params(("arbitrary", "arbitrary", "arbitrary"), 48),
    )(p, p, p, p, lb, gw)


def _hgrn_bwd(p, lb, gw, oraw, sh, dg, b_loc, t_len):
    n = p.shape[0]
    tb = min(REC_TB, t_len)
    nt, nck = t_len // tb, tb // CH
    assert HP == REC_HEADS

    def body(qp_ref, fp_ref, i_ref, z_ref, lb_ref, gw_ref, oraw_ref, dg_ref, sh_ref,
             dp_ref, dlb_ref, dgw_ref,
             q_s, k_s, b_s, do_s, dqv_s, dk_s, db_s, dst_ref, car_r, car_dst, car_dec, car_a, car_da, car_x):
        dq_ref, df_ref, di_ref, dz_ref = (dp_ref.at[:, part * REC_WIDTH:(part + 1) * REC_WIDTH] for part in range(4))
        b, t = pl.program_id(1), pl.program_id(2)

        @pl.when(t == 0)
        def _():
            dst_ref[...] = jnp.zeros_like(dst_ref)

        @pl.when((b == 0) & (t == 0))
        def _():
            dlb_ref[...] = jnp.zeros_like(dlb_ref)
            dgw_ref[...] = jnp.zeros_like(dgw_ref)

        lbv, qp, fp = lb_ref[...], qp_ref[...], fp_ref[...]
        qv, g, kk, sig_neg = _gates(qp, fp, lbv)
        q_s[...] = qv
        k_s[...] = kk
        _cumsum_chunks(_tri(True), g, b_s, tb)
        gwv = gw_ref[...]
        for hp in range(HP):
            cs = slice(hp * REC_K, (hp + 1) * REC_K)
            o, zc, dgv = oraw_ref[:, cs], z_ref[:, cs], dg_ref[:, cs]
            rn = lax.rsqrt(jnp.mean(o * o, axis=-1, keepdims=True) + NORM_EPS)
            on = o * rn
            sgz = _sigmoid(zc)
            dz_ref[:, cs] = (dgv * (on * gwv) * (sgz * (1.0 + zc * (1.0 - sgz)))).astype(BF16)
            dpre = dgv * (zc * sgz)
            dgw_ref[hp] += jnp.sum(dpre * on, axis=0, keepdims=True)
            don = dpre * gwv
            do_s[:, cs] = rn * (don - on * jnp.mean(don * on, axis=-1, keepdims=True))

        ones = jnp.ones((REC_K, REC_K), BF16)
        sub = lax.broadcasted_iota(jnp.int32, (SUBLANES, REC_K), 0)
        rowid = lax.broadcasted_iota(jnp.int32, (CH, REC_K), 0)
        ngrp = CH // SUBLANES
        piece_row = lax.broadcasted_iota(jnp.int32, (1, sum(_ROWS_OF)), 1)
        key_of = jnp.zeros((1, sum(_ROWS_OF)), jnp.int32)
        for s in range(1, CH):
            key_of = jnp.where(piece_row >= _OFF_OF[s], s, key_of)
        pick = (key_of == lax.broadcasted_iota(jnp.int32, (CH, sum(_ROWS_OF)), 0)).astype(BF16)

        def operands(c, hp):
            rs, cs = _chunk_rows(c), slice(hp * REC_K, (hp + 1) * REC_K)
            return rs, cs, q_s[rs, cs], k_s[rs, cs], b_s[rs, cs], i_ref[rs, cs], do_s[rs, cs]

        def issue(c, hp, slot):
            _, _, q, k, bc, v, do = operands(c, hp)
            st, dst = sh_ref[hp, 0, c], dst_ref[hp]
            qe, kd = q * jnp.exp(bc), k * jnp.exp(bc[CH - 1:CH, :] - bc)
            do_bf, dst_bf = do.astype(BF16), dst.astype(BF16)
            car_r[slot, hp, 0:CH] = jnp.dot(do_bf, st.astype(BF16), preferred_element_type=F32)
            car_r[slot, hp, CH:2 * CH] = jnp.dot(v.astype(BF16), dst_bf, preferred_element_type=F32)
            car_r[slot, hp, 2 * CH:3 * CH] = lax.dot_general(kd.astype(BF16), dst_bf, NT_DIMS, preferred_element_type=F32)
            car_dst[slot, hp] = lax.dot_general(do_bf, qe.astype(BF16), TN_DIMS, preferred_element_type=F32)
            dec = jnp.concatenate([_pair_rows(bc, s) for s in range(CH)], axis=0)
            qk = jnp.concatenate([q[_ROW0[s]:_ROW1[s], :] * k[s:s + 1, :] for s in range(CH)], axis=0)
            x = jnp.concatenate([do[_ROW0[s]:_ROW1[s], :] * v[s:s + 1, :] for s in range(CH)], axis=0)
            car_dec[slot, hp] = dec
            car_a[slot, hp] = jnp.dot((qk * dec).astype(BF16), ones, preferred_element_type=F32)
            car_da[slot, hp] = jnp.dot(x.astype(BF16), ones, preferred_element_type=F32)
            qe1, _, ke0, _ = _cross_half(q, k, bc)
            qe1_bf, ke0_bf = qe1.astype(BF16), ke0.astype(BF16)
            do1_bf, v0_bf = do[HALF:, :].astype(BF16), v[:HALF, :].astype(BF16)
            car_x[slot, hp, 0:HALF] = lax.dot_general(ke0_bf, qe1_bf, NT_DIMS, preferred_element_type=F32)
            car_x[slot, hp, HALF:2 * HALF] = lax.dot_general(do1_bf, v0_bf, NT_DIMS, preferred_element_type=F32)
            car_x[slot, hp, 2 * HALF:3 * HALF] = lax.dot_general(v0_bf, do1_bf, NT_DIMS, preferred_element_type=F32)

        def advance_state(c, hp, slot):
            ebl = jnp.exp(b_s[_chunk_rows(c, CH - SUBLANES, SUBLANES), hp * REC_K:(hp + 1) * REC_K][SUBLANES - 1:, :])
            st, dst = sh_ref[hp, 0, c], dst_ref[hp]
            dst_ref[hp] = dst * ebl + car_dst[slot, hp]
            return ebl * jnp.sum(st * dst, axis=0, keepdims=True)

        def cross(c, hp, slot):
            _, _, q, k, bc, v, do = operands(c, hp)
            qe1, _, ke0, _ = _cross_half(q, k, bc)
            xs = car_x[slot, hp]
            dqe1 = jnp.dot(xs[HALF:2 * HALF].astype(BF16), ke0.astype(BF16), preferred_element_type=F32)
            dke0 = jnp.dot(xs[2 * HALF:].astype(BF16), qe1.astype(BF16), preferred_element_type=F32)
            dv1 = jnp.dot(xs[:HALF].astype(BF16), do[HALF:, :].astype(BF16), preferred_element_type=F32)
            return dqe1, dke0, dv1

        def retire(c, slot):
            dbl_state = [advance_state(c, hp, slot) for hp in range(HP)]
            yield
            crossed = [cross(c, hp, slot) for hp in range(HP)]
            for hp in range(HP):
                finish(c, hp, slot, dbl_state[hp], *crossed[hp])

        def step(c, slot):
            closing = retire(c + 1, slot)
            next(closing)
            for hp in range(HP):
                issue(c, hp, 1 - slot)
            next(closing, None)

        def trip(j, carry):
            step(nck - 2 - 2 * j, 0)
            step(nck - 3 - 2 * j, 1)
            return carry

        def finish(c, hp, slot, dbl_state, dqe1, dke0, dv1):
            rs, cs, q, k, bc, v, do = operands(c, hp)
            eb, ekd = jnp.exp(bc), jnp.exp(bc[CH - 1:CH, :] - bc)
            qe, kd = q * eb, k * ekd
            qe1, e1, ke0, e0 = _cross_half(q, k, bc)
            dqe, dkd, dv = car_r[slot, hp, 0:CH], car_r[slot, hp, CH:2 * CH], car_r[slot, hp, 2 * CH:3 * CH]
            a, da, decs = car_a[slot, hp], car_da[slot, hp], car_dec[slot, hp]
            dec = [decs[_OFF_OF[s]:_OFF_OF[s] + _ROWS_OF[s], :] for s in range(CH)]
            dbl = jnp.sum(dkd * kd, axis=0, keepdims=True) + dbl_state
            dq_acc = [jnp.zeros((SUBLANES, REC_K), F32) for _ in range(ngrp)]
            uk, uv = [], []
            for s in range(CH):
                j = s // SUBLANES
                r0 = j * SUBLANES
                ks = k[s:s + 1, :]
                for jj in range(j, _ROW1[s] // SUBLANES):
                    lo, hi = _OFF_OF[s] + (jj - j) * SUBLANES, _OFF_OF[s] + (jj - j + 1) * SUBLANES
                    a_blk, da_blk = a[lo:hi, :], da[lo:hi, :]
                    if jj == j:
                        keep = sub >= s - r0
                        a_blk, da_blk = jnp.where(keep, a_blk, 0.0), jnp.where(keep, da_blk, 0.0)
                    rows = slice(jj * SUBLANES, (jj + 1) * SUBLANES)
                    tt = da_blk * dec[s][(jj - j) * SUBLANES:(jj - j + 1) * SUBLANES, :]
                    dq_acc[jj] = dq_acc[jj] + tt * ks
                    uk.append(tt * q[rows, :])
                    uv.append(a_blk * do[rows, :])
            dk_in = jnp.dot(pick, jnp.concatenate(uk, axis=0).astype(BF16), preferred_element_type=F32)
            dv_in = jnp.dot(pick, jnp.concatenate(uv, axis=0).astype(BF16), preferred_element_type=F32)
            zero_half = jnp.zeros((HALF, REC_K), F32)
            dq_x = jnp.concatenate([zero_half, dqe1 * e1], axis=0)
            dk_x = jnp.concatenate([dke0 * e0, zero_half], axis=0)
            dv_x = jnp.concatenate([dv1, zero_half], axis=0)
            db_x = jnp.concatenate([-(dke0 * ke0), dqe1 * qe1], axis=0)
            dq_in = jnp.concatenate(dq_acc, axis=0)
            dqv_s[rs, cs] = dqe * eb + dq_in + dq_x
            dk_s[rs, cs] = dkd * ekd + dk_in + dk_x
            di_ref[rs, cs] = (dv + dv_in + dv_x).astype(BF16)
            db = dqe * qe - dkd * kd + q * dq_in - k * dk_in + db_x
            db_s[rs, cs] = db + jnp.where(rowid == CH - 1, dbl, 0.0)

        assert nck % 2 == 0
        for hp in range(HP):
            issue(nck - 1, hp, 0)
        lax.fori_loop(0, nck // 2 - 1, trip, 0)
        step(0, 0)
        for _ in retire(0, 1):
            pass
        up = _tri(False)
        sgq = _sigmoid(qp)
        dq_ref[...] = (dqv_s[...] * (sgq * (1.0 + qp * (1.0 - sgq)))).astype(BF16)
        dlb_acc = jnp.zeros((1, _HW), F32)
        for r in range(0, tb, _CUM_ROWS):
            rows = slice(r, r + _CUM_ROWS)
            dgl = _dot3(up, db_s[rows, :])
            dfg = dgl * jnp.exp(-g[rows, :]) - dk_s[rows, :]
            sn = sig_neg[rows, :]
            df_ref[rows, :] = (dfg * (1.0 - lbv) * (1.0 - sn) * sn).astype(BF16)
            dlb_acc = dlb_acc + jnp.sum(dfg * sn, axis=0, keepdims=True)
        dlb_ref[...] += dlb_acc

    blk, head, lbs, gws, hist = _hgrn_specs(tb, nt, True)
    out_specs = [pl.BlockSpec((tb, REC_IN), lambda h, b, t: (b * nt + nt - 1 - t, 0)), lbs,
                 pl.BlockSpec((HP, 1, REC_K), lambda h, b, t: (h, 0, 0))]
    out_shape = [jax.ShapeDtypeStruct((n, REC_IN), BF16),
                 jax.ShapeDtypeStruct((1, REC_WIDTH), F32), jax.ShapeDtypeStruct((REC_HEADS, 1, REC_K), F32)]
    return pl.pallas_call(
        body, name="hgrn_bwd", grid=(REC_HEADS // HP, b_loc, nt),
        in_specs=[blk(0), blk(1), blk(2), blk(3), lbs, gws, head, head, hist],
        out_specs=out_specs, out_shape=out_shape,
        scratch_shapes=[pltpu.VMEM((tb, _HW), F32)] * 7 + [pltpu.VMEM((HP, REC_K, REC_K), F32)] + [
            pltpu.VMEM((2, HP, 3 * CH, REC_K), F32), pltpu.VMEM((2, HP, REC_K, REC_K), F32)] + [
            pltpu.VMEM((2, HP, sum(_ROWS_OF), REC_K), F32)] * 3 + [pltpu.VMEM((2, HP, 3 * HALF, HALF), F32)],
        compiler_params=_params(("arbitrary", "arbitrary", "arbitrary"), 56),
    )(p, p, p, p, lb, gw, oraw, dg, sh)


def _postnorm_bwd_nt(dxo, y, qw, w, name):
    n = dxo.shape[0]

    def body(dx_ref, y_ref, qw_ref, w_ref, dg_ref, dy_ref, dqw_ref, db_ref):
        @pl.when(pl.program_id(0) == 0)
        def _():
            dqw_ref[...] = jnp.zeros_like(dqw_ref)
            db_ref[...] = jnp.zeros_like(db_ref)

        yv, dxv = y_ref[...], dx_ref[...]
        r = lax.rsqrt(jnp.mean(yv * yv, axis=-1, keepdims=True) + NORM_EPS)
        u = yv * r
        du = dxv * qw_ref[...]
        dy = r * (du - u * jnp.mean(du * u, axis=-1, keepdims=True))
        dqw_ref[...] += jnp.sum(dxv * u, axis=0, keepdims=True)
        db_ref[...] += jnp.sum(dy, axis=0, keepdims=True)
        dyb = dy.astype(BF16)
        dy_ref[...] = dyb
        dg_ref[...] = lax.dot_general(dyb, w_ref[...], NT_DIMS, preferred_element_type=F32)

    tm = _wide_rows(n)
    rows = pl.BlockSpec((tm, D_MODEL), lambda i: (i, 0))
    const = lambda shape: pl.BlockSpec(shape, lambda i: (0, 0))
    return pl.pallas_call(
        body, name=name, grid=(n // tm,), in_specs=[rows, rows, const((1, D_MODEL)), const((D_MODEL, D_MODEL))],
        out_specs=[rows, rows, const((1, D_MODEL)), const((1, D_MODEL))],
        out_shape=[jax.ShapeDtypeStruct((n, D_MODEL), F32), jax.ShapeDtypeStruct((n, D_MODEL), BF16),
                   jax.ShapeDtypeStruct((1, D_MODEL), F32), jax.ShapeDtypeStruct((1, D_MODEL), F32)],
        compiler_params=_params(("arbitrary",), 48),
    )(dxo, y, qw, w)


def _nt_prenorm_bwd(dps, w, x, pw, dxo, has_bias, name, parts=()):
    n = x.shape[0]
    widths = [d.shape[1] for d in dps]
    m = sum(widths)
    npieces, nparts, steps = len(dps), len(parts), n // TM

    def body(*refs):
        dp_refs = refs[:npieces]
        w_ref, x_ref, pw_ref, dxo_ref = refs[npieces:npieces + 4]
        part_refs = refs[npieces + 4:npieces + 4 + nparts]
        dx_ref, dpw_ref, db_ref = refs[npieces + 4 + nparts:npieces + 7 + nparts]
        land_refs = refs[npieces + 7 + nparts:npieces + 7 + 2 * nparts]
        sems = refs[npieces + 7 + 2 * nparts:]

        @pl.when(pl.program_id(0) == 0)
        def _():
            dpw_ref[...] = jnp.zeros_like(dpw_ref)
            db_ref[...] = jnp.zeros_like(db_ref)
            if nparts:
                _scatter_start(part_refs, land_refs, sems)

        dh = jnp.zeros((TM, D_MODEL), F32)
        off = 0
        for dp_ref, wd in zip(dp_refs, widths):
            cn = _col_chunk(wd)
            for j in range(0, wd, cn):
                dpc = dp_ref[:, j:j + cn]
                if has_bias:
                    db_ref[:, off + j:off + j + cn] += jnp.sum(dpc, axis=0, keepdims=True)
                dh = dh + lax.dot_general(dpc.astype(BF16), w_ref[:, off + j:off + j + cn], NT_DIMS, preferred_element_type=F32)
            off += wd
        xv = x_ref[...]
        r = lax.rsqrt(jnp.mean(xv * xv, axis=-1, keepdims=True) + NORM_EPS)
        xn = xv * r
        dpw_ref[...] += jnp.sum(dh * xn, axis=0, keepdims=True)
        dxn = dh * pw_ref[...]
        dx_ref[...] = dxo_ref[...] + r * (dxn - xn * jnp.mean(dxn * xn, axis=-1, keepdims=True))

        if nparts:
            @pl.when(pl.program_id(0) == steps - 1)
            def _():
                _scatter_wait(part_refs, land_refs, sems)

    rows = pl.BlockSpec((TM, D_MODEL), lambda i: (i, 0))
    const = lambda shape: pl.BlockSpec(shape, lambda i: (0, 0))
    hbm = pl.BlockSpec(memory_space=pl.ANY)
    in_specs = ([pl.BlockSpec((TM, wd), lambda i: (i, 0)) for wd in widths] + [const((D_MODEL, m)), rows, const((1, D_MODEL)), rows]
                + [hbm] * nparts)
    return pl.pallas_call(
        body, name=name, grid=(steps,), in_specs=in_specs,
        out_specs=[rows, const((1, D_MODEL)), const((1, m))] + [hbm] * nparts,
        out_shape=[jax.ShapeDtypeStruct((n, D_MODEL), F32), jax.ShapeDtypeStruct((1, D_MODEL), F32),
                   jax.ShapeDtypeStruct((1, m), F32)] + _scatter_lands(parts),
        scratch_shapes=_scatter_sems(nparts) if nparts else [],
        compiler_params=_params(("arbitrary",), 56),
    )(*dps, w, x, pw, dxo, *parts)


def _matmul_tn(a, b, name):
    n, k = a.shape
    m = b.shape[1]
    tk, tm, tn = k, _col_chunk(m), TN_ROWS if n % TN_ROWS == 0 else n

    def body(a_ref, b_ref, o_ref):
        @pl.when(pl.program_id(2) == 0)
        def _():
            o_ref[...] = jnp.zeros_like(o_ref)

        o_ref[...] += lax.dot_general(a_ref[...], b_ref[...].astype(BF16), TN_DIMS, preferred_element_type=F32)

    return pl.pallas_call(
        body, name=name, grid=(k // tk, m // tm, n // tn),
        in_specs=[pl.BlockSpec((tn, tk), lambda i, j, l: (l, i)), pl.BlockSpec((tn, tm), lambda i, j, l: (l, j))],
        out_specs=pl.BlockSpec((tk, tm), lambda i, j, l: (i, j)),
        out_shape=jax.ShapeDtypeStruct((k, m), F32),
        compiler_params=_params(("arbitrary", "arbitrary", "arbitrary"), 48),
    )(a, b)


def _matmul_tn_by_owner(a, b, name):
    n, k = a.shape
    c = b.shape[1] // N_DEV
    tn = TN_ROWS if n % TN_ROWS == 0 else n
    steps = n // tn
    per = 2

    def body(a_ref, b_ref, o_ref, w_ref):
        @pl.when(pl.program_id(1) == 0)
        def _():
            o_ref[...] = jnp.zeros_like(o_ref)

        r = lax.dot_general(a_ref[...], b_ref[...], TN_DIMS, preferred_element_type=F32)
        for j in range(per):
            o_ref[j] += r[:, j * c:(j + 1) * c]

        @pl.when(pl.program_id(1) == steps - 1)
        def _():
            w_ref[...] = o_ref[...].astype(BF16)

    out = pl.BlockSpec((per, k, c), lambda j, l: (j, 0, 0))
    return pl.pallas_call(
        body, name=name, grid=(N_DEV // per, steps),
        in_specs=[pl.BlockSpec((tn, k), lambda j, l: (l, 0)), pl.BlockSpec((tn, per * c), lambda j, l: (l, j))],
        out_specs=[out, out],
        out_shape=[jax.ShapeDtypeStruct((N_DEV, k, c), F32), jax.ShapeDtypeStruct((N_DEV, k, c), BF16)],
        compiler_params=_params(("arbitrary", "arbitrary"), 48),
    )(a, b)


def _by_owner_cols(dw):
    k, m = dw.shape
    return dw.reshape(k, N_DEV, m // N_DEV).transpose(1, 0, 2)


def _own_and_bf16(part):
    return lax.dynamic_index_in_dim(part, _my_id(), 0, keepdims=False), part.astype(BF16)


def _step(x, ct, st, tgt, pre_w, post_w, wa_in, ba_in, sinks, wa_out_shard, ba_out, wr_in_shard, lb_logits, gnorm_w, wr_out_shard, b_loc, t_len):
    nb = t_len // BLK
    lb = _lower_bound(lb_logits)
    p0, h0, ga_out = _norm_matmul(x, pre_w[0:1], wa_in, ba_in, "attn_in_proj", [wa_out_shard])
    wa_out = ga_out.reshape(ATTN_WIDTH, D_MODEL)
    o0, g0, gr_in = _attn_fwd(p0, ct, st, sinks, b_loc, nb, [wr_in_shard])
    wr_in = gr_in.transpose(1, 0, 2).reshape(D_MODEL, REC_IN)
    y0, x1 = _outproj_postnorm(g0, wa_out, ba_out, x, post_w[0:1], "attn_out_proj")
    p1, h1, gr_out = _norm_matmul(x1, pre_w[1:2], wr_in, None, "rec_in_proj", [wr_out_shard])
    wr_out = gr_out.reshape(REC_WIDTH, D_MODEL)
    o1, g1, sh = _hgrn_fwd(p1, lb, gnorm_w, b_loc, t_len)
    dx2, dg1, dy1, dpost1, loss_tile = _outproj_loss_bwd(g1, wr_out, x1, post_w[1:2], tgt, "rec_out_proj_loss_bwd")
    d_wr_out = _matmul_tn(g1, dy1, "rec_w_out_grad")
    dp1, dlb, dgw = _hgrn_bwd(p1, lb, gnorm_w, o1, sh, dg1, b_loc, t_len)
    dx1, dpre1, _ = _nt_prenorm_bwd([dp1], wr_in, x1, pre_w[1:2], dx2, False, "rec_in_bwd")
    part_r_in, wire_r_in = _matmul_tn_by_owner(h1, dp1, "rec_w_in_grad")
    own_r_in = lax.dynamic_index_in_dim(part_r_in, _my_id(), 0, keepdims=False)
    dg0, dy0, dpost0, dba_out = _postnorm_bwd_nt(dx1, y0, post_w[0:1], wa_out, "attn_out_bwd")
    d_wa_out = _matmul_tn(g0, dy0, "attn_w_out_grad")
    owns, wires = zip(*[_own_and_bf16(part) for part in (
        d_wr_out.reshape(N_DEV, REC_WIDTH // N_DEV, D_MODEL), d_wa_out.reshape(N_DEV, ATTN_WIDTH // N_DEV, D_MODEL))])
    owns, wires = (own_r_in,) + owns, (wire_r_in,) + wires
    dp0, dsink_tile, *lands = _attn_bwd(p0, ct, st, sinks, o0, dg0, b_loc, nb, list(wires))
    d_wa_in = _matmul_tn(h0, dp0, "attn_w_in_grad")
    own_a_in, wire_a_in = _own_and_bf16(_by_owner_cols(_qkvz(d_wa_in)))
    dx0, dpre0, dba_in, land_a_in = _nt_prenorm_bwd([dp0], wa_in, x, pre_w[0:1], dx1, True, "attn_in_bwd", [wire_a_in])
    small = dict(pre=jnp.concatenate([dpre0, dpre1], axis=0), post=jnp.concatenate([dpost0, dpost1], axis=0),
                 ba_in=dba_in, sinks=dsink_tile[0:1, 0:N_HEADS], ba_out=dba_out, lb=dlb, gnorm=jnp.sum(dgw, axis=0))
    return loss_tile, dx0, list(zip(lands, owns)) + [(land_a_in, own_a_in)], small


def _my_id():
    return lax.axis_index("x") * 4 + lax.axis_index("y") * 2 + lax.axis_index("c")


def _peer(k):
    x, y, c = lax.axis_index("x"), lax.axis_index("y"), lax.axis_index("c")
    return (x ^ ((k >> 2) & 1), y ^ ((k >> 1) & 1), c ^ (k & 1))


def _peer_id(k):
    return _my_id() ^ k


def _all_gather_by_chip(shard, pos_col):
    n = pos_col.shape[0]

    def body(x_ref, p_ref, f_ref, s_ref, out_ref, ct_ref, st_ref, send_sems, recv_sems, local_sem):
        x, y, c = lax.axis_index("x"), lax.axis_index("y"), lax.axis_index("c")
        me, sibling = (x, y, c), (x, y, 1 - c)
        chips = [(1 - x, y), (x, 1 - y), (1 - x, 1 - y)]

        def rows(px, py, pc):
            return out_ref.at[4 * px + 2 * py + pc]

        def copy(k, block, to, src=None):
            return pltpu.make_async_remote_copy(src_ref=rows(*block) if src is None else src, dst_ref=rows(*block),
                                                send_sem=send_sems.at[k], recv_sem=recv_sems.at[k], device_id=to, device_id_type=MESH)

        mine = pltpu.make_async_copy(x_ref, rows(*me), local_sem)
        mine.start()
        first = [copy(0, me, sibling, src=x_ref)] + [copy(1 + j, me, (*chip, c), src=x_ref) for j, chip in enumerate(chips)]
        for cp in first:
            cp.start()
        _rope_tables_into(p_ref, f_ref, s_ref, ct_ref, st_ref)
        passed = [copy(4 + j, (*chip, c), sibling) for j, chip in enumerate(chips)]
        for j, chip in enumerate(chips):
            copy(1 + j, (*chip, c), me).wait_recv()
            passed[j].start()
        copy(0, sibling, me).wait_recv()
        for j, chip in enumerate(chips):
            copy(4 + j, (*chip, 1 - c), me).wait_recv()
        for cp in first + passed:
            cp.wait_send()
        mine.wait()

    hbm, vmem = pl.BlockSpec(memory_space=pl.ANY), pl.BlockSpec(memory_space=pltpu.VMEM)
    return pl.pallas_call(
        body, name="comm_all_gather_by_chip", in_specs=[hbm, vmem, vmem, vmem], out_specs=[hbm, vmem, vmem],
        out_shape=[jax.ShapeDtypeStruct((N_DEV,) + shard.shape, shard.dtype)] + [jax.ShapeDtypeStruct((n, LANES), F32)] * 2,
        scratch_shapes=[pltpu.SemaphoreType.DMA((N_DEV - 1,)), pltpu.SemaphoreType.DMA((N_DEV - 1,)), pltpu.SemaphoreType.DMA],
        compiler_params=_params(None, 32),
    )(shard, pos_col, *_rope_lanes())


def _gather_shapes(shards):
    return [jax.ShapeDtypeStruct((N_DEV,) + s.shape, s.dtype) for s in shards]


def _gather_sems(nsh):
    return [pltpu.SemaphoreType.DMA((nsh, N_DEV - 1)), pltpu.SemaphoreType.DMA((nsh, N_DEV - 1)), pltpu.SemaphoreType.DMA((nsh,))]


def _gather_copies(ins, outs, sems, received):
    send_sems, recv_sems, local_sems = sems
    me = _my_id()
    local = [pltpu.make_async_copy(ins[a], outs[a].at[me], local_sems.at[a]) for a in range(len(ins))]
    remote = [pltpu.make_async_remote_copy(
        src_ref=ins[a], dst_ref=outs[a].at[_peer_id(k) if received else me], send_sem=send_sems.at[a, k - 1],
        recv_sem=recv_sems.at[a, k - 1], device_id=_peer(k), device_id_type=MESH)
        for a in range(len(ins)) for k in range(1, N_DEV)]
    return local, remote


def _gather_start(ins, outs, sems):
    local, sends = _gather_copies(ins, outs, sems, False)
    for cp in local + sends:
        cp.start()


def _gather_wait(ins, outs, sems):
    local, recvs = _gather_copies(ins, outs, sems, True)
    for cp in recvs:
        cp.wait_recv()
    for cp in recvs:
        cp.wait_send()
    for cp in local:
        cp.wait()


def _scatter_lands(parts):
    return [jax.ShapeDtypeStruct((N_DEV - 1,) + p.shape[1:], p.dtype) for p in parts]


def _scatter_sems(nparts):
    return [pltpu.SemaphoreType.DMA((nparts, N_DEV - 1)), pltpu.SemaphoreType.DMA((nparts, N_DEV - 1))]


def _scatter_copies(parts, lands, sems):
    send_sems, recv_sems = sems
    return [pltpu.make_async_remote_copy(
        src_ref=parts[a].at[_peer_id(k)], dst_ref=lands[a].at[k - 1], send_sem=send_sems.at[a, k - 1],
        recv_sem=recv_sems.at[a, k - 1], device_id=_peer(k), device_id_type=MESH)
        for a in range(len(parts)) for k in range(1, N_DEV)]


def _scatter_start(parts, lands, sems):
    for cp in _scatter_copies(parts, lands, sems):
        cp.start()


def _scatter_wait(parts, lands, sems):
    copies = _scatter_copies(parts, lands, sems)
    for cp in copies:
        cp.wait_recv()
    for cp in copies:
        cp.wait_send()


def _adamw(w, g, m, v):
    m2 = ADAM_B1 * m + (1.0 - ADAM_B1) * g
    v2 = ADAM_B2 * v + (1.0 - ADAM_B2) * (g * g)
    m_hat = m2 / (1.0 - ADAM_B1 ** ADAM_STEP)
    v_hat = v2 / (1.0 - ADAM_B2 ** ADAM_STEP)
    delta = -ADAM_LR * (m_hat / (jnp.sqrt(v_hat) + ADAM_EPS) + ADAM_WD * w)
    return delta, m2, v2


def _sum_adamw(land, own, w, m, v, name):
    r, c = own.shape
    rb = min(r, 256)

    def body(land_ref, own_ref, w_ref, m_ref, v_ref, g_ref, d_ref, m2_ref, v2_ref):
        me = _my_id()
        g = jnp.zeros((rb, c), F32)
        for dev in range(N_DEV):
            k = dev ^ me
            g = g + jnp.where(k == 0, own_ref[...], land_ref[jnp.maximum(k - 1, 0)].astype(F32))
        delta, m2, v2 = _adamw(w_ref[...], g, m_ref[...], v_ref[...])
        g_ref[...] = g
        d_ref[...] = delta
        m2_ref[...] = m2
        v2_ref[...] = v2

    rows = pl.BlockSpec((rb, c), lambda i: (i, 0))
    return pl.pallas_call(
        body, name=name, grid=(r // rb,), in_specs=[pl.BlockSpec((N_DEV - 1, rb, c), lambda i: (0, i, 0))] + [rows] * 4,
        out_specs=[rows] * 4, out_shape=[jax.ShapeDtypeStruct((r, c), F32)] * 4,
        compiler_params=_params(("arbitrary",), 32),
    )(land, own, w, m, v)


_SMALL = [("pre_norm_w", 2048), ("post_norm_w", 2048), ("attn_b_in", 2304), ("attn_sinks", 16), ("attn_b_out", 1024),
          ("rec_lb_logits", 2048), ("rec_gnorm_w", 128), ("loss", 1)]
_TILE = SUBLANES * LANES


def _small_rows(size):
    return -(-size // _TILE) * SUBLANES


_SMALL_OFF = {}
_r = 0
for _name, _size in _SMALL:
    _SMALL_OFF[_name] = _r
    _r += _small_rows(_size)
_SMALL_ROWS = _r


def _pack_small(pieces):
    out = []
    for name, size in _SMALL:
        flat = pieces[name].reshape(-1).astype(F32)
        out.append(jnp.pad(flat, (0, _small_rows(size) * LANES - size)).reshape(-1, LANES))
    return jnp.concatenate(out, axis=0)


def _unpack_small(packed, shapes):
    return {name: packed[_SMALL_OFF[name]:_SMALL_OFF[name] + _small_rows(size)].reshape(-1)[:size].reshape(shapes[name])
            for name, size in _SMALL}


def _small_allreduce_adamw(gpart, w, m, v):
    lb0 = _SMALL_OFF["rec_lb_logits"]

    def body(gp_ref, w_ref, m_ref, v_ref, g_ref, d_ref, m2_ref, v2_ref, land_ref, send_sems, recv_sems):
        me = _my_id()
        sent = []
        for k in range(1, N_DEV):
            cp = pltpu.make_async_remote_copy(src_ref=gp_ref, dst_ref=land_ref.at[k - 1], send_sem=send_sems.at[k - 1],
                                              recv_sem=recv_sems.at[k - 1], device_id=_peer(k), device_id_type=MESH)
            cp.start()
            sent.append(cp)
        for cp in sent:
            cp.wait_recv()
        for cp in sent:
            cp.wait_send()
        g = jnp.zeros((_SMALL_ROWS, LANES), F32)
        for dev in range(N_DEV):
            k = dev ^ me
            g = g + jnp.where(k == 0, gp_ref[...], land_ref[jnp.maximum(k - 1, 0)])
        g_ref[...] = g
        l0, l1 = w_ref[lb0:lb0 + SUBLANES, :], w_ref[lb0 + SUBLANES:lb0 + 2 * SUBLANES, :]
        mx = jnp.maximum(l0, l1)
        e0, e1 = jnp.exp(l0 - mx), jnp.exp(l1 - mx)
        p1 = e1 / (e0 + e1)
        dl1 = (1.0 - p1) * p1 * g[lb0:lb0 + SUBLANES, :]
        g_ref[lb0:lb0 + SUBLANES, :] = -dl1
        g_ref[lb0 + SUBLANES:lb0 + 2 * SUBLANES, :] = dl1
        delta, m2, v2 = _adamw(w_ref[...], g_ref[...], m_ref[...], v_ref[...])
        d_ref[...] = delta
        m2_ref[...] = m2
        v2_ref[...] = v2

    vmem = pl.BlockSpec(memory_space=pltpu.VMEM)
    return pl.pallas_call(
        body, name="comm_small_allreduce_adamw", in_specs=[vmem] * 4, out_specs=[vmem] * 4,
        out_shape=[jax.ShapeDtypeStruct((_SMALL_ROWS, LANES), F32)] * 4,
        scratch_shapes=[pltpu.VMEM((N_DEV - 1, _SMALL_ROWS, LANES), F32), pltpu.SemaphoreType.DMA((N_DEV - 1,)),
                        pltpu.SemaphoreType.DMA((N_DEV - 1,))],
    )(gpart, w, m, v)


def _qzkv(a):
    kv_end = ATTN_WIDTH + 2 * KV_WIDTH
    return jnp.concatenate([a[..., :ATTN_WIDTH], a[..., kv_end:], a[..., ATTN_WIDTH:kv_end]], axis=-1)


def _qkvz(a):
    return jnp.concatenate([a[..., :ATTN_WIDTH], a[..., 2 * ATTN_WIDTH:], a[..., ATTN_WIDTH:2 * ATTN_WIDTH]], axis=-1)


def kernel(x, positions, pre_norm_w, post_norm_w, attn_w_in, attn_b_in, attn_sinks, attn_w_out, attn_b_out, rec_w_in, rec_lb_logits, rec_gnorm_w, rec_w_out, loss_target, m_pre_norm_w, m_post_norm_w, m_attn_w_in, m_attn_b_in, m_attn_sinks, m_attn_w_out, m_attn_b_out, m_rec_w_in, m_rec_lb_logits, m_rec_gnorm_w, m_rec_w_out, v_pre_norm_w, v_post_norm_w, v_attn_w_in, v_attn_b_in, v_attn_sinks, v_attn_w_out, v_attn_b_out, v_rec_w_in, v_rec_lb_logits, v_rec_gnorm_w, v_rec_w_out):
    b_loc, t_len, _ = x.shape
    n = b_loc * t_len
    ga_in, ct, st = _all_gather_by_chip(attn_w_in[0].astype(BF16), positions.reshape(n, 1).astype(F32))
    wa_in = _qzkv(ga_in.transpose(1, 0, 2).reshape(D_MODEL, ATTN_IN))

    loss_tile, dx, landed, small = _step(
        x.reshape(n, D_MODEL), ct, st, loss_target.reshape(n, D_MODEL),
        pre_norm_w, post_norm_w, wa_in, _qzkv(attn_b_in), attn_sinks, attn_w_out[0].astype(BF16), attn_b_out,
        rec_w_in[0].astype(BF16), rec_lb_logits, rec_gnorm_w, rec_w_out[0].astype(BF16), b_loc, t_len)

    lift = lambda outs: tuple(a[None] for a in outs)
    (l_r_in, o_r_in), (l_r_out, o_r_out), (l_a_out, o_a_out), (l_a_in, o_a_in) = landed
    r_a_in = lift(_sum_adamw(l_a_in, o_a_in, attn_w_in[0], m_attn_w_in[0], v_attn_w_in[0], "adamw_attn_w_in"))
    r_r_in = lift(_sum_adamw(l_r_in, o_r_in, rec_w_in[0], m_rec_w_in[0], v_rec_w_in[0], "adamw_rec_w_in"))
    r_r_out = lift(_sum_adamw(l_r_out, o_r_out, rec_w_out[0], m_rec_w_out[0], v_rec_w_out[0], "adamw_rec_w_out"))
    r_a_out = lift(_sum_adamw(l_a_out, o_a_out, attn_w_out[0], m_attn_w_out[0], v_attn_w_out[0], "adamw_attn_w_out"))

    gsmall = dict(pre_norm_w=small["pre"], post_norm_w=small["post"], attn_b_in=_qkvz(small["ba_in"]), attn_sinks=small["sinks"],
                  attn_b_out=small["ba_out"], rec_lb_logits=jnp.concatenate([small["lb"], jnp.zeros_like(small["lb"])], axis=0),
                  rec_gnorm_w=small["gnorm"], loss=loss_tile[0:1, 0:1])
    nil = jnp.zeros((1, 1), F32)
    wsmall = dict(pre_norm_w=pre_norm_w, post_norm_w=post_norm_w, attn_b_in=attn_b_in, attn_sinks=attn_sinks,
                  attn_b_out=attn_b_out, rec_lb_logits=rec_lb_logits, rec_gnorm_w=rec_gnorm_w, loss=nil)
    msmall = dict(pre_norm_w=m_pre_norm_w, post_norm_w=m_post_norm_w, attn_b_in=m_attn_b_in, attn_sinks=m_attn_sinks,
                  attn_b_out=m_attn_b_out, rec_lb_logits=m_rec_lb_logits, rec_gnorm_w=m_rec_gnorm_w, loss=nil)
    vsmall = dict(pre_norm_w=v_pre_norm_w, post_norm_w=v_post_norm_w, attn_b_in=v_attn_b_in, attn_sinks=v_attn_sinks,
                  attn_b_out=v_attn_b_out, rec_lb_logits=v_rec_lb_logits, rec_gnorm_w=v_rec_gnorm_w, loss=nil)
    shapes = {k: a.shape for k, a in wsmall.items()}
    packed = _small_allreduce_adamw(_pack_small(gsmall), _pack_small(wsmall), _pack_small(msmall), _pack_small(vsmall))
    sg, sd, sm, sv = [_unpack_small(a, shapes) for a in packed]

    big = {"attn_w_in": r_a_in, "attn_w_out": r_a_out, "rec_w_in": r_r_in, "rec_w_out": r_r_out}
    order = ["pre_norm_w", "post_norm_w", "attn_w_in", "attn_b_in", "attn_sinks", "attn_w_out", "attn_b_out", "rec_w_in",
             "rec_lb_logits", "rec_gnorm_w", "rec_w_out"]
    outs = [sg["loss"][0, 0], dx.reshape(b_loc, t_len, D_MODEL)]
    for idx, small_set in enumerate((sg, sd, sm, sv)):
        outs += [big[nm][idx] if nm in big else small_set[nm] for nm in order]
    return tuple(outs)
```

```python
import numpy as np
import jax
import jax.numpy as jnp
from jax import lax
from jax.experimental import pallas as pl
from jax.experimental.pallas import tpu as pltpu

F32, BF16 = jnp.float32, jnp.bfloat16
MESH = pl.DeviceIdType.MESH
N_DEV = 8

D_MODEL = 1024
N_HEADS, HEAD_DIM, N_KV, GROUP = 16, 64, 2, 8
ATTN_WIDTH, KV_WIDTH = 1024, 128
ATTN_IN = 2 * ATTN_WIDTH + 2 * KV_WIDTH
BLK = 128
ROPE_THETA, ROPE_HALF = 500000.0, 8
REC_HEADS, REC_K = 8, 128
REC_WIDTH = REC_HEADS * REC_K
REC_IN = 4 * REC_WIDTH
TN_ROWS = 2048
CH = 32
NORM_EPS = 1e-6
F32_TINY = 1.1754944e-38
ADAM_LR, ADAM_B1, ADAM_B2, ADAM_EPS, ADAM_WD, ADAM_STEP = 0.001, 0.9, 0.999, 1e-08, 0.01, 10

LANES, SUBLANES = 128, 8
TM = 512
NT_DIMS = (((1,), (1,)), ((), ()))
TN_DIMS = (((0,), (0,)), ((), ()))
MB = 2 ** 20


def _params(sem=None, vmem_mb=48, **kw):
    return pltpu.CompilerParams(dimension_semantics=sem, vmem_limit_bytes=vmem_mb * MB, **kw)


def _wide_rows(n):
    return 2 * TM if n % (2 * TM) == 0 else TM


def _col_chunk(m):
    return 768 if m % 1024 else 1024


def _sigmoid(x):
    return 1.0 / (1.0 + jnp.exp(-x))


def _split3(x):
    hi = x.astype(BF16)
    r1 = x - hi.astype(F32)
    mid = r1.astype(BF16)
    lo = (r1 - mid.astype(F32)).astype(BF16)
    return hi, mid, lo


def _dot3(l_bf, x):
    hi, mid, lo = _split3(x)
    return (jnp.dot(l_bf, hi, preferred_element_type=F32) + jnp.dot(l_bf, mid, preferred_element_type=F32)
            + jnp.dot(l_bf, lo, preferred_element_type=F32))


def _rope_lanes():
    lane = np.arange(LANES) % HEAD_DIM
    inv = np.float32(ROPE_THETA) ** (-(np.arange(ROPE_HALF, dtype=np.float32) * np.float32(2.0) / np.float32(2 * ROPE_HALF)))
    freq = np.where(lane < 2 * ROPE_HALF, inv[lane % ROPE_HALF], 0.0).astype(np.float32)[None, :]
    sign = np.where(lane < ROPE_HALF, -1.0, np.where(lane < 2 * ROPE_HALF, 1.0, 0.0)).astype(np.float32)[None, :]
    return jnp.asarray(freq), jnp.asarray(sign)


def _rope_tables_into(p_ref, f_ref, s_ref, c_out, s_out):
    def rows(i, carry):
        rs = pl.ds(pl.multiple_of(i * TM, TM), TM)
        ang = p_ref[rs, :] * f_ref[...]
        c_out[rs, :] = jnp.cos(ang)
        s_out[rs, :] = jnp.sin(ang) * s_ref[...]
        return carry

    lax.fori_loop(0, p_ref.shape[0] // TM, rows, 0)


def _rope_apply(xv, c, s, lm):
    partner = jnp.where(lm < ROPE_HALF, pltpu.roll(xv, LANES - ROPE_HALF, 1), pltpu.roll(xv, ROPE_HALF, 1))
    return xv * c + partner * s


def _rope_bwd(dy, c, s, lm):
    t = dy * s
    partner = jnp.where(lm < ROPE_HALF, pltpu.roll(t, LANES - ROPE_HALF, 1),
                        jnp.where(lm < 2 * ROPE_HALF, pltpu.roll(t, ROPE_HALF, 1), 0.0))
    return dy * c + partner


def _lower_bound(lb_logits):
    def body(l_ref, o_ref):
        l0, l1 = l_ref[0:1, :], l_ref[1:2, :]
        m = jnp.maximum(l0, l1)
        e0, e1 = jnp.exp(l0 - m), jnp.exp(l1 - m)
        o_ref[...] = e1 / (e0 + e1)

    return pl.pallas_call(body, name="lower_bound", out_shape=jax.ShapeDtypeStruct((1, lb_logits.shape[1]), F32))(lb_logits)


def _norm_matmul(x, pw, w, bias, name, shards=()):
    n, m = x.shape[0], w.shape[1]
    cn = _col_chunk(m)
    has_bias = bias is not None
    nsh, steps = len(shards), n // TM

    def body(*refs):
        refs = list(refs)
        x_ref, pw_ref, w_ref = refs[:3]
        b_ref = refs[3] if has_bias else None
        refs = refs[4 if has_bias else 3:]
        sh_in, (p_ref, h_ref), sh_out, sems = refs[:nsh], refs[nsh:nsh + 2], refs[nsh + 2:2 * nsh + 2], refs[2 * nsh + 2:]
        if nsh:
            @pl.when(pl.program_id(0) == 0)
            def _():
                _gather_start(sh_in, sh_out, sems)

        xv = x_ref[...]
        r = lax.rsqrt(jnp.mean(xv * xv, axis=-1, keepdims=True) + NORM_EPS)
        h = ((xv * r) * pw_ref[...]).astype(BF16)
        h_ref[...] = h
        for j in range(0, m, cn):
            acc = jnp.dot(h, w_ref[:, j:j + cn], preferred_element_type=F32)
            if has_bias:
                acc = acc + b_ref[:, j:j + cn]
            p_ref[:, j:j + cn] = acc

        if nsh:
            @pl.when(pl.program_id(0) == steps - 1)
            def _():
                _gather_wait(sh_in, sh_out, sems)

    rows = pl.BlockSpec((TM, D_MODEL), lambda i: (i, 0))
    const = lambda shape: pl.BlockSpec(shape, lambda i: (0, 0))
    hbm = pl.BlockSpec(memory_space=pl.ANY)
    in_specs = [rows, const((1, D_MODEL)), const((D_MODEL, m))] + ([const((1, m))] if has_bias else []) + [hbm] * nsh
    args = (x, pw, w) + ((bias,) if has_bias else ()) + tuple(shards)
    return pl.pallas_call(
        body, name=name, grid=(steps,), in_specs=in_specs,
        out_specs=[pl.BlockSpec((TM, m), lambda i: (i, 0)), rows] + [hbm] * nsh,
        out_shape=[jax.ShapeDtypeStruct((n, m), F32), jax.ShapeDtypeStruct((n, D_MODEL), BF16)] + _gather_shapes(shards),
        scratch_shapes=_gather_sems(nsh) if nsh else [],
        compiler_params=_params(("arbitrary",), 56),
    )(*args)


def _outproj_postnorm(g, w, bias, xres, qw, name):
    n = g.shape[0]

    def body(g_ref, w_ref, b_ref, x_ref, qw_ref, y_ref, o_ref):
        y = jnp.dot(g_ref[...], w_ref[...], preferred_element_type=F32) + b_ref[...]
        y_ref[...] = y
        r = lax.rsqrt(jnp.mean(y * y, axis=-1, keepdims=True) + NORM_EPS)
        o_ref[...] = x_ref[...] + (y * r) * qw_ref[...]

    tm = _wide_rows(n)
    rows = pl.BlockSpec((tm, D_MODEL), lambda i: (i, 0))
    const = lambda shape: pl.BlockSpec(shape, lambda i: (0, 0))
    return pl.pallas_call(
        body, name=name, grid=(n // tm,),
        in_specs=[rows, const((D_MODEL, D_MODEL)), const((1, D_MODEL)), rows, const((1, D_MODEL))],
        out_specs=[rows, rows], out_shape=[jax.ShapeDtypeStruct((n, D_MODEL), F32)] * 2,
        compiler_params=_params(("arbitrary",), 48),
    )(g, w, bias, xres, qw)


def _outproj_loss_bwd(g, w, xres, qw, tgt, name):
    n = g.shape[0]
    tm = _wide_rows(n)
    steps = n // tm

    def body(g_ref, w_ref, x_ref, qw_ref, t_ref, dx_ref, dg_ref, dy_ref, dqw_ref, loss_ref, acc_ref):
        i = pl.program_id(0)

        @pl.when(i == 0)
        def _():
            acc_ref[...] = jnp.zeros_like(acc_ref)
            dqw_ref[...] = jnp.zeros_like(dqw_ref)

        y = jnp.dot(g_ref[...], w_ref[...], preferred_element_type=F32)
        r = lax.rsqrt(jnp.mean(y * y, axis=-1, keepdims=True) + NORM_EPS)
        u = y * r
        e = (x_ref[...] + u * qw_ref[...]) - t_ref[...]
        dxn = e * (1.0 / D_MODEL)
        dx_ref[...] = dxn
        acc_ref[...] += jnp.sum(e * e, axis=0, keepdims=True)
        du = dxn * qw_ref[...]
        dy = (r * (du - u * jnp.mean(du * u, axis=-1, keepdims=True))).astype(BF16)
        dqw_ref[...] += jnp.sum(dxn * u, axis=0, keepdims=True)
        dy_ref[...] = dy
        dg_ref[...] = lax.dot_general(dy, w_ref[...], NT_DIMS, preferred_element_type=F32)

        @pl.when(i == steps - 1)
        def _():
            loss_ref[...] = jnp.full(loss_ref.shape, jnp.sum(acc_ref[...]) * (0.5 / D_MODEL), F32)

    rows = pl.BlockSpec((tm, D_MODEL), lambda i: (i, 0))
    const = lambda shape: pl.BlockSpec(shape, lambda i: (0, 0))
    return pl.pallas_call(
        body, name=name, grid=(steps,),
        in_specs=[rows, const((D_MODEL, D_MODEL)), rows, const((1, D_MODEL)), rows],
        out_specs=[rows, rows, rows, const((1, D_MODEL)), const((SUBLANES, LANES))],
        out_shape=[jax.ShapeDtypeStruct((n, D_MODEL), F32), jax.ShapeDtypeStruct((n, D_MODEL), F32),
                   jax.ShapeDtypeStruct((n, D_MODEL), BF16), jax.ShapeDtypeStruct((1, D_MODEL), F32),
                   jax.ShapeDtypeStruct((SUBLANES, LANES), F32)],
        scratch_shapes=[pltpu.VMEM((1, D_MODEL), F32)], compiler_params=_params(("arbitrary",), 56),
    )(g, w, xres, qw, tgt)


_QCOL, _ZCOL, _KCOL, _VCOL = 0, 1024, 2048, 2176


def _head_stack(chunks, heads, lt64):
    return jnp.concatenate([jnp.where(lt64 if n % 2 == 0 else ~lt64, chunks[n // 2], 0.0) for n in heads], axis=0)


def _dup_half(x, h, lt64):
    r = pltpu.roll(x, HEAD_DIM, 1)
    return jnp.where(lt64, x, r) if h == 0 else jnp.where(lt64, r, x)


def _pair_chunk(xt, c2):
    a, b = 2 * c2, 2 * c2 + 1
    return jnp.concatenate([xt[:HEAD_DIM, a * BLK:(a + 1) * BLK], xt[HEAD_DIM:, b * BLK:(b + 1) * BLK]], axis=0).T


def _attn_mask_t(i):
    key = lax.broadcasted_iota(jnp.int32, (2 * BLK, BLK), 0)
    qry = lax.broadcasted_iota(jnp.int32, (2 * BLK, BLK), 1)
    valid = (key > qry) & (key <= qry + BLK) & ((key >= BLK) | (i > 0))
    return jnp.tile(jnp.where(valid, 0.0, -1e30), (1, GROUP))


def _attn_probs_t(s, heads, sink_ref, mask):
    s = s + mask
    head = lax.broadcasted_iota(jnp.int32, (1, len(heads) * BLK), 1) >> 7
    sk = jnp.zeros((1, len(heads) * BLK), F32)
    for j, n in enumerate(heads):
        sk = jnp.where(head == j, sink_ref[0, n], sk)
    m = jnp.maximum(jnp.max(s, axis=0, keepdims=True), sk)
    p = jnp.exp(s - m)
    esk = jnp.exp(sk - m)
    inv = 1.0 / (jnp.sum(p, axis=0, keepdims=True) + esk)
    return p * inv, esk * inv


def _attn_fwd(p, ct, st, sinks, b_loc, nb, shards):
    n = p.shape[0]
    nsh = len(shards)

    def body(sink_ref, q_ref, z_ref, kc_ref, kp_ref, vc_ref, vp_ref, cc_ref, sc_ref, cp_ref, sp_ref, *rest):
        sh_in, (o_ref, g_ref), sh_out, sems = rest[:nsh], rest[nsh:nsh + 2], rest[nsh + 2:2 * nsh + 2], rest[2 * nsh + 2:]
        b, i = pl.program_id(0), pl.program_id(1)

        @pl.when((b == 0) & (i == 0))
        def _():
            _gather_start(sh_in, sh_out, sems)

        lane = lax.broadcasted_iota(jnp.int32, (BLK, LANES), 1)
        lm = lane & (HEAD_DIM - 1)
        cc, sc = cc_ref[...], sc_ref[...]
        kcat = jnp.concatenate([_rope_apply(kp_ref[...], cp_ref[...], sp_ref[...], lm),
                                _rope_apply(kc_ref[...], cc, sc, lm)], axis=0)
        vcat = jnp.concatenate([vp_ref[...], vc_ref[...]], axis=0)
        qr = [_rope_apply(q_ref[:, c * LANES:(c + 1) * LANES], cc, sc, lm) * (HEAD_DIM ** -0.5) for c in range(8)]
        valid = _attn_mask_t(i)
        lt64, lt64k = lane < HEAD_DIM, lax.broadcasted_iota(jnp.int32, (2 * BLK, LANES), 1) < HEAD_DIM
        def kv_head(h):
            heads = list(range(h * GROUP, (h + 1) * GROUP))
            kext, vext = _dup_half(kcat, h, lt64k).astype(BF16), _dup_half(vcat, h, lt64k).astype(BF16)
            qst = _head_stack(qr, heads, lt64).astype(BF16)
            s = lax.dot_general(kext, qst, NT_DIMS, preferred_element_type=F32)
            yield
            pn, _ = _attn_probs_t(s, heads, sink_ref, valid)
            ot = lax.dot_general(vext, pn.astype(BF16), TN_DIMS, preferred_element_type=F32)
            yield
            for c2 in range(GROUP // 2):
                oc = _pair_chunk(ot, c2)
                cols = slice((4 * h + c2) * LANES, (4 * h + c2 + 1) * LANES)
                zc = z_ref[:, cols]
                o_ref[:, cols] = oc
                g_ref[:, cols] = (oc * (zc * _sigmoid(zc))).astype(BF16)

        _in_stages([kv_head(h) for h in range(N_KV)])

        @pl.when((b == b_loc - 1) & (i == nb - 1))
        def _():
            _gather_wait(sh_in, sh_out, sems)

    cur = lambda b, i: b * nb + i
    prev = lambda b, i: b * nb + jnp.maximum(i - 1, 0)
    wide = lambda cb: pl.BlockSpec((BLK, ATTN_WIDTH), lambda b, i: (cur(b, i), cb))
    kv = lambda rowf, cb: pl.BlockSpec((BLK, LANES), lambda b, i: (rowf(b, i), cb))
    hbm = pl.BlockSpec(memory_space=pl.ANY)
    in_specs = [pl.BlockSpec(memory_space=pltpu.SMEM), wide(0), wide(1),
                kv(cur, _KCOL // LANES), kv(prev, _KCOL // LANES), kv(cur, _VCOL // LANES), kv(prev, _VCOL // LANES),
                kv(cur, 0), kv(cur, 0), kv(prev, 0), kv(prev, 0)] + [hbm] * nsh
    return pl.pallas_call(
        body, name="attn_fwd", grid=(b_loc, nb), in_specs=in_specs, out_specs=[wide(0), wide(0)] + [hbm] * nsh,
        out_shape=[jax.ShapeDtypeStruct((n, ATTN_WIDTH), F32), jax.ShapeDtypeStruct((n, ATTN_WIDTH), BF16)] + _gather_shapes(shards),
        scratch_shapes=_gather_sems(nsh), compiler_params=_params(("arbitrary", "arbitrary"), 48),
    )(sinks, p, p, p, p, p, p, ct, st, ct, st, *shards)


def _attn_bwd(p, ct, st, sinks, o, dg, b_loc, nb, parts):
    n = p.shape[0]
    nparts = len(parts)

    def body(sink_ref, q_ref, z_ref, kc_ref, kp_ref, vc_ref, vp_ref, cc_ref, sc_ref, cp_ref, sp_ref, o_ref, dg_ref, *rest):
        part_refs, (dp_ref, ds_ref), land_refs = rest[:nparts], rest[nparts:nparts + 2], rest[nparts + 2:2 * nparts + 2]
        dq_s, dz_s, dk_s, dv_s = rest[2 * nparts + 2:2 * nparts + 6]
        sems = rest[2 * nparts + 6:]
        b, i = pl.program_id(0), pl.program_id(1)

        @pl.when((b == 0) & (i == 0))
        def _():
            _scatter_start(part_refs, land_refs, sems)

        @pl.when((b == b_loc - 1) & (i == nb))
        def _():
            _scatter_wait(part_refs, land_refs, sems)

        lane = lax.broadcasted_iota(jnp.int32, (BLK, LANES), 1)
        lm = lane & (HEAD_DIM - 1)

        @pl.when((b == 0) & (i == 0))
        def _():
            ds_ref[...] = jnp.zeros_like(ds_ref)

        @pl.when(i < nb)
        def _compute():
            cc, sc = cc_ref[...], sc_ref[...]
            kcat = jnp.concatenate([_rope_apply(kp_ref[...], cp_ref[...], sp_ref[...], lm),
                                    _rope_apply(kc_ref[...], cc, sc, lm)], axis=0)
            vcat = jnp.concatenate([vp_ref[...], vc_ref[...]], axis=0)
            qr = [_rope_apply(q_ref[:, c * LANES:(c + 1) * LANES], cc, sc, lm) * (HEAD_DIM ** -0.5) for c in range(8)]
            valid = _attn_mask_t(i)
            lt64, lt64k = lane < HEAD_DIM, lax.broadcasted_iota(jnp.int32, (2 * BLK, LANES), 1) < HEAD_DIM
            do_chunks, doo_chunks, dz_chunks = [], [], []
            for c in range(8):
                cols = slice(c * LANES, (c + 1) * LANES)
                zc, oc, dgc = z_ref[:, cols], o_ref[:, cols], dg_ref[:, cols]
                sg = _sigmoid(zc)
                do_chunks.append(dgc * (zc * sg))
                dz_chunks.append(dgc * oc * (sg * (1.0 + zc * (1.0 - sg))))
                doo_chunks.append(do_chunks[c] * oc)
            dq_chunks = [None] * 8
            dk_h, dv_h, ds_parts = [None] * N_KV, [None] * N_KV, [None] * N_KV
            tile_lane = lax.broadcasted_iota(jnp.int32, (SUBLANES, LANES), 1)
            tile_row = lax.broadcasted_iota(jnp.int32, (SUBLANES, LANES), 0)
            ones8 = jnp.ones((SUBLANES, LANES), BF16)

            def kv_head(h):
                heads = list(range(h * GROUP, (h + 1) * GROUP))
                kext = _dup_half(kcat, h, lt64k)
                kext_bf, kext_t = kext.astype(BF16), kext.T.astype(BF16)
                vext = _dup_half(vcat, h, lt64k).astype(BF16)
                qst = _head_stack(qr, heads, lt64).astype(BF16)
                pn, psink = _attn_probs_t(lax.dot_general(kext_bf, qst, NT_DIMS, preferred_element_type=F32), heads, sink_ref, valid)
                do_bf = _head_stack(do_chunks, heads, lt64).astype(BF16)
                delta = sum(lax.dot_general(ones8, part, NT_DIMS, preferred_element_type=F32)
                            for part in _split3(_head_stack(doo_chunks, heads, lt64)))[0:1, :]
                dpt = lax.dot_general(vext, do_bf, NT_DIMS, preferred_element_type=F32)
                dst = (pn * (dpt - delta)).astype(BF16)
                sink_term = psink * delta
                ds_acc = jnp.zeros((SUBLANES, LANES), F32)
                for j, n in enumerate(heads):
                    val = -jnp.sum(sink_term[:, j * BLK:(j + 1) * BLK])
                    ds_acc = ds_acc + jnp.where((tile_lane == n) & (tile_row == 0), val, 0.0)
                ds_parts[h] = ds_acc
                dqt = jnp.dot(kext_t, dst, preferred_element_type=F32) * (HEAD_DIM ** -0.5)
                dk_ext = jnp.dot(dst, qst, preferred_element_type=F32)
                dv_ext = jnp.dot(pn.astype(BF16), do_bf, preferred_element_type=F32)
                dk_h[h] = dk_ext + pltpu.roll(dk_ext, HEAD_DIM, 1)
                dv_h[h] = dv_ext + pltpu.roll(dv_ext, HEAD_DIM, 1)
                for c2 in range(GROUP // 2):
                    dq_chunks[4 * h + c2] = _rope_bwd(_pair_chunk(dqt, c2), cc, sc, lm)

            for h in range(N_KV):
                kv_head(h)
            ds_ref[...] += ds_parts[0] + ds_parts[1]
            dk_full = jnp.where(lt64k, dk_h[0], dk_h[1])
            dv_full = jnp.where(lt64k, dv_h[0], dv_h[1])

            @pl.when(i >= 1)
            def _emit():
                dp_ref[:, _QCOL:_QCOL + ATTN_WIDTH] = dq_s[...]
                dp_ref[:, _ZCOL:_ZCOL + ATTN_WIDTH] = dz_s[...]
                dp_ref[:, _KCOL:_KCOL + KV_WIDTH] = _rope_bwd(dk_s[...] + dk_full[:BLK], cp_ref[...], sp_ref[...], lm)
                dp_ref[:, _VCOL:_VCOL + KV_WIDTH] = dv_s[...] + dv_full[:BLK]

            for c in range(8):
                dq_s[:, c * LANES:(c + 1) * LANES] = dq_chunks[c]
                dz_s[:, c * LANES:(c + 1) * LANES] = dz_chunks[c]
            dk_s[...] = dk_full[BLK:]
            dv_s[...] = dv_full[BLK:]

        @pl.when(i == nb)
        def _final():
            dp_ref[:, _QCOL:_QCOL + ATTN_WIDTH] = dq_s[...]
            dp_ref[:, _ZCOL:_ZCOL + ATTN_WIDTH] = dz_s[...]
            dp_ref[:, _KCOL:_KCOL + KV_WIDTH] = _rope_bwd(dk_s[...], cc_ref[...], sc_ref[...], lm)
            dp_ref[:, _VCOL:_VCOL + KV_WIDTH] = dv_s[...]

    cur = lambda b, i: b * nb + jnp.minimum(i, nb - 1)
    prev = lambda b, i: b * nb + jnp.maximum(jnp.minimum(i, nb - 1) - 1, 0)
    emit = lambda b, i: b * nb + jnp.maximum(i - 1, 0)
    hbm = pl.BlockSpec(memory_space=pl.ANY)
    wide = lambda cb: pl.BlockSpec((BLK, ATTN_WIDTH), lambda b, i: (cur(b, i), cb))
    kv = lambda rowf, cb: pl.BlockSpec((BLK, LANES), lambda b, i: (rowf(b, i), cb))
    in_specs = [pl.BlockSpec(memory_space=pltpu.SMEM), wide(0), wide(1),
                kv(cur, _KCOL // LANES), kv(prev, _KCOL // LANES), kv(cur, _VCOL // LANES), kv(prev, _VCOL // LANES),
                kv(cur, 0), kv(cur, 0), kv(prev, 0), kv(prev, 0), wide(0), wide(0)] + [hbm] * nparts
    out_specs = [pl.BlockSpec((BLK, ATTN_IN), lambda b, i: (emit(b, i), 0)),
                 pl.BlockSpec((SUBLANES, LANES), lambda b, i: (0, 0))] + [hbm] * nparts
    return pl.pallas_call(
        body, name="attn_bwd", grid=(b_loc, nb + 1), in_specs=in_specs, out_specs=out_specs,
        out_shape=[jax.ShapeDtypeStruct((n, ATTN_IN), F32), jax.ShapeDtypeStruct((SUBLANES, LANES), F32)] + _scatter_lands(parts),
        scratch_shapes=[pltpu.VMEM((BLK, ATTN_WIDTH), F32), pltpu.VMEM((BLK, ATTN_WIDTH), F32),
                        pltpu.VMEM((BLK, KV_WIDTH), F32), pltpu.VMEM((BLK, KV_WIDTH), F32)] + _scatter_sems(nparts),
        compiler_params=_params(("arbitrary", "arbitrary"), 48),
    )(sinks, p, p, p, p, p, p, ct, st, ct, st, o, dg, *parts)


_CUM_ROWS = 256
HALF = CH // 2
_ROW0 = [SUBLANES * (s // SUBLANES) for s in range(CH)]
_ROW1 = [HALF * (s // HALF + 1) for s in range(CH)]
_ROWS_OF = [_ROW1[s] - _ROW0[s] for s in range(CH)]
_OFF_OF = [sum(_ROWS_OF[:s]) for s in range(CH)]


def _tri(lower):
    r = lax.broadcasted_iota(jnp.int32, (_CUM_ROWS, _CUM_ROWS), 0)
    c = lax.broadcasted_iota(jnp.int32, (_CUM_ROWS, _CUM_ROWS), 1)
    same = (r ^ c) < CH
    return (same & ((c <= r) if lower else (c >= r))).astype(BF16)


def _gates(qp, fp, lb):
    e = jnp.exp(-jnp.abs(fp))
    r = 1.0 / (1.0 + e)
    sig_neg = jnp.where(fp >= 0, e, 1.0) * r
    sig = jnp.where(fp >= 0, 1.0, e) * r
    g = jnp.log(jnp.maximum(lb + (1.0 - lb) * sig, F32_TINY))
    return qp * _sigmoid(qp), g, (1.0 - lb) * sig_neg, sig_neg


def _pair_rows(bc, s):
    diff = bc[_ROW0[s]:_ROW1[s], :] - bc[s:s + 1, :]
    head = jnp.minimum(diff[:SUBLANES, :], 0.0)
    return jnp.exp(head if diff.shape[0] == SUBLANES else jnp.concatenate([head, diff[SUBLANES:, :]], axis=0))


def _cross_half(q, k, bc):
    r = bc[HALF - 1:HALF, :]
    e1, e0 = jnp.exp(bc[HALF:, :] - r), jnp.exp(r - bc[:HALF, :])
    return q[HALF:, :] * e1, e1, k[:HALF, :] * e0, e0


HP = 8
REC_TB = 256
_HW = HP * REC_K


def _hgrn_specs(tb, nt, reverse):
    tmap = (lambda t: nt - 1 - t) if reverse else (lambda t: t)
    groups = REC_HEADS // HP
    blk = lambda cb: pl.BlockSpec((tb, _HW), lambda h, b, t: (b * nt + tmap(t), cb * groups + h))
    head = pl.BlockSpec((tb, _HW), lambda h, b, t: (b * nt + tmap(t), h))
    lbs = pl.BlockSpec((1, _HW), lambda h, b, t: (0, h))
    gws = pl.BlockSpec((1, REC_K), lambda h, b, t: (0, 0))
    hist = pl.BlockSpec((HP, 1, tb // CH, REC_K, REC_K), lambda h, b, t: (h, b, tmap(t), 0, 0))
    return blk, head, lbs, gws, hist


def _chunk_rows(c, first=0, size=CH):
    start = c * CH + first
    return pl.ds(start if isinstance(start, int) else pl.multiple_of(start, CH if first % CH == 0 else SUBLANES), size)


def _in_stages(heads):
    live = list(heads)
    while live:
        live = [g for g in live if next(g, live) is not live]


def _cumsum_chunks(tri, x, out_ref, tb):
    for r in range(0, tb, _CUM_ROWS):
        out_ref[r:r + _CUM_ROWS, :] = _dot3(tri, x[r:r + _CUM_ROWS, :])


def _hgrn_fwd(p, lb, gw, b_loc, t_len):
    n = p.shape[0]
    tb = min(REC_TB, t_len)
    nt, nck = t_len // tb, tb // CH

    def body(qp_ref, fp_ref, i_ref, z_ref, lb_ref, gw_ref, oraw_ref, g_ref, sh_ref, q_s, k_s, b_s, st_ref,
             car_o, car_a, car_s, car_st):
        @pl.when(pl.program_id(2) == 0)
        def _():
            st_ref[...] = jnp.zeros_like(st_ref)

        qv, g, kk, _ = _gates(qp_ref[...], fp_ref[...], lb_ref[...])
        q_s[...] = qv
        k_s[...] = kk
        _cumsum_chunks(_tri(True), g, b_s, tb)
        ones = jnp.ones((REC_K, REC_K), BF16)
        sub = lax.broadcasted_iota(jnp.int32, (SUBLANES, REC_K), 0)

        rows_of = _chunk_rows

        def issue(c, hp):
            rs, cs = rows_of(c), slice(hp * REC_K, (hp + 1) * REC_K)
            q, k, bc, v = q_s[rs, cs], k_s[rs, cs], b_s[rs, cs], i_ref[rs, cs]
            st = st_ref[hp]
            sh_ref[hp, 0, c] = st
            o = lax.dot_general((q * jnp.exp(bc)).astype(BF16), st.astype(BF16), NT_DIMS, preferred_element_type=F32)
            w = jnp.concatenate([q[_ROW0[s]:_ROW1[s], :] * _pair_rows(bc, s) * k[s:s + 1, :] for s in range(CH)], axis=0)
            a = jnp.dot(w.astype(BF16), ones, preferred_element_type=F32)
            qe1, _, ke0, _ = _cross_half(q, k, bc)
            s10 = lax.dot_general(qe1.astype(BF16), ke0.astype(BF16), NT_DIMS, preferred_element_type=F32)
            kd = k * jnp.exp(bc[CH - 1:CH, :] - bc)
            st_new = lax.dot_general(v.astype(BF16), kd.astype(BF16), TN_DIMS, preferred_element_type=F32)
            return o, a, s10, st_new

        def advance_state(c, hp, st_new):
            bl = b_s[_chunk_rows(c, CH - SUBLANES, SUBLANES), hp * REC_K:(hp + 1) * REC_K][SUBLANES - 1:, :]
            st_ref[hp] = st_ref[hp] * jnp.exp(bl) + st_new

        def cross(c, hp, s10):
            v0 = i_ref[_chunk_rows(c, 0, HALF), hp * REC_K:(hp + 1) * REC_K]
            return jnp.dot(s10.astype(BF16), v0.astype(BF16), preferred_element_type=F32)

        def finish(c, hp, o, a, o_cross):
            rs, cs = rows_of(c), slice(hp * REC_K, (hp + 1) * REC_K)
            v = i_ref[rs, cs]
            acc = [jnp.zeros((SUBLANES, REC_K), F32) for _ in range(CH // SUBLANES)]
            for s in range(CH):
                j = s // SUBLANES
                vs = v[s:s + 1, :]
                for jj in range(j, _ROW1[s] // SUBLANES):
                    blk = a[_OFF_OF[s] + (jj - j) * SUBLANES:_OFF_OF[s] + (jj - j + 1) * SUBLANES, :]
                    if jj == j:
                        blk = jnp.where(sub >= s - j * SUBLANES, blk, 0.0)
                    acc[jj] = acc[jj] + blk * vs
            oraw_ref[rs, cs] = o + jnp.concatenate(acc, axis=0) + jnp.concatenate([jnp.zeros((HALF, REC_K), F32), o_cross], axis=0)

        def park(slot, results):
            for hp, (o, a, s10, st_new) in enumerate(results):
                car_o[slot, hp], car_a[slot, hp], car_s[slot, hp], car_st[slot, hp] = o, a, s10, st_new

        def retire(c, slot):
            for hp in range(HP):
                advance_state(c, hp, car_st[slot, hp])
            yield
            crosses = [cross(c, hp, car_s[slot, hp]) for hp in range(HP)]
            for hp in range(HP):
                finish(c, hp, car_o[slot, hp], car_a[slot, hp], crosses[hp])

        def step(c, slot):
            closing = retire(c - 1, slot)
            next(closing)
            park(1 - slot, [issue(c, hp) for hp in range(HP)])
            next(closing, None)

        def trip(j, carry):
            step(2 * j + 1, 0)
            step(2 * j + 2, 1)
            return carry

        assert nck % 2 == 0
        park(0, [issue(0, hp) for hp in range(HP)])
        lax.fori_loop(0, nck // 2 - 1, trip, 0)
        step(nck - 1, 0)
        for _ in retire(nck - 1, 1):
            pass
        for hp in range(HP):
            cs = slice(hp * REC_K, (hp + 1) * REC_K)
            o, zc = oraw_ref[:, cs], z_ref[:, cs]
            on = (o * lax.rsqrt(jnp.mean(o * o, axis=-1, keepdims=True) + NORM_EPS)) * gw_ref[...]
            g_ref[:, cs] = (on * (zc * _sigmoid(zc))).astype(BF16)

    blk, head, lbs, gws, hist = _hgrn_specs(tb, nt, False)
    return pl.pallas_call(
        body, name="hgrn_fwd", grid=(REC_HEADS // HP, b_loc, nt),
        in_specs=[blk(0), blk(1), blk(2), blk(3), lbs, gws], out_specs=[head, head, hist],
        out_shape=[jax.ShapeDtypeStruct((n, REC_WIDTH), F32), jax.ShapeDtypeStruct((n, REC_WIDTH), BF16),
                   jax.ShapeDtypeStruct((REC_HEADS, b_loc, t_len // CH, REC_K, REC_K), F32)],
        scratch_shapes=[pltpu.VMEM((tb, _HW), F32)] * 3 + [pltpu.VMEM((HP, REC_K, REC_K), F32)] + [
            pltpu.VMEM((2, HP, CH, REC_K), F32), pltpu.VMEM((2, HP, sum(_ROWS_OF), REC_K), F32),
            pltpu.VMEM((2, HP, HALF, HALF), F32), pltpu.VMEM((2, HP, REC_K, REC_K), F32)],
        compiler_params=_params(("arbitrary", "arbitrary", "arbitrary"), 48),
    )(p, p, p, p, lb, gw)


def _hgrn_bwd(p, lb, gw, oraw, sh, dg, b_loc, t_len):
    n = p.shape[0]
    tb = min(REC_TB, t_len)
    nt, nck = t_len // tb, tb // CH
    assert HP == REC_HEADS

    def body(qp_ref, fp_ref, i_ref, z_ref, lb_ref, gw_ref, oraw_ref, dg_ref, sh_ref,
             dp_ref, dlb_ref, dgw_ref,
             q_s, k_s, b_s, do_s, dqv_s, dk_s, db_s, dst_ref, car_r, car_dst, car_dec, car_a, car_da, car_x):
        dq_ref, df_ref, di_ref, dz_ref = (dp_ref.at[:, part * REC_WIDTH:(part + 1) * REC_WIDTH] for part in range(4))
        b, t = pl.program_id(1), pl.program_id(2)

        @pl.when(t == 0)
        def _():
            dst_ref[...] = jnp.zeros_like(dst_ref)

        @pl.when((b == 0) & (t == 0))
        def _():
            dlb_ref[...] = jnp.zeros_like(dlb_ref)
            dgw_ref[...] = jnp.zeros_like(dgw_ref)

        lbv, qp, fp = lb_ref[...], qp_ref[...], fp_ref[...]
        qv, g, kk, sig_neg = _gates(qp, fp, lbv)
        q_s[...] = qv
        k_s[...] = kk
        _cumsum_chunks(_tri(True), g, b_s, tb)
        gwv = gw_ref[...]
        for hp in range(HP):
            cs = slice(hp * REC_K, (hp + 1) * REC_K)
            o, zc, dgv = oraw_ref[:, cs], z_ref[:, cs], dg_ref[:, cs]
            rn = lax.rsqrt(jnp.mean(o * o, axis=-1, keepdims=True) + NORM_EPS)
            on = o * rn
            sgz = _sigmoid(zc)
            dz_ref[:, cs] = (dgv * (on * gwv) * (sgz * (1.0 + zc * (1.0 - sgz)))).astype(BF16)
            dpre = dgv * (zc * sgz)
            dgw_ref[hp] += jnp.sum(dpre * on, axis=0, keepdims=True)
            don = dpre * gwv
            do_s[:, cs] = rn * (don - on * jnp.mean(don * on, axis=-1, keepdims=True))

        ones = jnp.ones((REC_K, REC_K), BF16)
        sub = lax.broadcasted_iota(jnp.int32, (SUBLANES, REC_K), 0)
        rowid = lax.broadcasted_iota(jnp.int32, (CH, REC_K), 0)
        ngrp = CH // SUBLANES
        piece_row = lax.broadcasted_iota(jnp.int32, (1, sum(_ROWS_OF)), 1)
        key_of = jnp.zeros((1, sum(_ROWS_OF)), jnp.int32)
        for s in range(1, CH):
            key_of = jnp.where(piece_row >= _OFF_OF[s], s, key_of)
        pick = (key_of == lax.broadcasted_iota(jnp.int32, (CH, sum(_ROWS_OF)), 0)).astype(BF16)

        def operands(c, hp):
            rs, cs = _chunk_rows(c), slice(hp * REC_K, (hp + 1) * REC_K)
            return rs, cs, q_s[rs, cs], k_s[rs, cs], b_s[rs, cs], i_ref[rs, cs], do_s[rs, cs]

        def issue(c, hp, slot):
            _, _, q, k, bc, v, do = operands(c, hp)
            st, dst = sh_ref[hp, 0, c], dst_ref[hp]
            qe, kd = q * jnp.exp(bc), k * jnp.exp(bc[CH - 1:CH, :] - bc)
            do_bf, dst_bf = do.astype(BF16), dst.astype(BF16)
            car_r[slot, hp, 0:CH] = jnp.dot(do_bf, st.astype(BF16), preferred_element_type=F32)
            car_r[slot, hp, CH:2 * CH] = jnp.dot(v.astype(BF16), dst_bf, preferred_element_type=F32)
            car_r[slot, hp, 2 * CH:3 * CH] = lax.dot_general(kd.astype(BF16), dst_bf, NT_DIMS, preferred_element_type=F32)
            car_dst[slot, hp] = lax.dot_general(do_bf, qe.astype(BF16), TN_DIMS, preferred_element_type=F32)
            dec = jnp.concatenate([_pair_rows(bc, s) for s in range(CH)], axis=0)
            qk = jnp.concatenate([q[_ROW0[s]:_ROW1[s], :] * k[s:s + 1, :] for s in range(CH)], axis=0)
            x = jnp.concatenate([do[_ROW0[s]:_ROW1[s], :] * v[s:s + 1, :] for s in range(CH)], axis=0)
            car_dec[slot, hp] = dec
            car_a[slot, hp] = jnp.dot((qk * dec).astype(BF16), ones, preferred_element_type=F32)
            car_da[slot, hp] = jnp.dot(x.astype(BF16), ones, preferred_element_type=F32)
            qe1, _, ke0, _ = _cross_half(q, k, bc)
            qe1_bf, ke0_bf = qe1.astype(BF16), ke0.astype(BF16)
            do1_bf, v0_bf = do[HALF:, :].astype(BF16), v[:HALF, :].astype(BF16)
            car_x[slot, hp, 0:HALF] = lax.dot_general(ke0_bf, qe1_bf, NT_DIMS, preferred_element_type=F32)
            car_x[slot, hp, HALF:2 * HALF] = lax.dot_general(do1_bf, v0_bf, NT_DIMS, preferred_element_type=F32)
            car_x[slot, hp, 2 * HALF:3 * HALF] = lax.dot_general(v0_bf, do1_bf, NT_DIMS, preferred_element_type=F32)

        def advance_state(c, hp, slot):
            ebl = jnp.exp(b_s[_chunk_rows(c, CH - SUBLANES, SUBLANES), hp * REC_K:(hp + 1) * REC_K][SUBLANES - 1:, :])
            st, dst = sh_ref[hp, 0, c], dst_ref[hp]
            dst_ref[hp] = dst * ebl + car_dst[slot, hp]
            return ebl * jnp.sum(st * dst, axis=0, keepdims=True)

        def cross(c, hp, slot):
            _, _, q, k, bc, v, do = operands(c, hp)
            qe1, _, ke0, _ = _cross_half(q, k, bc)
            xs = car_x[slot, hp]
            dqe1 = jnp.dot(xs[HALF:2 * HALF].astype(BF16), ke0.astype(BF16), preferred_element_type=F32)
            dke0 = jnp.dot(xs[2 * HALF:].astype(BF16), qe1.astype(BF16), preferred_element_type=F32)
            dv1 = jnp.dot(xs[:HALF].astype(BF16), do[HALF:, :].astype(BF16), preferred_element_type=F32)
            return dqe1, dke0, dv1

        def retire(c, slot):
            dbl_state = [advance_state(c, hp, slot) for hp in range(HP)]
            yield
            crossed = [cross(c, hp, slot) for hp in range(HP)]
            for hp in range(HP):
                finish(c, hp, slot, dbl_state[hp], *crossed[hp])

        def step(c, slot):
            closing = retire(c + 1, slot)
            next(closing)
            for hp in range(HP):
                issue(c, hp, 1 - slot)
            next(closing, None)

        def trip(j, carry):
            step(nck - 2 - 2 * j, 0)
            step(nck - 3 - 2 * j, 1)
            return carry

        def finish(c, hp, slot, dbl_state, dqe1, dke0, dv1):
            rs, cs, q, k, bc, v, do = operands(c, hp)
            eb, ekd = jnp.exp(bc), jnp.exp(bc[CH - 1:CH, :] - bc)
            qe, kd = q * eb, k * ekd
            qe1, e1, ke0, e0 = _cross_half(q, k, bc)
            dqe, dkd, dv = car_r[slot, hp, 0:CH], car_r[slot, hp, CH:2 * CH], car_r[slot, hp, 2 * CH:3 * CH]
            a, da, decs = car_a[slot, hp], car_da[slot, hp], car_dec[slot, hp]
            dec = [decs[_OFF_OF[s]:_OFF_OF[s] + _ROWS_OF[s], :] for s in range(CH)]
            dbl = jnp.sum(dkd * kd, axis=0, keepdims=True) + dbl_state
            dq_acc = [jnp.zeros((SUBLANES, REC_K), F32) for _ in range(ngrp)]
            uk, uv = [], []
            for s in range(CH):
                j = s // SUBLANES
                r0 = j * SUBLANES
                ks = k[s:s + 1, :]
                for jj in range(j, _ROW1[s] // SUBLANES):
                    lo, hi = _OFF_OF[s] + (jj - j) * SUBLANES, _OFF_OF[s] + (jj - j + 1) * SUBLANES
                    a_blk, da_blk = a[lo:hi, :], da[lo:hi, :]
                    if jj == j:
                        keep = sub >= s - r0
                        a_blk, da_blk = jnp.where(keep, a_blk, 0.0), jnp.where(keep, da_blk, 0.0)
                    rows = slice(jj * SUBLANES, (jj + 1) * SUBLANES)
                    tt = da_blk * dec[s][(jj - j) * SUBLANES:(jj - j + 1) * SUBLANES, :]
                    dq_acc[jj] = dq_acc[jj] + tt * ks
                    uk.append(tt * q[rows, :])
                    uv.append(a_blk * do[rows, :])
            dk_in = jnp.dot(pick, jnp.concatenate(uk, axis=0).astype(BF16), preferred_element_type=F32)
            dv_in = jnp.dot(pick, jnp.concatenate(uv, axis=0).astype(BF16), preferred_element_type=F32)
            zero_half = jnp.zeros((HALF, REC_K), F32)
            dq_x = jnp.concatenate([zero_half, dqe1 * e1], axis=0)
            dk_x = jnp.concatenate([dke0 * e0, zero_half], axis=0)
            dv_x = jnp.concatenate([dv1, zero_half], axis=0)
            db_x = jnp.concatenate([-(dke0 * ke0), dqe1 * qe1], axis=0)
            dq_in = jnp.concatenate(dq_acc, axis=0)
            dqv_s[rs, cs] = dqe * eb + dq_in + dq_x
            dk_s[rs, cs] = dkd * ekd + dk_in + dk_x
            di_ref[rs, cs] = (dv + dv_in + dv_x).astype(BF16)
            db = dqe * qe - dkd * kd + q * dq_in - k * dk_in + db_x
            db_s[rs, cs] = db + jnp.where(rowid == CH - 1, dbl, 0.0)

        assert nck % 2 == 0
        for hp in range(HP):
            issue(nck - 1, hp, 0)
        lax.fori_loop(0, nck // 2 - 1, trip, 0)
        step(0, 0)
        for _ in retire(0, 1):
            pass
        up = _tri(False)
        sgq = _sigmoid(qp)
        dq_ref[...] = (dqv_s[...] * (sgq * (1.0 + qp * (1.0 - sgq)))).astype(BF16)
        dlb_acc = jnp.zeros((1, _HW), F32)
        for r in range(0, tb, _CUM_ROWS):
            rows = slice(r, r + _CUM_ROWS)
            dgl = _dot3(up, db_s[rows, :])
            dfg = dgl * jnp.exp(-g[rows, :]) - dk_s[rows, :]
            sn = sig_neg[rows, :]
            df_ref[rows, :] = (dfg * (1.0 - lbv) * (1.0 - sn) * sn).astype(BF16)
            dlb_acc = dlb_acc + jnp.sum(dfg * sn, axis=0, keepdims=True)
        dlb_ref[...] += dlb_acc

    blk, head, lbs, gws, hist = _hgrn_specs(tb, nt, True)
    out_specs = [pl.BlockSpec((tb, REC_IN), lambda h, b, t: (b * nt + nt - 1 - t, 0)), lbs,
                 pl.BlockSpec((HP, 1, REC_K), lambda h, b, t: (h, 0, 0))]
    out_shape = [jax.ShapeDtypeStruct((n, REC_IN), BF16),
                 jax.ShapeDtypeStruct((1, REC_WIDTH), F32), jax.ShapeDtypeStruct((REC_HEADS, 1, REC_K), F32)]
    return pl.pallas_call(
        body, name="hgrn_bwd", grid=(REC_HEADS // HP, b_loc, nt),
        in_specs=[blk(0), blk(1), blk(2), blk(3), lbs, gws, head, head, hist],
        out_specs=out_specs, out_shape=out_shape,
        scratch_shapes=[pltpu.VMEM((tb, _HW), F32)] * 7 + [pltpu.VMEM((HP, REC_K, REC_K), F32)] + [
            pltpu.VMEM((2, HP, 3 * CH, REC_K), F32), pltpu.VMEM((2, HP, REC_K, REC_K), F32)] + [
            pltpu.VMEM((2, HP, sum(_ROWS_OF), REC_K), F32)] * 3 + [pltpu.VMEM((2, HP, 3 * HALF, HALF), F32)],
        compiler_params=_params(("arbitrary", "arbitrary", "arbitrary"), 56),
    )(p, p, p, p, lb, gw, oraw, dg, sh)


def _postnorm_bwd_nt(dxo, y, qw, w, name):
    n = dxo.shape[0]

    def body(dx_ref, y_ref, qw_ref, w_ref, dg_ref, dy_ref, dqw_ref, db_ref):
        @pl.when(pl.program_id(0) == 0)
        def _():
            dqw_ref[...] = jnp.zeros_like(dqw_ref)
            db_ref[...] = jnp.zeros_like(db_ref)

        yv, dxv = y_ref[...], dx_ref[...]
        r = lax.rsqrt(jnp.mean(yv * yv, axis=-1, keepdims=True) + NORM_EPS)
        u = yv * r
        du = dxv * qw_ref[...]
        dy = r * (du - u * jnp.mean(du * u, axis=-1, keepdims=True))
        dqw_ref[...] += jnp.sum(dxv * u, axis=0, keepdims=True)
        db_ref[...] += jnp.sum(dy, axis=0, keepdims=True)
        dyb = dy.astype(BF16)
        dy_ref[...] = dyb
        dg_ref[...] = lax.dot_general(dyb, w_ref[...], NT_DIMS, preferred_element_type=F32)

    tm = _wide_rows(n)
    rows = pl.BlockSpec((tm, D_MODEL), lambda i: (i, 0))
    const = lambda shape: pl.BlockSpec(shape, lambda i: (0, 0))
    return pl.pallas_call(
        body, name=name, grid=(n // tm,), in_specs=[rows, rows, const((1, D_MODEL)), const((D_MODEL, D_MODEL))],
        out_specs=[rows, rows, const((1, D_MODEL)), const((1, D_MODEL))],
        out_shape=[jax.ShapeDtypeStruct((n, D_MODEL), F32), jax.ShapeDtypeStruct((n, D_MODEL), BF16),
                   jax.ShapeDtypeStruct((1, D_MODEL), F32), jax.ShapeDtypeStruct((1, D_MODEL), F32)],
        compiler_params=_params(("arbitrary",), 48),
    )(dxo, y, qw, w)


def _nt_prenorm_bwd(dps, w, x, pw, dxo, has_bias, name, parts=()):
    n = x.shape[0]
    widths = [d.shape[1] for d in dps]
    m = sum(widths)
    npieces, nparts, steps = len(dps), len(parts), n // TM

    def body(*refs):
        dp_refs = refs[:npieces]
        w_ref, x_ref, pw_ref, dxo_ref = refs[npieces:npieces + 4]
        part_refs = refs[npieces + 4:npieces + 4 + nparts]
        dx_ref, dpw_ref, db_ref = refs[npieces + 4 + nparts:npieces + 7 + nparts]
        land_refs = refs[npieces + 7 + nparts:npieces + 7 + 2 * nparts]
        sems = refs[npieces + 7 + 2 * nparts:]

        @pl.when(pl.program_id(0) == 0)
        def _():
            dpw_ref[...] = jnp.zeros_like(dpw_ref)
            db_ref[...] = jnp.zeros_like(db_ref)
            if nparts:
                _scatter_start(part_refs, land_refs, sems)

        dh = jnp.zeros((TM, D_MODEL), F32)
        off = 0
        for dp_ref, wd in zip(dp_refs, widths):
            cn = _col_chunk(wd)
            for j in range(0, wd, cn):
                dpc = dp_ref[:, j:j + cn]
                if has_bias:
                    db_ref[:, off + j:off + j + cn] += jnp.sum(dpc, axis=0, keepdims=True)
                dh = dh + lax.dot_general(dpc.astype(BF16), w_ref[:, off + j:off + j + cn], NT_DIMS, preferred_element_type=F32)
            off += wd
        xv = x_ref[...]
        r = lax.rsqrt(jnp.mean(xv * xv, axis=-1, keepdims=True) + NORM_EPS)
        xn = xv * r
        dpw_ref[...] += jnp.sum(dh * xn, axis=0, keepdims=True)
        dxn = dh * pw_ref[...]
        dx_ref[...] = dxo_ref[...] + r * (dxn - xn * jnp.mean(dxn * xn, axis=-1, keepdims=True))

        if nparts:
            @pl.when(pl.program_id(0) == steps - 1)
            def _():
                _scatter_wait(part_refs, land_refs, sems)

    rows = pl.BlockSpec((TM, D_MODEL), lambda i: (i, 0))
    const = lambda shape: pl.BlockSpec(shape, lambda i: (0, 0))
    hbm = pl.BlockSpec(memory_space=pl.ANY)
    in_specs = ([pl.BlockSpec((TM, wd), lambda i: (i, 0)) for wd in widths] + [const((D_MODEL, m)), rows, const((1, D_MODEL)), rows]
                + [hbm] * nparts)
    return pl.pallas_call(
        body, name=name, grid=(steps,), in_specs=in_specs,
        out_specs=[rows, const((1, D_MODEL)), const((1, m))] + [hbm] * nparts,
        out_shape=[jax.ShapeDtypeStruct((n, D_MODEL), F32), jax.ShapeDtypeStruct((1, D_MODEL), F32),
                   jax.ShapeDtypeStruct((1, m), F32)] + _scatter_lands(parts),
        scratch_shapes=_scatter_sems(nparts) if nparts else [],
        compiler_params=_params(("arbitrary",), 56),
    )(*dps, w, x, pw, dxo, *parts)


def _matmul_tn(a, b, name):
    n, k = a.shape
    m = b.shape[1]
    tk, tm, tn = k, _col_chunk(m), TN_ROWS if n % TN_ROWS == 0 else n

    def body(a_ref, b_ref, o_ref):
        @pl.when(pl.program_id(2) == 0)
        def _():
            o_ref[...] = jnp.zeros_like(o_ref)

        o_ref[...] += lax.dot_general(a_ref[...], b_ref[...].astype(BF16), TN_DIMS, preferred_element_type=F32)

    return pl.pallas_call(
        body, name=name, grid=(k // tk, m // tm, n // tn),
        in_specs=[pl.BlockSpec((tn, tk), lambda i, j, l: (l, i)), pl.BlockSpec((tn, tm), lambda i, j, l: (l, j))],
        out_specs=pl.BlockSpec((tk, tm), lambda i, j, l: (i, j)),
        out_shape=jax.ShapeDtypeStruct((k, m), F32),
        compiler_params=_params(("arbitrary", "arbitrary", "arbitrary"), 48),
    )(a, b)


def _matmul_tn_by_owner(a, b, name):
    n, k = a.shape
    c = b.shape[1] // N_DEV
    tn = TN_ROWS if n % TN_ROWS == 0 else n
    steps = n // tn
    per = 2

    def body(a_ref, b_ref, o_ref, w_ref):
        @pl.when(pl.program_id(1) == 0)
        def _():
            o_ref[...] = jnp.zeros_like(o_ref)

        r = lax.dot_general(a_ref[...], b_ref[...], TN_DIMS, preferred_element_type=F32)
        for j in range(per):
            o_ref[j] += r[:, j * c:(j + 1) * c]

        @pl.when(pl.program_id(1) == steps - 1)
        def _():
            w_ref[...] = o_ref[...].astype(BF16)

    out = pl.BlockSpec((per, k, c), lambda j, l: (j, 0, 0))
    return pl.pallas_call(
        body, name=name, grid=(N_DEV // per, steps),
        in_specs=[pl.BlockSpec((tn, k), lambda j, l: (l, 0)), pl.BlockSpec((tn, per * c), lambda j, l: (l, j))],
        out_specs=[out, out],
        out_shape=[jax.ShapeDtypeStruct((N_DEV, k, c), F32), jax.ShapeDtypeStruct((N_DEV, k, c), BF16)],
        compiler_params=_params(("arbitrary", "arbitrary"), 48),
    )(a, b)


def _by_owner_cols(dw):
    k, m = dw.shape
    return dw.reshape(k, N_DEV, m // N_DEV).transpose(1, 0, 2)


def _own_and_bf16(part):
    return lax.dynamic_index_in_dim(part, _my_id(), 0, keepdims=False), part.astype(BF16)


def _step(x, ct, st, tgt, pre_w, post_w, wa_in, ba_in, sinks, wa_out_shard, ba_out, wr_in_shard, lb_logits, gnorm_w, wr_out_shard, b_loc, t_len):
    nb = t_len // BLK
    lb = _lower_bound(lb_logits)
    p0, h0, ga_out = _norm_matmul(x, pre_w[0:1], wa_in, ba_in, "attn_in_proj", [wa_out_shard])
    wa_out = ga_out.reshape(ATTN_WIDTH, D_MODEL)
    o0, g0, gr_in = _attn_fwd(p0, ct, st, sinks, b_loc, nb, [wr_in_shard])
    wr_in = gr_in.transpose(1, 0, 2).reshape(D_MODEL, REC_IN)
    y0, x1 = _outproj_postnorm(g0, wa_out, ba_out, x, post_w[0:1], "attn_out_proj")
    p1, h1, gr_out = _norm_matmul(x1, pre_w[1:2], wr_in, None, "rec_in_proj", [wr_out_shard])
    wr_out = gr_out.reshape(REC_WIDTH, D_MODEL)
    o1, g1, sh = _hgrn_fwd(p1, lb, gnorm_w, b_loc, t_len)
    dx2, dg1, dy1, dpost1, loss_tile = _outproj_loss_bwd(g1, wr_out, x1, post_w[1:2], tgt, "rec_out_proj_loss_bwd")
    d_wr_out = _matmul_tn(g1, dy1, "rec_w_out_grad")
    dp1, dlb, dgw = _hgrn_bwd(p1, lb, gnorm_w, o1, sh, dg1, b_loc, t_len)
    dx1, dpre1, _ = _nt_prenorm_bwd([dp1], wr_in, x1, pre_w[1:2], dx2, False, "rec_in_bwd")
    part_r_in, wire_r_in = _matmul_tn_by_owner(h1, dp1, "rec_w_in_grad")
    own_r_in = lax.dynamic_index_in_dim(part_r_in, _my_id(), 0, keepdims=False)
    dg0, dy0, dpost0, dba_out = _postnorm_bwd_nt(dx1, y0, post_w[0:1], wa_out, "attn_out_bwd")
    d_wa_out = _matmul_tn(g0, dy0, "attn_w_out_grad")
    owns, wires = zip(*[_own_and_bf16(part) for part in (
        d_wr_out.reshape(N_DEV, REC_WIDTH // N_DEV, D_MODEL), d_wa_out.reshape(N_DEV, ATTN_WIDTH // N_DEV, D_MODEL))])
    owns, wires = (own_r_in,) + owns, (wire_r_in,) + wires
    dp0, dsink_tile, *lands = _attn_bwd(p0, ct, st, sinks, o0, dg0, b_loc, nb, list(wires))
    d_wa_in = _matmul_tn(h0, dp0, "attn_w_in_grad")
    own_a_in, wire_a_in = _own_and_bf16(_by_owner_cols(_qkvz(d_wa_in)))
    dx0, dpre0, dba_in, land_a_in = _nt_prenorm_bwd([dp0], wa_in, x, pre_w[0:1], dx1, True, "attn_in_bwd", [wire_a_in])
    small = dict(pre=jnp.concatenate([dpre0, dpre1], axis=0), post=jnp.concatenate([dpost0, dpost1], axis=0),
                 ba_in=dba_in, sinks=dsink_tile[0:1, 0:N_HEADS], ba_out=dba_out, lb=dlb, gnorm=jnp.sum(dgw, axis=0))
    return loss_tile, dx0, list(zip(lands, owns)) + [(land_a_in, own_a_in)], small


def _my_id():
    return lax.axis_index("x") * 4 + lax.axis_index("y") * 2 + lax.axis_index("c")


def _peer(k):
    x, y, c = lax.axis_index("x"), lax.axis_index("y"), lax.axis_index("c")
    return (x ^ ((k >> 2) & 1), y ^ ((k >> 1) & 1), c ^ (k & 1))


def _peer_id(k):
    return _my_id() ^ k


def _all_gather_by_chip(shard, pos_col):
    n = pos_col.shape[0]

    def body(x_ref, p_ref, f_ref, s_ref, out_ref, ct_ref, st_ref, send_sems, recv_sems, local_sem):
        x, y, c = lax.axis_index("x"), lax.axis_index("y"), lax.axis_index("c")
        me, sibling = (x, y, c), (x, y, 1 - c)
        chips = [(1 - x, y), (x, 1 - y), (1 - x, 1 - y)]

        def rows(px, py, pc):
            return out_ref.at[4 * px + 2 * py + pc]

        def copy(k, block, to, src=None):
            return pltpu.make_async_remote_copy(src_ref=rows(*block) if src is None else src, dst_ref=rows(*block),
                                                send_sem=send_sems.at[k], recv_sem=recv_sems.at[k], device_id=to, device_id_type=MESH)

        mine = pltpu.make_async_copy(x_ref, rows(*me), local_sem)
        mine.start()
        first = [copy(0, me, sibling, src=x_ref)] + [copy(1 + j, me, (*chip, c), src=x_ref) for j, chip in enumerate(chips)]
        for cp in first:
            cp.start()
        _rope_tables_into(p_ref, f_ref, s_ref, ct_ref, st_ref)
        passed = [copy(4 + j, (*chip, c), sibling) for j, chip in enumerate(chips)]
        for j, chip in enumerate(chips):
            copy(1 + j, (*chip, c), me).wait_recv()
            passed[j].start()
        copy(0, sibling, me).wait_recv()
        for j, chip in enumerate(chips):
            copy(4 + j, (*chip, 1 - c), me).wait_recv()
        for cp in first + passed:
            cp.wait_send()
        mine.wait()

    hbm, vmem = pl.BlockSpec(memory_space=pl.ANY), pl.BlockSpec(memory_space=pltpu.VMEM)
    return pl.pallas_call(
        body, name="comm_all_gather_by_chip", in_specs=[hbm, vmem, vmem, vmem], out_specs=[hbm, vmem, vmem],
        out_shape=[jax.ShapeDtypeStruct((N_DEV,) + shard.shape, shard.dtype)] + [jax.ShapeDtypeStruct((n, LANES), F32)] * 2,
        scratch_shapes=[pltpu.SemaphoreType.DMA((N_DEV - 1,)), pltpu.SemaphoreType.DMA((N_DEV - 1,)), pltpu.SemaphoreType.DMA],
        compiler_params=_params(None, 32),
    )(shard, pos_col, *_rope_lanes())


def _gather_shapes(shards):
    return [jax.ShapeDtypeStruct((N_DEV,) + s.shape, s.dtype) for s in shards]


def _gather_sems(nsh):
    return [pltpu.SemaphoreType.DMA((nsh, N_DEV - 1)), pltpu.SemaphoreType.DMA((nsh, N_DEV - 1)), pltpu.SemaphoreType.DMA((nsh,))]


def _gather_copies(ins, outs, sems, received):
    send_sems, recv_sems, local_sems = sems
    me = _my_id()
    local = [pltpu.make_async_copy(ins[a], outs[a].at[me], local_sems.at[a]) for a in range(len(ins))]
    remote = [pltpu.make_async_remote_copy(
        src_ref=ins[a], dst_ref=outs[a].at[_peer_id(k) if received else me], send_sem=send_sems.at[a, k - 1],
        recv_sem=recv_sems.at[a, k - 1], device_id=_peer(k), device_id_type=MESH)
        for a in range(len(ins)) for k in range(1, N_DEV)]
    return local, remote


def _gather_start(ins, outs, sems):
    local, sends = _gather_copies(ins, outs, sems, False)
    for cp in local + sends:
        cp.start()


def _gather_wait(ins, outs, sems):
    local, recvs = _gather_copies(ins, outs, sems, True)
    for cp in recvs:
        cp.wait_recv()
    for cp in recvs:
        cp.wait_send()
    for cp in local:
        cp.wait()


def _scatter_lands(parts):
    return [jax.ShapeDtypeStruct((N_DEV - 1,) + p.shape[1:], p.dtype) for p in parts]


def _scatter_sems(nparts):
    return [pltpu.SemaphoreType.DMA((nparts, N_DEV - 1)), pltpu.SemaphoreType.DMA((nparts, N_DEV - 1))]


def _scatter_copies(parts, lands, sems):
    send_sems, recv_sems = sems
    return [pltpu.make_async_remote_copy(
        src_ref=parts[a].at[_peer_id(k)], dst_ref=lands[a].at[k - 1], send_sem=send_sems.at[a, k - 1],
        recv_sem=recv_sems.at[a, k - 1], device_id=_peer(k), device_id_type=MESH)
        for a in range(len(parts)) for k in range(1, N_DEV)]


def _scatter_start(parts, lands, sems):
    for cp in _scatter_copies(parts, lands, sems):
        cp.start()


def _scatter_wait(parts, lands, sems):
    copies = _scatter_copies(parts, lands, sems)
    for cp in copies:
        cp.wait_recv()
    for cp in copies:
        cp.wait_send()


def _adamw(w, g, m, v):
    m2 = ADAM_B1 * m + (1.0 - ADAM_B1) * g
    v2 = ADAM_B2 * v + (1.0 - ADAM_B2) * (g * g)
    m_hat = m2 / (1.0 - ADAM_B1 ** ADAM_STEP)
    v_hat = v2 / (1.0 - ADAM_B2 ** ADAM_STEP)
    delta = -ADAM_LR * (m_hat / (jnp.sqrt(v_hat) + ADAM_EPS) + ADAM_WD * w)
    return delta, m2, v2


def _sum_adamw(land, own, w, m, v, name):
    r, c = own.shape
    rb = min(r, 256)

    def body(land_ref, own_ref, w_ref, m_ref, v_ref, g_ref, d_ref, m2_ref, v2_ref):
        me = _my_id()
        g = jnp.zeros((rb, c), F32)
        for dev in range(N_DEV):
            k = dev ^ me
            g = g + jnp.where(k == 0, own_ref[...], land_ref[jnp.maximum(k - 1, 0)].astype(F32))
        delta, m2, v2 = _adamw(w_ref[...], g, m_ref[...], v_ref[...])
        g_ref[...] = g
        d_ref[...] = delta
        m2_ref[...] = m2
        v2_ref[...] = v2

    rows = pl.BlockSpec((rb, c), lambda i: (i, 0))
    return pl.pallas_call(
        body, name=name, grid=(r // rb,), in_specs=[pl.BlockSpec((N_DEV - 1, rb, c), lambda i: (0, i, 0))] + [rows] * 4,
        out_specs=[rows] * 4, out_shape=[jax.ShapeDtypeStruct((r, c), F32)] * 4,
        compiler_params=_params(("arbitrary",), 32),
    )(land, own, w, m, v)


_SMALL = [("pre_norm_w", 2048), ("post_norm_w", 2048), ("attn_b_in", 2304), ("attn_sinks", 16), ("attn_b_out", 1024),
          ("rec_lb_logits", 2048), ("rec_gnorm_w", 128), ("loss", 1)]
_TILE = SUBLANES * LANES


def _small_rows(size):
    return -(-size // _TILE) * SUBLANES


_SMALL_OFF = {}
_r = 0
for _name, _size in _SMALL:
    _SMALL_OFF[_name] = _r
    _r += _small_rows(_size)
_SMALL_ROWS = _r


def _pack_small(pieces):
    out = []
    for name, size in _SMALL:
        flat = pieces[name].reshape(-1).astype(F32)
        out.append(jnp.pad(flat, (0, _small_rows(size) * LANES - size)).reshape(-1, LANES))
    return jnp.concatenate(out, axis=0)


def _unpack_small(packed, shapes):
    return {name: packed[_SMALL_OFF[name]:_SMALL_OFF[name] + _small_rows(size)].reshape(-1)[:size].reshape(shapes[name])
            for name, size in _SMALL}


def _small_allreduce_adamw(gpart, w, m, v):
    lb0 = _SMALL_OFF["rec_lb_logits"]

    def body(gp_ref, w_ref, m_ref, v_ref, g_ref, d_ref, m2_ref, v2_ref, land_ref, send_sems, recv_sems):
        me = _my_id()
        sent = []
        for k in range(1, N_DEV):
            cp = pltpu.make_async_remote_copy(src_ref=gp_ref, dst_ref=land_ref.at[k - 1], send_sem=send_sems.at[k - 1],
                                              recv_sem=recv_sems.at[k - 1], device_id=_peer(k), device_id_type=MESH)
            cp.start()
            sent.append(cp)
        for cp in sent:
            cp.wait_recv()
        for cp in sent:
            cp.wait_send()
        g = jnp.zeros((_SMALL_ROWS, LANES), F32)
        for dev in range(N_DEV):
            k = dev ^ me
            g = g + jnp.where(k == 0, gp_ref[...], land_ref[jnp.maximum(k - 1, 0)])
        g_ref[...] = g
        l0, l1 = w_ref[lb0:lb0 + SUBLANES, :], w_ref[lb0 + SUBLANES:lb0 + 2 * SUBLANES, :]
        mx = jnp.maximum(l0, l1)
        e0, e1 = jnp.exp(l0 - mx), jnp.exp(l1 - mx)
        p1 = e1 / (e0 + e1)
        dl1 = (1.0 - p1) * p1 * g[lb0:lb0 + SUBLANES, :]
        g_ref[lb0:lb0 + SUBLANES, :] = -dl1
        g_ref[lb0 + SUBLANES:lb0 + 2 * SUBLANES, :] = dl1
        delta, m2, v2 = _adamw(w_ref[...], g_ref[...], m_ref[...], v_ref[...])
        d_ref[...] = delta
        m2_ref[...] = m2
        v2_ref[...] = v2

    vmem = pl.BlockSpec(memory_space=pltpu.VMEM)
    return pl.pallas_call(
        body, name="comm_small_allreduce_adamw", in_specs=[vmem] * 4, out_specs=[vmem] * 4,
        out_shape=[jax.ShapeDtypeStruct((_SMALL_ROWS, LANES), F32)] * 4,
        scratch_shapes=[pltpu.VMEM((N_DEV - 1, _SMALL_ROWS, LANES), F32), pltpu.SemaphoreType.DMA((N_DEV - 1,)),
                        pltpu.SemaphoreType.DMA((N_DEV - 1,))],
    )(gpart, w, m, v)


def _qzkv(a):
    kv_end = ATTN_WIDTH + 2 * KV_WIDTH
    return jnp.concatenate([a[..., :ATTN_WIDTH], a[..., kv_end:], a[..., ATTN_WIDTH:kv_end]], axis=-1)


def _qkvz(a):
    return jnp.concatenate([a[..., :ATTN_WIDTH], a[..., 2 * ATTN_WIDTH:], a[..., ATTN_WIDTH:2 * ATTN_WIDTH]], axis=-1)


def kernel(x, positions, pre_norm_w, post_norm_w, attn_w_in, attn_b_in, attn_sinks, attn_w_out, attn_b_out, rec_w_in, rec_lb_logits, rec_gnorm_w, rec_w_out, loss_target, m_pre_norm_w, m_post_norm_w, m_attn_w_in, m_attn_b_in, m_attn_sinks, m_attn_w_out, m_attn_b_out, m_rec_w_in, m_rec_lb_logits, m_rec_gnorm_w, m_rec_w_out, v_pre_norm_w, v_post_norm_w, v_attn_w_in, v_attn_b_in, v_attn_sinks, v_attn_w_out, v_attn_b_out, v_rec_w_in, v_rec_lb_logits, v_rec_gnorm_w, v_rec_w_out):
    b_loc, t_len, _ = x.shape
    n = b_loc * t_len
    ga_in, ct, st = _all_gather_by_chip(attn_w_in[0].astype(BF16), positions.reshape(n, 1).astype(F32))
    wa_in = _qzkv(ga_in.transpose(1, 0, 2).reshape(D_MODEL, ATTN_IN))

    loss_tile, dx, landed, small = _step(
        x.reshape(n, D_MODEL), ct, st, loss_target.reshape(n, D_MODEL),
        pre_norm_w, post_norm_w, wa_in, _qzkv(attn_b_in), attn_sinks, attn_w_out[0].astype(BF16), attn_b_out,
        rec_w_in[0].astype(BF16), rec_lb_logits, rec_gnorm_w, rec_w_out[0].astype(BF16), b_loc, t_len)

    lift = lambda outs: tuple(a[None] for a in outs)
    (l_r_in, o_r_in), (l_r_out, o_r_out), (l_a_out, o_a_out), (l_a_in, o_a_in) = landed
    r_a_in = lift(_sum_adamw(l_a_in, o_a_in, attn_w_in[0], m_attn_w_in[0], v_attn_w_in[0], "adamw_attn_w_in"))
    r_r_in = lift(_sum_adamw(l_r_in, o_r_in, rec_w_in[0], m_rec_w_in[0], v_rec_w_in[0], "adamw_rec_w_in"))
    r_r_out = lift(_sum_adamw(l_r_out, o_r_out, rec_w_out[0], m_rec_w_out[0], v_rec_w_out[0], "adamw_rec_w_out"))
    r_a_out = lift(_sum_adamw(l_a_out, o_a_out, attn_w_out[0], m_attn_w_out[0], v_attn_w_out[0], "adamw_attn_w_out"))

    gsmall = dict(pre_norm_w=small["pre"], post_norm_w=small["post"], attn_b_in=_qkvz(small["ba_in"]), attn_sinks=small["sinks"],
                  attn_b_out=small["ba_out"], rec_lb_logits=jnp.concatenate([small["lb"], jnp.zeros_like(small["lb"])], axis=0),
                  rec_gnorm_w=small["gnorm"], loss=loss_tile[0:1, 0:1])
    nil = jnp.zeros((1, 1), F32)
    wsmall = dict(pre_norm_w=pre_norm_w, post_norm_w=post_norm_w, attn_b_in=attn_b_in, attn_sinks=attn_sinks,
                  attn_b_out=attn_b_out, rec_lb_logits=rec_lb_logits, rec_gnorm_w=rec_gnorm_w, loss=nil)
    msmall = dict(pre_norm_w=m_pre_norm_w, post_norm_w=m_post_norm_w, attn_b_in=m_attn_b_in, attn_sinks=m_attn_sinks,
                  attn_b_out=m_attn_b_out, rec_lb_logits=m_rec_lb_logits, rec_gnorm_w=m_rec_gnorm_w, loss=nil)
    vsmall = dict(pre_norm_w=v_pre_norm_w, post_norm_w=v_post_norm_w, attn_b_in=v_attn_b_in, attn_sinks=v_attn_sinks,
                  attn_b_out=v_attn_b_out, rec_lb_logits=v_rec_lb_logits, rec_gnorm_w=v_rec_gnorm_w, loss=nil)
    shapes = {k: a.shape for k, a in wsmall.items()}
    packed = _small_allreduce_adamw(_pack_small(gsmall), _pack_small(wsmall), _pack_small(msmall), _pack_small(vsmall))
    sg, sd, sm, sv = [_unpack_small(a, shapes) for a in packed]

    big = {"attn_w_in": r_a_in, "attn_w_out": r_a_out, "rec_w_in": r_r_in, "rec_w_out": r_r_out}
    order = ["pre_norm_w", "post_norm_w", "attn_w_in", "attn_b_in", "attn_sinks", "attn_w_out", "attn_b_out", "rec_w_in",
             "rec_lb_logits", "rec_gnorm_w", "rec_w_out"]
    outs = [sg["loss"][0, 0], dx.reshape(b_loc, t_len, D_MODEL)]
    for idx, small_set in enumerate((sg, sd, sm, sv)):
        outs += [big[nm][idx] if nm in big else small_set[nm] for nm in order]
    return tuple(outs)
```
